```python
import jax
import jax.numpy as jnp
from jax import lax
import numpy as np

D_MODEL = 1024
BATCH = 4
SEQ = 8192
DEPTH = 2
DEC_BATCH = 32
DEC_SEQ = 16
PAST_LEN = 2048

CHUNK = 64
N_META = 16
Q_BLOCK = 128
LEAD = 128
N_PAD = LEAD - N_META
N_EVEN = (DEPTH + 1) // 2
N_ODD = DEPTH // 2
A_HEADS = 4
A_DK = 64
A_DV = 128
A_RANK = 16
A_GATE_NORM = 16.0
B_HEADS = 4
B_DH = 128
C_HEADS = 4
C_DQK = 128
C_DV = 256
GATE_CAP = 15.0
N_GROUPS = 4
N_EXP = 8
TOP_K = 2
D_EXPERT = 512
EPS = 1e-6
NEG = -1e30
A_QK = A_HEADS * A_DK
A_V = A_HEADS * A_DV
B_W = B_HEADS * B_DH
C_QK = C_HEADS * C_DQK
C_V = C_HEADS * C_DV
EVEN_SIZES = (A_QK, A_QK, A_V, A_V, A_RANK, B_W, B_W, B_W, B_HEADS)
ODD_SIZES = (C_QK, C_QK, C_V, C_V, C_HEADS, C_HEADS)
EVEN_WIDTH = 2 * A_QK + 2 * A_V + A_RANK + 3 * B_W + B_HEADS
ODD_WIDTH = 2 * C_QK + 2 * C_V + 2 * C_HEADS
F32 = jnp.float32

kernel_name = 'streaming_gla_fox_mlstm_hmoe'


def rmsnorm(x, g):
    xf = x.astype(F32)
    y = xf * lax.rsqrt(jnp.mean(xf * xf, axis=-1, keepdims=True) + EPS)
    return (y * g.astype(F32)).astype(x.dtype)


def split_cols(y, sizes):
    idx = np.cumsum(np.array(sizes))[:-1].tolist()
    return jnp.split(y, idx, axis=-1)


def to_chunks(a, nc):
    a = a.astype(F32)
    a = a.reshape(a.shape[0], nc, CHUNK, *a.shape[2:])
    return jnp.moveaxis(a, (1, 3), (0, 2))


def from_chunks(a):
    a = jnp.moveaxis(a, (0, 2), (1, 3))
    return a.reshape(a.shape[0], -1, *a.shape[3:])


def head_major(a):
    return jnp.swapaxes(a.astype(F32), 1, 2)


def gla_chunk(S, q, k, v, loga):
    L = q.shape[2]
    b = jnp.cumsum(loga, axis=2)
    causal = jnp.tril(jnp.ones((L, L), bool))
    diff = b[:, :, :, None, :] - b[:, :, None, :, :]
    decay = jnp.exp(jnp.where(causal[:, :, None], diff, -jnp.inf))
    A = jnp.einsum('bhtd,bhsd,bhtsd->bhts', q, k, decay)
    o = jnp.einsum('bhtd,bhdv->bhtv', q * jnp.exp(b), S) + jnp.einsum('bhts,bhsv->bhtv', A, v)
    bL = b[:, :, -1:, :]
    S_new = jnp.exp(bL[:, :, 0])[..., None] * S + jnp.einsum('bhsd,bhsv->bhdv', k * jnp.exp(bL - b), v)
    return S_new, o


def fox_attend(q, k, v, Fq, Fk, mask):
    s = jnp.einsum('bqhd,bkhd->bhqk', q, k).astype(F32) * (B_DH ** -0.5)
    s = s + jnp.swapaxes(Fq, 1, 2)[..., :, None] - jnp.swapaxes(Fk, 1, 2)[..., None, :]
    p = jax.nn.softmax(jnp.where(mask, s, NEG), axis=-1)
    return jnp.einsum('bhqk,bkhd->bqhd', p.astype(v.dtype), v)


def fox_prompt(q, k, v, logf, valid):
    B_, T_ = q.shape[:2]
    F = jnp.cumsum(logf, axis=1)
    kpos = jnp.arange(T_)

    def block(i):
        start = i * Q_BLOCK
        qpos = start + jnp.arange(Q_BLOCK)
        mask = (kpos[None, :] <= qpos[:, None]) & valid[None, :]
        qb = lax.dynamic_slice_in_dim(q, start, Q_BLOCK, axis=1)
        Fq = lax.dynamic_slice_in_dim(F, start, Q_BLOCK, axis=1)
        return fox_attend(qb, k, v, Fq, F, mask)

    o = lax.map(block, jnp.arange(T_ // Q_BLOCK))
    return jnp.moveaxis(o, 0, 1).reshape(B_, T_, B_HEADS, B_DH)


def fox_sample(q, k, v, logf, ck, cv, clf):
    P = ck.shape[1]
    S_ = q.shape[1]
    k_all = jnp.concatenate([ck.astype(k.dtype), k], axis=1)
    v_all = jnp.concatenate([cv.astype(v.dtype), v], axis=1)
    F = jnp.cumsum(jnp.concatenate([clf.astype(F32), logf], axis=1), axis=1)
    mask = jnp.arange(P + S_)[None, :] <= (P + jnp.arange(S_))[:, None]
    return fox_attend(q, k_all, v_all, F[:, P:], F, mask)


def even_project(xn, w_in, w_g2, b_g, b_f):
    B_, T_, _ = xn.shape
    qa, ka, va, ra, ga, qb, kb, vb, fb = split_cols(xn @ w_in, EVEN_SIZES)
    qa = qa.reshape(B_, T_, A_HEADS, A_DK) * (A_DK ** -0.5)
    ka = ka.reshape(B_, T_, A_HEADS, A_DK)
    va = va.reshape(B_, T_, A_HEADS, A_DV)
    z = (ga @ w_g2 + b_g).astype(F32)
    loga = (jax.nn.log_sigmoid(z) / A_GATE_NORM).reshape(B_, T_, A_HEADS, A_DK)
    qb = qb.reshape(B_, T_, B_HEADS, B_DH)
    kb = kb.reshape(B_, T_, B_HEADS, B_DH)
    vb = vb.reshape(B_, T_, B_HEADS, B_DH)
    logf = jax.nn.log_sigmoid((fb + b_f).astype(F32))
    return qa, ka, va, ra, loga, qb, kb, vb, logf


def even_output(oa, ra, ob, g_a, w_o):
    B_, T_ = ra.shape[:2]
    dt = ra.dtype
    ya = rmsnorm(oa, g_a) * jax.nn.silu(ra.astype(F32)).reshape(B_, T_, A_HEADS, A_DV)
    cat = jnp.concatenate([ya.reshape(B_, T_, A_V).astype(dt), ob.reshape(B_, T_, B_W).astype(dt)], axis=-1)
    return cat @ w_o


def even_mixer_prompt(xn, valid, w_in, w_g2, b_g, b_f, g_a, w_o):
    qa, ka, va, ra, loga, qb, kb, vb, logf = even_project(xn, w_in, w_g2, b_g, b_f)
    vm = valid[None, :, None, None]
    ka = jnp.where(vm, ka, 0)
    loga = jnp.where(vm, loga, 0.0)
    logf = jnp.where(valid[None, :, None], logf, 0.0)
    B_, T_ = xn.shape[:2]
    nc = T_ // CHUNK
    S0 = jnp.zeros((B_, A_HEADS, A_DK, A_DV), F32)
    xs = (to_chunks(qa, nc), to_chunks(ka, nc), to_chunks(va, nc), to_chunks(loga, nc))
    S, oa = lax.scan(lambda S, c: gla_chunk(S, *c), S0, xs)
    oa = from_chunks(oa)
    ob = fox_prompt(qb, kb, vb, logf, valid)
    y = even_output(oa, ra, ob, g_a, w_o)
    return y, (S, kb[:, N_PAD:], vb[:, N_PAD:], logf[:, N_PAD:])


def even_mixer_sample(xn, S, ck, cv, clf, w_in, w_g2, b_g, b_f, g_a, w_o):
    qa, ka, va, ra, loga, qb, kb, vb, logf = even_project(xn, w_in, w_g2, b_g, b_f)
    S_new, oa = gla_chunk(S.astype(F32), head_major(qa), head_major(ka), head_major(va), head_major(loga))
    oa = jnp.swapaxes(oa, 1, 2)
    ob = fox_sample(qb, kb, vb, logf, ck, cv, clf)
    y = even_output(oa, ra, ob, g_a, w_o)
    return y, (S_new, kb, vb, logf)


def mlstm_chunk(carry, q, k, v, logi, logf):
    C, n, m = carry
    L = q.shape[2]
    causal = jnp.tril(jnp.ones((L, L), bool))
    b = jnp.cumsum(logf, axis=-1)
    D = jnp.where(causal, b[..., :, None] - b[..., None, :] + logi[..., None, :], -jnp.inf)
    inter = b + m[..., None]
    m_t = jnp.maximum(inter, jnp.max(D, axis=-1))
    P = jnp.exp(D - m_t[..., None])
    w_inter = jnp.exp(inter - m_t)
    Sqk = jnp.einsum('bhtd,bhsd->bhts', q, k) * P
    num = w_inter[..., None] * jnp.einsum('bhvd,bhtd->bhtv', C, q) + jnp.einsum('bhts,bhsv->bhtv', Sqk, v)
    den = w_inter * jnp.einsum('bhd,bhtd->bht', n, q) + jnp.sum(Sqk, axis=-1)
    h = num / jnp.maximum(jnp.abs(den), jnp.exp(-m_t))[..., None]
    bL = b[..., -1]
    g = bL[..., None] - b + logi
    m_new = jnp.maximum(bL + m, jnp.max(g, axis=-1))
    wk = jnp.exp(g - m_new[..., None])
    wC = jnp.exp(bL + m - m_new)
    C_new = wC[..., None, None] * C + jnp.einsum('bhs,bhsv,bhsd->bhvd', wk, v, k)
    n_new = wC[..., None] * n + jnp.einsum('bhs,bhsd->bhd', wk, k)
    return (C_new, n_new, m_new), h


def odd_project(xn, w_in, b_gate):
    B_, T_, _ = xn.shape
    q, k, v, o, ig, fg = split_cols(xn @ w_in, ODD_SIZES)
    q = q.reshape(B_, T_, C_HEADS, C_DQK)
    k = k.reshape(B_, T_, C_HEADS, C_DQK) * (C_DQK ** -0.5)
    v = v.reshape(B_, T_, C_HEADS, C_DV)
    gates = (jnp.concatenate([ig, fg], axis=-1) + b_gate).astype(F32)
    gates = GATE_CAP * jnp.tanh(gates / GATE_CAP)
    logi = gates[..., :C_HEADS]
    logf = jax.nn.log_sigmoid(gates[..., C_HEADS:])
    return q, k, v, o, logi, logf


def odd_output(h, o, g_c, w_o):
    B_, T_ = o.shape[:2]
    y = rmsnorm(h, g_c) * jax.nn.sigmoid(o.astype(F32)).reshape(B_, T_, C_HEADS, C_DV)
    return y.reshape(B_, T_, C_V).astype(o.dtype) @ w_o


def odd_mixer_prompt(xn, valid, w_in, b_gate, g_c, w_o):
    q, k, v, o, logi, logf = odd_project(xn, w_in, b_gate)
    vm = valid[None, :, None]
    logi = jnp.where(vm, logi, -jnp.inf)
    logf = jnp.where(vm, logf, 0.0)
    B_, T_ = xn.shape[:2]
    nc = T_ // CHUNK
    carry0 = (jnp.zeros((B_, C_HEADS, C_DV, C_DQK), F32), jnp.zeros((B_, C_HEADS, C_DQK), F32), jnp.zeros((B_, C_HEADS), F32))
    xs = (to_chunks(q, nc), to_chunks(k, nc), to_chunks(v, nc), to_chunks(logi, nc), to_chunks(logf, nc))
    carry, h = lax.scan(lambda c, a: mlstm_chunk(c, *a), carry0, xs)
    y = odd_output(from_chunks(h), o, g_c, w_o)
    return y, carry


def odd_mixer_sample(xn, C, n, m, w_in, b_gate, g_c, w_o):
    q, k, v, o, logi, logf = odd_project(xn, w_in, b_gate)
    carry0 = (C.astype(F32), n.astype(F32), m.astype(F32))
    carry, h = mlstm_chunk(carry0, head_major(q), head_major(k), head_major(v), head_major(logi), head_major(logf))
    y = odd_output(jnp.swapaxes(h, 1, 2), o, g_c, w_o)
    return y, carry


def hier_moe(xn, w_rg, b_rg, w_re, b_re, w_gate, w_up, w_down):
    shp = xn.shape
    x2 = xn.reshape(-1, shp[-1])
    pg = jax.nn.softmax((x2 @ w_rg + b_rg).astype(F32), axis=-1)
    onehot_g = jax.nn.one_hot(jnp.argmax(pg, axis=-1), N_GROUPS, dtype=F32)
    wg = jnp.max(pg, axis=-1)
    le = (x2 @ w_re + b_re).astype(F32).reshape(-1, N_GROUPS, N_EXP)
    le_sel = jnp.einsum('nge,ng->ne', le, onehot_g)
    pe = jax.nn.softmax(le_sel, axis=-1)
    top_w, top_i = lax.top_k(pe, TOP_K)
    top_w = top_w / jnp.sum(top_w, axis=-1, keepdims=True)
    ew = jnp.sum(jax.nn.one_hot(top_i, N_EXP, dtype=F32) * top_w[..., None], axis=1)
    comb = (onehot_g[:, :, None] * (wg[:, None] * ew)[:, None, :]).astype(xn.dtype)
    y = jnp.zeros_like(x2)
    for g in range(N_GROUPS):
        hg = jax.nn.silu(jnp.einsum('nd,edf->nef', x2, w_gate[g])) * jnp.einsum('nd,edf->nef', x2, w_up[g])
        y = y + jnp.einsum('nef,efd->nd', hg * comb[:, g, :, None], w_down[g])
    return y.reshape(shp)


def stack_state(xs, dt):
    return jnp.stack(xs).astype(dt)


def setup_inputs(seed: int = 0) -> dict:
    key = jax.random.key(seed)
    ks = jax.random.split(key, 32)

    def nrm(k, shape, scale=1.0):
        return jax.random.normal(k, shape, jnp.float32) * scale

    d = D_MODEL
    return {
        'x_prompt': nrm(ks[0], (BATCH, SEQ, d)),
        'x_sample': nrm(ks[1], (DEC_BATCH, DEC_SEQ, d)),
        'state_gla': nrm(ks[2], (N_EVEN, DEC_BATCH, A_HEADS, A_DK, A_DV), 0.5),
        'cache_fox_k': nrm(ks[3], (N_EVEN, DEC_BATCH, PAST_LEN, B_HEADS, B_DH)),
        'cache_fox_v': nrm(ks[4], (N_EVEN, DEC_BATCH, PAST_LEN, B_HEADS, B_DH)),
        'cache_fox_logf': jax.nn.log_sigmoid(3.0 + nrm(ks[5], (N_EVEN, DEC_BATCH, PAST_LEN, B_HEADS))),
        'state_mlstm_c': nrm(ks[6], (N_ODD, DEC_BATCH, C_HEADS, C_DV, C_DQK), 0.1),
        'state_mlstm_n': nrm(ks[7], (N_ODD, DEC_BATCH, C_HEADS, C_DQK), 0.1),
        'state_mlstm_m': nrm(ks[8], (N_ODD, DEC_BATCH, C_HEADS)),
        'meta_tokens': nrm(ks[9], (N_META, d)),
        'norm_mix': 1.0 + nrm(ks[10], (DEPTH, d), 0.02),
        'norm_ffn': 1.0 + nrm(ks[11], (DEPTH, d), 0.02),
        'norm_final': 1.0 + nrm(ks[12], (d,), 0.02),
        'w_in_even': nrm(ks[13], (N_EVEN, d, EVEN_WIDTH), d ** -0.5),
        'w_gla_gate2': nrm(ks[14], (N_EVEN, A_RANK, A_QK), A_RANK ** -0.5),
        'b_gla_gate': nrm(ks[15], (N_EVEN, A_QK), 0.1),
        'b_fox_f': 3.0 + nrm(ks[16], (N_EVEN, B_HEADS), 0.1),
        'g_gla_out': 1.0 + nrm(ks[17], (N_EVEN, A_DV), 0.02),
        'w_out_even': nrm(ks[18], (N_EVEN, A_V + B_W, d), (A_V + B_W) ** -0.5),
        'w_in_odd': nrm(ks[19], (N_ODD, d, ODD_WIDTH), d ** -0.5),
        'b_mlstm_gate': jnp.concatenate([nrm(ks[20], (N_ODD, C_HEADS), 0.1), 3.0 + nrm(ks[21], (N_ODD, C_HEADS), 0.1)], axis=-1),
        'g_mlstm_out': 1.0 + nrm(ks[22], (N_ODD, C_DV), 0.02),
        'w_out_odd': nrm(ks[23], (N_ODD, C_V, d), C_V ** -0.5),
        'w_router_group': nrm(ks[24], (DEPTH, d, N_GROUPS), d ** -0.5),
        'b_router_group': nrm(ks[25], (DEPTH, N_GROUPS), 0.01),
        'w_router_expert': nrm(ks[26], (DEPTH, d, N_GROUPS * N_EXP), d ** -0.5),
        'b_router_expert': nrm(ks[27], (DEPTH, N_GROUPS * N_EXP), 0.01),
        'w_exp_gate': nrm(ks[28], (DEPTH, N_GROUPS, N_EXP, d, D_EXPERT), d ** -0.5),
        'w_exp_up': nrm(ks[29], (DEPTH, N_GROUPS, N_EXP, d, D_EXPERT), d ** -0.5),
        'w_exp_down': nrm(ks[30], (DEPTH, N_GROUPS, N_EXP, D_EXPERT, d), D_EXPERT ** -0.5),
    }


def reference(x_prompt, x_sample, state_gla, cache_fox_k, cache_fox_v, cache_fox_logf, state_mlstm_c, state_mlstm_n, state_mlstm_m, meta_tokens, norm_mix, norm_ffn, norm_final, w_in_even, w_gla_gate2, b_gla_gate, b_fox_f, g_gla_out, w_out_even, w_in_odd, b_mlstm_gate, g_mlstm_out, w_out_odd, w_router_group, b_router_group, w_router_expert, b_router_expert, w_exp_gate, w_exp_up, w_exp_down):
    dt = x_prompt.dtype
    b_p = x_prompt.shape[0]
    hp = jnp.concatenate([jnp.zeros((b_p, N_PAD, D_MODEL), dt),
                          jnp.broadcast_to(meta_tokens.astype(dt)[None], (b_p, N_META, D_MODEL)),
                          x_prompt], axis=1)
    valid = jnp.arange(hp.shape[1]) >= N_PAD
    hs = x_sample
    pg, pk, pv, pf, pc, pn, pm = [], [], [], [], [], [], []
    sg, sk, sv, sf, sc, sn, sm = [], [], [], [], [], [], []
    for l in range(DEPTH):
        xp = rmsnorm(hp, norm_mix[l])
        xs = rmsnorm(hs, norm_mix[l])
        if l % 2 == 0:
            e = l // 2
            wts = (w_in_even[e], w_gla_gate2[e], b_gla_gate[e], b_fox_f[e], g_gla_out[e], w_out_even[e])
            yp, (S_p, k_p, v_p, f_p) = even_mixer_prompt(xp, valid, *wts)
            ys, (S_s, k_s, v_s, f_s) = even_mixer_sample(xs, state_gla[e], cache_fox_k[e], cache_fox_v[e], cache_fox_logf[e], *wts)
            pg.append(S_p); pk.append(k_p); pv.append(v_p); pf.append(f_p)
            sg.append(S_s); sk.append(k_s); sv.append(v_s); sf.append(f_s)
        else:
            o = l // 2
            wts = (w_in_odd[o], b_mlstm_gate[o], g_mlstm_out[o], w_out_odd[o])
            yp, (c_p, n_p, m_p) = odd_mixer_prompt(xp, valid, *wts)
            ys, (c_s, n_s, m_s) = odd_mixer_sample(xs, state_mlstm_c[o], state_mlstm_n[o], state_mlstm_m[o], *wts)
            pc.append(c_p); pn.append(n_p); pm.append(m_p)
            sc.append(c_s); sn.append(n_s); sm.append(m_s)
        hp = hp + yp.astype(dt)
        hs = hs + ys.astype(dt)
        mw = (w_router_group[l], b_router_group[l], w_router_expert[l], b_router_expert[l], w_exp_gate[l], w_exp_up[l], w_exp_down[l])
        hp = hp + hier_moe(rmsnorm(hp, norm_ffn[l]), *mw).astype(dt)
        hs = hs + hier_moe(rmsnorm(hs, norm_ffn[l]), *mw).astype(dt)
    y_prompt = rmsnorm(hp, norm_final)[:, LEAD:]
    y_sample = rmsnorm(hs, norm_final)
    return (y_prompt, y_sample,
            stack_state(pg, dt), stack_state(pk, dt), stack_state(pv, dt), stack_state(pf, dt),
            stack_state(pc, dt), stack_state(pn, dt), stack_state(pm, dt),
            stack_state(sg, dt), stack_state(sk, dt), stack_state(sv, dt), stack_state(sf, dt),
            stack_state(sc, dt), stack_state(sn, dt), stack_state(sm, dt))
```

```python
import functools

import jax
import jax.numpy as jnp
from jax import lax
from jax.experimental import pallas as pl
from jax.experimental.pallas import tpu as pltpu

F32 = jnp.float32
BF16 = jnp.bfloat16

CHUNK = 64
N_META = 16
LEAD = 128
N_PAD = LEAD - N_META
A_HEADS, A_DK, A_DV, A_RANK = 4, 64, 128, 16
A_GATE_NORM = 16.0
B_HEADS, B_DH = 4, 128
C_HEADS, C_DQK, C_DV = 4, 128, 256
GATE_CAP = 15.0
N_GROUPS, N_EXP = 4, 8
EPS = 1e-6
NEG = -1e30
A_QK = A_HEADS * A_DK
A_V = A_HEADS * A_DV
B_W = B_HEADS * B_DH
C_QK = C_HEADS * C_DQK
C_V = C_HEADS * C_DV

LANES = 128
SUBLANES = 8
VMEM_LIMIT_BYTES = 56 * 1024 * 1024
GLA_SUB = 16
MAIN_W = 3072
PROJ_W = MAIN_W + LANES

_NT = (((1,), (1,)), ((), ()))
_TN = (((0,), (0,)), ((), ()))
_NN = (((1,), (0,)), ((), ()))


def _params(sem):
    return pltpu.CompilerParams(dimension_semantics=sem, vmem_limit_bytes=VMEM_LIMIT_BYTES)


def _tile(n, pref, mult):
    t = (min(pref, n) // mult) * mult
    while t > mult and n % t:
        t -= mult
    assert t >= mult and n % t == 0, (n, pref, mult)
    return t


def _dot(a, b, dims=_NN):
    return lax.dot_general(a, b, dims, preferred_element_type=F32)


def _split(x):
    hi = x.astype(BF16)
    lo = (x - hi.astype(F32)).astype(BF16)
    return hi, lo


def _dot3(a, b, dims=_NN):
    ah, al = _split(a)
    bh, bl = _split(b)
    return _dot(ah, bh, dims) + _dot(ah, bl, dims) + _dot(al, bh, dims)


def _log_sigmoid(x):
    return jnp.minimum(x, 0.0) - jnp.log1p(jnp.exp(-jnp.abs(x)))


def _sigmoid(x):
    return 1.0 / (1.0 + jnp.exp(-x))


def _rms(x, g):
    return x * lax.rsqrt(jnp.mean(x * x, axis=-1, keepdims=True) + EPS) * g


def _cumsum_rows(x):
    n = x.shape[0]
    row = lax.broadcasted_iota(jnp.int32, x.shape, 0)
    s = 1
    while s < n:
        x = x + jnp.where(row >= s, pltpu.roll(x, s, axis=0), 0.0)
        s *= 2
    return x


def _proj_kernel(x_ref, g_ref, w_ref, cs_ref, o_ref, ob_ref, *, col_chunk):
    xn = _rms(x_ref[...], g_ref[...]).astype(BF16)
    for c0 in range(0, PROJ_W, col_chunk):
        c1 = min(c0 + col_chunk, PROJ_W)
        y = _dot(xn, w_ref[:, c0:c1])
        o_ref[:, c0:c1] = y
        if c0 < MAIN_W:
            m1 = min(c1, MAIN_W)
            ob_ref[:, c0:m1] = (y[:, :m1 - c0] * cs_ref[:, c0:m1]).astype(BF16)


def _proj(h, g, w_packed, colscale):
    n, d = h.shape
    tm = _tile(n, 512, 16)
    return pl.pallas_call(
        functools.partial(_proj_kernel, col_chunk=640),
        grid=(n // tm,),
        in_specs=[pl.BlockSpec((tm, d), lambda i: (i, 0)),
                  pl.BlockSpec((1, d), lambda i: (0, 0)),
                  pl.BlockSpec((d, PROJ_W), lambda i: (0, 0)),
                  pl.BlockSpec((1, MAIN_W), lambda i: (0, 0))],
        out_specs=[pl.BlockSpec((tm, PROJ_W), lambda i: (i, 0)),
                   pl.BlockSpec((tm, MAIN_W), lambda i: (i, 0))],
        out_shape=[jax.ShapeDtypeStruct((n, PROJ_W), F32),
                   jax.ShapeDtypeStruct((n, MAIN_W), BF16)],
        compiler_params=_params(("parallel",)),
        name="proj",
    )(h, g.reshape(1, d), w_packed, colscale)


def _gate_scan_kernel(x_ref, bias_ref, isf_ref, val_ref, cum_ref, *, mode, act_start, valid_start, valid_end, seg):
    x = x_ref[...]
    lane = lax.broadcasted_iota(jnp.int32, x.shape, 1)
    valid = (lane >= valid_start) & (lane < valid_end)
    if mode == "fox":
        val = jnp.where(lane >= act_start, _log_sigmoid(x + bias_ref[...]), x)
        val = jnp.where(valid, val, 0.0)
        add = val
    else:
        gate = GATE_CAP * jnp.tanh((x + bias_ref[...]) / GATE_CAP)
        isf = isf_ref[...] > 0.5
        val = jnp.where(isf, jnp.where(valid, _log_sigmoid(gate), 0.0),
                        jnp.where(valid, gate, -jnp.inf))
        add = jnp.where(isf, val, 0.0)
    val_ref[...] = val
    n = x.shape[1]
    pos = lane if seg is None else lane % seg
    limit = n if seg is None else seg
    s = 1
    while s < limit:
        add = add + jnp.where(pos >= s, pltpu.roll(add, s, axis=1), 0.0)
        s *= 2
    cum_ref[...] = add


def _gate_scan(x, bias, isf, *, mode, act_start, valid_start, valid_end, seg):
    r, n = x.shape
    full = lambda shape: pl.BlockSpec(shape, lambda i: (0,) * len(shape))
    return pl.pallas_call(
        functools.partial(_gate_scan_kernel, mode=mode, act_start=act_start,
                          valid_start=valid_start, valid_end=valid_end, seg=seg),
        grid=(1,),
        in_specs=[full((r, n)), full((r, 1)), full((r, 1))],
        out_specs=[full((r, n)), full((r, n))],
        out_shape=[jax.ShapeDtypeStruct((r, n), F32)] * 2,
        compiler_params=_params(("arbitrary",)),
        name="gate_scan_" + mode,
    )(x, bias, isf)


def _pad_lanes(x):
    n = x.shape[-1]
    m = -(-n // LANES) * LANES
    return x if m == n else jnp.pad(x, ((0, 0), (0, m - n)))


def _gla_kernel(qk_ref, v_ref, sm_ref, s0_ref, wg2_ref, wg2t_ref, bgr_ref, bgc_ref, *rest,
                L, sub, lead_pad, aliased):
    if aliased:
        rest = rest[1:]
    o_ref, sout_ref, s_scr = rest
    c = pl.program_id(1)
    nh, dk, dv = A_HEADS, A_DK, A_DV

    @pl.when(c == 0)
    def _():
        s_scr[...] = jnp.zeros_like(s_scr)
        for h in range(nh):
            s_scr[h * dk:(h + 1) * dk, h * dv:(h + 1) * dv] = s0_ref[0, h]

    qk = qk_ref[...]
    q = qk[:, :A_QK] * (A_DK ** -0.5)
    k = qk[:, A_QK:]
    v = v_ref[...]
    ga = sm_ref[:, :A_RANK]
    row = lax.broadcasted_iota(jnp.int32, (L, 1), 0)
    valid = (c * L + row) >= lead_pad
    z = _dot3(ga, wg2_ref[...]) + bgr_ref[...]
    loga = jnp.where(valid, _log_sigmoid(z) / A_GATE_NORM, 0.0)
    k = jnp.where(valid, k, 0.0)
    b = _cumsum_rows(loga)
    b_last = b[L - 1:L, :]
    lane_t = lax.broadcasted_iota(jnp.int32, (1, L), 1)
    zt = _dot3(wg2t_ref[...], ga, _NT) + bgc_ref[...]
    logat = jnp.where((c * L + lane_t) >= lead_pad, _log_sigmoid(zt) / A_GATE_NORM, 0.0)
    b_last_col = jnp.sum(logat, axis=1, keepdims=True)

    qhead = lax.broadcasted_iota(jnp.int32, (1, A_QK), 1) // dk
    vhead = lax.broadcasted_iota(jnp.int32, (1, A_V), 1) // dv
    vb = v.astype(BF16)
    zero_b = jnp.zeros((), BF16)
    v_bd = jnp.concatenate([jnp.where(vhead == h, vb, zero_b) for h in range(nh)], axis=0)

    rows_all = lax.broadcasted_iota(jnp.int32, (L, 1), 0)
    a_rows = []
    for i in range(L // sub):
        r0 = i * sub
        ci = jnp.zeros((1, A_QK), F32) if i == 0 else b[r0 - 1:r0, :]
        qt = (q[r0:r0 + sub] * jnp.exp(b[r0:r0 + sub] - ci)).astype(BF16)
        kt = jnp.where(rows_all < r0 + sub, k * jnp.exp(ci - b), 0.0).astype(BF16)
        k_stack = jnp.concatenate([jnp.where(qhead == h, kt, zero_b) for h in range(nh)], axis=0)
        a_rows.append(_dot(qt, k_stack, _NT))
    a = a_rows[0] if len(a_rows) == 1 else jnp.concatenate(a_rows, axis=0)
    t_idx = lax.broadcasted_iota(jnp.int32, (L, nh * L), 0)
    s_idx = lax.broadcasted_iota(jnp.int32, (L, nh * L), 1) % L
    a = jnp.where(s_idx <= t_idx, a, 0.0)
    o_intra = _dot(a.astype(BF16), v_bd)

    s_full = s_scr[...]
    o_inter = _dot((q * jnp.exp(b)).astype(BF16), s_full.astype(BF16))
    o_ref[...] = o_inter + o_intra

    k_hat = (k * jnp.exp(b_last - b)).astype(BF16)
    upd = _dot(k_hat, vb, _TN)
    khead_col = lax.broadcasted_iota(jnp.int32, (A_QK, 1), 0) // dk
    s_new = jnp.exp(b_last_col) * s_full + jnp.where(khead_col == vhead, upd, 0.0)
    s_scr[...] = s_new

    @pl.when(c == pl.num_programs(1) - 1)
    def _():
        for h in range(nh):
            sout_ref[0, h] = s_new[h * dk:(h + 1) * dk, h * dv:(h + 1) * dv]


def _gla(p, s0, wg2, bg, *, nb, nc, L, row_block0, lead_pad, n_total, prev_out=None):
    aliased = prev_out is not None
    rb = lambda b, c: row_block0 + b * nc + c
    in_specs = [pl.BlockSpec((L, 2 * A_QK), lambda b, c: (rb(b, c), 0)),
                pl.BlockSpec((L, A_V), lambda b, c: (rb(b, c), 2 * A_QK // A_V)),
                pl.BlockSpec((L, LANES), lambda b, c: (rb(b, c), MAIN_W // LANES)),
                pl.BlockSpec((1, A_HEADS, A_DK, A_DV), lambda b, c: (b, 0, 0, 0)),
                pl.BlockSpec((A_RANK, A_QK), lambda b, c: (0, 0)),
                pl.BlockSpec((A_QK, A_RANK), lambda b, c: (0, 0)),
                pl.BlockSpec((1, A_QK), lambda b, c: (0, 0)),
                pl.BlockSpec((A_QK, 1), lambda b, c: (0, 0))]
    args = [p, p, p, s0, wg2, wg2.T, bg.reshape(1, A_QK), bg.reshape(A_QK, 1)]
    io_alias = {}
    if aliased:
        in_specs.append(pl.BlockSpec(memory_space=pl.ANY))
        args.append(prev_out)
        io_alias = {len(args) - 1: 0}
    return pl.pallas_call(
        functools.partial(_gla_kernel, L=L, sub=min(GLA_SUB, L), lead_pad=lead_pad, aliased=aliased),
        grid=(nb, nc),
        in_specs=in_specs,
        out_specs=[pl.BlockSpec((L, A_V), lambda b, c: (rb(b, c), 0)),
                   pl.BlockSpec((1, A_HEADS, A_DK, A_DV), lambda b, c: (b, 0, 0, 0))],
        out_shape=[jax.ShapeDtypeStruct((n_total, A_V), F32),
                   jax.ShapeDtypeStruct((nb, A_HEADS, A_DK, A_DV), F32)],
        scratch_shapes=[pltpu.VMEM((A_QK, A_V), F32)],
        input_output_aliases=io_alias,
        compiler_params=_params(("parallel", "arbitrary")),
        name="gla_L%d" % L,
    )(*args)


def _flash_kernel(q_ref, k_ref, v_ref, fk_ref, o_ref, m_scr, l_scr, acc_scr, *, blk, lead_pad):
    i = pl.program_id(2)
    j = pl.program_id(3)

    @pl.when(j == 0)
    def _():
        m_scr[...] = jnp.full_like(m_scr, -jnp.inf)
        l_scr[...] = jnp.zeros_like(l_scr)
        acc_scr[...] = jnp.zeros_like(acc_scr)

    @pl.when(j <= i)
    def _():
        s = _dot(q_ref[...], k_ref[...], _NT) - fk_ref[0, 0]
        qpos = i * blk + lax.broadcasted_iota(jnp.int32, (blk, blk), 0)
        kpos = j * blk + lax.broadcasted_iota(jnp.int32, (blk, blk), 1)
        s = jnp.where((kpos <= qpos) & (kpos >= lead_pad), s, NEG)
        m_prev = m_scr[...]
        m_new = jnp.maximum(m_prev, jnp.max(s, axis=1, keepdims=True))
        alpha = jnp.exp(m_prev - m_new)
        p = jnp.exp(s - m_new)
        l_scr[...] = alpha * l_scr[...] + jnp.sum(p, axis=1, keepdims=True)
        acc_scr[...] = alpha * acc_scr[...] + _dot(p.astype(BF16), v_ref[...])
        m_scr[...] = m_new

    @pl.when(j == i)
    def _():
        o_ref[...] = acc_scr[...] / l_scr[...]


def _flash(pb, fk, *, nb, t, n_total):
    blk = _tile(t, 640, LANES)
    nq = t // blk
    qc, kc, vc = (A_QK * 2 + A_V * 2) // LANES, (A_QK * 2 + A_V * 2 + B_W) // LANES, (A_QK * 2 + A_V * 2 + 2 * B_W) // LANES
    return pl.pallas_call(
        functools.partial(_flash_kernel, blk=blk, lead_pad=N_PAD),
        grid=(nb, B_HEADS, nq, nq),
        in_specs=[pl.BlockSpec((blk, B_DH), lambda b, h, i, j: (b * nq + i, qc + h)),
                  pl.BlockSpec((blk, B_DH), lambda b, h, i, j: (b * nq + jnp.minimum(i, j), kc + h)),
                  pl.BlockSpec((blk, B_DH), lambda b, h, i, j: (b * nq + jnp.minimum(i, j), vc + h)),
                  pl.BlockSpec((1, 1, 1, blk), lambda b, h, i, j: (b, h, 0, jnp.minimum(i, j)))],
        out_specs=pl.BlockSpec((blk, B_DH), lambda b, h, i, j: (b * nq + i, h)),
        out_shape=jax.ShapeDtypeStruct((n_total, B_W), F32),
        scratch_shapes=[pltpu.VMEM((blk, 1), F32), pltpu.VMEM((blk, 1), F32), pltpu.VMEM((blk, B_DH), F32)],
        compiler_params=_params(("parallel", "parallel", "parallel", "arbitrary")),
        name="fox_flash",
    )(pb, pb, pb, fk)


def _fox_sample_kernel(q_ref, kn_ref, vn_ref, kc_ref, vc_ref, f_ref, prev_ref, o_ref, *, past, ds):
    del prev_ref
    t_idx = lax.broadcasted_iota(jnp.int32, (ds, ds), 0)
    s_idx = lax.broadcasted_iota(jnp.int32, (ds, ds), 1)
    for h in range(B_HEADS):
        sl = slice(h * B_DH, (h + 1) * B_DH)
        q = q_ref[:, sl]
        f_row = f_ref[0, h]
        s_c = _dot(q, kc_ref[0, :, sl].astype(BF16), _NT) - f_row[:, :past]
        s_n = _dot(q, kn_ref[:, sl], _NT) - f_row[:, past:past + ds]
        s_n = jnp.where(s_idx <= t_idx, s_n, NEG)
        m = jnp.maximum(jnp.max(s_c, axis=1, keepdims=True), jnp.max(s_n, axis=1, keepdims=True))
        p_c = jnp.exp(s_c - m)
        p_n = jnp.exp(s_n - m)
        l = jnp.sum(p_c, axis=1, keepdims=True) + jnp.sum(p_n, axis=1, keepdims=True)
        o = _dot(p_c.astype(BF16), vc_ref[0, :, sl].astype(BF16)) + _dot(p_n.astype(BF16), vn_ref[:, sl])
        o_ref[:, sl] = o / l


def _fox_sample(pb, kc, vc, f_all, prev_out, *, nb, ds, row_block0):
    past = kc.shape[1]
    lanes = f_all.shape[-1]
    base = (A_QK * 2 + A_V * 2) // B_W
    rb = lambda b: row_block0 + b
    return pl.pallas_call(
        functools.partial(_fox_sample_kernel, past=past, ds=ds),
        grid=(nb,),
        in_specs=[pl.BlockSpec((ds, B_W), lambda b: (rb(b), base)),
                  pl.BlockSpec((ds, B_W), lambda b: (rb(b), base + 1)),
                  pl.BlockSpec((ds, B_W), lambda b: (rb(b), base + 2)),
                  pl.BlockSpec((1, past, B_W), lambda b: (b, 0, 0)),
                  pl.BlockSpec((1, past, B_W), lambda b: (b, 0, 0)),
                  pl.BlockSpec((1, B_HEADS, 1, lanes), lambda b: (b, 0, 0, 0)),
                  pl.BlockSpec(memory_space=pl.ANY)],
        out_specs=pl.BlockSpec((ds, B_W), lambda b: (rb(b), 0)),
        out_shape=jax.ShapeDtypeStruct(prev_out.shape, F32),
        input_output_aliases={6: 0},
        compiler_params=_params(("parallel",)),
        name="fox_sample",
    )(pb, pb, pb, kc, vc, f_all, prev_out)


def _mlstm_kernel(q_ref, k_ref, v_ref, lir_ref, br_ref, lic_ref, bc_ref, c0_ref, n0_ref, m0_ref, *rest,
                  L, aliased):
    if aliased:
        rest = rest[1:]
    h_ref, cout_ref, nout_ref, mout_ref, c_scr, n_scr, m_scr = rest
    c = pl.program_id(1)

    @pl.when(c == 0)
    def _():
        c_scr[...] = c0_ref[0]
        n_scr[...] = n0_ref[0]
        m_scr[...] = m0_ref[0]

    t_idx = lax.broadcasted_iota(jnp.int32, (L, L), 0)
    s_idx = lax.broadcasted_iota(jnp.int32, (L, L), 1)
    causal = s_idx <= t_idx
    for h in range(C_HEADS):
        qf = q_ref[:, h * C_DQK:(h + 1) * C_DQK]
        kf = k_ref[:, h * C_DQK:(h + 1) * C_DQK] * (C_DQK ** -0.5)
        vb = v_ref[:, h * C_DV:(h + 1) * C_DV].astype(BF16)
        qb = qf.astype(BF16)
        b_col = bc_ref[0, 0, :, h:h + 1]
        li_col = lic_ref[0, 0, :, h:h + 1]
        b_row = br_ref[0, 0, h:h + 1, :]
        li_row = lir_ref[0, 0, h:h + 1, :]
        m_prev = m_scr[h:h + 1, :]
        c_prev = c_scr[h]
        n_prev = n_scr[h:h + 1, :]

        d = jnp.where(causal, b_col - b_row + li_row, -jnp.inf)
        inter = b_col + m_prev
        m_t = jnp.maximum(inter, jnp.max(d, axis=1, keepdims=True))
        pm = jnp.exp(d - m_t)
        w_inter = jnp.exp(inter - m_t)
        sqk = _dot(qb, kf.astype(BF16), _NT) * pm
        num = w_inter * _dot(qb, c_prev.astype(BF16), _NT) + _dot(sqk.astype(BF16), vb)
        den = w_inter * jnp.sum(qf * n_prev, axis=1, keepdims=True) + jnp.sum(sqk, axis=1, keepdims=True)
        h_ref[:, h * C_DV:(h + 1) * C_DV] = num / jnp.maximum(jnp.abs(den), jnp.exp(-m_t))

        b_last = b_row[:, L - 1:L]
        g_row = b_last - b_row + li_row
        g_col = b_last - b_col + li_col
        m_new = jnp.maximum(b_last + m_prev, jnp.max(g_row, axis=1, keepdims=True))
        w_c = jnp.exp(b_last + m_prev - m_new)
        kw = kf * jnp.exp(g_col - m_new)
        c_new = w_c * c_prev + _dot(vb, kw.astype(BF16), _TN)
        n_new = w_c * n_prev + jnp.sum(kw, axis=0, keepdims=True)
        c_scr[h] = c_new
        n_scr[h:h + 1, :] = n_new
        m_scr[h:h + 1, :] = m_new

    @pl.when(c == pl.num_programs(1) - 1)
    def _():
        cout_ref[0] = c_scr[...]
        nout_ref[0] = n_scr[...]
        mout_ref[0] = m_scr[...]


def _mlstm(p, li_row, b_row, li_col, b_col, c0, n0, m0, *, nb, nc, L, row_block0, n_total, prev_out=None):
    aliased = prev_out is not None
    rb = lambda b, c: row_block0 + b * nc + c
    in_specs = [pl.BlockSpec((L, C_QK), lambda b, c: (rb(b, c), 0)),
                pl.BlockSpec((L, C_QK), lambda b, c: (rb(b, c), 1)),
                pl.BlockSpec((L, C_V), lambda b, c: (rb(b, c), 2 * C_QK // C_V)),
                pl.BlockSpec((1, 1, C_HEADS, L), lambda b, c: (b, c, 0, 0)),
                pl.BlockSpec((1, 1, C_HEADS, L), lambda b, c: (b, c, 0, 0)),
                pl.BlockSpec((1, 1, L, C_HEADS), lambda b, c: (b, c, 0, 0)),
                pl.BlockSpec((1, 1, L, C_HEADS), lambda b, c: (b, c, 0, 0)),
                pl.BlockSpec((1, C_HEADS, C_DV, C_DQK), lambda b, c: (b, 0, 0, 0)),
                pl.BlockSpec((1, C_HEADS, C_DQK), lambda b, c: (b, 0, 0)),
                pl.BlockSpec((1, C_HEADS, 1), lambda b, c: (b, 0, 0))]
    args = [p, p, p, li_row, b_row, li_col, b_col, c0, n0, m0]
    io_alias = {}
    if aliased:
        in_specs.append(pl.BlockSpec(memory_space=pl.ANY))
        args.append(prev_out)
        io_alias = {len(args) - 1: 0}
    return pl.pallas_call(
        functools.partial(_mlstm_kernel, L=L, aliased=aliased),
        grid=(nb, nc),
        in_specs=in_specs,
        out_specs=[pl.BlockSpec((L, C_V), lambda b, c: (rb(b, c), 0)),
                   pl.BlockSpec((1, C_HEADS, C_DV, C_DQK), lambda b, c: (b, 0, 0, 0)),
                   pl.BlockSpec((1, C_HEADS, C_DQK), lambda b, c: (b, 0, 0)),
                   pl.BlockSpec((1, C_HEADS, 1), lambda b, c: (b, 0, 0))],
        out_shape=[jax.ShapeDtypeStruct((n_total, C_V), F32),
                   jax.ShapeDtypeStruct((nb, C_HEADS, C_DV, C_DQK), F32),
                   jax.ShapeDtypeStruct((nb, C_HEADS, C_DQK), F32),
                   jax.ShapeDtypeStruct((nb, C_HEADS, 1), F32)],
        scratch_shapes=[pltpu.VMEM((C_HEADS, C_DV, C_DQK), F32),
                        pltpu.VMEM((C_HEADS, C_DQK), F32),
                        pltpu.VMEM((C_HEADS, 1), F32)],
        input_output_aliases=io_alias,
        compiler_params=_params(("parallel", "arbitrary")),
        name="mlstm_L%d" % L,
    )(*args)


def _mixout_kernel(*refs, hd, act, has_b):
    if has_b:
        h_ref, a_ref, r_ref, b_ref, ga_ref, wo_ref, gf_ref, wrh_ref, wrl_ref, br_ref, h1_ref, xn_ref, info_ref = refs
    else:
        h_ref, a_ref, r_ref, ga_ref, wo_ref, gf_ref, wrh_ref, wrl_ref, br_ref, h1_ref, xn_ref, info_ref = refs
    a = a_ref[...]
    r = r_ref[...]
    gate = r * _sigmoid(r) if act == "silu" else _sigmoid(r)
    parts = []
    for hh in range(a.shape[1] // hd):
        sl = slice(hh * hd, (hh + 1) * hd)
        parts.append((_rms(a[:, sl], ga_ref[...]) * gate[:, sl]).astype(BF16))
    if has_b:
        parts.append(b_ref[...].astype(BF16))
    cat = jnp.concatenate(parts, axis=1)
    h1 = h_ref[...] + _dot(cat, wo_ref[...])
    h1_ref[...] = h1
    xn = _rms(h1, gf_ref[...])
    xn_ref[...] = xn
    xh, xl = _split(xn)
    logits = _dot(xh, wrh_ref[...]) + _dot(xh, wrl_ref[...]) + _dot(xl, wrh_ref[...]) + br_ref[...]

    lane = lax.broadcasted_iota(jnp.int32, logits.shape, 1)
    lanef = lane.astype(F32)
    is_g = lane < N_GROUPS
    gl = jnp.where(is_g, logits, -jnp.inf)
    gmax = jnp.max(gl, axis=1, keepdims=True)
    gidx = jnp.min(jnp.where(gl == gmax, lanef, float(LANES)), axis=1, keepdims=True)
    wg = 1.0 / jnp.sum(jnp.where(is_g, jnp.exp(gl - gmax), 0.0), axis=1, keepdims=True)
    lo = N_GROUPS + N_EXP * gidx
    el = jnp.where((lanef >= lo) & (lanef < lo + N_EXP), logits, -jnp.inf)
    m1 = jnp.max(el, axis=1, keepdims=True)
    i1 = jnp.min(jnp.where(el == m1, lanef, float(LANES)), axis=1, keepdims=True)
    el2 = jnp.where(lanef == i1, -jnp.inf, el)
    m2 = jnp.max(el2, axis=1, keepdims=True)
    i2 = jnp.min(jnp.where(el2 == m2, lanef, float(LANES)), axis=1, keepdims=True)
    t = jnp.exp(m2 - m1)
    w1 = wg / (1.0 + t)
    w2 = wg * t / (1.0 + t)
    info_ref[...] = jnp.where(lane == 0, i1 - N_GROUPS,
                              jnp.where(lane == 1, i2 - N_GROUPS,
                                        jnp.where(lane == 2, w1, jnp.where(lane == 3, w2, 0.0))))


def _mixout(h, a, a_col, r, r_col, b, g_head, w_o, g_ffn, wr_hi, wr_lo, b_r, *, hd, act):
    n, d = h.shape
    tm = _tile(n, 256, SUBLANES)
    wa = w_o.shape[0] if b is None else w_o.shape[0] - B_W
    has_b = b is not None
    row = lambda i: (i, 0)
    const = lambda i: (0, 0)
    in_specs = [pl.BlockSpec((tm, d), row),
                pl.BlockSpec((tm, wa), lambda i: (i, a_col)),
                pl.BlockSpec((tm, wa), lambda i: (i, r_col))]
    args = [h, a, r]
    if has_b:
        in_specs.append(pl.BlockSpec((tm, B_W), row))
        args.append(b)
    in_specs += [pl.BlockSpec((1, hd), const), pl.BlockSpec(w_o.shape, const), pl.BlockSpec((1, d), const),
                 pl.BlockSpec((d, LANES), const), pl.BlockSpec((d, LANES), const), pl.BlockSpec((1, LANES), const)]
    args += [g_head.reshape(1, hd), w_o, g_ffn.reshape(1, d), wr_hi, wr_lo, b_r]
    return pl.pallas_call(
        functools.partial(_mixout_kernel, hd=hd, act=act, has_b=has_b),
        grid=(n // tm,),
        in_specs=in_specs,
        out_specs=[pl.BlockSpec((tm, d), row), pl.BlockSpec((tm, d), row), pl.BlockSpec((tm, LANES), row)],
        out_shape=[jax.ShapeDtypeStruct((n, d), F32), jax.ShapeDtypeStruct((n, d), F32),
                   jax.ShapeDtypeStruct((n, LANES), F32)],
        compiler_params=_params(("parallel",)),
        name="mixout_" + act,
    )(*args)


def _row_copy(src_hbm, dst_ref, src_row, dst_row, sem):
    return pltpu.make_async_copy(src_hbm.at[pl.ds(src_row, 1), :], dst_ref.at[pl.ds(dst_row, 1), :], sem)


def _gather_kernel(idx_ref, src_hbm, o_ref, sem, *, tm):
    def start(r, carry):
        _row_copy(src_hbm, o_ref, idx_ref[0, 0, r], r, sem).start()
        return carry

    def wait(r, carry):
        _row_copy(src_hbm, o_ref, 0, r, sem).wait()
        return carry

    lax.fori_loop(0, tm, start, 0)
    lax.fori_loop(0, tm, wait, 0)


def _gather_rows(src, idx, *, tm):
    n_slots = idx.shape[0]
    d = src.shape[1]
    nt = n_slots // tm
    return pl.pallas_call(
        functools.partial(_gather_kernel, tm=tm),
        grid=(nt,),
        in_specs=[pl.BlockSpec((1, 1, tm), lambda i: (i, 0, 0), memory_space=pltpu.SMEM),
                  pl.BlockSpec(memory_space=pl.ANY)],
        out_specs=pl.BlockSpec((tm, d), lambda i: (i, 0)),
        out_shape=jax.ShapeDtypeStruct((n_slots, d), src.dtype),
        scratch_shapes=[pltpu.SemaphoreType.DMA(())],
        compiler_params=_params(("arbitrary",)),
        name="moe_gather",
    )(idx.reshape(nt, 1, tm), src)


def _expert_kernel(te_ref, nu_ref, x_ref, sw_ref, wg_ref, wu_ref, wd_ref, y_ref):
    i = pl.program_id(0)

    @pl.when(i < nu_ref[0])
    def _():
        x = x_ref[...].astype(BF16)
        g = _dot(x, wg_ref[0].astype(BF16))
        u = _dot(x, wu_ref[0].astype(BF16))
        hg = g * _sigmoid(g) * u * sw_ref[...]
        y_ref[...] = _dot(hg.astype(BF16), wd_ref[0].astype(BF16))

    @pl.when(i >= nu_ref[0])
    def _():
        y_ref[...] = jnp.zeros_like(y_ref)


def _experts(xs, slot_w, tile_expert, n_used, w_gate, w_up, w_down, *, tm):
    n_slots, d = xs.shape
    f = w_gate.shape[-1]
    grid_spec = pltpu.PrefetchScalarGridSpec(
        num_scalar_prefetch=2,
        grid=(n_slots // tm,),
        in_specs=[pl.BlockSpec((tm, d), lambda i, te, nu: (i, 0)),
                  pl.BlockSpec((tm, 1), lambda i, te, nu: (i, 0)),
                  pl.BlockSpec((1, d, f), lambda i, te, nu: (te[i], 0, 0)),
                  pl.BlockSpec((1, d, f), lambda i, te, nu: (te[i], 0, 0)),
                  pl.BlockSpec((1, f, d), lambda i, te, nu: (te[i], 0, 0))],
        out_specs=pl.BlockSpec((tm, d), lambda i, te, nu: (i, 0)),
    )
    return pl.pallas_call(
        _expert_kernel,
        grid_spec=grid_spec,
        out_shape=jax.ShapeDtypeStruct((n_slots, d), F32),
        compiler_params=_params(("arbitrary",)),
        name="moe_experts",
    )(tile_expert, n_used, xs, slot_w, w_gate, w_up, w_down)


def _combine_kernel(i0_ref, i1_ref, h_ref, g_ref, ys_hbm, o_ref, buf0, buf1, sem, *, tm, final_norm):
    def start(r, carry):
        _row_copy(ys_hbm, buf0, i0_ref[0, 0, r], r, sem.at[0]).start()
        _row_copy(ys_hbm, buf1, i1_ref[0, 0, r], r, sem.at[1]).start()
        return carry

    def wait(r, carry):
        _row_copy(ys_hbm, buf0, 0, r, sem.at[0]).wait()
        _row_copy(ys_hbm, buf1, 0, r, sem.at[1]).wait()
        return carry

    lax.fori_loop(0, tm, start, 0)
    lax.fori_loop(0, tm, wait, 0)
    h2 = h_ref[...] + (buf0[...] + buf1[...])
    o_ref[...] = _rms(h2, g_ref[...]) if final_norm else h2


def _combine(h, ys, pos0, pos1, g, *, final_norm):
    n, d = h.shape
    tm = _tile(n, 256, SUBLANES)
    nt = n // tm
    idx_spec = pl.BlockSpec((1, 1, tm), lambda i: (i, 0, 0), memory_space=pltpu.SMEM)
    return pl.pallas_call(
        functools.partial(_combine_kernel, tm=tm, final_norm=final_norm),
        grid=(nt,),
        in_specs=[idx_spec, idx_spec,
                  pl.BlockSpec((tm, d), lambda i: (i, 0)),
                  pl.BlockSpec((1, d), lambda i: (0, 0)),
                  pl.BlockSpec(memory_space=pl.ANY)],
        out_specs=pl.BlockSpec((tm, d), lambda i: (i, 0)),
        out_shape=jax.ShapeDtypeStruct((n, d), F32),
        scratch_shapes=[pltpu.VMEM((tm, d), F32), pltpu.VMEM((tm, d), F32), pltpu.SemaphoreType.DMA((2,))],
        compiler_params=_params(("arbitrary",)),
        name="moe_combine",
    )(pos0.reshape(nt, 1, tm), pos1.reshape(nt, 1, tm), h, g.reshape(1, d), ys)


def _moe(h1, xn, info, w_gate, w_up, w_down, expert_base, g_next, *, final_norm):
    n, d = h1.shape
    n_e = N_GROUPS * N_EXP
    tm = _tile(2 * n, 256, SUBLANES)
    n_tiles = (2 * n) // tm + n_e
    n_slots = n_tiles * tm
    eid = info[:, :2].astype(jnp.int32).reshape(-1)
    wts = info[:, 2:4].reshape(-1)
    order = jnp.argsort(eid, stable=True).astype(jnp.int32)
    rank = jnp.argsort(order).astype(jnp.int32)
    counts = jnp.sum(eid[:, None] == jnp.arange(n_e, dtype=jnp.int32)[None, :], axis=0).astype(jnp.int32)
    starts = jnp.cumsum(counts) - counts
    padded = ((counts + tm - 1) // tm) * tm
    pad_starts = jnp.cumsum(padded) - padded
    n_used = (jnp.sum(padded) // tm).astype(jnp.int32)
    slot_of_pick = pad_starts[eid] + rank - starts[eid]
    slots = jnp.arange(n_slots, dtype=jnp.int32)
    e_of_slot = jnp.clip(jnp.searchsorted(pad_starts, slots, side="right") - 1, 0, n_e - 1).astype(jnp.int32)
    r_of_slot = slots - pad_starts[e_of_slot]
    live = r_of_slot < counts[e_of_slot]
    pick_of_slot = order[jnp.clip(starts[e_of_slot] + r_of_slot, 0, 2 * n - 1)]
    src_token = jnp.where(live, pick_of_slot // 2, 0).astype(jnp.int32)
    slot_w = jnp.where(live, wts[pick_of_slot], 0.0).reshape(n_slots, 1)
    tile_expert = (expert_base + e_of_slot[::tm]).astype(jnp.int32)

    xs = _gather_rows(xn, src_token, tm=tm)
    ys = _experts(xs, slot_w, tile_expert, n_used.reshape(1), w_gate, w_up, w_down, tm=tm)
    pos = slot_of_pick.reshape(n, 2)
    return _combine(h1, ys, pos[:, 0], pos[:, 1], g_next, final_norm=final_norm)


def _router_weights(w_rg, b_rg, w_re, b_re):
    d = w_rg.shape[0]
    pad = LANES - N_GROUPS - N_GROUPS * N_EXP
    w = jnp.concatenate([w_rg, w_re, jnp.zeros((d, pad), F32)], axis=1)
    b = jnp.concatenate([b_rg, b_re, jnp.zeros((pad,), F32)]).reshape(1, LANES)
    hi = w.astype(BF16)
    lo = (w - hi.astype(F32)).astype(BF16)
    return hi, lo, b


def _rows_to_lanes(x, nb, t, nch):
    return x.reshape(nb, t, nch).transpose(0, 2, 1).reshape(nb * nch, t)


def _even_layer(h, dims, g_mix, w_in, w_g2, b_g, b_f, g_a, w_o, state_gla, ck, cv, clf):
    nb, t, db, ds, npr, n = dims
    d = h.shape[1]
    qa, ka, va, ra, ga, qb, kb, vb, fb = jnp.split(
        w_in, [A_QK, 2 * A_QK, 2 * A_QK + A_V, 2 * A_QK + 2 * A_V, 2 * A_QK + 2 * A_V + A_RANK,
               2 * A_QK + 2 * A_V + A_RANK + B_W, 2 * A_QK + 2 * A_V + A_RANK + 2 * B_W,
               2 * A_QK + 2 * A_V + A_RANK + 3 * B_W], axis=1)
    w_packed = jnp.concatenate(
        [qa, ka, va, ra, qb, kb, vb, ga, fb, jnp.zeros((d, LANES - A_RANK - B_HEADS), F32)], axis=1).astype(BF16)
    q_off = 2 * A_QK + 2 * A_V
    colscale = jnp.ones((1, MAIN_W), F32).at[:, q_off:q_off + B_W].set(B_DH ** -0.5)
    p, pb = _proj(h, g_mix, w_packed, colscale)

    fcol = MAIN_W + A_RANK
    fb_p = _rows_to_lanes(p[:npr, fcol:fcol + B_HEADS], nb, t, B_HEADS)
    zeros_col = lambda r: jnp.zeros((r, 1), F32)
    logf_p, f_p = _gate_scan(fb_p, jnp.tile(b_f, nb).reshape(-1, 1), zeros_col(nb * B_HEADS),
                             mode="fox", act_start=0, valid_start=N_PAD, valid_end=t, seg=None)
    past = ck.shape[1]
    fb_s = _rows_to_lanes(p[npr:, fcol:fcol + B_HEADS], db, ds, B_HEADS)
    clf_t = clf.transpose(0, 2, 1).reshape(db * B_HEADS, past)
    x_s = _pad_lanes(jnp.concatenate([clf_t, fb_s], axis=1))
    logf_s, f_s = _gate_scan(x_s, jnp.tile(b_f, db).reshape(-1, 1), zeros_col(db * B_HEADS),
                             mode="fox", act_start=past, valid_start=0, valid_end=past + ds, seg=None)

    nc = t // CHUNK
    oa, s_p = _gla(p, jnp.zeros((nb, A_HEADS, A_DK, A_DV), F32), w_g2, b_g,
                   nb=nb, nc=nc, L=CHUNK, row_block0=0, lead_pad=N_PAD, n_total=n)
    oa, s_s = _gla(p, state_gla, w_g2, b_g, nb=db, nc=1, L=ds, row_block0=npr // ds, lead_pad=0,
                   n_total=n, prev_out=oa)

    ob = _flash(pb, f_p.reshape(nb, B_HEADS, 1, t), nb=nb, t=t, n_total=n)
    ob = _fox_sample(pb, ck.reshape(db, past, B_W), cv.reshape(db, past, B_W),
                     f_s.reshape(db, B_HEADS, 1, -1), ob, nb=db, ds=ds, row_block0=npr // ds)

    kcol = q_off + B_W
    states = dict(
        s_p=s_p, s_s=s_s,
        k_p=p[:npr, kcol:kcol + B_W].reshape(nb, t, B_HEADS, B_DH)[:, N_PAD:],
        v_p=p[:npr, kcol + B_W:kcol + 2 * B_W].reshape(nb, t, B_HEADS, B_DH)[:, N_PAD:],
        f_p=logf_p.reshape(nb, B_HEADS, t).transpose(0, 2, 1)[:, N_PAD:],
        k_s=p[npr:, kcol:kcol + B_W].reshape(db, ds, B_HEADS, B_DH),
        v_s=p[npr:, kcol + B_W:kcol + 2 * B_W].reshape(db, ds, B_HEADS, B_DH),
        f_s=logf_s[:, past:past + ds].reshape(db, B_HEADS, ds).transpose(0, 2, 1))
    return (oa, 0, p, (2 * A_QK + A_V) // A_V, ob, g_a, w_o.astype(BF16)), states


def _chunk_rows(x, nb, nch, nc, L):
    x = x[:, :nc * L].reshape(nb, nch, nc, L)
    return x.transpose(0, 2, 1, 3), x.transpose(0, 2, 3, 1)


def _odd_layer(h, dims, g_mix, w_in, b_gate, g_c, w_o, c0, n0, m0):
    nb, t, db, ds, npr, n = dims
    d = h.shape[1]
    w_packed = jnp.concatenate(
        [w_in, jnp.zeros((d, LANES - 2 * C_HEADS), F32)], axis=1).astype(BF16)
    p, _ = _proj(h, g_mix, w_packed, jnp.ones((1, MAIN_W), F32))

    ng = 2 * C_HEADS
    isf = (jnp.arange(ng) >= C_HEADS).astype(F32)
    nc = t // CHUNK

    def gates(rows, nbatch, tt, valid_start, seg):
        x = _pad_lanes(_rows_to_lanes(rows, nbatch, tt, ng))
        val, cum = _gate_scan(x, jnp.tile(b_gate, nbatch).reshape(-1, 1), jnp.tile(isf, nbatch).reshape(-1, 1),
                              mode="mlstm", act_start=0, valid_start=valid_start, valid_end=tt, seg=seg)
        return val.reshape(nbatch, ng, -1), cum.reshape(nbatch, ng, -1)

    val_p, cum_p = gates(p[:npr, MAIN_W:MAIN_W + ng], nb, t, N_PAD, CHUNK)
    val_s, cum_s = gates(p[npr:, MAIN_W:MAIN_W + ng], db, ds, 0, ds)

    def chunked(val, cum, nbatch, ncs, L):
        li_row, li_col = _chunk_rows(val[:, :C_HEADS].reshape(nbatch * C_HEADS, -1), nbatch, C_HEADS, ncs, L)
        b_row, b_col = _chunk_rows(cum[:, C_HEADS:].reshape(nbatch * C_HEADS, -1), nbatch, C_HEADS, ncs, L)
        return li_row, b_row, li_col, b_col

    zc = jnp.zeros((nb, C_HEADS, C_DV, C_DQK), F32)
    zn = jnp.zeros((nb, C_HEADS, C_DQK), F32)
    zm = jnp.zeros((nb, C_HEADS, 1), F32)
    hm, c_p, n_p, m_p = _mlstm(p, *chunked(val_p, cum_p, nb, nc, CHUNK), zc, zn, zm,
                               nb=nb, nc=nc, L=CHUNK, row_block0=0, n_total=n)
    hm, c_s, n_s, m_s = _mlstm(p, *chunked(val_s, cum_s, db, 1, ds), c0, n0, m0.reshape(db, C_HEADS, 1),
                               nb=db, nc=1, L=ds, row_block0=npr // ds, n_total=n, prev_out=hm)
    states = dict(c_p=c_p, n_p=n_p, m_p=m_p.reshape(nb, C_HEADS), c_s=c_s, n_s=n_s, m_s=m_s.reshape(db, C_HEADS))
    return (hm, 0, p, (2 * C_QK + C_V) // C_V, None, g_c, w_o.astype(BF16)), states


def kernel(x_prompt, x_sample, state_gla, cache_fox_k, cache_fox_v, cache_fox_logf, state_mlstm_c, state_mlstm_n, state_mlstm_m, meta_tokens, norm_mix, norm_ffn, norm_final, w_in_even, w_gla_gate2, b_gla_gate, b_fox_f, g_gla_out, w_out_even, w_in_odd, b_mlstm_gate, g_mlstm_out, w_out_odd, w_router_group, b_router_group, w_router_expert, b_router_expert, w_exp_gate, w_exp_up, w_exp_down):
    nb, seq, d = x_prompt.shape
    db, ds, _ = x_sample.shape
    t = LEAD + seq
    npr, nsm = nb * t, db * ds
    n = npr + nsm
    dims = (nb, t, db, ds, npr, n)
    depth = norm_mix.shape[0]
    n_e = N_GROUPS * N_EXP
    f = w_exp_gate.shape[-1]

    hp = jnp.concatenate([jnp.zeros((nb, N_PAD, d), F32),
                          jnp.broadcast_to(meta_tokens[None], (nb, N_META, d)), x_prompt], axis=1)
    h = jnp.concatenate([hp.reshape(npr, d), x_sample.reshape(nsm, d)], axis=0)
    wg_all = w_exp_gate.reshape(depth * n_e, d, f)
    wu_all = w_exp_up.reshape(depth * n_e, d, f)
    wd_all = w_exp_down.reshape(depth * n_e, f, d)

    even, odd = [], []
    for l in range(depth):
        if l % 2 == 0:
            e = l // 2
            mix, st = _even_layer(h, dims, norm_mix[l], w_in_even[e], w_gla_gate2[e], b_gla_gate[e], b_fox_f[e],
                                  g_gla_out[e], w_out_even[e], state_gla[e], cache_fox_k[e], cache_fox_v[e],
                                  cache_fox_logf[e])
            even.append(st)
            hd, act = A_DV, "silu"
        else:
            o = l // 2
            mix, st = _odd_layer(h, dims, norm_mix[l], w_in_odd[o], b_mlstm_gate[o], g_mlstm_out[o], w_out_odd[o],
                                 state_mlstm_c[o], state_mlstm_n[o], state_mlstm_m[o])
            odd.append(st)
            hd, act = C_DV, "sigmoid"
        a, a_col, r, r_col, b, g_head, w_o = mix
        wr_hi, wr_lo, b_r = _router_weights(w_router_group[l], b_router_group[l], w_router_expert[l],
                                            b_router_expert[l])
        h1, xn, info = _mixout(h, a, a_col, r, r_col, b, g_head, w_o, norm_ffn[l], wr_hi, wr_lo, b_r, hd=hd, act=act)
        last = l == depth - 1
        h = _moe(h1, xn, info, wg_all, wu_all, wd_all, l * n_e, norm_final if last else norm_ffn[l], final_norm=last)

    y_prompt = h[:npr].reshape(nb, t, d)[:, LEAD:]
    y_sample = h[npr:].reshape(db, ds, d)
    stack = lambda sts, key: jnp.stack([s[key] for s in sts])
    return (y_prompt, y_sample,
            stack(even, "s_p"), stack(even, "k_p"), stack(even, "v_p"), stack(even, "f_p"),
            stack(odd, "c_p"), stack(odd, "n_p"), stack(odd, "m_p"),
            stack(even, "s_s"), stack(even, "k_s"), stack(even, "v_s"), stack(even, "f_s"),
            stack(odd, "c_s"), stack(odd, "n_s"), stack(odd, "m_s"))
```

```python
import functools

import jax
import jax.numpy as jnp
from jax import lax
from jax.experimental import pallas as pl
from jax.experimental.pallas import tpu as pltpu

F32 = jnp.float32
BF16 = jnp.bfloat16

CHUNK = 64
N_META = 16
LEAD = 128
N_PAD = LEAD - N_META
A_HEADS, A_DK, A_DV, A_RANK = 4, 64, 128, 16
A_GATE_NORM = 16.0
B_HEADS, B_DH = 4, 128
C_HEADS, C_DQK, C_DV = 4, 128, 256
GATE_CAP = 15.0
N_GROUPS, N_EXP = 4, 8
EPS = 1e-6
NEG = -1e30
LOG2E = 1.4426950408889634
A_QK = A_HEADS * A_DK
A_V = A_HEADS * A_DV
B_W = B_HEADS * B_DH
C_QK = C_HEADS * C_DQK
C_V = C_HEADS * C_DV

LANES = 128
SUBLANES = 8
VMEM_LIMIT_BYTES = 56 * 1024 * 1024
GLA_SUB = 16
MAIN_W = 3072
PROJ_W = MAIN_W + LANES

_NT = (((1,), (1,)), ((), ()))
_TN = (((0,), (0,)), ((), ()))
_NN = (((1,), (0,)), ((), ()))


def _params(sem):
    return pltpu.CompilerParams(dimension_semantics=sem, vmem_limit_bytes=VMEM_LIMIT_BYTES)


def _tile(n, pref, mult):
    t = (min(pref, n) // mult) * mult
    while t > mult and n % t:
        t -= mult
    assert t >= mult and n % t == 0, (n, pref, mult)
    return t


def _dot(a, b, dims=_NN):
    return lax.dot_general(a, b, dims, preferred_element_type=F32)


def _split(x):
    hi = x.astype(BF16)
    lo = (x - hi.astype(F32)).astype(BF16)
    return hi, lo


def _dot3(a, b, dims=_NN):
    ah, al = _split(a)
    bh, bl = _split(b)
    return _dot(ah, bh, dims) + _dot(ah, bl, dims) + _dot(al, bh, dims)


def _log_sigmoid(x):
    return jnp.minimum(x, 0.0) - jnp.log1p(jnp.exp(-jnp.abs(x)))


def _sigmoid(x):
    return 1.0 / (1.0 + jnp.exp(-x))


def _rms(x, g):
    return x * lax.rsqrt(jnp.mean(x * x, axis=-1, keepdims=True) + EPS) * g


def _cumsum_rows(x):
    n = x.shape[0]
    row = lax.broadcasted_iota(jnp.int32, x.shape, 0)
    s = 1
    while s < n:
        x = x + jnp.where(row >= s, pltpu.roll(x, s, axis=0), 0.0)
        s *= 2
    return x


def _proj_kernel(x_ref, g_ref, w_ref, cs_ref, o_ref, ob_ref, *, col_chunk):
    xn = _rms(x_ref[...], g_ref[...]).astype(BF16)
    for c0 in range(0, PROJ_W, col_chunk):
        c1 = min(c0 + col_chunk, PROJ_W)
        y = _dot(xn, w_ref[:, c0:c1])
        o_ref[:, c0:c1] = y
        if c0 < MAIN_W:
            m1 = min(c1, MAIN_W)
            ob_ref[:, c0:m1] = (y[:, :m1 - c0] * cs_ref[:, c0:m1]).astype(BF16)


def _proj_precise_kernel(x_ref, g_ref, w_ref, prev_ref, o_ref, *, col_chunk):
    del prev_ref
    xn = _rms(x_ref[...], g_ref[...])
    for c0 in range(0, PROJ_W, col_chunk):
        c1 = min(c0 + col_chunk, PROJ_W)
        o_ref[:, c0:c1] = _dot3(xn, w_ref[:, c0:c1])


def _proj(h, g, w_packed, colscale, *, n_rows):
    n, d = h.shape
    tm = _tile(n_rows, 512, 16)
    return pl.pallas_call(
        functools.partial(_proj_kernel, col_chunk=640),
        grid=(n_rows // tm,),
        in_specs=[pl.BlockSpec((tm, d), lambda i: (i, 0)),
                  pl.BlockSpec((1, d), lambda i: (0, 0)),
                  pl.BlockSpec((d, PROJ_W), lambda i: (0, 0)),
                  pl.BlockSpec((1, MAIN_W), lambda i: (0, 0))],
        out_specs=[pl.BlockSpec((tm, PROJ_W), lambda i: (i, 0)),
                   pl.BlockSpec((tm, MAIN_W), lambda i: (i, 0))],
        out_shape=[jax.ShapeDtypeStruct((n, PROJ_W), F32),
                   jax.ShapeDtypeStruct((n_rows, MAIN_W), BF16)],
        compiler_params=_params(("parallel",)),
        name="proj",
    )(h, g.reshape(1, d), w_packed.astype(BF16), colscale)


def _proj_precise(h, g, w_packed, prev, *, row0):
    n, d = h.shape
    tm = _tile(n - row0, 512, SUBLANES)
    assert row0 % tm == 0
    return pl.pallas_call(
        functools.partial(_proj_precise_kernel, col_chunk=640),
        grid=((n - row0) // tm,),
        in_specs=[pl.BlockSpec((tm, d), lambda i: (row0 // tm + i, 0)),
                  pl.BlockSpec((1, d), lambda i: (0, 0)),
                  pl.BlockSpec((d, PROJ_W), lambda i: (0, 0)),
                  pl.BlockSpec(memory_space=pl.ANY)],
        out_specs=pl.BlockSpec((tm, PROJ_W), lambda i: (row0 // tm + i, 0)),
        out_shape=jax.ShapeDtypeStruct((n, PROJ_W), F32),
        input_output_aliases={3: 0},
        compiler_params=_params(("parallel",)),
        name="proj_precise",
    )(h, g.reshape(1, d), w_packed, prev)


def _gate_scan_kernel(x_ref, bias_ref, isf_ref, val_ref, cum_ref, *, mode, act_start, valid_start, valid_end, seg, stride):
    x = x_ref[...]
    lane = lax.broadcasted_iota(jnp.int32, x.shape, 1)
    valid = (lane >= valid_start) & (lane < valid_end)
    if mode == "fox":
        val = jnp.where(lane >= act_start, _log_sigmoid(x + bias_ref[...]), x)
        val = jnp.where(valid, val, 0.0)
        add = val
    else:
        gate = GATE_CAP * jnp.tanh((x + bias_ref[...]) / GATE_CAP)
        isf = isf_ref[...] > 0.5
        val = jnp.where(isf, jnp.where(valid, _log_sigmoid(gate), 0.0),
                        jnp.where(valid, gate, -jnp.inf))
        add = jnp.where(isf, val, 0.0)
    val_ref[...] = val
    n = x.shape[1]
    pos = lane if seg is None else lane % seg
    limit = n if seg is None else seg
    s = stride
    while s < limit:
        add = add + jnp.where(pos >= s, pltpu.roll(add, s, axis=1), 0.0)
        s *= 2
    cum_ref[...] = add


def _gate_scan(x, bias, isf, *, mode, act_start, valid_start, valid_end, seg, stride=1):
    r, n = x.shape
    full = lambda shape: pl.BlockSpec(shape, lambda i: (0,) * len(shape))
    return pl.pallas_call(
        functools.partial(_gate_scan_kernel, mode=mode, act_start=act_start,
                          valid_start=valid_start, valid_end=valid_end, seg=seg, stride=stride),
        grid=(1,),
        in_specs=[full((r, n)), full(bias.shape), full((r, 1))],
        out_specs=[full((r, n)), full((r, n))],
        out_shape=[jax.ShapeDtypeStruct((r, n), F32)] * 2,
        compiler_params=_params(("arbitrary",)),
        name="gate_scan_" + mode,
    )(x, bias, isf)


def _pad_lanes(x):
    n = x.shape[-1]
    m = -(-n // LANES) * LANES
    return x if m == n else jnp.pad(x, ((0, 0), (0, m - n)))


def _gla_kernel(qk_ref, v_ref, sm_ref, s0_ref, wg2_ref, wg2t_ref, bgr_ref, bgc_ref, *rest,
                L, sub, lead_pad, aliased, precise):
    if aliased:
        rest = rest[1:]
    o_ref, sout_ref, s_scr = rest
    c = pl.program_id(1)
    nh, dk, dv = A_HEADS, A_DK, A_DV
    cast = (lambda x: x) if precise else (lambda x: x.astype(BF16))
    mm = _dot3 if precise else _dot

    @pl.when(c == 0)
    def _():
        s_scr[...] = jnp.zeros_like(s_scr)
        for h in range(nh):
            s_scr[h * dk:(h + 1) * dk, h * dv:(h + 1) * dv] = s0_ref[0, h]

    qk = qk_ref[...]
    q = qk[:, :A_QK] * (A_DK ** -0.5)
    k = qk[:, A_QK:]
    v = v_ref[...]
    ga = sm_ref[:, :A_RANK]
    row = lax.broadcasted_iota(jnp.int32, (L, 1), 0)
    valid = (c * L + row) >= lead_pad
    z = _dot3(ga, wg2_ref[...]) + bgr_ref[...]
    loga = jnp.where(valid, _log_sigmoid(z) / A_GATE_NORM, 0.0)
    k = jnp.where(valid, k, 0.0)
    b = _cumsum_rows(loga)
    b_last = b[L - 1:L, :]
    lane_t = lax.broadcasted_iota(jnp.int32, (1, L), 1)
    zt = _dot3(wg2t_ref[...], ga, _NT) + bgc_ref[...]
    logat = jnp.where((c * L + lane_t) >= lead_pad, _log_sigmoid(zt) / A_GATE_NORM, 0.0)
    b_last_col = jnp.sum(logat, axis=1, keepdims=True)

    qhead = lax.broadcasted_iota(jnp.int32, (1, A_QK), 1) // dk
    vhead = lax.broadcasted_iota(jnp.int32, (1, A_V), 1) // dv
    vb = cast(v)
    zero_b = jnp.zeros((), vb.dtype)
    v_bd = jnp.concatenate([jnp.where(vhead == h, vb, zero_b) for h in range(nh)], axis=0)

    rows_all = lax.broadcasted_iota(jnp.int32, (L, 1), 0)
    a_rows = []
    for i in range(L // sub):
        r0 = i * sub
        ci = jnp.zeros((1, A_QK), F32) if i == 0 else b[r0 - 1:r0, :]
        qt = cast(q[r0:r0 + sub] * jnp.exp(b[r0:r0 + sub] - ci))
        kt = cast(jnp.where(rows_all < r0 + sub, k * jnp.exp(ci - b), 0.0))
        k_stack = jnp.concatenate([jnp.where(qhead == h, kt, zero_b) for h in range(nh)], axis=0)
        a_rows.append(mm(qt, k_stack, _NT))
    a = a_rows[0] if len(a_rows) == 1 else jnp.concatenate(a_rows, axis=0)
    t_idx = lax.broadcasted_iota(jnp.int32, (L, nh * L), 0)
    s_idx = lax.broadcasted_iota(jnp.int32, (L, nh * L), 1) % L
    a = jnp.where(s_idx <= t_idx, a, 0.0)
    o_intra = mm(cast(a), v_bd)

    s_full = s_scr[...]
    o_inter = mm(cast(q * jnp.exp(b)), cast(s_full))
    o_ref[...] = o_inter + o_intra

    k_hat = cast(k * jnp.exp(b_last - b))
    upd = mm(k_hat, vb, _TN)
    khead_col = lax.broadcasted_iota(jnp.int32, (A_QK, 1), 0) // dk
    s_new = jnp.exp(b_last_col) * s_full + jnp.where(khead_col == vhead, upd, 0.0)
    s_scr[...] = s_new

    @pl.when(c == pl.num_programs(1) - 1)
    def _():
        for h in range(nh):
            sout_ref[0, h] = s_new[h * dk:(h + 1) * dk, h * dv:(h + 1) * dv]


def _gla(p, s0, wg2, bg, *, nb, nc, L, row_block0, lead_pad, n_total, prev_out=None, precise=False):
    aliased = prev_out is not None
    rb = lambda b, c: row_block0 + b * nc + c
    in_specs = [pl.BlockSpec((L, 2 * A_QK), lambda b, c: (rb(b, c), 0)),
                pl.BlockSpec((L, A_V), lambda b, c: (rb(b, c), 2 * A_QK // A_V)),
                pl.BlockSpec((L, LANES), lambda b, c: (rb(b, c), MAIN_W // LANES)),
                pl.BlockSpec((1, A_HEADS, A_DK, A_DV), lambda b, c: (b, 0, 0, 0)),
                pl.BlockSpec((A_RANK, A_QK), lambda b, c: (0, 0)),
                pl.BlockSpec((A_QK, A_RANK), lambda b, c: (0, 0)),
                pl.BlockSpec((1, A_QK), lambda b, c: (0, 0)),
                pl.BlockSpec((A_QK, 1), lambda b, c: (0, 0))]
    args = [p, p, p, s0, wg2, wg2.T, bg.reshape(1, A_QK), bg.reshape(A_QK, 1)]
    io_alias = {}
    if aliased:
        in_specs.append(pl.BlockSpec(memory_space=pl.ANY))
        args.append(prev_out)
        io_alias = {len(args) - 1: 0}
    return pl.pallas_call(
        functools.partial(_gla_kernel, L=L, sub=min(GLA_SUB, L), lead_pad=lead_pad, aliased=aliased,
                          precise=precise),
        grid=(nb, nc),
        in_specs=in_specs,
        out_specs=[pl.BlockSpec((L, A_V), lambda b, c: (rb(b, c), 0)),
                   pl.BlockSpec((1, A_HEADS, A_DK, A_DV), lambda b, c: (b, 0, 0, 0))],
        out_shape=[jax.ShapeDtypeStruct((n_total, A_V), F32),
                   jax.ShapeDtypeStruct((nb, A_HEADS, A_DK, A_DV), F32)],
        scratch_shapes=[pltpu.VMEM((A_QK, A_V), F32)],
        input_output_aliases=io_alias,
        compiler_params=_params(("parallel", "arbitrary")),
        name="gla_L%d" % L,
    )(*args)


def _flash_kernel(qi_ref, kj_ref, q_ref, k_ref, v_ref, fk_ref, o_ref, m_scr, l_scr, acc_scr, *, blk, lead_pad):
    step = pl.program_id(2)
    i = qi_ref[step]
    j = kj_ref[step]

    @pl.when(j == 0)
    def _():
        m_scr[...] = jnp.full_like(m_scr, -jnp.inf)
        l_scr[...] = jnp.zeros_like(l_scr)
        acc_scr[...] = jnp.zeros_like(acc_scr)

    def update(masked):
        s = _dot(q_ref[...], k_ref[...], _NT) - fk_ref[0, 0]
        if masked:
            qpos = i * blk + lax.broadcasted_iota(jnp.int32, (blk, blk), 0)
            kpos = j * blk + lax.broadcasted_iota(jnp.int32, (blk, blk), 1)
            s = jnp.where((kpos <= qpos) & (kpos >= lead_pad), s, NEG)
        m_prev = m_scr[...]
        m_new = jnp.maximum(m_prev, jnp.max(s, axis=1, keepdims=True))
        alpha = jnp.exp2(m_prev - m_new)
        p = jnp.exp2(s - m_new)
        l_scr[...] = alpha * l_scr[...] + jnp.sum(p, axis=1, keepdims=True)
        acc_scr[...] = alpha * acc_scr[...] + _dot(p.astype(BF16), v_ref[...])
        m_scr[...] = m_new

    edge = (j == i) | (j == 0)
    pl.when(edge)(functools.partial(update, True))
    pl.when(jnp.logical_not(edge))(functools.partial(update, False))

    @pl.when(j == i)
    def _():
        o_ref[...] = acc_scr[...] / l_scr[...]


def _flash(pb, fk, *, nb, t, n_total):
    blk = _tile(t, 640, LANES)
    nq = t // blk
    pairs = [(i, j) for i in range(nq) for j in range(i + 1)]
    qi = jnp.asarray([p[0] for p in pairs], jnp.int32)
    kj = jnp.asarray([p[1] for p in pairs], jnp.int32)
    qc, kc, vc = (A_QK * 2 + A_V * 2) // LANES, (A_QK * 2 + A_V * 2 + B_W) // LANES, (A_QK * 2 + A_V * 2 + 2 * B_W) // LANES
    grid_spec = pltpu.PrefetchScalarGridSpec(
        num_scalar_prefetch=2,
        grid=(nb, B_HEADS, len(pairs)),
        in_specs=[pl.BlockSpec((blk, B_DH), lambda b, h, s, qi, kj: (b * nq + qi[s], qc + h)),
                  pl.BlockSpec((blk, B_DH), lambda b, h, s, qi, kj: (b * nq + kj[s], kc + h)),
                  pl.BlockSpec((blk, B_DH), lambda b, h, s, qi, kj: (b * nq + kj[s], vc + h)),
                  pl.BlockSpec((1, 1, 1, blk), lambda b, h, s, qi, kj: (b, h, 0, kj[s]))],
        out_specs=pl.BlockSpec((blk, B_DH), lambda b, h, s, qi, kj: (b * nq + qi[s], h)),
        scratch_shapes=[pltpu.VMEM((blk, 1), F32), pltpu.VMEM((blk, 1), F32), pltpu.VMEM((blk, B_DH), F32)],
    )
    return pl.pallas_call(
        functools.partial(_flash_kernel, blk=blk, lead_pad=N_PAD),
        grid_spec=grid_spec,
        out_shape=jax.ShapeDtypeStruct((n_total, B_W), F32),
        compiler_params=_params(("parallel", "parallel", "arbitrary")),
        name="fox_flash",
    )(qi, kj, pb, pb, pb, fk)


def _fox_sample_kernel(q_ref, kn_ref, vn_ref, kc_ref, vc_ref, fc_ref, fn_ref, prev_ref, o_ref, *, ds):
    del prev_ref
    nh = B_HEADS
    stack = lambda ref: jnp.concatenate([ref[:, h * B_DH:(h + 1) * B_DH] for h in range(nh)], axis=0)
    q = stack(q_ref) * (B_DH ** -0.5 * LOG2E)
    rows = nh * ds
    n_c = kc_ref.shape[1]
    qh_c = lax.broadcasted_iota(jnp.int32, (rows, n_c), 0) // ds
    kh_c = lax.broadcasted_iota(jnp.int32, (rows, n_c), 1) % nh
    s_c = _dot3(q, kc_ref[0], _NT) - fc_ref[0]
    s_c = jnp.where(qh_c == kh_c, s_c, NEG)
    r_i = lax.broadcasted_iota(jnp.int32, (rows, rows), 0)
    c_i = lax.broadcasted_iota(jnp.int32, (rows, rows), 1)
    s_n = _dot3(q, stack(kn_ref), _NT) - fn_ref[0]
    s_n = jnp.where((r_i // ds == c_i // ds) & (c_i % ds <= r_i % ds), s_n, NEG)
    m = jnp.maximum(jnp.max(s_c, axis=1, keepdims=True), jnp.max(s_n, axis=1, keepdims=True))
    p_c = jnp.exp2(s_c - m)
    p_n = jnp.exp2(s_n - m)
    l = jnp.sum(p_c, axis=1, keepdims=True) + jnp.sum(p_n, axis=1, keepdims=True)
    o = (_dot3(p_c, vc_ref[0]) + _dot3(p_n, stack(vn_ref))) / l
    for h in range(nh):
        o_ref[:, h * B_DH:(h + 1) * B_DH] = o[h * ds:(h + 1) * ds]


def _fox_sample(pb, kc, vc, f_cache, f_new, prev_out, *, nb, ds, row_block0):
    n_c = kc.shape[1]
    base = (A_QK * 2 + A_V * 2) // B_W
    rb = lambda b: row_block0 + b
    return pl.pallas_call(
        functools.partial(_fox_sample_kernel, ds=ds),
        grid=(nb,),
        in_specs=[pl.BlockSpec((ds, B_W), lambda b: (rb(b), base)),
                  pl.BlockSpec((ds, B_W), lambda b: (rb(b), base + 1)),
                  pl.BlockSpec((ds, B_W), lambda b: (rb(b), base + 2)),
                  pl.BlockSpec((1, n_c, B_DH), lambda b: (b, 0, 0)),
                  pl.BlockSpec((1, n_c, B_DH), lambda b: (b, 0, 0)),
                  pl.BlockSpec((1, 1, n_c), lambda b: (b, 0, 0)),
                  pl.BlockSpec((1, 1, B_HEADS * ds), lambda b: (b, 0, 0)),
                  pl.BlockSpec(memory_space=pl.ANY)],
        out_specs=pl.BlockSpec((ds, B_W), lambda b: (rb(b), 0)),
        out_shape=jax.ShapeDtypeStruct(prev_out.shape, F32),
        input_output_aliases={7: 0},
        compiler_params=_params(("parallel",)),
        name="fox_sample",
    )(pb, pb, pb, kc, vc, f_cache, f_new, prev_out)


def _mlstm_kernel(q_ref, k_ref, v_ref, lir_ref, br_ref, lic_ref, bc_ref, c0_ref, n0_ref, m0_ref, *rest,
                  L, aliased):
    if aliased:
        rest = rest[1:]
    h_ref, cout_ref, nout_ref, mout_ref, c_scr, n_scr, m_scr = rest
    c = pl.program_id(1)

    @pl.when(c == 0)
    def _():
        c_scr[...] = c0_ref[0]
        n_scr[...] = n0_ref[0]
        m_scr[...] = m0_ref[0]

    t_idx = lax.broadcasted_iota(jnp.int32, (L, L), 0)
    s_idx = lax.broadcasted_iota(jnp.int32, (L, L), 1)
    causal = s_idx <= t_idx
    for h in range(C_HEADS):
        qf = q_ref[:, h * C_DQK:(h + 1) * C_DQK]
        kf = k_ref[:, h * C_DQK:(h + 1) * C_DQK] * (C_DQK ** -0.5)
        vb = v_ref[:, h * C_DV:(h + 1) * C_DV].astype(BF16)
        qb = qf.astype(BF16)
        b_col = bc_ref[0, 0, :, h:h + 1]
        li_col = lic_ref[0, 0, :, h:h + 1]
        b_row = br_ref[0, 0, h:h + 1, :]
        li_row = lir_ref[0, 0, h:h + 1, :]
        m_prev = m_scr[h:h + 1, :]
        c_prev = c_scr[h]
        n_prev = n_scr[h:h + 1, :]

        d = jnp.where(causal, b_col - b_row + li_row, -jnp.inf)
        inter = b_col + m_prev
        m_t = jnp.maximum(inter, jnp.max(d, axis=1, keepdims=True))
        pm = jnp.exp(d - m_t)
        w_inter = jnp.exp(inter - m_t)
        sqk = _dot(qb, kf.astype(BF16), _NT) * pm
        num = w_inter * _dot(qb, c_prev.astype(BF16), _NT) + _dot(sqk.astype(BF16), vb)
        den = w_inter * jnp.sum(qf * n_prev, axis=1, keepdims=True) + jnp.sum(sqk, axis=1, keepdims=True)
        h_ref[:, h * C_DV:(h + 1) * C_DV] = num / jnp.maximum(jnp.abs(den), jnp.exp(-m_t))

        b_last = b_row[:, L - 1:L]
        g_row = b_last - b_row + li_row
        g_col = b_last - b_col + li_col
        m_new = jnp.maximum(b_last + m_prev, jnp.max(g_row, axis=1, keepdims=True))
        w_c = jnp.exp(b_last + m_prev - m_new)
        kw = kf * jnp.exp(g_col - m_new)
        c_new = w_c * c_prev + _dot(vb, kw.astype(BF16), _TN)
        n_new = w_c * n_prev + jnp.sum(kw, axis=0, keepdims=True)
        c_scr[h] = c_new
        n_scr[h:h + 1, :] = n_new
        m_scr[h:h + 1, :] = m_new

    @pl.when(c == pl.num_programs(1) - 1)
    def _():
        cout_ref[0] = c_scr[...]
        nout_ref[0] = n_scr[...]
        mout_ref[0] = m_scr[...]


def _mlstm(p, li_row, b_row, li_col, b_col, c0, n0, m0, *, nb, nc, L, row_block0, n_total, prev_out=None):
    aliased = prev_out is not None
    rb = lambda b, c: row_block0 + b * nc + c
    in_specs = [pl.BlockSpec((L, C_QK), lambda b, c: (rb(b, c), 0)),
                pl.BlockSpec((L, C_QK), lambda b, c: (rb(b, c), 1)),
                pl.BlockSpec((L, C_V), lambda b, c: (rb(b, c), 2 * C_QK // C_V)),
                pl.BlockSpec((1, 1, C_HEADS, L), lambda b, c: (b, c, 0, 0)),
                pl.BlockSpec((1, 1, C_HEADS, L), lambda b, c: (b, c, 0, 0)),
                pl.BlockSpec((1, 1, L, C_HEADS), lambda b, c: (b, c, 0, 0)),
                pl.BlockSpec((1, 1, L, C_HEADS), lambda b, c: (b, c, 0, 0)),
                pl.BlockSpec((1, C_HEADS, C_DV, C_DQK), lambda b, c: (b, 0, 0, 0)),
                pl.BlockSpec((1, C_HEADS, C_DQK), lambda b, c: (b, 0, 0)),
                pl.BlockSpec((1, C_HEADS, 1), lambda b, c: (b, 0, 0))]
    args = [p, p, p, li_row, b_row, li_col, b_col, c0, n0, m0]
    io_alias = {}
    if aliased:
        in_specs.append(pl.BlockSpec(memory_space=pl.ANY))
        args.append(prev_out)
        io_alias = {len(args) - 1: 0}
    return pl.pallas_call(
        functools.partial(_mlstm_kernel, L=L, aliased=aliased),
        grid=(nb, nc),
        in_specs=in_specs,
        out_specs=[pl.BlockSpec((L, C_V), lambda b, c: (rb(b, c), 0)),
                   pl.BlockSpec((1, C_HEADS, C_DV, C_DQK), lambda b, c: (b, 0, 0, 0)),
                   pl.BlockSpec((1, C_HEADS, C_DQK), lambda b, c: (b, 0, 0)),
                   pl.BlockSpec((1, C_HEADS, 1), lambda b, c: (b, 0, 0))],
        out_shape=[jax.ShapeDtypeStruct((n_total, C_V), F32),
                   jax.ShapeDtypeStruct((nb, C_HEADS, C_DV, C_DQK), F32),
                   jax.ShapeDtypeStruct((nb, C_HEADS, C_DQK), F32),
                   jax.ShapeDtypeStruct((nb, C_HEADS, 1), F32)],
        scratch_shapes=[pltpu.VMEM((C_HEADS, C_DV, C_DQK), F32),
                        pltpu.VMEM((C_HEADS, C_DQK), F32),
                        pltpu.VMEM((C_HEADS, 1), F32)],
        input_output_aliases=io_alias,
        compiler_params=_params(("parallel", "arbitrary")),
        name="mlstm_L%d" % L,
    )(*args)


def _mixout_kernel(*refs, hd, act, has_b, precise, n_prev):
    refs = list(refs)
    h_ref, a_ref, r_ref = refs[:3]
    b_ref = refs[3] if has_b else None
    k = 4 if has_b else 3
    ga_ref, wo_ref, gf_ref, wrh_ref, wrl_ref, br_ref, cnt0_ref = refs[k:k + 7]
    h1_ref, xn_ref, info_ref, cnt_ref, cnt_scr = refs[k + 7 + n_prev:]
    cast = (lambda x: x) if precise else (lambda x: x.astype(BF16))
    a = a_ref[...]
    r = r_ref[...]
    gate = r * _sigmoid(r) if act == "silu" else _sigmoid(r)
    parts = []
    for hh in range(a.shape[1] // hd):
        sl = slice(hh * hd, (hh + 1) * hd)
        parts.append(cast(_rms(a[:, sl], ga_ref[...]) * gate[:, sl]))
    if has_b:
        parts.append(cast(b_ref[...]))
    cat = jnp.concatenate(parts, axis=1)
    h1 = h_ref[...] + (_dot3(cat, wo_ref[...]) if precise else _dot(cat, wo_ref[...]))
    h1_ref[...] = h1
    xn = _rms(h1, gf_ref[...])
    xn_ref[...] = xn
    xh, xl = _split(xn)
    logits = _dot(xh, wrh_ref[...]) + _dot(xh, wrl_ref[...]) + _dot(xl, wrh_ref[...]) + br_ref[...]

    lane = lax.broadcasted_iota(jnp.int32, logits.shape, 1)
    lanef = lane.astype(F32)
    is_g = lane < N_GROUPS
    gl = jnp.where(is_g, logits, -jnp.inf)
    gmax = jnp.max(gl, axis=1, keepdims=True)
    gidx = jnp.min(jnp.where(gl == gmax, lanef, float(LANES)), axis=1, keepdims=True)
    wg = 1.0 / jnp.sum(jnp.where(is_g, jnp.exp(gl - gmax), 0.0), axis=1, keepdims=True)
    lo = N_GROUPS + N_EXP * gidx
    el = jnp.where((lanef >= lo) & (lanef < lo + N_EXP), logits, -jnp.inf)
    m1 = jnp.max(el, axis=1, keepdims=True)
    i1 = jnp.min(jnp.where(el == m1, lanef, float(LANES)), axis=1, keepdims=True)
    el2 = jnp.where(lanef == i1, -jnp.inf, el)
    m2 = jnp.max(el2, axis=1, keepdims=True)
    i2 = jnp.min(jnp.where(el2 == m2, lanef, float(LANES)), axis=1, keepdims=True)
    t = jnp.exp(m2 - m1)
    w1 = wg / (1.0 + t)
    w2 = wg * t / (1.0 + t)
    e1 = i1 - N_GROUPS
    e2 = i2 - N_GROUPS

    @pl.when(pl.program_id(0) == 0)
    def _():
        cnt_scr[...] = cnt0_ref[...]

    tm = logits.shape[0]
    pick = jnp.where((lanef == e1) | (lanef == e2), 1.0, 0.0)
    earlier = (lax.broadcasted_iota(jnp.int32, (tm, tm), 1) < lax.broadcasted_iota(jnp.int32, (tm, tm), 0))
    before = _dot(jnp.where(earlier, 1.0, 0.0).astype(BF16), pick.astype(BF16)) + cnt_scr[...]
    r1 = jnp.sum(jnp.where(lanef == e1, before, 0.0), axis=1, keepdims=True)
    r2 = jnp.sum(jnp.where(lanef == e2, before, 0.0), axis=1, keepdims=True)
    cnt_new = cnt_scr[...] + jnp.sum(pick, axis=0, keepdims=True)
    cnt_scr[...] = cnt_new
    cnt_ref[...] = cnt_new
    info_ref[...] = jnp.where(lane == 0, e1, jnp.where(lane == 1, e2, jnp.where(lane == 2, w1, jnp.where(
        lane == 3, w2, jnp.where(lane == 4, r1, jnp.where(lane == 5, r2, 0.0))))))


def _mixout(h, a, a_col, r, r_col, b, g_head, w_o, g_ffn, wr_hi, wr_lo, b_r, counts0, *, hd, act,
            row0, n_rows, prev=None):
    n, d = h.shape
    precise = prev is not None
    tm = _tile(n_rows, 256, SUBLANES)
    assert row0 % tm == 0
    blk0 = row0 // tm
    wa = w_o.shape[0] if b is None else w_o.shape[0] - B_W
    has_b = b is not None
    row = lambda i: (blk0 + i, 0)
    const = lambda i: (0, 0)
    in_specs = [pl.BlockSpec((tm, d), row),
                pl.BlockSpec((tm, wa), lambda i: (blk0 + i, a_col)),
                pl.BlockSpec((tm, wa), lambda i: (blk0 + i, r_col))]
    args = [h, a, r]
    if has_b:
        in_specs.append(pl.BlockSpec((tm, B_W), row))
        args.append(b)
    in_specs += [pl.BlockSpec((1, hd), const), pl.BlockSpec(w_o.shape, const), pl.BlockSpec((1, d), const),
                 pl.BlockSpec((d, LANES), const), pl.BlockSpec((d, LANES), const), pl.BlockSpec((1, LANES), const),
                 pl.BlockSpec((1, LANES), const)]
    args += [g_head.reshape(1, hd), w_o if precise else w_o.astype(BF16), g_ffn.reshape(1, d), wr_hi, wr_lo, b_r,
             counts0]
    io_alias = {}
    if precise:
        for k, arr in enumerate(prev):
            in_specs.append(pl.BlockSpec(memory_space=pl.ANY))
            args.append(arr)
            io_alias[len(args) - 1] = k
    return pl.pallas_call(
        functools.partial(_mixout_kernel, hd=hd, act=act, has_b=has_b, precise=precise, n_prev=len(io_alias)),
        grid=(n_rows // tm,),
        in_specs=in_specs,
        out_specs=[pl.BlockSpec((tm, d), row), pl.BlockSpec((tm, d), row), pl.BlockSpec((tm, LANES), row),
                   pl.BlockSpec((1, LANES), const)],
        out_shape=[jax.ShapeDtypeStruct((n, d), F32), jax.ShapeDtypeStruct((n, d), F32),
                   jax.ShapeDtypeStruct((n, LANES), F32), jax.ShapeDtypeStruct((1, LANES), F32)],
        scratch_shapes=[pltpu.VMEM((1, LANES), F32)],
        input_output_aliases=io_alias,
        compiler_params=_params(("arbitrary",)),
        name="mixout_" + act + ("_precise" if precise else ""),
    )(*args)


def _row_copy(src_hbm, dst_ref, src_row, dst_row, sem):
    return pltpu.make_async_copy(src_hbm.at[pl.ds(src_row, 1), :], dst_ref.at[pl.ds(dst_row, 1), :], sem)


def _gather_kernel(idx_ref, src_hbm, o_ref, sem, *, tm):
    def start(r, carry):
        _row_copy(src_hbm, o_ref, idx_ref[0, 0, r], r, sem).start()
        return carry

    def wait(r, carry):
        _row_copy(src_hbm, o_ref, 0, r, sem).wait()
        return carry

    lax.fori_loop(0, tm, start, 0)
    lax.fori_loop(0, tm, wait, 0)


def _gather_rows(src, idx, *, tm):
    n_slots = idx.shape[0]
    d = src.shape[1]
    nt = n_slots // tm
    return pl.pallas_call(
        functools.partial(_gather_kernel, tm=tm),
        grid=(nt,),
        in_specs=[pl.BlockSpec((1, 1, tm), lambda i: (i, 0, 0), memory_space=pltpu.SMEM),
                  pl.BlockSpec(memory_space=pl.ANY)],
        out_specs=pl.BlockSpec((tm, d), lambda i: (i, 0)),
        out_shape=jax.ShapeDtypeStruct((n_slots, d), src.dtype),
        scratch_shapes=[pltpu.SemaphoreType.DMA(())],
        compiler_params=_params(("arbitrary",)),
        name="moe_gather",
    )(idx.reshape(nt, 1, tm), src)


def _expert_kernel(te_ref, nu_ref, x_ref, wg_ref, wu_ref, wd_ref, y_ref, wgb, wub, wdb):
    i = pl.program_id(0)
    live = i < nu_ref[0]
    new_expert = (i == 0) | (te_ref[i] != te_ref[jnp.maximum(i - 1, 0)])

    @pl.when(live & new_expert)
    def _():
        wgb[...] = wg_ref[0].astype(BF16)
        wub[...] = wu_ref[0].astype(BF16)
        wdb[...] = wd_ref[0].astype(BF16)

    @pl.when(live)
    def _():
        x = x_ref[...].astype(BF16)
        g = _dot(x, wgb[...])
        u = _dot(x, wub[...])
        y_ref[...] = _dot((g * _sigmoid(g) * u).astype(BF16), wdb[...])

    @pl.when(jnp.logical_not(live))
    def _():
        y_ref[...] = jnp.zeros_like(y_ref)


def _experts(xs, tile_expert, n_used, w_gate, w_up, w_down, *, tm):
    n_slots, d = xs.shape
    f = w_gate.shape[-1]
    grid_spec = pltpu.PrefetchScalarGridSpec(
        num_scalar_prefetch=2,
        grid=(n_slots // tm,),
        in_specs=[pl.BlockSpec((tm, d), lambda i, te, nu: (i, 0)),
                  pl.BlockSpec((1, d, f), lambda i, te, nu: (te[i], 0, 0)),
                  pl.BlockSpec((1, d, f), lambda i, te, nu: (te[i], 0, 0)),
                  pl.BlockSpec((1, f, d), lambda i, te, nu: (te[i], 0, 0))],
        out_specs=pl.BlockSpec((tm, d), lambda i, te, nu: (i, 0)),
        scratch_shapes=[pltpu.VMEM((d, f), BF16), pltpu.VMEM((d, f), BF16), pltpu.VMEM((f, d), BF16)],
    )
    return pl.pallas_call(
        _expert_kernel,
        grid_spec=grid_spec,
        out_shape=jax.ShapeDtypeStruct((n_slots, d), F32),
        compiler_params=_params(("arbitrary",)),
        name="moe_experts",
    )(tile_expert, n_used, xs, w_gate, w_up, w_down)


def _combine_kernel(i0_ref, i1_ref, h_ref, info_ref, g_ref, ys_hbm, o_ref, buf0, buf1, sem, *, tm, final_norm):
    def start(r, carry):
        _row_copy(ys_hbm, buf0, i0_ref[0, 0, r], r, sem.at[0]).start()
        _row_copy(ys_hbm, buf1, i1_ref[0, 0, r], r, sem.at[1]).start()
        return carry

    def wait(r, carry):
        _row_copy(ys_hbm, buf0, 0, r, sem.at[0]).wait()
        _row_copy(ys_hbm, buf1, 0, r, sem.at[1]).wait()
        return carry

    lax.fori_loop(0, tm, start, 0)
    lax.fori_loop(0, tm, wait, 0)
    h2 = h_ref[...] + (info_ref[:, 2:3] * buf0[...] + info_ref[:, 3:4] * buf1[...])
    o_ref[...] = _rms(h2, g_ref[...]) if final_norm else h2


def _combine(h, ys, info, pos0, pos1, g, *, final_norm):
    n, d = h.shape
    tm = _tile(n, 256, SUBLANES)
    nt = n // tm
    idx_spec = pl.BlockSpec((1, 1, tm), lambda i: (i, 0, 0), memory_space=pltpu.SMEM)
    return pl.pallas_call(
        functools.partial(_combine_kernel, tm=tm, final_norm=final_norm),
        grid=(nt,),
        in_specs=[idx_spec, idx_spec,
                  pl.BlockSpec((tm, d), lambda i: (i, 0)),
                  pl.BlockSpec((tm, LANES), lambda i: (i, 0)),
                  pl.BlockSpec((1, d), lambda i: (0, 0)),
                  pl.BlockSpec(memory_space=pl.ANY)],
        out_specs=pl.BlockSpec((tm, d), lambda i: (i, 0)),
        out_shape=jax.ShapeDtypeStruct((n, d), F32),
        scratch_shapes=[pltpu.VMEM((tm, d), F32), pltpu.VMEM((tm, d), F32), pltpu.SemaphoreType.DMA((2,))],
        compiler_params=_params(("arbitrary",)),
        name="moe_combine",
    )(pos0.reshape(nt, 1, tm), pos1.reshape(nt, 1, tm), h, info, g.reshape(1, d), ys)


def _moe(h1, xn, info, counts_f, w_gate, w_up, w_down, expert_base, g_next, *, final_norm):
    n, d = h1.shape
    n_e = N_GROUPS * N_EXP
    tm = _tile(2 * n, 256, SUBLANES)
    n_tiles = (2 * n) // tm + n_e
    n_slots = n_tiles * tm
    experts = jnp.arange(n_e, dtype=jnp.int32)
    counts = counts_f[0, :n_e].astype(jnp.int32)
    starts = jnp.cumsum(counts) - counts
    padded = ((counts + tm - 1) // tm) * tm
    pad_ends = jnp.cumsum(padded)
    pad_starts = pad_ends - padded
    n_used = (pad_ends[-1] // tm).astype(jnp.int32)
    eid = info[:, 0:2].astype(jnp.int32)
    rank = info[:, 4:6].astype(jnp.int32)
    lookup = lambda table, e: jnp.sum(jnp.where(e[..., None] == experts, table, 0), axis=-1)
    slot_of_pick = lookup(pad_starts, eid) + rank
    tok = jnp.broadcast_to(jnp.arange(n, dtype=jnp.int32)[:, None], (n, 2)).reshape(-1)
    _, tok_sorted = lax.sort_key_val(slot_of_pick.reshape(-1), tok)
    slots = jnp.arange(n_slots, dtype=jnp.int32)
    e_of_slot = jnp.minimum(jnp.sum(slots[:, None] >= pad_ends[None, :], axis=1), n_e - 1).astype(jnp.int32)
    r_of_slot = slots - lookup(pad_starts, e_of_slot)
    live = r_of_slot < lookup(counts, e_of_slot)
    src_token = jnp.where(live, tok_sorted[jnp.clip(lookup(starts, e_of_slot) + r_of_slot, 0, 2 * n - 1)], 0)
    tile_expert = (expert_base + e_of_slot[::tm]).astype(jnp.int32)

    xs = _gather_rows(xn, src_token.astype(jnp.int32), tm=tm)
    ys = _experts(xs, tile_expert, n_used.reshape(1), w_gate, w_up, w_down, tm=tm)
    return _combine(h1, ys, info, slot_of_pick[:, 0], slot_of_pick[:, 1], g_next, final_norm=final_norm)


def _router_weights(w_rg, b_rg, w_re, b_re):
    d = w_rg.shape[0]
    pad = LANES - N_GROUPS - N_GROUPS * N_EXP
    w = jnp.concatenate([w_rg, w_re, jnp.zeros((d, pad), F32)], axis=1)
    b = jnp.concatenate([b_rg, b_re, jnp.zeros((pad,), F32)]).reshape(1, LANES)
    hi = w.astype(BF16)
    lo = (w - hi.astype(F32)).astype(BF16)
    return hi, lo, b


def _rows_to_lanes(x, nb, t, nch):
    return x.reshape(nb, t, nch).transpose(0, 2, 1).reshape(nb * nch, t)


def _even_layer(h, dims, g_mix, w_in, w_g2, b_g, b_f, g_a, w_o, state_gla, ck, cv, clf):
    nb, t, db, ds, npr, n = dims
    d = h.shape[1]
    qa, ka, va, ra, ga, qb, kb, vb, fb = jnp.split(
        w_in, [A_QK, 2 * A_QK, 2 * A_QK + A_V, 2 * A_QK + 2 * A_V, 2 * A_QK + 2 * A_V + A_RANK,
               2 * A_QK + 2 * A_V + A_RANK + B_W, 2 * A_QK + 2 * A_V + A_RANK + 2 * B_W,
               2 * A_QK + 2 * A_V + A_RANK + 3 * B_W], axis=1)
    w_packed = jnp.concatenate(
        [qa, ka, va, ra, qb, kb, vb, ga, fb, jnp.zeros((d, LANES - A_RANK - B_HEADS), F32)], axis=1)
    q_off = 2 * A_QK + 2 * A_V
    colscale = jnp.ones((1, MAIN_W), F32).at[:, q_off:q_off + B_W].set(B_DH ** -0.5 * LOG2E)
    p, pb = _proj(h, g_mix, w_packed, colscale, n_rows=npr)
    p = _proj_precise(h, g_mix, w_packed, p, row0=npr)

    nh = B_HEADS
    fcol = MAIN_W + A_RANK
    zeros_col = lambda r: jnp.zeros((r, 1), F32)
    bias_row = lambda lanes: jnp.tile(b_f, lanes // nh).reshape(1, lanes)
    fb_p = p[:npr, fcol:fcol + nh].reshape(nb, t * nh)
    logf_p, f_p = _gate_scan(fb_p, bias_row(t * nh), zeros_col(nb), mode="fox", act_start=0,
                             valid_start=N_PAD * nh, valid_end=t * nh, seg=None, stride=nh)
    past = ck.shape[1]
    n_c = past * nh
    x_s = _pad_lanes(jnp.concatenate([clf.reshape(db, n_c), p[npr:, fcol:fcol + nh].reshape(db, ds * nh)], axis=1))
    logf_s, f_s = _gate_scan(x_s, bias_row(x_s.shape[1]), zeros_col(db), mode="fox", act_start=n_c,
                             valid_start=0, valid_end=n_c + ds * nh, seg=None, stride=nh)

    nc = t // CHUNK
    oa, s_p = _gla(p, jnp.zeros((nb, A_HEADS, A_DK, A_DV), F32), w_g2, b_g,
                   nb=nb, nc=nc, L=CHUNK, row_block0=0, lead_pad=N_PAD, n_total=n)
    oa, s_s = _gla(p, state_gla, w_g2, b_g, nb=db, nc=1, L=ds, row_block0=npr // ds, lead_pad=0,
                   n_total=n, prev_out=oa, precise=True)

    fk = (f_p * LOG2E).reshape(nb, t, nh).transpose(0, 2, 1).reshape(nb, nh, 1, t)
    ob = _flash(pb, fk, nb=nb, t=t, n_total=n)
    f_cache = (f_s[:, :n_c] * LOG2E).reshape(db, 1, n_c)
    f_new = (f_s[:, n_c:n_c + ds * nh] * LOG2E).reshape(db, ds, nh).transpose(0, 2, 1).reshape(db, 1, nh * ds)
    ob = _fox_sample(p, ck.reshape(db, n_c, B_DH), cv.reshape(db, n_c, B_DH), f_cache, f_new, ob,
                     nb=db, ds=ds, row_block0=npr // ds)

    kcol = q_off + B_W
    states = dict(
        s_p=s_p, s_s=s_s,
        k_p=p[:npr, kcol:kcol + B_W].reshape(nb, t, B_HEADS, B_DH)[:, N_PAD:],
        v_p=p[:npr, kcol + B_W:kcol + 2 * B_W].reshape(nb, t, B_HEADS, B_DH)[:, N_PAD:],
        f_p=logf_p.reshape(nb, t, nh)[:, N_PAD:],
        k_s=p[npr:, kcol:kcol + B_W].reshape(db, ds, B_HEADS, B_DH),
        v_s=p[npr:, kcol + B_W:kcol + 2 * B_W].reshape(db, ds, B_HEADS, B_DH),
        f_s=logf_s[:, n_c:n_c + ds * nh].reshape(db, ds, nh))
    return (oa, 0, p, (2 * A_QK + A_V) // A_V, ob, g_a, w_o), states


def _chunk_rows(x, nb, nch, nc, L):
    x = x[:, :nc * L].reshape(nb, nch, nc, L)
    return x.transpose(0, 2, 1, 3), x.transpose(0, 2, 3, 1)


def _odd_layer(h, dims, g_mix, w_in, b_gate, g_c, w_o, c0, n0, m0):
    nb, t, db, ds, npr, n = dims
    d = h.shape[1]
    w_packed = jnp.concatenate(
        [w_in, jnp.zeros((d, LANES - 2 * C_HEADS), F32)], axis=1)
    p, _ = _proj(h, g_mix, w_packed, jnp.ones((1, MAIN_W), F32), n_rows=npr)
    p = _proj_precise(h, g_mix, w_packed, p, row0=npr)

    ng = 2 * C_HEADS
    isf = (jnp.arange(ng) >= C_HEADS).astype(F32)
    nc = t // CHUNK

    def gates(rows, nbatch, tt, valid_start, seg):
        x = _pad_lanes(_rows_to_lanes(rows, nbatch, tt, ng))
        val, cum = _gate_scan(x, jnp.tile(b_gate, nbatch).reshape(-1, 1), jnp.tile(isf, nbatch).reshape(-1, 1),
                              mode="mlstm", act_start=0, valid_start=valid_start, valid_end=tt, seg=seg)
        return val.reshape(nbatch, ng, -1), cum.reshape(nbatch, ng, -1)

    val_p, cum_p = gates(p[:npr, MAIN_W:MAIN_W + ng], nb, t, N_PAD, CHUNK)
    val_s, cum_s = gates(p[npr:, MAIN_W:MAIN_W + ng], db, ds, 0, ds)

    def chunked(val, cum, nbatch, ncs, L):
        li_row, li_col = _chunk_rows(val[:, :C_HEADS].reshape(nbatch * C_HEADS, -1), nbatch, C_HEADS, ncs, L)
        b_row, b_col = _chunk_rows(cum[:, C_HEADS:].reshape(nbatch * C_HEADS, -1), nbatch, C_HEADS, ncs, L)
        return li_row, b_row, li_col, b_col

    zc = jnp.zeros((nb, C_HEADS, C_DV, C_DQK), F32)
    zn = jnp.zeros((nb, C_HEADS, C_DQK), F32)
    zm = jnp.zeros((nb, C_HEADS, 1), F32)
    hm, c_p, n_p, m_p = _mlstm(p, *chunked(val_p, cum_p, nb, nc, CHUNK), zc, zn, zm,
                               nb=nb, nc=nc, L=CHUNK, row_block0=0, n_total=n)
    hm, c_s, n_s, m_s = _mlstm(p, *chunked(val_s, cum_s, db, 1, ds), c0, n0, m0.reshape(db, C_HEADS, 1),
                               nb=db, nc=1, L=ds, row_block0=npr // ds, n_total=n, prev_out=hm)
    states = dict(c_p=c_p, n_p=n_p, m_p=m_p.reshape(nb, C_HEADS), c_s=c_s, n_s=n_s, m_s=m_s.reshape(db, C_HEADS))
    return (hm, 0, p, (2 * C_QK + C_V) // C_V, None, g_c, w_o), states


def kernel(x_prompt, x_sample, state_gla, cache_fox_k, cache_fox_v, cache_fox_logf, state_mlstm_c, state_mlstm_n, state_mlstm_m, meta_tokens, norm_mix, norm_ffn, norm_final, w_in_even, w_gla_gate2, b_gla_gate, b_fox_f, g_gla_out, w_out_even, w_in_odd, b_mlstm_gate, g_mlstm_out, w_out_odd, w_router_group, b_router_group, w_router_expert, b_router_expert, w_exp_gate, w_exp_up, w_exp_down):
    nb, seq, d = x_prompt.shape
    db, ds, _ = x_sample.shape
    t = LEAD + seq
    npr, nsm = nb * t, db * ds
    n = npr + nsm
    dims = (nb, t, db, ds, npr, n)
    depth = norm_mix.shape[0]
    n_e = N_GROUPS * N_EXP
    f = w_exp_gate.shape[-1]

    hp = jnp.concatenate([jnp.zeros((nb, N_PAD, d), F32),
                          jnp.broadcast_to(meta_tokens[None], (nb, N_META, d)), x_prompt], axis=1)
    h = jnp.concatenate([hp.reshape(npr, d), x_sample.reshape(nsm, d)], axis=0)
    wg_all = w_exp_gate.reshape(depth * n_e, d, f)
    wu_all = w_exp_up.reshape(depth * n_e, d, f)
    wd_all = w_exp_down.reshape(depth * n_e, f, d)

    even, odd = [], []
    for l in range(depth):
        if l % 2 == 0:
            e = l // 2
            mix, st = _even_layer(h, dims, norm_mix[l], w_in_even[e], w_gla_gate2[e], b_gla_gate[e], b_fox_f[e],
                                  g_gla_out[e], w_out_even[e], state_gla[e], cache_fox_k[e], cache_fox_v[e],
                                  cache_fox_logf[e])
            even.append(st)
            hd, act = A_DV, "silu"
        else:
            o = l // 2
            mix, st = _odd_layer(h, dims, norm_mix[l], w_in_odd[o], b_mlstm_gate[o], g_mlstm_out[o], w_out_odd[o],
                                 state_mlstm_c[o], state_mlstm_n[o], state_mlstm_m[o])
            odd.append(st)
            hd, act = C_DV, "sigmoid"
        a, a_col, r, r_col, b, g_head, w_o = mix
        wr_hi, wr_lo, b_r = _router_weights(w_router_group[l], b_router_group[l], w_router_expert[l],
                                            b_router_expert[l])
        mix_args = (h, a, a_col, r, r_col, b, g_head, w_o, norm_ffn[l], wr_hi, wr_lo, b_r)
        h1, xn, info, counts = _mixout(*mix_args, jnp.zeros((1, LANES), F32), hd=hd, act=act, row0=0, n_rows=npr)
        h1, xn, info, counts = _mixout(*mix_args, counts, hd=hd, act=act, row0=npr, n_rows=nsm,
                                       prev=(h1, xn, info))
        last = l == depth - 1
        h = _moe(h1, xn, info, counts, wg_all, wu_all, wd_all, l * n_e, norm_final if last else norm_ffn[l],
                 final_norm=last)

    y_prompt = h[:npr].reshape(nb, t, d)[:, LEAD:]
    y_sample = h[npr:].reshape(db, ds, d)
    stack = lambda sts, key: jnp.stack([s[key] for s in sts])
    return (y_prompt, y_sample,
            stack(even, "s_p"), stack(even, "k_p"), stack(even, "v_p"), stack(even, "f_p"),
            stack(odd, "c_p"), stack(odd, "n_p"), stack(odd, "m_p"),
            stack(even, "s_s"), stack(even, "k_s"), stack(even, "v_s"), stack(even, "f_s"),
            stack(odd, "c_s"), stack(odd, "n_s"), stack(odd, "m_s"))
```

```python
import functools

import jax
import jax.numpy as jnp
from jax import lax
from jax.experimental import pallas as pl
from jax.experimental.pallas import tpu as pltpu
from jax.experimental.pallas import tpu_sc as plsc

F32 = jnp.float32
BF16 = jnp.bfloat16

CHUNK = 64
N_META = 16
LEAD = 128
N_PAD = LEAD - N_META
A_HEADS, A_DK, A_DV, A_RANK = 4, 64, 128, 16
A_GATE_NORM = 16.0
B_HEADS, B_DH = 4, 128
C_HEADS, C_DQK, C_DV = 4, 128, 256
GATE_CAP = 15.0
N_GROUPS, N_EXP = 4, 8
EPS = 1e-6
NEG = -1e30
LOG2E = 1.4426950408889634
A_QK = A_HEADS * A_DK
A_V = A_HEADS * A_DV
B_W = B_HEADS * B_DH
C_QK = C_HEADS * C_DQK
C_V = C_HEADS * C_DV

LANES = 128
SUBLANES = 8
VMEM_LIMIT_BYTES = 56 * 1024 * 1024
GLA_SUB = 16
SC_CORES, SC_SUBCORES = 2, 16
SC_GATHER_ROWS = 16
SC_GATHER_BUFS = 4
MAIN_W = 3072
PROJ_W = MAIN_W + LANES

_NT = (((1,), (1,)), ((), ()))
_TN = (((0,), (0,)), ((), ()))
_NN = (((1,), (0,)), ((), ()))


def _params(sem):
    return pltpu.CompilerParams(dimension_semantics=sem, vmem_limit_bytes=VMEM_LIMIT_BYTES)


def _tile(n, pref, mult):
    t = (min(pref, n) // mult) * mult
    while t > mult and n % t:
        t -= mult
    assert t >= mult and n % t == 0, (n, pref, mult)
    return t


def _dot(a, b, dims=_NN):
    return lax.dot_general(a, b, dims, preferred_element_type=F32)


def _split(x):
    hi = x.astype(BF16)
    lo = (x - hi.astype(F32)).astype(BF16)
    return hi, lo


def _dot3(a, b, dims=_NN):
    ah, al = _split(a)
    bh, bl = _split(b)
    return _dot(ah, bh, dims) + _dot(ah, bl, dims) + _dot(al, bh, dims)


def _log_sigmoid(x):
    return jnp.minimum(x, 0.0) - jnp.log1p(jnp.exp(-jnp.abs(x)))


def _sigmoid(x):
    return 1.0 / (1.0 + jnp.exp(-x))


def _rms(x, g):
    return x * lax.rsqrt(jnp.mean(x * x, axis=-1, keepdims=True) + EPS) * g


def _pack_bf16_pairs(x):
    w = x.shape[1] // 2
    hi = lax.bitcast_convert_type(x[:, :w].astype(BF16).astype(F32), jnp.int32)
    lo = lax.bitcast_convert_type(x[:, w:].astype(BF16).astype(F32), jnp.int32)
    return hi | lax.shift_right_logical(lo, 16)


def _unpack_bf16_pairs(p):
    hi = lax.bitcast_convert_type(p & jnp.int32(-65536), F32)
    lo = lax.bitcast_convert_type(lax.shift_left(p, 16), F32)
    return jnp.concatenate([hi, lo], axis=1)


def _cumsum_rows(x):
    n = x.shape[0]
    row = lax.broadcasted_iota(jnp.int32, x.shape, 0)
    s = 1
    while s < n:
        x = x + jnp.where(row >= s, pltpu.roll(x, s, axis=0), 0.0)
        s *= 2
    return x


def _proj_kernel(x_ref, g_ref, w_ref, cs_ref, o_ref, ob_ref, *, col_chunk):
    xn = _rms(x_ref[...], g_ref[...]).astype(BF16)
    for c0 in range(0, PROJ_W, col_chunk):
        c1 = min(c0 + col_chunk, PROJ_W)
        y = _dot(xn, w_ref[:, c0:c1])
        o_ref[:, c0:c1] = y
        if c0 < MAIN_W:
            m1 = min(c1, MAIN_W)
            ob_ref[:, c0:m1] = (y[:, :m1 - c0] * cs_ref[:, c0:m1]).astype(BF16)


def _proj_precise_kernel(x_ref, g_ref, w_ref, prev_ref, o_ref, *, col_chunk):
    del prev_ref
    xn = _rms(x_ref[...], g_ref[...])
    for c0 in range(0, PROJ_W, col_chunk):
        c1 = min(c0 + col_chunk, PROJ_W)
        o_ref[:, c0:c1] = _dot3(xn, w_ref[:, c0:c1])


def _proj(h, g, w_packed, colscale, *, n_rows):
    n, d = h.shape
    tm = _tile(n_rows, 512, 16)
    return pl.pallas_call(
        functools.partial(_proj_kernel, col_chunk=640),
        grid=(n_rows // tm,),
        in_specs=[pl.BlockSpec((tm, d), lambda i: (i, 0)),
                  pl.BlockSpec((1, d), lambda i: (0, 0)),
                  pl.BlockSpec((d, PROJ_W), lambda i: (0, 0)),
                  pl.BlockSpec((1, MAIN_W), lambda i: (0, 0))],
        out_specs=[pl.BlockSpec((tm, PROJ_W), lambda i: (i, 0)),
                   pl.BlockSpec((tm, MAIN_W), lambda i: (i, 0))],
        out_shape=[jax.ShapeDtypeStruct((n, PROJ_W), F32),
                   jax.ShapeDtypeStruct((n_rows, MAIN_W), BF16)],
        compiler_params=_params(("parallel",)),
        name="proj",
    )(h, g.reshape(1, d), w_packed.astype(BF16), colscale)


def _proj_precise(h, g, w_packed, prev, *, row0):
    n, d = h.shape
    tm = _tile(n - row0, 512, SUBLANES)
    assert row0 % tm == 0
    return pl.pallas_call(
        functools.partial(_proj_precise_kernel, col_chunk=640),
        grid=((n - row0) // tm,),
        in_specs=[pl.BlockSpec((tm, d), lambda i: (row0 // tm + i, 0)),
                  pl.BlockSpec((1, d), lambda i: (0, 0)),
                  pl.BlockSpec((d, PROJ_W), lambda i: (0, 0)),
                  pl.BlockSpec(memory_space=pl.ANY)],
        out_specs=pl.BlockSpec((tm, PROJ_W), lambda i: (row0 // tm + i, 0)),
        out_shape=jax.ShapeDtypeStruct((n, PROJ_W), F32),
        input_output_aliases={3: 0},
        compiler_params=_params(("parallel",)),
        name="proj_precise",
    )(h, g.reshape(1, d), w_packed, prev)


def _gate_scan_kernel(x_ref, bias_ref, isf_ref, val_ref, cum_ref, *, mode, act_start, valid_start, valid_end, seg, stride):
    x = x_ref[...]
    lane = lax.broadcasted_iota(jnp.int32, x.shape, 1)
    valid = (lane >= valid_start) & (lane < valid_end)
    if mode == "fox":
        val = jnp.where(lane >= act_start, _log_sigmoid(x + bias_ref[...]), x)
        val = jnp.where(valid, val, 0.0)
        add = val
    else:
        gate = GATE_CAP * jnp.tanh((x + bias_ref[...]) / GATE_CAP)
        isf = isf_ref[...] > 0.5
        val = jnp.where(isf, jnp.where(valid, _log_sigmoid(gate), 0.0),
                        jnp.where(valid, gate, -jnp.inf))
        add = jnp.where(isf, val, 0.0)
    val_ref[...] = val
    n = x.shape[1]
    pos = lane if seg is None else lane % seg
    limit = n if seg is None else seg
    s = stride
    while s < limit:
        add = add + jnp.where(pos >= s, pltpu.roll(add, s, axis=1), 0.0)
        s *= 2
    cum_ref[...] = add


def _gate_scan(x, bias, isf, *, mode, act_start, valid_start, valid_end, seg, stride=1):
    r, n = x.shape
    full = lambda shape: pl.BlockSpec(shape, lambda i: (0,) * len(shape))
    return pl.pallas_call(
        functools.partial(_gate_scan_kernel, mode=mode, act_start=act_start,
                          valid_start=valid_start, valid_end=valid_end, seg=seg, stride=stride),
        grid=(1,),
        in_specs=[full((r, n)), full(bias.shape), full((r, 1))],
        out_specs=[full((r, n)), full((r, n))],
        out_shape=[jax.ShapeDtypeStruct((r, n), F32)] * 2,
        compiler_params=_params(("arbitrary",)),
        name="gate_scan_" + mode,
    )(x, bias, isf)


def _pad_lanes(x):
    n = x.shape[-1]
    m = -(-n // LANES) * LANES
    return x if m == n else jnp.pad(x, ((0, 0), (0, m - n)))


def _gla_kernel(qk_ref, v_ref, sm_ref, s0_ref, wg2_ref, wg2t_ref, bgr_ref, bgc_ref, *rest,
                L, sub, lead_pad, aliased, precise):
    if aliased:
        rest = rest[1:]
    o_ref, sout_ref, s_scr = rest
    c = pl.program_id(1)
    nh, dk, dv = A_HEADS, A_DK, A_DV
    cast = (lambda x: x) if precise else (lambda x: x.astype(BF16))
    mm = _dot3 if precise else _dot

    @pl.when(c == 0)
    def _():
        s_scr[...] = jnp.zeros_like(s_scr)
        for h in range(nh):
            s_scr[h * dk:(h + 1) * dk, h * dv:(h + 1) * dv] = s0_ref[0, h]

    qk = qk_ref[...]
    q = qk[:, :A_QK] * (A_DK ** -0.5)
    k = qk[:, A_QK:]
    v = v_ref[...]
    ga = sm_ref[:, :A_RANK]
    row = lax.broadcasted_iota(jnp.int32, (L, 1), 0)
    valid = (c * L + row) >= lead_pad
    z = _dot3(ga, wg2_ref[...]) + bgr_ref[...]
    loga = jnp.where(valid, _log_sigmoid(z) / A_GATE_NORM, 0.0)
    k = jnp.where(valid, k, 0.0)
    b = _cumsum_rows(loga)
    b_last = b[L - 1:L, :]
    lane_t = lax.broadcasted_iota(jnp.int32, (1, L), 1)
    zt = _dot3(wg2t_ref[...], ga, _NT) + bgc_ref[...]
    logat = jnp.where((c * L + lane_t) >= lead_pad, _log_sigmoid(zt) / A_GATE_NORM, 0.0)
    b_last_col = jnp.sum(logat, axis=1, keepdims=True)

    qhead = lax.broadcasted_iota(jnp.int32, (1, A_QK), 1) // dk
    vhead = lax.broadcasted_iota(jnp.int32, (1, A_V), 1) // dv
    vb = cast(v)
    zero_b = jnp.zeros((), vb.dtype)
    v_bd = jnp.concatenate([jnp.where(vhead == h, vb, zero_b) for h in range(nh)], axis=0)

    rows_all = lax.broadcasted_iota(jnp.int32, (L, 1), 0)
    a_rows = []
    for i in range(L // sub):
        r0 = i * sub
        ci = jnp.zeros((1, A_QK), F32) if i == 0 else b[r0 - 1:r0, :]
        qt = cast(q[r0:r0 + sub] * jnp.exp(b[r0:r0 + sub] - ci))
        kt = cast(jnp.where(rows_all < r0 + sub, k * jnp.exp(ci - b), 0.0))
        k_stack = jnp.concatenate([jnp.where(qhead == h, kt, zero_b) for h in range(nh)], axis=0)
        a_rows.append(mm(qt, k_stack, _NT))
    a = a_rows[0] if len(a_rows) == 1 else jnp.concatenate(a_rows, axis=0)
    t_idx = lax.broadcasted_iota(jnp.int32, (L, nh * L), 0)
    s_idx = lax.broadcasted_iota(jnp.int32, (L, nh * L), 1) % L
    a = jnp.where(s_idx <= t_idx, a, 0.0)
    o_intra = mm(cast(a), v_bd)

    s_full = s_scr[...]
    o_inter = mm(cast(q * jnp.exp(b)), cast(s_full))
    o_ref[...] = o_inter + o_intra

    k_hat = cast(k * jnp.exp(b_last - b))
    upd = mm(k_hat, vb, _TN)
    khead_col = lax.broadcasted_iota(jnp.int32, (A_QK, 1), 0) // dk
    s_new = jnp.exp(b_last_col) * s_full + jnp.where(khead_col == vhead, upd, 0.0)
    s_scr[...] = s_new

    @pl.when(c == pl.num_programs(1) - 1)
    def _():
        for h in range(nh):
            sout_ref[0, h] = s_new[h * dk:(h + 1) * dk, h * dv:(h + 1) * dv]


def _gla(p, s0, wg2, bg, *, nb, nc, L, row_block0, lead_pad, n_total, prev_out=None, precise=False):
    aliased = prev_out is not None
    rb = lambda b, c: row_block0 + b * nc + c
    in_specs = [pl.BlockSpec((L, 2 * A_QK), lambda b, c: (rb(b, c), 0)),
                pl.BlockSpec((L, A_V), lambda b, c: (rb(b, c), 2 * A_QK // A_V)),
                pl.BlockSpec((L, LANES), lambda b, c: (rb(b, c), MAIN_W // LANES)),
                pl.BlockSpec((1, A_HEADS, A_DK, A_DV), lambda b, c: (b, 0, 0, 0)),
                pl.BlockSpec((A_RANK, A_QK), lambda b, c: (0, 0)),
                pl.BlockSpec((A_QK, A_RANK), lambda b, c: (0, 0)),
                pl.BlockSpec((1, A_QK), lambda b, c: (0, 0)),
                pl.BlockSpec((A_QK, 1), lambda b, c: (0, 0))]
    args = [p, p, p, s0, wg2, wg2.T, bg.reshape(1, A_QK), bg.reshape(A_QK, 1)]
    io_alias = {}
    if aliased:
        in_specs.append(pl.BlockSpec(memory_space=pl.ANY))
        args.append(prev_out)
        io_alias = {len(args) - 1: 0}
    return pl.pallas_call(
        functools.partial(_gla_kernel, L=L, sub=min(GLA_SUB, L), lead_pad=lead_pad, aliased=aliased,
                          precise=precise),
        grid=(nb, nc),
        in_specs=in_specs,
        out_specs=[pl.BlockSpec((L, A_V), lambda b, c: (rb(b, c), 0)),
                   pl.BlockSpec((1, A_HEADS, A_DK, A_DV), lambda b, c: (b, 0, 0, 0))],
        out_shape=[jax.ShapeDtypeStruct((n_total, A_V), F32),
                   jax.ShapeDtypeStruct((nb, A_HEADS, A_DK, A_DV), F32)],
        scratch_shapes=[pltpu.VMEM((A_QK, A_V), F32)],
        input_output_aliases=io_alias,
        compiler_params=_params(("parallel", "arbitrary")),
        name="gla_L%d" % L,
    )(*args)


def _flash_kernel(qi_ref, kj_ref, q_ref, k_ref, v_ref, fk_ref, o_ref, m_scr, l_scr, acc_scr, *, blk, lead_pad):
    step = pl.program_id(2)
    i = qi_ref[step]
    j = kj_ref[step]

    @pl.when(j == 0)
    def _():
        m_scr[...] = jnp.full_like(m_scr, -jnp.inf)
        l_scr[...] = jnp.zeros_like(l_scr)
        acc_scr[...] = jnp.zeros_like(acc_scr)

    def update(masked):
        s = _dot(q_ref[...], k_ref[...], _NT) - fk_ref[0, 0]
        if masked:
            qpos = i * blk + lax.broadcasted_iota(jnp.int32, (blk, blk), 0)
            kpos = j * blk + lax.broadcasted_iota(jnp.int32, (blk, blk), 1)
            s = jnp.where((kpos <= qpos) & (kpos >= lead_pad), s, NEG)
        m_prev = m_scr[...]
        m_new = jnp.maximum(m_prev, jnp.max(s, axis=1, keepdims=True))
        alpha = jnp.exp2(m_prev - m_new)
        p = jnp.exp2(s - m_new)
        l_scr[...] = alpha * l_scr[...] + jnp.sum(p, axis=1, keepdims=True)
        acc_scr[...] = alpha * acc_scr[...] + _dot(p.astype(BF16), v_ref[...])
        m_scr[...] = m_new

    edge = (j == i) | (j == 0)
    pl.when(edge)(functools.partial(update, True))
    pl.when(jnp.logical_not(edge))(functools.partial(update, False))

    @pl.when(j == i)
    def _():
        o_ref[...] = acc_scr[...] / l_scr[...]


def _flash(pb, fk, *, nb, t, n_total):
    blk = _tile(t, 640, LANES)
    nq = t // blk
    pairs = [(i, j) for i in range(nq) for j in range(i + 1)]
    qi = jnp.asarray([p[0] for p in pairs], jnp.int32)
    kj = jnp.asarray([p[1] for p in pairs], jnp.int32)
    qc, kc, vc = (A_QK * 2 + A_V * 2) // LANES, (A_QK * 2 + A_V * 2 + B_W) // LANES, (A_QK * 2 + A_V * 2 + 2 * B_W) // LANES
    grid_spec = pltpu.PrefetchScalarGridSpec(
        num_scalar_prefetch=2,
        grid=(nb, B_HEADS, len(pairs)),
        in_specs=[pl.BlockSpec((blk, B_DH), lambda b, h, s, qi, kj: (b * nq + qi[s], qc + h)),
                  pl.BlockSpec((blk, B_DH), lambda b, h, s, qi, kj: (b * nq + kj[s], kc + h)),
                  pl.BlockSpec((blk, B_DH), lambda b, h, s, qi, kj: (b * nq + kj[s], vc + h)),
                  pl.BlockSpec((1, 1, 1, blk), lambda b, h, s, qi, kj: (b, h, 0, kj[s]))],
        out_specs=pl.BlockSpec((blk, B_DH), lambda b, h, s, qi, kj: (b * nq + qi[s], h)),
        scratch_shapes=[pltpu.VMEM((blk, 1), F32), pltpu.VMEM((blk, 1), F32), pltpu.VMEM((blk, B_DH), F32)],
    )
    return pl.pallas_call(
        functools.partial(_flash_kernel, blk=blk, lead_pad=N_PAD),
        grid_spec=grid_spec,
        out_shape=jax.ShapeDtypeStruct((n_total, B_W), F32),
        compiler_params=_params(("parallel", "parallel", "arbitrary")),
        name="fox_flash",
    )(qi, kj, pb, pb, pb, fk)


def _fox_sample_kernel(q_ref, kn_ref, vn_ref, kc_ref, vc_ref, fc_ref, fn_ref, prev_ref, o_ref, *, ds):
    del prev_ref
    nh = B_HEADS
    stack = lambda ref: jnp.concatenate([ref[:, h * B_DH:(h + 1) * B_DH] for h in range(nh)], axis=0)
    q = stack(q_ref) * (B_DH ** -0.5 * LOG2E)
    rows = nh * ds
    n_c = kc_ref.shape[1]
    qh_c = lax.broadcasted_iota(jnp.int32, (rows, n_c), 0) // ds
    kh_c = lax.broadcasted_iota(jnp.int32, (rows, n_c), 1) % nh
    s_c = _dot3(q, kc_ref[0], _NT) - fc_ref[0]
    s_c = jnp.where(qh_c == kh_c, s_c, NEG)
    r_i = lax.broadcasted_iota(jnp.int32, (rows, rows), 0)
    c_i = lax.broadcasted_iota(jnp.int32, (rows, rows), 1)
    s_n = _dot3(q, stack(kn_ref), _NT) - fn_ref[0]
    s_n = jnp.where((r_i // ds == c_i // ds) & (c_i % ds <= r_i % ds), s_n, NEG)
    m = jnp.maximum(jnp.max(s_c, axis=1, keepdims=True), jnp.max(s_n, axis=1, keepdims=True))
    p_c = jnp.exp2(s_c - m)
    p_n = jnp.exp2(s_n - m)
    l = jnp.sum(p_c, axis=1, keepdims=True) + jnp.sum(p_n, axis=1, keepdims=True)
    o = (_dot3(p_c, vc_ref[0]) + _dot3(p_n, stack(vn_ref))) / l
    for h in range(nh):
        o_ref[:, h * B_DH:(h + 1) * B_DH] = o[h * ds:(h + 1) * ds]


def _fox_sample(pb, kc, vc, f_cache, f_new, prev_out, *, nb, ds, row_block0):
    n_c = kc.shape[1]
    base = (A_QK * 2 + A_V * 2) // B_W
    rb = lambda b: row_block0 + b
    return pl.pallas_call(
        functools.partial(_fox_sample_kernel, ds=ds),
        grid=(nb,),
        in_specs=[pl.BlockSpec((ds, B_W), lambda b: (rb(b), base)),
                  pl.BlockSpec((ds, B_W), lambda b: (rb(b), base + 1)),
                  pl.BlockSpec((ds, B_W), lambda b: (rb(b), base + 2)),
                  pl.BlockSpec((1, n_c, B_DH), lambda b: (b, 0, 0)),
                  pl.BlockSpec((1, n_c, B_DH), lambda b: (b, 0, 0)),
                  pl.BlockSpec((1, 1, n_c), lambda b: (b, 0, 0)),
                  pl.BlockSpec((1, 1, B_HEADS * ds), lambda b: (b, 0, 0)),
                  pl.BlockSpec(memory_space=pl.ANY)],
        out_specs=pl.BlockSpec((ds, B_W), lambda b: (rb(b), 0)),
        out_shape=jax.ShapeDtypeStruct(prev_out.shape, F32),
        input_output_aliases={7: 0},
        compiler_params=_params(("parallel",)),
        name="fox_sample",
    )(pb, pb, pb, kc, vc, f_cache, f_new, prev_out)


def _mlstm_kernel(q_ref, k_ref, v_ref, lir_ref, br_ref, lic_ref, bc_ref, c0_ref, n0_ref, m0_ref, *rest,
                  L, aliased):
    if aliased:
        rest = rest[1:]
    h_ref, cout_ref, nout_ref, mout_ref, c_scr, n_scr, m_scr = rest
    c = pl.program_id(1)

    @pl.when(c == 0)
    def _():
        c_scr[...] = c0_ref[0]
        n_scr[...] = n0_ref[0]
        m_scr[...] = m0_ref[0]

    t_idx = lax.broadcasted_iota(jnp.int32, (L, L), 0)
    s_idx = lax.broadcasted_iota(jnp.int32, (L, L), 1)
    causal = s_idx <= t_idx
    for h in range(C_HEADS):
        qf = q_ref[:, h * C_DQK:(h + 1) * C_DQK]
        kf = k_ref[:, h * C_DQK:(h + 1) * C_DQK] * (C_DQK ** -0.5)
        vb = v_ref[:, h * C_DV:(h + 1) * C_DV].astype(BF16)
        qb = qf.astype(BF16)
        b_col = bc_ref[0, 0, :, h:h + 1]
        li_col = lic_ref[0, 0, :, h:h + 1]
        b_row = br_ref[0, 0, h:h + 1, :]
        li_row = lir_ref[0, 0, h:h + 1, :]
        m_prev = m_scr[h:h + 1, :]
        c_prev = c_scr[h]
        n_prev = n_scr[h:h + 1, :]

        d = jnp.where(causal, b_col - b_row + li_row, -jnp.inf)
        inter = b_col + m_prev
        m_t = jnp.maximum(inter, jnp.max(d, axis=1, keepdims=True))
        pm = jnp.exp(d - m_t)
        w_inter = jnp.exp(inter - m_t)
        sqk = _dot(qb, kf.astype(BF16), _NT) * pm
        num = w_inter * _dot(qb, c_prev.astype(BF16), _NT) + _dot(sqk.astype(BF16), vb)
        den = w_inter * jnp.sum(qf * n_prev, axis=1, keepdims=True) + jnp.sum(sqk, axis=1, keepdims=True)
        h_ref[:, h * C_DV:(h + 1) * C_DV] = num / jnp.maximum(jnp.abs(den), jnp.exp(-m_t))

        b_last = b_row[:, L - 1:L]
        g_row = b_last - b_row + li_row
        g_col = b_last - b_col + li_col
        m_new = jnp.maximum(b_last + m_prev, jnp.max(g_row, axis=1, keepdims=True))
        w_c = jnp.exp(b_last + m_prev - m_new)
        kw = kf * jnp.exp(g_col - m_new)
        c_new = w_c * c_prev + _dot(vb, kw.astype(BF16), _TN)
        n_new = w_c * n_prev + jnp.sum(kw, axis=0, keepdims=True)
        c_scr[h] = c_new
        n_scr[h:h + 1, :] = n_new
        m_scr[h:h + 1, :] = m_new

    @pl.when(c == pl.num_programs(1) - 1)
    def _():
        cout_ref[0] = c_scr[...]
        nout_ref[0] = n_scr[...]
        mout_ref[0] = m_scr[...]


def _mlstm(p, li_row, b_row, li_col, b_col, c0, n0, m0, *, nb, nc, L, row_block0, n_total, prev_out=None):
    aliased = prev_out is not None
    rb = lambda b, c: row_block0 + b * nc + c
    in_specs = [pl.BlockSpec((L, C_QK), lambda b, c: (rb(b, c), 0)),
                pl.BlockSpec((L, C_QK), lambda b, c: (rb(b, c), 1)),
                pl.BlockSpec((L, C_V), lambda b, c: (rb(b, c), 2 * C_QK // C_V)),
                pl.BlockSpec((1, 1, C_HEADS, L), lambda b, c: (b, c, 0, 0)),
                pl.BlockSpec((1, 1, C_HEADS, L), lambda b, c: (b, c, 0, 0)),
                pl.BlockSpec((1, 1, L, C_HEADS), lambda b, c: (b, c, 0, 0)),
                pl.BlockSpec((1, 1, L, C_HEADS), lambda b, c: (b, c, 0, 0)),
                pl.BlockSpec((1, C_HEADS, C_DV, C_DQK), lambda b, c: (b, 0, 0, 0)),
                pl.BlockSpec((1, C_HEADS, C_DQK), lambda b, c: (b, 0, 0)),
                pl.BlockSpec((1, C_HEADS, 1), lambda b, c: (b, 0, 0))]
    args = [p, p, p, li_row, b_row, li_col, b_col, c0, n0, m0]
    io_alias = {}
    if aliased:
        in_specs.append(pl.BlockSpec(memory_space=pl.ANY))
        args.append(prev_out)
        io_alias = {len(args) - 1: 0}
    return pl.pallas_call(
        functools.partial(_mlstm_kernel, L=L, aliased=aliased),
        grid=(nb, nc),
        in_specs=in_specs,
        out_specs=[pl.BlockSpec((L, C_V), lambda b, c: (rb(b, c), 0)),
                   pl.BlockSpec((1, C_HEADS, C_DV, C_DQK), lambda b, c: (b, 0, 0, 0)),
                   pl.BlockSpec((1, C_HEADS, C_DQK), lambda b, c: (b, 0, 0)),
                   pl.BlockSpec((1, C_HEADS, 1), lambda b, c: (b, 0, 0))],
        out_shape=[jax.ShapeDtypeStruct((n_total, C_V), F32),
                   jax.ShapeDtypeStruct((nb, C_HEADS, C_DV, C_DQK), F32),
                   jax.ShapeDtypeStruct((nb, C_HEADS, C_DQK), F32),
                   jax.ShapeDtypeStruct((nb, C_HEADS, 1), F32)],
        scratch_shapes=[pltpu.VMEM((C_HEADS, C_DV, C_DQK), F32),
                        pltpu.VMEM((C_HEADS, C_DQK), F32),
                        pltpu.VMEM((C_HEADS, 1), F32)],
        input_output_aliases=io_alias,
        compiler_params=_params(("parallel", "arbitrary")),
        name="mlstm_L%d" % L,
    )(*args)


def _mixout_kernel(*refs, hd, act, has_b, precise, n_prev):
    refs = list(refs)
    h_ref, a_ref, r_ref = refs[:3]
    b_ref = refs[3] if has_b else None
    k = 4 if has_b else 3
    ga_ref, wo_ref, gf_ref, wrh_ref, wrl_ref, br_ref, cnt0_ref = refs[k:k + 7]
    h1_ref, xn_ref, info_ref, cnt_ref, cnt_scr = refs[k + 7 + n_prev:]
    cast = (lambda x: x) if precise else (lambda x: x.astype(BF16))
    a = a_ref[...]
    r = r_ref[...]
    gate = r * _sigmoid(r) if act == "silu" else _sigmoid(r)
    parts = []
    for hh in range(a.shape[1] // hd):
        sl = slice(hh * hd, (hh + 1) * hd)
        parts.append(cast(_rms(a[:, sl], ga_ref[...]) * gate[:, sl]))
    if has_b:
        parts.append(cast(b_ref[...]))
    cat = jnp.concatenate(parts, axis=1)
    h1 = h_ref[...] + (_dot3(cat, wo_ref[...]) if precise else _dot(cat, wo_ref[...]))
    h1_ref[...] = h1
    xn = _rms(h1, gf_ref[...])
    xn_ref[...] = _pack_bf16_pairs(xn)
    xh, xl = _split(xn)
    logits = _dot(xh, wrh_ref[...]) + _dot(xh, wrl_ref[...]) + _dot(xl, wrh_ref[...]) + br_ref[...]

    lane = lax.broadcasted_iota(jnp.int32, logits.shape, 1)
    lanef = lane.astype(F32)
    is_g = lane < N_GROUPS
    gl = jnp.where(is_g, logits, -jnp.inf)
    gmax = jnp.max(gl, axis=1, keepdims=True)
    gidx = jnp.min(jnp.where(gl == gmax, lanef, float(LANES)), axis=1, keepdims=True)
    wg = 1.0 / jnp.sum(jnp.where(is_g, jnp.exp(gl - gmax), 0.0), axis=1, keepdims=True)
    lo = N_GROUPS + N_EXP * gidx
    el = jnp.where((lanef >= lo) & (lanef < lo + N_EXP), logits, -jnp.inf)
    m1 = jnp.max(el, axis=1, keepdims=True)
    i1 = jnp.min(jnp.where(el == m1, lanef, float(LANES)), axis=1, keepdims=True)
    el2 = jnp.where(lanef == i1, -jnp.inf, el)
    m2 = jnp.max(el2, axis=1, keepdims=True)
    i2 = jnp.min(jnp.where(el2 == m2, lanef, float(LANES)), axis=1, keepdims=True)
    t = jnp.exp(m2 - m1)
    w1 = wg / (1.0 + t)
    w2 = wg * t / (1.0 + t)
    e1 = i1 - N_GROUPS
    e2 = i2 - N_GROUPS

    @pl.when(pl.program_id(0) == 0)
    def _():
        cnt_scr[...] = cnt0_ref[...]

    tm = logits.shape[0]
    pick = jnp.where((lanef == e1) | (lanef == e2), 1.0, 0.0)
    earlier = (lax.broadcasted_iota(jnp.int32, (tm, tm), 1) < lax.broadcasted_iota(jnp.int32, (tm, tm), 0))
    before = _dot(jnp.where(earlier, 1.0, 0.0).astype(BF16), pick.astype(BF16)) + cnt_scr[...]
    r1 = jnp.sum(jnp.where(lanef == e1, before, 0.0), axis=1, keepdims=True)
    r2 = jnp.sum(jnp.where(lanef == e2, before, 0.0), axis=1, keepdims=True)
    cnt_new = cnt_scr[...] + jnp.sum(pick, axis=0, keepdims=True)
    cnt_scr[...] = cnt_new
    cnt_ref[...] = cnt_new
    info_ref[...] = jnp.where(lane == 0, e1, jnp.where(lane == 1, e2, jnp.where(lane == 2, w1, jnp.where(
        lane == 3, w2, jnp.where(lane == 4, r1, jnp.where(lane == 5, r2, 0.0))))))


def _mixout(h, a, a_col, r, r_col, b, g_head, w_o, g_ffn, wr_hi, wr_lo, b_r, counts0, *, hd, act,
            row0, n_rows, prev=None):
    n, d = h.shape
    precise = prev is not None
    tm = _tile(n_rows, 256, SUBLANES)
    assert row0 % tm == 0
    blk0 = row0 // tm
    wa = w_o.shape[0] if b is None else w_o.shape[0] - B_W
    has_b = b is not None
    row = lambda i: (blk0 + i, 0)
    const = lambda i: (0, 0)
    in_specs = [pl.BlockSpec((tm, d), row),
                pl.BlockSpec((tm, wa), lambda i: (blk0 + i, a_col)),
                pl.BlockSpec((tm, wa), lambda i: (blk0 + i, r_col))]
    args = [h, a, r]
    if has_b:
        in_specs.append(pl.BlockSpec((tm, B_W), row))
        args.append(b)
    in_specs += [pl.BlockSpec((1, hd), const), pl.BlockSpec(w_o.shape, const), pl.BlockSpec((1, d), const),
                 pl.BlockSpec((d, LANES), const), pl.BlockSpec((d, LANES), const), pl.BlockSpec((1, LANES), const),
                 pl.BlockSpec((1, LANES), const)]
    args += [g_head.reshape(1, hd), w_o if precise else w_o.astype(BF16), g_ffn.reshape(1, d), wr_hi, wr_lo, b_r,
             counts0]
    io_alias = {}
    if precise:
        for k, arr in enumerate(prev):
            in_specs.append(pl.BlockSpec(memory_space=pl.ANY))
            args.append(arr)
            io_alias[len(args) - 1] = k
    return pl.pallas_call(
        functools.partial(_mixout_kernel, hd=hd, act=act, has_b=has_b, precise=precise, n_prev=len(io_alias)),
        grid=(n_rows // tm,),
        in_specs=in_specs,
        out_specs=[pl.BlockSpec((tm, d), row), pl.BlockSpec((tm, d // 2), row), pl.BlockSpec((tm, LANES), row),
                   pl.BlockSpec((1, LANES), const)],
        out_shape=[jax.ShapeDtypeStruct((n, d), F32), jax.ShapeDtypeStruct((n, d // 2), jnp.int32),
                   jax.ShapeDtypeStruct((n, LANES), F32), jax.ShapeDtypeStruct((1, LANES), F32)],
        scratch_shapes=[pltpu.VMEM((1, LANES), F32)],
        input_output_aliases=io_alias,
        compiler_params=_params(("arbitrary",)),
        name="mixout_" + act + ("_precise" if precise else ""),
    )(*args)


def _sc_gather(table, idx):
    r = idx.shape[0]
    w = table.shape[1]
    n_workers = SC_CORES * SC_SUBCORES
    per_worker = r // n_workers
    step = SC_GATHER_ROWS * SC_GATHER_BUFS
    assert r % n_workers == 0 and per_worker % step == 0, (r, n_workers, step)
    mesh = plsc.VectorSubcoreMesh(core_axis_name="c", subcore_axis_name="s")

    @functools.partial(
        pl.kernel, mesh=mesh,
        out_type=jax.ShapeDtypeStruct((r, w), table.dtype),
        scratch_types=[pltpu.VMEM((SC_GATHER_BUFS, SC_GATHER_ROWS), jnp.int32),
                       pltpu.VMEM((SC_GATHER_BUFS, SC_GATHER_ROWS, w), table.dtype),
                       pltpu.SemaphoreType.DMA((SC_GATHER_BUFS,)),
                       pltpu.SemaphoreType.DMA((SC_GATHER_BUFS,))],
    )
    def gather(table_hbm, idx_hbm, out_hbm, idx_v, rows_v, gather_sem, store_sem):
        worker = lax.axis_index("s") * SC_CORES + lax.axis_index("c")
        base = worker * per_worker

        @pl.loop(0, per_worker // step)
        def _(j):
            off = pl.multiple_of(base + j * step, step)
            rows = lambda b: pl.ds(off + b * SC_GATHER_ROWS, SC_GATHER_ROWS)
            gathers, stores = [], []
            for b in range(SC_GATHER_BUFS):
                pltpu.sync_copy(idx_hbm.at[rows(b)], idx_v.at[b])
                gathers.append(pltpu.async_copy(table_hbm.at[idx_v.at[b]], rows_v.at[b], gather_sem.at[b]))
            for b in range(SC_GATHER_BUFS):
                gathers[b].wait()
                stores.append(pltpu.async_copy(rows_v.at[b], out_hbm.at[rows(b)], store_sem.at[b]))
            for b in range(SC_GATHER_BUFS):
                stores[b].wait()

    return gather(table, idx)


def _expert_kernel(te_ref, nu_ref, x_ref, wg_ref, wu_ref, wd_ref, y_ref, wgb, wub, wdb):
    i = pl.program_id(0)
    live = i < nu_ref[0]
    new_expert = (i == 0) | (te_ref[i] != te_ref[jnp.maximum(i - 1, 0)])

    @pl.when(live & new_expert)
    def _():
        wgb[...] = wg_ref[0].astype(BF16)
        wub[...] = wu_ref[0].astype(BF16)
        wdb[...] = wd_ref[0].astype(BF16)

    @pl.when(live)
    def _():
        x = _unpack_bf16_pairs(x_ref[...]).astype(BF16)
        g = _dot(x, wgb[...])
        u = _dot(x, wub[...])
        y_ref[...] = _pack_bf16_pairs(_dot((g * _sigmoid(g) * u).astype(BF16), wdb[...]))

    @pl.when(jnp.logical_not(live))
    def _():
        y_ref[...] = jnp.zeros_like(y_ref)


def _experts(xs, tile_expert, n_used, w_gate, w_up, w_down, *, tm):
    n_slots, dp = xs.shape
    d, f = w_gate.shape[-2:]
    grid_spec = pltpu.PrefetchScalarGridSpec(
        num_scalar_prefetch=2,
        grid=(n_slots // tm,),
        in_specs=[pl.BlockSpec((tm, dp), lambda i, te, nu: (i, 0)),
                  pl.BlockSpec((1, d, f), lambda i, te, nu: (te[i], 0, 0)),
                  pl.BlockSpec((1, d, f), lambda i, te, nu: (te[i], 0, 0)),
                  pl.BlockSpec((1, f, d), lambda i, te, nu: (te[i], 0, 0))],
        out_specs=pl.BlockSpec((tm, dp), lambda i, te, nu: (i, 0)),
        scratch_shapes=[pltpu.VMEM((d, f), BF16), pltpu.VMEM((d, f), BF16), pltpu.VMEM((f, d), BF16)],
    )
    return pl.pallas_call(
        _expert_kernel,
        grid_spec=grid_spec,
        out_shape=jax.ShapeDtypeStruct((n_slots, dp), jnp.int32),
        compiler_params=_params(("arbitrary",)),
        name="moe_experts",
    )(tile_expert, n_used, xs, w_gate, w_up, w_down)


def _combine_kernel(h_ref, info_ref, g_ref, y0_ref, y1_ref, o_ref, *, final_norm):
    h2 = h_ref[...] + (info_ref[:, 2:3] * _unpack_bf16_pairs(y0_ref[...])
                       + info_ref[:, 3:4] * _unpack_bf16_pairs(y1_ref[...]))
    o_ref[...] = _rms(h2, g_ref[...]) if final_norm else h2


def _combine(h, yg, info, g, *, final_norm):
    n, d = h.shape
    tm = _tile(n, 512, SUBLANES)
    nt = n // tm
    return pl.pallas_call(
        functools.partial(_combine_kernel, final_norm=final_norm),
        grid=(nt,),
        in_specs=[pl.BlockSpec((tm, d), lambda i: (i, 0)),
                  pl.BlockSpec((tm, LANES), lambda i: (i, 0)),
                  pl.BlockSpec((1, d), lambda i: (0, 0)),
                  pl.BlockSpec((tm, d // 2), lambda i: (i, 0)),
                  pl.BlockSpec((tm, d // 2), lambda i: (nt + i, 0))],
        out_specs=pl.BlockSpec((tm, d), lambda i: (i, 0)),
        out_shape=jax.ShapeDtypeStruct((n, d), F32),
        compiler_params=_params(("parallel",)),
        name="moe_combine",
    )(h, info, g.reshape(1, d), yg, yg)


def _moe(h1, xn, info, counts_f, w_gate, w_up, w_down, expert_base, g_next, *, final_norm):
    n, d = h1.shape
    n_e = N_GROUPS * N_EXP
    tm = _tile(2 * n, 256, SUBLANES)
    n_tiles = (2 * n) // tm + n_e
    n_slots = n_tiles * tm
    experts = jnp.arange(n_e, dtype=jnp.int32)
    counts = counts_f[0, :n_e].astype(jnp.int32)
    starts = jnp.cumsum(counts) - counts
    padded = ((counts + tm - 1) // tm) * tm
    pad_ends = jnp.cumsum(padded)
    pad_starts = pad_ends - padded
    n_used = (pad_ends[-1] // tm).astype(jnp.int32)
    eid = info[:, 0:2].astype(jnp.int32)
    rank = info[:, 4:6].astype(jnp.int32)
    lookup = lambda table, e: jnp.sum(jnp.where(e[..., None] == experts, table, 0), axis=-1)
    slot_of_pick = lookup(pad_starts, eid) + rank
    tok = jnp.broadcast_to(jnp.arange(n, dtype=jnp.int32)[:, None], (n, 2)).reshape(-1)
    _, tok_sorted = lax.sort_key_val(slot_of_pick.reshape(-1), tok)
    slots = jnp.arange(n_slots, dtype=jnp.int32)
    e_of_slot = jnp.minimum(jnp.sum(slots[:, None] >= pad_ends[None, :], axis=1), n_e - 1).astype(jnp.int32)
    r_of_slot = slots - lookup(pad_starts, e_of_slot)
    live = r_of_slot < lookup(counts, e_of_slot)
    src_token = jnp.where(live, tok_sorted[jnp.clip(lookup(starts, e_of_slot) + r_of_slot, 0, 2 * n - 1)], 0)
    tile_expert = (expert_base + e_of_slot[::tm]).astype(jnp.int32)

    xs = _sc_gather(xn, src_token.astype(jnp.int32))
    ys = _experts(xs, tile_expert, n_used.reshape(1), w_gate, w_up, w_down, tm=tm)
    yg = _sc_gather(ys, slot_of_pick.T.reshape(-1))
    return _combine(h1, yg, info, g_next, final_norm=final_norm)


def _router_weights(w_rg, b_rg, w_re, b_re):
    d = w_rg.shape[0]
    pad = LANES - N_GROUPS - N_GROUPS * N_EXP
    w = jnp.concatenate([w_rg, w_re, jnp.zeros((d, pad), F32)], axis=1)
    b = jnp.concatenate([b_rg, b_re, jnp.zeros((pad,), F32)]).reshape(1, LANES)
    hi = w.astype(BF16)
    lo = (w - hi.astype(F32)).astype(BF16)
    return hi, lo, b


def _rows_to_lanes(x, nb, t, nch):
    return x.reshape(nb, t, nch).transpose(0, 2, 1).reshape(nb * nch, t)


def _even_layer(h, dims, g_mix, w_in, w_g2, b_g, b_f, g_a, w_o, state_gla, ck, cv, clf):
    nb, t, db, ds, npr, n = dims
    d = h.shape[1]
    qa, ka, va, ra, ga, qb, kb, vb, fb = jnp.split(
        w_in, [A_QK, 2 * A_QK, 2 * A_QK + A_V, 2 * A_QK + 2 * A_V, 2 * A_QK + 2 * A_V + A_RANK,
               2 * A_QK + 2 * A_V + A_RANK + B_W, 2 * A_QK + 2 * A_V + A_RANK + 2 * B_W,
               2 * A_QK + 2 * A_V + A_RANK + 3 * B_W], axis=1)
    w_packed = jnp.concatenate(
        [qa, ka, va, ra, qb, kb, vb, ga, fb, jnp.zeros((d, LANES - A_RANK - B_HEADS), F32)], axis=1)
    q_off = 2 * A_QK + 2 * A_V
    colscale = jnp.ones((1, MAIN_W), F32).at[:, q_off:q_off + B_W].set(B_DH ** -0.5 * LOG2E)
    p, pb = _proj(h, g_mix, w_packed, colscale, n_rows=npr)
    p = _proj_precise(h, g_mix, w_packed, p, row0=npr)

    nh = B_HEADS
    fcol = MAIN_W + A_RANK
    zeros_col = lambda r: jnp.zeros((r, 1), F32)
    bias_row = lambda lanes: jnp.tile(b_f, lanes // nh).reshape(1, lanes)
    fb_p = p[:npr, fcol:fcol + nh].reshape(nb, t * nh)
    logf_p, f_p = _gate_scan(fb_p, bias_row(t * nh), zeros_col(nb), mode="fox", act_start=0,
                             valid_start=N_PAD * nh, valid_end=t * nh, seg=None, stride=nh)
    past = ck.shape[1]
    n_c = past * nh
    x_s = _pad_lanes(jnp.concatenate([clf.reshape(db, n_c), p[npr:, fcol:fcol + nh].reshape(db, ds * nh)], axis=1))
    logf_s, f_s = _gate_scan(x_s, bias_row(x_s.shape[1]), zeros_col(db), mode="fox", act_start=n_c,
                             valid_start=0, valid_end=n_c + ds * nh, seg=None, stride=nh)

    nc = t // CHUNK
    oa, s_p = _gla(p, jnp.zeros((nb, A_HEADS, A_DK, A_DV), F32), w_g2, b_g,
                   nb=nb, nc=nc, L=CHUNK, row_block0=0, lead_pad=N_PAD, n_total=n)
    oa, s_s = _gla(p, state_gla, w_g2, b_g, nb=db, nc=1, L=ds, row_block0=npr // ds, lead_pad=0,
                   n_total=n, prev_out=oa, precise=True)

    fk = (f_p * LOG2E).reshape(nb, t, nh).transpose(0, 2, 1).reshape(nb, nh, 1, t)
    ob = _flash(pb, fk, nb=nb, t=t, n_total=n)
    f_cache = (f_s[:, :n_c] * LOG2E).reshape(db, 1, n_c)
    f_new = (f_s[:, n_c:n_c + ds * nh] * LOG2E).reshape(db, ds, nh).transpose(0, 2, 1).reshape(db, 1, nh * ds)
    ob = _fox_sample(p, ck.reshape(db, n_c, B_DH), cv.reshape(db, n_c, B_DH), f_cache, f_new, ob,
                     nb=db, ds=ds, row_block0=npr // ds)

    kcol = q_off + B_W
    states = dict(
        s_p=s_p, s_s=s_s,
        k_p=p[:npr, kcol:kcol + B_W].reshape(nb, t, B_HEADS, B_DH)[:, N_PAD:],
        v_p=p[:npr, kcol + B_W:kcol + 2 * B_W].reshape(nb, t, B_HEADS, B_DH)[:, N_PAD:],
        f_p=logf_p.reshape(nb, t, nh)[:, N_PAD:],
        k_s=p[npr:, kcol:kcol + B_W].reshape(db, ds, B_HEADS, B_DH),
        v_s=p[npr:, kcol + B_W:kcol + 2 * B_W].reshape(db, ds, B_HEADS, B_DH),
        f_s=logf_s[:, n_c:n_c + ds * nh].reshape(db, ds, nh))
    return (oa, 0, p, (2 * A_QK + A_V) // A_V, ob, g_a, w_o), states


def _chunk_rows(x, nb, nch, nc, L):
    x = x[:, :nc * L].reshape(nb, nch, nc, L)
    return x.transpose(0, 2, 1, 3), x.transpose(0, 2, 3, 1)


def _odd_layer(h, dims, g_mix, w_in, b_gate, g_c, w_o, c0, n0, m0):
    nb, t, db, ds, npr, n = dims
    d = h.shape[1]
    w_packed = jnp.concatenate(
        [w_in, jnp.zeros((d, LANES - 2 * C_HEADS), F32)], axis=1)
    p, _ = _proj(h, g_mix, w_packed, jnp.ones((1, MAIN_W), F32), n_rows=npr)
    p = _proj_precise(h, g_mix, w_packed, p, row0=npr)

    ng = 2 * C_HEADS
    isf = (jnp.arange(ng) >= C_HEADS).astype(F32)
    nc = t // CHUNK

    def gates(rows, nbatch, tt, valid_start, seg):
        x = _pad_lanes(_rows_to_lanes(rows, nbatch, tt, ng))
        val, cum = _gate_scan(x, jnp.tile(b_gate, nbatch).reshape(-1, 1), jnp.tile(isf, nbatch).reshape(-1, 1),
                              mode="mlstm", act_start=0, valid_start=valid_start, valid_end=tt, seg=seg)
        return val.reshape(nbatch, ng, -1), cum.reshape(nbatch, ng, -1)

    val_p, cum_p = gates(p[:npr, MAIN_W:MAIN_W + ng], nb, t, N_PAD, CHUNK)
    val_s, cum_s = gates(p[npr:, MAIN_W:MAIN_W + ng], db, ds, 0, ds)

    def chunked(val, cum, nbatch, ncs, L):
        li_row, li_col = _chunk_rows(val[:, :C_HEADS].reshape(nbatch * C_HEADS, -1), nbatch, C_HEADS, ncs, L)
        b_row, b_col = _chunk_rows(cum[:, C_HEADS:].reshape(nbatch * C_HEADS, -1), nbatch, C_HEADS, ncs, L)
        return li_row, b_row, li_col, b_col

    zc = jnp.zeros((nb, C_HEADS, C_DV, C_DQK), F32)
    zn = jnp.zeros((nb, C_HEADS, C_DQK), F32)
    zm = jnp.zeros((nb, C_HEADS, 1), F32)
    hm, c_p, n_p, m_p = _mlstm(p, *chunked(val_p, cum_p, nb, nc, CHUNK), zc, zn, zm,
                               nb=nb, nc=nc, L=CHUNK, row_block0=0, n_total=n)
    hm, c_s, n_s, m_s = _mlstm(p, *chunked(val_s, cum_s, db, 1, ds), c0, n0, m0.reshape(db, C_HEADS, 1),
                               nb=db, nc=1, L=ds, row_block0=npr // ds, n_total=n, prev_out=hm)
    states = dict(c_p=c_p, n_p=n_p, m_p=m_p.reshape(nb, C_HEADS), c_s=c_s, n_s=n_s, m_s=m_s.reshape(db, C_HEADS))
    return (hm, 0, p, (2 * C_QK + C_V) // C_V, None, g_c, w_o), states


def kernel(x_prompt, x_sample, state_gla, cache_fox_k, cache_fox_v, cache_fox_logf, state_mlstm_c, state_mlstm_n, state_mlstm_m, meta_tokens, norm_mix, norm_ffn, norm_final, w_in_even, w_gla_gate2, b_gla_gate, b_fox_f, g_gla_out, w_out_even, w_in_odd, b_mlstm_gate, g_mlstm_out, w_out_odd, w_router_group, b_router_group, w_router_expert, b_router_expert, w_exp_gate, w_exp_up, w_exp_down):
    nb, seq, d = x_prompt.shape
    db, ds, _ = x_sample.shape
    t = LEAD + seq
    npr, nsm = nb * t, db * ds
    n = npr + nsm
    dims = (nb, t, db, ds, npr, n)
    depth = norm_mix.shape[0]
    n_e = N_GROUPS * N_EXP
    f = w_exp_gate.shape[-1]

    hp = jnp.concatenate([jnp.zeros((nb, N_PAD, d), F32),
                          jnp.broadcast_to(meta_tokens[None], (nb, N_META, d)), x_prompt], axis=1)
    h = jnp.concatenate([hp.reshape(npr, d), x_sample.reshape(nsm, d)], axis=0)
    wg_all = w_exp_gate.reshape(depth * n_e, d, f)
    wu_all = w_exp_up.reshape(depth * n_e, d, f)
    wd_all = w_exp_down.reshape(depth * n_e, f, d)

    even, odd = [], []
    for l in range(depth):
        if l % 2 == 0:
            e = l // 2
            mix, st = _even_layer(h, dims, norm_mix[l], w_in_even[e], w_gla_gate2[e], b_gla_gate[e], b_fox_f[e],
                                  g_gla_out[e], w_out_even[e], state_gla[e], cache_fox_k[e], cache_fox_v[e],
                                  cache_fox_logf[e])
            even.append(st)
            hd, act = A_DV, "silu"
        else:
            o = l // 2
            mix, st = _odd_layer(h, dims, norm_mix[l], w_in_odd[o], b_mlstm_gate[o], g_mlstm_out[o], w_out_odd[o],
                                 state_mlstm_c[o], state_mlstm_n[o], state_mlstm_m[o])
            odd.append(st)
            hd, act = C_DV, "sigmoid"
        a, a_col, r, r_col, b, g_head, w_o = mix
        wr_hi, wr_lo, b_r = _router_weights(w_router_group[l], b_router_group[l], w_router_expert[l],
                                            b_router_expert[l])
        mix_args = (h, a, a_col, r, r_col, b, g_head, w_o, norm_ffn[l], wr_hi, wr_lo, b_r)
        h1, xn, info, counts = _mixout(*mix_args, jnp.zeros((1, LANES), F32), hd=hd, act=act, row0=0, n_rows=npr)
        h1, xn, info, counts = _mixout(*mix_args, counts, hd=hd, act=act, row0=npr, n_rows=nsm,
                                       prev=(h1, xn, info))
        last = l == depth - 1
        h = _moe(h1, xn, info, counts, wg_all, wu_all, wd_all, l * n_e, norm_final if last else norm_ffn[l],
                 final_norm=last)

    y_prompt = h[:npr].reshape(nb, t, d)[:, LEAD:]
    y_sample = h[npr:].reshape(db, ds, d)
    stack = lambda sts, key: jnp.stack([s[key] for s in sts])
    return (y_prompt, y_sample,
            stack(even, "s_p"), stack(even, "k_p"), stack(even, "v_p"), stack(even, "f_p"),
            stack(odd, "c_p"), stack(odd, "n_p"), stack(odd, "m_p"),
            stack(even, "s_s"), stack(even, "k_s"), stack(even, "v_s"), stack(even, "f_s"),
            stack(odd, "c_s"), stack(odd, "n_s"), stack(odd, "m_s"))
```

```python
import functools

import jax
import jax.numpy as jnp
from jax import lax
from jax.experimental import pallas as pl
from jax.experimental.pallas import tpu as pltpu
from jax.experimental.pallas import tpu_sc as plsc

F32 = jnp.float32
BF16 = jnp.bfloat16

CHUNK = 64
N_META = 16
LEAD = 128
N_PAD = LEAD - N_META
A_HEADS, A_DK, A_DV, A_RANK = 4, 64, 128, 16
A_GATE_NORM = 16.0
B_HEADS, B_DH = 4, 128
C_HEADS, C_DQK, C_DV = 4, 128, 256
GATE_CAP = 15.0
N_GROUPS, N_EXP = 4, 8
EPS = 1e-6
NEG = -1e30
LOG2E = 1.4426950408889634
A_QK = A_HEADS * A_DK
A_V = A_HEADS * A_DV
B_W = B_HEADS * B_DH
C_QK = C_HEADS * C_DQK
C_V = C_HEADS * C_DV

LANES = 128
SUBLANES = 8
VMEM_LIMIT_BYTES = 56 * 1024 * 1024
GLA_SUB = 16
SC_CORES, SC_SUBCORES = 2, 16
SC_GATHER_ROWS = 16
SC_GATHER_BUFS = 4
FLASH_HEADS = 2
MAIN_W = 3072
PROJ_W = MAIN_W + LANES

_NT = (((1,), (1,)), ((), ()))
_TN = (((0,), (0,)), ((), ()))
_NN = (((1,), (0,)), ((), ()))


def _params(sem):
    return pltpu.CompilerParams(dimension_semantics=sem, vmem_limit_bytes=VMEM_LIMIT_BYTES)


def _tile(n, pref, mult):
    t = (min(pref, n) // mult) * mult
    while t > mult and n % t:
        t -= mult
    assert t >= mult and n % t == 0, (n, pref, mult)
    return t


def _dot(a, b, dims=_NN):
    return lax.dot_general(a, b, dims, preferred_element_type=F32)


def _split(x):
    hi = x.astype(BF16)
    lo = (x - hi.astype(F32)).astype(BF16)
    return hi, lo


def _dot3(a, b, dims=_NN):
    ah, al = _split(a)
    bh, bl = _split(b)
    return _dot(ah, bh, dims) + _dot(ah, bl, dims) + _dot(al, bh, dims)


def _log_sigmoid(x):
    return jnp.minimum(x, 0.0) - jnp.log1p(jnp.exp(-jnp.abs(x)))


def _sigmoid(x):
    return 1.0 / (1.0 + jnp.exp(-x))


def _rms(x, g):
    return x * lax.rsqrt(jnp.mean(x * x, axis=-1, keepdims=True) + EPS) * g


def _pack_bf16_pairs(x):
    w = x.shape[1] // 2
    hi = lax.bitcast_convert_type(x[:, :w].astype(BF16).astype(F32), jnp.int32)
    lo = lax.bitcast_convert_type(x[:, w:].astype(BF16).astype(F32), jnp.int32)
    return hi | lax.shift_right_logical(lo, 16)


def _unpack_bf16_pairs(p):
    hi = lax.bitcast_convert_type(p & jnp.int32(-65536), F32)
    lo = lax.bitcast_convert_type(lax.shift_left(p, 16), F32)
    return jnp.concatenate([hi, lo], axis=1)


def _cumsum_rows(x):
    n = x.shape[0]
    row = lax.broadcasted_iota(jnp.int32, x.shape, 0)
    s = 1
    while s < n:
        x = x + jnp.where(row >= s, pltpu.roll(x, s, axis=0), 0.0)
        s *= 2
    return x


def _proj_kernel(x_ref, g_ref, w_ref, cs_ref, o_ref, ob_ref, *, col_chunk):
    xn = _rms(x_ref[...], g_ref[...]).astype(BF16)
    for c0 in range(0, PROJ_W, col_chunk):
        c1 = min(c0 + col_chunk, PROJ_W)
        y = _dot(xn, w_ref[:, c0:c1])
        o_ref[:, c0:c1] = y
        if c0 < MAIN_W:
            m1 = min(c1, MAIN_W)
            ob_ref[:, c0:m1] = (y[:, :m1 - c0] * cs_ref[:, c0:m1]).astype(BF16)


def _proj_precise_kernel(x_ref, g_ref, w_ref, prev_ref, o_ref, *, col_chunk):
    del prev_ref
    xn = _rms(x_ref[...], g_ref[...])
    for c0 in range(0, PROJ_W, col_chunk):
        c1 = min(c0 + col_chunk, PROJ_W)
        o_ref[:, c0:c1] = _dot3(xn, w_ref[:, c0:c1])


def _proj(h, g, w_packed, colscale, *, n_rows):
    n, d = h.shape
    tm = _tile(n_rows, 512, 16)
    return pl.pallas_call(
        functools.partial(_proj_kernel, col_chunk=640),
        grid=(n_rows // tm,),
        in_specs=[pl.BlockSpec((tm, d), lambda i: (i, 0)),
                  pl.BlockSpec((1, d), lambda i: (0, 0)),
                  pl.BlockSpec((d, PROJ_W), lambda i: (0, 0)),
                  pl.BlockSpec((1, MAIN_W), lambda i: (0, 0))],
        out_specs=[pl.BlockSpec((tm, PROJ_W), lambda i: (i, 0)),
                   pl.BlockSpec((tm, MAIN_W), lambda i: (i, 0))],
        out_shape=[jax.ShapeDtypeStruct((n, PROJ_W), F32),
                   jax.ShapeDtypeStruct((n_rows, MAIN_W), BF16)],
        compiler_params=_params(("parallel",)),
        name="proj",
    )(h, g.reshape(1, d), w_packed.astype(BF16), colscale)


def _proj_precise(h, g, w_packed, prev, *, row0):
    n, d = h.shape
    tm = _tile(n - row0, 512, SUBLANES)
    assert row0 % tm == 0
    return pl.pallas_call(
        functools.partial(_proj_precise_kernel, col_chunk=640),
        grid=((n - row0) // tm,),
        in_specs=[pl.BlockSpec((tm, d), lambda i: (row0 // tm + i, 0)),
                  pl.BlockSpec((1, d), lambda i: (0, 0)),
                  pl.BlockSpec((d, PROJ_W), lambda i: (0, 0)),
                  pl.BlockSpec(memory_space=pl.ANY)],
        out_specs=pl.BlockSpec((tm, PROJ_W), lambda i: (row0 // tm + i, 0)),
        out_shape=jax.ShapeDtypeStruct((n, PROJ_W), F32),
        input_output_aliases={3: 0},
        compiler_params=_params(("parallel",)),
        name="proj_precise",
    )(h, g.reshape(1, d), w_packed, prev)


def _gate_scan_kernel(x_ref, bias_ref, isf_ref, val_ref, cum_ref, *, mode, act_start, valid_start, valid_end, seg, stride):
    x = x_ref[...]
    lane = lax.broadcasted_iota(jnp.int32, x.shape, 1)
    valid = (lane >= valid_start) & (lane < valid_end)
    if mode == "fox":
        val = jnp.where(lane >= act_start, _log_sigmoid(x + bias_ref[...]), x)
        val = jnp.where(valid, val, 0.0)
        add = val
    else:
        gate = GATE_CAP * jnp.tanh((x + bias_ref[...]) / GATE_CAP)
        isf = isf_ref[...] > 0.5
        val = jnp.where(isf, jnp.where(valid, _log_sigmoid(gate), 0.0),
                        jnp.where(valid, gate, -jnp.inf))
        add = jnp.where(isf, val, 0.0)
    val_ref[...] = val
    n = x.shape[1]
    pos = lane if seg is None else lane % seg
    limit = n if seg is None else seg
    s = stride
    while s < limit:
        add = add + jnp.where(pos >= s, pltpu.roll(add, s, axis=1), 0.0)
        s *= 2
    cum_ref[...] = add


def _gate_scan(x, bias, isf, *, mode, act_start, valid_start, valid_end, seg, stride=1):
    r, n = x.shape
    full = lambda shape: pl.BlockSpec(shape, lambda i: (0,) * len(shape))
    return pl.pallas_call(
        functools.partial(_gate_scan_kernel, mode=mode, act_start=act_start,
                          valid_start=valid_start, valid_end=valid_end, seg=seg, stride=stride),
        grid=(1,),
        in_specs=[full((r, n)), full(bias.shape), full((r, 1))],
        out_specs=[full((r, n)), full((r, n))],
        out_shape=[jax.ShapeDtypeStruct((r, n), F32)] * 2,
        compiler_params=_params(("arbitrary",)),
        name="gate_scan_" + mode,
    )(x, bias, isf)


def _pad_lanes(x):
    n = x.shape[-1]
    m = -(-n // LANES) * LANES
    return x if m == n else jnp.pad(x, ((0, 0), (0, m - n)))


def _gla_kernel(qk_ref, v_ref, sm_ref, s0_ref, wg2_ref, wg2t_ref, bgr_ref, bgc_ref, *rest,
                L, sub, lead_pad, aliased, precise):
    if aliased:
        rest = rest[1:]
    o_ref, sout_ref, s_scr = rest
    c = pl.program_id(1)
    nh, dk, dv = A_HEADS, A_DK, A_DV
    cast = (lambda x: x) if precise else (lambda x: x.astype(BF16))
    mm = _dot3 if precise else _dot

    @pl.when(c == 0)
    def _():
        s_scr[...] = jnp.zeros_like(s_scr)
        for h in range(nh):
            s_scr[h * dk:(h + 1) * dk, h * dv:(h + 1) * dv] = s0_ref[0, h]

    qk = qk_ref[...]
    q = qk[:, :A_QK] * (A_DK ** -0.5)
    k = qk[:, A_QK:]
    v = v_ref[...]
    ga = sm_ref[:, :A_RANK]
    row = lax.broadcasted_iota(jnp.int32, (L, 1), 0)
    valid = (c * L + row) >= lead_pad
    z = _dot3(ga, wg2_ref[...]) + bgr_ref[...]
    loga = jnp.where(valid, _log_sigmoid(z) / A_GATE_NORM, 0.0)
    k = jnp.where(valid, k, 0.0)
    b = _cumsum_rows(loga)
    b_last = b[L - 1:L, :]
    lane_t = lax.broadcasted_iota(jnp.int32, (1, L), 1)
    zt = _dot3(wg2t_ref[...], ga, _NT) + bgc_ref[...]
    logat = jnp.where((c * L + lane_t) >= lead_pad, _log_sigmoid(zt) / A_GATE_NORM, 0.0)
    b_last_col = jnp.sum(logat, axis=1, keepdims=True)

    qhead = lax.broadcasted_iota(jnp.int32, (1, A_QK), 1) // dk
    vhead = lax.broadcasted_iota(jnp.int32, (1, A_V), 1) // dv
    vb = cast(v)
    zero_b = jnp.zeros((), vb.dtype)
    v_bd = jnp.concatenate([jnp.where(vhead == h, vb, zero_b) for h in range(nh)], axis=0)

    rows_all = lax.broadcasted_iota(jnp.int32, (L, 1), 0)
    a_rows = []
    for i in range(L // sub):
        r0 = i * sub
        ci = jnp.zeros((1, A_QK), F32) if i == 0 else b[r0 - 1:r0, :]
        qt = cast(q[r0:r0 + sub] * jnp.exp(b[r0:r0 + sub] - ci))
        kt = cast(jnp.where(rows_all < r0 + sub, k * jnp.exp(ci - b), 0.0))
        k_stack = jnp.concatenate([jnp.where(qhead == h, kt, zero_b) for h in range(nh)], axis=0)
        a_rows.append(mm(qt, k_stack, _NT))
    a = a_rows[0] if len(a_rows) == 1 else jnp.concatenate(a_rows, axis=0)
    t_idx = lax.broadcasted_iota(jnp.int32, (L, nh * L), 0)
    s_idx = lax.broadcasted_iota(jnp.int32, (L, nh * L), 1) % L
    a = jnp.where(s_idx <= t_idx, a, 0.0)
    o_intra = mm(cast(a), v_bd)

    s_full = s_scr[...]
    o_inter = mm(cast(q * jnp.exp(b)), cast(s_full))
    o_ref[...] = o_inter + o_intra

    k_hat = cast(k * jnp.exp(b_last - b))
    upd = mm(k_hat, vb, _TN)
    khead_col = lax.broadcasted_iota(jnp.int32, (A_QK, 1), 0) // dk
    s_new = jnp.exp(b_last_col) * s_full + jnp.where(khead_col == vhead, upd, 0.0)
    s_scr[...] = s_new

    @pl.when(c == pl.num_programs(1) - 1)
    def _():
        for h in range(nh):
            sout_ref[0, h] = s_new[h * dk:(h + 1) * dk, h * dv:(h + 1) * dv]


def _gla(p, s0, wg2, bg, *, nb, nc, L, row_block0, lead_pad, n_total, prev_out=None, precise=False):
    aliased = prev_out is not None
    rb = lambda b, c: row_block0 + b * nc + c
    in_specs = [pl.BlockSpec((L, 2 * A_QK), lambda b, c: (rb(b, c), 0)),
                pl.BlockSpec((L, A_V), lambda b, c: (rb(b, c), 2 * A_QK // A_V)),
                pl.BlockSpec((L, LANES), lambda b, c: (rb(b, c), MAIN_W // LANES)),
                pl.BlockSpec((1, A_HEADS, A_DK, A_DV), lambda b, c: (b, 0, 0, 0)),
                pl.BlockSpec((A_RANK, A_QK), lambda b, c: (0, 0)),
                pl.BlockSpec((A_QK, A_RANK), lambda b, c: (0, 0)),
                pl.BlockSpec((1, A_QK), lambda b, c: (0, 0)),
                pl.BlockSpec((A_QK, 1), lambda b, c: (0, 0))]
    args = [p, p, p, s0, wg2, wg2.T, bg.reshape(1, A_QK), bg.reshape(A_QK, 1)]
    io_alias = {}
    if aliased:
        in_specs.append(pl.BlockSpec(memory_space=pl.ANY))
        args.append(prev_out)
        io_alias = {len(args) - 1: 0}
    return pl.pallas_call(
        functools.partial(_gla_kernel, L=L, sub=min(GLA_SUB, L), lead_pad=lead_pad, aliased=aliased,
                          precise=precise),
        grid=(nb, nc),
        in_specs=in_specs,
        out_specs=[pl.BlockSpec((L, A_V), lambda b, c: (rb(b, c), 0)),
                   pl.BlockSpec((1, A_HEADS, A_DK, A_DV), lambda b, c: (b, 0, 0, 0))],
        out_shape=[jax.ShapeDtypeStruct((n_total, A_V), F32),
                   jax.ShapeDtypeStruct((nb, A_HEADS, A_DK, A_DV), F32)],
        scratch_shapes=[pltpu.VMEM((A_QK, A_V), F32)],
        input_output_aliases=io_alias,
        compiler_params=_params(("parallel", "arbitrary")),
        name="gla_L%d" % L,
    )(*args)


def _flash_kernel(qi_ref, kj_ref, q_ref, k_ref, v_ref, fk_ref, o_ref, m_scr, l_scr, acc_scr, *, blk, lead_pad):
    step = pl.program_id(2)
    i = qi_ref[step]
    j = kj_ref[step]

    @pl.when(j == 0)
    def _():
        m_scr[...] = jnp.full_like(m_scr, -jnp.inf)
        l_scr[...] = jnp.zeros_like(l_scr)
        acc_scr[...] = jnp.zeros_like(acc_scr)

    def update(masked):
        for g in range(FLASH_HEADS):
            sl = slice(g * B_DH, (g + 1) * B_DH)
            s = _dot(q_ref[:, sl], k_ref[:, sl], _NT) - fk_ref[0, g]
            if masked:
                qpos = i * blk + lax.broadcasted_iota(jnp.int32, (blk, blk), 0)
                kpos = j * blk + lax.broadcasted_iota(jnp.int32, (blk, blk), 1)
                s = jnp.where((kpos <= qpos) & (kpos >= lead_pad), s, NEG)
            m_prev = m_scr[g]
            m_new = jnp.maximum(m_prev, jnp.max(s, axis=1, keepdims=True))
            alpha = jnp.exp2(m_prev - m_new)
            p = jnp.exp2(s - m_new)
            l_scr[g] = alpha * l_scr[g] + jnp.sum(p, axis=1, keepdims=True)
            acc_scr[g] = alpha * acc_scr[g] + _dot(p.astype(BF16), v_ref[:, sl])
            m_scr[g] = m_new

    edge = (j == i) | (j == 0)
    pl.when(edge)(functools.partial(update, True))
    pl.when(jnp.logical_not(edge))(functools.partial(update, False))

    @pl.when(j == i)
    def _():
        for g in range(FLASH_HEADS):
            o_ref[:, g * B_DH:(g + 1) * B_DH] = acc_scr[g] / l_scr[g]


def _flash(pb, fk, *, nb, t, n_total):
    blk = _tile(t, 640, LANES)
    nq = t // blk
    hg = FLASH_HEADS
    w = hg * B_DH
    pairs = [(i, j) for i in range(nq) for j in range(i + 1)]
    qi = jnp.asarray([p[0] for p in pairs], jnp.int32)
    kj = jnp.asarray([p[1] for p in pairs], jnp.int32)
    qc, kc, vc = (A_QK * 2 + A_V * 2) // w, (A_QK * 2 + A_V * 2 + B_W) // w, (A_QK * 2 + A_V * 2 + 2 * B_W) // w
    grid_spec = pltpu.PrefetchScalarGridSpec(
        num_scalar_prefetch=2,
        grid=(nb, B_HEADS // hg, len(pairs)),
        in_specs=[pl.BlockSpec((blk, w), lambda b, h, s, qi, kj: (b * nq + qi[s], qc + h)),
                  pl.BlockSpec((blk, w), lambda b, h, s, qi, kj: (b * nq + kj[s], kc + h)),
                  pl.BlockSpec((blk, w), lambda b, h, s, qi, kj: (b * nq + kj[s], vc + h)),
                  pl.BlockSpec((1, hg, 1, blk), lambda b, h, s, qi, kj: (b, h, 0, kj[s]))],
        out_specs=pl.BlockSpec((blk, w), lambda b, h, s, qi, kj: (b * nq + qi[s], h)),
        scratch_shapes=[pltpu.VMEM((hg, blk, 1), F32), pltpu.VMEM((hg, blk, 1), F32),
                        pltpu.VMEM((hg, blk, B_DH), F32)],
    )
    return pl.pallas_call(
        functools.partial(_flash_kernel, blk=blk, lead_pad=N_PAD),
        grid_spec=grid_spec,
        out_shape=jax.ShapeDtypeStruct((n_total, B_W), F32),
        compiler_params=_params(("parallel", "parallel", "arbitrary")),
        name="fox_flash",
    )(qi, kj, pb, pb, pb, fk)


def _fox_sample_kernel(q_ref, kn_ref, vn_ref, kc_ref, vc_ref, fc_ref, fn_ref, prev_ref, o_ref, *, ds):
    del prev_ref
    nh = B_HEADS
    stack = lambda ref: jnp.concatenate([ref[:, h * B_DH:(h + 1) * B_DH] for h in range(nh)], axis=0)
    q = stack(q_ref) * (B_DH ** -0.5 * LOG2E)
    rows = nh * ds
    n_c = kc_ref.shape[1]
    qh_c = lax.broadcasted_iota(jnp.int32, (rows, n_c), 0) // ds
    kh_c = lax.broadcasted_iota(jnp.int32, (rows, n_c), 1) % nh
    s_c = _dot3(q, kc_ref[0], _NT) - fc_ref[0]
    s_c = jnp.where(qh_c == kh_c, s_c, NEG)
    r_i = lax.broadcasted_iota(jnp.int32, (rows, rows), 0)
    c_i = lax.broadcasted_iota(jnp.int32, (rows, rows), 1)
    s_n = _dot3(q, stack(kn_ref), _NT) - fn_ref[0]
    s_n = jnp.where((r_i // ds == c_i // ds) & (c_i % ds <= r_i % ds), s_n, NEG)
    m = jnp.maximum(jnp.max(s_c, axis=1, keepdims=True), jnp.max(s_n, axis=1, keepdims=True))
    p_c = jnp.exp2(s_c - m)
    p_n = jnp.exp2(s_n - m)
    l = jnp.sum(p_c, axis=1, keepdims=True) + jnp.sum(p_n, axis=1, keepdims=True)
    o = (_dot3(p_c, vc_ref[0]) + _dot3(p_n, stack(vn_ref))) / l
    for h in range(nh):
        o_ref[:, h * B_DH:(h + 1) * B_DH] = o[h * ds:(h + 1) * ds]


def _fox_sample(pb, kc, vc, f_cache, f_new, prev_out, *, nb, ds, row_block0):
    n_c = kc.shape[1]
    base = (A_QK * 2 + A_V * 2) // B_W
    rb = lambda b: row_block0 + b
    return pl.pallas_call(
        functools.partial(_fox_sample_kernel, ds=ds),
        grid=(nb,),
        in_specs=[pl.BlockSpec((ds, B_W), lambda b: (rb(b), base)),
                  pl.BlockSpec((ds, B_W), lambda b: (rb(b), base + 1)),
                  pl.BlockSpec((ds, B_W), lambda b: (rb(b), base + 2)),
                  pl.BlockSpec((1, n_c, B_DH), lambda b: (b, 0, 0)),
                  pl.BlockSpec((1, n_c, B_DH), lambda b: (b, 0, 0)),
                  pl.BlockSpec((1, 1, n_c), lambda b: (b, 0, 0)),
                  pl.BlockSpec((1, 1, B_HEADS * ds), lambda b: (b, 0, 0)),
                  pl.BlockSpec(memory_space=pl.ANY)],
        out_specs=pl.BlockSpec((ds, B_W), lambda b: (rb(b), 0)),
        out_shape=jax.ShapeDtypeStruct(prev_out.shape, F32),
        input_output_aliases={7: 0},
        compiler_params=_params(("parallel",)),
        name="fox_sample",
    )(pb, pb, pb, kc, vc, f_cache, f_new, prev_out)


def _mlstm_kernel(q_ref, k_ref, v_ref, lir_ref, br_ref, lic_ref, bc_ref, c0_ref, n0_ref, m0_ref, *rest,
                  L, aliased):
    if aliased:
        rest = rest[1:]
    h_ref, cout_ref, nout_ref, mout_ref, c_scr, n_scr, m_scr = rest
    c = pl.program_id(1)

    @pl.when(c == 0)
    def _():
        c_scr[...] = c0_ref[0]
        n_scr[...] = n0_ref[0]
        m_scr[...] = m0_ref[0]

    t_idx = lax.broadcasted_iota(jnp.int32, (L, L), 0)
    s_idx = lax.broadcasted_iota(jnp.int32, (L, L), 1)
    causal = s_idx <= t_idx
    for h in range(C_HEADS):
        qf = q_ref[:, h * C_DQK:(h + 1) * C_DQK]
        kf = k_ref[:, h * C_DQK:(h + 1) * C_DQK] * (C_DQK ** -0.5)
        vb = v_ref[:, h * C_DV:(h + 1) * C_DV].astype(BF16)
        qb = qf.astype(BF16)
        b_col = bc_ref[0, 0, :, h:h + 1]
        li_col = lic_ref[0, 0, :, h:h + 1]
        b_row = br_ref[0, 0, h:h + 1, :]
        li_row = lir_ref[0, 0, h:h + 1, :]
        m_prev = m_scr[h:h + 1, :]
        c_prev = c_scr[h]
        n_prev = n_scr[h:h + 1, :]

        d = jnp.where(causal, b_col - b_row + li_row, -jnp.inf)
        inter = b_col + m_prev
        m_t = jnp.maximum(inter, jnp.max(d, axis=1, keepdims=True))
        pm = jnp.exp(d - m_t)
        w_inter = jnp.exp(inter - m_t)
        sqk = _dot(qb, kf.astype(BF16), _NT) * pm
        num = w_inter * _dot(qb, c_prev.astype(BF16), _NT) + _dot(sqk.astype(BF16), vb)
        den = w_inter * jnp.sum(qf * n_prev, axis=1, keepdims=True) + jnp.sum(sqk, axis=1, keepdims=True)
        h_ref[:, h * C_DV:(h + 1) * C_DV] = num / jnp.maximum(jnp.abs(den), jnp.exp(-m_t))

        b_last = b_row[:, L - 1:L]
        g_row = b_last - b_row + li_row
        g_col = b_last - b_col + li_col
        m_new = jnp.maximum(b_last + m_prev, jnp.max(g_row, axis=1, keepdims=True))
        w_c = jnp.exp(b_last + m_prev - m_new)
        kw = kf * jnp.exp(g_col - m_new)
        c_new = w_c * c_prev + _dot(vb, kw.astype(BF16), _TN)
        n_new = w_c * n_prev + jnp.sum(kw, axis=0, keepdims=True)
        c_scr[h] = c_new
        n_scr[h:h + 1, :] = n_new
        m_scr[h:h + 1, :] = m_new

    @pl.when(c == pl.num_programs(1) - 1)
    def _():
        cout_ref[0] = c_scr[...]
        nout_ref[0] = n_scr[...]
        mout_ref[0] = m_scr[...]


def _mlstm(p, li_row, b_row, li_col, b_col, c0, n0, m0, *, nb, nc, L, row_block0, n_total, prev_out=None):
    aliased = prev_out is not None
    rb = lambda b, c: row_block0 + b * nc + c
    in_specs = [pl.BlockSpec((L, C_QK), lambda b, c: (rb(b, c), 0)),
                pl.BlockSpec((L, C_QK), lambda b, c: (rb(b, c), 1)),
                pl.BlockSpec((L, C_V), lambda b, c: (rb(b, c), 2 * C_QK // C_V)),
                pl.BlockSpec((1, 1, C_HEADS, L), lambda b, c: (b, c, 0, 0)),
                pl.BlockSpec((1, 1, C_HEADS, L), lambda b, c: (b, c, 0, 0)),
                pl.BlockSpec((1, 1, L, C_HEADS), lambda b, c: (b, c, 0, 0)),
                pl.BlockSpec((1, 1, L, C_HEADS), lambda b, c: (b, c, 0, 0)),
                pl.BlockSpec((1, C_HEADS, C_DV, C_DQK), lambda b, c: (b, 0, 0, 0)),
                pl.BlockSpec((1, C_HEADS, C_DQK), lambda b, c: (b, 0, 0)),
                pl.BlockSpec((1, C_HEADS, 1), lambda b, c: (b, 0, 0))]
    args = [p, p, p, li_row, b_row, li_col, b_col, c0, n0, m0]
    io_alias = {}
    if aliased:
        in_specs.append(pl.BlockSpec(memory_space=pl.ANY))
        args.append(prev_out)
        io_alias = {len(args) - 1: 0}
    return pl.pallas_call(
        functools.partial(_mlstm_kernel, L=L, aliased=aliased),
        grid=(nb, nc),
        in_specs=in_specs,
        out_specs=[pl.BlockSpec((L, C_V), lambda b, c: (rb(b, c), 0)),
                   pl.BlockSpec((1, C_HEADS, C_DV, C_DQK), lambda b, c: (b, 0, 0, 0)),
                   pl.BlockSpec((1, C_HEADS, C_DQK), lambda b, c: (b, 0, 0)),
                   pl.BlockSpec((1, C_HEADS, 1), lambda b, c: (b, 0, 0))],
        out_shape=[jax.ShapeDtypeStruct((n_total, C_V), F32),
                   jax.ShapeDtypeStruct((nb, C_HEADS, C_DV, C_DQK), F32),
                   jax.ShapeDtypeStruct((nb, C_HEADS, C_DQK), F32),
                   jax.ShapeDtypeStruct((nb, C_HEADS, 1), F32)],
        scratch_shapes=[pltpu.VMEM((C_HEADS, C_DV, C_DQK), F32),
                        pltpu.VMEM((C_HEADS, C_DQK), F32),
                        pltpu.VMEM((C_HEADS, 1), F32)],
        input_output_aliases=io_alias,
        compiler_params=_params(("parallel", "arbitrary")),
        name="mlstm_L%d" % L,
    )(*args)


def _mixout_kernel(*refs, hd, act, has_b, precise, n_prev):
    refs = list(refs)
    h_ref, a_ref, r_ref = refs[:3]
    b_ref = refs[3] if has_b else None
    k = 4 if has_b else 3
    ga_ref, wo_ref, gf_ref, wrh_ref, wrl_ref, br_ref, cnt0_ref = refs[k:k + 7]
    h1_ref, xn_ref, info_ref, cnt_ref, cnt_scr = refs[k + 7 + n_prev:]
    cast = (lambda x: x) if precise else (lambda x: x.astype(BF16))
    a = a_ref[...]
    r = r_ref[...]
    gate = r * _sigmoid(r) if act == "silu" else _sigmoid(r)
    parts = []
    for hh in range(a.shape[1] // hd):
        sl = slice(hh * hd, (hh + 1) * hd)
        parts.append(cast(_rms(a[:, sl], ga_ref[...]) * gate[:, sl]))
    if has_b:
        parts.append(cast(b_ref[...]))
    cat = jnp.concatenate(parts, axis=1)
    h1 = h_ref[...] + (_dot3(cat, wo_ref[...]) if precise else _dot(cat, wo_ref[...]))
    h1_ref[...] = h1
    xn = _rms(h1, gf_ref[...])
    xn_ref[...] = _pack_bf16_pairs(xn)
    xh, xl = _split(xn)
    logits = _dot(xh, wrh_ref[...]) + _dot(xh, wrl_ref[...]) + _dot(xl, wrh_ref[...]) + br_ref[...]

    lane = lax.broadcasted_iota(jnp.int32, logits.shape, 1)
    lanef = lane.astype(F32)
    is_g = lane < N_GROUPS
    gl = jnp.where(is_g, logits, -jnp.inf)
    gmax = jnp.max(gl, axis=1, keepdims=True)
    gidx = jnp.min(jnp.where(gl == gmax, lanef, float(LANES)), axis=1, keepdims=True)
    wg = 1.0 / jnp.sum(jnp.where(is_g, jnp.exp(gl - gmax), 0.0), axis=1, keepdims=True)
    lo = N_GROUPS + N_EXP * gidx
    el = jnp.where((lanef >= lo) & (lanef < lo + N_EXP), logits, -jnp.inf)
    m1 = jnp.max(el, axis=1, keepdims=True)
    i1 = jnp.min(jnp.where(el == m1, lanef, float(LANES)), axis=1, keepdims=True)
    el2 = jnp.where(lanef == i1, -jnp.inf, el)
    m2 = jnp.max(el2, axis=1, keepdims=True)
    i2 = jnp.min(jnp.where(el2 == m2, lanef, float(LANES)), axis=1, keepdims=True)
    t = jnp.exp(m2 - m1)
    w1 = wg / (1.0 + t)
    w2 = wg * t / (1.0 + t)
    e1 = i1 - N_GROUPS
    e2 = i2 - N_GROUPS

    @pl.when(pl.program_id(0) == 0)
    def _():
        cnt_scr[...] = cnt0_ref[...]

    tm = logits.shape[0]
    pick = jnp.where((lanef == e1) | (lanef == e2), 1.0, 0.0)
    earlier = (lax.broadcasted_iota(jnp.int32, (tm, tm), 1) < lax.broadcasted_iota(jnp.int32, (tm, tm), 0))
    before = _dot(jnp.where(earlier, 1.0, 0.0).astype(BF16), pick.astype(BF16)) + cnt_scr[...]
    r1 = jnp.sum(jnp.where(lanef == e1, before, 0.0), axis=1, keepdims=True)
    r2 = jnp.sum(jnp.where(lanef == e2, before, 0.0), axis=1, keepdims=True)
    cnt_new = cnt_scr[...] + jnp.sum(pick, axis=0, keepdims=True)
    cnt_scr[...] = cnt_new
    cnt_ref[...] = cnt_new
    info_ref[...] = jnp.where(lane == 0, e1, jnp.where(lane == 1, e2, jnp.where(lane == 2, w1, jnp.where(
        lane == 3, w2, jnp.where(lane == 4, r1, jnp.where(lane == 5, r2, 0.0))))))


def _mixout(h, a, a_col, r, r_col, b, g_head, w_o, g_ffn, wr_hi, wr_lo, b_r, counts0, *, hd, act,
            row0, n_rows, prev=None):
    n, d = h.shape
    precise = prev is not None
    tm = _tile(n_rows, 256, SUBLANES)
    assert row0 % tm == 0
    blk0 = row0 // tm
    wa = w_o.shape[0] if b is None else w_o.shape[0] - B_W
    has_b = b is not None
    row = lambda i: (blk0 + i, 0)
    const = lambda i: (0, 0)
    in_specs = [pl.BlockSpec((tm, d), row),
                pl.BlockSpec((tm, wa), lambda i: (blk0 + i, a_col)),
                pl.BlockSpec((tm, wa), lambda i: (blk0 + i, r_col))]
    args = [h, a, r]
    if has_b:
        in_specs.append(pl.BlockSpec((tm, B_W), row))
        args.append(b)
    in_specs += [pl.BlockSpec((1, hd), const), pl.BlockSpec(w_o.shape, const), pl.BlockSpec((1, d), const),
                 pl.BlockSpec((d, LANES), const), pl.BlockSpec((d, LANES), const), pl.BlockSpec((1, LANES), const),
                 pl.BlockSpec((1, LANES), const)]
    args += [g_head.reshape(1, hd), w_o if precise else w_o.astype(BF16), g_ffn.reshape(1, d), wr_hi, wr_lo, b_r,
             counts0]
    io_alias = {}
    if precise:
        for k, arr in enumerate(prev):
            in_specs.append(pl.BlockSpec(memory_space=pl.ANY))
            args.append(arr)
            io_alias[len(args) - 1] = k
    return pl.pallas_call(
        functools.partial(_mixout_kernel, hd=hd, act=act, has_b=has_b, precise=precise, n_prev=len(io_alias)),
        grid=(n_rows // tm,),
        in_specs=in_specs,
        out_specs=[pl.BlockSpec((tm, d), row), pl.BlockSpec((tm, d // 2), row), pl.BlockSpec((tm, LANES), row),
                   pl.BlockSpec((1, LANES), const)],
        out_shape=[jax.ShapeDtypeStruct((n, d), F32), jax.ShapeDtypeStruct((n, d // 2), jnp.int32),
                   jax.ShapeDtypeStruct((n, LANES), F32), jax.ShapeDtypeStruct((1, LANES), F32)],
        scratch_shapes=[pltpu.VMEM((1, LANES), F32)],
        input_output_aliases=io_alias,
        compiler_params=_params(("arbitrary",)),
        name="mixout_" + act + ("_precise" if precise else ""),
    )(*args)


def _sc_gather(table, idx):
    r = idx.shape[0]
    w = table.shape[1]
    n_workers = SC_CORES * SC_SUBCORES
    per_worker = r // n_workers
    step = SC_GATHER_ROWS * SC_GATHER_BUFS
    assert r % n_workers == 0 and per_worker % step == 0, (r, n_workers, step)
    mesh = plsc.VectorSubcoreMesh(core_axis_name="c", subcore_axis_name="s")

    @functools.partial(
        pl.kernel, mesh=mesh,
        out_type=jax.ShapeDtypeStruct((r, w), table.dtype),
        scratch_types=[pltpu.VMEM((SC_GATHER_BUFS, SC_GATHER_ROWS), jnp.int32),
                       pltpu.VMEM((SC_GATHER_BUFS, SC_GATHER_ROWS, w), table.dtype),
                       pltpu.SemaphoreType.DMA((SC_GATHER_BUFS,)),
                       pltpu.SemaphoreType.DMA((SC_GATHER_BUFS,))],
    )
    def gather(table_hbm, idx_hbm, out_hbm, idx_v, rows_v, gather_sem, store_sem):
        worker = lax.axis_index("s") * SC_CORES + lax.axis_index("c")
        base = worker * per_worker

        @pl.loop(0, per_worker // step)
        def _(j):
            off = pl.multiple_of(base + j * step, step)
            rows = lambda b: pl.ds(off + b * SC_GATHER_ROWS, SC_GATHER_ROWS)
            gathers, stores = [], []
            for b in range(SC_GATHER_BUFS):
                pltpu.sync_copy(idx_hbm.at[rows(b)], idx_v.at[b])
                gathers.append(pltpu.async_copy(table_hbm.at[idx_v.at[b]], rows_v.at[b], gather_sem.at[b]))
            for b in range(SC_GATHER_BUFS):
                gathers[b].wait()
                stores.append(pltpu.async_copy(rows_v.at[b], out_hbm.at[rows(b)], store_sem.at[b]))
            for b in range(SC_GATHER_BUFS):
                stores[b].wait()

    return gather(table, idx)


def _expert_kernel(te_ref, nu_ref, x_ref, wg_ref, wu_ref, wd_ref, y_ref, wgb, wub, wdb):
    i = pl.program_id(0)
    live = i < nu_ref[0]
    new_expert = (i == 0) | (te_ref[i] != te_ref[jnp.maximum(i - 1, 0)])

    @pl.when(live & new_expert)
    def _():
        wgb[...] = wg_ref[0].astype(BF16)
        wub[...] = wu_ref[0].astype(BF16)
        wdb[...] = wd_ref[0].astype(BF16)

    @pl.when(live)
    def _():
        x = _unpack_bf16_pairs(x_ref[...]).astype(BF16)
        g = _dot(x, wgb[...])
        u = _dot(x, wub[...])
        y_ref[...] = _pack_bf16_pairs(_dot((g * _sigmoid(g) * u).astype(BF16), wdb[...]))

    @pl.when(jnp.logical_not(live))
    def _():
        y_ref[...] = jnp.zeros_like(y_ref)


def _experts(xs, tile_expert, n_used, w_gate, w_up, w_down, *, tm):
    n_slots, dp = xs.shape
    d, f = w_gate.shape[-2:]
    grid_spec = pltpu.PrefetchScalarGridSpec(
        num_scalar_prefetch=2,
        grid=(n_slots // tm,),
        in_specs=[pl.BlockSpec((tm, dp), lambda i, te, nu: (i, 0)),
                  pl.BlockSpec((1, d, f), lambda i, te, nu: (te[i], 0, 0)),
                  pl.BlockSpec((1, d, f), lambda i, te, nu: (te[i], 0, 0)),
                  pl.BlockSpec((1, f, d), lambda i, te, nu: (te[i], 0, 0))],
        out_specs=pl.BlockSpec((tm, dp), lambda i, te, nu: (i, 0)),
        scratch_shapes=[pltpu.VMEM((d, f), BF16), pltpu.VMEM((d, f), BF16), pltpu.VMEM((f, d), BF16)],
    )
    return pl.pallas_call(
        _expert_kernel,
        grid_spec=grid_spec,
        out_shape=jax.ShapeDtypeStruct((n_slots, dp), jnp.int32),
        compiler_params=_params(("arbitrary",)),
        name="moe_experts",
    )(tile_expert, n_used, xs, w_gate, w_up, w_down)


def _combine_kernel(h_ref, info_ref, g_ref, y0_ref, y1_ref, o_ref, *, final_norm):
    h2 = h_ref[...] + (info_ref[:, 2:3] * _unpack_bf16_pairs(y0_ref[...])
                       + info_ref[:, 3:4] * _unpack_bf16_pairs(y1_ref[...]))
    o_ref[...] = _rms(h2, g_ref[...]) if final_norm else h2


def _combine(h, yg, info, g, *, final_norm):
    n, d = h.shape
    tm = _tile(n, 512, SUBLANES)
    nt = n // tm
    return pl.pallas_call(
        functools.partial(_combine_kernel, final_norm=final_norm),
        grid=(nt,),
        in_specs=[pl.BlockSpec((tm, d), lambda i: (i, 0)),
                  pl.BlockSpec((tm, LANES), lambda i: (i, 0)),
                  pl.BlockSpec((1, d), lambda i: (0, 0)),
                  pl.BlockSpec((tm, d // 2), lambda i: (i, 0)),
                  pl.BlockSpec((tm, d // 2), lambda i: (nt + i, 0))],
        out_specs=pl.BlockSpec((tm, d), lambda i: (i, 0)),
        out_shape=jax.ShapeDtypeStruct((n, d), F32),
        compiler_params=_params(("parallel",)),
        name="moe_combine",
    )(h, info, g.reshape(1, d), yg, yg)


def _moe(h1, xn, info, counts_f, w_gate, w_up, w_down, expert_base, g_next, *, final_norm):
    n, d = h1.shape
    n_e = N_GROUPS * N_EXP
    tm = _tile(2 * n, 256, SUBLANES)
    n_tiles = (2 * n) // tm + n_e
    n_slots = n_tiles * tm
    experts = jnp.arange(n_e, dtype=jnp.int32)
    counts = counts_f[0, :n_e].astype(jnp.int32)
    starts = jnp.cumsum(counts) - counts
    padded = ((counts + tm - 1) // tm) * tm
    pad_ends = jnp.cumsum(padded)
    pad_starts = pad_ends - padded
    n_used = (pad_ends[-1] // tm).astype(jnp.int32)
    eid = info[:, 0:2].astype(jnp.int32)
    rank = info[:, 4:6].astype(jnp.int32)
    lookup = lambda table, e: jnp.sum(jnp.where(e[..., None] == experts, table, 0), axis=-1)
    slot_of_pick = lookup(pad_starts, eid) + rank
    tok = jnp.broadcast_to(jnp.arange(n, dtype=jnp.int32)[:, None], (n, 2)).reshape(-1)
    _, tok_sorted = lax.sort_key_val(slot_of_pick.reshape(-1), tok)
    slots = jnp.arange(n_slots, dtype=jnp.int32)
    e_of_slot = jnp.minimum(jnp.sum(slots[:, None] >= pad_ends[None, :], axis=1), n_e - 1).astype(jnp.int32)
    r_of_slot = slots - lookup(pad_starts, e_of_slot)
    live = r_of_slot < lookup(counts, e_of_slot)
    src_token = jnp.where(live, tok_sorted[jnp.clip(lookup(starts, e_of_slot) + r_of_slot, 0, 2 * n - 1)], slots % n)
    tile_expert = (expert_base + e_of_slot[::tm]).astype(jnp.int32)

    xs = _sc_gather(xn, src_token.astype(jnp.int32))
    ys = _experts(xs, tile_expert, n_used.reshape(1), w_gate, w_up, w_down, tm=tm)
    yg = _sc_gather(ys, slot_of_pick.T.reshape(-1))
    return _combine(h1, yg, info, g_next, final_norm=final_norm)


def _router_weights(w_rg, b_rg, w_re, b_re):
    d = w_rg.shape[0]
    pad = LANES - N_GROUPS - N_GROUPS * N_EXP
    w = jnp.concatenate([w_rg, w_re, jnp.zeros((d, pad), F32)], axis=1)
    b = jnp.concatenate([b_rg, b_re, jnp.zeros((pad,), F32)]).reshape(1, LANES)
    hi = w.astype(BF16)
    lo = (w - hi.astype(F32)).astype(BF16)
    return hi, lo, b


def _rows_to_lanes(x, nb, t, nch):
    return x.reshape(nb, t, nch).transpose(0, 2, 1).reshape(nb * nch, t)


def _even_layer(h, dims, g_mix, w_in, w_g2, b_g, b_f, g_a, w_o, state_gla, ck, cv, clf):
    nb, t, db, ds, npr, n = dims
    d = h.shape[1]
    qa, ka, va, ra, ga, qb, kb, vb, fb = jnp.split(
        w_in, [A_QK, 2 * A_QK, 2 * A_QK + A_V, 2 * A_QK + 2 * A_V, 2 * A_QK + 2 * A_V + A_RANK,
               2 * A_QK + 2 * A_V + A_RANK + B_W, 2 * A_QK + 2 * A_V + A_RANK + 2 * B_W,
               2 * A_QK + 2 * A_V + A_RANK + 3 * B_W], axis=1)
    w_packed = jnp.concatenate(
        [qa, ka, va, ra, qb, kb, vb, ga, fb, jnp.zeros((d, LANES - A_RANK - B_HEADS), F32)], axis=1)
    q_off = 2 * A_QK + 2 * A_V
    colscale = jnp.ones((1, MAIN_W), F32).at[:, q_off:q_off + B_W].set(B_DH ** -0.5 * LOG2E)
    p, pb = _proj(h, g_mix, w_packed, colscale, n_rows=npr)
    p = _proj_precise(h, g_mix, w_packed, p, row0=npr)

    nh = B_HEADS
    fcol = MAIN_W + A_RANK
    zeros_col = lambda r: jnp.zeros((r, 1), F32)
    bias_row = lambda lanes: jnp.tile(b_f, lanes // nh).reshape(1, lanes)
    fb_p = p[:npr, fcol:fcol + nh].reshape(nb, t * nh)
    logf_p, f_p = _gate_scan(fb_p, bias_row(t * nh), zeros_col(nb), mode="fox", act_start=0,
                             valid_start=N_PAD * nh, valid_end=t * nh, seg=None, stride=nh)
    past = ck.shape[1]
    n_c = past * nh
    x_s = _pad_lanes(jnp.concatenate([clf.reshape(db, n_c), p[npr:, fcol:fcol + nh].reshape(db, ds * nh)], axis=1))
    logf_s, f_s = _gate_scan(x_s, bias_row(x_s.shape[1]), zeros_col(db), mode="fox", act_start=n_c,
                             valid_start=0, valid_end=n_c + ds * nh, seg=None, stride=nh)

    nc = t // CHUNK
    oa, s_p = _gla(p, jnp.zeros((nb, A_HEADS, A_DK, A_DV), F32), w_g2, b_g,
                   nb=nb, nc=nc, L=CHUNK, row_block0=0, lead_pad=N_PAD, n_total=n)
    oa, s_s = _gla(p, state_gla, w_g2, b_g, nb=db, nc=1, L=ds, row_block0=npr // ds, lead_pad=0,
                   n_total=n, prev_out=oa, precise=True)

    fk = (f_p * LOG2E).reshape(nb, t, nh).transpose(0, 2, 1).reshape(nb, nh, 1, t)
    ob = _flash(pb, fk, nb=nb, t=t, n_total=n)
    f_cache = (f_s[:, :n_c] * LOG2E).reshape(db, 1, n_c)
    f_new = (f_s[:, n_c:n_c + ds * nh] * LOG2E).reshape(db, ds, nh).transpose(0, 2, 1).reshape(db, 1, nh * ds)
    ob = _fox_sample(p, ck.reshape(db, n_c, B_DH), cv.reshape(db, n_c, B_DH), f_cache, f_new, ob,
                     nb=db, ds=ds, row_block0=npr // ds)

    kcol = q_off + B_W
    states = dict(
        s_p=s_p, s_s=s_s,
        k_p=p[:npr, kcol:kcol + B_W].reshape(nb, t, B_HEADS, B_DH)[:, N_PAD:],
        v_p=p[:npr, kcol + B_W:kcol + 2 * B_W].reshape(nb, t, B_HEADS, B_DH)[:, N_PAD:],
        f_p=logf_p.reshape(nb, t, nh)[:, N_PAD:],
        k_s=p[npr:, kcol:kcol + B_W].reshape(db, ds, B_HEADS, B_DH),
        v_s=p[npr:, kcol + B_W:kcol + 2 * B_W].reshape(db, ds, B_HEADS, B_DH),
        f_s=logf_s[:, n_c:n_c + ds * nh].reshape(db, ds, nh))
    return (oa, 0, p, (2 * A_QK + A_V) // A_V, ob, g_a, w_o), states


def _chunk_rows(x, nb, nch, nc, L):
    x = x[:, :nc * L].reshape(nb, nch, nc, L)
    return x.transpose(0, 2, 1, 3), x.transpose(0, 2, 3, 1)


def _odd_layer(h, dims, g_mix, w_in, b_gate, g_c, w_o, c0, n0, m0):
    nb, t, db, ds, npr, n = dims
    d = h.shape[1]
    w_packed = jnp.concatenate(
        [w_in, jnp.zeros((d, LANES - 2 * C_HEADS), F32)], axis=1)
    p, _ = _proj(h, g_mix, w_packed, jnp.ones((1, MAIN_W), F32), n_rows=npr)
    p = _proj_precise(h, g_mix, w_packed, p, row0=npr)

    ng = 2 * C_HEADS
    isf = (jnp.arange(ng) >= C_HEADS).astype(F32)
    nc = t // CHUNK

    def gates(rows, nbatch, tt, valid_start, seg):
        x = _pad_lanes(_rows_to_lanes(rows, nbatch, tt, ng))
        val, cum = _gate_scan(x, jnp.tile(b_gate, nbatch).reshape(-1, 1), jnp.tile(isf, nbatch).reshape(-1, 1),
                              mode="mlstm", act_start=0, valid_start=valid_start, valid_end=tt, seg=seg)
        return val.reshape(nbatch, ng, -1), cum.reshape(nbatch, ng, -1)

    val_p, cum_p = gates(p[:npr, MAIN_W:MAIN_W + ng], nb, t, N_PAD, CHUNK)
    val_s, cum_s = gates(p[npr:, MAIN_W:MAIN_W + ng], db, ds, 0, ds)

    def chunked(val, cum, nbatch, ncs, L):
        li_row, li_col = _chunk_rows(val[:, :C_HEADS].reshape(nbatch * C_HEADS, -1), nbatch, C_HEADS, ncs, L)
        b_row, b_col = _chunk_rows(cum[:, C_HEADS:].reshape(nbatch * C_HEADS, -1), nbatch, C_HEADS, ncs, L)
        return li_row, b_row, li_col, b_col

    zc = jnp.zeros((nb, C_HEADS, C_DV, C_DQK), F32)
    zn = jnp.zeros((nb, C_HEADS, C_DQK), F32)
    zm = jnp.zeros((nb, C_HEADS, 1), F32)
    hm, c_p, n_p, m_p = _mlstm(p, *chunked(val_p, cum_p, nb, nc, CHUNK), zc, zn, zm,
                               nb=nb, nc=nc, L=CHUNK, row_block0=0, n_total=n)
    hm, c_s, n_s, m_s = _mlstm(p, *chunked(val_s, cum_s, db, 1, ds), c0, n0, m0.reshape(db, C_HEADS, 1),
                               nb=db, nc=1, L=ds, row_block0=npr // ds, n_total=n, prev_out=hm)
    states = dict(c_p=c_p, n_p=n_p, m_p=m_p.reshape(nb, C_HEADS), c_s=c_s, n_s=n_s, m_s=m_s.reshape(db, C_HEADS))
    return (hm, 0, p, (2 * C_QK + C_V) // C_V, None, g_c, w_o), states


def kernel(x_prompt, x_sample, state_gla, cache_fox_k, cache_fox_v, cache_fox_logf, state_mlstm_c, state_mlstm_n, state_mlstm_m, meta_tokens, norm_mix, norm_ffn, norm_final, w_in_even, w_gla_gate2, b_gla_gate, b_fox_f, g_gla_out, w_out_even, w_in_odd, b_mlstm_gate, g_mlstm_out, w_out_odd, w_router_group, b_router_group, w_router_expert, b_router_expert, w_exp_gate, w_exp_up, w_exp_down):
    nb, seq, d = x_prompt.shape
    db, ds, _ = x_sample.shape
    t = LEAD + seq
    npr, nsm = nb * t, db * ds
    n = npr + nsm
    dims = (nb, t, db, ds, npr, n)
    depth = norm_mix.shape[0]
    n_e = N_GROUPS * N_EXP
    f = w_exp_gate.shape[-1]

    hp = jnp.concatenate([jnp.zeros((nb, N_PAD, d), F32),
                          jnp.broadcast_to(meta_tokens[None], (nb, N_META, d)), x_prompt], axis=1)
    h = jnp.concatenate([hp.reshape(npr, d), x_sample.reshape(nsm, d)], axis=0)
    wg_all = w_exp_gate.reshape(depth * n_e, d, f)
    wu_all = w_exp_up.reshape(depth * n_e, d, f)
    wd_all = w_exp_down.reshape(depth * n_e, f, d)

    even, odd = [], []
    for l in range(depth):
        if l % 2 == 0:
            e = l // 2
            mix, st = _even_layer(h, dims, norm_mix[l], w_in_even[e], w_gla_gate2[e], b_gla_gate[e], b_fox_f[e],
                                  g_gla_out[e], w_out_even[e], state_gla[e], cache_fox_k[e], cache_fox_v[e],
                                  cache_fox_logf[e])
            even.append(st)
            hd, act = A_DV, "silu"
        else:
            o = l // 2
            mix, st = _odd_layer(h, dims, norm_mix[l], w_in_odd[o], b_mlstm_gate[o], g_mlstm_out[o], w_out_odd[o],
                                 state_mlstm_c[o], state_mlstm_n[o], state_mlstm_m[o])
            odd.append(st)
            hd, act = C_DV, "sigmoid"
        a, a_col, r, r_col, b, g_head, w_o = mix
        wr_hi, wr_lo, b_r = _router_weights(w_router_group[l], b_router_group[l], w_router_expert[l],
                                            b_router_expert[l])
        mix_args = (h, a, a_col, r, r_col, b, g_head, w_o, norm_ffn[l], wr_hi, wr_lo, b_r)
        h1, xn, info, counts = _mixout(*mix_args, jnp.zeros((1, LANES), F32), hd=hd, act=act, row0=0, n_rows=npr)
        h1, xn, info, counts = _mixout(*mix_args, counts, hd=hd, act=act, row0=npr, n_rows=nsm,
                                       prev=(h1, xn, info))
        last = l == depth - 1
        h = _moe(h1, xn, info, counts, wg_all, wu_all, wd_all, l * n_e, norm_final if last else norm_ffn[l],
                 final_norm=last)

    y_prompt = h[:npr].reshape(nb, t, d)[:, LEAD:]
    y_sample = h[npr:].reshape(db, ds, d)
    stack = lambda sts, key: jnp.stack([s[key] for s in sts])
    return (y_prompt, y_sample,
            stack(even, "s_p"), stack(even, "k_p"), stack(even, "v_p"), stack(even, "f_p"),
            stack(odd, "c_p"), stack(odd, "n_p"), stack(odd, "m_p"),
            stack(even, "s_s"), stack(even, "k_s"), stack(even, "v_s"), stack(even, "f_s"),
            stack(odd, "c_s"), stack(odd, "n_s"), stack(odd, "m_s"))
```

```python
import functools

import jax
import jax.numpy as jnp
from jax import lax
from jax.experimental import pallas as pl
from jax.experimental.pallas import tpu as pltpu
from jax.experimental.pallas import tpu_sc as plsc

F32 = jnp.float32
BF16 = jnp.bfloat16

CHUNK = 64
N_META = 16
LEAD = 128
N_PAD = LEAD - N_META
A_HEADS, A_DK, A_DV, A_RANK = 4, 64, 128, 16
A_GATE_NORM = 16.0
B_HEADS, B_DH = 4, 128
C_HEADS, C_DQK, C_DV = 4, 128, 256
GATE_CAP = 15.0
N_GROUPS, N_EXP = 4, 8
EPS = 1e-6
NEG = -1e30
LOG2E = 1.4426950408889634
A_QK = A_HEADS * A_DK
A_V = A_HEADS * A_DV
B_W = B_HEADS * B_DH
C_QK = C_HEADS * C_DQK
C_V = C_HEADS * C_DV

LANES = 128
SUBLANES = 8
VMEM_LIMIT_BYTES = 56 * 1024 * 1024
GLA_SUB = 16
SC_CORES, SC_SUBCORES = 2, 16
SC_GATHER_ROWS = 16
SC_GATHER_BUFS = 4
FLASH_HEADS = 2
MAIN_W = 3072
PROJ_W = MAIN_W + LANES

_NT = (((1,), (1,)), ((), ()))
_TN = (((0,), (0,)), ((), ()))
_NN = (((1,), (0,)), ((), ()))


def _params(sem):
    return pltpu.CompilerParams(dimension_semantics=sem, vmem_limit_bytes=VMEM_LIMIT_BYTES)


def _tile(n, pref, mult):
    t = (min(pref, n) // mult) * mult
    while t > mult and n % t:
        t -= mult
    assert t >= mult and n % t == 0, (n, pref, mult)
    return t


def _dot(a, b, dims=_NN):
    return lax.dot_general(a, b, dims, preferred_element_type=F32)


def _split(x):
    hi = x.astype(BF16)
    lo = (x - hi.astype(F32)).astype(BF16)
    return hi, lo


def _dot3(a, b, dims=_NN):
    ah, al = _split(a)
    bh, bl = _split(b)
    return _dot(ah, bh, dims) + _dot(ah, bl, dims) + _dot(al, bh, dims)


def _log_sigmoid(x):
    return jnp.minimum(x, 0.0) - jnp.log1p(jnp.exp(-jnp.abs(x)))


def _sigmoid(x):
    return 1.0 / (1.0 + jnp.exp(-x))


def _rms(x, g):
    return x * lax.rsqrt(jnp.mean(x * x, axis=-1, keepdims=True) + EPS) * g


def _pack_bf16_pairs(x):
    w = x.shape[1] // 2
    hi = lax.bitcast_convert_type(x[:, :w].astype(BF16).astype(F32), jnp.int32)
    lo = lax.bitcast_convert_type(x[:, w:].astype(BF16).astype(F32), jnp.int32)
    return hi | lax.shift_right_logical(lo, 16)


def _unpack_bf16_pairs(p):
    hi = lax.bitcast_convert_type(p & jnp.int32(-65536), F32)
    lo = lax.bitcast_convert_type(lax.shift_left(p, 16), F32)
    return jnp.concatenate([hi, lo], axis=1)


def _cumsum_rows(x):
    n = x.shape[0]
    row = lax.broadcasted_iota(jnp.int32, x.shape, 0)
    s = 1
    while s < n:
        x = x + jnp.where(row >= s, pltpu.roll(x, s, axis=0), 0.0)
        s *= 2
    return x


def _proj_kernel(x_ref, g_ref, w_ref, *rest, col_chunk, kv_col):
    if kv_col is None:
        (o_ref,) = rest
    else:
        cs_ref, o_ref, ob_ref, *kv_refs = rest
    tm = x_ref.shape[0]
    xn = _rms(x_ref[...], g_ref[...]).astype(BF16)
    for c0 in range(0, PROJ_W, col_chunk):
        c1 = min(c0 + col_chunk, PROJ_W)
        y = _dot(xn, w_ref[:, c0:c1])
        o_ref[:, c0:c1] = y
        if kv_col is not None:
            if c0 < MAIN_W:
                m1 = min(c1, MAIN_W)
                ob_ref[:, c0:m1] = (y[:, :m1 - c0] * cs_ref[:, c0:m1]).astype(BF16)
            for g0 in range(c0, c1, LANES):
                rel = g0 - kv_col
                if 0 <= rel < 2 * B_W:
                    head = (rel % B_W) // B_DH
                    kv_refs[rel // B_W][pl.ds(head, tm, stride=B_HEADS), :] = y[:, g0 - c0:g0 - c0 + LANES]


def _proj_precise_kernel(x_ref, g_ref, w_ref, prev_ref, o_ref, *, col_chunk):
    del prev_ref
    xn = _rms(x_ref[...], g_ref[...])
    for c0 in range(0, PROJ_W, col_chunk):
        c1 = min(c0 + col_chunk, PROJ_W)
        o_ref[:, c0:c1] = _dot3(xn, w_ref[:, c0:c1])


def _proj(h, g, w_packed, *, n_rows, colscale=None, kv_col=None):
    n, d = h.shape
    tm = _tile(n_rows, 512, 16)
    in_specs = [pl.BlockSpec((tm, d), lambda i: (i, 0)),
                pl.BlockSpec((1, d), lambda i: (0, 0)),
                pl.BlockSpec((d, PROJ_W), lambda i: (0, 0))]
    args = [h, g.reshape(1, d), w_packed.astype(BF16)]
    out_specs = [pl.BlockSpec((tm, PROJ_W), lambda i: (i, 0))]
    out_shape = [jax.ShapeDtypeStruct((n, PROJ_W), F32)]
    if kv_col is not None:
        in_specs.append(pl.BlockSpec((1, MAIN_W), lambda i: (0, 0)))
        args.append(colscale)
        out_specs += [pl.BlockSpec((tm, MAIN_W), lambda i: (i, 0))] + [pl.BlockSpec((tm * B_HEADS, B_DH), lambda i: (i, 0))] * 2
        out_shape += ([jax.ShapeDtypeStruct((n_rows, MAIN_W), BF16)]
                      + [jax.ShapeDtypeStruct((n_rows * B_HEADS, B_DH), F32)] * 2)
    return pl.pallas_call(
        functools.partial(_proj_kernel, col_chunk=640, kv_col=kv_col),
        grid=(n_rows // tm,),
        in_specs=in_specs,
        out_specs=out_specs,
        out_shape=out_shape,
        compiler_params=_params(("parallel",)),
        name="proj",
    )(*args)


def _proj_precise(h, g, w_packed, prev, *, row0):
    n, d = h.shape
    tm = _tile(n - row0, 512, SUBLANES)
    assert row0 % tm == 0
    return pl.pallas_call(
        functools.partial(_proj_precise_kernel, col_chunk=640),
        grid=((n - row0) // tm,),
        in_specs=[pl.BlockSpec((tm, d), lambda i: (row0 // tm + i, 0)),
                  pl.BlockSpec((1, d), lambda i: (0, 0)),
                  pl.BlockSpec((d, PROJ_W), lambda i: (0, 0)),
                  pl.BlockSpec(memory_space=pl.ANY)],
        out_specs=pl.BlockSpec((tm, PROJ_W), lambda i: (row0 // tm + i, 0)),
        out_shape=jax.ShapeDtypeStruct((n, PROJ_W), F32),
        input_output_aliases={3: 0},
        compiler_params=_params(("parallel",)),
        name="proj_precise",
    )(h, g.reshape(1, d), w_packed, prev)


def _gate_scan_kernel(x_ref, bias_ref, isf_ref, val_ref, cum_ref, *, mode, act_start, valid_start, valid_end, seg, stride):
    x = x_ref[...]
    lane = lax.broadcasted_iota(jnp.int32, x.shape, 1)
    valid = (lane >= valid_start) & (lane < valid_end)
    if mode == "fox":
        val = jnp.where(lane >= act_start, _log_sigmoid(x + bias_ref[...]), x)
        val = jnp.where(valid, val, 0.0)
        add = val
    else:
        gate = GATE_CAP * jnp.tanh((x + bias_ref[...]) / GATE_CAP)
        isf = isf_ref[...] > 0.5
        val = jnp.where(isf, jnp.where(valid, _log_sigmoid(gate), 0.0),
                        jnp.where(valid, gate, -jnp.inf))
        add = jnp.where(isf, val, 0.0)
    val_ref[...] = val
    n = x.shape[1]
    pos = lane if seg is None else lane % seg
    limit = n if seg is None else seg
    s = stride
    while s < limit:
        add = add + jnp.where(pos >= s, pltpu.roll(add, s, axis=1), 0.0)
        s *= 2
    cum_ref[...] = add


def _gate_scan(x, bias, isf, *, mode, act_start, valid_start, valid_end, seg, stride=1):
    r, n = x.shape
    full = lambda shape: pl.BlockSpec(shape, lambda i: (0,) * len(shape))
    return pl.pallas_call(
        functools.partial(_gate_scan_kernel, mode=mode, act_start=act_start,
                          valid_start=valid_start, valid_end=valid_end, seg=seg, stride=stride),
        grid=(1,),
        in_specs=[full((r, n)), full(bias.shape), full((r, 1))],
        out_specs=[full((r, n)), full((r, n))],
        out_shape=[jax.ShapeDtypeStruct((r, n), F32)] * 2,
        compiler_params=_params(("arbitrary",)),
        name="gate_scan_" + mode,
    )(x, bias, isf)


def _pad_lanes(x):
    n = x.shape[-1]
    m = -(-n // LANES) * LANES
    return x if m == n else jnp.pad(x, ((0, 0), (0, m - n)))


def _gla_kernel(qk_ref, v_ref, sm_ref, s0_ref, wg2_ref, wg2t_ref, bgr_ref, bgc_ref, *rest,
                L, sub, lead_pad, aliased, precise):
    if aliased:
        rest = rest[1:]
    o_ref, sout_ref, s_scr = rest
    c = pl.program_id(1)
    nh, dk, dv = A_HEADS, A_DK, A_DV
    cast = (lambda x: x) if precise else (lambda x: x.astype(BF16))
    mm = _dot3 if precise else _dot

    @pl.when(c == 0)
    def _():
        s_scr[...] = jnp.zeros_like(s_scr)
        for h in range(nh):
            s_scr[h * dk:(h + 1) * dk, h * dv:(h + 1) * dv] = s0_ref[0, h]

    qk = qk_ref[...]
    q = qk[:, :A_QK] * (A_DK ** -0.5)
    k = qk[:, A_QK:]
    v = v_ref[...]
    ga = sm_ref[:, :A_RANK]
    row = lax.broadcasted_iota(jnp.int32, (L, 1), 0)
    valid = (c * L + row) >= lead_pad
    z = _dot3(ga, wg2_ref[...]) + bgr_ref[...]
    loga = jnp.where(valid, _log_sigmoid(z) / A_GATE_NORM, 0.0)
    k = jnp.where(valid, k, 0.0)
    b = _cumsum_rows(loga)
    b_last = b[L - 1:L, :]
    lane_t = lax.broadcasted_iota(jnp.int32, (1, L), 1)
    zt = _dot3(wg2t_ref[...], ga, _NT) + bgc_ref[...]
    logat = jnp.where((c * L + lane_t) >= lead_pad, _log_sigmoid(zt) / A_GATE_NORM, 0.0)
    b_last_col = jnp.sum(logat, axis=1, keepdims=True)

    qhead = lax.broadcasted_iota(jnp.int32, (1, A_QK), 1) // dk
    vhead = lax.broadcasted_iota(jnp.int32, (1, A_V), 1) // dv
    vb = cast(v)
    zero_b = jnp.zeros((), vb.dtype)
    v_bd = jnp.concatenate([jnp.where(vhead == h, vb, zero_b) for h in range(nh)], axis=0)

    rows_all = lax.broadcasted_iota(jnp.int32, (L, 1), 0)
    a_rows = []
    for i in range(L // sub):
        r0 = i * sub
        ci = jnp.zeros((1, A_QK), F32) if i == 0 else b[r0 - 1:r0, :]
        qt = cast(q[r0:r0 + sub] * jnp.exp(b[r0:r0 + sub] - ci))
        kt = cast(jnp.where(rows_all < r0 + sub, k * jnp.exp(ci - b), 0.0))
        k_stack = jnp.concatenate([jnp.where(qhead == h, kt, zero_b) for h in range(nh)], axis=0)
        a_rows.append(mm(qt, k_stack, _NT))
    a = a_rows[0] if len(a_rows) == 1 else jnp.concatenate(a_rows, axis=0)
    t_idx = lax.broadcasted_iota(jnp.int32, (L, nh * L), 0)
    s_idx = lax.broadcasted_iota(jnp.int32, (L, nh * L), 1) % L
    a = jnp.where(s_idx <= t_idx, a, 0.0)
    o_intra = mm(cast(a), v_bd)

    s_full = s_scr[...]
    o_inter = mm(cast(q * jnp.exp(b)), cast(s_full))
    o_ref[...] = o_inter + o_intra

    k_hat = cast(k * jnp.exp(b_last - b))
    upd = mm(k_hat, vb, _TN)
    khead_col = lax.broadcasted_iota(jnp.int32, (A_QK, 1), 0) // dk
    s_new = jnp.exp(b_last_col) * s_full + jnp.where(khead_col == vhead, upd, 0.0)
    s_scr[...] = s_new

    @pl.when(c == pl.num_programs(1) - 1)
    def _():
        for h in range(nh):
            sout_ref[0, h] = s_new[h * dk:(h + 1) * dk, h * dv:(h + 1) * dv]


def _gla(p, s0, wg2, bg, *, nb, nc, L, row_block0, lead_pad, n_total, prev_out=None, precise=False):
    aliased = prev_out is not None
    rb = lambda b, c: row_block0 + b * nc + c
    in_specs = [pl.BlockSpec((L, 2 * A_QK), lambda b, c: (rb(b, c), 0)),
                pl.BlockSpec((L, A_V), lambda b, c: (rb(b, c), 2 * A_QK // A_V)),
                pl.BlockSpec((L, LANES), lambda b, c: (rb(b, c), MAIN_W // LANES)),
                pl.BlockSpec((1, A_HEADS, A_DK, A_DV), lambda b, c: (b, 0, 0, 0)),
                pl.BlockSpec((A_RANK, A_QK), lambda b, c: (0, 0)),
                pl.BlockSpec((A_QK, A_RANK), lambda b, c: (0, 0)),
                pl.BlockSpec((1, A_QK), lambda b, c: (0, 0)),
                pl.BlockSpec((A_QK, 1), lambda b, c: (0, 0))]
    args = [p, p, p, s0, wg2, wg2.T, bg.reshape(1, A_QK), bg.reshape(A_QK, 1)]
    io_alias = {}
    if aliased:
        in_specs.append(pl.BlockSpec(memory_space=pl.ANY))
        args.append(prev_out)
        io_alias = {len(args) - 1: 0}
    return pl.pallas_call(
        functools.partial(_gla_kernel, L=L, sub=min(GLA_SUB, L), lead_pad=lead_pad, aliased=aliased,
                          precise=precise),
        grid=(nb, nc),
        in_specs=in_specs,
        out_specs=[pl.BlockSpec((L, A_V), lambda b, c: (rb(b, c), 0)),
                   pl.BlockSpec((1, A_HEADS, A_DK, A_DV), lambda b, c: (b, 0, 0, 0))],
        out_shape=[jax.ShapeDtypeStruct((n_total, A_V), F32),
                   jax.ShapeDtypeStruct((nb, A_HEADS, A_DK, A_DV), F32)],
        scratch_shapes=[pltpu.VMEM((A_QK, A_V), F32)],
        input_output_aliases=io_alias,
        compiler_params=_params(("parallel", "arbitrary")),
        name="gla_L%d" % L,
    )(*args)


def _flash_kernel(qi_ref, kj_ref, q_ref, k_ref, v_ref, fk_ref, o_ref, m_scr, l_scr, acc_scr, *, blk, lead_pad):
    step = pl.program_id(2)
    i = qi_ref[step]
    j = kj_ref[step]

    @pl.when(j == 0)
    def _():
        m_scr[...] = jnp.full_like(m_scr, -jnp.inf)
        l_scr[...] = jnp.zeros_like(l_scr)
        acc_scr[...] = jnp.zeros_like(acc_scr)

    def update(masked):
        for g in range(FLASH_HEADS):
            sl = slice(g * B_DH, (g + 1) * B_DH)
            s = _dot(q_ref[:, sl], k_ref[:, sl], _NT) - fk_ref[0, g]
            if masked:
                qpos = i * blk + lax.broadcasted_iota(jnp.int32, (blk, blk), 0)
                kpos = j * blk + lax.broadcasted_iota(jnp.int32, (blk, blk), 1)
                s = jnp.where((kpos <= qpos) & (kpos >= lead_pad), s, NEG)
            m_prev = m_scr[g]
            m_new = jnp.maximum(m_prev, jnp.max(s, axis=1, keepdims=True))
            alpha = jnp.exp2(m_prev - m_new)
            p = jnp.exp2(s - m_new)
            l_scr[g] = alpha * l_scr[g] + jnp.sum(p, axis=1, keepdims=True)
            acc_scr[g] = alpha * acc_scr[g] + _dot(p.astype(BF16), v_ref[:, sl])
            m_scr[g] = m_new

    edge = (j == i) | (j == 0)
    pl.when(edge)(functools.partial(update, True))
    pl.when(jnp.logical_not(edge))(functools.partial(update, False))

    @pl.when(j == i)
    def _():
        for g in range(FLASH_HEADS):
            o_ref[:, g * B_DH:(g + 1) * B_DH] = acc_scr[g] / l_scr[g]


def _flash(pb, fk, *, nb, t, n_total):
    blk = _tile(t, 640, LANES)
    nq = t // blk
    hg = FLASH_HEADS
    w = hg * B_DH
    pairs = [(i, j) for i in range(nq) for j in range(i + 1)]
    qi = jnp.asarray([p[0] for p in pairs], jnp.int32)
    kj = jnp.asarray([p[1] for p in pairs], jnp.int32)
    qc, kc, vc = (A_QK * 2 + A_V * 2) // w, (A_QK * 2 + A_V * 2 + B_W) // w, (A_QK * 2 + A_V * 2 + 2 * B_W) // w
    grid_spec = pltpu.PrefetchScalarGridSpec(
        num_scalar_prefetch=2,
        grid=(nb, B_HEADS // hg, len(pairs)),
        in_specs=[pl.BlockSpec((blk, w), lambda b, h, s, qi, kj: (b * nq + qi[s], qc + h)),
                  pl.BlockSpec((blk, w), lambda b, h, s, qi, kj: (b * nq + kj[s], kc + h)),
                  pl.BlockSpec((blk, w), lambda b, h, s, qi, kj: (b * nq + kj[s], vc + h)),
                  pl.BlockSpec((1, hg, 1, blk), lambda b, h, s, qi, kj: (b, h, 0, kj[s]))],
        out_specs=pl.BlockSpec((blk, w), lambda b, h, s, qi, kj: (b * nq + qi[s], h)),
        scratch_shapes=[pltpu.VMEM((hg, blk, 1), F32), pltpu.VMEM((hg, blk, 1), F32),
                        pltpu.VMEM((hg, blk, B_DH), F32)],
    )
    return pl.pallas_call(
        functools.partial(_flash_kernel, blk=blk, lead_pad=N_PAD),
        grid_spec=grid_spec,
        out_shape=jax.ShapeDtypeStruct((n_total, B_W), F32),
        compiler_params=_params(("parallel", "parallel", "arbitrary")),
        name="fox_flash",
    )(qi, kj, pb, pb, pb, fk)


def _fox_sample_kernel(q_ref, kn_ref, vn_ref, kc_ref, vc_ref, fc_ref, fn_ref, prev_ref, o_ref, *, ds):
    del prev_ref
    nh = B_HEADS
    stack = lambda ref: jnp.concatenate([ref[:, h * B_DH:(h + 1) * B_DH] for h in range(nh)], axis=0)
    q = stack(q_ref) * (B_DH ** -0.5 * LOG2E)
    rows = nh * ds
    n_c = kc_ref.shape[1]
    qh_c = lax.broadcasted_iota(jnp.int32, (rows, n_c), 0) // ds
    kh_c = lax.broadcasted_iota(jnp.int32, (rows, n_c), 1) % nh
    s_c = _dot3(q, kc_ref[0], _NT) - fc_ref[0]
    s_c = jnp.where(qh_c == kh_c, s_c, NEG)
    r_i = lax.broadcasted_iota(jnp.int32, (rows, rows), 0)
    c_i = lax.broadcasted_iota(jnp.int32, (rows, rows), 1)
    s_n = _dot3(q, stack(kn_ref), _NT) - fn_ref[0]
    s_n = jnp.where((r_i // ds == c_i // ds) & (c_i % ds <= r_i % ds), s_n, NEG)
    m = jnp.maximum(jnp.max(s_c, axis=1, keepdims=True), jnp.max(s_n, axis=1, keepdims=True))
    p_c = jnp.exp2(s_c - m)
    p_n = jnp.exp2(s_n - m)
    l = jnp.sum(p_c, axis=1, keepdims=True) + jnp.sum(p_n, axis=1, keepdims=True)
    o = (_dot3(p_c, vc_ref[0]) + _dot3(p_n, stack(vn_ref))) / l
    for h in range(nh):
        o_ref[:, h * B_DH:(h + 1) * B_DH] = o[h * ds:(h + 1) * ds]


def _fox_sample(pb, kc, vc, f_cache, f_new, prev_out, *, nb, ds, row_block0):
    n_c = kc.shape[1]
    base = (A_QK * 2 + A_V * 2) // B_W
    rb = lambda b: row_block0 + b
    return pl.pallas_call(
        functools.partial(_fox_sample_kernel, ds=ds),
        grid=(nb,),
        in_specs=[pl.BlockSpec((ds, B_W), lambda b: (rb(b), base)),
                  pl.BlockSpec((ds, B_W), lambda b: (rb(b), base + 1)),
                  pl.BlockSpec((ds, B_W), lambda b: (rb(b), base + 2)),
                  pl.BlockSpec((1, n_c, B_DH), lambda b: (b, 0, 0)),
                  pl.BlockSpec((1, n_c, B_DH), lambda b: (b, 0, 0)),
                  pl.BlockSpec((1, 1, n_c), lambda b: (b, 0, 0)),
                  pl.BlockSpec((1, 1, B_HEADS * ds), lambda b: (b, 0, 0)),
                  pl.BlockSpec(memory_space=pl.ANY)],
        out_specs=pl.BlockSpec((ds, B_W), lambda b: (rb(b), 0)),
        out_shape=jax.ShapeDtypeStruct(prev_out.shape, F32),
        input_output_aliases={7: 0},
        compiler_params=_params(("parallel",)),
        name="fox_sample",
    )(pb, pb, pb, kc, vc, f_cache, f_new, prev_out)


def _mlstm_kernel(q_ref, k_ref, v_ref, lir_ref, br_ref, lic_ref, bc_ref, c0_ref, n0_ref, m0_ref, *rest,
                  L, aliased):
    if aliased:
        rest = rest[1:]
    h_ref, cout_ref, nout_ref, mout_ref, c_scr, n_scr, m_scr = rest
    c = pl.program_id(1)

    @pl.when(c == 0)
    def _():
        c_scr[...] = c0_ref[0]
        n_scr[...] = n0_ref[0]
        m_scr[...] = m0_ref[0]

    nh = C_HEADS
    hl = nh * L
    seg = lax.broadcasted_iota(jnp.int32, (1, hl), 1) // L
    causal = (lax.broadcasted_iota(jnp.int32, (L, hl), 1) % L) <= lax.broadcasted_iota(jnp.int32, (L, hl), 0)

    def per_head(vals):
        out = vals[0]
        for h in range(1, nh):
            out = jnp.where(seg == h, vals[h], out)
        return out

    seg_max = lambda x, h: jnp.max(jnp.where(seg == h, x, -jnp.inf), axis=1, keepdims=True)

    qf = q_ref[...]
    kf = k_ref[...] * (C_DQK ** -0.5)
    qb = qf.astype(BF16)
    kb = kf.astype(BF16)
    vb = v_ref[...].astype(BF16)
    b_row = br_ref[0, 0]
    li_row = lir_ref[0, 0]
    b_col = [bc_ref[0, 0, :, h:h + 1] for h in range(nh)]
    li_col = [lic_ref[0, 0, :, h:h + 1] for h in range(nh)]
    m_prev = [m_scr[h:h + 1, :] for h in range(nh)]

    d = jnp.where(causal, per_head(b_col) - b_row + li_row, -jnp.inf)
    inter = [b_col[h] + m_prev[h] for h in range(nh)]
    m_t = [jnp.maximum(inter[h], seg_max(d, h)) for h in range(nh)]
    pm = jnp.exp(d - per_head(m_t))
    w_inter = [jnp.exp(inter[h] - m_t[h]) for h in range(nh)]

    khead = lax.broadcasted_iota(jnp.int32, (1, C_QK), 1) // C_DQK
    vhead = lax.broadcasted_iota(jnp.int32, (1, C_V), 1) // C_DV
    zero_b = jnp.zeros((), BF16)
    k_stack = jnp.concatenate([jnp.where(khead == h, kb, zero_b) for h in range(nh)], axis=0)
    v_bd = jnp.concatenate([jnp.where(vhead == h, vb, zero_b) for h in range(nh)], axis=0)
    sqk = _dot(qb, k_stack, _NT) * pm
    sv = _dot(sqk.astype(BF16), v_bd)

    b_last = [b_row[:, h * L + L - 1:h * L + L] for h in range(nh)]
    g_row = per_head(b_last) - b_row + li_row
    for h in range(nh):
        qk_sl = slice(h * C_DQK, (h + 1) * C_DQK)
        v_sl = slice(h * C_DV, (h + 1) * C_DV)
        c_prev = c_scr[h]
        n_prev = n_scr[h:h + 1, :]
        num = w_inter[h] * _dot(qb[:, qk_sl], c_prev.astype(BF16), _NT) + sv[:, v_sl]
        den = (w_inter[h] * jnp.sum(qf[:, qk_sl] * n_prev, axis=1, keepdims=True)
               + jnp.sum(jnp.where(seg == h, sqk, 0.0), axis=1, keepdims=True))
        h_ref[:, v_sl] = num / jnp.maximum(jnp.abs(den), jnp.exp(-m_t[h]))

        m_new = jnp.maximum(b_last[h] + m_prev[h], seg_max(g_row, h))
        w_c = jnp.exp(b_last[h] + m_prev[h] - m_new)
        kw = kf[:, qk_sl] * jnp.exp(b_last[h] - b_col[h] + li_col[h] - m_new)
        c_scr[h] = w_c * c_prev + _dot(vb[:, v_sl], kw.astype(BF16), _TN)
        n_scr[h:h + 1, :] = w_c * n_prev + jnp.sum(kw, axis=0, keepdims=True)
        m_scr[h:h + 1, :] = m_new

    @pl.when(c == pl.num_programs(1) - 1)
    def _():
        cout_ref[0] = c_scr[...]
        nout_ref[0] = n_scr[...]
        mout_ref[0] = m_scr[...]


def _mlstm(p, li_row, b_row, li_col, b_col, c0, n0, m0, *, nb, nc, L, row_block0, n_total, prev_out=None):
    aliased = prev_out is not None
    rb = lambda b, c: row_block0 + b * nc + c
    in_specs = [pl.BlockSpec((L, C_QK), lambda b, c: (rb(b, c), 0)),
                pl.BlockSpec((L, C_QK), lambda b, c: (rb(b, c), 1)),
                pl.BlockSpec((L, C_V), lambda b, c: (rb(b, c), 2 * C_QK // C_V)),
                pl.BlockSpec((1, 1, 1, C_HEADS * L), lambda b, c: (b, c, 0, 0)),
                pl.BlockSpec((1, 1, 1, C_HEADS * L), lambda b, c: (b, c, 0, 0)),
                pl.BlockSpec((1, 1, L, C_HEADS), lambda b, c: (b, c, 0, 0)),
                pl.BlockSpec((1, 1, L, C_HEADS), lambda b, c: (b, c, 0, 0)),
                pl.BlockSpec((1, C_HEADS, C_DV, C_DQK), lambda b, c: (b, 0, 0, 0)),
                pl.BlockSpec((1, C_HEADS, C_DQK), lambda b, c: (b, 0, 0)),
                pl.BlockSpec((1, C_HEADS, 1), lambda b, c: (b, 0, 0))]
    args = [p, p, p, li_row, b_row, li_col, b_col, c0, n0, m0]
    io_alias = {}
    if aliased:
        in_specs.append(pl.BlockSpec(memory_space=pl.ANY))
        args.append(prev_out)
        io_alias = {len(args) - 1: 0}
    return pl.pallas_call(
        functools.partial(_mlstm_kernel, L=L, aliased=aliased),
        grid=(nb, nc),
        in_specs=in_specs,
        out_specs=[pl.BlockSpec((L, C_V), lambda b, c: (rb(b, c), 0)),
                   pl.BlockSpec((1, C_HEADS, C_DV, C_DQK), lambda b, c: (b, 0, 0, 0)),
                   pl.BlockSpec((1, C_HEADS, C_DQK), lambda b, c: (b, 0, 0)),
                   pl.BlockSpec((1, C_HEADS, 1), lambda b, c: (b, 0, 0))],
        out_shape=[jax.ShapeDtypeStruct((n_total, C_V), F32),
                   jax.ShapeDtypeStruct((nb, C_HEADS, C_DV, C_DQK), F32),
                   jax.ShapeDtypeStruct((nb, C_HEADS, C_DQK), F32),
                   jax.ShapeDtypeStruct((nb, C_HEADS, 1), F32)],
        scratch_shapes=[pltpu.VMEM((C_HEADS, C_DV, C_DQK), F32),
                        pltpu.VMEM((C_HEADS, C_DQK), F32),
                        pltpu.VMEM((C_HEADS, 1), F32)],
        input_output_aliases=io_alias,
        compiler_params=_params(("parallel", "arbitrary")),
        name="mlstm_L%d" % L,
    )(*args)


def _mixout_kernel(*refs, hd, act, has_b, precise, n_prev):
    refs = list(refs)
    h_ref, a_ref, r_ref = refs[:3]
    b_ref = refs[3] if has_b else None
    k = 4 if has_b else 3
    ga_ref, wo_ref, gf_ref, wrh_ref, wrl_ref, br_ref, cnt0_ref = refs[k:k + 7]
    h1_ref, xn_ref, info_ref, cnt_ref, cnt_scr = refs[k + 7 + n_prev:]
    cast = (lambda x: x) if precise else (lambda x: x.astype(BF16))
    a = a_ref[...]
    r = r_ref[...]
    gate = r * _sigmoid(r) if act == "silu" else _sigmoid(r)
    parts = []
    for hh in range(a.shape[1] // hd):
        sl = slice(hh * hd, (hh + 1) * hd)
        parts.append(cast(_rms(a[:, sl], ga_ref[...]) * gate[:, sl]))
    if has_b:
        parts.append(cast(b_ref[...]))
    cat = jnp.concatenate(parts, axis=1)
    h1 = h_ref[...] + (_dot3(cat, wo_ref[...]) if precise else _dot(cat, wo_ref[...]))
    h1_ref[...] = h1
    xn = _rms(h1, gf_ref[...])
    xn_ref[...] = _pack_bf16_pairs(xn)
    xh, xl = _split(xn)
    logits = _dot(xh, wrh_ref[...]) + _dot(xh, wrl_ref[...]) + _dot(xl, wrh_ref[...]) + br_ref[...]

    lane = lax.broadcasted_iota(jnp.int32, logits.shape, 1)
    lanef = lane.astype(F32)
    is_g = lane < N_GROUPS
    gl = jnp.where(is_g, logits, -jnp.inf)
    gmax = jnp.max(gl, axis=1, keepdims=True)
    gidx = jnp.min(jnp.where(gl == gmax, lanef, float(LANES)), axis=1, keepdims=True)
    wg = 1.0 / jnp.sum(jnp.where(is_g, jnp.exp(gl - gmax), 0.0), axis=1, keepdims=True)
    lo = N_GROUPS + N_EXP * gidx
    el = jnp.where((lanef >= lo) & (lanef < lo + N_EXP), logits, -jnp.inf)
    m1 = jnp.max(el, axis=1, keepdims=True)
    i1 = jnp.min(jnp.where(el == m1, lanef, float(LANES)), axis=1, keepdims=True)
    el2 = jnp.where(lanef == i1, -jnp.inf, el)
    m2 = jnp.max(el2, axis=1, keepdims=True)
    i2 = jnp.min(jnp.where(el2 == m2, lanef, float(LANES)), axis=1, keepdims=True)
    t = jnp.exp(m2 - m1)
    w1 = wg / (1.0 + t)
    w2 = wg * t / (1.0 + t)
    e1 = i1 - N_GROUPS
    e2 = i2 - N_GROUPS

    @pl.when(pl.program_id(0) == 0)
    def _():
        cnt_scr[...] = cnt0_ref[...]

    tm = logits.shape[0]
    pick = jnp.where((lanef == e1) | (lanef == e2), 1.0, 0.0)
    earlier = (lax.broadcasted_iota(jnp.int32, (tm, tm), 1) < lax.broadcasted_iota(jnp.int32, (tm, tm), 0))
    before = _dot(jnp.where(earlier, 1.0, 0.0).astype(BF16), pick.astype(BF16)) + cnt_scr[...]
    r1 = jnp.sum(jnp.where(lanef == e1, before, 0.0), axis=1, keepdims=True)
    r2 = jnp.sum(jnp.where(lanef == e2, before, 0.0), axis=1, keepdims=True)
    cnt_new = cnt_scr[...] + jnp.sum(pick, axis=0, keepdims=True)
    cnt_scr[...] = cnt_new
    cnt_ref[...] = cnt_new
    info_ref[...] = jnp.where(lane == 0, e1, jnp.where(lane == 1, e2, jnp.where(lane == 2, w1, jnp.where(
        lane == 3, w2, jnp.where(lane == 4, r1, jnp.where(lane == 5, r2, 0.0))))))


def _mixout(h, a, a_col, r, r_col, b, g_head, w_o, g_ffn, wr_hi, wr_lo, b_r, counts0, *, hd, act,
            row0, n_rows, prev=None):
    n, d = h.shape
    precise = prev is not None
    tm = _tile(n_rows, 256, SUBLANES)
    assert row0 % tm == 0
    blk0 = row0 // tm
    wa = w_o.shape[0] if b is None else w_o.shape[0] - B_W
    has_b = b is not None
    row = lambda i: (blk0 + i, 0)
    const = lambda i: (0, 0)
    in_specs = [pl.BlockSpec((tm, d), row),
                pl.BlockSpec((tm, wa), lambda i: (blk0 + i, a_col)),
                pl.BlockSpec((tm, wa), lambda i: (blk0 + i, r_col))]
    args = [h, a, r]
    if has_b:
        in_specs.append(pl.BlockSpec((tm, B_W), row))
        args.append(b)
    in_specs += [pl.BlockSpec((1, hd), const), pl.BlockSpec(w_o.shape, const), pl.BlockSpec((1, d), const),
                 pl.BlockSpec((d, LANES), const), pl.BlockSpec((d, LANES), const), pl.BlockSpec((1, LANES), const),
                 pl.BlockSpec((1, LANES), const)]
    args += [g_head.reshape(1, hd), w_o if precise else w_o.astype(BF16), g_ffn.reshape(1, d), wr_hi, wr_lo, b_r,
             counts0]
    io_alias = {}
    if precise:
        for k, arr in enumerate(prev):
            in_specs.append(pl.BlockSpec(memory_space=pl.ANY))
            args.append(arr)
            io_alias[len(args) - 1] = k
    return pl.pallas_call(
        functools.partial(_mixout_kernel, hd=hd, act=act, has_b=has_b, precise=precise, n_prev=len(io_alias)),
        grid=(n_rows // tm,),
        in_specs=in_specs,
        out_specs=[pl.BlockSpec((tm, d), row), pl.BlockSpec((tm, d // 2), row), pl.BlockSpec((tm, LANES), row),
                   pl.BlockSpec((1, LANES), const)],
        out_shape=[jax.ShapeDtypeStruct((n, d), F32), jax.ShapeDtypeStruct((n, d // 2), jnp.int32),
                   jax.ShapeDtypeStruct((n, LANES), F32), jax.ShapeDtypeStruct((1, LANES), F32)],
        scratch_shapes=[pltpu.VMEM((1, LANES), F32)],
        input_output_aliases=io_alias,
        compiler_params=_params(("arbitrary",)),
        name="mixout_" + act + ("_precise" if precise else ""),
    )(*args)


def _sc_gather(table, idx):
    r = idx.shape[0]
    w = table.shape[1]
    n_workers = SC_CORES * SC_SUBCORES
    per_worker = r // n_workers
    step = SC_GATHER_ROWS * SC_GATHER_BUFS
    assert r % n_workers == 0 and per_worker % step == 0, (r, n_workers, step)
    mesh = plsc.VectorSubcoreMesh(core_axis_name="c", subcore_axis_name="s")

    @functools.partial(
        pl.kernel, mesh=mesh,
        out_type=jax.ShapeDtypeStruct((r, w), table.dtype),
        scratch_types=[pltpu.VMEM((SC_GATHER_BUFS, SC_GATHER_ROWS), jnp.int32),
                       pltpu.VMEM((SC_GATHER_BUFS, SC_GATHER_ROWS, w), table.dtype),
                       pltpu.SemaphoreType.DMA((SC_GATHER_BUFS,)),
                       pltpu.SemaphoreType.DMA((SC_GATHER_BUFS,))],
    )
    def gather(table_hbm, idx_hbm, out_hbm, idx_v, rows_v, gather_sem, store_sem):
        worker = lax.axis_index("s") * SC_CORES + lax.axis_index("c")
        base = worker * per_worker

        @pl.loop(0, per_worker // step)
        def _(j):
            off = pl.multiple_of(base + j * step, step)
            rows = lambda b: pl.ds(off + b * SC_GATHER_ROWS, SC_GATHER_ROWS)
            gathers, stores = [], []
            for b in range(SC_GATHER_BUFS):
                pltpu.sync_copy(idx_hbm.at[rows(b)], idx_v.at[b])
                gathers.append(pltpu.async_copy(table_hbm.at[idx_v.at[b]], rows_v.at[b], gather_sem.at[b]))
            for b in range(SC_GATHER_BUFS):
                gathers[b].wait()
                stores.append(pltpu.async_copy(rows_v.at[b], out_hbm.at[rows(b)], store_sem.at[b]))
            for b in range(SC_GATHER_BUFS):
                stores[b].wait()

    return gather(table, idx)


def _expert_kernel(te_ref, nu_ref, x_ref, wg_ref, wu_ref, wd_ref, y_ref, wgb, wub, wdb):
    i = pl.program_id(0)
    live = i < nu_ref[0]
    new_expert = (i == 0) | (te_ref[i] != te_ref[jnp.maximum(i - 1, 0)])

    @pl.when(live & new_expert)
    def _():
        wgb[...] = wg_ref[0].astype(BF16)
        wub[...] = wu_ref[0].astype(BF16)
        wdb[...] = wd_ref[0].astype(BF16)

    @pl.when(live)
    def _():
        x = _unpack_bf16_pairs(x_ref[...]).astype(BF16)
        g = _dot(x, wgb[...])
        u = _dot(x, wub[...])
        y_ref[...] = _pack_bf16_pairs(_dot((g * _sigmoid(g) * u).astype(BF16), wdb[...]))

    @pl.when(jnp.logical_not(live))
    def _():
        y_ref[...] = jnp.zeros_like(y_ref)


def _experts(xs, tile_expert, n_used, w_gate, w_up, w_down, *, tm):
    n_slots, dp = xs.shape
    d, f = w_gate.shape[-2:]
    grid_spec = pltpu.PrefetchScalarGridSpec(
        num_scalar_prefetch=2,
        grid=(n_slots // tm,),
        in_specs=[pl.BlockSpec((tm, dp), lambda i, te, nu: (i, 0)),
                  pl.BlockSpec((1, d, f), lambda i, te, nu: (te[i], 0, 0)),
                  pl.BlockSpec((1, d, f), lambda i, te, nu: (te[i], 0, 0)),
                  pl.BlockSpec((1, f, d), lambda i, te, nu: (te[i], 0, 0))],
        out_specs=pl.BlockSpec((tm, dp), lambda i, te, nu: (i, 0)),
        scratch_shapes=[pltpu.VMEM((d, f), BF16), pltpu.VMEM((d, f), BF16), pltpu.VMEM((f, d), BF16)],
    )
    return pl.pallas_call(
        _expert_kernel,
        grid_spec=grid_spec,
        out_shape=jax.ShapeDtypeStruct((n_slots, dp), jnp.int32),
        compiler_params=_params(("arbitrary",)),
        name="moe_experts",
    )(tile_expert, n_used, xs, w_gate, w_up, w_down)


def _combine_kernel(h_ref, info_ref, g_ref, y0_ref, y1_ref, o_ref, *, final_norm):
    h2 = h_ref[...] + (info_ref[:, 2:3] * _unpack_bf16_pairs(y0_ref[...])
                       + info_ref[:, 3:4] * _unpack_bf16_pairs(y1_ref[...]))
    o_ref[...] = _rms(h2, g_ref[...]) if final_norm else h2


def _combine(h, yg, info, g, *, final_norm):
    n, d = h.shape
    tm = _tile(n, 512, SUBLANES)
    nt = n // tm
    return pl.pallas_call(
        functools.partial(_combine_kernel, final_norm=final_norm),
        grid=(nt,),
        in_specs=[pl.BlockSpec((tm, d), lambda i: (i, 0)),
                  pl.BlockSpec((tm, LANES), lambda i: (i, 0)),
                  pl.BlockSpec((1, d), lambda i: (0, 0)),
                  pl.BlockSpec((tm, d // 2), lambda i: (i, 0)),
                  pl.BlockSpec((tm, d // 2), lambda i: (nt + i, 0))],
        out_specs=pl.BlockSpec((tm, d), lambda i: (i, 0)),
        out_shape=jax.ShapeDtypeStruct((n, d), F32),
        compiler_params=_params(("parallel",)),
        name="moe_combine",
    )(h, info, g.reshape(1, d), yg, yg)


def _moe(h1, xn, info, counts_f, w_gate, w_up, w_down, expert_base, g_next, *, final_norm):
    n, d = h1.shape
    n_e = N_GROUPS * N_EXP
    tm = _tile(2 * n, 256, SUBLANES)
    n_tiles = (2 * n) // tm + n_e
    n_slots = n_tiles * tm
    experts = jnp.arange(n_e, dtype=jnp.int32)
    counts = counts_f[0, :n_e].astype(jnp.int32)
    starts = jnp.cumsum(counts) - counts
    padded = ((counts + tm - 1) // tm) * tm
    pad_ends = jnp.cumsum(padded)
    pad_starts = pad_ends - padded
    n_used = (pad_ends[-1] // tm).astype(jnp.int32)
    eid = info[:, 0:2].astype(jnp.int32)
    rank = info[:, 4:6].astype(jnp.int32)
    lookup = lambda table, e: jnp.sum(jnp.where(e[..., None] == experts, table, 0), axis=-1)
    slot_of_pick = lookup(pad_starts, eid) + rank
    tok = jnp.broadcast_to(jnp.arange(n, dtype=jnp.int32)[:, None], (n, 2)).reshape(-1)
    _, tok_sorted = lax.sort_key_val(slot_of_pick.reshape(-1), tok)
    slots = jnp.arange(n_slots, dtype=jnp.int32)
    e_of_slot = jnp.minimum(jnp.sum(slots[:, None] >= pad_ends[None, :], axis=1), n_e - 1).astype(jnp.int32)
    r_of_slot = slots - lookup(pad_starts, e_of_slot)
    live = r_of_slot < lookup(counts, e_of_slot)
    src_token = jnp.where(live, tok_sorted[jnp.clip(lookup(starts, e_of_slot) + r_of_slot, 0, 2 * n - 1)], slots % n)
    tile_expert = (expert_base + e_of_slot[::tm]).astype(jnp.int32)

    xs = _sc_gather(xn, src_token.astype(jnp.int32))
    ys = _experts(xs, tile_expert, n_used.reshape(1), w_gate, w_up, w_down, tm=tm)
    yg = _sc_gather(ys, slot_of_pick.T.reshape(-1))
    return _combine(h1, yg, info, g_next, final_norm=final_norm)


def _router_weights(w_rg, b_rg, w_re, b_re):
    d = w_rg.shape[0]
    pad = LANES - N_GROUPS - N_GROUPS * N_EXP
    w = jnp.concatenate([w_rg, w_re, jnp.zeros((d, pad), F32)], axis=1)
    b = jnp.concatenate([b_rg, b_re, jnp.zeros((pad,), F32)]).reshape(1, LANES)
    hi = w.astype(BF16)
    lo = (w - hi.astype(F32)).astype(BF16)
    return hi, lo, b


def _rows_to_lanes(x, nb, t, nch):
    return x.reshape(nb, t, nch).transpose(0, 2, 1).reshape(nb * nch, t)


def _even_layer(h, dims, g_mix, w_in, w_g2, b_g, b_f, g_a, w_o, state_gla, ck, cv, clf):
    nb, t, db, ds, npr, n = dims
    d = h.shape[1]
    qa, ka, va, ra, ga, qb, kb, vb, fb = jnp.split(
        w_in, [A_QK, 2 * A_QK, 2 * A_QK + A_V, 2 * A_QK + 2 * A_V, 2 * A_QK + 2 * A_V + A_RANK,
               2 * A_QK + 2 * A_V + A_RANK + B_W, 2 * A_QK + 2 * A_V + A_RANK + 2 * B_W,
               2 * A_QK + 2 * A_V + A_RANK + 3 * B_W], axis=1)
    w_packed = jnp.concatenate(
        [qa, ka, va, ra, qb, kb, vb, ga, fb, jnp.zeros((d, LANES - A_RANK - B_HEADS), F32)], axis=1)
    q_off = 2 * A_QK + 2 * A_V
    colscale = jnp.ones((1, MAIN_W), F32).at[:, q_off:q_off + B_W].set(B_DH ** -0.5 * LOG2E)
    p, pb, k_rows, v_rows = _proj(h, g_mix, w_packed, n_rows=npr, colscale=colscale, kv_col=q_off + B_W)
    p = _proj_precise(h, g_mix, w_packed, p, row0=npr)

    nh = B_HEADS
    fcol = MAIN_W + A_RANK
    zeros_col = lambda r: jnp.zeros((r, 1), F32)
    bias_row = lambda lanes: jnp.tile(b_f, lanes // nh).reshape(1, lanes)
    fb_p = p[:npr, fcol:fcol + nh].reshape(nb, t * nh)
    logf_p, f_p = _gate_scan(fb_p, bias_row(t * nh), zeros_col(nb), mode="fox", act_start=0,
                             valid_start=N_PAD * nh, valid_end=t * nh, seg=None, stride=nh)
    past = ck.shape[1]
    n_c = past * nh
    x_s = _pad_lanes(jnp.concatenate([clf.reshape(db, n_c), p[npr:, fcol:fcol + nh].reshape(db, ds * nh)], axis=1))
    logf_s, f_s = _gate_scan(x_s, bias_row(x_s.shape[1]), zeros_col(db), mode="fox", act_start=n_c,
                             valid_start=0, valid_end=n_c + ds * nh, seg=None, stride=nh)

    nc = t // CHUNK
    oa, s_p = _gla(p, jnp.zeros((nb, A_HEADS, A_DK, A_DV), F32), w_g2, b_g,
                   nb=nb, nc=nc, L=CHUNK, row_block0=0, lead_pad=N_PAD, n_total=n)
    oa, s_s = _gla(p, state_gla, w_g2, b_g, nb=db, nc=1, L=ds, row_block0=npr // ds, lead_pad=0,
                   n_total=n, prev_out=oa, precise=True)

    fk = (f_p * LOG2E).reshape(nb, t, nh).transpose(0, 2, 1).reshape(nb, nh, 1, t)
    ob = _flash(pb, fk, nb=nb, t=t, n_total=n)
    f_cache = (f_s[:, :n_c] * LOG2E).reshape(db, 1, n_c)
    f_new = (f_s[:, n_c:n_c + ds * nh] * LOG2E).reshape(db, ds, nh).transpose(0, 2, 1).reshape(db, 1, nh * ds)
    ob = _fox_sample(p, ck.reshape(db, n_c, B_DH), cv.reshape(db, n_c, B_DH), f_cache, f_new, ob,
                     nb=db, ds=ds, row_block0=npr // ds)

    kcol = q_off + B_W
    states = dict(
        s_p=s_p, s_s=s_s,
        k_p=k_rows.reshape(nb, t, B_HEADS, B_DH)[:, N_PAD:],
        v_p=v_rows.reshape(nb, t, B_HEADS, B_DH)[:, N_PAD:],
        f_p=logf_p.reshape(nb, t, nh)[:, N_PAD:],
        k_s=p[npr:, kcol:kcol + B_W].reshape(db, ds, B_HEADS, B_DH),
        v_s=p[npr:, kcol + B_W:kcol + 2 * B_W].reshape(db, ds, B_HEADS, B_DH),
        f_s=logf_s[:, n_c:n_c + ds * nh].reshape(db, ds, nh))
    return (oa, 0, p, (2 * A_QK + A_V) // A_V, ob, g_a, w_o), states


def _chunk_rows(x, nb, nch, nc, L):
    x = x[:, :nc * L].reshape(nb, nch, nc, L)
    return x.transpose(0, 2, 1, 3).reshape(nb, nc, 1, nch * L), x.transpose(0, 2, 3, 1)


def _odd_layer(h, dims, g_mix, w_in, b_gate, g_c, w_o, c0, n0, m0):
    nb, t, db, ds, npr, n = dims
    d = h.shape[1]
    w_packed = jnp.concatenate(
        [w_in, jnp.zeros((d, LANES - 2 * C_HEADS), F32)], axis=1)
    (p,) = _proj(h, g_mix, w_packed, n_rows=npr)
    p = _proj_precise(h, g_mix, w_packed, p, row0=npr)

    ng = 2 * C_HEADS
    isf = (jnp.arange(ng) >= C_HEADS).astype(F32)
    nc = t // CHUNK

    def gates(rows, nbatch, tt, valid_start, seg):
        x = _pad_lanes(_rows_to_lanes(rows, nbatch, tt, ng))
        val, cum = _gate_scan(x, jnp.tile(b_gate, nbatch).reshape(-1, 1), jnp.tile(isf, nbatch).reshape(-1, 1),
                              mode="mlstm", act_start=0, valid_start=valid_start, valid_end=tt, seg=seg)
        return val.reshape(nbatch, ng, -1), cum.reshape(nbatch, ng, -1)

    val_p, cum_p = gates(p[:npr, MAIN_W:MAIN_W + ng], nb, t, N_PAD, CHUNK)
    val_s, cum_s = gates(p[npr:, MAIN_W:MAIN_W + ng], db, ds, 0, ds)

    def chunked(val, cum, nbatch, ncs, L):
        li_row, li_col = _chunk_rows(val[:, :C_HEADS].reshape(nbatch * C_HEADS, -1), nbatch, C_HEADS, ncs, L)
        b_row, b_col = _chunk_rows(cum[:, C_HEADS:].reshape(nbatch * C_HEADS, -1), nbatch, C_HEADS, ncs, L)
        return li_row, b_row, li_col, b_col

    zc = jnp.zeros((nb, C_HEADS, C_DV, C_DQK), F32)
    zn = jnp.zeros((nb, C_HEADS, C_DQK), F32)
    zm = jnp.zeros((nb, C_HEADS, 1), F32)
    hm, c_p, n_p, m_p = _mlstm(p, *chunked(val_p, cum_p, nb, nc, CHUNK), zc, zn, zm,
                               nb=nb, nc=nc, L=CHUNK, row_block0=0, n_total=n)
    hm, c_s, n_s, m_s = _mlstm(p, *chunked(val_s, cum_s, db, 1, ds), c0, n0, m0.reshape(db, C_HEADS, 1),
                               nb=db, nc=1, L=ds, row_block0=npr // ds, n_total=n, prev_out=hm)
    states = dict(c_p=c_p, n_p=n_p, m_p=m_p.reshape(nb, C_HEADS), c_s=c_s, n_s=n_s, m_s=m_s.reshape(db, C_HEADS))
    return (hm, 0, p, (2 * C_QK + C_V) // C_V, None, g_c, w_o), states


def kernel(x_prompt, x_sample, state_gla, cache_fox_k, cache_fox_v, cache_fox_logf, state_mlstm_c, state_mlstm_n, state_mlstm_m, meta_tokens, norm_mix, norm_ffn, norm_final, w_in_even, w_gla_gate2, b_gla_gate, b_fox_f, g_gla_out, w_out_even, w_in_odd, b_mlstm_gate, g_mlstm_out, w_out_odd, w_router_group, b_router_group, w_router_expert, b_router_expert, w_exp_gate, w_exp_up, w_exp_down):
    nb, seq, d = x_prompt.shape
    db, ds, _ = x_sample.shape
    t = LEAD + seq
    npr, nsm = nb * t, db * ds
    n = npr + nsm
    dims = (nb, t, db, ds, npr, n)
    depth = norm_mix.shape[0]
    n_e = N_GROUPS * N_EXP
    f = w_exp_gate.shape[-1]

    pad_rows = jnp.zeros((N_PAD, d), F32)
    h = jnp.concatenate([piece for b in range(nb) for piece in (pad_rows, meta_tokens, x_prompt[b])]
                        + [x_sample.reshape(nsm, d)], axis=0)
    wg_all = w_exp_gate.reshape(depth * n_e, d, f)
    wu_all = w_exp_up.reshape(depth * n_e, d, f)
    wd_all = w_exp_down.reshape(depth * n_e, f, d)

    even, odd = [], []
    for l in range(depth):
        if l % 2 == 0:
            e = l // 2
            mix, st = _even_layer(h, dims, norm_mix[l], w_in_even[e], w_gla_gate2[e], b_gla_gate[e], b_fox_f[e],
                                  g_gla_out[e], w_out_even[e], state_gla[e], cache_fox_k[e], cache_fox_v[e],
                                  cache_fox_logf[e])
            even.append(st)
            hd, act = A_DV, "silu"
        else:
            o = l // 2
            mix, st = _odd_layer(h, dims, norm_mix[l], w_in_odd[o], b_mlstm_gate[o], g_mlstm_out[o], w_out_odd[o],
                                 state_mlstm_c[o], state_mlstm_n[o], state_mlstm_m[o])
            odd.append(st)
            hd, act = C_DV, "sigmoid"
        a, a_col, r, r_col, b, g_head, w_o = mix
        wr_hi, wr_lo, b_r = _router_weights(w_router_group[l], b_router_group[l], w_router_expert[l],
                                            b_router_expert[l])
        mix_args = (h, a, a_col, r, r_col, b, g_head, w_o, norm_ffn[l], wr_hi, wr_lo, b_r)
        h1, xn, info, counts = _mixout(*mix_args, jnp.zeros((1, LANES), F32), hd=hd, act=act, row0=0, n_rows=npr)
        h1, xn, info, counts = _mixout(*mix_args, counts, hd=hd, act=act, row0=npr, n_rows=nsm,
                                       prev=(h1, xn, info))
        last = l == depth - 1
        h = _moe(h1, xn, info, counts, wg_all, wu_all, wd_all, l * n_e, norm_final if last else norm_ffn[l],
                 final_norm=last)

    y_prompt = h[:npr].reshape(nb, t, d)[:, LEAD:]
    y_sample = h[npr:].reshape(db, ds, d)
    stack = lambda sts, key: jnp.stack([s[key] for s in sts])
    return (y_prompt, y_sample,
            stack(even, "s_p"), stack(even, "k_p"), stack(even, "v_p"), stack(even, "f_p"),
            stack(odd, "c_p"), stack(odd, "n_p"), stack(odd, "m_p"),
            stack(even, "s_s"), stack(even, "k_s"), stack(even, "v_s"), stack(even, "f_s"),
            stack(odd, "c_s"), stack(odd, "n_s"), stack(odd, "m_s"))
```

```python
import functools
import math

import jax
import jax.numpy as jnp
from jax import lax
from jax.experimental import pallas as pl
from jax.experimental.pallas import tpu as pltpu
from jax.experimental.pallas import tpu_sc as plsc

F32 = jnp.float32
BF16 = jnp.bfloat16

CHUNK = 64
N_META = 16
LEAD = 128
N_PAD = LEAD - N_META
A_HEADS, A_DK, A_DV, A_RANK = 4, 64, 128, 16
A_GATE_NORM = 16.0
B_HEADS, B_DH = 4, 128
C_HEADS, C_DQK, C_DV = 4, 128, 256
GATE_CAP = 15.0
N_GROUPS, N_EXP = 4, 8
EPS = 1e-6
NEG = -1e30
LOG2E = 1.4426950408889634
A_QK = A_HEADS * A_DK
A_V = A_HEADS * A_DV
B_W = B_HEADS * B_DH
C_QK = C_HEADS * C_DQK
C_V = C_HEADS * C_DV

LANES = 128
SUBLANES = 8
VMEM_LIMIT_BYTES = 56 * 1024 * 1024
GLA_SUB = 16
SC_CORES, SC_SUBCORES = 2, 16
SC_GATHER_ROWS = 16
SC_GATHER_BUFS = 4
FLASH_HEADS = 2
MAIN_W = 3072
PROJ_W = MAIN_W + LANES

_NT = (((1,), (1,)), ((), ()))
_TN = (((0,), (0,)), ((), ()))
_NN = (((1,), (0,)), ((), ()))


def _params(sem):
    return pltpu.CompilerParams(dimension_semantics=sem, vmem_limit_bytes=VMEM_LIMIT_BYTES)


def _tile(n, pref, mult):
    t = (min(pref, n) // mult) * mult
    while t > mult and n % t:
        t -= mult
    assert t >= mult and n % t == 0, (n, pref, mult)
    return t


def _dot(a, b, dims=_NN):
    return lax.dot_general(a, b, dims, preferred_element_type=F32)


def _split(x):
    hi = x.astype(BF16)
    lo = (x - hi.astype(F32)).astype(BF16)
    return hi, lo


def _dot3(a, b, dims=_NN):
    ah, al = _split(a)
    bh, bl = _split(b)
    return _dot(ah, bh, dims) + _dot(ah, bl, dims) + _dot(al, bh, dims)


def _log_sigmoid(x):
    return jnp.minimum(x, 0.0) - jnp.log1p(jnp.exp(-jnp.abs(x)))


def _sigmoid(x):
    return 1.0 / (1.0 + jnp.exp(-x))


def _rms(x, g):
    return x * lax.rsqrt(jnp.mean(x * x, axis=-1, keepdims=True) + EPS) * g


def _pack_bf16_pairs(x):
    w = x.shape[1] // 2
    hi = lax.bitcast_convert_type(x[:, :w].astype(BF16).astype(F32), jnp.int32)
    lo = lax.bitcast_convert_type(x[:, w:].astype(BF16).astype(F32), jnp.int32)
    return hi | lax.shift_right_logical(lo, 16)


def _unpack_bf16_pairs(p):
    hi = lax.bitcast_convert_type(p & jnp.int32(-65536), F32)
    lo = lax.bitcast_convert_type(lax.shift_left(p, 16), F32)
    return jnp.concatenate([hi, lo], axis=1)


def _cumsum_rows(x):
    n = x.shape[0]
    row = lax.broadcasted_iota(jnp.int32, x.shape, 0)
    s = 1
    while s < n:
        x = x + jnp.where(row >= s, pltpu.roll(x, s, axis=0), 0.0)
        s *= 2
    return x


def _proj_kernel(x_ref, g_ref, w_ref, *rest, col_chunk, kv_col):
    if kv_col is None:
        (o_ref,) = rest
    else:
        cs_ref, o_ref, ob_ref, *kv_refs = rest
    tm = x_ref.shape[0]
    xn = _rms(x_ref[...], g_ref[...]).astype(BF16)
    for c0 in range(0, PROJ_W, col_chunk):
        c1 = min(c0 + col_chunk, PROJ_W)
        y = _dot(xn, w_ref[:, c0:c1])
        o_ref[:, c0:c1] = y
        if kv_col is not None:
            if c0 < MAIN_W:
                m1 = min(c1, MAIN_W)
                ob_ref[:, c0:m1] = (y[:, :m1 - c0] * cs_ref[:, c0:m1]).astype(BF16)
            for g0 in range(c0, c1, LANES):
                rel = g0 - kv_col
                if 0 <= rel < 2 * B_W:
                    head = (rel % B_W) // B_DH
                    kv_refs[rel // B_W][pl.ds(head, tm, stride=B_HEADS), :] = y[:, g0 - c0:g0 - c0 + LANES]


def _proj_precise_kernel(x_ref, g_ref, w_ref, prev_ref, o_ref, *, col_chunk):
    del prev_ref
    xn = _rms(x_ref[...], g_ref[...])
    for c0 in range(0, PROJ_W, col_chunk):
        c1 = min(c0 + col_chunk, PROJ_W)
        o_ref[:, c0:c1] = _dot3(xn, w_ref[:, c0:c1])


def _proj(h, g, w_packed, *, n_rows, colscale=None, kv_col=None):
    n, d = h.shape
    tm = _tile(n_rows, 512, 16)
    in_specs = [pl.BlockSpec((tm, d), lambda i: (i, 0)),
                pl.BlockSpec((1, d), lambda i: (0, 0)),
                pl.BlockSpec((d, PROJ_W), lambda i: (0, 0))]
    args = [h, g.reshape(1, d), w_packed.astype(BF16)]
    out_specs = [pl.BlockSpec((tm, PROJ_W), lambda i: (i, 0))]
    out_shape = [jax.ShapeDtypeStruct((n, PROJ_W), F32)]
    if kv_col is not None:
        in_specs.append(pl.BlockSpec((1, MAIN_W), lambda i: (0, 0)))
        args.append(colscale)
        out_specs += [pl.BlockSpec((tm, MAIN_W), lambda i: (i, 0))] + [pl.BlockSpec((tm * B_HEADS, B_DH), lambda i: (i, 0))] * 2
        out_shape += ([jax.ShapeDtypeStruct((n_rows, MAIN_W), BF16)]
                      + [jax.ShapeDtypeStruct((n_rows * B_HEADS, B_DH), F32)] * 2)
    return pl.pallas_call(
        functools.partial(_proj_kernel, col_chunk=640, kv_col=kv_col),
        grid=(n_rows // tm,),
        in_specs=in_specs,
        out_specs=out_specs,
        out_shape=out_shape,
        compiler_params=_params(("parallel",)),
        name="proj",
    )(*args)


def _proj_precise(h, g, w_packed, prev, *, row0):
    n, d = h.shape
    tm = _tile(n - row0, 512, SUBLANES)
    assert row0 % tm == 0
    return pl.pallas_call(
        functools.partial(_proj_precise_kernel, col_chunk=640),
        grid=((n - row0) // tm,),
        in_specs=[pl.BlockSpec((tm, d), lambda i: (row0 // tm + i, 0)),
                  pl.BlockSpec((1, d), lambda i: (0, 0)),
                  pl.BlockSpec((d, PROJ_W), lambda i: (0, 0)),
                  pl.BlockSpec(memory_space=pl.ANY)],
        out_specs=pl.BlockSpec((tm, PROJ_W), lambda i: (row0 // tm + i, 0)),
        out_shape=jax.ShapeDtypeStruct((n, PROJ_W), F32),
        input_output_aliases={3: 0},
        compiler_params=_params(("parallel",)),
        name="proj_precise",
    )(h, g.reshape(1, d), w_packed, prev)


def _gate_scan_kernel(x_ref, bias_ref, isf_ref, val_ref, cum_ref, *, mode, act_start, valid_start, valid_end, seg, stride):
    x = x_ref[...]
    lane = lax.broadcasted_iota(jnp.int32, x.shape, 1)
    valid = (lane >= valid_start) & (lane < valid_end)
    if mode == "fox":
        val = jnp.where(lane >= act_start, _log_sigmoid(x + bias_ref[...]), x)
        val = jnp.where(valid, val, 0.0)
        add = val
    else:
        gate = GATE_CAP * jnp.tanh((x + bias_ref[...]) / GATE_CAP)
        isf = isf_ref[...] > 0.5
        val = jnp.where(isf, jnp.where(valid, _log_sigmoid(gate), 0.0),
                        jnp.where(valid, gate, -jnp.inf))
        add = jnp.where(isf, val, 0.0)
    val_ref[...] = val
    n = x.shape[1]
    pos = lane if seg is None else lane % seg
    limit = n if seg is None else seg
    s = stride
    while s < limit:
        add = add + jnp.where(pos >= s, pltpu.roll(add, s, axis=1), 0.0)
        s *= 2
    cum_ref[...] = add


def _gate_scan(x, bias, isf, *, mode, act_start, valid_start, valid_end, seg, stride=1):
    r, n = x.shape
    full = lambda shape: pl.BlockSpec(shape, lambda i: (0,) * len(shape))
    return pl.pallas_call(
        functools.partial(_gate_scan_kernel, mode=mode, act_start=act_start,
                          valid_start=valid_start, valid_end=valid_end, seg=seg, stride=stride),
        grid=(1,),
        in_specs=[full((r, n)), full(bias.shape), full((r, 1))],
        out_specs=[full((r, n)), full((r, n))],
        out_shape=[jax.ShapeDtypeStruct((r, n), F32)] * 2,
        compiler_params=_params(("arbitrary",)),
        name="gate_scan_" + mode,
    )(x, bias, isf)


def _pad_lanes(x):
    n = x.shape[-1]
    m = -(-n // LANES) * LANES
    return x if m == n else jnp.pad(x, ((0, 0), (0, m - n)))


def _gla_kernel(qk_ref, v_ref, sm_ref, s0_ref, wg2_ref, wg2t_ref, bgr_ref, bgc_ref, *rest,
                L, sub, lead_pad, aliased, precise):
    if aliased:
        rest = rest[1:]
    o_ref, sout_ref, s_scr = rest
    c = pl.program_id(1)
    nh, dk, dv = A_HEADS, A_DK, A_DV
    cast = (lambda x: x) if precise else (lambda x: x.astype(BF16))
    mm = _dot3 if precise else _dot

    @pl.when(c == 0)
    def _():
        s_scr[...] = jnp.zeros_like(s_scr)
        for h in range(nh):
            s_scr[h * dk:(h + 1) * dk, h * dv:(h + 1) * dv] = s0_ref[0, h]

    qk = qk_ref[...]
    q = qk[:, :A_QK] * (A_DK ** -0.5)
    k = qk[:, A_QK:]
    v = v_ref[...]
    ga = sm_ref[:, :A_RANK]
    row = lax.broadcasted_iota(jnp.int32, (L, 1), 0)
    valid = (c * L + row) >= lead_pad
    z = _dot3(ga, wg2_ref[...]) + bgr_ref[...]
    loga = jnp.where(valid, _log_sigmoid(z) / A_GATE_NORM, 0.0)
    k = jnp.where(valid, k, 0.0)
    b = _cumsum_rows(loga)
    b_last = b[L - 1:L, :]
    lane_t = lax.broadcasted_iota(jnp.int32, (1, L), 1)
    zt = _dot3(wg2t_ref[...], ga, _NT) + bgc_ref[...]
    logat = jnp.where((c * L + lane_t) >= lead_pad, _log_sigmoid(zt) / A_GATE_NORM, 0.0)
    b_last_col = jnp.sum(logat, axis=1, keepdims=True)

    qhead = lax.broadcasted_iota(jnp.int32, (1, A_QK), 1) // dk
    vhead = lax.broadcasted_iota(jnp.int32, (1, A_V), 1) // dv
    vb = cast(v)
    zero_b = jnp.zeros((), vb.dtype)
    v_bd = jnp.concatenate([jnp.where(vhead == h, vb, zero_b) for h in range(nh)], axis=0)

    rows_all = lax.broadcasted_iota(jnp.int32, (L, 1), 0)
    a_rows = []
    for i in range(L // sub):
        r0 = i * sub
        ci = jnp.zeros((1, A_QK), F32) if i == 0 else b[r0 - 1:r0, :]
        qt = cast(q[r0:r0 + sub] * jnp.exp(b[r0:r0 + sub] - ci))
        kt = cast(jnp.where(rows_all < r0 + sub, k * jnp.exp(ci - b), 0.0))
        k_stack = jnp.concatenate([jnp.where(qhead == h, kt, zero_b) for h in range(nh)], axis=0)
        a_rows.append(mm(qt, k_stack, _NT))
    a = a_rows[0] if len(a_rows) == 1 else jnp.concatenate(a_rows, axis=0)
    t_idx = lax.broadcasted_iota(jnp.int32, (L, nh * L), 0)
    s_idx = lax.broadcasted_iota(jnp.int32, (L, nh * L), 1) % L
    a = jnp.where(s_idx <= t_idx, a, 0.0)
    o_intra = mm(cast(a), v_bd)

    s_full = s_scr[...]
    o_inter = mm(cast(q * jnp.exp(b)), cast(s_full))
    o_ref[...] = o_inter + o_intra

    k_hat = cast(k * jnp.exp(b_last - b))
    upd = mm(k_hat, vb, _TN)
    khead_col = lax.broadcasted_iota(jnp.int32, (A_QK, 1), 0) // dk
    s_new = jnp.exp(b_last_col) * s_full + jnp.where(khead_col == vhead, upd, 0.0)
    s_scr[...] = s_new

    @pl.when(c == pl.num_programs(1) - 1)
    def _():
        for h in range(nh):
            sout_ref[0, h] = s_new[h * dk:(h + 1) * dk, h * dv:(h + 1) * dv]


def _gla(p, s0, wg2, bg, *, nb, nc, L, row_block0, lead_pad, n_total, prev_out=None, precise=False):
    aliased = prev_out is not None
    rb = lambda b, c: row_block0 + b * nc + c
    in_specs = [pl.BlockSpec((L, 2 * A_QK), lambda b, c: (rb(b, c), 0)),
                pl.BlockSpec((L, A_V), lambda b, c: (rb(b, c), 2 * A_QK // A_V)),
                pl.BlockSpec((L, LANES), lambda b, c: (rb(b, c), MAIN_W // LANES)),
                pl.BlockSpec((1, A_HEADS, A_DK, A_DV), lambda b, c: (b, 0, 0, 0)),
                pl.BlockSpec((A_RANK, A_QK), lambda b, c: (0, 0)),
                pl.BlockSpec((A_QK, A_RANK), lambda b, c: (0, 0)),
                pl.BlockSpec((1, A_QK), lambda b, c: (0, 0)),
                pl.BlockSpec((A_QK, 1), lambda b, c: (0, 0))]
    args = [p, p, p, s0, wg2, wg2.T, bg.reshape(1, A_QK), bg.reshape(A_QK, 1)]
    io_alias = {}
    if aliased:
        in_specs.append(pl.BlockSpec(memory_space=pl.ANY))
        args.append(prev_out)
        io_alias = {len(args) - 1: 0}
    return pl.pallas_call(
        functools.partial(_gla_kernel, L=L, sub=min(GLA_SUB, L), lead_pad=lead_pad, aliased=aliased,
                          precise=precise),
        grid=(nb, nc),
        in_specs=in_specs,
        out_specs=[pl.BlockSpec((L, A_V), lambda b, c: (rb(b, c), 0)),
                   pl.BlockSpec((1, A_HEADS, A_DK, A_DV), lambda b, c: (b, 0, 0, 0))],
        out_shape=[jax.ShapeDtypeStruct((n_total, A_V), F32),
                   jax.ShapeDtypeStruct((nb, A_HEADS, A_DK, A_DV), F32)],
        scratch_shapes=[pltpu.VMEM((A_QK, A_V), F32)],
        input_output_aliases=io_alias,
        compiler_params=_params(("parallel", "arbitrary")),
        name="gla_L%d" % L,
    )(*args)


def _flash_kernel(qi_ref, kj_ref, q_ref, k_ref, v_ref, fk_ref, o_ref, m_scr, l_scr, acc_scr, *, blk, lead_pad):
    step = pl.program_id(2)
    i = qi_ref[step]
    j = kj_ref[step]

    @pl.when(j == 0)
    def _():
        m_scr[...] = jnp.full_like(m_scr, -jnp.inf)
        l_scr[...] = jnp.zeros_like(l_scr)
        acc_scr[...] = jnp.zeros_like(acc_scr)

    def update(masked):
        for g in range(FLASH_HEADS):
            sl = slice(g * B_DH, (g + 1) * B_DH)
            s = _dot(q_ref[:, sl], k_ref[:, sl], _NT) - fk_ref[0, g]
            if masked:
                qpos = i * blk + lax.broadcasted_iota(jnp.int32, (blk, blk), 0)
                kpos = j * blk + lax.broadcasted_iota(jnp.int32, (blk, blk), 1)
                s = jnp.where((kpos <= qpos) & (kpos >= lead_pad), s, NEG)
            m_prev = m_scr[g]
            m_new = jnp.maximum(m_prev, jnp.max(s, axis=1, keepdims=True))
            alpha = jnp.exp2(m_prev - m_new)
            p = jnp.exp2(s - m_new)
            l_scr[g] = alpha * l_scr[g] + jnp.sum(p, axis=1, keepdims=True)
            acc_scr[g] = alpha * acc_scr[g] + _dot(p.astype(BF16), v_ref[:, sl])
            m_scr[g] = m_new

    edge = (j == i) | (j == 0)
    pl.when(edge)(functools.partial(update, True))
    pl.when(jnp.logical_not(edge))(functools.partial(update, False))

    @pl.when(j == i)
    def _():
        for g in range(FLASH_HEADS):
            o_ref[:, g * B_DH:(g + 1) * B_DH] = acc_scr[g] / l_scr[g]


def _flash(pb, fk, *, nb, t, n_total):
    blk = _tile(t, 640, LANES)
    nq = t // blk
    hg = FLASH_HEADS
    w = hg * B_DH
    pairs = [(i, j) for i in range(nq) for j in range(i + 1)]
    qi = jnp.asarray([p[0] for p in pairs], jnp.int32)
    kj = jnp.asarray([p[1] for p in pairs], jnp.int32)
    qc, kc, vc = (A_QK * 2 + A_V * 2) // w, (A_QK * 2 + A_V * 2 + B_W) // w, (A_QK * 2 + A_V * 2 + 2 * B_W) // w
    grid_spec = pltpu.PrefetchScalarGridSpec(
        num_scalar_prefetch=2,
        grid=(nb, B_HEADS // hg, len(pairs)),
        in_specs=[pl.BlockSpec((blk, w), lambda b, h, s, qi, kj: (b * nq + qi[s], qc + h)),
                  pl.BlockSpec((blk, w), lambda b, h, s, qi, kj: (b * nq + kj[s], kc + h)),
                  pl.BlockSpec((blk, w), lambda b, h, s, qi, kj: (b * nq + kj[s], vc + h)),
                  pl.BlockSpec((1, hg, 1, blk), lambda b, h, s, qi, kj: (b, h, 0, kj[s]))],
        out_specs=pl.BlockSpec((blk, w), lambda b, h, s, qi, kj: (b * nq + qi[s], h)),
        scratch_shapes=[pltpu.VMEM((hg, blk, 1), F32), pltpu.VMEM((hg, blk, 1), F32),
                        pltpu.VMEM((hg, blk, B_DH), F32)],
    )
    return pl.pallas_call(
        functools.partial(_flash_kernel, blk=blk, lead_pad=N_PAD),
        grid_spec=grid_spec,
        out_shape=jax.ShapeDtypeStruct((n_total, B_W), F32),
        compiler_params=_params(("parallel", "parallel", "arbitrary")),
        name="fox_flash",
    )(qi, kj, pb, pb, pb, fk)


def _fox_sample_kernel(q_ref, kn_ref, vn_ref, kc_ref, vc_ref, fc_ref, fn_ref, prev_ref, o_ref, *, ds):
    del prev_ref
    nh = B_HEADS
    stack = lambda ref: jnp.concatenate([ref[:, h * B_DH:(h + 1) * B_DH] for h in range(nh)], axis=0)
    q = stack(q_ref) * (B_DH ** -0.5 * LOG2E)
    rows = nh * ds
    n_c = kc_ref.shape[1]
    qh_c = lax.broadcasted_iota(jnp.int32, (rows, n_c), 0) // ds
    kh_c = lax.broadcasted_iota(jnp.int32, (rows, n_c), 1) % nh
    s_c = _dot3(q, kc_ref[0], _NT) - fc_ref[0]
    s_c = jnp.where(qh_c == kh_c, s_c, NEG)
    r_i = lax.broadcasted_iota(jnp.int32, (rows, rows), 0)
    c_i = lax.broadcasted_iota(jnp.int32, (rows, rows), 1)
    s_n = _dot3(q, stack(kn_ref), _NT) - fn_ref[0]
    s_n = jnp.where((r_i // ds == c_i // ds) & (c_i % ds <= r_i % ds), s_n, NEG)
    m = jnp.maximum(jnp.max(s_c, axis=1, keepdims=True), jnp.max(s_n, axis=1, keepdims=True))
    p_c = jnp.exp2(s_c - m)
    p_n = jnp.exp2(s_n - m)
    l = jnp.sum(p_c, axis=1, keepdims=True) + jnp.sum(p_n, axis=1, keepdims=True)
    o = (_dot3(p_c, vc_ref[0]) + _dot3(p_n, stack(vn_ref))) / l
    for h in range(nh):
        o_ref[:, h * B_DH:(h + 1) * B_DH] = o[h * ds:(h + 1) * ds]


def _fox_sample(pb, kc, vc, f_cache, f_new, prev_out, *, nb, ds, row_block0):
    n_c = kc.shape[1]
    base = (A_QK * 2 + A_V * 2) // B_W
    rb = lambda b: row_block0 + b
    return pl.pallas_call(
        functools.partial(_fox_sample_kernel, ds=ds),
        grid=(nb,),
        in_specs=[pl.BlockSpec((ds, B_W), lambda b: (rb(b), base)),
                  pl.BlockSpec((ds, B_W), lambda b: (rb(b), base + 1)),
                  pl.BlockSpec((ds, B_W), lambda b: (rb(b), base + 2)),
                  pl.BlockSpec((1, n_c, B_DH), lambda b: (b, 0, 0)),
                  pl.BlockSpec((1, n_c, B_DH), lambda b: (b, 0, 0)),
                  pl.BlockSpec((1, 1, n_c), lambda b: (b, 0, 0)),
                  pl.BlockSpec((1, 1, B_HEADS * ds), lambda b: (b, 0, 0)),
                  pl.BlockSpec(memory_space=pl.ANY)],
        out_specs=pl.BlockSpec((ds, B_W), lambda b: (rb(b), 0)),
        out_shape=jax.ShapeDtypeStruct(prev_out.shape, F32),
        input_output_aliases={7: 0},
        compiler_params=_params(("parallel",)),
        name="fox_sample",
    )(pb, pb, pb, kc, vc, f_cache, f_new, prev_out)


def _mlstm_kernel(q_ref, k_ref, v_ref, lir_ref, br_ref, lic_ref, bc_ref, c0_ref, n0_ref, m0_ref, *rest,
                  L, aliased):
    if aliased:
        rest = rest[1:]
    h_ref, cout_ref, nout_ref, mout_ref, c_scr, n_scr, m_scr = rest
    c = pl.program_id(1)

    @pl.when(c == 0)
    def _():
        c_scr[...] = c0_ref[0]
        n_scr[...] = n0_ref[0]
        m_scr[...] = m0_ref[0]

    nh = C_HEADS
    hl = nh * L
    seg = lax.broadcasted_iota(jnp.int32, (1, hl), 1) // L
    causal = (lax.broadcasted_iota(jnp.int32, (L, hl), 1) % L) <= lax.broadcasted_iota(jnp.int32, (L, hl), 0)

    def per_head(vals):
        out = vals[0]
        for h in range(1, nh):
            out = jnp.where(seg == h, vals[h], out)
        return out

    seg_max = lambda x, h: jnp.max(jnp.where(seg == h, x, -jnp.inf), axis=1, keepdims=True)

    qf = q_ref[...]
    kf = k_ref[...] * (C_DQK ** -0.5)
    qb = qf.astype(BF16)
    kb = kf.astype(BF16)
    vb = v_ref[...].astype(BF16)
    b_row = br_ref[0, 0]
    li_row = lir_ref[0, 0]
    b_col = [bc_ref[0, 0, :, h:h + 1] for h in range(nh)]
    li_col = [lic_ref[0, 0, :, h:h + 1] for h in range(nh)]
    m_prev = [m_scr[h:h + 1, :] for h in range(nh)]

    d = jnp.where(causal, per_head(b_col) - b_row + li_row, -jnp.inf)
    inter = [b_col[h] + m_prev[h] for h in range(nh)]
    m_t = [jnp.maximum(inter[h], seg_max(d, h)) for h in range(nh)]
    pm = jnp.exp(d - per_head(m_t))
    w_inter = [jnp.exp(inter[h] - m_t[h]) for h in range(nh)]

    khead = lax.broadcasted_iota(jnp.int32, (1, C_QK), 1) // C_DQK
    vhead = lax.broadcasted_iota(jnp.int32, (1, C_V), 1) // C_DV
    zero_b = jnp.zeros((), BF16)
    k_stack = jnp.concatenate([jnp.where(khead == h, kb, zero_b) for h in range(nh)], axis=0)
    v_bd = jnp.concatenate([jnp.where(vhead == h, vb, zero_b) for h in range(nh)], axis=0)
    sqk = _dot(qb, k_stack, _NT) * pm
    sv = _dot(sqk.astype(BF16), v_bd)

    b_last = [b_row[:, h * L + L - 1:h * L + L] for h in range(nh)]
    g_row = per_head(b_last) - b_row + li_row
    for h in range(nh):
        qk_sl = slice(h * C_DQK, (h + 1) * C_DQK)
        v_sl = slice(h * C_DV, (h + 1) * C_DV)
        c_prev = c_scr[h]
        n_prev = n_scr[h:h + 1, :]
        num = w_inter[h] * _dot(qb[:, qk_sl], c_prev.astype(BF16), _NT) + sv[:, v_sl]
        den = (w_inter[h] * jnp.sum(qf[:, qk_sl] * n_prev, axis=1, keepdims=True)
               + jnp.sum(jnp.where(seg == h, sqk, 0.0), axis=1, keepdims=True))
        h_ref[:, v_sl] = num / jnp.maximum(jnp.abs(den), jnp.exp(-m_t[h]))

        m_new = jnp.maximum(b_last[h] + m_prev[h], seg_max(g_row, h))
        w_c = jnp.exp(b_last[h] + m_prev[h] - m_new)
        kw = kf[:, qk_sl] * jnp.exp(b_last[h] - b_col[h] + li_col[h] - m_new)
        c_scr[h] = w_c * c_prev + _dot(vb[:, v_sl], kw.astype(BF16), _TN)
        n_scr[h:h + 1, :] = w_c * n_prev + jnp.sum(kw, axis=0, keepdims=True)
        m_scr[h:h + 1, :] = m_new

    @pl.when(c == pl.num_programs(1) - 1)
    def _():
        cout_ref[0] = c_scr[...]
        nout_ref[0] = n_scr[...]
        mout_ref[0] = m_scr[...]


def _mlstm(p, li_row, b_row, li_col, b_col, c0, n0, m0, *, nb, nc, L, row_block0, n_total, prev_out=None):
    aliased = prev_out is not None
    rb = lambda b, c: row_block0 + b * nc + c
    in_specs = [pl.BlockSpec((L, C_QK), lambda b, c: (rb(b, c), 0)),
                pl.BlockSpec((L, C_QK), lambda b, c: (rb(b, c), 1)),
                pl.BlockSpec((L, C_V), lambda b, c: (rb(b, c), 2 * C_QK // C_V)),
                pl.BlockSpec((1, 1, 1, C_HEADS * L), lambda b, c: (b, c, 0, 0)),
                pl.BlockSpec((1, 1, 1, C_HEADS * L), lambda b, c: (b, c, 0, 0)),
                pl.BlockSpec((1, 1, L, C_HEADS), lambda b, c: (b, c, 0, 0)),
                pl.BlockSpec((1, 1, L, C_HEADS), lambda b, c: (b, c, 0, 0)),
                pl.BlockSpec((1, C_HEADS, C_DV, C_DQK), lambda b, c: (b, 0, 0, 0)),
                pl.BlockSpec((1, C_HEADS, C_DQK), lambda b, c: (b, 0, 0)),
                pl.BlockSpec((1, C_HEADS, 1), lambda b, c: (b, 0, 0))]
    args = [p, p, p, li_row, b_row, li_col, b_col, c0, n0, m0]
    io_alias = {}
    if aliased:
        in_specs.append(pl.BlockSpec(memory_space=pl.ANY))
        args.append(prev_out)
        io_alias = {len(args) - 1: 0}
    return pl.pallas_call(
        functools.partial(_mlstm_kernel, L=L, aliased=aliased),
        grid=(nb, nc),
        in_specs=in_specs,
        out_specs=[pl.BlockSpec((L, C_V), lambda b, c: (rb(b, c), 0)),
                   pl.BlockSpec((1, C_HEADS, C_DV, C_DQK), lambda b, c: (b, 0, 0, 0)),
                   pl.BlockSpec((1, C_HEADS, C_DQK), lambda b, c: (b, 0, 0)),
                   pl.BlockSpec((1, C_HEADS, 1), lambda b, c: (b, 0, 0))],
        out_shape=[jax.ShapeDtypeStruct((n_total, C_V), F32),
                   jax.ShapeDtypeStruct((nb, C_HEADS, C_DV, C_DQK), F32),
                   jax.ShapeDtypeStruct((nb, C_HEADS, C_DQK), F32),
                   jax.ShapeDtypeStruct((nb, C_HEADS, 1), F32)],
        scratch_shapes=[pltpu.VMEM((C_HEADS, C_DV, C_DQK), F32),
                        pltpu.VMEM((C_HEADS, C_DQK), F32),
                        pltpu.VMEM((C_HEADS, 1), F32)],
        input_output_aliases=io_alias,
        compiler_params=_params(("parallel", "arbitrary")),
        name="mlstm_L%d" % L,
    )(*args)


def _mixout_kernel(*refs, hd, act, has_b, precise, n_prev):
    refs = list(refs)
    h_ref, a_ref, r_ref = refs[:3]
    b_ref = refs[3] if has_b else None
    k = 4 if has_b else 3
    ga_ref, wo_ref, gf_ref, wrh_ref, wrl_ref, br_ref, cnt0_ref = refs[k:k + 7]
    h1_ref, xn_ref, info_ref, cnt_ref, cnt_scr = refs[k + 7 + n_prev:]
    cast = (lambda x: x) if precise else (lambda x: x.astype(BF16))
    a = a_ref[...]
    r = r_ref[...]
    gate = r * _sigmoid(r) if act == "silu" else _sigmoid(r)
    parts = []
    for hh in range(a.shape[1] // hd):
        sl = slice(hh * hd, (hh + 1) * hd)
        parts.append(cast(_rms(a[:, sl], ga_ref[...]) * gate[:, sl]))
    if has_b:
        parts.append(cast(b_ref[...]))
    cat = jnp.concatenate(parts, axis=1)
    h1 = h_ref[...] + (_dot3(cat, wo_ref[...]) if precise else _dot(cat, wo_ref[...]))
    h1_ref[...] = h1
    xn = _rms(h1, gf_ref[...])
    xn_ref[...] = _pack_bf16_pairs(xn)
    xh, xl = _split(xn)
    logits = _dot(xh, wrh_ref[...]) + _dot(xh, wrl_ref[...]) + _dot(xl, wrh_ref[...]) + br_ref[...]

    lane = lax.broadcasted_iota(jnp.int32, logits.shape, 1)
    lanef = lane.astype(F32)
    is_g = lane < N_GROUPS
    gl = jnp.where(is_g, logits, -jnp.inf)
    gmax = jnp.max(gl, axis=1, keepdims=True)
    gidx = jnp.min(jnp.where(gl == gmax, lanef, float(LANES)), axis=1, keepdims=True)
    wg = 1.0 / jnp.sum(jnp.where(is_g, jnp.exp(gl - gmax), 0.0), axis=1, keepdims=True)
    lo = N_GROUPS + N_EXP * gidx
    el = jnp.where((lanef >= lo) & (lanef < lo + N_EXP), logits, -jnp.inf)
    m1 = jnp.max(el, axis=1, keepdims=True)
    i1 = jnp.min(jnp.where(el == m1, lanef, float(LANES)), axis=1, keepdims=True)
    el2 = jnp.where(lanef == i1, -jnp.inf, el)
    m2 = jnp.max(el2, axis=1, keepdims=True)
    i2 = jnp.min(jnp.where(el2 == m2, lanef, float(LANES)), axis=1, keepdims=True)
    t = jnp.exp(m2 - m1)
    w1 = wg / (1.0 + t)
    w2 = wg * t / (1.0 + t)
    e1 = i1 - N_GROUPS
    e2 = i2 - N_GROUPS

    @pl.when(pl.program_id(0) == 0)
    def _():
        cnt_scr[...] = cnt0_ref[...]

    tm = logits.shape[0]
    pick = jnp.where((lanef == e1) | (lanef == e2), 1.0, 0.0)
    earlier = (lax.broadcasted_iota(jnp.int32, (tm, tm), 1) < lax.broadcasted_iota(jnp.int32, (tm, tm), 0))
    before = _dot(jnp.where(earlier, 1.0, 0.0).astype(BF16), pick.astype(BF16)) + cnt_scr[...]
    r1 = jnp.sum(jnp.where(lanef == e1, before, 0.0), axis=1, keepdims=True)
    r2 = jnp.sum(jnp.where(lanef == e2, before, 0.0), axis=1, keepdims=True)
    cnt_new = cnt_scr[...] + jnp.sum(pick, axis=0, keepdims=True)
    cnt_scr[...] = cnt_new
    cnt_ref[...] = cnt_new
    info_ref[...] = jnp.where(lane == 0, e1, jnp.where(lane == 1, e2, jnp.where(lane == 2, w1, jnp.where(
        lane == 3, w2, jnp.where(lane == 4, r1, jnp.where(lane == 5, r2, 0.0))))))


def _mixout(h, a, a_col, r, r_col, b, g_head, w_o, g_ffn, wr_hi, wr_lo, b_r, counts0, *, hd, act,
            row0, n_rows, prev=None):
    n, d = h.shape
    precise = prev is not None
    tm = _tile(n_rows, 512, SUBLANES)
    assert row0 % tm == 0
    blk0 = row0 // tm
    wa = w_o.shape[0] if b is None else w_o.shape[0] - B_W
    has_b = b is not None
    row = lambda i: (blk0 + i, 0)
    const = lambda i: (0, 0)
    in_specs = [pl.BlockSpec((tm, d), row),
                pl.BlockSpec((tm, wa), lambda i: (blk0 + i, a_col)),
                pl.BlockSpec((tm, wa), lambda i: (blk0 + i, r_col))]
    args = [h, a, r]
    if has_b:
        in_specs.append(pl.BlockSpec((tm, B_W), row))
        args.append(b)
    in_specs += [pl.BlockSpec((1, hd), const), pl.BlockSpec(w_o.shape, const), pl.BlockSpec((1, d), const),
                 pl.BlockSpec((d, LANES), const), pl.BlockSpec((d, LANES), const), pl.BlockSpec((1, LANES), const),
                 pl.BlockSpec((1, LANES), const)]
    args += [g_head.reshape(1, hd), w_o if precise else w_o.astype(BF16), g_ffn.reshape(1, d), wr_hi, wr_lo, b_r,
             counts0]
    io_alias = {}
    if precise:
        for k, arr in enumerate(prev):
            in_specs.append(pl.BlockSpec(memory_space=pl.ANY))
            args.append(arr)
            io_alias[len(args) - 1] = k
    return pl.pallas_call(
        functools.partial(_mixout_kernel, hd=hd, act=act, has_b=has_b, precise=precise, n_prev=len(io_alias)),
        grid=(n_rows // tm,),
        in_specs=in_specs,
        out_specs=[pl.BlockSpec((tm, d), row), pl.BlockSpec((tm, d // 2), row), pl.BlockSpec((tm, LANES), row),
                   pl.BlockSpec((1, LANES), const)],
        out_shape=[jax.ShapeDtypeStruct((n, d), F32), jax.ShapeDtypeStruct((n, d // 2), jnp.int32),
                   jax.ShapeDtypeStruct((n, LANES), F32), jax.ShapeDtypeStruct((1, LANES), F32)],
        scratch_shapes=[pltpu.VMEM((1, LANES), F32)],
        input_output_aliases=io_alias,
        compiler_params=_params(("arbitrary",)),
        name="mixout_" + act + ("_precise" if precise else ""),
    )(*args)


def _sc_gather(table, idx):
    r = idx.shape[0]
    w = table.shape[1]
    n_workers = SC_CORES * SC_SUBCORES
    per_worker = r // n_workers
    step = SC_GATHER_ROWS * SC_GATHER_BUFS
    assert r % n_workers == 0 and per_worker % step == 0, (r, n_workers, step)
    mesh = plsc.VectorSubcoreMesh(core_axis_name="c", subcore_axis_name="s")

    @functools.partial(
        pl.kernel, mesh=mesh,
        out_type=jax.ShapeDtypeStruct((r, w), table.dtype),
        scratch_types=[pltpu.VMEM((SC_GATHER_BUFS, SC_GATHER_ROWS), jnp.int32),
                       pltpu.VMEM((SC_GATHER_BUFS, SC_GATHER_ROWS, w), table.dtype),
                       pltpu.SemaphoreType.DMA((SC_GATHER_BUFS,)),
                       pltpu.SemaphoreType.DMA((SC_GATHER_BUFS,))],
    )
    def gather(table_hbm, idx_hbm, out_hbm, idx_v, rows_v, gather_sem, store_sem):
        worker = lax.axis_index("s") * SC_CORES + lax.axis_index("c")
        base = worker * per_worker

        @pl.loop(0, per_worker // step)
        def _(j):
            off = pl.multiple_of(base + j * step, step)
            rows = lambda b: pl.ds(off + b * SC_GATHER_ROWS, SC_GATHER_ROWS)
            gathers, stores = [], []
            for b in range(SC_GATHER_BUFS):
                pltpu.sync_copy(idx_hbm.at[rows(b)], idx_v.at[b])
                gathers.append(pltpu.async_copy(table_hbm.at[idx_v.at[b]], rows_v.at[b], gather_sem.at[b]))
            for b in range(SC_GATHER_BUFS):
                gathers[b].wait()
                stores.append(pltpu.async_copy(rows_v.at[b], out_hbm.at[rows(b)], store_sem.at[b]))
            for b in range(SC_GATHER_BUFS):
                stores[b].wait()

    return gather(table, idx)


def _expert_kernel(te_ref, nu_ref, x_ref, wg_ref, wu_ref, wd_ref, y_ref, wgb, wub, wdb):
    i = pl.program_id(0)
    live = i < nu_ref[0]
    new_expert = (i == 0) | (te_ref[i] != te_ref[jnp.maximum(i - 1, 0)])

    @pl.when(live & new_expert)
    def _():
        wgb[...] = wg_ref[0].astype(BF16)
        wub[...] = wu_ref[0].astype(BF16)
        wdb[...] = wd_ref[0].astype(BF16)

    @pl.when(live)
    def _():
        x = _unpack_bf16_pairs(x_ref[...]).astype(BF16)
        g = _dot(x, wgb[...])
        u = _dot(x, wub[...])
        y_ref[...] = _pack_bf16_pairs(_dot((g * _sigmoid(g) * u).astype(BF16), wdb[...]))

    @pl.when(jnp.logical_not(live))
    def _():
        y_ref[...] = jnp.zeros_like(y_ref)


def _experts(xs, tile_expert, n_used, w_gate, w_up, w_down, *, tm):
    n_slots, dp = xs.shape
    d, f = w_gate.shape[-2:]
    grid_spec = pltpu.PrefetchScalarGridSpec(
        num_scalar_prefetch=2,
        grid=(n_slots // tm,),
        in_specs=[pl.BlockSpec((tm, dp), lambda i, te, nu: (i, 0)),
                  pl.BlockSpec((1, d, f), lambda i, te, nu: (te[i], 0, 0)),
                  pl.BlockSpec((1, d, f), lambda i, te, nu: (te[i], 0, 0)),
                  pl.BlockSpec((1, f, d), lambda i, te, nu: (te[i], 0, 0))],
        out_specs=pl.BlockSpec((tm, dp), lambda i, te, nu: (i, 0)),
        scratch_shapes=[pltpu.VMEM((d, f), BF16), pltpu.VMEM((d, f), BF16), pltpu.VMEM((f, d), BF16)],
    )
    return pl.pallas_call(
        _expert_kernel,
        grid_spec=grid_spec,
        out_shape=jax.ShapeDtypeStruct((n_slots, dp), jnp.int32),
        compiler_params=_params(("arbitrary",)),
        name="moe_experts",
    )(tile_expert, n_used, xs, w_gate, w_up, w_down)


def _combine_kernel(h_ref, info_ref, g_ref, y0_ref, y1_ref, o_ref, *, final_norm):
    h2 = h_ref[...] + (info_ref[:, 2:3] * _unpack_bf16_pairs(y0_ref[...])
                       + info_ref[:, 3:4] * _unpack_bf16_pairs(y1_ref[...]))
    o_ref[...] = _rms(h2, g_ref[...]) if final_norm else h2


def _combine(h, yg, info, g, *, final_norm):
    n, d = h.shape
    tm = _tile(n, 512, SUBLANES)
    nt = n // tm
    return pl.pallas_call(
        functools.partial(_combine_kernel, final_norm=final_norm),
        grid=(nt,),
        in_specs=[pl.BlockSpec((tm, d), lambda i: (i, 0)),
                  pl.BlockSpec((tm, LANES), lambda i: (i, 0)),
                  pl.BlockSpec((1, d), lambda i: (0, 0)),
                  pl.BlockSpec((tm, d // 2), lambda i: (i, 0)),
                  pl.BlockSpec((tm, d // 2), lambda i: (nt + i, 0))],
        out_specs=pl.BlockSpec((tm, d), lambda i: (i, 0)),
        out_shape=jax.ShapeDtypeStruct((n, d), F32),
        compiler_params=_params(("parallel",)),
        name="moe_combine",
    )(h, info, g.reshape(1, d), yg, yg)


def _final_kernel(h_ref, info_ref, g_ref, y0_ref, y1_ref, yp_ref, ys_ref, *, tiles_per_batch, lead_tiles, prompt_tiles):
    i = pl.program_id(0)
    h2 = h_ref[...] + (info_ref[:, 2:3] * _unpack_bf16_pairs(y0_ref[...])
                       + info_ref[:, 3:4] * _unpack_bf16_pairs(y1_ref[...]))
    out = _rms(h2, g_ref[...])

    @pl.when((i < prompt_tiles) & (i % tiles_per_batch >= lead_tiles))
    def _():
        yp_ref[...] = out

    @pl.when(i >= prompt_tiles)
    def _():
        ys_ref[...] = out


def _final_combine(h, yg, info, g, *, nb, t, n_sample):
    n, d = h.shape
    tm = _tile(math.gcd(LEAD, n_sample), LANES, SUBLANES)
    nt = n // tm
    tpb, lead_tiles = t // tm, LEAD // tm
    prompt_tiles = nb * tpb
    keep = tpb - lead_tiles

    def prompt_block(i):
        ip = jnp.minimum(i, prompt_tiles - 1)
        return (ip // tpb) * keep + jnp.maximum(ip % tpb - lead_tiles, 0), 0

    return pl.pallas_call(
        functools.partial(_final_kernel, tiles_per_batch=tpb, lead_tiles=lead_tiles, prompt_tiles=prompt_tiles),
        grid=(nt,),
        in_specs=[pl.BlockSpec((tm, d), lambda i: (i, 0)),
                  pl.BlockSpec((tm, LANES), lambda i: (i, 0)),
                  pl.BlockSpec((1, d), lambda i: (0, 0)),
                  pl.BlockSpec((tm, d // 2), lambda i: (i, 0)),
                  pl.BlockSpec((tm, d // 2), lambda i: (nt + i, 0))],
        out_specs=[pl.BlockSpec((tm, d), prompt_block),
                   pl.BlockSpec((tm, d), lambda i: (jnp.maximum(i - prompt_tiles, 0), 0))],
        out_shape=[jax.ShapeDtypeStruct((nb * keep * tm, d), F32), jax.ShapeDtypeStruct((n_sample, d), F32)],
        compiler_params=_params(("arbitrary",)),
        name="moe_combine_final",
    )(h, info, g.reshape(1, d), yg, yg)


def _moe(h1, xn, info, counts_f, w_gate, w_up, w_down, expert_base, g_next, *, final=None):
    n, d = h1.shape
    n_e = N_GROUPS * N_EXP
    tm = _tile(2 * n, 256, SUBLANES)
    n_tiles = (2 * n) // tm + n_e
    n_slots = n_tiles * tm
    experts = jnp.arange(n_e, dtype=jnp.int32)
    counts = counts_f[0, :n_e].astype(jnp.int32)
    starts = jnp.cumsum(counts) - counts
    padded = ((counts + tm - 1) // tm) * tm
    pad_ends = jnp.cumsum(padded)
    pad_starts = pad_ends - padded
    n_used = (pad_ends[-1] // tm).astype(jnp.int32)
    eid = info[:, 0:2].astype(jnp.int32)
    rank = info[:, 4:6].astype(jnp.int32)
    lookup = lambda table, e: jnp.sum(jnp.where(e[..., None] == experts, table, 0), axis=-1)
    slot_of_pick = lookup(pad_starts, eid) + rank
    tok = jnp.broadcast_to(jnp.arange(n, dtype=jnp.int32)[:, None], (n, 2)).reshape(-1)
    _, tok_sorted = lax.sort_key_val(slot_of_pick.reshape(-1), tok)
    slots = jnp.arange(n_slots, dtype=jnp.int32)
    e_of_slot = jnp.minimum(jnp.sum(slots[:, None] >= pad_ends[None, :], axis=1), n_e - 1).astype(jnp.int32)
    r_of_slot = slots - lookup(pad_starts, e_of_slot)
    live = r_of_slot < lookup(counts, e_of_slot)
    src_token = jnp.where(live, tok_sorted[jnp.clip(lookup(starts, e_of_slot) + r_of_slot, 0, 2 * n - 1)], slots % n)
    tile_expert = (expert_base + e_of_slot[::tm]).astype(jnp.int32)

    xs = _sc_gather(xn, src_token.astype(jnp.int32))
    ys = _experts(xs, tile_expert, n_used.reshape(1), w_gate, w_up, w_down, tm=tm)
    yg = _sc_gather(ys, slot_of_pick.T.reshape(-1))
    if final is not None:
        return _final_combine(h1, yg, info, g_next, **final)
    return _combine(h1, yg, info, g_next, final_norm=False)


def _router_weights(w_rg, b_rg, w_re, b_re):
    d = w_rg.shape[0]
    pad = LANES - N_GROUPS - N_GROUPS * N_EXP
    w = jnp.concatenate([w_rg, w_re, jnp.zeros((d, pad), F32)], axis=1)
    b = jnp.concatenate([b_rg, b_re, jnp.zeros((pad,), F32)]).reshape(1, LANES)
    hi = w.astype(BF16)
    lo = (w - hi.astype(F32)).astype(BF16)
    return hi, lo, b


def _rows_to_lanes(x, nb, t, nch):
    return x.reshape(nb, t, nch).transpose(0, 2, 1).reshape(nb * nch, t)


def _even_layer(h, dims, g_mix, w_in, w_g2, b_g, b_f, g_a, w_o, state_gla, ck, cv, clf):
    nb, t, db, ds, npr, n = dims
    d = h.shape[1]
    qa, ka, va, ra, ga, qb, kb, vb, fb = jnp.split(
        w_in, [A_QK, 2 * A_QK, 2 * A_QK + A_V, 2 * A_QK + 2 * A_V, 2 * A_QK + 2 * A_V + A_RANK,
               2 * A_QK + 2 * A_V + A_RANK + B_W, 2 * A_QK + 2 * A_V + A_RANK + 2 * B_W,
               2 * A_QK + 2 * A_V + A_RANK + 3 * B_W], axis=1)
    w_packed = jnp.concatenate(
        [qa, ka, va, ra, qb, kb, vb, ga, fb, jnp.zeros((d, LANES - A_RANK - B_HEADS), F32)], axis=1)
    q_off = 2 * A_QK + 2 * A_V
    colscale = jnp.ones((1, MAIN_W), F32).at[:, q_off:q_off + B_W].set(B_DH ** -0.5 * LOG2E)
    p, pb, k_rows, v_rows = _proj(h, g_mix, w_packed, n_rows=npr, colscale=colscale, kv_col=q_off + B_W)
    p = _proj_precise(h, g_mix, w_packed, p, row0=npr)

    nh = B_HEADS
    fcol = MAIN_W + A_RANK
    zeros_col = lambda r: jnp.zeros((r, 1), F32)
    bias_row = lambda lanes: jnp.tile(b_f, lanes // nh).reshape(1, lanes)
    fb_p = p[:npr, fcol:fcol + nh].reshape(nb, t * nh)
    logf_p, f_p = _gate_scan(fb_p, bias_row(t * nh), zeros_col(nb), mode="fox", act_start=0,
                             valid_start=N_PAD * nh, valid_end=t * nh, seg=None, stride=nh)
    past = ck.shape[1]
    n_c = past * nh
    x_s = _pad_lanes(jnp.concatenate([clf.reshape(db, n_c), p[npr:, fcol:fcol + nh].reshape(db, ds * nh)], axis=1))
    logf_s, f_s = _gate_scan(x_s, bias_row(x_s.shape[1]), zeros_col(db), mode="fox", act_start=n_c,
                             valid_start=0, valid_end=n_c + ds * nh, seg=None, stride=nh)

    nc = t // CHUNK
    oa, s_p = _gla(p, jnp.zeros((nb, A_HEADS, A_DK, A_DV), F32), w_g2, b_g,
                   nb=nb, nc=nc, L=CHUNK, row_block0=0, lead_pad=N_PAD, n_total=n)
    oa, s_s = _gla(p, state_gla, w_g2, b_g, nb=db, nc=1, L=ds, row_block0=npr // ds, lead_pad=0,
                   n_total=n, prev_out=oa, precise=True)

    fk = (f_p * LOG2E).reshape(nb, t, nh).transpose(0, 2, 1).reshape(nb, nh, 1, t)
    ob = _flash(pb, fk, nb=nb, t=t, n_total=n)
    f_cache = (f_s[:, :n_c] * LOG2E).reshape(db, 1, n_c)
    f_new = (f_s[:, n_c:n_c + ds * nh] * LOG2E).reshape(db, ds, nh).transpose(0, 2, 1).reshape(db, 1, nh * ds)
    ob = _fox_sample(p, ck.reshape(db, n_c, B_DH), cv.reshape(db, n_c, B_DH), f_cache, f_new, ob,
                     nb=db, ds=ds, row_block0=npr // ds)

    kcol = q_off + B_W
    states = dict(
        s_p=s_p, s_s=s_s,
        k_p=k_rows.reshape(nb, t, B_HEADS, B_DH)[:, N_PAD:],
        v_p=v_rows.reshape(nb, t, B_HEADS, B_DH)[:, N_PAD:],
        f_p=logf_p.reshape(nb, t, nh)[:, N_PAD:],
        k_s=p[npr:, kcol:kcol + B_W].reshape(db, ds, B_HEADS, B_DH),
        v_s=p[npr:, kcol + B_W:kcol + 2 * B_W].reshape(db, ds, B_HEADS, B_DH),
        f_s=logf_s[:, n_c:n_c + ds * nh].reshape(db, ds, nh))
    return (oa, 0, p, (2 * A_QK + A_V) // A_V, ob, g_a, w_o), states


def _chunk_rows(x, nb, nch, nc, L):
    x = x[:, :nc * L].reshape(nb, nch, nc, L)
    return x.transpose(0, 2, 1, 3).reshape(nb, nc, 1, nch * L), x.transpose(0, 2, 3, 1)


def _odd_layer(h, dims, g_mix, w_in, b_gate, g_c, w_o, c0, n0, m0):
    nb, t, db, ds, npr, n = dims
    d = h.shape[1]
    w_packed = jnp.concatenate(
        [w_in, jnp.zeros((d, LANES - 2 * C_HEADS), F32)], axis=1)
    (p,) = _proj(h, g_mix, w_packed, n_rows=npr)
    p = _proj_precise(h, g_mix, w_packed, p, row0=npr)

    ng = 2 * C_HEADS
    isf = (jnp.arange(ng) >= C_HEADS).astype(F32)
    nc = t // CHUNK

    def gates(rows, nbatch, tt, valid_start, seg):
        x = _pad_lanes(_rows_to_lanes(rows, nbatch, tt, ng))
        val, cum = _gate_scan(x, jnp.tile(b_gate, nbatch).reshape(-1, 1), jnp.tile(isf, nbatch).reshape(-1, 1),
                              mode="mlstm", act_start=0, valid_start=valid_start, valid_end=tt, seg=seg)
        return val.reshape(nbatch, ng, -1), cum.reshape(nbatch, ng, -1)

    val_p, cum_p = gates(p[:npr, MAIN_W:MAIN_W + ng], nb, t, N_PAD, CHUNK)
    val_s, cum_s = gates(p[npr:, MAIN_W:MAIN_W + ng], db, ds, 0, ds)

    def chunked(val, cum, nbatch, ncs, L):
        li_row, li_col = _chunk_rows(val[:, :C_HEADS].reshape(nbatch * C_HEADS, -1), nbatch, C_HEADS, ncs, L)
        b_row, b_col = _chunk_rows(cum[:, C_HEADS:].reshape(nbatch * C_HEADS, -1), nbatch, C_HEADS, ncs, L)
        return li_row, b_row, li_col, b_col

    zc = jnp.zeros((nb, C_HEADS, C_DV, C_DQK), F32)
    zn = jnp.zeros((nb, C_HEADS, C_DQK), F32)
    zm = jnp.zeros((nb, C_HEADS, 1), F32)
    hm, c_p, n_p, m_p = _mlstm(p, *chunked(val_p, cum_p, nb, nc, CHUNK), zc, zn, zm,
                               nb=nb, nc=nc, L=CHUNK, row_block0=0, n_total=n)
    hm, c_s, n_s, m_s = _mlstm(p, *chunked(val_s, cum_s, db, 1, ds), c0, n0, m0.reshape(db, C_HEADS, 1),
                               nb=db, nc=1, L=ds, row_block0=npr // ds, n_total=n, prev_out=hm)
    states = dict(c_p=c_p, n_p=n_p, m_p=m_p.reshape(nb, C_HEADS), c_s=c_s, n_s=n_s, m_s=m_s.reshape(db, C_HEADS))
    return (hm, 0, p, (2 * C_QK + C_V) // C_V, None, g_c, w_o), states


def kernel(x_prompt, x_sample, state_gla, cache_fox_k, cache_fox_v, cache_fox_logf, state_mlstm_c, state_mlstm_n, state_mlstm_m, meta_tokens, norm_mix, norm_ffn, norm_final, w_in_even, w_gla_gate2, b_gla_gate, b_fox_f, g_gla_out, w_out_even, w_in_odd, b_mlstm_gate, g_mlstm_out, w_out_odd, w_router_group, b_router_group, w_router_expert, b_router_expert, w_exp_gate, w_exp_up, w_exp_down):
    nb, seq, d = x_prompt.shape
    db, ds, _ = x_sample.shape
    t = LEAD + seq
    npr, nsm = nb * t, db * ds
    n = npr + nsm
    dims = (nb, t, db, ds, npr, n)
    depth = norm_mix.shape[0]
    n_e = N_GROUPS * N_EXP
    f = w_exp_gate.shape[-1]

    pad_rows = jnp.zeros((N_PAD, d), F32)
    h = jnp.concatenate([piece for b in range(nb) for piece in (pad_rows, meta_tokens, x_prompt[b])]
                        + [x_sample.reshape(nsm, d)], axis=0)
    wg_all = w_exp_gate.reshape(depth * n_e, d, f)
    wu_all = w_exp_up.reshape(depth * n_e, d, f)
    wd_all = w_exp_down.reshape(depth * n_e, f, d)

    even, odd = [], []
    for l in range(depth):
        if l % 2 == 0:
            e = l // 2
            mix, st = _even_layer(h, dims, norm_mix[l], w_in_even[e], w_gla_gate2[e], b_gla_gate[e], b_fox_f[e],
                                  g_gla_out[e], w_out_even[e], state_gla[e], cache_fox_k[e], cache_fox_v[e],
                                  cache_fox_logf[e])
            even.append(st)
            hd, act = A_DV, "silu"
        else:
            o = l // 2
            mix, st = _odd_layer(h, dims, norm_mix[l], w_in_odd[o], b_mlstm_gate[o], g_mlstm_out[o], w_out_odd[o],
                                 state_mlstm_c[o], state_mlstm_n[o], state_mlstm_m[o])
            odd.append(st)
            hd, act = C_DV, "sigmoid"
        a, a_col, r, r_col, b, g_head, w_o = mix
        wr_hi, wr_lo, b_r = _router_weights(w_router_group[l], b_router_group[l], w_router_expert[l],
                                            b_router_expert[l])
        mix_args = (h, a, a_col, r, r_col, b, g_head, w_o, norm_ffn[l], wr_hi, wr_lo, b_r)
        h1, xn, info, counts = _mixout(*mix_args, jnp.zeros((1, LANES), F32), hd=hd, act=act, row0=0, n_rows=npr)
        h1, xn, info, counts = _mixout(*mix_args, counts, hd=hd, act=act, row0=npr, n_rows=nsm,
                                       prev=(h1, xn, info))
        last = l == depth - 1
        h = _moe(h1, xn, info, counts, wg_all, wu_all, wd_all, l * n_e, norm_final if last else norm_ffn[l],
                 final=dict(nb=nb, t=t, n_sample=nsm) if last else None)

    y_prompt = h[0].reshape(nb, seq, d)
    y_sample = h[1].reshape(db, ds, d)
    stack = lambda sts, key: jnp.stack([s[key] for s in sts])
    return (y_prompt, y_sample,
            stack(even, "s_p"), stack(even, "k_p"), stack(even, "v_p"), stack(even, "f_p"),
            stack(odd, "c_p"), stack(odd, "n_p"), stack(odd, "m_p"),
            stack(even, "s_s"), stack(even, "k_s"), stack(even, "v_s"), stack(even, "f_s"),
            stack(odd, "c_s"), stack(odd, "n_s"), stack(odd, "m_s"))
```

```python
import functools
import math

import jax
import jax.numpy as jnp
from jax import lax
from jax.experimental import pallas as pl
from jax.experimental.pallas import tpu as pltpu
from jax.experimental.pallas import tpu_sc as plsc

F32 = jnp.float32
BF16 = jnp.bfloat16

CHUNK = 64
N_META = 16
LEAD = 128
N_PAD = LEAD - N_META
A_HEADS, A_DK, A_DV, A_RANK = 4, 64, 128, 16
A_GATE_NORM = 16.0
B_HEADS, B_DH = 4, 128
C_HEADS, C_DQK, C_DV = 4, 128, 256
GATE_CAP = 15.0
N_GROUPS, N_EXP = 4, 8
EPS = 1e-6
NEG = -1e30
LOG2E = 1.4426950408889634
A_QK = A_HEADS * A_DK
A_V = A_HEADS * A_DV
B_W = B_HEADS * B_DH
C_QK = C_HEADS * C_DQK
C_V = C_HEADS * C_DV

LANES = 128
SUBLANES = 8
VMEM_LIMIT_BYTES = 56 * 1024 * 1024
GLA_SUB = 16
SC_CORES, SC_SUBCORES = 2, 16
SC_GATHER_ROWS = 16
SC_GATHER_BUFS = 4
SC_SCATTER_ROWS = 16
SC_SCATTER_BUFS = 3
FLASH_HEADS = 2
MAIN_W = 3072
PROJ_W = MAIN_W + LANES

_NT = (((1,), (1,)), ((), ()))
_TN = (((0,), (0,)), ((), ()))
_NN = (((1,), (0,)), ((), ()))


def _params(sem):
    return pltpu.CompilerParams(dimension_semantics=sem, vmem_limit_bytes=VMEM_LIMIT_BYTES)


def _tile(n, pref, mult):
    t = (min(pref, n) // mult) * mult
    while t > mult and n % t:
        t -= mult
    assert t >= mult and n % t == 0, (n, pref, mult)
    return t


def _dot(a, b, dims=_NN):
    return lax.dot_general(a, b, dims, preferred_element_type=F32)


def _split(x):
    hi = x.astype(BF16)
    lo = (x - hi.astype(F32)).astype(BF16)
    return hi, lo


def _dot3(a, b, dims=_NN):
    ah, al = _split(a)
    bh, bl = _split(b)
    return _dot(ah, bh, dims) + _dot(ah, bl, dims) + _dot(al, bh, dims)


def _log_sigmoid(x):
    return jnp.minimum(x, 0.0) - jnp.log1p(jnp.exp(-jnp.abs(x)))


def _sigmoid(x):
    return 1.0 / (1.0 + jnp.exp(-x))


def _rms(x, g):
    return x * lax.rsqrt(jnp.mean(x * x, axis=-1, keepdims=True) + EPS) * g


def _pack_bf16_pairs(x):
    w = x.shape[1] // 2
    hi = lax.bitcast_convert_type(x[:, :w].astype(BF16).astype(F32), jnp.int32)
    lo = lax.bitcast_convert_type(x[:, w:].astype(BF16).astype(F32), jnp.int32)
    return hi | lax.shift_right_logical(lo, 16)


def _unpack_bf16_pairs(p):
    hi = lax.bitcast_convert_type(p & jnp.int32(-65536), F32)
    lo = lax.bitcast_convert_type(lax.shift_left(p, 16), F32)
    return jnp.concatenate([hi, lo], axis=1)


def _cumsum_rows(x):
    n = x.shape[0]
    row = lax.broadcasted_iota(jnp.int32, x.shape, 0)
    s = 1
    while s < n:
        x = x + jnp.where(row >= s, pltpu.roll(x, s, axis=0), 0.0)
        s *= 2
    return x


def _proj_kernel(x_ref, g_ref, w_ref, *rest, col_chunk, kv_col):
    if kv_col is None:
        (o_ref,) = rest
    else:
        cs_ref, o_ref, ob_ref, *kv_refs = rest
    tm = x_ref.shape[0]
    xn = _rms(x_ref[...], g_ref[...]).astype(BF16)
    for c0 in range(0, PROJ_W, col_chunk):
        c1 = min(c0 + col_chunk, PROJ_W)
        y = _dot(xn, w_ref[:, c0:c1])
        o_ref[:, c0:c1] = y
        if kv_col is not None:
            if c0 < MAIN_W:
                m1 = min(c1, MAIN_W)
                ob_ref[:, c0:m1] = (y[:, :m1 - c0] * cs_ref[:, c0:m1]).astype(BF16)
            for g0 in range(c0, c1, LANES):
                rel = g0 - kv_col
                if 0 <= rel < 2 * B_W:
                    head = (rel % B_W) // B_DH
                    kv_refs[rel // B_W][pl.ds(head, tm, stride=B_HEADS), :] = y[:, g0 - c0:g0 - c0 + LANES]


def _proj_precise_kernel(x_ref, g_ref, w_ref, prev_ref, o_ref, *, col_chunk):
    del prev_ref
    xn = _rms(x_ref[...], g_ref[...])
    for c0 in range(0, PROJ_W, col_chunk):
        c1 = min(c0 + col_chunk, PROJ_W)
        o_ref[:, c0:c1] = _dot3(xn, w_ref[:, c0:c1])


def _proj(h, g, w_packed, *, n_rows, colscale=None, kv_col=None):
    n, d = h.shape
    tm = _tile(n_rows, 512, 16)
    in_specs = [pl.BlockSpec((tm, d), lambda i: (i, 0)),
                pl.BlockSpec((1, d), lambda i: (0, 0)),
                pl.BlockSpec((d, PROJ_W), lambda i: (0, 0))]
    args = [h, g.reshape(1, d), w_packed.astype(BF16)]
    out_specs = [pl.BlockSpec((tm, PROJ_W), lambda i: (i, 0))]
    out_shape = [jax.ShapeDtypeStruct((n, PROJ_W), F32)]
    if kv_col is not None:
        in_specs.append(pl.BlockSpec((1, MAIN_W), lambda i: (0, 0)))
        args.append(colscale)
        out_specs += [pl.BlockSpec((tm, MAIN_W), lambda i: (i, 0))] + [pl.BlockSpec((tm * B_HEADS, B_DH), lambda i: (i, 0))] * 2
        out_shape += ([jax.ShapeDtypeStruct((n_rows, MAIN_W), BF16)]
                      + [jax.ShapeDtypeStruct((n_rows * B_HEADS, B_DH), F32)] * 2)
    return pl.pallas_call(
        functools.partial(_proj_kernel, col_chunk=640, kv_col=kv_col),
        grid=(n_rows // tm,),
        in_specs=in_specs,
        out_specs=out_specs,
        out_shape=out_shape,
        compiler_params=_params(("parallel",)),
        name="proj",
    )(*args)


def _proj_precise(h, g, w_packed, prev, *, row0):
    n, d = h.shape
    tm = _tile(n - row0, 512, SUBLANES)
    assert row0 % tm == 0
    return pl.pallas_call(
        functools.partial(_proj_precise_kernel, col_chunk=640),
        grid=((n - row0) // tm,),
        in_specs=[pl.BlockSpec((tm, d), lambda i: (row0 // tm + i, 0)),
                  pl.BlockSpec((1, d), lambda i: (0, 0)),
                  pl.BlockSpec((d, PROJ_W), lambda i: (0, 0)),
                  pl.BlockSpec(memory_space=pl.ANY)],
        out_specs=pl.BlockSpec((tm, PROJ_W), lambda i: (row0 // tm + i, 0)),
        out_shape=jax.ShapeDtypeStruct((n, PROJ_W), F32),
        input_output_aliases={3: 0},
        compiler_params=_params(("parallel",)),
        name="proj_precise",
    )(h, g.reshape(1, d), w_packed, prev)


def _gate_scan_kernel(x_ref, bias_ref, isf_ref, val_ref, cum_ref, *, mode, act_start, valid_start, valid_end, seg, stride):
    x = x_ref[...]
    lane = lax.broadcasted_iota(jnp.int32, x.shape, 1)
    valid = (lane >= valid_start) & (lane < valid_end)
    if mode == "fox":
        val = jnp.where(lane >= act_start, _log_sigmoid(x + bias_ref[...]), x)
        val = jnp.where(valid, val, 0.0)
        add = val
    else:
        gate = GATE_CAP * jnp.tanh((x + bias_ref[...]) / GATE_CAP)
        isf = isf_ref[...] > 0.5
        val = jnp.where(isf, jnp.where(valid, _log_sigmoid(gate), 0.0),
                        jnp.where(valid, gate, -jnp.inf))
        add = jnp.where(isf, val, 0.0)
    val_ref[...] = val
    n = x.shape[1]
    pos = lane if seg is None else lane % seg
    limit = n if seg is None else seg
    s = stride
    while s < limit:
        add = add + jnp.where(pos >= s, pltpu.roll(add, s, axis=1), 0.0)
        s *= 2
    cum_ref[...] = add


def _gate_scan(x, bias, isf, *, mode, act_start, valid_start, valid_end, seg, stride=1):
    r, n = x.shape
    full = lambda shape: pl.BlockSpec(shape, lambda i: (0,) * len(shape))
    return pl.pallas_call(
        functools.partial(_gate_scan_kernel, mode=mode, act_start=act_start,
                          valid_start=valid_start, valid_end=valid_end, seg=seg, stride=stride),
        grid=(1,),
        in_specs=[full((r, n)), full(bias.shape), full((r, 1))],
        out_specs=[full((r, n)), full((r, n))],
        out_shape=[jax.ShapeDtypeStruct((r, n), F32)] * 2,
        compiler_params=_params(("arbitrary",)),
        name="gate_scan_" + mode,
    )(x, bias, isf)


def _pad_lanes(x):
    n = x.shape[-1]
    m = -(-n // LANES) * LANES
    return x if m == n else jnp.pad(x, ((0, 0), (0, m - n)))


def _gla_kernel(qk_ref, v_ref, sm_ref, s0_ref, wg2_ref, wg2t_ref, bgr_ref, bgc_ref, *rest,
                L, sub, lead_pad, aliased, precise):
    if aliased:
        rest = rest[1:]
    o_ref, sout_ref, s_scr = rest
    c = pl.program_id(1)
    nh, dk, dv = A_HEADS, A_DK, A_DV
    cast = (lambda x: x) if precise else (lambda x: x.astype(BF16))
    mm = _dot3 if precise else _dot

    @pl.when(c == 0)
    def _():
        s_scr[...] = jnp.zeros_like(s_scr)
        for h in range(nh):
            s_scr[h * dk:(h + 1) * dk, h * dv:(h + 1) * dv] = s0_ref[0, h]

    qk = qk_ref[...]
    q = qk[:, :A_QK] * (A_DK ** -0.5)
    k = qk[:, A_QK:]
    v = v_ref[...]
    ga = sm_ref[:, :A_RANK]
    row = lax.broadcasted_iota(jnp.int32, (L, 1), 0)
    valid = (c * L + row) >= lead_pad
    z = _dot3(ga, wg2_ref[...]) + bgr_ref[...]
    loga = jnp.where(valid, _log_sigmoid(z) / A_GATE_NORM, 0.0)
    k = jnp.where(valid, k, 0.0)
    b = _cumsum_rows(loga)
    b_last = b[L - 1:L, :]
    lane_t = lax.broadcasted_iota(jnp.int32, (1, L), 1)
    zt = _dot3(wg2t_ref[...], ga, _NT) + bgc_ref[...]
    logat = jnp.where((c * L + lane_t) >= lead_pad, _log_sigmoid(zt) / A_GATE_NORM, 0.0)
    b_last_col = jnp.sum(logat, axis=1, keepdims=True)

    qhead = lax.broadcasted_iota(jnp.int32, (1, A_QK), 1) // dk
    vhead = lax.broadcasted_iota(jnp.int32, (1, A_V), 1) // dv
    vb = cast(v)
    zero_b = jnp.zeros((), vb.dtype)
    v_bd = jnp.concatenate([jnp.where(vhead == h, vb, zero_b) for h in range(nh)], axis=0)

    rows_all = lax.broadcasted_iota(jnp.int32, (L, 1), 0)
    a_rows = []
    for i in range(L // sub):
        r0 = i * sub
        ci = jnp.zeros((1, A_QK), F32) if i == 0 else b[r0 - 1:r0, :]
        qt = cast(q[r0:r0 + sub] * jnp.exp(b[r0:r0 + sub] - ci))
        kt = cast(jnp.where(rows_all < r0 + sub, k * jnp.exp(ci - b), 0.0))
        k_stack = jnp.concatenate([jnp.where(qhead == h, kt, zero_b) for h in range(nh)], axis=0)
        a_rows.append(mm(qt, k_stack, _NT))
    a = a_rows[0] if len(a_rows) == 1 else jnp.concatenate(a_rows, axis=0)
    t_idx = lax.broadcasted_iota(jnp.int32, (L, nh * L), 0)
    s_idx = lax.broadcasted_iota(jnp.int32, (L, nh * L), 1) % L
    a = jnp.where(s_idx <= t_idx, a, 0.0)
    o_intra = mm(cast(a), v_bd)

    s_full = s_scr[...]
    o_inter = mm(cast(q * jnp.exp(b)), cast(s_full))
    o_ref[...] = o_inter + o_intra

    k_hat = cast(k * jnp.exp(b_last - b))
    upd = mm(k_hat, vb, _TN)
    khead_col = lax.broadcasted_iota(jnp.int32, (A_QK, 1), 0) // dk
    s_new = jnp.exp(b_last_col) * s_full + jnp.where(khead_col == vhead, upd, 0.0)
    s_scr[...] = s_new

    @pl.when(c == pl.num_programs(1) - 1)
    def _():
        for h in range(nh):
            sout_ref[0, h] = s_new[h * dk:(h + 1) * dk, h * dv:(h + 1) * dv]


def _gla(p, s0, wg2, bg, *, nb, nc, L, row_block0, lead_pad, n_total, prev_out=None, precise=False):
    aliased = prev_out is not None
    rb = lambda b, c: row_block0 + b * nc + c
    in_specs = [pl.BlockSpec((L, 2 * A_QK), lambda b, c: (rb(b, c), 0)),
                pl.BlockSpec((L, A_V), lambda b, c: (rb(b, c), 2 * A_QK // A_V)),
                pl.BlockSpec((L, LANES), lambda b, c: (rb(b, c), MAIN_W // LANES)),
                pl.BlockSpec((1, A_HEADS, A_DK, A_DV), lambda b, c: (b, 0, 0, 0)),
                pl.BlockSpec((A_RANK, A_QK), lambda b, c: (0, 0)),
                pl.BlockSpec((A_QK, A_RANK), lambda b, c: (0, 0)),
                pl.BlockSpec((1, A_QK), lambda b, c: (0, 0)),
                pl.BlockSpec((A_QK, 1), lambda b, c: (0, 0))]
    args = [p, p, p, s0, wg2, wg2.T, bg.reshape(1, A_QK), bg.reshape(A_QK, 1)]
    io_alias = {}
    if aliased:
        in_specs.append(pl.BlockSpec(memory_space=pl.ANY))
        args.append(prev_out)
        io_alias = {len(args) - 1: 0}
    return pl.pallas_call(
        functools.partial(_gla_kernel, L=L, sub=min(GLA_SUB, L), lead_pad=lead_pad, aliased=aliased,
                          precise=precise),
        grid=(nb, nc),
        in_specs=in_specs,
        out_specs=[pl.BlockSpec((L, A_V), lambda b, c: (rb(b, c), 0)),
                   pl.BlockSpec((1, A_HEADS, A_DK, A_DV), lambda b, c: (b, 0, 0, 0))],
        out_shape=[jax.ShapeDtypeStruct((n_total, A_V), F32),
                   jax.ShapeDtypeStruct((nb, A_HEADS, A_DK, A_DV), F32)],
        scratch_shapes=[pltpu.VMEM((A_QK, A_V), F32)],
        input_output_aliases=io_alias,
        compiler_params=_params(("parallel", "arbitrary")),
        name="gla_L%d" % L,
    )(*args)


def _flash_kernel(qi_ref, kj_ref, q_ref, k_ref, v_ref, fk_ref, o_ref, m_scr, l_scr, acc_scr, *, blk, lead_pad):
    step = pl.program_id(2)
    i = qi_ref[step]
    j = kj_ref[step]

    @pl.when(j == 0)
    def _():
        m_scr[...] = jnp.full_like(m_scr, -jnp.inf)
        l_scr[...] = jnp.zeros_like(l_scr)
        acc_scr[...] = jnp.zeros_like(acc_scr)

    def update(masked):
        for g in range(FLASH_HEADS):
            sl = slice(g * B_DH, (g + 1) * B_DH)
            s = _dot(q_ref[:, sl], k_ref[:, sl], _NT) - fk_ref[0, g]
            if masked:
                qpos = i * blk + lax.broadcasted_iota(jnp.int32, (blk, blk), 0)
                kpos = j * blk + lax.broadcasted_iota(jnp.int32, (blk, blk), 1)
                s = jnp.where((kpos <= qpos) & (kpos >= lead_pad), s, NEG)
            m_prev = m_scr[g]
            m_new = jnp.maximum(m_prev, jnp.max(s, axis=1, keepdims=True))
            alpha = jnp.exp2(m_prev - m_new)
            p = jnp.exp2(s - m_new)
            l_scr[g] = alpha * l_scr[g] + jnp.sum(p, axis=1, keepdims=True)
            acc_scr[g] = alpha * acc_scr[g] + _dot(p.astype(BF16), v_ref[:, sl])
            m_scr[g] = m_new

    edge = (j == i) | (j == 0)
    pl.when(edge)(functools.partial(update, True))
    pl.when(jnp.logical_not(edge))(functools.partial(update, False))

    @pl.when(j == i)
    def _():
        for g in range(FLASH_HEADS):
            o_ref[:, g * B_DH:(g + 1) * B_DH] = acc_scr[g] / l_scr[g]


def _flash(pb, fk, *, nb, t, n_total):
    blk = _tile(t, 640, LANES)
    nq = t // blk
    hg = FLASH_HEADS
    w = hg * B_DH
    pairs = [(i, j) for i in range(nq) for j in range(i + 1)]
    qi = jnp.asarray([p[0] for p in pairs], jnp.int32)
    kj = jnp.asarray([p[1] for p in pairs], jnp.int32)
    qc, kc, vc = (A_QK * 2 + A_V * 2) // w, (A_QK * 2 + A_V * 2 + B_W) // w, (A_QK * 2 + A_V * 2 + 2 * B_W) // w
    grid_spec = pltpu.PrefetchScalarGridSpec(
        num_scalar_prefetch=2,
        grid=(nb, B_HEADS // hg, len(pairs)),
        in_specs=[pl.BlockSpec((blk, w), lambda b, h, s, qi, kj: (b * nq + qi[s], qc + h)),
                  pl.BlockSpec((blk, w), lambda b, h, s, qi, kj: (b * nq + kj[s], kc + h)),
                  pl.BlockSpec((blk, w), lambda b, h, s, qi, kj: (b * nq + kj[s], vc + h)),
                  pl.BlockSpec((1, hg, 1, blk), lambda b, h, s, qi, kj: (b, h, 0, kj[s]))],
        out_specs=pl.BlockSpec((blk, w), lambda b, h, s, qi, kj: (b * nq + qi[s], h)),
        scratch_shapes=[pltpu.VMEM((hg, blk, 1), F32), pltpu.VMEM((hg, blk, 1), F32),
                        pltpu.VMEM((hg, blk, B_DH), F32)],
    )
    return pl.pallas_call(
        functools.partial(_flash_kernel, blk=blk, lead_pad=N_PAD),
        grid_spec=grid_spec,
        out_shape=jax.ShapeDtypeStruct((n_total, B_W), F32),
        compiler_params=_params(("parallel", "parallel", "arbitrary")),
        name="fox_flash",
    )(qi, kj, pb, pb, pb, fk)


def _fox_sample_kernel(q_ref, kn_ref, vn_ref, kc_ref, vc_ref, fc_ref, fn_ref, prev_ref, o_ref, *, ds):
    del prev_ref
    nh = B_HEADS
    stack = lambda ref: jnp.concatenate([ref[:, h * B_DH:(h + 1) * B_DH] for h in range(nh)], axis=0)
    q = stack(q_ref) * (B_DH ** -0.5 * LOG2E)
    rows = nh * ds
    n_c = kc_ref.shape[1]
    qh_c = lax.broadcasted_iota(jnp.int32, (rows, n_c), 0) // ds
    kh_c = lax.broadcasted_iota(jnp.int32, (rows, n_c), 1) % nh
    s_c = _dot3(q, kc_ref[0], _NT) - fc_ref[0]
    s_c = jnp.where(qh_c == kh_c, s_c, NEG)
    r_i = lax.broadcasted_iota(jnp.int32, (rows, rows), 0)
    c_i = lax.broadcasted_iota(jnp.int32, (rows, rows), 1)
    s_n = _dot3(q, stack(kn_ref), _NT) - fn_ref[0]
    s_n = jnp.where((r_i // ds == c_i // ds) & (c_i % ds <= r_i % ds), s_n, NEG)
    m = jnp.maximum(jnp.max(s_c, axis=1, keepdims=True), jnp.max(s_n, axis=1, keepdims=True))
    p_c = jnp.exp2(s_c - m)
    p_n = jnp.exp2(s_n - m)
    l = jnp.sum(p_c, axis=1, keepdims=True) + jnp.sum(p_n, axis=1, keepdims=True)
    o = (_dot3(p_c, vc_ref[0]) + _dot3(p_n, stack(vn_ref))) / l
    for h in range(nh):
        o_ref[:, h * B_DH:(h + 1) * B_DH] = o[h * ds:(h + 1) * ds]


def _fox_sample(pb, kc, vc, f_cache, f_new, prev_out, *, nb, ds, row_block0):
    n_c = kc.shape[1]
    base = (A_QK * 2 + A_V * 2) // B_W
    rb = lambda b: row_block0 + b
    return pl.pallas_call(
        functools.partial(_fox_sample_kernel, ds=ds),
        grid=(nb,),
        in_specs=[pl.BlockSpec((ds, B_W), lambda b: (rb(b), base)),
                  pl.BlockSpec((ds, B_W), lambda b: (rb(b), base + 1)),
                  pl.BlockSpec((ds, B_W), lambda b: (rb(b), base + 2)),
                  pl.BlockSpec((1, n_c, B_DH), lambda b: (b, 0, 0)),
                  pl.BlockSpec((1, n_c, B_DH), lambda b: (b, 0, 0)),
                  pl.BlockSpec((1, 1, n_c), lambda b: (b, 0, 0)),
                  pl.BlockSpec((1, 1, B_HEADS * ds), lambda b: (b, 0, 0)),
                  pl.BlockSpec(memory_space=pl.ANY)],
        out_specs=pl.BlockSpec((ds, B_W), lambda b: (rb(b), 0)),
        out_shape=jax.ShapeDtypeStruct(prev_out.shape, F32),
        input_output_aliases={7: 0},
        compiler_params=_params(("parallel",)),
        name="fox_sample",
    )(pb, pb, pb, kc, vc, f_cache, f_new, prev_out)


def _mlstm_kernel(q_ref, k_ref, v_ref, lir_ref, br_ref, lic_ref, bc_ref, c0_ref, n0_ref, m0_ref, *rest,
                  L, aliased):
    if aliased:
        rest = rest[1:]
    h_ref, cout_ref, nout_ref, mout_ref, c_scr, n_scr, m_scr = rest
    c = pl.program_id(1)

    @pl.when(c == 0)
    def _():
        c_scr[...] = c0_ref[0]
        n_scr[...] = n0_ref[0]
        m_scr[...] = m0_ref[0]

    nh = C_HEADS
    hl = nh * L
    seg = lax.broadcasted_iota(jnp.int32, (1, hl), 1) // L
    causal = (lax.broadcasted_iota(jnp.int32, (L, hl), 1) % L) <= lax.broadcasted_iota(jnp.int32, (L, hl), 0)

    def per_head(vals):
        out = vals[0]
        for h in range(1, nh):
            out = jnp.where(seg == h, vals[h], out)
        return out

    seg_max = lambda x, h: jnp.max(jnp.where(seg == h, x, -jnp.inf), axis=1, keepdims=True)

    qf = q_ref[...]
    kf = k_ref[...] * (C_DQK ** -0.5)
    qb = qf.astype(BF16)
    kb = kf.astype(BF16)
    vb = v_ref[...].astype(BF16)
    b_row = br_ref[0, 0]
    li_row = lir_ref[0, 0]
    b_col = [bc_ref[0, 0, :, h:h + 1] for h in range(nh)]
    li_col = [lic_ref[0, 0, :, h:h + 1] for h in range(nh)]
    m_prev = [m_scr[h:h + 1, :] for h in range(nh)]

    d = jnp.where(causal, per_head(b_col) - b_row + li_row, -jnp.inf)
    inter = [b_col[h] + m_prev[h] for h in range(nh)]
    m_t = [jnp.maximum(inter[h], seg_max(d, h)) for h in range(nh)]
    pm = jnp.exp(d - per_head(m_t))
    w_inter = [jnp.exp(inter[h] - m_t[h]) for h in range(nh)]

    khead = lax.broadcasted_iota(jnp.int32, (1, C_QK), 1) // C_DQK
    vhead = lax.broadcasted_iota(jnp.int32, (1, C_V), 1) // C_DV
    zero_b = jnp.zeros((), BF16)
    k_stack = jnp.concatenate([jnp.where(khead == h, kb, zero_b) for h in range(nh)], axis=0)
    v_bd = jnp.concatenate([jnp.where(vhead == h, vb, zero_b) for h in range(nh)], axis=0)
    sqk = _dot(qb, k_stack, _NT) * pm
    sv = _dot(sqk.astype(BF16), v_bd)

    b_last = [b_row[:, h * L + L - 1:h * L + L] for h in range(nh)]
    g_row = per_head(b_last) - b_row + li_row
    for h in range(nh):
        qk_sl = slice(h * C_DQK, (h + 1) * C_DQK)
        v_sl = slice(h * C_DV, (h + 1) * C_DV)
        c_prev = c_scr[h]
        n_prev = n_scr[h:h + 1, :]
        num = w_inter[h] * _dot(qb[:, qk_sl], c_prev.astype(BF16), _NT) + sv[:, v_sl]
        den = (w_inter[h] * jnp.sum(qf[:, qk_sl] * n_prev, axis=1, keepdims=True)
               + jnp.sum(jnp.where(seg == h, sqk, 0.0), axis=1, keepdims=True))
        h_ref[:, v_sl] = num / jnp.maximum(jnp.abs(den), jnp.exp(-m_t[h]))

        m_new = jnp.maximum(b_last[h] + m_prev[h], seg_max(g_row, h))
        w_c = jnp.exp(b_last[h] + m_prev[h] - m_new)
        kw = kf[:, qk_sl] * jnp.exp(b_last[h] - b_col[h] + li_col[h] - m_new)
        c_scr[h] = w_c * c_prev + _dot(vb[:, v_sl], kw.astype(BF16), _TN)
        n_scr[h:h + 1, :] = w_c * n_prev + jnp.sum(kw, axis=0, keepdims=True)
        m_scr[h:h + 1, :] = m_new

    @pl.when(c == pl.num_programs(1) - 1)
    def _():
        cout_ref[0] = c_scr[...]
        nout_ref[0] = n_scr[...]
        mout_ref[0] = m_scr[...]


def _mlstm(p, li_row, b_row, li_col, b_col, c0, n0, m0, *, nb, nc, L, row_block0, n_total, prev_out=None):
    aliased = prev_out is not None
    rb = lambda b, c: row_block0 + b * nc + c
    in_specs = [pl.BlockSpec((L, C_QK), lambda b, c: (rb(b, c), 0)),
                pl.BlockSpec((L, C_QK), lambda b, c: (rb(b, c), 1)),
                pl.BlockSpec((L, C_V), lambda b, c: (rb(b, c), 2 * C_QK // C_V)),
                pl.BlockSpec((1, 1, 1, C_HEADS * L), lambda b, c: (b, c, 0, 0)),
                pl.BlockSpec((1, 1, 1, C_HEADS * L), lambda b, c: (b, c, 0, 0)),
                pl.BlockSpec((1, 1, L, C_HEADS), lambda b, c: (b, c, 0, 0)),
                pl.BlockSpec((1, 1, L, C_HEADS), lambda b, c: (b, c, 0, 0)),
                pl.BlockSpec((1, C_HEADS, C_DV, C_DQK), lambda b, c: (b, 0, 0, 0)),
                pl.BlockSpec((1, C_HEADS, C_DQK), lambda b, c: (b, 0, 0)),
                pl.BlockSpec((1, C_HEADS, 1), lambda b, c: (b, 0, 0))]
    args = [p, p, p, li_row, b_row, li_col, b_col, c0, n0, m0]
    io_alias = {}
    if aliased:
        in_specs.append(pl.BlockSpec(memory_space=pl.ANY))
        args.append(prev_out)
        io_alias = {len(args) - 1: 0}
    return pl.pallas_call(
        functools.partial(_mlstm_kernel, L=L, aliased=aliased),
        grid=(nb, nc),
        in_specs=in_specs,
        out_specs=[pl.BlockSpec((L, C_V), lambda b, c: (rb(b, c), 0)),
                   pl.BlockSpec((1, C_HEADS, C_DV, C_DQK), lambda b, c: (b, 0, 0, 0)),
                   pl.BlockSpec((1, C_HEADS, C_DQK), lambda b, c: (b, 0, 0)),
                   pl.BlockSpec((1, C_HEADS, 1), lambda b, c: (b, 0, 0))],
        out_shape=[jax.ShapeDtypeStruct((n_total, C_V), F32),
                   jax.ShapeDtypeStruct((nb, C_HEADS, C_DV, C_DQK), F32),
                   jax.ShapeDtypeStruct((nb, C_HEADS, C_DQK), F32),
                   jax.ShapeDtypeStruct((nb, C_HEADS, 1), F32)],
        scratch_shapes=[pltpu.VMEM((C_HEADS, C_DV, C_DQK), F32),
                        pltpu.VMEM((C_HEADS, C_DQK), F32),
                        pltpu.VMEM((C_HEADS, 1), F32)],
        input_output_aliases=io_alias,
        compiler_params=_params(("parallel", "arbitrary")),
        name="mlstm_L%d" % L,
    )(*args)


def _mixout_kernel(*refs, hd, act, has_b, precise, n_prev):
    refs = list(refs)
    h_ref, a_ref, r_ref = refs[:3]
    b_ref = refs[3] if has_b else None
    k = 4 if has_b else 3
    ga_ref, wo_ref, gf_ref, wrh_ref, wrl_ref, br_ref, cnt0_ref = refs[k:k + 7]
    h1_ref, xn_ref, info_ref, cnt_ref, cnt_scr = refs[k + 7 + n_prev:]
    cast = (lambda x: x) if precise else (lambda x: x.astype(BF16))
    a = a_ref[...]
    r = r_ref[...]
    gate = r * _sigmoid(r) if act == "silu" else _sigmoid(r)
    parts = []
    for hh in range(a.shape[1] // hd):
        sl = slice(hh * hd, (hh + 1) * hd)
        parts.append(cast(_rms(a[:, sl], ga_ref[...]) * gate[:, sl]))
    if has_b:
        parts.append(cast(b_ref[...]))
    cat = jnp.concatenate(parts, axis=1)
    h1 = h_ref[...] + (_dot3(cat, wo_ref[...]) if precise else _dot(cat, wo_ref[...]))
    h1_ref[...] = h1
    xn = _rms(h1, gf_ref[...])
    xn_ref[...] = _pack_bf16_pairs(xn)
    xh, xl = _split(xn)
    logits = _dot(xh, wrh_ref[...]) + _dot(xh, wrl_ref[...]) + _dot(xl, wrh_ref[...]) + br_ref[...]

    lane = lax.broadcasted_iota(jnp.int32, logits.shape, 1)
    lanef = lane.astype(F32)
    is_g = lane < N_GROUPS
    gl = jnp.where(is_g, logits, -jnp.inf)
    gmax = jnp.max(gl, axis=1, keepdims=True)
    gidx = jnp.min(jnp.where(gl == gmax, lanef, float(LANES)), axis=1, keepdims=True)
    wg = 1.0 / jnp.sum(jnp.where(is_g, jnp.exp(gl - gmax), 0.0), axis=1, keepdims=True)
    lo = N_GROUPS + N_EXP * gidx
    el = jnp.where((lanef >= lo) & (lanef < lo + N_EXP), logits, -jnp.inf)
    m1 = jnp.max(el, axis=1, keepdims=True)
    i1 = jnp.min(jnp.where(el == m1, lanef, float(LANES)), axis=1, keepdims=True)
    el2 = jnp.where(lanef == i1, -jnp.inf, el)
    m2 = jnp.max(el2, axis=1, keepdims=True)
    i2 = jnp.min(jnp.where(el2 == m2, lanef, float(LANES)), axis=1, keepdims=True)
    t = jnp.exp(m2 - m1)
    w1 = wg / (1.0 + t)
    w2 = wg * t / (1.0 + t)
    e1 = i1 - N_GROUPS
    e2 = i2 - N_GROUPS

    @pl.when(pl.program_id(0) == 0)
    def _():
        cnt_scr[...] = cnt0_ref[...]

    tm = logits.shape[0]
    pick = jnp.where((lanef == e1) | (lanef == e2), 1.0, 0.0)
    earlier = (lax.broadcasted_iota(jnp.int32, (tm, tm), 1) < lax.broadcasted_iota(jnp.int32, (tm, tm), 0))
    before = _dot(jnp.where(earlier, 1.0, 0.0).astype(BF16), pick.astype(BF16)) + cnt_scr[...]
    r1 = jnp.sum(jnp.where(lanef == e1, before, 0.0), axis=1, keepdims=True)
    r2 = jnp.sum(jnp.where(lanef == e2, before, 0.0), axis=1, keepdims=True)
    cnt_new = cnt_scr[...] + jnp.sum(pick, axis=0, keepdims=True)
    cnt_scr[...] = cnt_new
    cnt_ref[...] = cnt_new
    info_ref[...] = jnp.where(lane == 0, e1, jnp.where(lane == 1, e2, jnp.where(lane == 2, w1, jnp.where(
        lane == 3, w2, jnp.where(lane == 4, r1, jnp.where(lane == 5, r2, 0.0))))))


def _mixout(h, a, a_col, r, r_col, b, g_head, w_o, g_ffn, wr_hi, wr_lo, b_r, counts0, *, hd, act,
            row0, n_rows, prev=None):
    n, d = h.shape
    precise = prev is not None
    tm = _tile(n_rows, 512, SUBLANES)
    assert row0 % tm == 0
    blk0 = row0 // tm
    wa = w_o.shape[0] if b is None else w_o.shape[0] - B_W
    has_b = b is not None
    row = lambda i: (blk0 + i, 0)
    const = lambda i: (0, 0)
    in_specs = [pl.BlockSpec((tm, d), row),
                pl.BlockSpec((tm, wa), lambda i: (blk0 + i, a_col)),
                pl.BlockSpec((tm, wa), lambda i: (blk0 + i, r_col))]
    args = [h, a, r]
    if has_b:
        in_specs.append(pl.BlockSpec((tm, B_W), row))
        args.append(b)
    in_specs += [pl.BlockSpec((1, hd), const), pl.BlockSpec(w_o.shape, const), pl.BlockSpec((1, d), const),
                 pl.BlockSpec((d, LANES), const), pl.BlockSpec((d, LANES), const), pl.BlockSpec((1, LANES), const),
                 pl.BlockSpec((1, LANES), const)]
    args += [g_head.reshape(1, hd), w_o if precise else w_o.astype(BF16), g_ffn.reshape(1, d), wr_hi, wr_lo, b_r,
             counts0]
    io_alias = {}
    if precise:
        for k, arr in enumerate(prev):
            in_specs.append(pl.BlockSpec(memory_space=pl.ANY))
            args.append(arr)
            io_alias[len(args) - 1] = k
    return pl.pallas_call(
        functools.partial(_mixout_kernel, hd=hd, act=act, has_b=has_b, precise=precise, n_prev=len(io_alias)),
        grid=(n_rows // tm,),
        in_specs=in_specs,
        out_specs=[pl.BlockSpec((tm, d), row), pl.BlockSpec((tm, d // 2), row), pl.BlockSpec((tm, LANES), row),
                   pl.BlockSpec((1, LANES), const)],
        out_shape=[jax.ShapeDtypeStruct((n, d), F32), jax.ShapeDtypeStruct((n, d // 2), jnp.int32),
                   jax.ShapeDtypeStruct((n, LANES), F32), jax.ShapeDtypeStruct((1, LANES), F32)],
        scratch_shapes=[pltpu.VMEM((1, LANES), F32)],
        input_output_aliases=io_alias,
        compiler_params=_params(("arbitrary",)),
        name="mixout_" + act + ("_precise" if precise else ""),
    )(*args)


def _sc_gather(table, idx):
    r = idx.shape[0]
    w = table.shape[1]
    n_workers = SC_CORES * SC_SUBCORES
    per_worker = r // n_workers
    step = SC_GATHER_ROWS * SC_GATHER_BUFS
    assert r % n_workers == 0 and per_worker % step == 0, (r, n_workers, step)
    mesh = plsc.VectorSubcoreMesh(core_axis_name="c", subcore_axis_name="s")

    @functools.partial(
        pl.kernel, mesh=mesh,
        out_type=jax.ShapeDtypeStruct((r, w), table.dtype),
        scratch_types=[pltpu.VMEM((SC_GATHER_BUFS, SC_GATHER_ROWS), jnp.int32),
                       pltpu.VMEM((SC_GATHER_BUFS, SC_GATHER_ROWS, w), table.dtype),
                       pltpu.SemaphoreType.DMA((SC_GATHER_BUFS,)),
                       pltpu.SemaphoreType.DMA((SC_GATHER_BUFS,))],
    )
    def gather(table_hbm, idx_hbm, out_hbm, idx_v, rows_v, gather_sem, store_sem):
        worker = lax.axis_index("s") * SC_CORES + lax.axis_index("c")
        base = worker * per_worker

        @pl.loop(0, per_worker // step)
        def _(j):
            off = pl.multiple_of(base + j * step, step)
            rows = lambda b: pl.ds(off + b * SC_GATHER_ROWS, SC_GATHER_ROWS)
            gathers, stores = [], []
            for b in range(SC_GATHER_BUFS):
                pltpu.sync_copy(idx_hbm.at[rows(b)], idx_v.at[b])
                gathers.append(pltpu.async_copy(table_hbm.at[idx_v.at[b]], rows_v.at[b], gather_sem.at[b]))
            for b in range(SC_GATHER_BUFS):
                gathers[b].wait()
                stores.append(pltpu.async_copy(rows_v.at[b], out_hbm.at[rows(b)], store_sem.at[b]))
            for b in range(SC_GATHER_BUFS):
                stores[b].wait()

    return gather(table, idx)


def _sc_scatter_pairs(rows, slots, n_slots):
    n, w = rows.shape
    n_workers = SC_CORES * SC_SUBCORES
    per_worker = n // n_workers
    step = SC_SCATTER_ROWS * SC_SCATTER_BUFS
    assert n % n_workers == 0 and per_worker % step == 0, (n, n_workers, step)
    mesh = plsc.VectorSubcoreMesh(core_axis_name="c", subcore_axis_name="s")

    @functools.partial(
        pl.kernel, mesh=mesh,
        out_type=jax.ShapeDtypeStruct((n_slots, w), rows.dtype),
        scratch_types=[pltpu.VMEM((2 * SC_SCATTER_BUFS, SC_SCATTER_ROWS), jnp.int32),
                       pltpu.VMEM((SC_SCATTER_BUFS, SC_SCATTER_ROWS, w), rows.dtype),
                       pltpu.SemaphoreType.DMA((SC_SCATTER_BUFS,)),
                       pltpu.SemaphoreType.DMA((SC_SCATTER_BUFS,))],
    )
    def scatter(rows_hbm, slots_hbm, out_hbm, idx_v, rows_v, load_sem, store_sem):
        worker = lax.axis_index("s") * SC_CORES + lax.axis_index("c")
        base = worker * per_worker

        @pl.loop(0, per_worker // step)
        def _(j):
            off = pl.multiple_of(base + j * step, SC_SCATTER_ROWS)
            loads, stores = [], []
            for b in range(SC_SCATTER_BUFS):
                r0 = off + b * SC_SCATTER_ROWS
                pltpu.sync_copy(slots_hbm.at[pl.ds(r0, SC_SCATTER_ROWS)], idx_v.at[2 * b])
                pltpu.sync_copy(slots_hbm.at[pl.ds(n + r0, SC_SCATTER_ROWS)], idx_v.at[2 * b + 1])
                loads.append(pltpu.async_copy(rows_hbm.at[pl.ds(r0, SC_SCATTER_ROWS)], rows_v.at[b], load_sem.at[b]))
            for b in range(SC_SCATTER_BUFS):
                loads[b].wait()
                for k in range(2):
                    stores.append(pltpu.async_copy(rows_v.at[b], out_hbm.at[idx_v.at[2 * b + k]], store_sem.at[b]))
            for copy in stores:
                copy.wait()

    return scatter(rows, slots)


def _expert_kernel(te_ref, nu_ref, x_ref, wg_ref, wu_ref, wd_ref, y_ref, wgb, wub, wdb):
    i = pl.program_id(0)
    live = i < nu_ref[0]
    new_expert = (i == 0) | (te_ref[i] != te_ref[jnp.maximum(i - 1, 0)])

    @pl.when(live & new_expert)
    def _():
        wgb[...] = wg_ref[0].astype(BF16)
        wub[...] = wu_ref[0].astype(BF16)
        wdb[...] = wd_ref[0].astype(BF16)

    @pl.when(live)
    def _():
        x = _unpack_bf16_pairs(x_ref[...]).astype(BF16)
        g = _dot(x, wgb[...])
        u = _dot(x, wub[...])
        y_ref[...] = _pack_bf16_pairs(_dot((g * _sigmoid(g) * u).astype(BF16), wdb[...]))

    @pl.when(jnp.logical_not(live))
    def _():
        y_ref[...] = jnp.zeros_like(y_ref)


def _experts(xs, tile_expert, n_used, w_gate, w_up, w_down, *, tm):
    n_slots, dp = xs.shape
    d, f = w_gate.shape[-2:]
    grid_spec = pltpu.PrefetchScalarGridSpec(
        num_scalar_prefetch=2,
        grid=(n_slots // tm,),
        in_specs=[pl.BlockSpec((tm, dp), lambda i, te, nu: (i, 0)),
                  pl.BlockSpec((1, d, f), lambda i, te, nu: (te[i], 0, 0)),
                  pl.BlockSpec((1, d, f), lambda i, te, nu: (te[i], 0, 0)),
                  pl.BlockSpec((1, f, d), lambda i, te, nu: (te[i], 0, 0))],
        out_specs=pl.BlockSpec((tm, dp), lambda i, te, nu: (i, 0)),
        scratch_shapes=[pltpu.VMEM((d, f), BF16), pltpu.VMEM((d, f), BF16), pltpu.VMEM((f, d), BF16)],
    )
    return pl.pallas_call(
        _expert_kernel,
        grid_spec=grid_spec,
        out_shape=jax.ShapeDtypeStruct((n_slots, dp), jnp.int32),
        compiler_params=_params(("arbitrary",)),
        name="moe_experts",
    )(tile_expert, n_used, xs, w_gate, w_up, w_down)


def _combine_kernel(h_ref, info_ref, g_ref, y0_ref, y1_ref, o_ref, *, final_norm):
    h2 = h_ref[...] + (info_ref[:, 2:3] * _unpack_bf16_pairs(y0_ref[...])
                       + info_ref[:, 3:4] * _unpack_bf16_pairs(y1_ref[...]))
    o_ref[...] = _rms(h2, g_ref[...]) if final_norm else h2


def _combine(h, yg, info, g, *, final_norm):
    n, d = h.shape
    tm = _tile(n, 512, SUBLANES)
    nt = n // tm
    return pl.pallas_call(
        functools.partial(_combine_kernel, final_norm=final_norm),
        grid=(nt,),
        in_specs=[pl.BlockSpec((tm, d), lambda i: (i, 0)),
                  pl.BlockSpec((tm, LANES), lambda i: (i, 0)),
                  pl.BlockSpec((1, d), lambda i: (0, 0)),
                  pl.BlockSpec((tm, d // 2), lambda i: (i, 0)),
                  pl.BlockSpec((tm, d // 2), lambda i: (nt + i, 0))],
        out_specs=pl.BlockSpec((tm, d), lambda i: (i, 0)),
        out_shape=jax.ShapeDtypeStruct((n, d), F32),
        compiler_params=_params(("parallel",)),
        name="moe_combine",
    )(h, info, g.reshape(1, d), yg, yg)


def _final_kernel(h_ref, info_ref, g_ref, y0_ref, y1_ref, yp_ref, ys_ref, *, tiles_per_batch, lead_tiles, prompt_tiles):
    i = pl.program_id(0)
    h2 = h_ref[...] + (info_ref[:, 2:3] * _unpack_bf16_pairs(y0_ref[...])
                       + info_ref[:, 3:4] * _unpack_bf16_pairs(y1_ref[...]))
    out = _rms(h2, g_ref[...])

    @pl.when((i < prompt_tiles) & (i % tiles_per_batch >= lead_tiles))
    def _():
        yp_ref[...] = out

    @pl.when(i >= prompt_tiles)
    def _():
        ys_ref[...] = out


def _final_combine(h, yg, info, g, *, nb, t, n_sample):
    n, d = h.shape
    tm = _tile(math.gcd(LEAD, n_sample), LANES, SUBLANES)
    nt = n // tm
    tpb, lead_tiles = t // tm, LEAD // tm
    prompt_tiles = nb * tpb
    keep = tpb - lead_tiles

    def prompt_block(i):
        ip = jnp.minimum(i, prompt_tiles - 1)
        return (ip // tpb) * keep + jnp.maximum(ip % tpb - lead_tiles, 0), 0

    return pl.pallas_call(
        functools.partial(_final_kernel, tiles_per_batch=tpb, lead_tiles=lead_tiles, prompt_tiles=prompt_tiles),
        grid=(nt,),
        in_specs=[pl.BlockSpec((tm, d), lambda i: (i, 0)),
                  pl.BlockSpec((tm, LANES), lambda i: (i, 0)),
                  pl.BlockSpec((1, d), lambda i: (0, 0)),
                  pl.BlockSpec((tm, d // 2), lambda i: (i, 0)),
                  pl.BlockSpec((tm, d // 2), lambda i: (nt + i, 0))],
        out_specs=[pl.BlockSpec((tm, d), prompt_block),
                   pl.BlockSpec((tm, d), lambda i: (jnp.maximum(i - prompt_tiles, 0), 0))],
        out_shape=[jax.ShapeDtypeStruct((nb * keep * tm, d), F32), jax.ShapeDtypeStruct((n_sample, d), F32)],
        compiler_params=_params(("arbitrary",)),
        name="moe_combine_final",
    )(h, info, g.reshape(1, d), yg, yg)


def _moe(h1, xn, info, counts_f, w_gate, w_up, w_down, expert_base, g_next, *, final=None):
    n, d = h1.shape
    n_e = N_GROUPS * N_EXP
    tm = _tile(2 * n, 256, SUBLANES)
    n_tiles = (2 * n) // tm + n_e
    n_slots = n_tiles * tm
    experts = jnp.arange(n_e, dtype=jnp.int32)
    counts = counts_f[0, :n_e].astype(jnp.int32)
    padded = ((counts + tm - 1) // tm) * tm
    pad_ends = jnp.cumsum(padded)
    pad_starts = pad_ends - padded
    n_used = (pad_ends[-1] // tm).astype(jnp.int32)
    eid = info[:, 0:2].astype(jnp.int32)
    rank = info[:, 4:6].astype(jnp.int32)
    slot_of_pick = jnp.sum(jnp.where(eid[..., None] == experts, pad_starts, 0), axis=-1) + rank
    slots = slot_of_pick.T.reshape(-1)
    tile_starts = jnp.arange(n_tiles, dtype=jnp.int32) * tm
    tile_expert = expert_base + jnp.minimum(jnp.sum(tile_starts[:, None] >= pad_ends[None, :], axis=1), n_e - 1)

    xs = _sc_scatter_pairs(xn, slots, n_slots)
    ys = _experts(xs, tile_expert.astype(jnp.int32), n_used.reshape(1), w_gate, w_up, w_down, tm=tm)
    yg = _sc_gather(ys, slots)
    if final is not None:
        return _final_combine(h1, yg, info, g_next, **final)
    return _combine(h1, yg, info, g_next, final_norm=False)


def _router_weights(w_rg, b_rg, w_re, b_re):
    d = w_rg.shape[0]
    pad = LANES - N_GROUPS - N_GROUPS * N_EXP
    w = jnp.concatenate([w_rg, w_re, jnp.zeros((d, pad), F32)], axis=1)
    b = jnp.concatenate([b_rg, b_re, jnp.zeros((pad,), F32)]).reshape(1, LANES)
    hi = w.astype(BF16)
    lo = (w - hi.astype(F32)).astype(BF16)
    return hi, lo, b


def _rows_to_lanes(x, nb, t, nch):
    return x.reshape(nb, t, nch).transpose(0, 2, 1).reshape(nb * nch, t)


def _even_layer(h, dims, g_mix, w_in, w_g2, b_g, b_f, g_a, w_o, state_gla, ck, cv, clf):
    nb, t, db, ds, npr, n = dims
    d = h.shape[1]
    qa, ka, va, ra, ga, qb, kb, vb, fb = jnp.split(
        w_in, [A_QK, 2 * A_QK, 2 * A_QK + A_V, 2 * A_QK + 2 * A_V, 2 * A_QK + 2 * A_V + A_RANK,
               2 * A_QK + 2 * A_V + A_RANK + B_W, 2 * A_QK + 2 * A_V + A_RANK + 2 * B_W,
               2 * A_QK + 2 * A_V + A_RANK + 3 * B_W], axis=1)
    w_packed = jnp.concatenate(
        [qa, ka, va, ra, qb, kb, vb, ga, fb, jnp.zeros((d, LANES - A_RANK - B_HEADS), F32)], axis=1)
    q_off = 2 * A_QK + 2 * A_V
    colscale = jnp.ones((1, MAIN_W), F32).at[:, q_off:q_off + B_W].set(B_DH ** -0.5 * LOG2E)
    p, pb, k_rows, v_rows = _proj(h, g_mix, w_packed, n_rows=npr, colscale=colscale, kv_col=q_off + B_W)
    p = _proj_precise(h, g_mix, w_packed, p, row0=npr)

    nh = B_HEADS
    fcol = MAIN_W + A_RANK
    zeros_col = lambda r: jnp.zeros((r, 1), F32)
    bias_row = lambda lanes: jnp.tile(b_f, lanes // nh).reshape(1, lanes)
    fb_p = p[:npr, fcol:fcol + nh].reshape(nb, t * nh)
    logf_p, f_p = _gate_scan(fb_p, bias_row(t * nh), zeros_col(nb), mode="fox", act_start=0,
                             valid_start=N_PAD * nh, valid_end=t * nh, seg=None, stride=nh)
    past = ck.shape[1]
    n_c = past * nh
    x_s = _pad_lanes(jnp.concatenate([clf.reshape(db, n_c), p[npr:, fcol:fcol + nh].reshape(db, ds * nh)], axis=1))
    logf_s, f_s = _gate_scan(x_s, bias_row(x_s.shape[1]), zeros_col(db), mode="fox", act_start=n_c,
                             valid_start=0, valid_end=n_c + ds * nh, seg=None, stride=nh)

    nc = t // CHUNK
    oa, s_p = _gla(p, jnp.zeros((nb, A_HEADS, A_DK, A_DV), F32), w_g2, b_g,
                   nb=nb, nc=nc, L=CHUNK, row_block0=0, lead_pad=N_PAD, n_total=n)
    oa, s_s = _gla(p, state_gla, w_g2, b_g, nb=db, nc=1, L=ds, row_block0=npr // ds, lead_pad=0,
                   n_total=n, prev_out=oa, precise=True)

    fk = (f_p * LOG2E).reshape(nb, t, nh).transpose(0, 2, 1).reshape(nb, nh, 1, t)
    ob = _flash(pb, fk, nb=nb, t=t, n_total=n)
    f_cache = (f_s[:, :n_c] * LOG2E).reshape(db, 1, n_c)
    f_new = (f_s[:, n_c:n_c + ds * nh] * LOG2E).reshape(db, ds, nh).transpose(0, 2, 1).reshape(db, 1, nh * ds)
    ob = _fox_sample(p, ck.reshape(db, n_c, B_DH), cv.reshape(db, n_c, B_DH), f_cache, f_new, ob,
                     nb=db, ds=ds, row_block0=npr // ds)

    kcol = q_off + B_W
    states = dict(
        s_p=s_p, s_s=s_s,
        k_p=k_rows.reshape(nb, t, B_HEADS, B_DH)[:, N_PAD:],
        v_p=v_rows.reshape(nb, t, B_HEADS, B_DH)[:, N_PAD:],
        f_p=logf_p.reshape(nb, t, nh)[:, N_PAD:],
        k_s=p[npr:, kcol:kcol + B_W].reshape(db, ds, B_HEADS, B_DH),
        v_s=p[npr:, kcol + B_W:kcol + 2 * B_W].reshape(db, ds, B_HEADS, B_DH),
        f_s=logf_s[:, n_c:n_c + ds * nh].reshape(db, ds, nh))
    return (oa, 0, p, (2 * A_QK + A_V) // A_V, ob, g_a, w_o), states


def _chunk_rows(x, nb, nch, nc, L):
    x = x[:, :nc * L].reshape(nb, nch, nc, L)
    return x.transpose(0, 2, 1, 3).reshape(nb, nc, 1, nch * L), x.transpose(0, 2, 3, 1)


def _odd_layer(h, dims, g_mix, w_in, b_gate, g_c, w_o, c0, n0, m0):
    nb, t, db, ds, npr, n = dims
    d = h.shape[1]
    w_packed = jnp.concatenate(
        [w_in, jnp.zeros((d, LANES - 2 * C_HEADS), F32)], axis=1)
    (p,) = _proj(h, g_mix, w_packed, n_rows=npr)
    p = _proj_precise(h, g_mix, w_packed, p, row0=npr)

    ng = 2 * C_HEADS
    isf = (jnp.arange(ng) >= C_HEADS).astype(F32)
    nc = t // CHUNK

    def gates(rows, nbatch, tt, valid_start, seg):
        x = _pad_lanes(_rows_to_lanes(rows, nbatch, tt, ng))
        val, cum = _gate_scan(x, jnp.tile(b_gate, nbatch).reshape(-1, 1), jnp.tile(isf, nbatch).reshape(-1, 1),
                              mode="mlstm", act_start=0, valid_start=valid_start, valid_end=tt, seg=seg)
        return val.reshape(nbatch, ng, -1), cum.reshape(nbatch, ng, -1)

    val_p, cum_p = gates(p[:npr, MAIN_W:MAIN_W + ng], nb, t, N_PAD, CHUNK)
    val_s, cum_s = gates(p[npr:, MAIN_W:MAIN_W + ng], db, ds, 0, ds)

    def chunked(val, cum, nbatch, ncs, L):
        li_row, li_col = _chunk_rows(val[:, :C_HEADS].reshape(nbatch * C_HEADS, -1), nbatch, C_HEADS, ncs, L)
        b_row, b_col = _chunk_rows(cum[:, C_HEADS:].reshape(nbatch * C_HEADS, -1), nbatch, C_HEADS, ncs, L)
        return li_row, b_row, li_col, b_col

    zc = jnp.zeros((nb, C_HEADS, C_DV, C_DQK), F32)
    zn = jnp.zeros((nb, C_HEADS, C_DQK), F32)
    zm = jnp.zeros((nb, C_HEADS, 1), F32)
    hm, c_p, n_p, m_p = _mlstm(p, *chunked(val_p, cum_p, nb, nc, CHUNK), zc, zn, zm,
                               nb=nb, nc=nc, L=CHUNK, row_block0=0, n_total=n)
    hm, c_s, n_s, m_s = _mlstm(p, *chunked(val_s, cum_s, db, 1, ds), c0, n0, m0.reshape(db, C_HEADS, 1),
                               nb=db, nc=1, L=ds, row_block0=npr // ds, n_total=n, prev_out=hm)
    states = dict(c_p=c_p, n_p=n_p, m_p=m_p.reshape(nb, C_HEADS), c_s=c_s, n_s=n_s, m_s=m_s.reshape(db, C_HEADS))
    return (hm, 0, p, (2 * C_QK + C_V) // C_V, None, g_c, w_o), states


def kernel(x_prompt, x_sample, state_gla, cache_fox_k, cache_fox_v, cache_fox_logf, state_mlstm_c, state_mlstm_n, state_mlstm_m, meta_tokens, norm_mix, norm_ffn, norm_final, w_in_even, w_gla_gate2, b_gla_gate, b_fox_f, g_gla_out, w_out_even, w_in_odd, b_mlstm_gate, g_mlstm_out, w_out_odd, w_router_group, b_router_group, w_router_expert, b_router_expert, w_exp_gate, w_exp_up, w_exp_down):
    nb, seq, d = x_prompt.shape
    db, ds, _ = x_sample.shape
    t = LEAD + seq
    npr, nsm = nb * t, db * ds
    n = npr + nsm
    dims = (nb, t, db, ds, npr, n)
    depth = norm_mix.shape[0]
    n_e = N_GROUPS * N_EXP
    f = w_exp_gate.shape[-1]

    pad_rows = jnp.zeros((N_PAD, d), F32)
    h = jnp.concatenate([piece for b in range(nb) for piece in (pad_rows, meta_tokens, x_prompt[b])]
                        + [x_sample.reshape(nsm, d)], axis=0)
    wg_all = w_exp_gate.reshape(depth * n_e, d, f)
    wu_all = w_exp_up.reshape(depth * n_e, d, f)
    wd_all = w_exp_down.reshape(depth * n_e, f, d)

    even, odd = [], []
    for l in range(depth):
        if l % 2 == 0:
            e = l // 2
            mix, st = _even_layer(h, dims, norm_mix[l], w_in_even[e], w_gla_gate2[e], b_gla_gate[e], b_fox_f[e],
                                  g_gla_out[e], w_out_even[e], state_gla[e], cache_fox_k[e], cache_fox_v[e],
                                  cache_fox_logf[e])
            even.append(st)
            hd, act = A_DV, "silu"
        else:
            o = l // 2
            mix, st = _odd_layer(h, dims, norm_mix[l], w_in_odd[o], b_mlstm_gate[o], g_mlstm_out[o], w_out_odd[o],
                                 state_mlstm_c[o], state_mlstm_n[o], state_mlstm_m[o])
            odd.append(st)
            hd, act = C_DV, "sigmoid"
        a, a_col, r, r_col, b, g_head, w_o = mix
        wr_hi, wr_lo, b_r = _router_weights(w_router_group[l], b_router_group[l], w_router_expert[l],
                                            b_router_expert[l])
        mix_args = (h, a, a_col, r, r_col, b, g_head, w_o, norm_ffn[l], wr_hi, wr_lo, b_r)
        h1, xn, info, counts = _mixout(*mix_args, jnp.zeros((1, LANES), F32), hd=hd, act=act, row0=0, n_rows=npr)
        h1, xn, info, counts = _mixout(*mix_args, counts, hd=hd, act=act, row0=npr, n_rows=nsm,
                                       prev=(h1, xn, info))
        last = l == depth - 1
        h = _moe(h1, xn, info, counts, wg_all, wu_all, wd_all, l * n_e, norm_final if last else norm_ffn[l],
                 final=dict(nb=nb, t=t, n_sample=nsm) if last else None)

    y_prompt = h[0].reshape(nb, seq, d)
    y_sample = h[1].reshape(db, ds, d)
    stack = lambda sts, key: jnp.stack([s[key] for s in sts])
    return (y_prompt, y_sample,
            stack(even, "s_p"), stack(even, "k_p"), stack(even, "v_p"), stack(even, "f_p"),
            stack(odd, "c_p"), stack(odd, "n_p"), stack(odd, "m_p"),
            stack(even, "s_s"), stack(even, "k_s"), stack(even, "v_s"), stack(even, "f_s"),
            stack(odd, "c_s"), stack(odd, "n_s"), stack(odd, "m_s"))
```

```python
import functools
import math

import jax
import jax.numpy as jnp
from jax import lax
from jax.experimental import pallas as pl
from jax.experimental.pallas import tpu as pltpu
from jax.experimental.pallas import tpu_sc as plsc

F32 = jnp.float32
BF16 = jnp.bfloat16

CHUNK = 64
N_META = 16
LEAD = 128
N_PAD = LEAD - N_META
A_HEADS, A_DK, A_DV, A_RANK = 4, 64, 128, 16
A_GATE_NORM = 16.0
B_HEADS, B_DH = 4, 128
C_HEADS, C_DQK, C_DV = 4, 128, 256
GATE_CAP = 15.0
N_GROUPS, N_EXP = 4, 8
EPS = 1e-6
NEG = -1e30
LOG2E = 1.4426950408889634
A_QK = A_HEADS * A_DK
A_V = A_HEADS * A_DV
B_W = B_HEADS * B_DH
C_QK = C_HEADS * C_DQK
C_V = C_HEADS * C_DV

LANES = 128
SUBLANES = 8
VMEM_LIMIT_BYTES = 56 * 1024 * 1024
GLA_SUB = 16
SC_CORES, SC_SUBCORES = 2, 16
SC_GATHER_ROWS = 16
SC_GATHER_BUFS = 4
MOE_TILE = 512
SC_SCATTER_ROWS = 16
SC_SCATTER_BUFS = 3
FLASH_HEADS = 2
MAIN_W = 3072
PROJ_W = MAIN_W + LANES

_NT = (((1,), (1,)), ((), ()))
_TN = (((0,), (0,)), ((), ()))
_NN = (((1,), (0,)), ((), ()))


def _params(sem):
    return pltpu.CompilerParams(dimension_semantics=sem, vmem_limit_bytes=VMEM_LIMIT_BYTES)


def _tile(n, pref, mult):
    t = (min(pref, n) // mult) * mult
    while t > mult and n % t:
        t -= mult
    assert t >= mult and n % t == 0, (n, pref, mult)
    return t


def _dot(a, b, dims=_NN):
    return lax.dot_general(a, b, dims, preferred_element_type=F32)


def _split(x):
    hi = x.astype(BF16)
    lo = (x - hi.astype(F32)).astype(BF16)
    return hi, lo


def _dot3(a, b, dims=_NN):
    ah, al = _split(a)
    bh, bl = _split(b)
    return _dot(ah, bh, dims) + _dot(ah, bl, dims) + _dot(al, bh, dims)


def _log_sigmoid(x):
    return jnp.minimum(x, 0.0) - jnp.log1p(jnp.exp(-jnp.abs(x)))


def _sigmoid(x):
    return 1.0 / (1.0 + jnp.exp(-x))


def _rms(x, g):
    return x * lax.rsqrt(jnp.mean(x * x, axis=-1, keepdims=True) + EPS) * g


def _pack_bf16_pairs(x):
    w = x.shape[1] // 2
    hi = lax.bitcast_convert_type(x[:, :w].astype(BF16).astype(F32), jnp.int32)
    lo = lax.bitcast_convert_type(x[:, w:].astype(BF16).astype(F32), jnp.int32)
    return hi | lax.shift_right_logical(lo, 16)


def _unpack_bf16_pairs(p):
    hi = lax.bitcast_convert_type(p & jnp.int32(-65536), F32)
    lo = lax.bitcast_convert_type(lax.shift_left(p, 16), F32)
    return jnp.concatenate([hi, lo], axis=1)


def _cumsum_rows(x):
    n = x.shape[0]
    row = lax.broadcasted_iota(jnp.int32, x.shape, 0)
    s = 1
    while s < n:
        x = x + jnp.where(row >= s, pltpu.roll(x, s, axis=0), 0.0)
        s *= 2
    return x


def _proj_kernel(x_ref, g_ref, w_ref, *rest, col_chunk, kv_col):
    if kv_col is None:
        (o_ref,) = rest
    else:
        cs_ref, o_ref, ob_ref, *kv_refs = rest
    tm = x_ref.shape[0]
    xn = _rms(x_ref[...], g_ref[...]).astype(BF16)
    for c0 in range(0, PROJ_W, col_chunk):
        c1 = min(c0 + col_chunk, PROJ_W)
        y = _dot(xn, w_ref[:, c0:c1])
        o_ref[:, c0:c1] = y
        if kv_col is not None:
            if c0 < MAIN_W:
                m1 = min(c1, MAIN_W)
                ob_ref[:, c0:m1] = (y[:, :m1 - c0] * cs_ref[:, c0:m1]).astype(BF16)
            for g0 in range(c0, c1, LANES):
                rel = g0 - kv_col
                if 0 <= rel < 2 * B_W:
                    head = (rel % B_W) // B_DH
                    kv_refs[rel // B_W][pl.ds(head, tm, stride=B_HEADS), :] = y[:, g0 - c0:g0 - c0 + LANES]


def _proj_precise_kernel(x_ref, g_ref, w_ref, prev_ref, o_ref, *, col_chunk):
    del prev_ref
    xn = _rms(x_ref[...], g_ref[...])
    for c0 in range(0, PROJ_W, col_chunk):
        c1 = min(c0 + col_chunk, PROJ_W)
        o_ref[:, c0:c1] = _dot3(xn, w_ref[:, c0:c1])


def _proj(h, g, w_packed, *, n_rows, colscale=None, kv_col=None):
    n, d = h.shape
    tm = _tile(n_rows, 512, 16)
    in_specs = [pl.BlockSpec((tm, d), lambda i: (i, 0)),
                pl.BlockSpec((1, d), lambda i: (0, 0)),
                pl.BlockSpec((d, PROJ_W), lambda i: (0, 0))]
    args = [h, g.reshape(1, d), w_packed.astype(BF16)]
    out_specs = [pl.BlockSpec((tm, PROJ_W), lambda i: (i, 0))]
    out_shape = [jax.ShapeDtypeStruct((n, PROJ_W), F32)]
    if kv_col is not None:
        in_specs.append(pl.BlockSpec((1, MAIN_W), lambda i: (0, 0)))
        args.append(colscale)
        out_specs += [pl.BlockSpec((tm, MAIN_W), lambda i: (i, 0))] + [pl.BlockSpec((tm * B_HEADS, B_DH), lambda i: (i, 0))] * 2
        out_shape += ([jax.ShapeDtypeStruct((n_rows, MAIN_W), BF16)]
                      + [jax.ShapeDtypeStruct((n_rows * B_HEADS, B_DH), F32)] * 2)
    return pl.pallas_call(
        functools.partial(_proj_kernel, col_chunk=640, kv_col=kv_col),
        grid=(n_rows // tm,),
        in_specs=in_specs,
        out_specs=out_specs,
        out_shape=out_shape,
        compiler_params=_params(("parallel",)),
        name="proj",
    )(*args)


def _proj_precise(h, g, w_packed, prev, *, row0):
    n, d = h.shape
    tm = _tile(n - row0, 512, SUBLANES)
    assert row0 % tm == 0
    return pl.pallas_call(
        functools.partial(_proj_precise_kernel, col_chunk=640),
        grid=((n - row0) // tm,),
        in_specs=[pl.BlockSpec((tm, d), lambda i: (row0 // tm + i, 0)),
                  pl.BlockSpec((1, d), lambda i: (0, 0)),
                  pl.BlockSpec((d, PROJ_W), lambda i: (0, 0)),
                  pl.BlockSpec(memory_space=pl.ANY)],
        out_specs=pl.BlockSpec((tm, PROJ_W), lambda i: (row0 // tm + i, 0)),
        out_shape=jax.ShapeDtypeStruct((n, PROJ_W), F32),
        input_output_aliases={3: 0},
        compiler_params=_params(("parallel",)),
        name="proj_precise",
    )(h, g.reshape(1, d), w_packed, prev)


def _gate_scan_kernel(x_ref, bias_ref, isf_ref, val_ref, cum_ref, *, mode, act_start, valid_start, valid_end, seg, stride):
    x = x_ref[...]
    lane = lax.broadcasted_iota(jnp.int32, x.shape, 1)
    valid = (lane >= valid_start) & (lane < valid_end)
    if mode == "fox":
        val = jnp.where(lane >= act_start, _log_sigmoid(x + bias_ref[...]), x)
        val = jnp.where(valid, val, 0.0)
        add = val
    else:
        gate = GATE_CAP * jnp.tanh((x + bias_ref[...]) / GATE_CAP)
        isf = isf_ref[...] > 0.5
        val = jnp.where(isf, jnp.where(valid, _log_sigmoid(gate), 0.0),
                        jnp.where(valid, gate, -jnp.inf))
        add = jnp.where(isf, val, 0.0)
    val_ref[...] = val
    n = x.shape[1]
    pos = lane if seg is None else lane % seg
    limit = n if seg is None else seg
    s = stride
    while s < limit:
        add = add + jnp.where(pos >= s, pltpu.roll(add, s, axis=1), 0.0)
        s *= 2
    cum_ref[...] = add


def _gate_scan(x, bias, isf, *, mode, act_start, valid_start, valid_end, seg, stride=1):
    r, n = x.shape
    full = lambda shape: pl.BlockSpec(shape, lambda i: (0,) * len(shape))
    return pl.pallas_call(
        functools.partial(_gate_scan_kernel, mode=mode, act_start=act_start,
                          valid_start=valid_start, valid_end=valid_end, seg=seg, stride=stride),
        grid=(1,),
        in_specs=[full((r, n)), full(bias.shape), full((r, 1))],
        out_specs=[full((r, n)), full((r, n))],
        out_shape=[jax.ShapeDtypeStruct((r, n), F32)] * 2,
        compiler_params=_params(("arbitrary",)),
        name="gate_scan_" + mode,
    )(x, bias, isf)


def _pad_lanes(x):
    n = x.shape[-1]
    m = -(-n // LANES) * LANES
    return x if m == n else jnp.pad(x, ((0, 0), (0, m - n)))


def _gla_kernel(qk_ref, v_ref, sm_ref, s0_ref, wg2_ref, wg2t_ref, bgr_ref, bgc_ref, *rest,
                L, sub, lead_pad, aliased, precise):
    if aliased:
        rest = rest[1:]
    o_ref, sout_ref, s_scr = rest
    c = pl.program_id(1)
    nh, dk, dv = A_HEADS, A_DK, A_DV
    cast = (lambda x: x) if precise else (lambda x: x.astype(BF16))
    mm = _dot3 if precise else _dot

    @pl.when(c == 0)
    def _():
        s_scr[...] = jnp.zeros_like(s_scr)
        for h in range(nh):
            s_scr[h * dk:(h + 1) * dk, h * dv:(h + 1) * dv] = s0_ref[0, h]

    qk = qk_ref[...]
    q = qk[:, :A_QK] * (A_DK ** -0.5)
    k = qk[:, A_QK:]
    v = v_ref[...]
    ga = sm_ref[:, :A_RANK]
    row = lax.broadcasted_iota(jnp.int32, (L, 1), 0)
    valid = (c * L + row) >= lead_pad
    z = _dot3(ga, wg2_ref[...]) + bgr_ref[...]
    loga = jnp.where(valid, _log_sigmoid(z) / A_GATE_NORM, 0.0)
    k = jnp.where(valid, k, 0.0)
    b = _cumsum_rows(loga)
    b_last = b[L - 1:L, :]
    lane_t = lax.broadcasted_iota(jnp.int32, (1, L), 1)
    zt = _dot3(wg2t_ref[...], ga, _NT) + bgc_ref[...]
    logat = jnp.where((c * L + lane_t) >= lead_pad, _log_sigmoid(zt) / A_GATE_NORM, 0.0)
    b_last_col = jnp.sum(logat, axis=1, keepdims=True)

    qhead = lax.broadcasted_iota(jnp.int32, (1, A_QK), 1) // dk
    vhead = lax.broadcasted_iota(jnp.int32, (1, A_V), 1) // dv
    vb = cast(v)
    zero_b = jnp.zeros((), vb.dtype)
    v_bd = jnp.concatenate([jnp.where(vhead == h, vb, zero_b) for h in range(nh)], axis=0)

    rows_all = lax.broadcasted_iota(jnp.int32, (L, 1), 0)
    a_rows = []
    for i in range(L // sub):
        r0 = i * sub
        ci = jnp.zeros((1, A_QK), F32) if i == 0 else b[r0 - 1:r0, :]
        qt = cast(q[r0:r0 + sub] * jnp.exp(b[r0:r0 + sub] - ci))
        kt = cast(jnp.where(rows_all < r0 + sub, k * jnp.exp(ci - b), 0.0))
        k_stack = jnp.concatenate([jnp.where(qhead == h, kt, zero_b) for h in range(nh)], axis=0)
        a_rows.append(mm(qt, k_stack, _NT))
    a = a_rows[0] if len(a_rows) == 1 else jnp.concatenate(a_rows, axis=0)
    t_idx = lax.broadcasted_iota(jnp.int32, (L, nh * L), 0)
    s_idx = lax.broadcasted_iota(jnp.int32, (L, nh * L), 1) % L
    a = jnp.where(s_idx <= t_idx, a, 0.0)
    o_intra = mm(cast(a), v_bd)

    s_full = s_scr[...]
    o_inter = mm(cast(q * jnp.exp(b)), cast(s_full))
    o_ref[...] = o_inter + o_intra

    k_hat = cast(k * jnp.exp(b_last - b))
    upd = mm(k_hat, vb, _TN)
    khead_col = lax.broadcasted_iota(jnp.int32, (A_QK, 1), 0) // dk
    s_new = jnp.exp(b_last_col) * s_full + jnp.where(khead_col == vhead, upd, 0.0)
    s_scr[...] = s_new

    @pl.when(c == pl.num_programs(1) - 1)
    def _():
        for h in range(nh):
            sout_ref[0, h] = s_new[h * dk:(h + 1) * dk, h * dv:(h + 1) * dv]


def _gla(p, s0, wg2, bg, *, nb, nc, L, row_block0, lead_pad, n_total, prev_out=None, precise=False):
    aliased = prev_out is not None
    rb = lambda b, c: row_block0 + b * nc + c
    in_specs = [pl.BlockSpec((L, 2 * A_QK), lambda b, c: (rb(b, c), 0)),
                pl.BlockSpec((L, A_V), lambda b, c: (rb(b, c), 2 * A_QK // A_V)),
                pl.BlockSpec((L, LANES), lambda b, c: (rb(b, c), MAIN_W // LANES)),
                pl.BlockSpec((1, A_HEADS, A_DK, A_DV), lambda b, c: (b, 0, 0, 0)),
                pl.BlockSpec((A_RANK, A_QK), lambda b, c: (0, 0)),
                pl.BlockSpec((A_QK, A_RANK), lambda b, c: (0, 0)),
                pl.BlockSpec((1, A_QK), lambda b, c: (0, 0)),
                pl.BlockSpec((A_QK, 1), lambda b, c: (0, 0))]
    args = [p, p, p, s0, wg2, wg2.T, bg.reshape(1, A_QK), bg.reshape(A_QK, 1)]
    io_alias = {}
    if aliased:
        in_specs.append(pl.BlockSpec(memory_space=pl.ANY))
        args.append(prev_out)
        io_alias = {len(args) - 1: 0}
    return pl.pallas_call(
        functools.partial(_gla_kernel, L=L, sub=min(GLA_SUB, L), lead_pad=lead_pad, aliased=aliased,
                          precise=precise),
        grid=(nb, nc),
        in_specs=in_specs,
        out_specs=[pl.BlockSpec((L, A_V), lambda b, c: (rb(b, c), 0)),
                   pl.BlockSpec((1, A_HEADS, A_DK, A_DV), lambda b, c: (b, 0, 0, 0))],
        out_shape=[jax.ShapeDtypeStruct((n_total, A_V), F32),
                   jax.ShapeDtypeStruct((nb, A_HEADS, A_DK, A_DV), F32)],
        scratch_shapes=[pltpu.VMEM((A_QK, A_V), F32)],
        input_output_aliases=io_alias,
        compiler_params=_params(("parallel", "arbitrary")),
        name="gla_L%d" % L,
    )(*args)


def _flash_kernel(qi_ref, kj_ref, q_ref, k_ref, v_ref, fk_ref, o_ref, m_scr, l_scr, acc_scr, *, blk, lead_pad):
    step = pl.program_id(2)
    i = qi_ref[step]
    j = kj_ref[step]

    @pl.when(j == 0)
    def _():
        m_scr[...] = jnp.full_like(m_scr, -jnp.inf)
        l_scr[...] = jnp.zeros_like(l_scr)
        acc_scr[...] = jnp.zeros_like(acc_scr)

    def update(masked):
        for g in range(FLASH_HEADS):
            sl = slice(g * B_DH, (g + 1) * B_DH)
            s = _dot(q_ref[:, sl], k_ref[:, sl], _NT) - fk_ref[0, g]
            if masked:
                qpos = i * blk + lax.broadcasted_iota(jnp.int32, (blk, blk), 0)
                kpos = j * blk + lax.broadcasted_iota(jnp.int32, (blk, blk), 1)
                s = jnp.where((kpos <= qpos) & (kpos >= lead_pad), s, NEG)
            m_prev = m_scr[g]
            m_new = jnp.maximum(m_prev, jnp.max(s, axis=1, keepdims=True))
            alpha = jnp.exp2(m_prev - m_new)
            p = jnp.exp2(s - m_new)
            l_scr[g] = alpha * l_scr[g] + jnp.sum(p, axis=1, keepdims=True)
            acc_scr[g] = alpha * acc_scr[g] + _dot(p.astype(BF16), v_ref[:, sl])
            m_scr[g] = m_new

    edge = (j == i) | (j == 0)
    pl.when(edge)(functools.partial(update, True))
    pl.when(jnp.logical_not(edge))(functools.partial(update, False))

    @pl.when(j == i)
    def _():
        for g in range(FLASH_HEADS):
            o_ref[:, g * B_DH:(g + 1) * B_DH] = acc_scr[g] / l_scr[g]


def _flash(pb, fk, *, nb, t, n_total):
    blk = _tile(t, 640, LANES)
    nq = t // blk
    hg = FLASH_HEADS
    w = hg * B_DH
    pairs = [(i, j) for i in range(nq) for j in range(i + 1)]
    qi = jnp.asarray([p[0] for p in pairs], jnp.int32)
    kj = jnp.asarray([p[1] for p in pairs], jnp.int32)
    qc, kc, vc = (A_QK * 2 + A_V * 2) // w, (A_QK * 2 + A_V * 2 + B_W) // w, (A_QK * 2 + A_V * 2 + 2 * B_W) // w
    grid_spec = pltpu.PrefetchScalarGridSpec(
        num_scalar_prefetch=2,
        grid=(nb, B_HEADS // hg, len(pairs)),
        in_specs=[pl.BlockSpec((blk, w), lambda b, h, s, qi, kj: (b * nq + qi[s], qc + h)),
                  pl.BlockSpec((blk, w), lambda b, h, s, qi, kj: (b * nq + kj[s], kc + h)),
                  pl.BlockSpec((blk, w), lambda b, h, s, qi, kj: (b * nq + kj[s], vc + h)),
                  pl.BlockSpec((1, hg, 1, blk), lambda b, h, s, qi, kj: (b, h, 0, kj[s]))],
        out_specs=pl.BlockSpec((blk, w), lambda b, h, s, qi, kj: (b * nq + qi[s], h)),
        scratch_shapes=[pltpu.VMEM((hg, blk, 1), F32), pltpu.VMEM((hg, blk, 1), F32),
                        pltpu.VMEM((hg, blk, B_DH), F32)],
    )
    return pl.pallas_call(
        functools.partial(_flash_kernel, blk=blk, lead_pad=N_PAD),
        grid_spec=grid_spec,
        out_shape=jax.ShapeDtypeStruct((n_total, B_W), F32),
        compiler_params=_params(("parallel", "parallel", "arbitrary")),
        name="fox_flash",
    )(qi, kj, pb, pb, pb, fk)


def _fox_sample_kernel(q_ref, kn_ref, vn_ref, kc_ref, vc_ref, fc_ref, fn_ref, prev_ref, o_ref, *, ds):
    del prev_ref
    nh = B_HEADS
    stack = lambda ref: jnp.concatenate([ref[:, h * B_DH:(h + 1) * B_DH] for h in range(nh)], axis=0)
    q = stack(q_ref) * (B_DH ** -0.5 * LOG2E)
    rows = nh * ds
    n_c = kc_ref.shape[1]
    qh_c = lax.broadcasted_iota(jnp.int32, (rows, n_c), 0) // ds
    kh_c = lax.broadcasted_iota(jnp.int32, (rows, n_c), 1) % nh
    s_c = _dot3(q, kc_ref[0], _NT) - fc_ref[0]
    s_c = jnp.where(qh_c == kh_c, s_c, NEG)
    r_i = lax.broadcasted_iota(jnp.int32, (rows, rows), 0)
    c_i = lax.broadcasted_iota(jnp.int32, (rows, rows), 1)
    s_n = _dot3(q, stack(kn_ref), _NT) - fn_ref[0]
    s_n = jnp.where((r_i // ds == c_i // ds) & (c_i % ds <= r_i % ds), s_n, NEG)
    m = jnp.maximum(jnp.max(s_c, axis=1, keepdims=True), jnp.max(s_n, axis=1, keepdims=True))
    p_c = jnp.exp2(s_c - m)
    p_n = jnp.exp2(s_n - m)
    l = jnp.sum(p_c, axis=1, keepdims=True) + jnp.sum(p_n, axis=1, keepdims=True)
    o = (_dot3(p_c, vc_ref[0]) + _dot3(p_n, stack(vn_ref))) / l
    for h in range(nh):
        o_ref[:, h * B_DH:(h + 1) * B_DH] = o[h * ds:(h + 1) * ds]


def _fox_sample(pb, kc, vc, f_cache, f_new, prev_out, *, nb, ds, row_block0):
    n_c = kc.shape[1]
    base = (A_QK * 2 + A_V * 2) // B_W
    rb = lambda b: row_block0 + b
    return pl.pallas_call(
        functools.partial(_fox_sample_kernel, ds=ds),
        grid=(nb,),
        in_specs=[pl.BlockSpec((ds, B_W), lambda b: (rb(b), base)),
                  pl.BlockSpec((ds, B_W), lambda b: (rb(b), base + 1)),
                  pl.BlockSpec((ds, B_W), lambda b: (rb(b), base + 2)),
                  pl.BlockSpec((1, n_c, B_DH), lambda b: (b, 0, 0)),
                  pl.BlockSpec((1, n_c, B_DH), lambda b: (b, 0, 0)),
                  pl.BlockSpec((1, 1, n_c), lambda b: (b, 0, 0)),
                  pl.BlockSpec((1, 1, B_HEADS * ds), lambda b: (b, 0, 0)),
                  pl.BlockSpec(memory_space=pl.ANY)],
        out_specs=pl.BlockSpec((ds, B_W), lambda b: (rb(b), 0)),
        out_shape=jax.ShapeDtypeStruct(prev_out.shape, F32),
        input_output_aliases={7: 0},
        compiler_params=_params(("parallel",)),
        name="fox_sample",
    )(pb, pb, pb, kc, vc, f_cache, f_new, prev_out)


def _mlstm_kernel(q_ref, k_ref, v_ref, lir_ref, br_ref, lic_ref, bc_ref, c0_ref, n0_ref, m0_ref, *rest,
                  L, aliased):
    if aliased:
        rest = rest[1:]
    h_ref, cout_ref, nout_ref, mout_ref, c_scr, n_scr, m_scr = rest
    c = pl.program_id(1)

    @pl.when(c == 0)
    def _():
        c_scr[...] = c0_ref[0]
        n_scr[...] = n0_ref[0]
        m_scr[...] = m0_ref[0]

    nh = C_HEADS
    hl = nh * L
    seg = lax.broadcasted_iota(jnp.int32, (1, hl), 1) // L
    causal = (lax.broadcasted_iota(jnp.int32, (L, hl), 1) % L) <= lax.broadcasted_iota(jnp.int32, (L, hl), 0)

    def per_head(vals):
        out = vals[0]
        for h in range(1, nh):
            out = jnp.where(seg == h, vals[h], out)
        return out

    seg_max = lambda x, h: jnp.max(jnp.where(seg == h, x, -jnp.inf), axis=1, keepdims=True)

    qf = q_ref[...]
    kf = k_ref[...] * (C_DQK ** -0.5)
    qb = qf.astype(BF16)
    kb = kf.astype(BF16)
    vb = v_ref[...].astype(BF16)
    b_row = br_ref[0, 0]
    li_row = lir_ref[0, 0]
    b_col = [bc_ref[0, 0, :, h:h + 1] for h in range(nh)]
    li_col = [lic_ref[0, 0, :, h:h + 1] for h in range(nh)]
    m_prev = [m_scr[h:h + 1, :] for h in range(nh)]

    d = jnp.where(causal, per_head(b_col) - b_row + li_row, -jnp.inf)
    inter = [b_col[h] + m_prev[h] for h in range(nh)]
    m_t = [jnp.maximum(inter[h], seg_max(d, h)) for h in range(nh)]
    pm = jnp.exp(d - per_head(m_t))
    w_inter = [jnp.exp(inter[h] - m_t[h]) for h in range(nh)]

    khead = lax.broadcasted_iota(jnp.int32, (1, C_QK), 1) // C_DQK
    vhead = lax.broadcasted_iota(jnp.int32, (1, C_V), 1) // C_DV
    zero_b = jnp.zeros((), BF16)
    k_stack = jnp.concatenate([jnp.where(khead == h, kb, zero_b) for h in range(nh)], axis=0)
    v_bd = jnp.concatenate([jnp.where(vhead == h, vb, zero_b) for h in range(nh)], axis=0)
    sqk = _dot(qb, k_stack, _NT) * pm
    sv = _dot(sqk.astype(BF16), v_bd)

    b_last = [b_row[:, h * L + L - 1:h * L + L] for h in range(nh)]
    g_row = per_head(b_last) - b_row + li_row
    for h in range(nh):
        qk_sl = slice(h * C_DQK, (h + 1) * C_DQK)
        v_sl = slice(h * C_DV, (h + 1) * C_DV)
        c_prev = c_scr[h]
        n_prev = n_scr[h:h + 1, :]
        num = w_inter[h] * _dot(qb[:, qk_sl], c_prev.astype(BF16), _NT) + sv[:, v_sl]
        den = (w_inter[h] * jnp.sum(qf[:, qk_sl] * n_prev, axis=1, keepdims=True)
               + jnp.sum(jnp.where(seg == h, sqk, 0.0), axis=1, keepdims=True))
        h_ref[:, v_sl] = num / jnp.maximum(jnp.abs(den), jnp.exp(-m_t[h]))

        m_new = jnp.maximum(b_last[h] + m_prev[h], seg_max(g_row, h))
        w_c = jnp.exp(b_last[h] + m_prev[h] - m_new)
        kw = kf[:, qk_sl] * jnp.exp(b_last[h] - b_col[h] + li_col[h] - m_new)
        c_scr[h] = w_c * c_prev + _dot(vb[:, v_sl], kw.astype(BF16), _TN)
        n_scr[h:h + 1, :] = w_c * n_prev + jnp.sum(kw, axis=0, keepdims=True)
        m_scr[h:h + 1, :] = m_new

    @pl.when(c == pl.num_programs(1) - 1)
    def _():
        cout_ref[0] = c_scr[...]
        nout_ref[0] = n_scr[...]
        mout_ref[0] = m_scr[...]


def _mlstm(p, li_row, b_row, li_col, b_col, c0, n0, m0, *, nb, nc, L, row_block0, n_total, prev_out=None):
    aliased = prev_out is not None
    rb = lambda b, c: row_block0 + b * nc + c
    in_specs = [pl.BlockSpec((L, C_QK), lambda b, c: (rb(b, c), 0)),
                pl.BlockSpec((L, C_QK), lambda b, c: (rb(b, c), 1)),
                pl.BlockSpec((L, C_V), lambda b, c: (rb(b, c), 2 * C_QK // C_V)),
                pl.BlockSpec((1, 1, 1, C_HEADS * L), lambda b, c: (b, c, 0, 0)),
                pl.BlockSpec((1, 1, 1, C_HEADS * L), lambda b, c: (b, c, 0, 0)),
                pl.BlockSpec((1, 1, L, C_HEADS), lambda b, c: (b, c, 0, 0)),
                pl.BlockSpec((1, 1, L, C_HEADS), lambda b, c: (b, c, 0, 0)),
                pl.BlockSpec((1, C_HEADS, C_DV, C_DQK), lambda b, c: (b, 0, 0, 0)),
                pl.BlockSpec((1, C_HEADS, C_DQK), lambda b, c: (b, 0, 0)),
                pl.BlockSpec((1, C_HEADS, 1), lambda b, c: (b, 0, 0))]
    args = [p, p, p, li_row, b_row, li_col, b_col, c0, n0, m0]
    io_alias = {}
    if aliased:
        in_specs.append(pl.BlockSpec(memory_space=pl.ANY))
        args.append(prev_out)
        io_alias = {len(args) - 1: 0}
    return pl.pallas_call(
        functools.partial(_mlstm_kernel, L=L, aliased=aliased),
        grid=(nb, nc),
        in_specs=in_specs,
        out_specs=[pl.BlockSpec((L, C_V), lambda b, c: (rb(b, c), 0)),
                   pl.BlockSpec((1, C_HEADS, C_DV, C_DQK), lambda b, c: (b, 0, 0, 0)),
                   pl.BlockSpec((1, C_HEADS, C_DQK), lambda b, c: (b, 0, 0)),
                   pl.BlockSpec((1, C_HEADS, 1), lambda b, c: (b, 0, 0))],
        out_shape=[jax.ShapeDtypeStruct((n_total, C_V), F32),
                   jax.ShapeDtypeStruct((nb, C_HEADS, C_DV, C_DQK), F32),
                   jax.ShapeDtypeStruct((nb, C_HEADS, C_DQK), F32),
                   jax.ShapeDtypeStruct((nb, C_HEADS, 1), F32)],
        scratch_shapes=[pltpu.VMEM((C_HEADS, C_DV, C_DQK), F32),
                        pltpu.VMEM((C_HEADS, C_DQK), F32),
                        pltpu.VMEM((C_HEADS, 1), F32)],
        input_output_aliases=io_alias,
        compiler_params=_params(("parallel", "arbitrary")),
        name="mlstm_L%d" % L,
    )(*args)


def _mixout_kernel(*refs, hd, act, has_b, precise, n_prev):
    refs = list(refs)
    h_ref, a_ref, r_ref = refs[:3]
    b_ref = refs[3] if has_b else None
    k = 4 if has_b else 3
    ga_ref, wo_ref, gf_ref, wrh_ref, wrl_ref, br_ref, cnt0_ref = refs[k:k + 7]
    h1_ref, xn_ref, info_ref, cnt_ref, cnt_scr = refs[k + 7 + n_prev:]
    cast = (lambda x: x) if precise else (lambda x: x.astype(BF16))
    a = a_ref[...]
    r = r_ref[...]
    gate = r * _sigmoid(r) if act == "silu" else _sigmoid(r)
    parts = []
    for hh in range(a.shape[1] // hd):
        sl = slice(hh * hd, (hh + 1) * hd)
        parts.append(cast(_rms(a[:, sl], ga_ref[...]) * gate[:, sl]))
    if has_b:
        parts.append(cast(b_ref[...]))
    cat = jnp.concatenate(parts, axis=1)
    h1 = h_ref[...] + (_dot3(cat, wo_ref[...]) if precise else _dot(cat, wo_ref[...]))
    h1_ref[...] = h1
    xn = _rms(h1, gf_ref[...])
    xn_ref[...] = _pack_bf16_pairs(xn)
    xh, xl = _split(xn)
    logits = _dot(xh, wrh_ref[...]) + _dot(xh, wrl_ref[...]) + _dot(xl, wrh_ref[...]) + br_ref[...]

    lane = lax.broadcasted_iota(jnp.int32, logits.shape, 1)
    lanef = lane.astype(F32)
    is_g = lane < N_GROUPS
    gl = jnp.where(is_g, logits, -jnp.inf)
    gmax = jnp.max(gl, axis=1, keepdims=True)
    gidx = jnp.min(jnp.where(gl == gmax, lanef, float(LANES)), axis=1, keepdims=True)
    wg = 1.0 / jnp.sum(jnp.where(is_g, jnp.exp(gl - gmax), 0.0), axis=1, keepdims=True)
    lo = N_GROUPS + N_EXP * gidx
    el = jnp.where((lanef >= lo) & (lanef < lo + N_EXP), logits, -jnp.inf)
    m1 = jnp.max(el, axis=1, keepdims=True)
    i1 = jnp.min(jnp.where(el == m1, lanef, float(LANES)), axis=1, keepdims=True)
    el2 = jnp.where(lanef == i1, -jnp.inf, el)
    m2 = jnp.max(el2, axis=1, keepdims=True)
    i2 = jnp.min(jnp.where(el2 == m2, lanef, float(LANES)), axis=1, keepdims=True)
    t = jnp.exp(m2 - m1)
    w1 = wg / (1.0 + t)
    w2 = wg * t / (1.0 + t)
    e1 = i1 - N_GROUPS
    e2 = i2 - N_GROUPS

    @pl.when(pl.program_id(0) == 0)
    def _():
        cnt_scr[...] = cnt0_ref[...]

    tm = logits.shape[0]
    pick = jnp.where((lanef == e1) | (lanef == e2), 1.0, 0.0)
    earlier = (lax.broadcasted_iota(jnp.int32, (tm, tm), 1) < lax.broadcasted_iota(jnp.int32, (tm, tm), 0))
    before = _dot(jnp.where(earlier, 1.0, 0.0).astype(BF16), pick.astype(BF16)) + cnt_scr[...]
    r1 = jnp.sum(jnp.where(lanef == e1, before, 0.0), axis=1, keepdims=True)
    r2 = jnp.sum(jnp.where(lanef == e2, before, 0.0), axis=1, keepdims=True)
    cnt_new = cnt_scr[...] + jnp.sum(pick, axis=0, keepdims=True)
    cnt_scr[...] = cnt_new
    cnt_ref[...] = cnt_new
    info_ref[...] = jnp.where(lane == 0, e1, jnp.where(lane == 1, e2, jnp.where(lane == 2, w1, jnp.where(
        lane == 3, w2, jnp.where(lane == 4, r1, jnp.where(lane == 5, r2, 0.0))))))


def _mixout(h, a, a_col, r, r_col, b, g_head, w_o, g_ffn, wr_hi, wr_lo, b_r, counts0, *, hd, act,
            row0, n_rows, prev=None):
    n, d = h.shape
    precise = prev is not None
    tm = _tile(n_rows, 512, SUBLANES)
    assert row0 % tm == 0
    blk0 = row0 // tm
    wa = w_o.shape[0] if b is None else w_o.shape[0] - B_W
    has_b = b is not None
    row = lambda i: (blk0 + i, 0)
    const = lambda i: (0, 0)
    in_specs = [pl.BlockSpec((tm, d), row),
                pl.BlockSpec((tm, wa), lambda i: (blk0 + i, a_col)),
                pl.BlockSpec((tm, wa), lambda i: (blk0 + i, r_col))]
    args = [h, a, r]
    if has_b:
        in_specs.append(pl.BlockSpec((tm, B_W), row))
        args.append(b)
    in_specs += [pl.BlockSpec((1, hd), const), pl.BlockSpec(w_o.shape, const), pl.BlockSpec((1, d), const),
                 pl.BlockSpec((d, LANES), const), pl.BlockSpec((d, LANES), const), pl.BlockSpec((1, LANES), const),
                 pl.BlockSpec((1, LANES), const)]
    args += [g_head.reshape(1, hd), w_o if precise else w_o.astype(BF16), g_ffn.reshape(1, d), wr_hi, wr_lo, b_r,
             counts0]
    io_alias = {}
    if precise:
        for k, arr in enumerate(prev):
            in_specs.append(pl.BlockSpec(memory_space=pl.ANY))
            args.append(arr)
            io_alias[len(args) - 1] = k
    return pl.pallas_call(
        functools.partial(_mixout_kernel, hd=hd, act=act, has_b=has_b, precise=precise, n_prev=len(io_alias)),
        grid=(n_rows // tm,),
        in_specs=in_specs,
        out_specs=[pl.BlockSpec((tm, d), row), pl.BlockSpec((tm, d // 2), row), pl.BlockSpec((tm, LANES), row),
                   pl.BlockSpec((1, LANES), const)],
        out_shape=[jax.ShapeDtypeStruct((n, d), F32), jax.ShapeDtypeStruct((n, d // 2), jnp.int32),
                   jax.ShapeDtypeStruct((n, LANES), F32), jax.ShapeDtypeStruct((1, LANES), F32)],
        scratch_shapes=[pltpu.VMEM((1, LANES), F32)],
        input_output_aliases=io_alias,
        compiler_params=_params(("arbitrary",)),
        name="mixout_" + act + ("_precise" if precise else ""),
    )(*args)


def _sc_gather(table, idx):
    r = idx.shape[0]
    w = table.shape[1]
    n_workers = SC_CORES * SC_SUBCORES
    per_worker = r // n_workers
    step = SC_GATHER_ROWS * SC_GATHER_BUFS
    assert r % n_workers == 0 and per_worker % step == 0, (r, n_workers, step)
    mesh = plsc.VectorSubcoreMesh(core_axis_name="c", subcore_axis_name="s")

    @functools.partial(
        pl.kernel, mesh=mesh,
        out_type=jax.ShapeDtypeStruct((r, w), table.dtype),
        scratch_types=[pltpu.VMEM((SC_GATHER_BUFS, SC_GATHER_ROWS), jnp.int32),
                       pltpu.VMEM((SC_GATHER_BUFS, SC_GATHER_ROWS, w), table.dtype),
                       pltpu.SemaphoreType.DMA((SC_GATHER_BUFS,)),
                       pltpu.SemaphoreType.DMA((SC_GATHER_BUFS,))],
    )
    def gather(table_hbm, idx_hbm, out_hbm, idx_v, rows_v, gather_sem, store_sem):
        worker = lax.axis_index("s") * SC_CORES + lax.axis_index("c")
        base = worker * per_worker

        @pl.loop(0, per_worker // step)
        def _(j):
            off = pl.multiple_of(base + j * step, step)
            rows = lambda b: pl.ds(off + b * SC_GATHER_ROWS, SC_GATHER_ROWS)
            gathers, stores = [], []
            for b in range(SC_GATHER_BUFS):
                pltpu.sync_copy(idx_hbm.at[rows(b)], idx_v.at[b])
                gathers.append(pltpu.async_copy(table_hbm.at[idx_v.at[b]], rows_v.at[b], gather_sem.at[b]))
            for b in range(SC_GATHER_BUFS):
                gathers[b].wait()
                stores.append(pltpu.async_copy(rows_v.at[b], out_hbm.at[rows(b)], store_sem.at[b]))
            for b in range(SC_GATHER_BUFS):
                stores[b].wait()

    return gather(table, idx)


def _sc_scatter_pairs(rows, slots, n_slots):
    n, w = rows.shape
    n_workers = SC_CORES * SC_SUBCORES
    per_worker = n // n_workers
    step = SC_SCATTER_ROWS * SC_SCATTER_BUFS
    assert n % n_workers == 0 and per_worker % step == 0, (n, n_workers, step)
    mesh = plsc.VectorSubcoreMesh(core_axis_name="c", subcore_axis_name="s")

    @functools.partial(
        pl.kernel, mesh=mesh,
        out_type=jax.ShapeDtypeStruct((n_slots, w), rows.dtype),
        scratch_types=[pltpu.VMEM((2 * SC_SCATTER_BUFS, SC_SCATTER_ROWS), jnp.int32),
                       pltpu.VMEM((SC_SCATTER_BUFS, SC_SCATTER_ROWS, w), rows.dtype),
                       pltpu.SemaphoreType.DMA((SC_SCATTER_BUFS,)),
                       pltpu.SemaphoreType.DMA((SC_SCATTER_BUFS,))],
    )
    def scatter(rows_hbm, slots_hbm, out_hbm, idx_v, rows_v, load_sem, store_sem):
        worker = lax.axis_index("s") * SC_CORES + lax.axis_index("c")
        base = worker * per_worker

        @pl.loop(0, per_worker // step)
        def _(j):
            off = pl.multiple_of(base + j * step, SC_SCATTER_ROWS)
            loads, stores = [], []
            for b in range(SC_SCATTER_BUFS):
                r0 = off + b * SC_SCATTER_ROWS
                pltpu.sync_copy(slots_hbm.at[pl.ds(r0, SC_SCATTER_ROWS)], idx_v.at[2 * b])
                pltpu.sync_copy(slots_hbm.at[pl.ds(n + r0, SC_SCATTER_ROWS)], idx_v.at[2 * b + 1])
                loads.append(pltpu.async_copy(rows_hbm.at[pl.ds(r0, SC_SCATTER_ROWS)], rows_v.at[b], load_sem.at[b]))
            for b in range(SC_SCATTER_BUFS):
                loads[b].wait()
                for k in range(2):
                    stores.append(pltpu.async_copy(rows_v.at[b], out_hbm.at[idx_v.at[2 * b + k]], store_sem.at[b]))
            for copy in stores:
                copy.wait()

    return scatter(rows, slots)


def _expert_kernel(te_ref, nu_ref, x_ref, wg_ref, wu_ref, wd_ref, y_ref, wgb, wub, wdb):
    i = pl.program_id(0)
    live = i < nu_ref[0]
    new_expert = (i == 0) | (te_ref[i] != te_ref[jnp.maximum(i - 1, 0)])

    @pl.when(live & new_expert)
    def _():
        wgb[...] = wg_ref[0].astype(BF16)
        wub[...] = wu_ref[0].astype(BF16)
        wdb[...] = wd_ref[0].astype(BF16)

    @pl.when(live)
    def _():
        x = _unpack_bf16_pairs(x_ref[...]).astype(BF16)
        g = _dot(x, wgb[...])
        u = _dot(x, wub[...])
        y_ref[...] = _pack_bf16_pairs(_dot((g * _sigmoid(g) * u).astype(BF16), wdb[...]))

    @pl.when(jnp.logical_not(live))
    def _():
        y_ref[...] = jnp.zeros_like(y_ref)


def _experts(xs, tile_expert, n_used, w_gate, w_up, w_down, *, tm):
    n_slots, dp = xs.shape
    d, f = w_gate.shape[-2:]
    grid_spec = pltpu.PrefetchScalarGridSpec(
        num_scalar_prefetch=2,
        grid=(n_slots // tm,),
        in_specs=[pl.BlockSpec((tm, dp), lambda i, te, nu: (i, 0)),
                  pl.BlockSpec((1, d, f), lambda i, te, nu: (te[i], 0, 0)),
                  pl.BlockSpec((1, d, f), lambda i, te, nu: (te[i], 0, 0)),
                  pl.BlockSpec((1, f, d), lambda i, te, nu: (te[i], 0, 0))],
        out_specs=pl.BlockSpec((tm, dp), lambda i, te, nu: (i, 0)),
        scratch_shapes=[pltpu.VMEM((d, f), BF16), pltpu.VMEM((d, f), BF16), pltpu.VMEM((f, d), BF16)],
    )
    return pl.pallas_call(
        _expert_kernel,
        grid_spec=grid_spec,
        out_shape=jax.ShapeDtypeStruct((n_slots, dp), jnp.int32),
        compiler_params=_params(("arbitrary",)),
        name="moe_experts",
    )(tile_expert, n_used, xs, w_gate, w_up, w_down)


def _combine_kernel(h_ref, info_ref, g_ref, y0_ref, y1_ref, o_ref, *, final_norm):
    h2 = h_ref[...] + (info_ref[:, 2:3] * _unpack_bf16_pairs(y0_ref[...])
                       + info_ref[:, 3:4] * _unpack_bf16_pairs(y1_ref[...]))
    o_ref[...] = _rms(h2, g_ref[...]) if final_norm else h2


def _combine(h, yg, info, g, *, final_norm):
    n, d = h.shape
    tm = _tile(n, 512, SUBLANES)
    nt = n // tm
    return pl.pallas_call(
        functools.partial(_combine_kernel, final_norm=final_norm),
        grid=(nt,),
        in_specs=[pl.BlockSpec((tm, d), lambda i: (i, 0)),
                  pl.BlockSpec((tm, LANES), lambda i: (i, 0)),
                  pl.BlockSpec((1, d), lambda i: (0, 0)),
                  pl.BlockSpec((tm, d // 2), lambda i: (i, 0)),
                  pl.BlockSpec((tm, d // 2), lambda i: (nt + i, 0))],
        out_specs=pl.BlockSpec((tm, d), lambda i: (i, 0)),
        out_shape=jax.ShapeDtypeStruct((n, d), F32),
        compiler_params=_params(("parallel",)),
        name="moe_combine",
    )(h, info, g.reshape(1, d), yg, yg)


def _final_kernel(h_ref, info_ref, g_ref, y0_ref, y1_ref, yp_ref, ys_ref, *, tiles_per_batch, lead_tiles, prompt_tiles):
    i = pl.program_id(0)
    h2 = h_ref[...] + (info_ref[:, 2:3] * _unpack_bf16_pairs(y0_ref[...])
                       + info_ref[:, 3:4] * _unpack_bf16_pairs(y1_ref[...]))
    out = _rms(h2, g_ref[...])

    @pl.when((i < prompt_tiles) & (i % tiles_per_batch >= lead_tiles))
    def _():
        yp_ref[...] = out

    @pl.when(i >= prompt_tiles)
    def _():
        ys_ref[...] = out


def _final_combine(h, yg, info, g, *, nb, t, n_sample):
    n, d = h.shape
    tm = _tile(math.gcd(LEAD, n_sample), LANES, SUBLANES)
    nt = n // tm
    tpb, lead_tiles = t // tm, LEAD // tm
    prompt_tiles = nb * tpb
    keep = tpb - lead_tiles

    def prompt_block(i):
        ip = jnp.minimum(i, prompt_tiles - 1)
        return (ip // tpb) * keep + jnp.maximum(ip % tpb - lead_tiles, 0), 0

    return pl.pallas_call(
        functools.partial(_final_kernel, tiles_per_batch=tpb, lead_tiles=lead_tiles, prompt_tiles=prompt_tiles),
        grid=(nt,),
        in_specs=[pl.BlockSpec((tm, d), lambda i: (i, 0)),
                  pl.BlockSpec((tm, LANES), lambda i: (i, 0)),
                  pl.BlockSpec((1, d), lambda i: (0, 0)),
                  pl.BlockSpec((tm, d // 2), lambda i: (i, 0)),
                  pl.BlockSpec((tm, d // 2), lambda i: (nt + i, 0))],
        out_specs=[pl.BlockSpec((tm, d), prompt_block),
                   pl.BlockSpec((tm, d), lambda i: (jnp.maximum(i - prompt_tiles, 0), 0))],
        out_shape=[jax.ShapeDtypeStruct((nb * keep * tm, d), F32), jax.ShapeDtypeStruct((n_sample, d), F32)],
        compiler_params=_params(("arbitrary",)),
        name="moe_combine_final",
    )(h, info, g.reshape(1, d), yg, yg)


def _moe(h1, xn, info, counts_f, w_gate, w_up, w_down, expert_base, g_next, *, final=None):
    n, d = h1.shape
    n_e = N_GROUPS * N_EXP
    tm = _tile(2 * n, MOE_TILE, SUBLANES)
    n_tiles = (2 * n) // tm + n_e
    n_slots = n_tiles * tm
    experts = jnp.arange(n_e, dtype=jnp.int32)
    counts = counts_f[0, :n_e].astype(jnp.int32)
    padded = ((counts + tm - 1) // tm) * tm
    pad_ends = jnp.cumsum(padded)
    pad_starts = pad_ends - padded
    n_used = (pad_ends[-1] // tm).astype(jnp.int32)
    eid = info[:, 0:2].astype(jnp.int32)
    rank = info[:, 4:6].astype(jnp.int32)
    slot_of_pick = jnp.sum(jnp.where(eid[..., None] == experts, pad_starts, 0), axis=-1) + rank
    slots = slot_of_pick.T.reshape(-1)
    tile_starts = jnp.arange(n_tiles, dtype=jnp.int32) * tm
    tile_expert = expert_base + jnp.minimum(jnp.sum(tile_starts[:, None] >= pad_ends[None, :], axis=1), n_e - 1)

    xs = _sc_scatter_pairs(xn, slots, n_slots)
    ys = _experts(xs, tile_expert.astype(jnp.int32), n_used.reshape(1), w_gate, w_up, w_down, tm=tm)
    yg = _sc_gather(ys, slots)
    if final is not None:
        return _final_combine(h1, yg, info, g_next, **final)
    return _combine(h1, yg, info, g_next, final_norm=False)


def _router_weights(w_rg, b_rg, w_re, b_re):
    d = w_rg.shape[0]
    pad = LANES - N_GROUPS - N_GROUPS * N_EXP
    w = jnp.concatenate([w_rg, w_re, jnp.zeros((d, pad), F32)], axis=1)
    b = jnp.concatenate([b_rg, b_re, jnp.zeros((pad,), F32)]).reshape(1, LANES)
    hi = w.astype(BF16)
    lo = (w - hi.astype(F32)).astype(BF16)
    return hi, lo, b


def _rows_to_lanes(x, nb, t, nch):
    return x.reshape(nb, t, nch).transpose(0, 2, 1).reshape(nb * nch, t)


def _even_layer(h, dims, g_mix, w_in, w_g2, b_g, b_f, g_a, w_o, state_gla, ck, cv, clf):
    nb, t, db, ds, npr, n = dims
    d = h.shape[1]
    qa, ka, va, ra, ga, qb, kb, vb, fb = jnp.split(
        w_in, [A_QK, 2 * A_QK, 2 * A_QK + A_V, 2 * A_QK + 2 * A_V, 2 * A_QK + 2 * A_V + A_RANK,
               2 * A_QK + 2 * A_V + A_RANK + B_W, 2 * A_QK + 2 * A_V + A_RANK + 2 * B_W,
               2 * A_QK + 2 * A_V + A_RANK + 3 * B_W], axis=1)
    w_packed = jnp.concatenate(
        [qa, ka, va, ra, qb, kb, vb, ga, fb, jnp.zeros((d, LANES - A_RANK - B_HEADS), F32)], axis=1)
    q_off = 2 * A_QK + 2 * A_V
    colscale = jnp.ones((1, MAIN_W), F32).at[:, q_off:q_off + B_W].set(B_DH ** -0.5 * LOG2E)
    p, pb, k_rows, v_rows = _proj(h, g_mix, w_packed, n_rows=npr, colscale=colscale, kv_col=q_off + B_W)
    p = _proj_precise(h, g_mix, w_packed, p, row0=npr)

    nh = B_HEADS
    fcol = MAIN_W + A_RANK
    zeros_col = lambda r: jnp.zeros((r, 1), F32)
    bias_row = lambda lanes: jnp.tile(b_f, lanes // nh).reshape(1, lanes)
    fb_p = p[:npr, fcol:fcol + nh].reshape(nb, t * nh)
    logf_p, f_p = _gate_scan(fb_p, bias_row(t * nh), zeros_col(nb), mode="fox", act_start=0,
                             valid_start=N_PAD * nh, valid_end=t * nh, seg=None, stride=nh)
    past = ck.shape[1]
    n_c = past * nh
    x_s = _pad_lanes(jnp.concatenate([clf.reshape(db, n_c), p[npr:, fcol:fcol + nh].reshape(db, ds * nh)], axis=1))
    logf_s, f_s = _gate_scan(x_s, bias_row(x_s.shape[1]), zeros_col(db), mode="fox", act_start=n_c,
                             valid_start=0, valid_end=n_c + ds * nh, seg=None, stride=nh)

    nc = t // CHUNK
    oa, s_p = _gla(p, jnp.zeros((nb, A_HEADS, A_DK, A_DV), F32), w_g2, b_g,
                   nb=nb, nc=nc, L=CHUNK, row_block0=0, lead_pad=N_PAD, n_total=n)
    oa, s_s = _gla(p, state_gla, w_g2, b_g, nb=db, nc=1, L=ds, row_block0=npr // ds, lead_pad=0,
                   n_total=n, prev_out=oa, precise=True)

    fk = (f_p * LOG2E).reshape(nb, t, nh).transpose(0, 2, 1).reshape(nb, nh, 1, t)
    ob = _flash(pb, fk, nb=nb, t=t, n_total=n)
    f_cache = (f_s[:, :n_c] * LOG2E).reshape(db, 1, n_c)
    f_new = (f_s[:, n_c:n_c + ds * nh] * LOG2E).reshape(db, ds, nh).transpose(0, 2, 1).reshape(db, 1, nh * ds)
    ob = _fox_sample(p, ck.reshape(db, n_c, B_DH), cv.reshape(db, n_c, B_DH), f_cache, f_new, ob,
                     nb=db, ds=ds, row_block0=npr // ds)

    kcol = q_off + B_W
    states = dict(
        s_p=s_p, s_s=s_s,
        k_p=k_rows.reshape(nb, t, B_HEADS, B_DH)[:, N_PAD:],
        v_p=v_rows.reshape(nb, t, B_HEADS, B_DH)[:, N_PAD:],
        f_p=logf_p.reshape(nb, t, nh)[:, N_PAD:],
        k_s=p[npr:, kcol:kcol + B_W].reshape(db, ds, B_HEADS, B_DH),
        v_s=p[npr:, kcol + B_W:kcol + 2 * B_W].reshape(db, ds, B_HEADS, B_DH),
        f_s=logf_s[:, n_c:n_c + ds * nh].reshape(db, ds, nh))
    return (oa, 0, p, (2 * A_QK + A_V) // A_V, ob, g_a, w_o), states


def _chunk_rows(x, nb, nch, nc, L):
    x = x[:, :nc * L].reshape(nb, nch, nc, L)
    return x.transpose(0, 2, 1, 3).reshape(nb, nc, 1, nch * L), x.transpose(0, 2, 3, 1)


def _odd_layer(h, dims, g_mix, w_in, b_gate, g_c, w_o, c0, n0, m0):
    nb, t, db, ds, npr, n = dims
    d = h.shape[1]
    w_packed = jnp.concatenate(
        [w_in, jnp.zeros((d, LANES - 2 * C_HEADS), F32)], axis=1)
    (p,) = _proj(h, g_mix, w_packed, n_rows=npr)
    p = _proj_precise(h, g_mix, w_packed, p, row0=npr)

    ng = 2 * C_HEADS
    isf = (jnp.arange(ng) >= C_HEADS).astype(F32)
    nc = t // CHUNK

    def gates(rows, nbatch, tt, valid_start, seg):
        x = _pad_lanes(_rows_to_lanes(rows, nbatch, tt, ng))
        val, cum = _gate_scan(x, jnp.tile(b_gate, nbatch).reshape(-1, 1), jnp.tile(isf, nbatch).reshape(-1, 1),
                              mode="mlstm", act_start=0, valid_start=valid_start, valid_end=tt, seg=seg)
        return val.reshape(nbatch, ng, -1), cum.reshape(nbatch, ng, -1)

    val_p, cum_p = gates(p[:npr, MAIN_W:MAIN_W + ng], nb, t, N_PAD, CHUNK)
    val_s, cum_s = gates(p[npr:, MAIN_W:MAIN_W + ng], db, ds, 0, ds)

    def chunked(val, cum, nbatch, ncs, L):
        li_row, li_col = _chunk_rows(val[:, :C_HEADS].reshape(nbatch * C_HEADS, -1), nbatch, C_HEADS, ncs, L)
        b_row, b_col = _chunk_rows(cum[:, C_HEADS:].reshape(nbatch * C_HEADS, -1), nbatch, C_HEADS, ncs, L)
        return li_row, b_row, li_col, b_col

    zc = jnp.zeros((nb, C_HEADS, C_DV, C_DQK), F32)
    zn = jnp.zeros((nb, C_HEADS, C_DQK), F32)
    zm = jnp.zeros((nb, C_HEADS, 1), F32)
    hm, c_p, n_p, m_p = _mlstm(p, *chunked(val_p, cum_p, nb, nc, CHUNK), zc, zn, zm,
                               nb=nb, nc=nc, L=CHUNK, row_block0=0, n_total=n)
    hm, c_s, n_s, m_s = _mlstm(p, *chunked(val_s, cum_s, db, 1, ds), c0, n0, m0.reshape(db, C_HEADS, 1),
                               nb=db, nc=1, L=ds, row_block0=npr // ds, n_total=n, prev_out=hm)
    states = dict(c_p=c_p, n_p=n_p, m_p=m_p.reshape(nb, C_HEADS), c_s=c_s, n_s=n_s, m_s=m_s.reshape(db, C_HEADS))
    return (hm, 0, p, (2 * C_QK + C_V) // C_V, None, g_c, w_o), states


def kernel(x_prompt, x_sample, state_gla, cache_fox_k, cache_fox_v, cache_fox_logf, state_mlstm_c, state_mlstm_n, state_mlstm_m, meta_tokens, norm_mix, norm_ffn, norm_final, w_in_even, w_gla_gate2, b_gla_gate, b_fox_f, g_gla_out, w_out_even, w_in_odd, b_mlstm_gate, g_mlstm_out, w_out_odd, w_router_group, b_router_group, w_router_expert, b_router_expert, w_exp_gate, w_exp_up, w_exp_down):
    nb, seq, d = x_prompt.shape
    db, ds, _ = x_sample.shape
    t = LEAD + seq
    npr, nsm = nb * t, db * ds
    n = npr + nsm
    dims = (nb, t, db, ds, npr, n)
    depth = norm_mix.shape[0]
    n_e = N_GROUPS * N_EXP
    f = w_exp_gate.shape[-1]

    pad_rows = jnp.zeros((N_PAD, d), F32)
    h = jnp.concatenate([piece for b in range(nb) for piece in (pad_rows, meta_tokens, x_prompt[b])]
                        + [x_sample.reshape(nsm, d)], axis=0)
    wg_all = w_exp_gate.reshape(depth * n_e, d, f)
    wu_all = w_exp_up.reshape(depth * n_e, d, f)
    wd_all = w_exp_down.reshape(depth * n_e, f, d)

    even, odd = [], []
    for l in range(depth):
        if l % 2 == 0:
            e = l // 2
            mix, st = _even_layer(h, dims, norm_mix[l], w_in_even[e], w_gla_gate2[e], b_gla_gate[e], b_fox_f[e],
                                  g_gla_out[e], w_out_even[e], state_gla[e], cache_fox_k[e], cache_fox_v[e],
                                  cache_fox_logf[e])
            even.append(st)
            hd, act = A_DV, "silu"
        else:
            o = l // 2
            mix, st = _odd_layer(h, dims, norm_mix[l], w_in_odd[o], b_mlstm_gate[o], g_mlstm_out[o], w_out_odd[o],
                                 state_mlstm_c[o], state_mlstm_n[o], state_mlstm_m[o])
            odd.append(st)
            hd, act = C_DV, "sigmoid"
        a, a_col, r, r_col, b, g_head, w_o = mix
        wr_hi, wr_lo, b_r = _router_weights(w_router_group[l], b_router_group[l], w_router_expert[l],
                                            b_router_expert[l])
        mix_args = (h, a, a_col, r, r_col, b, g_head, w_o, norm_ffn[l], wr_hi, wr_lo, b_r)
        h1, xn, info, counts = _mixout(*mix_args, jnp.zeros((1, LANES), F32), hd=hd, act=act, row0=0, n_rows=npr)
        h1, xn, info, counts = _mixout(*mix_args, counts, hd=hd, act=act, row0=npr, n_rows=nsm,
                                       prev=(h1, xn, info))
        last = l == depth - 1
        h = _moe(h1, xn, info, counts, wg_all, wu_all, wd_all, l * n_e, norm_final if last else norm_ffn[l],
                 final=dict(nb=nb, t=t, n_sample=nsm) if last else None)

    y_prompt = h[0].reshape(nb, seq, d)
    y_sample = h[1].reshape(db, ds, d)
    stack = lambda sts, key: jnp.stack([s[key] for s in sts])
    return (y_prompt, y_sample,
            stack(even, "s_p"), stack(even, "k_p"), stack(even, "v_p"), stack(even, "f_p"),
            stack(odd, "c_p"), stack(odd, "n_p"), stack(odd, "m_p"),
            stack(even, "s_s"), stack(even, "k_s"), stack(even, "v_s"), stack(even, "f_s"),
            stack(odd, "c_s"), stack(odd, "n_s"), stack(odd, "m_s"))
```

```python
import functools
import math

import jax
import jax.numpy as jnp
from jax import lax
from jax.experimental import pallas as pl
from jax.experimental.pallas import tpu as pltpu
from jax.experimental.pallas import tpu_sc as plsc

F32 = jnp.float32
BF16 = jnp.bfloat16

CHUNK = 64
N_META = 16
LEAD = 128
N_PAD = LEAD - N_META
A_HEADS, A_DK, A_DV, A_RANK = 4, 64, 128, 16
A_GATE_NORM = 16.0
B_HEADS, B_DH = 4, 128
C_HEADS, C_DQK, C_DV = 4, 128, 256
GATE_CAP = 15.0
N_GROUPS, N_EXP = 4, 8
EPS = 1e-6
NEG = -1e30
LOG2E = 1.4426950408889634
A_QK = A_HEADS * A_DK
A_V = A_HEADS * A_DV
B_W = B_HEADS * B_DH
C_QK = C_HEADS * C_DQK
C_V = C_HEADS * C_DV

LANES = 128
SUBLANES = 8
VMEM_LIMIT_BYTES = 56 * 1024 * 1024
GLA_SUB = 16
SC_CORES, SC_SUBCORES = 2, 16
SC_GATHER_ROWS = 16
SC_GATHER_BUFS = 4
MOE_TILE = 1024
SC_SCATTER_ROWS = 16
SC_SCATTER_BUFS = 3
FLASH_HEADS = 4
MAIN_W = 3072
PROJ_W = MAIN_W + LANES

_NT = (((1,), (1,)), ((), ()))
_TN = (((0,), (0,)), ((), ()))
_NN = (((1,), (0,)), ((), ()))


def _params(sem):
    return pltpu.CompilerParams(dimension_semantics=sem, vmem_limit_bytes=VMEM_LIMIT_BYTES)


def _tile(n, pref, mult):
    t = (min(pref, n) // mult) * mult
    while t > mult and n % t:
        t -= mult
    assert t >= mult and n % t == 0, (n, pref, mult)
    return t


def _dot(a, b, dims=_NN):
    return lax.dot_general(a, b, dims, preferred_element_type=F32)


def _split(x):
    hi = x.astype(BF16)
    lo = (x - hi.astype(F32)).astype(BF16)
    return hi, lo


def _dot3(a, b, dims=_NN):
    ah, al = _split(a)
    bh, bl = _split(b)
    return _dot(ah, bh, dims) + _dot(ah, bl, dims) + _dot(al, bh, dims)


def _log_sigmoid(x):
    return jnp.minimum(x, 0.0) - jnp.log1p(jnp.exp(-jnp.abs(x)))


def _sigmoid(x):
    return 1.0 / (1.0 + jnp.exp(-x))


def _rms(x, g):
    return x * lax.rsqrt(jnp.mean(x * x, axis=-1, keepdims=True) + EPS) * g


def _pack_bf16_pairs(x):
    w = x.shape[1] // 2
    hi = lax.bitcast_convert_type(x[:, :w].astype(BF16).astype(F32), jnp.int32)
    lo = lax.bitcast_convert_type(x[:, w:].astype(BF16).astype(F32), jnp.int32)
    return hi | lax.shift_right_logical(lo, 16)


def _unpack_bf16_pairs(p):
    hi = lax.bitcast_convert_type(p & jnp.int32(-65536), F32)
    lo = lax.bitcast_convert_type(lax.shift_left(p, 16), F32)
    return jnp.concatenate([hi, lo], axis=1)


def _cumsum_rows(x):
    n = x.shape[0]
    row = lax.broadcasted_iota(jnp.int32, x.shape, 0)
    s = 1
    while s < n:
        x = x + jnp.where(row >= s, pltpu.roll(x, s, axis=0), 0.0)
        s *= 2
    return x


def _proj_kernel(x_ref, g_ref, w_ref, *rest, col_chunk, kv_col):
    if kv_col is None:
        (o_ref,) = rest
    else:
        cs_ref, o_ref, ob_ref, *kv_refs = rest
    tm = x_ref.shape[0]
    xn = _rms(x_ref[...], g_ref[...]).astype(BF16)
    for c0 in range(0, PROJ_W, col_chunk):
        c1 = min(c0 + col_chunk, PROJ_W)
        y = _dot(xn, w_ref[:, c0:c1])
        o_ref[:, c0:c1] = y
        if kv_col is not None:
            if c0 < MAIN_W:
                m1 = min(c1, MAIN_W)
                ob_ref[:, c0:m1] = (y[:, :m1 - c0] * cs_ref[:, c0:m1]).astype(BF16)
            for g0 in range(c0, c1, LANES):
                rel = g0 - kv_col
                if 0 <= rel < 2 * B_W:
                    head = (rel % B_W) // B_DH
                    kv_refs[rel // B_W][pl.ds(head, tm, stride=B_HEADS), :] = y[:, g0 - c0:g0 - c0 + LANES]


def _proj_precise_kernel(x_ref, g_ref, w_ref, prev_ref, o_ref, *, col_chunk):
    del prev_ref
    xn = _rms(x_ref[...], g_ref[...])
    for c0 in range(0, PROJ_W, col_chunk):
        c1 = min(c0 + col_chunk, PROJ_W)
        o_ref[:, c0:c1] = _dot3(xn, w_ref[:, c0:c1])


def _proj(h, g, w_packed, *, n_rows, colscale=None, kv_col=None):
    n, d = h.shape
    tm = _tile(n_rows, 512, 16)
    in_specs = [pl.BlockSpec((tm, d), lambda i: (i, 0)),
                pl.BlockSpec((1, d), lambda i: (0, 0)),
                pl.BlockSpec((d, PROJ_W), lambda i: (0, 0))]
    args = [h, g.reshape(1, d), w_packed.astype(BF16)]
    out_specs = [pl.BlockSpec((tm, PROJ_W), lambda i: (i, 0))]
    out_shape = [jax.ShapeDtypeStruct((n, PROJ_W), F32)]
    if kv_col is not None:
        in_specs.append(pl.BlockSpec((1, MAIN_W), lambda i: (0, 0)))
        args.append(colscale)
        out_specs += [pl.BlockSpec((tm, MAIN_W), lambda i: (i, 0))] + [pl.BlockSpec((tm * B_HEADS, B_DH), lambda i: (i, 0))] * 2
        out_shape += ([jax.ShapeDtypeStruct((n_rows, MAIN_W), BF16)]
                      + [jax.ShapeDtypeStruct((n_rows * B_HEADS, B_DH), F32)] * 2)
    return pl.pallas_call(
        functools.partial(_proj_kernel, col_chunk=640, kv_col=kv_col),
        grid=(n_rows // tm,),
        in_specs=in_specs,
        out_specs=out_specs,
        out_shape=out_shape,
        compiler_params=_params(("parallel",)),
        name="proj",
    )(*args)


def _proj_precise(h, g, w_packed, prev, *, row0):
    n, d = h.shape
    tm = _tile(n - row0, 512, SUBLANES)
    assert row0 % tm == 0
    return pl.pallas_call(
        functools.partial(_proj_precise_kernel, col_chunk=640),
        grid=((n - row0) // tm,),
        in_specs=[pl.BlockSpec((tm, d), lambda i: (row0 // tm + i, 0)),
                  pl.BlockSpec((1, d), lambda i: (0, 0)),
                  pl.BlockSpec((d, PROJ_W), lambda i: (0, 0)),
                  pl.BlockSpec(memory_space=pl.ANY)],
        out_specs=pl.BlockSpec((tm, PROJ_W), lambda i: (row0 // tm + i, 0)),
        out_shape=jax.ShapeDtypeStruct((n, PROJ_W), F32),
        input_output_aliases={3: 0},
        compiler_params=_params(("parallel",)),
        name="proj_precise",
    )(h, g.reshape(1, d), w_packed, prev)


def _gate_scan_kernel(x_ref, bias_ref, isf_ref, val_ref, cum_ref, *, mode, act_start, valid_start, valid_end, seg, stride):
    x = x_ref[...]
    lane = lax.broadcasted_iota(jnp.int32, x.shape, 1)
    valid = (lane >= valid_start) & (lane < valid_end)
    if mode == "fox":
        val = jnp.where(lane >= act_start, _log_sigmoid(x + bias_ref[...]), x)
        val = jnp.where(valid, val, 0.0)
        add = val
    else:
        gate = GATE_CAP * jnp.tanh((x + bias_ref[...]) / GATE_CAP)
        isf = isf_ref[...] > 0.5
        val = jnp.where(isf, jnp.where(valid, _log_sigmoid(gate), 0.0),
                        jnp.where(valid, gate, -jnp.inf))
        add = jnp.where(isf, val, 0.0)
    val_ref[...] = val
    n = x.shape[1]
    pos = lane if seg is None else lane % seg
    limit = n if seg is None else seg
    s = stride
    while s < limit:
        add = add + jnp.where(pos >= s, pltpu.roll(add, s, axis=1), 0.0)
        s *= 2
    cum_ref[...] = add


def _gate_scan(x, bias, isf, *, mode, act_start, valid_start, valid_end, seg, stride=1):
    r, n = x.shape
    full = lambda shape: pl.BlockSpec(shape, lambda i: (0,) * len(shape))
    return pl.pallas_call(
        functools.partial(_gate_scan_kernel, mode=mode, act_start=act_start,
                          valid_start=valid_start, valid_end=valid_end, seg=seg, stride=stride),
        grid=(1,),
        in_specs=[full((r, n)), full(bias.shape), full((r, 1))],
        out_specs=[full((r, n)), full((r, n))],
        out_shape=[jax.ShapeDtypeStruct((r, n), F32)] * 2,
        compiler_params=_params(("arbitrary",)),
        name="gate_scan_" + mode,
    )(x, bias, isf)


def _pad_lanes(x):
    n = x.shape[-1]
    m = -(-n // LANES) * LANES
    return x if m == n else jnp.pad(x, ((0, 0), (0, m - n)))


def _gla_kernel(qk_ref, v_ref, sm_ref, s0_ref, wg2_ref, wg2t_ref, bgr_ref, bgc_ref, *rest,
                L, sub, lead_pad, aliased, precise):
    if aliased:
        rest = rest[1:]
    o_ref, sout_ref, s_scr = rest
    c = pl.program_id(1)
    nh, dk, dv = A_HEADS, A_DK, A_DV
    cast = (lambda x: x) if precise else (lambda x: x.astype(BF16))
    mm = _dot3 if precise else _dot

    @pl.when(c == 0)
    def _():
        s_scr[...] = jnp.zeros_like(s_scr)
        for h in range(nh):
            s_scr[h * dk:(h + 1) * dk, h * dv:(h + 1) * dv] = s0_ref[0, h]

    qk = qk_ref[...]
    q = qk[:, :A_QK] * (A_DK ** -0.5)
    k = qk[:, A_QK:]
    v = v_ref[...]
    ga = sm_ref[:, :A_RANK]
    row = lax.broadcasted_iota(jnp.int32, (L, 1), 0)
    valid = (c * L + row) >= lead_pad
    z = _dot3(ga, wg2_ref[...]) + bgr_ref[...]
    loga = jnp.where(valid, _log_sigmoid(z) / A_GATE_NORM, 0.0)
    k = jnp.where(valid, k, 0.0)
    b = _cumsum_rows(loga)
    b_last = b[L - 1:L, :]
    lane_t = lax.broadcasted_iota(jnp.int32, (1, L), 1)
    zt = _dot3(wg2t_ref[...], ga, _NT) + bgc_ref[...]
    logat = jnp.where((c * L + lane_t) >= lead_pad, _log_sigmoid(zt) / A_GATE_NORM, 0.0)
    b_last_col = jnp.sum(logat, axis=1, keepdims=True)

    qhead = lax.broadcasted_iota(jnp.int32, (1, A_QK), 1) // dk
    vhead = lax.broadcasted_iota(jnp.int32, (1, A_V), 1) // dv
    vb = cast(v)
    zero_b = jnp.zeros((), vb.dtype)
    v_bd = jnp.concatenate([jnp.where(vhead == h, vb, zero_b) for h in range(nh)], axis=0)

    rows_all = lax.broadcasted_iota(jnp.int32, (L, 1), 0)
    a_rows = []
    for i in range(L // sub):
        r0 = i * sub
        ci = jnp.zeros((1, A_QK), F32) if i == 0 else b[r0 - 1:r0, :]
        qt = cast(q[r0:r0 + sub] * jnp.exp(b[r0:r0 + sub] - ci))
        kt = cast(jnp.where(rows_all < r0 + sub, k * jnp.exp(ci - b), 0.0))
        k_stack = jnp.concatenate([jnp.where(qhead == h, kt, zero_b) for h in range(nh)], axis=0)
        a_rows.append(mm(qt, k_stack, _NT))
    a = a_rows[0] if len(a_rows) == 1 else jnp.concatenate(a_rows, axis=0)
    t_idx = lax.broadcasted_iota(jnp.int32, (L, nh * L), 0)
    s_idx = lax.broadcasted_iota(jnp.int32, (L, nh * L), 1) % L
    a = jnp.where(s_idx <= t_idx, a, 0.0)
    o_intra = mm(cast(a), v_bd)

    s_full = s_scr[...]
    o_inter = mm(cast(q * jnp.exp(b)), cast(s_full))
    o_ref[...] = o_inter + o_intra

    k_hat = cast(k * jnp.exp(b_last - b))
    upd = mm(k_hat, vb, _TN)
    khead_col = lax.broadcasted_iota(jnp.int32, (A_QK, 1), 0) // dk
    s_new = jnp.exp(b_last_col) * s_full + jnp.where(khead_col == vhead, upd, 0.0)
    s_scr[...] = s_new

    @pl.when(c == pl.num_programs(1) - 1)
    def _():
        for h in range(nh):
            sout_ref[0, h] = s_new[h * dk:(h + 1) * dk, h * dv:(h + 1) * dv]


def _gla(p, s0, wg2, bg, *, nb, nc, L, row_block0, lead_pad, n_total, prev_out=None, precise=False):
    aliased = prev_out is not None
    rb = lambda b, c: row_block0 + b * nc + c
    in_specs = [pl.BlockSpec((L, 2 * A_QK), lambda b, c: (rb(b, c), 0)),
                pl.BlockSpec((L, A_V), lambda b, c: (rb(b, c), 2 * A_QK // A_V)),
                pl.BlockSpec((L, LANES), lambda b, c: (rb(b, c), MAIN_W // LANES)),
                pl.BlockSpec((1, A_HEADS, A_DK, A_DV), lambda b, c: (b, 0, 0, 0)),
                pl.BlockSpec((A_RANK, A_QK), lambda b, c: (0, 0)),
                pl.BlockSpec((A_QK, A_RANK), lambda b, c: (0, 0)),
                pl.BlockSpec((1, A_QK), lambda b, c: (0, 0)),
                pl.BlockSpec((A_QK, 1), lambda b, c: (0, 0))]
    args = [p, p, p, s0, wg2, wg2.T, bg.reshape(1, A_QK), bg.reshape(A_QK, 1)]
    io_alias = {}
    if aliased:
        in_specs.append(pl.BlockSpec(memory_space=pl.ANY))
        args.append(prev_out)
        io_alias = {len(args) - 1: 0}
    return pl.pallas_call(
        functools.partial(_gla_kernel, L=L, sub=min(GLA_SUB, L), lead_pad=lead_pad, aliased=aliased,
                          precise=precise),
        grid=(nb, nc),
        in_specs=in_specs,
        out_specs=[pl.BlockSpec((L, A_V), lambda b, c: (rb(b, c), 0)),
                   pl.BlockSpec((1, A_HEADS, A_DK, A_DV), lambda b, c: (b, 0, 0, 0))],
        out_shape=[jax.ShapeDtypeStruct((n_total, A_V), F32),
                   jax.ShapeDtypeStruct((nb, A_HEADS, A_DK, A_DV), F32)],
        scratch_shapes=[pltpu.VMEM((A_QK, A_V), F32)],
        input_output_aliases=io_alias,
        compiler_params=_params(("parallel", "arbitrary")),
        name="gla_L%d" % L,
    )(*args)


def _flash_kernel(qi_ref, kj_ref, q_ref, k_ref, v_ref, fk_ref, o_ref, m_scr, l_scr, acc_scr, *, blk, lead_pad):
    step = pl.program_id(2)
    i = qi_ref[step]
    j = kj_ref[step]

    @pl.when(j == 0)
    def _():
        m_scr[...] = jnp.full_like(m_scr, -jnp.inf)
        l_scr[...] = jnp.zeros_like(l_scr)
        acc_scr[...] = jnp.zeros_like(acc_scr)

    def update(masked):
        for g in range(FLASH_HEADS):
            sl = slice(g * B_DH, (g + 1) * B_DH)
            s = _dot(q_ref[:, sl], k_ref[:, sl], _NT) - fk_ref[0, g]
            if masked:
                qpos = i * blk + lax.broadcasted_iota(jnp.int32, (blk, blk), 0)
                kpos = j * blk + lax.broadcasted_iota(jnp.int32, (blk, blk), 1)
                s = jnp.where((kpos <= qpos) & (kpos >= lead_pad), s, NEG)
            m_prev = m_scr[g]
            m_new = jnp.maximum(m_prev, jnp.max(s, axis=1, keepdims=True))
            alpha = jnp.exp2(m_prev - m_new)
            p = jnp.exp2(s - m_new)
            l_scr[g] = alpha * l_scr[g] + jnp.sum(p, axis=1, keepdims=True)
            acc_scr[g] = alpha * acc_scr[g] + _dot(p.astype(BF16), v_ref[:, sl])
            m_scr[g] = m_new

    edge = (j == i) | (j == 0)
    pl.when(edge)(functools.partial(update, True))
    pl.when(jnp.logical_not(edge))(functools.partial(update, False))

    @pl.when(j == i)
    def _():
        for g in range(FLASH_HEADS):
            o_ref[:, g * B_DH:(g + 1) * B_DH] = acc_scr[g] / l_scr[g]


def _flash(pb, fk, *, nb, t, n_total):
    blk = _tile(t, 640, LANES)
    nq = t // blk
    hg = FLASH_HEADS
    w = hg * B_DH
    pairs = [(i, j) for i in range(nq) for j in range(i + 1)]
    qi = jnp.asarray([p[0] for p in pairs], jnp.int32)
    kj = jnp.asarray([p[1] for p in pairs], jnp.int32)
    qc, kc, vc = (A_QK * 2 + A_V * 2) // w, (A_QK * 2 + A_V * 2 + B_W) // w, (A_QK * 2 + A_V * 2 + 2 * B_W) // w
    grid_spec = pltpu.PrefetchScalarGridSpec(
        num_scalar_prefetch=2,
        grid=(nb, B_HEADS // hg, len(pairs)),
        in_specs=[pl.BlockSpec((blk, w), lambda b, h, s, qi, kj: (b * nq + qi[s], qc + h)),
                  pl.BlockSpec((blk, w), lambda b, h, s, qi, kj: (b * nq + kj[s], kc + h)),
                  pl.BlockSpec((blk, w), lambda b, h, s, qi, kj: (b * nq + kj[s], vc + h)),
                  pl.BlockSpec((1, hg, 1, blk), lambda b, h, s, qi, kj: (b, h, 0, kj[s]))],
        out_specs=pl.BlockSpec((blk, w), lambda b, h, s, qi, kj: (b * nq + qi[s], h)),
        scratch_shapes=[pltpu.VMEM((hg, blk, 1), F32), pltpu.VMEM((hg, blk, 1), F32),
                        pltpu.VMEM((hg, blk, B_DH), F32)],
    )
    return pl.pallas_call(
        functools.partial(_flash_kernel, blk=blk, lead_pad=N_PAD),
        grid_spec=grid_spec,
        out_shape=jax.ShapeDtypeStruct((n_total, B_W), F32),
        compiler_params=_params(("parallel", "parallel", "arbitrary")),
        name="fox_flash",
    )(qi, kj, pb, pb, pb, fk)


def _fox_sample_kernel(q_ref, kn_ref, vn_ref, kc_ref, vc_ref, fc_ref, fn_ref, prev_ref, o_ref, *, ds):
    del prev_ref
    nh = B_HEADS
    stack = lambda ref: jnp.concatenate([ref[:, h * B_DH:(h + 1) * B_DH] for h in range(nh)], axis=0)
    q = stack(q_ref) * (B_DH ** -0.5 * LOG2E)
    rows = nh * ds
    n_c = kc_ref.shape[1]
    qh_c = lax.broadcasted_iota(jnp.int32, (rows, n_c), 0) // ds
    kh_c = lax.broadcasted_iota(jnp.int32, (rows, n_c), 1) % nh
    s_c = _dot3(q, kc_ref[0], _NT) - fc_ref[0]
    s_c = jnp.where(qh_c == kh_c, s_c, NEG)
    r_i = lax.broadcasted_iota(jnp.int32, (rows, rows), 0)
    c_i = lax.broadcasted_iota(jnp.int32, (rows, rows), 1)
    s_n = _dot3(q, stack(kn_ref), _NT) - fn_ref[0]
    s_n = jnp.where((r_i // ds == c_i // ds) & (c_i % ds <= r_i % ds), s_n, NEG)
    m = jnp.maximum(jnp.max(s_c, axis=1, keepdims=True), jnp.max(s_n, axis=1, keepdims=True))
    p_c = jnp.exp2(s_c - m)
    p_n = jnp.exp2(s_n - m)
    l = jnp.sum(p_c, axis=1, keepdims=True) + jnp.sum(p_n, axis=1, keepdims=True)
    o = (_dot3(p_c, vc_ref[0]) + _dot3(p_n, stack(vn_ref))) / l
    for h in range(nh):
        o_ref[:, h * B_DH:(h + 1) * B_DH] = o[h * ds:(h + 1) * ds]


def _fox_sample(pb, kc, vc, f_cache, f_new, prev_out, *, nb, ds, row_block0):
    n_c = kc.shape[1]
    base = (A_QK * 2 + A_V * 2) // B_W
    rb = lambda b: row_block0 + b
    return pl.pallas_call(
        functools.partial(_fox_sample_kernel, ds=ds),
        grid=(nb,),
        in_specs=[pl.BlockSpec((ds, B_W), lambda b: (rb(b), base)),
                  pl.BlockSpec((ds, B_W), lambda b: (rb(b), base + 1)),
                  pl.BlockSpec((ds, B_W), lambda b: (rb(b), base + 2)),
                  pl.BlockSpec((1, n_c, B_DH), lambda b: (b, 0, 0)),
                  pl.BlockSpec((1, n_c, B_DH), lambda b: (b, 0, 0)),
                  pl.BlockSpec((1, 1, n_c), lambda b: (b, 0, 0)),
                  pl.BlockSpec((1, 1, B_HEADS * ds), lambda b: (b, 0, 0)),
                  pl.BlockSpec(memory_space=pl.ANY)],
        out_specs=pl.BlockSpec((ds, B_W), lambda b: (rb(b), 0)),
        out_shape=jax.ShapeDtypeStruct(prev_out.shape, F32),
        input_output_aliases={7: 0},
        compiler_params=_params(("parallel",)),
        name="fox_sample",
    )(pb, pb, pb, kc, vc, f_cache, f_new, prev_out)


def _mlstm_kernel(q_ref, k_ref, v_ref, lir_ref, br_ref, lic_ref, bc_ref, c0_ref, n0_ref, m0_ref, *rest,
                  L, aliased):
    if aliased:
        rest = rest[1:]
    h_ref, cout_ref, nout_ref, mout_ref, c_scr, n_scr, m_scr = rest
    c = pl.program_id(1)

    @pl.when(c == 0)
    def _():
        c_scr[...] = c0_ref[0]
        n_scr[...] = n0_ref[0]
        m_scr[...] = m0_ref[0]

    nh = C_HEADS
    hl = nh * L
    seg = lax.broadcasted_iota(jnp.int32, (1, hl), 1) // L
    causal = (lax.broadcasted_iota(jnp.int32, (L, hl), 1) % L) <= lax.broadcasted_iota(jnp.int32, (L, hl), 0)

    def per_head(vals):
        out = vals[0]
        for h in range(1, nh):
            out = jnp.where(seg == h, vals[h], out)
        return out

    seg_max = lambda x, h: jnp.max(jnp.where(seg == h, x, -jnp.inf), axis=1, keepdims=True)

    qf = q_ref[...]
    kf = k_ref[...] * (C_DQK ** -0.5)
    qb = qf.astype(BF16)
    kb = kf.astype(BF16)
    vb = v_ref[...].astype(BF16)
    b_row = br_ref[0, 0]
    li_row = lir_ref[0, 0]
    b_col = [bc_ref[0, 0, :, h:h + 1] for h in range(nh)]
    li_col = [lic_ref[0, 0, :, h:h + 1] for h in range(nh)]
    m_prev = [m_scr[h:h + 1, :] for h in range(nh)]

    d = jnp.where(causal, per_head(b_col) - b_row + li_row, -jnp.inf)
    inter = [b_col[h] + m_prev[h] for h in range(nh)]
    m_t = [jnp.maximum(inter[h], seg_max(d, h)) for h in range(nh)]
    pm = jnp.exp(d - per_head(m_t))
    w_inter = [jnp.exp(inter[h] - m_t[h]) for h in range(nh)]

    khead = lax.broadcasted_iota(jnp.int32, (1, C_QK), 1) // C_DQK
    vhead = lax.broadcasted_iota(jnp.int32, (1, C_V), 1) // C_DV
    zero_b = jnp.zeros((), BF16)
    k_stack = jnp.concatenate([jnp.where(khead == h, kb, zero_b) for h in range(nh)], axis=0)
    v_bd = jnp.concatenate([jnp.where(vhead == h, vb, zero_b) for h in range(nh)], axis=0)
    sqk = _dot(qb, k_stack, _NT) * pm
    sv = _dot(sqk.astype(BF16), v_bd)

    b_last = [b_row[:, h * L + L - 1:h * L + L] for h in range(nh)]
    g_row = per_head(b_last) - b_row + li_row
    for h in range(nh):
        qk_sl = slice(h * C_DQK, (h + 1) * C_DQK)
        v_sl = slice(h * C_DV, (h + 1) * C_DV)
        c_prev = c_scr[h]
        n_prev = n_scr[h:h + 1, :]
        num = w_inter[h] * _dot(qb[:, qk_sl], c_prev.astype(BF16), _NT) + sv[:, v_sl]
        den = (w_inter[h] * jnp.sum(qf[:, qk_sl] * n_prev, axis=1, keepdims=True)
               + jnp.sum(jnp.where(seg == h, sqk, 0.0), axis=1, keepdims=True))
        h_ref[:, v_sl] = num / jnp.maximum(jnp.abs(den), jnp.exp(-m_t[h]))

        m_new = jnp.maximum(b_last[h] + m_prev[h], seg_max(g_row, h))
        w_c = jnp.exp(b_last[h] + m_prev[h] - m_new)
        kw = kf[:, qk_sl] * jnp.exp(b_last[h] - b_col[h] + li_col[h] - m_new)
        c_scr[h] = w_c * c_prev + _dot(vb[:, v_sl], kw.astype(BF16), _TN)
        n_scr[h:h + 1, :] = w_c * n_prev + jnp.sum(kw, axis=0, keepdims=True)
        m_scr[h:h + 1, :] = m_new

    @pl.when(c == pl.num_programs(1) - 1)
    def _():
        cout_ref[0] = c_scr[...]
        nout_ref[0] = n_scr[...]
        mout_ref[0] = m_scr[...]


def _mlstm(p, li_row, b_row, li_col, b_col, c0, n0, m0, *, nb, nc, L, row_block0, n_total, prev_out=None):
    aliased = prev_out is not None
    rb = lambda b, c: row_block0 + b * nc + c
    in_specs = [pl.BlockSpec((L, C_QK), lambda b, c: (rb(b, c), 0)),
                pl.BlockSpec((L, C_QK), lambda b, c: (rb(b, c), 1)),
                pl.BlockSpec((L, C_V), lambda b, c: (rb(b, c), 2 * C_QK // C_V)),
                pl.BlockSpec((1, 1, 1, C_HEADS * L), lambda b, c: (b, c, 0, 0)),
                pl.BlockSpec((1, 1, 1, C_HEADS * L), lambda b, c: (b, c, 0, 0)),
                pl.BlockSpec((1, 1, L, C_HEADS), lambda b, c: (b, c, 0, 0)),
                pl.BlockSpec((1, 1, L, C_HEADS), lambda b, c: (b, c, 0, 0)),
                pl.BlockSpec((1, C_HEADS, C_DV, C_DQK), lambda b, c: (b, 0, 0, 0)),
                pl.BlockSpec((1, C_HEADS, C_DQK), lambda b, c: (b, 0, 0)),
                pl.BlockSpec((1, C_HEADS, 1), lambda b, c: (b, 0, 0))]
    args = [p, p, p, li_row, b_row, li_col, b_col, c0, n0, m0]
    io_alias = {}
    if aliased:
        in_specs.append(pl.BlockSpec(memory_space=pl.ANY))
        args.append(prev_out)
        io_alias = {len(args) - 1: 0}
    return pl.pallas_call(
        functools.partial(_mlstm_kernel, L=L, aliased=aliased),
        grid=(nb, nc),
        in_specs=in_specs,
        out_specs=[pl.BlockSpec((L, C_V), lambda b, c: (rb(b, c), 0)),
                   pl.BlockSpec((1, C_HEADS, C_DV, C_DQK), lambda b, c: (b, 0, 0, 0)),
                   pl.BlockSpec((1, C_HEADS, C_DQK), lambda b, c: (b, 0, 0)),
                   pl.BlockSpec((1, C_HEADS, 1), lambda b, c: (b, 0, 0))],
        out_shape=[jax.ShapeDtypeStruct((n_total, C_V), F32),
                   jax.ShapeDtypeStruct((nb, C_HEADS, C_DV, C_DQK), F32),
                   jax.ShapeDtypeStruct((nb, C_HEADS, C_DQK), F32),
                   jax.ShapeDtypeStruct((nb, C_HEADS, 1), F32)],
        scratch_shapes=[pltpu.VMEM((C_HEADS, C_DV, C_DQK), F32),
                        pltpu.VMEM((C_HEADS, C_DQK), F32),
                        pltpu.VMEM((C_HEADS, 1), F32)],
        input_output_aliases=io_alias,
        compiler_params=_params(("parallel", "arbitrary")),
        name="mlstm_L%d" % L,
    )(*args)


def _mixout_kernel(*refs, hd, act, has_b, precise, n_prev):
    refs = list(refs)
    h_ref, a_ref, r_ref = refs[:3]
    b_ref = refs[3] if has_b else None
    k = 4 if has_b else 3
    ga_ref, wo_ref, gf_ref, wrh_ref, wrl_ref, br_ref, cnt0_ref = refs[k:k + 7]
    h1_ref, xn_ref, info_ref, cnt_ref, cnt_scr = refs[k + 7 + n_prev:]
    cast = (lambda x: x) if precise else (lambda x: x.astype(BF16))
    a = a_ref[...]
    r = r_ref[...]
    gate = r * _sigmoid(r) if act == "silu" else _sigmoid(r)
    parts = []
    for hh in range(a.shape[1] // hd):
        sl = slice(hh * hd, (hh + 1) * hd)
        parts.append(cast(_rms(a[:, sl], ga_ref[...]) * gate[:, sl]))
    if has_b:
        parts.append(cast(b_ref[...]))
    cat = jnp.concatenate(parts, axis=1)
    h1 = h_ref[...] + (_dot3(cat, wo_ref[...]) if precise else _dot(cat, wo_ref[...]))
    h1_ref[...] = h1
    xn = _rms(h1, gf_ref[...])
    xn_ref[...] = _pack_bf16_pairs(xn)
    xh, xl = _split(xn)
    logits = _dot(xh, wrh_ref[...]) + _dot(xh, wrl_ref[...]) + _dot(xl, wrh_ref[...]) + br_ref[...]

    lane = lax.broadcasted_iota(jnp.int32, logits.shape, 1)
    lanef = lane.astype(F32)
    is_g = lane < N_GROUPS
    gl = jnp.where(is_g, logits, -jnp.inf)
    gmax = jnp.max(gl, axis=1, keepdims=True)
    gidx = jnp.min(jnp.where(gl == gmax, lanef, float(LANES)), axis=1, keepdims=True)
    wg = 1.0 / jnp.sum(jnp.where(is_g, jnp.exp(gl - gmax), 0.0), axis=1, keepdims=True)
    lo = N_GROUPS + N_EXP * gidx
    el = jnp.where((lanef >= lo) & (lanef < lo + N_EXP), logits, -jnp.inf)
    m1 = jnp.max(el, axis=1, keepdims=True)
    i1 = jnp.min(jnp.where(el == m1, lanef, float(LANES)), axis=1, keepdims=True)
    el2 = jnp.where(lanef == i1, -jnp.inf, el)
    m2 = jnp.max(el2, axis=1, keepdims=True)
    i2 = jnp.min(jnp.where(el2 == m2, lanef, float(LANES)), axis=1, keepdims=True)
    t = jnp.exp(m2 - m1)
    w1 = wg / (1.0 + t)
    w2 = wg * t / (1.0 + t)
    e1 = i1 - N_GROUPS
    e2 = i2 - N_GROUPS

    @pl.when(pl.program_id(0) == 0)
    def _():
        cnt_scr[...] = cnt0_ref[...]

    tm = logits.shape[0]
    pick = jnp.where((lanef == e1) | (lanef == e2), 1.0, 0.0)
    earlier = (lax.broadcasted_iota(jnp.int32, (tm, tm), 1) < lax.broadcasted_iota(jnp.int32, (tm, tm), 0))
    before = _dot(jnp.where(earlier, 1.0, 0.0).astype(BF16), pick.astype(BF16)) + cnt_scr[...]
    r1 = jnp.sum(jnp.where(lanef == e1, before, 0.0), axis=1, keepdims=True)
    r2 = jnp.sum(jnp.where(lanef == e2, before, 0.0), axis=1, keepdims=True)
    cnt_new = cnt_scr[...] + jnp.sum(pick, axis=0, keepdims=True)
    cnt_scr[...] = cnt_new
    cnt_ref[...] = cnt_new
    info_ref[...] = jnp.where(lane == 0, e1, jnp.where(lane == 1, e2, jnp.where(lane == 2, w1, jnp.where(
        lane == 3, w2, jnp.where(lane == 4, r1, jnp.where(lane == 5, r2, 0.0))))))


def _mixout(h, a, a_col, r, r_col, b, g_head, w_o, g_ffn, wr_hi, wr_lo, b_r, counts0, *, hd, act,
            row0, n_rows, prev=None):
    n, d = h.shape
    precise = prev is not None
    tm = _tile(n_rows, 512, SUBLANES)
    assert row0 % tm == 0
    blk0 = row0 // tm
    wa = w_o.shape[0] if b is None else w_o.shape[0] - B_W
    has_b = b is not None
    row = lambda i: (blk0 + i, 0)
    const = lambda i: (0, 0)
    in_specs = [pl.BlockSpec((tm, d), row),
                pl.BlockSpec((tm, wa), lambda i: (blk0 + i, a_col)),
                pl.BlockSpec((tm, wa), lambda i: (blk0 + i, r_col))]
    args = [h, a, r]
    if has_b:
        in_specs.append(pl.BlockSpec((tm, B_W), row))
        args.append(b)
    in_specs += [pl.BlockSpec((1, hd), const), pl.BlockSpec(w_o.shape, const), pl.BlockSpec((1, d), const),
                 pl.BlockSpec((d, LANES), const), pl.BlockSpec((d, LANES), const), pl.BlockSpec((1, LANES), const),
                 pl.BlockSpec((1, LANES), const)]
    args += [g_head.reshape(1, hd), w_o if precise else w_o.astype(BF16), g_ffn.reshape(1, d), wr_hi, wr_lo, b_r,
             counts0]
    io_alias = {}
    if precise:
        for k, arr in enumerate(prev):
            in_specs.append(pl.BlockSpec(memory_space=pl.ANY))
            args.append(arr)
            io_alias[len(args) - 1] = k
    return pl.pallas_call(
        functools.partial(_mixout_kernel, hd=hd, act=act, has_b=has_b, precise=precise, n_prev=len(io_alias)),
        grid=(n_rows // tm,),
        in_specs=in_specs,
        out_specs=[pl.BlockSpec((tm, d), row), pl.BlockSpec((tm, d // 2), row), pl.BlockSpec((tm, LANES), row),
                   pl.BlockSpec((1, LANES), const)],
        out_shape=[jax.ShapeDtypeStruct((n, d), F32), jax.ShapeDtypeStruct((n, d // 2), jnp.int32),
                   jax.ShapeDtypeStruct((n, LANES), F32), jax.ShapeDtypeStruct((1, LANES), F32)],
        scratch_shapes=[pltpu.VMEM((1, LANES), F32)],
        input_output_aliases=io_alias,
        compiler_params=_params(("arbitrary",)),
        name="mixout_" + act + ("_precise" if precise else ""),
    )(*args)


def _sc_gather(table, idx):
    r = idx.shape[0]
    w = table.shape[1]
    n_workers = SC_CORES * SC_SUBCORES
    per_worker = r // n_workers
    step = SC_GATHER_ROWS * SC_GATHER_BUFS
    assert r % n_workers == 0 and per_worker % step == 0, (r, n_workers, step)
    mesh = plsc.VectorSubcoreMesh(core_axis_name="c", subcore_axis_name="s")

    @functools.partial(
        pl.kernel, mesh=mesh,
        out_type=jax.ShapeDtypeStruct((r, w), table.dtype),
        scratch_types=[pltpu.VMEM((SC_GATHER_BUFS, SC_GATHER_ROWS), jnp.int32),
                       pltpu.VMEM((SC_GATHER_BUFS, SC_GATHER_ROWS, w), table.dtype),
                       pltpu.SemaphoreType.DMA((SC_GATHER_BUFS,)),
                       pltpu.SemaphoreType.DMA((SC_GATHER_BUFS,))],
    )
    def gather(table_hbm, idx_hbm, out_hbm, idx_v, rows_v, gather_sem, store_sem):
        worker = lax.axis_index("s") * SC_CORES + lax.axis_index("c")
        base = worker * per_worker

        @pl.loop(0, per_worker // step)
        def _(j):
            off = pl.multiple_of(base + j * step, step)
            rows = lambda b: pl.ds(off + b * SC_GATHER_ROWS, SC_GATHER_ROWS)
            gathers, stores = [], []
            for b in range(SC_GATHER_BUFS):
                pltpu.sync_copy(idx_hbm.at[rows(b)], idx_v.at[b])
                gathers.append(pltpu.async_copy(table_hbm.at[idx_v.at[b]], rows_v.at[b], gather_sem.at[b]))
            for b in range(SC_GATHER_BUFS):
                gathers[b].wait()
                stores.append(pltpu.async_copy(rows_v.at[b], out_hbm.at[rows(b)], store_sem.at[b]))
            for b in range(SC_GATHER_BUFS):
                stores[b].wait()

    return gather(table, idx)


def _sc_scatter_pairs(rows, slots, n_slots):
    n, w = rows.shape
    n_workers = SC_CORES * SC_SUBCORES
    per_worker = n // n_workers
    step = SC_SCATTER_ROWS * SC_SCATTER_BUFS
    assert n % n_workers == 0 and per_worker % step == 0, (n, n_workers, step)
    mesh = plsc.VectorSubcoreMesh(core_axis_name="c", subcore_axis_name="s")

    @functools.partial(
        pl.kernel, mesh=mesh,
        out_type=jax.ShapeDtypeStruct((n_slots, w), rows.dtype),
        scratch_types=[pltpu.VMEM((2 * SC_SCATTER_BUFS, SC_SCATTER_ROWS), jnp.int32),
                       pltpu.VMEM((SC_SCATTER_BUFS, SC_SCATTER_ROWS, w), rows.dtype),
                       pltpu.SemaphoreType.DMA((SC_SCATTER_BUFS,)),
                       pltpu.SemaphoreType.DMA((SC_SCATTER_BUFS,))],
    )
    def scatter(rows_hbm, slots_hbm, out_hbm, idx_v, rows_v, load_sem, store_sem):
        worker = lax.axis_index("s") * SC_CORES + lax.axis_index("c")
        base = worker * per_worker

        @pl.loop(0, per_worker // step)
        def _(j):
            off = pl.multiple_of(base + j * step, SC_SCATTER_ROWS)
            loads, stores = [], []
            for b in range(SC_SCATTER_BUFS):
                r0 = off + b * SC_SCATTER_ROWS
                pltpu.sync_copy(slots_hbm.at[pl.ds(r0, SC_SCATTER_ROWS)], idx_v.at[2 * b])
                pltpu.sync_copy(slots_hbm.at[pl.ds(n + r0, SC_SCATTER_ROWS)], idx_v.at[2 * b + 1])
                loads.append(pltpu.async_copy(rows_hbm.at[pl.ds(r0, SC_SCATTER_ROWS)], rows_v.at[b], load_sem.at[b]))
            for b in range(SC_SCATTER_BUFS):
                loads[b].wait()
                for k in range(2):
                    stores.append(pltpu.async_copy(rows_v.at[b], out_hbm.at[idx_v.at[2 * b + k]], store_sem.at[b]))
            for copy in stores:
                copy.wait()

    return scatter(rows, slots)


def _expert_kernel(te_ref, nu_ref, x_ref, wg_ref, wu_ref, wd_ref, y_ref, wgb, wub, wdb):
    i = pl.program_id(0)
    live = i < nu_ref[0]
    new_expert = (i == 0) | (te_ref[i] != te_ref[jnp.maximum(i - 1, 0)])

    @pl.when(live & new_expert)
    def _():
        wgb[...] = wg_ref[0].astype(BF16)
        wub[...] = wu_ref[0].astype(BF16)
        wdb[...] = wd_ref[0].astype(BF16)

    @pl.when(live)
    def _():
        x = _unpack_bf16_pairs(x_ref[...]).astype(BF16)
        g = _dot(x, wgb[...])
        u = _dot(x, wub[...])
        y_ref[...] = _pack_bf16_pairs(_dot((g * _sigmoid(g) * u).astype(BF16), wdb[...]))

    @pl.when(jnp.logical_not(live))
    def _():
        y_ref[...] = jnp.zeros_like(y_ref)


def _experts(xs, tile_expert, n_used, w_gate, w_up, w_down, *, tm):
    n_slots, dp = xs.shape
    d, f = w_gate.shape[-2:]
    grid_spec = pltpu.PrefetchScalarGridSpec(
        num_scalar_prefetch=2,
        grid=(n_slots // tm,),
        in_specs=[pl.BlockSpec((tm, dp), lambda i, te, nu: (i, 0)),
                  pl.BlockSpec((1, d, f), lambda i, te, nu: (te[i], 0, 0)),
                  pl.BlockSpec((1, d, f), lambda i, te, nu: (te[i], 0, 0)),
                  pl.BlockSpec((1, f, d), lambda i, te, nu: (te[i], 0, 0))],
        out_specs=pl.BlockSpec((tm, dp), lambda i, te, nu: (i, 0)),
        scratch_shapes=[pltpu.VMEM((d, f), BF16), pltpu.VMEM((d, f), BF16), pltpu.VMEM((f, d), BF16)],
    )
    return pl.pallas_call(
        _expert_kernel,
        grid_spec=grid_spec,
        out_shape=jax.ShapeDtypeStruct((n_slots, dp), jnp.int32),
        compiler_params=_params(("arbitrary",)),
        name="moe_experts",
    )(tile_expert, n_used, xs, w_gate, w_up, w_down)


def _combine_kernel(h_ref, info_ref, g_ref, y0_ref, y1_ref, o_ref, *, final_norm):
    h2 = h_ref[...] + (info_ref[:, 2:3] * _unpack_bf16_pairs(y0_ref[...])
                       + info_ref[:, 3:4] * _unpack_bf16_pairs(y1_ref[...]))
    o_ref[...] = _rms(h2, g_ref[...]) if final_norm else h2


def _combine(h, yg, info, g, *, final_norm):
    n, d = h.shape
    tm = _tile(n, 512, SUBLANES)
    nt = n // tm
    return pl.pallas_call(
        functools.partial(_combine_kernel, final_norm=final_norm),
        grid=(nt,),
        in_specs=[pl.BlockSpec((tm, d), lambda i: (i, 0)),
                  pl.BlockSpec((tm, LANES), lambda i: (i, 0)),
                  pl.BlockSpec((1, d), lambda i: (0, 0)),
                  pl.BlockSpec((tm, d // 2), lambda i: (i, 0)),
                  pl.BlockSpec((tm, d // 2), lambda i: (nt + i, 0))],
        out_specs=pl.BlockSpec((tm, d), lambda i: (i, 0)),
        out_shape=jax.ShapeDtypeStruct((n, d), F32),
        compiler_params=_params(("parallel",)),
        name="moe_combine",
    )(h, info, g.reshape(1, d), yg, yg)


def _final_kernel(h_ref, info_ref, g_ref, y0_ref, y1_ref, yp_ref, ys_ref, *, tiles_per_batch, lead_tiles, prompt_tiles):
    i = pl.program_id(0)
    h2 = h_ref[...] + (info_ref[:, 2:3] * _unpack_bf16_pairs(y0_ref[...])
                       + info_ref[:, 3:4] * _unpack_bf16_pairs(y1_ref[...]))
    out = _rms(h2, g_ref[...])

    @pl.when((i < prompt_tiles) & (i % tiles_per_batch >= lead_tiles))
    def _():
        yp_ref[...] = out

    @pl.when(i >= prompt_tiles)
    def _():
        ys_ref[...] = out


def _final_combine(h, yg, info, g, *, nb, t, n_sample):
    n, d = h.shape
    tm = _tile(math.gcd(LEAD, n_sample), LANES, SUBLANES)
    nt = n // tm
    tpb, lead_tiles = t // tm, LEAD // tm
    prompt_tiles = nb * tpb
    keep = tpb - lead_tiles

    def prompt_block(i):
        ip = jnp.minimum(i, prompt_tiles - 1)
        return (ip // tpb) * keep + jnp.maximum(ip % tpb - lead_tiles, 0), 0

    return pl.pallas_call(
        functools.partial(_final_kernel, tiles_per_batch=tpb, lead_tiles=lead_tiles, prompt_tiles=prompt_tiles),
        grid=(nt,),
        in_specs=[pl.BlockSpec((tm, d), lambda i: (i, 0)),
                  pl.BlockSpec((tm, LANES), lambda i: (i, 0)),
                  pl.BlockSpec((1, d), lambda i: (0, 0)),
                  pl.BlockSpec((tm, d // 2), lambda i: (i, 0)),
                  pl.BlockSpec((tm, d // 2), lambda i: (nt + i, 0))],
        out_specs=[pl.BlockSpec((tm, d), prompt_block),
                   pl.BlockSpec((tm, d), lambda i: (jnp.maximum(i - prompt_tiles, 0), 0))],
        out_shape=[jax.ShapeDtypeStruct((nb * keep * tm, d), F32), jax.ShapeDtypeStruct((n_sample, d), F32)],
        compiler_params=_params(("arbitrary",)),
        name="moe_combine_final",
    )(h, info, g.reshape(1, d), yg, yg)


def _moe(h1, xn, info, counts_f, w_gate, w_up, w_down, expert_base, g_next, *, final=None):
    n, d = h1.shape
    n_e = N_GROUPS * N_EXP
    tm = _tile(2 * n, MOE_TILE, SUBLANES)
    n_tiles = (2 * n) // tm + n_e
    n_slots = n_tiles * tm
    experts = jnp.arange(n_e, dtype=jnp.int32)
    counts = counts_f[0, :n_e].astype(jnp.int32)
    padded = ((counts + tm - 1) // tm) * tm
    pad_ends = jnp.cumsum(padded)
    pad_starts = pad_ends - padded
    n_used = (pad_ends[-1] // tm).astype(jnp.int32)
    eid = info[:, 0:2].astype(jnp.int32)
    rank = info[:, 4:6].astype(jnp.int32)
    slot_of_pick = jnp.sum(jnp.where(eid[..., None] == experts, pad_starts, 0), axis=-1) + rank
    slots = slot_of_pick.T.reshape(-1)
    tile_starts = jnp.arange(n_tiles, dtype=jnp.int32) * tm
    tile_expert = expert_base + jnp.minimum(jnp.sum(tile_starts[:, None] >= pad_ends[None, :], axis=1), n_e - 1)

    xs = _sc_scatter_pairs(xn, slots, n_slots)
    ys = _experts(xs, tile_expert.astype(jnp.int32), n_used.reshape(1), w_gate, w_up, w_down, tm=tm)
    yg = _sc_gather(ys, slots)
    if final is not None:
        return _final_combine(h1, yg, info, g_next, **final)
    return _combine(h1, yg, info, g_next, final_norm=False)


def _router_weights(w_rg, b_rg, w_re, b_re):
    d = w_rg.shape[0]
    pad = LANES - N_GROUPS - N_GROUPS * N_EXP
    w = jnp.concatenate([w_rg, w_re, jnp.zeros((d, pad), F32)], axis=1)
    b = jnp.concatenate([b_rg, b_re, jnp.zeros((pad,), F32)]).reshape(1, LANES)
    hi = w.astype(BF16)
    lo = (w - hi.astype(F32)).astype(BF16)
    return hi, lo, b


def _rows_to_lanes(x, nb, t, nch):
    return x.reshape(nb, t, nch).transpose(0, 2, 1).reshape(nb * nch, t)


def _even_layer(h, dims, g_mix, w_in, w_g2, b_g, b_f, g_a, w_o, state_gla, ck, cv, clf):
    nb, t, db, ds, npr, n = dims
    d = h.shape[1]
    qa, ka, va, ra, ga, qb, kb, vb, fb = jnp.split(
        w_in, [A_QK, 2 * A_QK, 2 * A_QK + A_V, 2 * A_QK + 2 * A_V, 2 * A_QK + 2 * A_V + A_RANK,
               2 * A_QK + 2 * A_V + A_RANK + B_W, 2 * A_QK + 2 * A_V + A_RANK + 2 * B_W,
               2 * A_QK + 2 * A_V + A_RANK + 3 * B_W], axis=1)
    w_packed = jnp.concatenate(
        [qa, ka, va, ra, qb, kb, vb, ga, fb, jnp.zeros((d, LANES - A_RANK - B_HEADS), F32)], axis=1)
    q_off = 2 * A_QK + 2 * A_V
    colscale = jnp.ones((1, MAIN_W), F32).at[:, q_off:q_off + B_W].set(B_DH ** -0.5 * LOG2E)
    p, pb, k_rows, v_rows = _proj(h, g_mix, w_packed, n_rows=npr, colscale=colscale, kv_col=q_off + B_W)
    p = _proj_precise(h, g_mix, w_packed, p, row0=npr)

    nh = B_HEADS
    fcol = MAIN_W + A_RANK
    zeros_col = lambda r: jnp.zeros((r, 1), F32)
    bias_row = lambda lanes: jnp.tile(b_f, lanes // nh).reshape(1, lanes)
    fb_p = p[:npr, fcol:fcol + nh].reshape(nb, t * nh)
    logf_p, f_p = _gate_scan(fb_p, bias_row(t * nh), zeros_col(nb), mode="fox", act_start=0,
                             valid_start=N_PAD * nh, valid_end=t * nh, seg=None, stride=nh)
    past = ck.shape[1]
    n_c = past * nh
    x_s = _pad_lanes(jnp.concatenate([clf.reshape(db, n_c), p[npr:, fcol:fcol + nh].reshape(db, ds * nh)], axis=1))
    logf_s, f_s = _gate_scan(x_s, bias_row(x_s.shape[1]), zeros_col(db), mode="fox", act_start=n_c,
                             valid_start=0, valid_end=n_c + ds * nh, seg=None, stride=nh)

    nc = t // CHUNK
    oa, s_p = _gla(p, jnp.zeros((nb, A_HEADS, A_DK, A_DV), F32), w_g2, b_g,
                   nb=nb, nc=nc, L=CHUNK, row_block0=0, lead_pad=N_PAD, n_total=n)
    oa, s_s = _gla(p, state_gla, w_g2, b_g, nb=db, nc=1, L=ds, row_block0=npr // ds, lead_pad=0,
                   n_total=n, prev_out=oa, precise=True)

    fk = (f_p * LOG2E).reshape(nb, t, nh).transpose(0, 2, 1).reshape(nb, nh, 1, t)
    ob = _flash(pb, fk, nb=nb, t=t, n_total=n)
    f_cache = (f_s[:, :n_c] * LOG2E).reshape(db, 1, n_c)
    f_new = (f_s[:, n_c:n_c + ds * nh] * LOG2E).reshape(db, ds, nh).transpose(0, 2, 1).reshape(db, 1, nh * ds)
    ob = _fox_sample(p, ck.reshape(db, n_c, B_DH), cv.reshape(db, n_c, B_DH), f_cache, f_new, ob,
                     nb=db, ds=ds, row_block0=npr // ds)

    kcol = q_off + B_W
    states = dict(
        s_p=s_p, s_s=s_s,
        k_p=k_rows.reshape(nb, t, B_HEADS, B_DH)[:, N_PAD:],
        v_p=v_rows.reshape(nb, t, B_HEADS, B_DH)[:, N_PAD:],
        f_p=logf_p.reshape(nb, t, nh)[:, N_PAD:],
        k_s=p[npr:, kcol:kcol + B_W].reshape(db, ds, B_HEADS, B_DH),
        v_s=p[npr:, kcol + B_W:kcol + 2 * B_W].reshape(db, ds, B_HEADS, B_DH),
        f_s=logf_s[:, n_c:n_c + ds * nh].reshape(db, ds, nh))
    return (oa, 0, p, (2 * A_QK + A_V) // A_V, ob, g_a, w_o), states


def _chunk_rows(x, nb, nch, nc, L):
    x = x[:, :nc * L].reshape(nb, nch, nc, L)
    return x.transpose(0, 2, 1, 3).reshape(nb, nc, 1, nch * L), x.transpose(0, 2, 3, 1)


def _odd_layer(h, dims, g_mix, w_in, b_gate, g_c, w_o, c0, n0, m0):
    nb, t, db, ds, npr, n = dims
    d = h.shape[1]
    w_packed = jnp.concatenate(
        [w_in, jnp.zeros((d, LANES - 2 * C_HEADS), F32)], axis=1)
    (p,) = _proj(h, g_mix, w_packed, n_rows=npr)
    p = _proj_precise(h, g_mix, w_packed, p, row0=npr)

    ng = 2 * C_HEADS
    isf = (jnp.arange(ng) >= C_HEADS).astype(F32)
    nc = t // CHUNK

    def gates(rows, nbatch, tt, valid_start, seg):
        x = _pad_lanes(_rows_to_lanes(rows, nbatch, tt, ng))
        val, cum = _gate_scan(x, jnp.tile(b_gate, nbatch).reshape(-1, 1), jnp.tile(isf, nbatch).reshape(-1, 1),
                              mode="mlstm", act_start=0, valid_start=valid_start, valid_end=tt, seg=seg)
        return val.reshape(nbatch, ng, -1), cum.reshape(nbatch, ng, -1)

    val_p, cum_p = gates(p[:npr, MAIN_W:MAIN_W + ng], nb, t, N_PAD, CHUNK)
    val_s, cum_s = gates(p[npr:, MAIN_W:MAIN_W + ng], db, ds, 0, ds)

    def chunked(val, cum, nbatch, ncs, L):
        li_row, li_col = _chunk_rows(val[:, :C_HEADS].reshape(nbatch * C_HEADS, -1), nbatch, C_HEADS, ncs, L)
        b_row, b_col = _chunk_rows(cum[:, C_HEADS:].reshape(nbatch * C_HEADS, -1), nbatch, C_HEADS, ncs, L)
        return li_row, b_row, li_col, b_col

    zc = jnp.zeros((nb, C_HEADS, C_DV, C_DQK), F32)
    zn = jnp.zeros((nb, C_HEADS, C_DQK), F32)
    zm = jnp.zeros((nb, C_HEADS, 1), F32)
    hm, c_p, n_p, m_p = _mlstm(p, *chunked(val_p, cum_p, nb, nc, CHUNK), zc, zn, zm,
                               nb=nb, nc=nc, L=CHUNK, row_block0=0, n_total=n)
    hm, c_s, n_s, m_s = _mlstm(p, *chunked(val_s, cum_s, db, 1, ds), c0, n0, m0.reshape(db, C_HEADS, 1),
                               nb=db, nc=1, L=ds, row_block0=npr // ds, n_total=n, prev_out=hm)
    states = dict(c_p=c_p, n_p=n_p, m_p=m_p.reshape(nb, C_HEADS), c_s=c_s, n_s=n_s, m_s=m_s.reshape(db, C_HEADS))
    return (hm, 0, p, (2 * C_QK + C_V) // C_V, None, g_c, w_o), states


def kernel(x_prompt, x_sample, state_gla, cache_fox_k, cache_fox_v, cache_fox_logf, state_mlstm_c, state_mlstm_n, state_mlstm_m, meta_tokens, norm_mix, norm_ffn, norm_final, w_in_even, w_gla_gate2, b_gla_gate, b_fox_f, g_gla_out, w_out_even, w_in_odd, b_mlstm_gate, g_mlstm_out, w_out_odd, w_router_group, b_router_group, w_router_expert, b_router_expert, w_exp_gate, w_exp_up, w_exp_down):
    nb, seq, d = x_prompt.shape
    db, ds, _ = x_sample.shape
    t = LEAD + seq
    npr, nsm = nb * t, db * ds
    n = npr + nsm
    dims = (nb, t, db, ds, npr, n)
    depth = norm_mix.shape[0]
    n_e = N_GROUPS * N_EXP
    f = w_exp_gate.shape[-1]

    pad_rows = jnp.zeros((N_PAD, d), F32)
    h = jnp.concatenate([piece for b in range(nb) for piece in (pad_rows, meta_tokens, x_prompt[b])]
                        + [x_sample.reshape(nsm, d)], axis=0)
    wg_all = w_exp_gate.reshape(depth * n_e, d, f)
    wu_all = w_exp_up.reshape(depth * n_e, d, f)
    wd_all = w_exp_down.reshape(depth * n_e, f, d)

    even, odd = [], []
    for l in range(depth):
        if l % 2 == 0:
            e = l // 2
            mix, st = _even_layer(h, dims, norm_mix[l], w_in_even[e], w_gla_gate2[e], b_gla_gate[e], b_fox_f[e],
                                  g_gla_out[e], w_out_even[e], state_gla[e], cache_fox_k[e], cache_fox_v[e],
                                  cache_fox_logf[e])
            even.append(st)
            hd, act = A_DV, "silu"
        else:
            o = l // 2
            mix, st = _odd_layer(h, dims, norm_mix[l], w_in_odd[o], b_mlstm_gate[o], g_mlstm_out[o], w_out_odd[o],
                                 state_mlstm_c[o], state_mlstm_n[o], state_mlstm_m[o])
            odd.append(st)
            hd, act = C_DV, "sigmoid"
        a, a_col, r, r_col, b, g_head, w_o = mix
        wr_hi, wr_lo, b_r = _router_weights(w_router_group[l], b_router_group[l], w_router_expert[l],
                                            b_router_expert[l])
        mix_args = (h, a, a_col, r, r_col, b, g_head, w_o, norm_ffn[l], wr_hi, wr_lo, b_r)
        h1, xn, info, counts = _mixout(*mix_args, jnp.zeros((1, LANES), F32), hd=hd, act=act, row0=0, n_rows=npr)
        h1, xn, info, counts = _mixout(*mix_args, counts, hd=hd, act=act, row0=npr, n_rows=nsm,
                                       prev=(h1, xn, info))
        last = l == depth - 1
        h = _moe(h1, xn, info, counts, wg_all, wu_all, wd_all, l * n_e, norm_final if last else norm_ffn[l],
                 final=dict(nb=nb, t=t, n_sample=nsm) if last else None)

    y_prompt = h[0].reshape(nb, seq, d)
    y_sample = h[1].reshape(db, ds, d)
    stack = lambda sts, key: jnp.stack([s[key] for s in sts])
    return (y_prompt, y_sample,
            stack(even, "s_p"), stack(even, "k_p"), stack(even, "v_p"), stack(even, "f_p"),
            stack(odd, "c_p"), stack(odd, "n_p"), stack(odd, "m_p"),
            stack(even, "s_s"), stack(even, "k_s"), stack(even, "v_s"), stack(even, "f_s"),
            stack(odd, "c_s"), stack(odd, "n_s"), stack(odd, "m_s"))
```

```python
import functools
import math

import jax
import jax.numpy as jnp
from jax import lax
from jax.experimental import pallas as pl
from jax.experimental.pallas import tpu as pltpu
from jax.experimental.pallas import tpu_sc as plsc

F32 = jnp.float32
BF16 = jnp.bfloat16

CHUNK = 64
N_META = 16
LEAD = 128
N_PAD = LEAD - N_META
A_HEADS, A_DK, A_DV, A_RANK = 4, 64, 128, 16
A_GATE_NORM = 16.0
B_HEADS, B_DH = 4, 128
C_HEADS, C_DQK, C_DV = 4, 128, 256
GATE_CAP = 15.0
N_GROUPS, N_EXP = 4, 8
EPS = 1e-6
NEG = -1e30
LOG2E = 1.4426950408889634
A_QK = A_HEADS * A_DK
A_V = A_HEADS * A_DV
B_W = B_HEADS * B_DH
C_QK = C_HEADS * C_DQK
C_V = C_HEADS * C_DV

LANES = 128
SUBLANES = 8
VMEM_LIMIT_BYTES = 56 * 1024 * 1024
GLA_SUB = 16
SC_CORES, SC_SUBCORES = 2, 16
SC_GATHER_ROWS = 16
SC_GATHER_BUFS = 4
MOE_TILE = 1024
CHUNKS_PER_STEP = 5
SC_SCATTER_ROWS = 16
SC_SCATTER_BUFS = 3
FLASH_HEADS = 4
MAIN_W = 3072
PROJ_W = MAIN_W + LANES

_NT = (((1,), (1,)), ((), ()))
_TN = (((0,), (0,)), ((), ()))
_NN = (((1,), (0,)), ((), ()))


def _params(sem):
    return pltpu.CompilerParams(dimension_semantics=sem, vmem_limit_bytes=VMEM_LIMIT_BYTES)


def _tile(n, pref, mult):
    t = (min(pref, n) // mult) * mult
    while t > mult and n % t:
        t -= mult
    assert t >= mult and n % t == 0, (n, pref, mult)
    return t


def _dot(a, b, dims=_NN):
    return lax.dot_general(a, b, dims, preferred_element_type=F32)


def _split(x):
    hi = x.astype(BF16)
    lo = (x - hi.astype(F32)).astype(BF16)
    return hi, lo


def _dot3(a, b, dims=_NN):
    ah, al = _split(a)
    bh, bl = _split(b)
    return _dot(ah, bh, dims) + _dot(ah, bl, dims) + _dot(al, bh, dims)


def _log_sigmoid(x):
    return jnp.minimum(x, 0.0) - jnp.log1p(jnp.exp(-jnp.abs(x)))


def _sigmoid(x):
    return 1.0 / (1.0 + jnp.exp(-x))


def _rms(x, g):
    return x * lax.rsqrt(jnp.mean(x * x, axis=-1, keepdims=True) + EPS) * g


def _pack_bf16_pairs(x):
    w = x.shape[1] // 2
    hi = lax.bitcast_convert_type(x[:, :w].astype(BF16).astype(F32), jnp.int32)
    lo = lax.bitcast_convert_type(x[:, w:].astype(BF16).astype(F32), jnp.int32)
    return hi | lax.shift_right_logical(lo, 16)


def _unpack_bf16_pairs(p):
    hi = lax.bitcast_convert_type(p & jnp.int32(-65536), F32)
    lo = lax.bitcast_convert_type(lax.shift_left(p, 16), F32)
    return jnp.concatenate([hi, lo], axis=1)


def _cumsum_rows(x):
    n = x.shape[0]
    row = lax.broadcasted_iota(jnp.int32, x.shape, 0)
    s = 1
    while s < n:
        x = x + jnp.where(row >= s, pltpu.roll(x, s, axis=0), 0.0)
        s *= 2
    return x


def _proj_kernel(x_ref, g_ref, w_ref, *rest, col_chunk, kv_col):
    if kv_col is None:
        (o_ref,) = rest
    else:
        cs_ref, o_ref, ob_ref, *kv_refs = rest
    tm = x_ref.shape[0]
    xn = _rms(x_ref[...], g_ref[...]).astype(BF16)
    for c0 in range(0, PROJ_W, col_chunk):
        c1 = min(c0 + col_chunk, PROJ_W)
        y = _dot(xn, w_ref[:, c0:c1])
        o_ref[:, c0:c1] = y
        if kv_col is not None:
            if c0 < MAIN_W:
                m1 = min(c1, MAIN_W)
                ob_ref[:, c0:m1] = (y[:, :m1 - c0] * cs_ref[:, c0:m1]).astype(BF16)
            for g0 in range(c0, c1, LANES):
                rel = g0 - kv_col
                if 0 <= rel < 2 * B_W:
                    head = (rel % B_W) // B_DH
                    kv_refs[rel // B_W][pl.ds(head, tm, stride=B_HEADS), :] = y[:, g0 - c0:g0 - c0 + LANES]


def _proj_precise_kernel(x_ref, g_ref, w_ref, prev_ref, o_ref, *, col_chunk):
    del prev_ref
    xn = _rms(x_ref[...], g_ref[...])
    for c0 in range(0, PROJ_W, col_chunk):
        c1 = min(c0 + col_chunk, PROJ_W)
        o_ref[:, c0:c1] = _dot3(xn, w_ref[:, c0:c1])


def _proj(h, g, w_packed, *, n_rows, colscale=None, kv_col=None):
    n, d = h.shape
    tm = _tile(n_rows, 512, 16)
    in_specs = [pl.BlockSpec((tm, d), lambda i: (i, 0)),
                pl.BlockSpec((1, d), lambda i: (0, 0)),
                pl.BlockSpec((d, PROJ_W), lambda i: (0, 0))]
    args = [h, g.reshape(1, d), w_packed.astype(BF16)]
    out_specs = [pl.BlockSpec((tm, PROJ_W), lambda i: (i, 0))]
    out_shape = [jax.ShapeDtypeStruct((n, PROJ_W), F32)]
    if kv_col is not None:
        in_specs.append(pl.BlockSpec((1, MAIN_W), lambda i: (0, 0)))
        args.append(colscale)
        out_specs += [pl.BlockSpec((tm, MAIN_W), lambda i: (i, 0))] + [pl.BlockSpec((tm * B_HEADS, B_DH), lambda i: (i, 0))] * 2
        out_shape += ([jax.ShapeDtypeStruct((n_rows, MAIN_W), BF16)]
                      + [jax.ShapeDtypeStruct((n_rows * B_HEADS, B_DH), F32)] * 2)
    return pl.pallas_call(
        functools.partial(_proj_kernel, col_chunk=640, kv_col=kv_col),
        grid=(n_rows // tm,),
        in_specs=in_specs,
        out_specs=out_specs,
        out_shape=out_shape,
        compiler_params=_params(("parallel",)),
        name="proj",
    )(*args)


def _proj_precise(h, g, w_packed, prev, *, row0):
    n, d = h.shape
    tm = _tile(n - row0, 512, SUBLANES)
    assert row0 % tm == 0
    return pl.pallas_call(
        functools.partial(_proj_precise_kernel, col_chunk=640),
        grid=((n - row0) // tm,),
        in_specs=[pl.BlockSpec((tm, d), lambda i: (row0 // tm + i, 0)),
                  pl.BlockSpec((1, d), lambda i: (0, 0)),
                  pl.BlockSpec((d, PROJ_W), lambda i: (0, 0)),
                  pl.BlockSpec(memory_space=pl.ANY)],
        out_specs=pl.BlockSpec((tm, PROJ_W), lambda i: (row0 // tm + i, 0)),
        out_shape=jax.ShapeDtypeStruct((n, PROJ_W), F32),
        input_output_aliases={3: 0},
        compiler_params=_params(("parallel",)),
        name="proj_precise",
    )(h, g.reshape(1, d), w_packed, prev)


def _gate_scan_kernel(x_ref, bias_ref, isf_ref, val_ref, cum_ref, *, mode, act_start, valid_start, valid_end, seg, stride):
    x = x_ref[...]
    lane = lax.broadcasted_iota(jnp.int32, x.shape, 1)
    valid = (lane >= valid_start) & (lane < valid_end)
    if mode == "fox":
        val = jnp.where(lane >= act_start, _log_sigmoid(x + bias_ref[...]), x)
        val = jnp.where(valid, val, 0.0)
        add = val
    else:
        gate = GATE_CAP * jnp.tanh((x + bias_ref[...]) / GATE_CAP)
        isf = isf_ref[...] > 0.5
        val = jnp.where(isf, jnp.where(valid, _log_sigmoid(gate), 0.0),
                        jnp.where(valid, gate, -jnp.inf))
        add = jnp.where(isf, val, 0.0)
    val_ref[...] = val
    n = x.shape[1]
    pos = lane if seg is None else lane % seg
    limit = n if seg is None else seg
    s = stride
    while s < limit:
        add = add + jnp.where(pos >= s, pltpu.roll(add, s, axis=1), 0.0)
        s *= 2
    cum_ref[...] = add


def _gate_scan(x, bias, isf, *, mode, act_start, valid_start, valid_end, seg, stride=1):
    r, n = x.shape
    full = lambda shape: pl.BlockSpec(shape, lambda i: (0,) * len(shape))
    return pl.pallas_call(
        functools.partial(_gate_scan_kernel, mode=mode, act_start=act_start,
                          valid_start=valid_start, valid_end=valid_end, seg=seg, stride=stride),
        grid=(1,),
        in_specs=[full((r, n)), full(bias.shape), full((r, 1))],
        out_specs=[full((r, n)), full((r, n))],
        out_shape=[jax.ShapeDtypeStruct((r, n), F32)] * 2,
        compiler_params=_params(("arbitrary",)),
        name="gate_scan_" + mode,
    )(x, bias, isf)


def _pad_lanes(x):
    n = x.shape[-1]
    m = -(-n // LANES) * LANES
    return x if m == n else jnp.pad(x, ((0, 0), (0, m - n)))


def _gla_kernel(qk_ref, v_ref, sm_ref, s0_ref, wg2_ref, wg2t_ref, bgr_ref, bgc_ref, *rest,
                L, cps, sub, lead_pad, aliased, precise):
    if aliased:
        rest = rest[1:]
    o_ref, sout_ref, s_scr = rest
    step = pl.program_id(1)
    for j in range(cps):
        rows = pl.ds(j * L, L)
        _gla_chunk(qk_ref.at[rows], v_ref.at[rows], sm_ref.at[rows], s0_ref, wg2_ref, wg2t_ref, bgr_ref, bgc_ref,
                   o_ref.at[rows], sout_ref, s_scr, c=step * cps + j, first=(j == 0), last=(j == cps - 1),
                   L=L, sub=sub, lead_pad=lead_pad, precise=precise)


def _gla_chunk(qk_ref, v_ref, sm_ref, s0_ref, wg2_ref, wg2t_ref, bgr_ref, bgc_ref, o_ref, sout_ref, s_scr, *,
               c, first, last, L, sub, lead_pad, precise):
    nh, dk, dv = A_HEADS, A_DK, A_DV
    cast = (lambda x: x) if precise else (lambda x: x.astype(BF16))
    mm = _dot3 if precise else _dot

    if first:
        @pl.when(pl.program_id(1) == 0)
        def _():
            s_scr[...] = jnp.zeros_like(s_scr)
            for h in range(nh):
                s_scr[h * dk:(h + 1) * dk, h * dv:(h + 1) * dv] = s0_ref[0, h]

    qk = qk_ref[...]
    q = qk[:, :A_QK] * (A_DK ** -0.5)
    k = qk[:, A_QK:]
    v = v_ref[...]
    ga = sm_ref[:, :A_RANK]
    row = lax.broadcasted_iota(jnp.int32, (L, 1), 0)
    valid = (c * L + row) >= lead_pad
    z = _dot3(ga, wg2_ref[...]) + bgr_ref[...]
    loga = jnp.where(valid, _log_sigmoid(z) / A_GATE_NORM, 0.0)
    k = jnp.where(valid, k, 0.0)
    b = _cumsum_rows(loga)
    b_last = b[L - 1:L, :]
    lane_t = lax.broadcasted_iota(jnp.int32, (1, L), 1)
    zt = _dot3(wg2t_ref[...], ga, _NT) + bgc_ref[...]
    logat = jnp.where((c * L + lane_t) >= lead_pad, _log_sigmoid(zt) / A_GATE_NORM, 0.0)
    b_last_col = jnp.sum(logat, axis=1, keepdims=True)

    qhead = lax.broadcasted_iota(jnp.int32, (1, A_QK), 1) // dk
    vhead = lax.broadcasted_iota(jnp.int32, (1, A_V), 1) // dv
    vb = cast(v)
    zero_b = jnp.zeros((), vb.dtype)
    v_bd = jnp.concatenate([jnp.where(vhead == h, vb, zero_b) for h in range(nh)], axis=0)

    rows_all = lax.broadcasted_iota(jnp.int32, (L, 1), 0)
    a_rows = []
    for i in range(L // sub):
        r0 = i * sub
        ci = jnp.zeros((1, A_QK), F32) if i == 0 else b[r0 - 1:r0, :]
        qt = cast(q[r0:r0 + sub] * jnp.exp(b[r0:r0 + sub] - ci))
        kt = cast(jnp.where(rows_all < r0 + sub, k * jnp.exp(ci - b), 0.0))
        k_stack = jnp.concatenate([jnp.where(qhead == h, kt, zero_b) for h in range(nh)], axis=0)
        a_rows.append(mm(qt, k_stack, _NT))
    a = a_rows[0] if len(a_rows) == 1 else jnp.concatenate(a_rows, axis=0)
    t_idx = lax.broadcasted_iota(jnp.int32, (L, nh * L), 0)
    s_idx = lax.broadcasted_iota(jnp.int32, (L, nh * L), 1) % L
    a = jnp.where(s_idx <= t_idx, a, 0.0)
    o_intra = mm(cast(a), v_bd)

    s_full = s_scr[...]
    o_inter = mm(cast(q * jnp.exp(b)), cast(s_full))
    o_ref[...] = o_inter + o_intra

    k_hat = cast(k * jnp.exp(b_last - b))
    upd = mm(k_hat, vb, _TN)
    khead_col = lax.broadcasted_iota(jnp.int32, (A_QK, 1), 0) // dk
    s_new = jnp.exp(b_last_col) * s_full + jnp.where(khead_col == vhead, upd, 0.0)
    s_scr[...] = s_new

    if last:
        @pl.when(pl.program_id(1) == pl.num_programs(1) - 1)
        def _():
            for h in range(nh):
                sout_ref[0, h] = s_new[h * dk:(h + 1) * dk, h * dv:(h + 1) * dv]


def _chunks_per_step(nc):
    return next(k for k in (CHUNKS_PER_STEP, 2, 1) if nc % k == 0)


def _gla(p, s0, wg2, bg, *, nb, nc, L, row_block0, lead_pad, n_total, prev_out=None, precise=False):
    aliased = prev_out is not None
    cps = _chunks_per_step(nc)
    assert row_block0 % cps == 0
    nc, L, row_block0, chunk = nc // cps, L * cps, row_block0 // cps, L
    rb = lambda b, c: row_block0 + b * nc + c
    in_specs = [pl.BlockSpec((L, 2 * A_QK), lambda b, c: (rb(b, c), 0)),
                pl.BlockSpec((L, A_V), lambda b, c: (rb(b, c), 2 * A_QK // A_V)),
                pl.BlockSpec((L, LANES), lambda b, c: (rb(b, c), MAIN_W // LANES)),
                pl.BlockSpec((1, A_HEADS, A_DK, A_DV), lambda b, c: (b, 0, 0, 0)),
                pl.BlockSpec((A_RANK, A_QK), lambda b, c: (0, 0)),
                pl.BlockSpec((A_QK, A_RANK), lambda b, c: (0, 0)),
                pl.BlockSpec((1, A_QK), lambda b, c: (0, 0)),
                pl.BlockSpec((A_QK, 1), lambda b, c: (0, 0))]
    args = [p, p, p, s0, wg2, wg2.T, bg.reshape(1, A_QK), bg.reshape(A_QK, 1)]
    io_alias = {}
    if aliased:
        in_specs.append(pl.BlockSpec(memory_space=pl.ANY))
        args.append(prev_out)
        io_alias = {len(args) - 1: 0}
    return pl.pallas_call(
        functools.partial(_gla_kernel, L=chunk, cps=cps, sub=min(GLA_SUB, chunk), lead_pad=lead_pad,
                          aliased=aliased, precise=precise),
        grid=(nb, nc),
        in_specs=in_specs,
        out_specs=[pl.BlockSpec((L, A_V), lambda b, c: (rb(b, c), 0)),
                   pl.BlockSpec((1, A_HEADS, A_DK, A_DV), lambda b, c: (b, 0, 0, 0))],
        out_shape=[jax.ShapeDtypeStruct((n_total, A_V), F32),
                   jax.ShapeDtypeStruct((nb, A_HEADS, A_DK, A_DV), F32)],
        scratch_shapes=[pltpu.VMEM((A_QK, A_V), F32)],
        input_output_aliases=io_alias,
        compiler_params=_params(("parallel", "arbitrary")),
        name="gla_L%d" % chunk,
    )(*args)


def _flash_kernel(qi_ref, kj_ref, q_ref, k_ref, v_ref, fk_ref, o_ref, m_scr, l_scr, acc_scr, *, blk, lead_pad):
    step = pl.program_id(2)
    i = qi_ref[step]
    j = kj_ref[step]

    @pl.when(j == 0)
    def _():
        m_scr[...] = jnp.full_like(m_scr, -jnp.inf)
        l_scr[...] = jnp.zeros_like(l_scr)
        acc_scr[...] = jnp.zeros_like(acc_scr)

    def update(masked):
        for g in range(FLASH_HEADS):
            sl = slice(g * B_DH, (g + 1) * B_DH)
            s = _dot(q_ref[:, sl], k_ref[:, sl], _NT) - fk_ref[0, g]
            if masked:
                qpos = i * blk + lax.broadcasted_iota(jnp.int32, (blk, blk), 0)
                kpos = j * blk + lax.broadcasted_iota(jnp.int32, (blk, blk), 1)
                s = jnp.where((kpos <= qpos) & (kpos >= lead_pad), s, NEG)
            m_prev = m_scr[g]
            m_new = jnp.maximum(m_prev, jnp.max(s, axis=1, keepdims=True))
            alpha = jnp.exp2(m_prev - m_new)
            p = jnp.exp2(s - m_new)
            l_scr[g] = alpha * l_scr[g] + jnp.sum(p, axis=1, keepdims=True)
            acc_scr[g] = alpha * acc_scr[g] + _dot(p.astype(BF16), v_ref[:, sl])
            m_scr[g] = m_new

    edge = (j == i) | (j == 0)
    pl.when(edge)(functools.partial(update, True))
    pl.when(jnp.logical_not(edge))(functools.partial(update, False))

    @pl.when(j == i)
    def _():
        for g in range(FLASH_HEADS):
            o_ref[:, g * B_DH:(g + 1) * B_DH] = acc_scr[g] / l_scr[g]


def _flash(pb, fk, *, nb, t, n_total):
    blk = _tile(t, 640, LANES)
    nq = t // blk
    hg = FLASH_HEADS
    w = hg * B_DH
    pairs = [(i, j) for i in range(nq) for j in range(i + 1)]
    qi = jnp.asarray([p[0] for p in pairs], jnp.int32)
    kj = jnp.asarray([p[1] for p in pairs], jnp.int32)
    qc, kc, vc = (A_QK * 2 + A_V * 2) // w, (A_QK * 2 + A_V * 2 + B_W) // w, (A_QK * 2 + A_V * 2 + 2 * B_W) // w
    grid_spec = pltpu.PrefetchScalarGridSpec(
        num_scalar_prefetch=2,
        grid=(nb, B_HEADS // hg, len(pairs)),
        in_specs=[pl.BlockSpec((blk, w), lambda b, h, s, qi, kj: (b * nq + qi[s], qc + h)),
                  pl.BlockSpec((blk, w), lambda b, h, s, qi, kj: (b * nq + kj[s], kc + h)),
                  pl.BlockSpec((blk, w), lambda b, h, s, qi, kj: (b * nq + kj[s], vc + h)),
                  pl.BlockSpec((1, hg, 1, blk), lambda b, h, s, qi, kj: (b, h, 0, kj[s]))],
        out_specs=pl.BlockSpec((blk, w), lambda b, h, s, qi, kj: (b * nq + qi[s], h)),
        scratch_shapes=[pltpu.VMEM((hg, blk, 1), F32), pltpu.VMEM((hg, blk, 1), F32),
                        pltpu.VMEM((hg, blk, B_DH), F32)],
    )
    return pl.pallas_call(
        functools.partial(_flash_kernel, blk=blk, lead_pad=N_PAD),
        grid_spec=grid_spec,
        out_shape=jax.ShapeDtypeStruct((n_total, B_W), F32),
        compiler_params=_params(("parallel", "parallel", "arbitrary")),
        name="fox_flash",
    )(qi, kj, pb, pb, pb, fk)


def _fox_sample_kernel(q_ref, kn_ref, vn_ref, kc_ref, vc_ref, fc_ref, fn_ref, prev_ref, o_ref, *, ds):
    del prev_ref
    nh = B_HEADS
    stack = lambda ref: jnp.concatenate([ref[:, h * B_DH:(h + 1) * B_DH] for h in range(nh)], axis=0)
    q = stack(q_ref) * (B_DH ** -0.5 * LOG2E)
    rows = nh * ds
    n_c = kc_ref.shape[1]
    qh_c = lax.broadcasted_iota(jnp.int32, (rows, n_c), 0) // ds
    kh_c = lax.broadcasted_iota(jnp.int32, (rows, n_c), 1) % nh
    s_c = _dot3(q, kc_ref[0], _NT) - fc_ref[0]
    s_c = jnp.where(qh_c == kh_c, s_c, NEG)
    r_i = lax.broadcasted_iota(jnp.int32, (rows, rows), 0)
    c_i = lax.broadcasted_iota(jnp.int32, (rows, rows), 1)
    s_n = _dot3(q, stack(kn_ref), _NT) - fn_ref[0]
    s_n = jnp.where((r_i // ds == c_i // ds) & (c_i % ds <= r_i % ds), s_n, NEG)
    m = jnp.maximum(jnp.max(s_c, axis=1, keepdims=True), jnp.max(s_n, axis=1, keepdims=True))
    p_c = jnp.exp2(s_c - m)
    p_n = jnp.exp2(s_n - m)
    l = jnp.sum(p_c, axis=1, keepdims=True) + jnp.sum(p_n, axis=1, keepdims=True)
    o = (_dot3(p_c, vc_ref[0]) + _dot3(p_n, stack(vn_ref))) / l
    for h in range(nh):
        o_ref[:, h * B_DH:(h + 1) * B_DH] = o[h * ds:(h + 1) * ds]


def _fox_sample(pb, kc, vc, f_cache, f_new, prev_out, *, nb, ds, row_block0):
    n_c = kc.shape[1]
    base = (A_QK * 2 + A_V * 2) // B_W
    rb = lambda b: row_block0 + b
    return pl.pallas_call(
        functools.partial(_fox_sample_kernel, ds=ds),
        grid=(nb,),
        in_specs=[pl.BlockSpec((ds, B_W), lambda b: (rb(b), base)),
                  pl.BlockSpec((ds, B_W), lambda b: (rb(b), base + 1)),
                  pl.BlockSpec((ds, B_W), lambda b: (rb(b), base + 2)),
                  pl.BlockSpec((1, n_c, B_DH), lambda b: (b, 0, 0)),
                  pl.BlockSpec((1, n_c, B_DH), lambda b: (b, 0, 0)),
                  pl.BlockSpec((1, 1, n_c), lambda b: (b, 0, 0)),
                  pl.BlockSpec((1, 1, B_HEADS * ds), lambda b: (b, 0, 0)),
                  pl.BlockSpec(memory_space=pl.ANY)],
        out_specs=pl.BlockSpec((ds, B_W), lambda b: (rb(b), 0)),
        out_shape=jax.ShapeDtypeStruct(prev_out.shape, F32),
        input_output_aliases={7: 0},
        compiler_params=_params(("parallel",)),
        name="fox_sample",
    )(pb, pb, pb, kc, vc, f_cache, f_new, prev_out)


def _mlstm_kernel(q_ref, k_ref, v_ref, lir_ref, br_ref, lic_ref, bc_ref, c0_ref, n0_ref, m0_ref, *rest,
                  L, cps, aliased):
    if aliased:
        rest = rest[1:]
    h_ref, cout_ref, nout_ref, mout_ref, c_scr, n_scr, m_scr = rest
    step = pl.program_id(1)

    @pl.when(step == 0)
    def _():
        c_scr[...] = c0_ref[0]
        n_scr[...] = n0_ref[0]
        m_scr[...] = m0_ref[0]

    for j in range(cps):
        rows = pl.ds(j * L, L)
        _mlstm_chunk(q_ref.at[rows], k_ref.at[rows], v_ref.at[rows], lir_ref.at[0, j], br_ref.at[0, j],
                     lic_ref.at[0, j], bc_ref.at[0, j], h_ref.at[rows], c_scr, n_scr, m_scr, L=L)

    @pl.when(step == pl.num_programs(1) - 1)
    def _():
        cout_ref[0] = c_scr[...]
        nout_ref[0] = n_scr[...]
        mout_ref[0] = m_scr[...]


def _mlstm_chunk(q_ref, k_ref, v_ref, lir_ref, br_ref, lic_ref, bc_ref, h_ref, c_scr, n_scr, m_scr, *, L):
    nh = C_HEADS
    hl = nh * L
    seg = lax.broadcasted_iota(jnp.int32, (1, hl), 1) // L
    causal = (lax.broadcasted_iota(jnp.int32, (L, hl), 1) % L) <= lax.broadcasted_iota(jnp.int32, (L, hl), 0)

    def per_head(vals):
        out = vals[0]
        for h in range(1, nh):
            out = jnp.where(seg == h, vals[h], out)
        return out

    seg_max = lambda x, h: jnp.max(jnp.where(seg == h, x, -jnp.inf), axis=1, keepdims=True)

    qf = q_ref[...]
    kf = k_ref[...] * (C_DQK ** -0.5)
    qb = qf.astype(BF16)
    kb = kf.astype(BF16)
    vb = v_ref[...].astype(BF16)
    b_row = br_ref[...]
    li_row = lir_ref[...]
    b_col = [bc_ref[:, h:h + 1] for h in range(nh)]
    li_col = [lic_ref[:, h:h + 1] for h in range(nh)]
    m_prev = [m_scr[h:h + 1, :] for h in range(nh)]

    d = jnp.where(causal, per_head(b_col) - b_row + li_row, -jnp.inf)
    inter = [b_col[h] + m_prev[h] for h in range(nh)]
    m_t = [jnp.maximum(inter[h], seg_max(d, h)) for h in range(nh)]
    pm = jnp.exp(d - per_head(m_t))
    w_inter = [jnp.exp(inter[h] - m_t[h]) for h in range(nh)]

    khead = lax.broadcasted_iota(jnp.int32, (1, C_QK), 1) // C_DQK
    vhead = lax.broadcasted_iota(jnp.int32, (1, C_V), 1) // C_DV
    zero_b = jnp.zeros((), BF16)
    k_stack = jnp.concatenate([jnp.where(khead == h, kb, zero_b) for h in range(nh)], axis=0)
    v_bd = jnp.concatenate([jnp.where(vhead == h, vb, zero_b) for h in range(nh)], axis=0)
    sqk = _dot(qb, k_stack, _NT) * pm
    sv = _dot(sqk.astype(BF16), v_bd)

    b_last = [b_row[:, h * L + L - 1:h * L + L] for h in range(nh)]
    g_row = per_head(b_last) - b_row + li_row
    for h in range(nh):
        qk_sl = slice(h * C_DQK, (h + 1) * C_DQK)
        v_sl = slice(h * C_DV, (h + 1) * C_DV)
        c_prev = c_scr[h]
        n_prev = n_scr[h:h + 1, :]
        num = w_inter[h] * _dot(qb[:, qk_sl], c_prev.astype(BF16), _NT) + sv[:, v_sl]
        den = (w_inter[h] * jnp.sum(qf[:, qk_sl] * n_prev, axis=1, keepdims=True)
               + jnp.sum(jnp.where(seg == h, sqk, 0.0), axis=1, keepdims=True))
        h_ref[:, v_sl] = num / jnp.maximum(jnp.abs(den), jnp.exp(-m_t[h]))

        m_new = jnp.maximum(b_last[h] + m_prev[h], seg_max(g_row, h))
        w_c = jnp.exp(b_last[h] + m_prev[h] - m_new)
        kw = kf[:, qk_sl] * jnp.exp(b_last[h] - b_col[h] + li_col[h] - m_new)
        c_scr[h] = w_c * c_prev + _dot(vb[:, v_sl], kw.astype(BF16), _TN)
        n_scr[h:h + 1, :] = w_c * n_prev + jnp.sum(kw, axis=0, keepdims=True)
        m_scr[h:h + 1, :] = m_new


def _mlstm(p, li_row, b_row, li_col, b_col, c0, n0, m0, *, nb, nc, L, row_block0, n_total, prev_out=None):
    aliased = prev_out is not None
    cps = _chunks_per_step(nc)
    assert row_block0 % cps == 0
    nc, row_block0, chunk, L = nc // cps, row_block0 // cps, L, L * cps
    rb = lambda b, c: row_block0 + b * nc + c
    in_specs = [pl.BlockSpec((L, C_QK), lambda b, c: (rb(b, c), 0)),
                pl.BlockSpec((L, C_QK), lambda b, c: (rb(b, c), 1)),
                pl.BlockSpec((L, C_V), lambda b, c: (rb(b, c), 2 * C_QK // C_V)),
                pl.BlockSpec((1, cps, 1, C_HEADS * chunk), lambda b, c: (b, c, 0, 0)),
                pl.BlockSpec((1, cps, 1, C_HEADS * chunk), lambda b, c: (b, c, 0, 0)),
                pl.BlockSpec((1, cps, chunk, C_HEADS), lambda b, c: (b, c, 0, 0)),
                pl.BlockSpec((1, cps, chunk, C_HEADS), lambda b, c: (b, c, 0, 0)),
                pl.BlockSpec((1, C_HEADS, C_DV, C_DQK), lambda b, c: (b, 0, 0, 0)),
                pl.BlockSpec((1, C_HEADS, C_DQK), lambda b, c: (b, 0, 0)),
                pl.BlockSpec((1, C_HEADS, 1), lambda b, c: (b, 0, 0))]
    args = [p, p, p, li_row, b_row, li_col, b_col, c0, n0, m0]
    io_alias = {}
    if aliased:
        in_specs.append(pl.BlockSpec(memory_space=pl.ANY))
        args.append(prev_out)
        io_alias = {len(args) - 1: 0}
    return pl.pallas_call(
        functools.partial(_mlstm_kernel, L=chunk, cps=cps, aliased=aliased),
        grid=(nb, nc),
        in_specs=in_specs,
        out_specs=[pl.BlockSpec((L, C_V), lambda b, c: (rb(b, c), 0)),
                   pl.BlockSpec((1, C_HEADS, C_DV, C_DQK), lambda b, c: (b, 0, 0, 0)),
                   pl.BlockSpec((1, C_HEADS, C_DQK), lambda b, c: (b, 0, 0)),
                   pl.BlockSpec((1, C_HEADS, 1), lambda b, c: (b, 0, 0))],
        out_shape=[jax.ShapeDtypeStruct((n_total, C_V), F32),
                   jax.ShapeDtypeStruct((nb, C_HEADS, C_DV, C_DQK), F32),
                   jax.ShapeDtypeStruct((nb, C_HEADS, C_DQK), F32),
                   jax.ShapeDtypeStruct((nb, C_HEADS, 1), F32)],
        scratch_shapes=[pltpu.VMEM((C_HEADS, C_DV, C_DQK), F32),
                        pltpu.VMEM((C_HEADS, C_DQK), F32),
                        pltpu.VMEM((C_HEADS, 1), F32)],
        input_output_aliases=io_alias,
        compiler_params=_params(("parallel", "arbitrary")),
        name="mlstm_L%d" % chunk,
    )(*args)


def _mixout_kernel(*refs, hd, act, has_b, precise, n_prev):
    refs = list(refs)
    h_ref, a_ref, r_ref = refs[:3]
    b_ref = refs[3] if has_b else None
    k = 4 if has_b else 3
    ga_ref, wo_ref, gf_ref, wrh_ref, wrl_ref, br_ref, cnt0_ref = refs[k:k + 7]
    h1_ref, xn_ref, info_ref, cnt_ref, cnt_scr = refs[k + 7 + n_prev:]
    cast = (lambda x: x) if precise else (lambda x: x.astype(BF16))
    a = a_ref[...]
    r = r_ref[...]
    gate = r * _sigmoid(r) if act == "silu" else _sigmoid(r)
    parts = []
    for hh in range(a.shape[1] // hd):
        sl = slice(hh * hd, (hh + 1) * hd)
        parts.append(cast(_rms(a[:, sl], ga_ref[...]) * gate[:, sl]))
    if has_b:
        parts.append(cast(b_ref[...]))
    cat = jnp.concatenate(parts, axis=1)
    h1 = h_ref[...] + (_dot3(cat, wo_ref[...]) if precise else _dot(cat, wo_ref[...]))
    h1_ref[...] = h1
    xn = _rms(h1, gf_ref[...])
    xn_ref[...] = _pack_bf16_pairs(xn)
    xh, xl = _split(xn)
    logits = _dot(xh, wrh_ref[...]) + _dot(xh, wrl_ref[...]) + _dot(xl, wrh_ref[...]) + br_ref[...]

    lane = lax.broadcasted_iota(jnp.int32, logits.shape, 1)
    lanef = lane.astype(F32)
    is_g = lane < N_GROUPS
    gl = jnp.where(is_g, logits, -jnp.inf)
    gmax = jnp.max(gl, axis=1, keepdims=True)
    gidx = jnp.min(jnp.where(gl == gmax, lanef, float(LANES)), axis=1, keepdims=True)
    wg = 1.0 / jnp.sum(jnp.where(is_g, jnp.exp(gl - gmax), 0.0), axis=1, keepdims=True)
    lo = N_GROUPS + N_EXP * gidx
    el = jnp.where((lanef >= lo) & (lanef < lo + N_EXP), logits, -jnp.inf)
    m1 = jnp.max(el, axis=1, keepdims=True)
    i1 = jnp.min(jnp.where(el == m1, lanef, float(LANES)), axis=1, keepdims=True)
    el2 = jnp.where(lanef == i1, -jnp.inf, el)
    m2 = jnp.max(el2, axis=1, keepdims=True)
    i2 = jnp.min(jnp.where(el2 == m2, lanef, float(LANES)), axis=1, keepdims=True)
    t = jnp.exp(m2 - m1)
    w1 = wg / (1.0 + t)
    w2 = wg * t / (1.0 + t)
    e1 = i1 - N_GROUPS
    e2 = i2 - N_GROUPS

    @pl.when(pl.program_id(0) == 0)
    def _():
        cnt_scr[...] = cnt0_ref[...]

    tm = logits.shape[0]
    pick = jnp.where((lanef == e1) | (lanef == e2), 1.0, 0.0)
    earlier = (lax.broadcasted_iota(jnp.int32, (tm, tm), 1) < lax.broadcasted_iota(jnp.int32, (tm, tm), 0))
    before = _dot(jnp.where(earlier, 1.0, 0.0).astype(BF16), pick.astype(BF16)) + cnt_scr[...]
    r1 = jnp.sum(jnp.where(lanef == e1, before, 0.0), axis=1, keepdims=True)
    r2 = jnp.sum(jnp.where(lanef == e2, before, 0.0), axis=1, keepdims=True)
    cnt_new = cnt_scr[...] + jnp.sum(pick, axis=0, keepdims=True)
    cnt_scr[...] = cnt_new
    cnt_ref[...] = cnt_new
    info_ref[...] = jnp.where(lane == 0, e1, jnp.where(lane == 1, e2, jnp.where(lane == 2, w1, jnp.where(
        lane == 3, w2, jnp.where(lane == 4, r1, jnp.where(lane == 5, r2, 0.0))))))


def _mixout(h, a, a_col, r, r_col, b, g_head, w_o, g_ffn, wr_hi, wr_lo, b_r, counts0, *, hd, act,
            row0, n_rows, prev=None):
    n, d = h.shape
    precise = prev is not None
    tm = _tile(n_rows, 512, SUBLANES)
    assert row0 % tm == 0
    blk0 = row0 // tm
    wa = w_o.shape[0] if b is None else w_o.shape[0] - B_W
    has_b = b is not None
    row = lambda i: (blk0 + i, 0)
    const = lambda i: (0, 0)
    in_specs = [pl.BlockSpec((tm, d), row),
                pl.BlockSpec((tm, wa), lambda i: (blk0 + i, a_col)),
                pl.BlockSpec((tm, wa), lambda i: (blk0 + i, r_col))]
    args = [h, a, r]
    if has_b:
        in_specs.append(pl.BlockSpec((tm, B_W), row))
        args.append(b)
    in_specs += [pl.BlockSpec((1, hd), const), pl.BlockSpec(w_o.shape, const), pl.BlockSpec((1, d), const),
                 pl.BlockSpec((d, LANES), const), pl.BlockSpec((d, LANES), const), pl.BlockSpec((1, LANES), const),
                 pl.BlockSpec((1, LANES), const)]
    args += [g_head.reshape(1, hd), w_o if precise else w_o.astype(BF16), g_ffn.reshape(1, d), wr_hi, wr_lo, b_r,
             counts0]
    io_alias = {}
    if precise:
        for k, arr in enumerate(prev):
            in_specs.append(pl.BlockSpec(memory_space=pl.ANY))
            args.append(arr)
            io_alias[len(args) - 1] = k
    return pl.pallas_call(
        functools.partial(_mixout_kernel, hd=hd, act=act, has_b=has_b, precise=precise, n_prev=len(io_alias)),
        grid=(n_rows // tm,),
        in_specs=in_specs,
        out_specs=[pl.BlockSpec((tm, d), row), pl.BlockSpec((tm, d // 2), row), pl.BlockSpec((tm, LANES), row),
                   pl.BlockSpec((1, LANES), const)],
        out_shape=[jax.ShapeDtypeStruct((n, d), F32), jax.ShapeDtypeStruct((n, d // 2), jnp.int32),
                   jax.ShapeDtypeStruct((n, LANES), F32), jax.ShapeDtypeStruct((1, LANES), F32)],
        scratch_shapes=[pltpu.VMEM((1, LANES), F32)],
        input_output_aliases=io_alias,
        compiler_params=_params(("arbitrary",)),
        name="mixout_" + act + ("_precise" if precise else ""),
    )(*args)


def _sc_gather(table, idx):
    r = idx.shape[0]
    w = table.shape[1]
    n_workers = SC_CORES * SC_SUBCORES
    per_worker = r // n_workers
    step = SC_GATHER_ROWS * SC_GATHER_BUFS
    assert r % n_workers == 0 and per_worker % step == 0, (r, n_workers, step)
    mesh = plsc.VectorSubcoreMesh(core_axis_name="c", subcore_axis_name="s")

    @functools.partial(
        pl.kernel, mesh=mesh,
        out_type=jax.ShapeDtypeStruct((r, w), table.dtype),
        scratch_types=[pltpu.VMEM((SC_GATHER_BUFS, SC_GATHER_ROWS), jnp.int32),
                       pltpu.VMEM((SC_GATHER_BUFS, SC_GATHER_ROWS, w), table.dtype),
                       pltpu.SemaphoreType.DMA((SC_GATHER_BUFS,)),
                       pltpu.SemaphoreType.DMA((SC_GATHER_BUFS,))],
    )
    def gather(table_hbm, idx_hbm, out_hbm, idx_v, rows_v, gather_sem, store_sem):
        worker = lax.axis_index("s") * SC_CORES + lax.axis_index("c")
        base = worker * per_worker

        @pl.loop(0, per_worker // step)
        def _(j):
            off = pl.multiple_of(base + j * step, step)
            rows = lambda b: pl.ds(off + b * SC_GATHER_ROWS, SC_GATHER_ROWS)
            gathers, stores = [], []
            for b in range(SC_GATHER_BUFS):
                pltpu.sync_copy(idx_hbm.at[rows(b)], idx_v.at[b])
                gathers.append(pltpu.async_copy(table_hbm.at[idx_v.at[b]], rows_v.at[b], gather_sem.at[b]))
            for b in range(SC_GATHER_BUFS):
                gathers[b].wait()
                stores.append(pltpu.async_copy(rows_v.at[b], out_hbm.at[rows(b)], store_sem.at[b]))
            for b in range(SC_GATHER_BUFS):
                stores[b].wait()

    return gather(table, idx)


def _sc_scatter_pairs(rows, slots, n_slots):
    n, w = rows.shape
    n_workers = SC_CORES * SC_SUBCORES
    per_worker = n // n_workers
    step = SC_SCATTER_ROWS * SC_SCATTER_BUFS
    assert n % n_workers == 0 and per_worker % step == 0, (n, n_workers, step)
    mesh = plsc.VectorSubcoreMesh(core_axis_name="c", subcore_axis_name="s")

    @functools.partial(
        pl.kernel, mesh=mesh,
        out_type=jax.ShapeDtypeStruct((n_slots, w), rows.dtype),
        scratch_types=[pltpu.VMEM((2 * SC_SCATTER_BUFS, SC_SCATTER_ROWS), jnp.int32),
                       pltpu.VMEM((SC_SCATTER_BUFS, SC_SCATTER_ROWS, w), rows.dtype),
                       pltpu.SemaphoreType.DMA((SC_SCATTER_BUFS,)),
                       pltpu.SemaphoreType.DMA((SC_SCATTER_BUFS,))],
    )
    def scatter(rows_hbm, slots_hbm, out_hbm, idx_v, rows_v, load_sem, store_sem):
        worker = lax.axis_index("s") * SC_CORES + lax.axis_index("c")
        base = worker * per_worker

        @pl.loop(0, per_worker // step)
        def _(j):
            off = pl.multiple_of(base + j * step, SC_SCATTER_ROWS)
            loads, stores = [], []
            for b in range(SC_SCATTER_BUFS):
                r0 = off + b * SC_SCATTER_ROWS
                pltpu.sync_copy(slots_hbm.at[pl.ds(r0, SC_SCATTER_ROWS)], idx_v.at[2 * b])
                pltpu.sync_copy(slots_hbm.at[pl.ds(n + r0, SC_SCATTER_ROWS)], idx_v.at[2 * b + 1])
                loads.append(pltpu.async_copy(rows_hbm.at[pl.ds(r0, SC_SCATTER_ROWS)], rows_v.at[b], load_sem.at[b]))
            for b in range(SC_SCATTER_BUFS):
                loads[b].wait()
                for k in range(2):
                    stores.append(pltpu.async_copy(rows_v.at[b], out_hbm.at[idx_v.at[2 * b + k]], store_sem.at[b]))
            for copy in stores:
                copy.wait()

    return scatter(rows, slots)


def _expert_kernel(te_ref, nu_ref, x_ref, wg_ref, wu_ref, wd_ref, y_ref, wgb, wub, wdb):
    i = pl.program_id(0)
    live = i < nu_ref[0]
    new_expert = (i == 0) | (te_ref[i] != te_ref[jnp.maximum(i - 1, 0)])

    @pl.when(live & new_expert)
    def _():
        wgb[...] = wg_ref[0].astype(BF16)
        wub[...] = wu_ref[0].astype(BF16)
        wdb[...] = wd_ref[0].astype(BF16)

    @pl.when(live)
    def _():
        x = _unpack_bf16_pairs(x_ref[...]).astype(BF16)
        g = _dot(x, wgb[...])
        u = _dot(x, wub[...])
        y_ref[...] = _pack_bf16_pairs(_dot((g * _sigmoid(g) * u).astype(BF16), wdb[...]))

    @pl.when(jnp.logical_not(live))
    def _():
        y_ref[...] = jnp.zeros_like(y_ref)


def _experts(xs, tile_expert, n_used, w_gate, w_up, w_down, *, tm):
    n_slots, dp = xs.shape
    d, f = w_gate.shape[-2:]
    grid_spec = pltpu.PrefetchScalarGridSpec(
        num_scalar_prefetch=2,
        grid=(n_slots // tm,),
        in_specs=[pl.BlockSpec((tm, dp), lambda i, te, nu: (i, 0)),
                  pl.BlockSpec((1, d, f), lambda i, te, nu: (te[i], 0, 0)),
                  pl.BlockSpec((1, d, f), lambda i, te, nu: (te[i], 0, 0)),
                  pl.BlockSpec((1, f, d), lambda i, te, nu: (te[i], 0, 0))],
        out_specs=pl.BlockSpec((tm, dp), lambda i, te, nu: (i, 0)),
        scratch_shapes=[pltpu.VMEM((d, f), BF16), pltpu.VMEM((d, f), BF16), pltpu.VMEM((f, d), BF16)],
    )
    return pl.pallas_call(
        _expert_kernel,
        grid_spec=grid_spec,
        out_shape=jax.ShapeDtypeStruct((n_slots, dp), jnp.int32),
        compiler_params=_params(("arbitrary",)),
        name="moe_experts",
    )(tile_expert, n_used, xs, w_gate, w_up, w_down)


def _combine_kernel(h_ref, info_ref, g_ref, y0_ref, y1_ref, o_ref, *, final_norm):
    h2 = h_ref[...] + (info_ref[:, 2:3] * _unpack_bf16_pairs(y0_ref[...])
                       + info_ref[:, 3:4] * _unpack_bf16_pairs(y1_ref[...]))
    o_ref[...] = _rms(h2, g_ref[...]) if final_norm else h2


def _combine(h, yg, info, g, *, final_norm):
    n, d = h.shape
    tm = _tile(n, 512, SUBLANES)
    nt = n // tm
    return pl.pallas_call(
        functools.partial(_combine_kernel, final_norm=final_norm),
        grid=(nt,),
        in_specs=[pl.BlockSpec((tm, d), lambda i: (i, 0)),
                  pl.BlockSpec((tm, LANES), lambda i: (i, 0)),
                  pl.BlockSpec((1, d), lambda i: (0, 0)),
                  pl.BlockSpec((tm, d // 2), lambda i: (i, 0)),
                  pl.BlockSpec((tm, d // 2), lambda i: (nt + i, 0))],
        out_specs=pl.BlockSpec((tm, d), lambda i: (i, 0)),
        out_shape=jax.ShapeDtypeStruct((n, d), F32),
        compiler_params=_params(("parallel",)),
        name="moe_combine",
    )(h, info, g.reshape(1, d), yg, yg)


def _final_kernel(h_ref, info_ref, g_ref, y0_ref, y1_ref, yp_ref, ys_ref, *, tiles_per_batch, lead_tiles, prompt_tiles):
    i = pl.program_id(0)
    h2 = h_ref[...] + (info_ref[:, 2:3] * _unpack_bf16_pairs(y0_ref[...])
                       + info_ref[:, 3:4] * _unpack_bf16_pairs(y1_ref[...]))
    out = _rms(h2, g_ref[...])

    @pl.when((i < prompt_tiles) & (i % tiles_per_batch >= lead_tiles))
    def _():
        yp_ref[...] = out

    @pl.when(i >= prompt_tiles)
    def _():
        ys_ref[...] = out


def _final_combine(h, yg, info, g, *, nb, t, n_sample):
    n, d = h.shape
    tm = _tile(math.gcd(LEAD, n_sample), LANES, SUBLANES)
    nt = n // tm
    tpb, lead_tiles = t // tm, LEAD // tm
    prompt_tiles = nb * tpb
    keep = tpb - lead_tiles

    def prompt_block(i):
        ip = jnp.minimum(i, prompt_tiles - 1)
        return (ip // tpb) * keep + jnp.maximum(ip % tpb - lead_tiles, 0), 0

    return pl.pallas_call(
        functools.partial(_final_kernel, tiles_per_batch=tpb, lead_tiles=lead_tiles, prompt_tiles=prompt_tiles),
        grid=(nt,),
        in_specs=[pl.BlockSpec((tm, d), lambda i: (i, 0)),
                  pl.BlockSpec((tm, LANES), lambda i: (i, 0)),
                  pl.BlockSpec((1, d), lambda i: (0, 0)),
                  pl.BlockSpec((tm, d // 2), lambda i: (i, 0)),
                  pl.BlockSpec((tm, d // 2), lambda i: (nt + i, 0))],
        out_specs=[pl.BlockSpec((tm, d), prompt_block),
                   pl.BlockSpec((tm, d), lambda i: (jnp.maximum(i - prompt_tiles, 0), 0))],
        out_shape=[jax.ShapeDtypeStruct((nb * keep * tm, d), F32), jax.ShapeDtypeStruct((n_sample, d), F32)],
        compiler_params=_params(("arbitrary",)),
        name="moe_combine_final",
    )(h, info, g.reshape(1, d), yg, yg)


def _moe(h1, xn, info, counts_f, w_gate, w_up, w_down, expert_base, g_next, *, final=None):
    n, d = h1.shape
    n_e = N_GROUPS * N_EXP
    tm = _tile(2 * n, MOE_TILE, SUBLANES)
    n_tiles = (2 * n) // tm + n_e
    n_slots = n_tiles * tm
    experts = jnp.arange(n_e, dtype=jnp.int32)
    counts = counts_f[0, :n_e].astype(jnp.int32)
    padded = ((counts + tm - 1) // tm) * tm
    pad_ends = jnp.cumsum(padded)
    pad_starts = pad_ends - padded
    n_used = (pad_ends[-1] // tm).astype(jnp.int32)
    eid = info[:, 0:2].astype(jnp.int32)
    rank = info[:, 4:6].astype(jnp.int32)
    slot_of_pick = jnp.sum(jnp.where(eid[..., None] == experts, pad_starts, 0), axis=-1) + rank
    slots = slot_of_pick.T.reshape(-1)
    tile_starts = jnp.arange(n_tiles, dtype=jnp.int32) * tm
    tile_expert = expert_base + jnp.minimum(jnp.sum(tile_starts[:, None] >= pad_ends[None, :], axis=1), n_e - 1)

    xs = _sc_scatter_pairs(xn, slots, n_slots)
    ys = _experts(xs, tile_expert.astype(jnp.int32), n_used.reshape(1), w_gate, w_up, w_down, tm=tm)
    yg = _sc_gather(ys, slots)
    if final is not None:
        return _final_combine(h1, yg, info, g_next, **final)
    return _combine(h1, yg, info, g_next, final_norm=False)


def _router_weights(w_rg, b_rg, w_re, b_re):
    d = w_rg.shape[0]
    pad = LANES - N_GROUPS - N_GROUPS * N_EXP
    w = jnp.concatenate([w_rg, w_re, jnp.zeros((d, pad), F32)], axis=1)
    b = jnp.concatenate([b_rg, b_re, jnp.zeros((pad,), F32)]).reshape(1, LANES)
    hi = w.astype(BF16)
    lo = (w - hi.astype(F32)).astype(BF16)
    return hi, lo, b


def _rows_to_lanes(x, nb, t, nch):
    return x.reshape(nb, t, nch).transpose(0, 2, 1).reshape(nb * nch, t)


def _even_layer(h, dims, g_mix, w_in, w_g2, b_g, b_f, g_a, w_o, state_gla, ck, cv, clf):
    nb, t, db, ds, npr, n = dims
    d = h.shape[1]
    qa, ka, va, ra, ga, qb, kb, vb, fb = jnp.split(
        w_in, [A_QK, 2 * A_QK, 2 * A_QK + A_V, 2 * A_QK + 2 * A_V, 2 * A_QK + 2 * A_V + A_RANK,
               2 * A_QK + 2 * A_V + A_RANK + B_W, 2 * A_QK + 2 * A_V + A_RANK + 2 * B_W,
               2 * A_QK + 2 * A_V + A_RANK + 3 * B_W], axis=1)
    w_packed = jnp.concatenate(
        [qa, ka, va, ra, qb, kb, vb, ga, fb, jnp.zeros((d, LANES - A_RANK - B_HEADS), F32)], axis=1)
    q_off = 2 * A_QK + 2 * A_V
    colscale = jnp.ones((1, MAIN_W), F32).at[:, q_off:q_off + B_W].set(B_DH ** -0.5 * LOG2E)
    p, pb, k_rows, v_rows = _proj(h, g_mix, w_packed, n_rows=npr, colscale=colscale, kv_col=q_off + B_W)
    p = _proj_precise(h, g_mix, w_packed, p, row0=npr)

    nh = B_HEADS
    fcol = MAIN_W + A_RANK
    zeros_col = lambda r: jnp.zeros((r, 1), F32)
    bias_row = lambda lanes: jnp.tile(b_f, lanes // nh).reshape(1, lanes)
    fb_p = p[:npr, fcol:fcol + nh].reshape(nb, t * nh)
    logf_p, f_p = _gate_scan(fb_p, bias_row(t * nh), zeros_col(nb), mode="fox", act_start=0,
                             valid_start=N_PAD * nh, valid_end=t * nh, seg=None, stride=nh)
    past = ck.shape[1]
    n_c = past * nh
    x_s = _pad_lanes(jnp.concatenate([clf.reshape(db, n_c), p[npr:, fcol:fcol + nh].reshape(db, ds * nh)], axis=1))
    logf_s, f_s = _gate_scan(x_s, bias_row(x_s.shape[1]), zeros_col(db), mode="fox", act_start=n_c,
                             valid_start=0, valid_end=n_c + ds * nh, seg=None, stride=nh)

    nc = t // CHUNK
    oa, s_p = _gla(p, jnp.zeros((nb, A_HEADS, A_DK, A_DV), F32), w_g2, b_g,
                   nb=nb, nc=nc, L=CHUNK, row_block0=0, lead_pad=N_PAD, n_total=n)
    oa, s_s = _gla(p, state_gla, w_g2, b_g, nb=db, nc=1, L=ds, row_block0=npr // ds, lead_pad=0,
                   n_total=n, prev_out=oa, precise=True)

    fk = (f_p * LOG2E).reshape(nb, t, nh).transpose(0, 2, 1).reshape(nb, nh, 1, t)
    ob = _flash(pb, fk, nb=nb, t=t, n_total=n)
    f_cache = (f_s[:, :n_c] * LOG2E).reshape(db, 1, n_c)
    f_new = (f_s[:, n_c:n_c + ds * nh] * LOG2E).reshape(db, ds, nh).transpose(0, 2, 1).reshape(db, 1, nh * ds)
    ob = _fox_sample(p, ck.reshape(db, n_c, B_DH), cv.reshape(db, n_c, B_DH), f_cache, f_new, ob,
                     nb=db, ds=ds, row_block0=npr // ds)

    kcol = q_off + B_W
    states = dict(
        s_p=s_p, s_s=s_s,
        k_p=k_rows.reshape(nb, t, B_HEADS, B_DH)[:, N_PAD:],
        v_p=v_rows.reshape(nb, t, B_HEADS, B_DH)[:, N_PAD:],
        f_p=logf_p.reshape(nb, t, nh)[:, N_PAD:],
        k_s=p[npr:, kcol:kcol + B_W].reshape(db, ds, B_HEADS, B_DH),
        v_s=p[npr:, kcol + B_W:kcol + 2 * B_W].reshape(db, ds, B_HEADS, B_DH),
        f_s=logf_s[:, n_c:n_c + ds * nh].reshape(db, ds, nh))
    return (oa, 0, p, (2 * A_QK + A_V) // A_V, ob, g_a, w_o), states


def _chunk_rows(x, nb, nch, nc, L):
    x = x[:, :nc * L].reshape(nb, nch, nc, L)
    return x.transpose(0, 2, 1, 3).reshape(nb, nc, 1, nch * L), x.transpose(0, 2, 3, 1)


def _odd_layer(h, dims, g_mix, w_in, b_gate, g_c, w_o, c0, n0, m0):
    nb, t, db, ds, npr, n = dims
    d = h.shape[1]
    w_packed = jnp.concatenate(
        [w_in, jnp.zeros((d, LANES - 2 * C_HEADS), F32)], axis=1)
    (p,) = _proj(h, g_mix, w_packed, n_rows=npr)
    p = _proj_precise(h, g_mix, w_packed, p, row0=npr)

    ng = 2 * C_HEADS
    isf = (jnp.arange(ng) >= C_HEADS).astype(F32)
    nc = t // CHUNK

    def gates(rows, nbatch, tt, valid_start, seg):
        x = _pad_lanes(_rows_to_lanes(rows, nbatch, tt, ng))
        val, cum = _gate_scan(x, jnp.tile(b_gate, nbatch).reshape(-1, 1), jnp.tile(isf, nbatch).reshape(-1, 1),
                              mode="mlstm", act_start=0, valid_start=valid_start, valid_end=tt, seg=seg)
        return val.reshape(nbatch, ng, -1), cum.reshape(nbatch, ng, -1)

    val_p, cum_p = gates(p[:npr, MAIN_W:MAIN_W + ng], nb, t, N_PAD, CHUNK)
    val_s, cum_s = gates(p[npr:, MAIN_W:MAIN_W + ng], db, ds, 0, ds)

    def chunked(val, cum, nbatch, ncs, L):
        li_row, li_col = _chunk_rows(val[:, :C_HEADS].reshape(nbatch * C_HEADS, -1), nbatch, C_HEADS, ncs, L)
        b_row, b_col = _chunk_rows(cum[:, C_HEADS:].reshape(nbatch * C_HEADS, -1), nbatch, C_HEADS, ncs, L)
        return li_row, b_row, li_col, b_col

    zc = jnp.zeros((nb, C_HEADS, C_DV, C_DQK), F32)
    zn = jnp.zeros((nb, C_HEADS, C_DQK), F32)
    zm = jnp.zeros((nb, C_HEADS, 1), F32)
    hm, c_p, n_p, m_p = _mlstm(p, *chunked(val_p, cum_p, nb, nc, CHUNK), zc, zn, zm,
                               nb=nb, nc=nc, L=CHUNK, row_block0=0, n_total=n)
    hm, c_s, n_s, m_s = _mlstm(p, *chunked(val_s, cum_s, db, 1, ds), c0, n0, m0.reshape(db, C_HEADS, 1),
                               nb=db, nc=1, L=ds, row_block0=npr // ds, n_total=n, prev_out=hm)
    states = dict(c_p=c_p, n_p=n_p, m_p=m_p.reshape(nb, C_HEADS), c_s=c_s, n_s=n_s, m_s=m_s.reshape(db, C_HEADS))
    return (hm, 0, p, (2 * C_QK + C_V) // C_V, None, g_c, w_o), states


def kernel(x_prompt, x_sample, state_gla, cache_fox_k, cache_fox_v, cache_fox_logf, state_mlstm_c, state_mlstm_n, state_mlstm_m, meta_tokens, norm_mix, norm_ffn, norm_final, w_in_even, w_gla_gate2, b_gla_gate, b_fox_f, g_gla_out, w_out_even, w_in_odd, b_mlstm_gate, g_mlstm_out, w_out_odd, w_router_group, b_router_group, w_router_expert, b_router_expert, w_exp_gate, w_exp_up, w_exp_down):
    nb, seq, d = x_prompt.shape
    db, ds, _ = x_sample.shape
    t = LEAD + seq
    npr, nsm = nb * t, db * ds
    n = npr + nsm
    dims = (nb, t, db, ds, npr, n)
    depth = norm_mix.shape[0]
    n_e = N_GROUPS * N_EXP
    f = w_exp_gate.shape[-1]

    pad_rows = jnp.zeros((N_PAD, d), F32)
    h = jnp.concatenate([piece for b in range(nb) for piece in (pad_rows, meta_tokens, x_prompt[b])]
                        + [x_sample.reshape(nsm, d)], axis=0)
    wg_all = w_exp_gate.reshape(depth * n_e, d, f)
    wu_all = w_exp_up.reshape(depth * n_e, d, f)
    wd_all = w_exp_down.reshape(depth * n_e, f, d)

    even, odd = [], []
    for l in range(depth):
        if l % 2 == 0:
            e = l // 2
            mix, st = _even_layer(h, dims, norm_mix[l], w_in_even[e], w_gla_gate2[e], b_gla_gate[e], b_fox_f[e],
                                  g_gla_out[e], w_out_even[e], state_gla[e], cache_fox_k[e], cache_fox_v[e],
                                  cache_fox_logf[e])
            even.append(st)
            hd, act = A_DV, "silu"
        else:
            o = l // 2
            mix, st = _odd_layer(h, dims, norm_mix[l], w_in_odd[o], b_mlstm_gate[o], g_mlstm_out[o], w_out_odd[o],
                                 state_mlstm_c[o], state_mlstm_n[o], state_mlstm_m[o])
            odd.append(st)
            hd, act = C_DV, "sigmoid"
        a, a_col, r, r_col, b, g_head, w_o = mix
        wr_hi, wr_lo, b_r = _router_weights(w_router_group[l], b_router_group[l], w_router_expert[l],
                                            b_router_expert[l])
        mix_args = (h, a, a_col, r, r_col, b, g_head, w_o, norm_ffn[l], wr_hi, wr_lo, b_r)
        h1, xn, info, counts = _mixout(*mix_args, jnp.zeros((1, LANES), F32), hd=hd, act=act, row0=0, n_rows=npr)
        h1, xn, info, counts = _mixout(*mix_args, counts, hd=hd, act=act, row0=npr, n_rows=nsm,
                                       prev=(h1, xn, info))
        last = l == depth - 1
        h = _moe(h1, xn, info, counts, wg_all, wu_all, wd_all, l * n_e, norm_final if last else norm_ffn[l],
                 final=dict(nb=nb, t=t, n_sample=nsm) if last else None)

    y_prompt = h[0].reshape(nb, seq, d)
    y_sample = h[1].reshape(db, ds, d)
    stack = lambda sts, key: jnp.stack([s[key] for s in sts])
    return (y_prompt, y_sample,
            stack(even, "s_p"), stack(even, "k_p"), stack(even, "v_p"), stack(even, "f_p"),
            stack(odd, "c_p"), stack(odd, "n_p"), stack(odd, "m_p"),
            stack(even, "s_s"), stack(even, "k_s"), stack(even, "v_s"), stack(even, "f_s"),
            stack(odd, "c_s"), stack(odd, "n_s"), stack(odd, "m_s"))
```

```python
import functools
import math

import jax
import jax.numpy as jnp
from jax import lax
from jax.experimental import pallas as pl
from jax.experimental.pallas import tpu as pltpu
from jax.experimental.pallas import tpu_sc as plsc

F32 = jnp.float32
BF16 = jnp.bfloat16

CHUNK = 64
N_META = 16
LEAD = 128
N_PAD = LEAD - N_META
A_HEADS, A_DK, A_DV, A_RANK = 4, 64, 128, 16
A_GATE_NORM = 16.0
B_HEADS, B_DH = 4, 128
C_HEADS, C_DQK, C_DV = 4, 128, 256
GATE_CAP = 15.0
N_GROUPS, N_EXP = 4, 8
EPS = 1e-6
NEG = -1e30
LOG2E = 1.4426950408889634
A_QK = A_HEADS * A_DK
A_V = A_HEADS * A_DV
B_W = B_HEADS * B_DH
C_QK = C_HEADS * C_DQK
C_V = C_HEADS * C_DV

LANES = 128
SUBLANES = 8
VMEM_LIMIT_BYTES = 56 * 1024 * 1024
GLA_SUB = 16
SC_CORES, SC_SUBCORES = 2, 16
SC_GATHER_ROWS = 16
SC_GATHER_BUFS = 4
MOE_TILE = 1024
GLA_CHUNKS_PER_STEP = 5
MLSTM_CHUNKS_PER_STEP = 2
FINAL_SUBTILES = 4
SC_SCATTER_ROWS = 16
SC_SCATTER_BUFS = 3
FLASH_HEADS = 4
MAIN_W = 3072
PROJ_W = MAIN_W + LANES

_NT = (((1,), (1,)), ((), ()))
_TN = (((0,), (0,)), ((), ()))
_NN = (((1,), (0,)), ((), ()))


def _params(sem):
    return pltpu.CompilerParams(dimension_semantics=sem, vmem_limit_bytes=VMEM_LIMIT_BYTES)


def _tile(n, pref, mult):
    t = (min(pref, n) // mult) * mult
    while t > mult and n % t:
        t -= mult
    assert t >= mult and n % t == 0, (n, pref, mult)
    return t


def _dot(a, b, dims=_NN):
    return lax.dot_general(a, b, dims, preferred_element_type=F32)


def _split(x):
    hi = x.astype(BF16)
    lo = (x - hi.astype(F32)).astype(BF16)
    return hi, lo


def _dot3(a, b, dims=_NN):
    ah, al = _split(a)
    bh, bl = _split(b)
    return _dot(ah, bh, dims) + _dot(ah, bl, dims) + _dot(al, bh, dims)


def _log_sigmoid(x):
    return jnp.minimum(x, 0.0) - jnp.log1p(jnp.exp(-jnp.abs(x)))


def _sigmoid(x):
    return 1.0 / (1.0 + jnp.exp(-x))


def _rms(x, g):
    return x * lax.rsqrt(jnp.mean(x * x, axis=-1, keepdims=True) + EPS) * g


def _pack_bf16_pairs(x):
    w = x.shape[1] // 2
    hi = lax.bitcast_convert_type(x[:, :w].astype(BF16).astype(F32), jnp.int32)
    lo = lax.bitcast_convert_type(x[:, w:].astype(BF16).astype(F32), jnp.int32)
    return hi | lax.shift_right_logical(lo, 16)


def _unpack_bf16_pairs(p):
    hi = lax.bitcast_convert_type(p & jnp.int32(-65536), F32)
    lo = lax.bitcast_convert_type(lax.shift_left(p, 16), F32)
    return jnp.concatenate([hi, lo], axis=1)


def _cumsum_rows(x):
    n = x.shape[0]
    row = lax.broadcasted_iota(jnp.int32, x.shape, 0)
    s = 1
    while s < n:
        x = x + jnp.where(row >= s, pltpu.roll(x, s, axis=0), 0.0)
        s *= 2
    return x


def _proj_kernel(x_ref, g_ref, w_ref, *rest, col_chunk, kv_col):
    if kv_col is None:
        (o_ref,) = rest
    else:
        cs_ref, o_ref, ob_ref, *kv_refs = rest
    tm = x_ref.shape[0]
    xn = _rms(x_ref[...], g_ref[...]).astype(BF16)
    for c0 in range(0, PROJ_W, col_chunk):
        c1 = min(c0 + col_chunk, PROJ_W)
        y = _dot(xn, w_ref[:, c0:c1])
        o_ref[:, c0:c1] = y
        if kv_col is not None:
            if c0 < MAIN_W:
                m1 = min(c1, MAIN_W)
                ob_ref[:, c0:m1] = (y[:, :m1 - c0] * cs_ref[:, c0:m1]).astype(BF16)
            for g0 in range(c0, c1, LANES):
                rel = g0 - kv_col
                if 0 <= rel < 2 * B_W:
                    head = (rel % B_W) // B_DH
                    kv_refs[rel // B_W][pl.ds(head, tm, stride=B_HEADS), :] = y[:, g0 - c0:g0 - c0 + LANES]


def _proj_precise_kernel(x_ref, g_ref, w_ref, prev_ref, o_ref, *, col_chunk):
    del prev_ref
    xn = _rms(x_ref[...], g_ref[...])
    for c0 in range(0, PROJ_W, col_chunk):
        c1 = min(c0 + col_chunk, PROJ_W)
        o_ref[:, c0:c1] = _dot3(xn, w_ref[:, c0:c1])


def _proj(h, g, w_packed, *, n_rows, colscale=None, kv_col=None):
    n, d = h.shape
    tm = _tile(n_rows, 512, 16)
    in_specs = [pl.BlockSpec((tm, d), lambda i: (i, 0)),
                pl.BlockSpec((1, d), lambda i: (0, 0)),
                pl.BlockSpec((d, PROJ_W), lambda i: (0, 0))]
    args = [h, g.reshape(1, d), w_packed.astype(BF16)]
    out_specs = [pl.BlockSpec((tm, PROJ_W), lambda i: (i, 0))]
    out_shape = [jax.ShapeDtypeStruct((n, PROJ_W), F32)]
    if kv_col is not None:
        in_specs.append(pl.BlockSpec((1, MAIN_W), lambda i: (0, 0)))
        args.append(colscale)
        out_specs += [pl.BlockSpec((tm, MAIN_W), lambda i: (i, 0))] + [pl.BlockSpec((tm * B_HEADS, B_DH), lambda i: (i, 0))] * 2
        out_shape += ([jax.ShapeDtypeStruct((n_rows, MAIN_W), BF16)]
                      + [jax.ShapeDtypeStruct((n_rows * B_HEADS, B_DH), F32)] * 2)
    return pl.pallas_call(
        functools.partial(_proj_kernel, col_chunk=640, kv_col=kv_col),
        grid=(n_rows // tm,),
        in_specs=in_specs,
        out_specs=out_specs,
        out_shape=out_shape,
        compiler_params=_params(("parallel",)),
        name="proj",
    )(*args)


def _proj_precise(h, g, w_packed, prev, *, row0):
    n, d = h.shape
    tm = _tile(n - row0, 512, SUBLANES)
    assert row0 % tm == 0
    return pl.pallas_call(
        functools.partial(_proj_precise_kernel, col_chunk=640),
        grid=((n - row0) // tm,),
        in_specs=[pl.BlockSpec((tm, d), lambda i: (row0 // tm + i, 0)),
                  pl.BlockSpec((1, d), lambda i: (0, 0)),
                  pl.BlockSpec((d, PROJ_W), lambda i: (0, 0)),
                  pl.BlockSpec(memory_space=pl.ANY)],
        out_specs=pl.BlockSpec((tm, PROJ_W), lambda i: (row0 // tm + i, 0)),
        out_shape=jax.ShapeDtypeStruct((n, PROJ_W), F32),
        input_output_aliases={3: 0},
        compiler_params=_params(("parallel",)),
        name="proj_precise",
    )(h, g.reshape(1, d), w_packed, prev)


def _gate_scan_kernel(x_ref, bias_ref, isf_ref, val_ref, cum_ref, *, mode, act_start, valid_start, valid_end, seg, stride):
    x = x_ref[...]
    lane = lax.broadcasted_iota(jnp.int32, x.shape, 1)
    valid = (lane >= valid_start) & (lane < valid_end)
    if mode == "fox":
        val = jnp.where(lane >= act_start, _log_sigmoid(x + bias_ref[...]), x)
        val = jnp.where(valid, val, 0.0)
        add = val
    else:
        gate = GATE_CAP * jnp.tanh((x + bias_ref[...]) / GATE_CAP)
        isf = isf_ref[...] > 0.5
        val = jnp.where(isf, jnp.where(valid, _log_sigmoid(gate), 0.0),
                        jnp.where(valid, gate, -jnp.inf))
        add = jnp.where(isf, val, 0.0)
    val_ref[...] = val
    n = x.shape[1]
    pos = lane if seg is None else lane % seg
    limit = n if seg is None else seg
    s = stride
    while s < limit:
        add = add + jnp.where(pos >= s, pltpu.roll(add, s, axis=1), 0.0)
        s *= 2
    cum_ref[...] = add


def _gate_scan(x, bias, isf, *, mode, act_start, valid_start, valid_end, seg, stride=1):
    r, n = x.shape
    full = lambda shape: pl.BlockSpec(shape, lambda i: (0,) * len(shape))
    return pl.pallas_call(
        functools.partial(_gate_scan_kernel, mode=mode, act_start=act_start,
                          valid_start=valid_start, valid_end=valid_end, seg=seg, stride=stride),
        grid=(1,),
        in_specs=[full((r, n)), full(bias.shape), full((r, 1))],
        out_specs=[full((r, n)), full((r, n))],
        out_shape=[jax.ShapeDtypeStruct((r, n), F32)] * 2,
        compiler_params=_params(("arbitrary",)),
        name="gate_scan_" + mode,
    )(x, bias, isf)


def _pad_lanes(x):
    n = x.shape[-1]
    m = -(-n // LANES) * LANES
    return x if m == n else jnp.pad(x, ((0, 0), (0, m - n)))


def _gla_kernel(qk_ref, v_ref, sm_ref, s0_ref, wg2_ref, wg2t_ref, bgr_ref, bgc_ref, *rest,
                L, cps, sub, lead_pad, aliased, precise):
    if aliased:
        rest = rest[1:]
    o_ref, sout_ref, s_scr = rest
    step = pl.program_id(1)
    for j in range(cps):
        rows = pl.ds(j * L, L)
        _gla_chunk(qk_ref.at[rows], v_ref.at[rows], sm_ref.at[rows], s0_ref, wg2_ref, wg2t_ref, bgr_ref, bgc_ref,
                   o_ref.at[rows], sout_ref, s_scr, c=step * cps + j, first=(j == 0), last=(j == cps - 1),
                   L=L, sub=sub, lead_pad=lead_pad, precise=precise)


def _gla_chunk(qk_ref, v_ref, sm_ref, s0_ref, wg2_ref, wg2t_ref, bgr_ref, bgc_ref, o_ref, sout_ref, s_scr, *,
               c, first, last, L, sub, lead_pad, precise):
    nh, dk, dv = A_HEADS, A_DK, A_DV
    cast = (lambda x: x) if precise else (lambda x: x.astype(BF16))
    mm = _dot3 if precise else _dot

    if first:
        @pl.when(pl.program_id(1) == 0)
        def _():
            s_scr[...] = jnp.zeros_like(s_scr)
            for h in range(nh):
                s_scr[h * dk:(h + 1) * dk, h * dv:(h + 1) * dv] = s0_ref[0, h]

    qk = qk_ref[...]
    q = qk[:, :A_QK] * (A_DK ** -0.5)
    k = qk[:, A_QK:]
    v = v_ref[...]
    ga = sm_ref[:, :A_RANK]
    row = lax.broadcasted_iota(jnp.int32, (L, 1), 0)
    valid = (c * L + row) >= lead_pad
    z = _dot3(ga, wg2_ref[...]) + bgr_ref[...]
    loga = jnp.where(valid, _log_sigmoid(z) / A_GATE_NORM, 0.0)
    k = jnp.where(valid, k, 0.0)
    b = _cumsum_rows(loga)
    b_last = b[L - 1:L, :]
    lane_t = lax.broadcasted_iota(jnp.int32, (1, L), 1)
    zt = _dot3(wg2t_ref[...], ga, _NT) + bgc_ref[...]
    logat = jnp.where((c * L + lane_t) >= lead_pad, _log_sigmoid(zt) / A_GATE_NORM, 0.0)
    b_last_col = jnp.sum(logat, axis=1, keepdims=True)

    qhead = lax.broadcasted_iota(jnp.int32, (1, A_QK), 1) // dk
    vhead = lax.broadcasted_iota(jnp.int32, (1, A_V), 1) // dv
    vb = cast(v)
    zero_b = jnp.zeros((), vb.dtype)
    v_bd = jnp.concatenate([jnp.where(vhead == h, vb, zero_b) for h in range(nh)], axis=0)

    rows_all = lax.broadcasted_iota(jnp.int32, (L, 1), 0)
    a_rows = []
    for i in range(L // sub):
        r0 = i * sub
        ci = jnp.zeros((1, A_QK), F32) if i == 0 else b[r0 - 1:r0, :]
        qt = cast(q[r0:r0 + sub] * jnp.exp(b[r0:r0 + sub] - ci))
        kt = cast(jnp.where(rows_all < r0 + sub, k * jnp.exp(ci - b), 0.0))
        k_stack = jnp.concatenate([jnp.where(qhead == h, kt, zero_b) for h in range(nh)], axis=0)
        a_rows.append(mm(qt, k_stack, _NT))
    a = a_rows[0] if len(a_rows) == 1 else jnp.concatenate(a_rows, axis=0)
    t_idx = lax.broadcasted_iota(jnp.int32, (L, nh * L), 0)
    s_idx = lax.broadcasted_iota(jnp.int32, (L, nh * L), 1) % L
    a = jnp.where(s_idx <= t_idx, a, 0.0)
    o_intra = mm(cast(a), v_bd)

    s_full = s_scr[...]
    o_inter = mm(cast(q * jnp.exp(b)), cast(s_full))
    o_ref[...] = o_inter + o_intra

    k_hat = cast(k * jnp.exp(b_last - b))
    upd = mm(k_hat, vb, _TN)
    khead_col = lax.broadcasted_iota(jnp.int32, (A_QK, 1), 0) // dk
    s_new = jnp.exp(b_last_col) * s_full + jnp.where(khead_col == vhead, upd, 0.0)
    s_scr[...] = s_new

    if last:
        @pl.when(pl.program_id(1) == pl.num_programs(1) - 1)
        def _():
            for h in range(nh):
                sout_ref[0, h] = s_new[h * dk:(h + 1) * dk, h * dv:(h + 1) * dv]


def _chunks_per_step(nc, want):
    return next(k for k in (want, 2, 1) if k <= want and nc % k == 0)


def _gla(p, s0, wg2, bg, *, nb, nc, L, row_block0, lead_pad, n_total, prev_out=None, precise=False):
    aliased = prev_out is not None
    cps = _chunks_per_step(nc, GLA_CHUNKS_PER_STEP)
    assert row_block0 % cps == 0
    nc, L, row_block0, chunk = nc // cps, L * cps, row_block0 // cps, L
    rb = lambda b, c: row_block0 + b * nc + c
    in_specs = [pl.BlockSpec((L, 2 * A_QK), lambda b, c: (rb(b, c), 0)),
                pl.BlockSpec((L, A_V), lambda b, c: (rb(b, c), 2 * A_QK // A_V)),
                pl.BlockSpec((L, LANES), lambda b, c: (rb(b, c), MAIN_W // LANES)),
                pl.BlockSpec((1, A_HEADS, A_DK, A_DV), lambda b, c: (b, 0, 0, 0)),
                pl.BlockSpec((A_RANK, A_QK), lambda b, c: (0, 0)),
                pl.BlockSpec((A_QK, A_RANK), lambda b, c: (0, 0)),
                pl.BlockSpec((1, A_QK), lambda b, c: (0, 0)),
                pl.BlockSpec((A_QK, 1), lambda b, c: (0, 0))]
    args = [p, p, p, s0, wg2, wg2.T, bg.reshape(1, A_QK), bg.reshape(A_QK, 1)]
    io_alias = {}
    if aliased:
        in_specs.append(pl.BlockSpec(memory_space=pl.ANY))
        args.append(prev_out)
        io_alias = {len(args) - 1: 0}
    return pl.pallas_call(
        functools.partial(_gla_kernel, L=chunk, cps=cps, sub=min(GLA_SUB, chunk), lead_pad=lead_pad,
                          aliased=aliased, precise=precise),
        grid=(nb, nc),
        in_specs=in_specs,
        out_specs=[pl.BlockSpec((L, A_V), lambda b, c: (rb(b, c), 0)),
                   pl.BlockSpec((1, A_HEADS, A_DK, A_DV), lambda b, c: (b, 0, 0, 0))],
        out_shape=[jax.ShapeDtypeStruct((n_total, A_V), F32),
                   jax.ShapeDtypeStruct((nb, A_HEADS, A_DK, A_DV), F32)],
        scratch_shapes=[pltpu.VMEM((A_QK, A_V), F32)],
        input_output_aliases=io_alias,
        compiler_params=_params(("parallel", "arbitrary")),
        name="gla_L%d" % chunk,
    )(*args)


def _flash_kernel(qi_ref, kj_ref, q_ref, k_ref, v_ref, fk_ref, o_ref, m_scr, l_scr, acc_scr, *, blk, lead_pad):
    step = pl.program_id(2)
    i = qi_ref[step]
    j = kj_ref[step]

    @pl.when(j == 0)
    def _():
        m_scr[...] = jnp.full_like(m_scr, -jnp.inf)
        l_scr[...] = jnp.zeros_like(l_scr)
        acc_scr[...] = jnp.zeros_like(acc_scr)

    def update(masked):
        for g in range(FLASH_HEADS):
            sl = slice(g * B_DH, (g + 1) * B_DH)
            s = _dot(q_ref[:, sl], k_ref[:, sl], _NT) - fk_ref[0, g]
            if masked:
                qpos = i * blk + lax.broadcasted_iota(jnp.int32, (blk, blk), 0)
                kpos = j * blk + lax.broadcasted_iota(jnp.int32, (blk, blk), 1)
                s = jnp.where((kpos <= qpos) & (kpos >= lead_pad), s, NEG)
            m_prev = m_scr[g]
            m_new = jnp.maximum(m_prev, jnp.max(s, axis=1, keepdims=True))
            alpha = jnp.exp2(m_prev - m_new)
            p = jnp.exp2(s - m_new)
            l_scr[g] = alpha * l_scr[g] + jnp.sum(p, axis=1, keepdims=True)
            acc_scr[g] = alpha * acc_scr[g] + _dot(p.astype(BF16), v_ref[:, sl])
            m_scr[g] = m_new

    edge = (j == i) | (j == 0)
    pl.when(edge)(functools.partial(update, True))
    pl.when(jnp.logical_not(edge))(functools.partial(update, False))

    @pl.when(j == i)
    def _():
        for g in range(FLASH_HEADS):
            o_ref[:, g * B_DH:(g + 1) * B_DH] = acc_scr[g] / l_scr[g]


def _flash(pb, fk, *, nb, t, n_total):
    blk = _tile(t, 640, LANES)
    nq = t // blk
    hg = FLASH_HEADS
    w = hg * B_DH
    pairs = [(i, j) for i in range(nq) for j in range(i + 1)]
    qi = jnp.asarray([p[0] for p in pairs], jnp.int32)
    kj = jnp.asarray([p[1] for p in pairs], jnp.int32)
    qc, kc, vc = (A_QK * 2 + A_V * 2) // w, (A_QK * 2 + A_V * 2 + B_W) // w, (A_QK * 2 + A_V * 2 + 2 * B_W) // w
    grid_spec = pltpu.PrefetchScalarGridSpec(
        num_scalar_prefetch=2,
        grid=(nb, B_HEADS // hg, len(pairs)),
        in_specs=[pl.BlockSpec((blk, w), lambda b, h, s, qi, kj: (b * nq + qi[s], qc + h)),
                  pl.BlockSpec((blk, w), lambda b, h, s, qi, kj: (b * nq + kj[s], kc + h)),
                  pl.BlockSpec((blk, w), lambda b, h, s, qi, kj: (b * nq + kj[s], vc + h)),
                  pl.BlockSpec((1, hg, 1, blk), lambda b, h, s, qi, kj: (b, h, 0, kj[s]))],
        out_specs=pl.BlockSpec((blk, w), lambda b, h, s, qi, kj: (b * nq + qi[s], h)),
        scratch_shapes=[pltpu.VMEM((hg, blk, 1), F32), pltpu.VMEM((hg, blk, 1), F32),
                        pltpu.VMEM((hg, blk, B_DH), F32)],
    )
    return pl.pallas_call(
        functools.partial(_flash_kernel, blk=blk, lead_pad=N_PAD),
        grid_spec=grid_spec,
        out_shape=jax.ShapeDtypeStruct((n_total, B_W), F32),
        compiler_params=_params(("parallel", "parallel", "arbitrary")),
        name="fox_flash",
    )(qi, kj, pb, pb, pb, fk)


def _fox_sample_kernel(q_ref, kn_ref, vn_ref, kc_ref, vc_ref, fc_ref, fn_ref, prev_ref, o_ref, *, ds):
    del prev_ref
    nh = B_HEADS
    stack = lambda ref: jnp.concatenate([ref[:, h * B_DH:(h + 1) * B_DH] for h in range(nh)], axis=0)
    q = stack(q_ref) * (B_DH ** -0.5 * LOG2E)
    rows = nh * ds
    n_c = kc_ref.shape[1]
    qh_c = lax.broadcasted_iota(jnp.int32, (rows, n_c), 0) // ds
    kh_c = lax.broadcasted_iota(jnp.int32, (rows, n_c), 1) % nh
    s_c = _dot3(q, kc_ref[0], _NT) - fc_ref[0]
    s_c = jnp.where(qh_c == kh_c, s_c, NEG)
    r_i = lax.broadcasted_iota(jnp.int32, (rows, rows), 0)
    c_i = lax.broadcasted_iota(jnp.int32, (rows, rows), 1)
    s_n = _dot3(q, stack(kn_ref), _NT) - fn_ref[0]
    s_n = jnp.where((r_i // ds == c_i // ds) & (c_i % ds <= r_i % ds), s_n, NEG)
    m = jnp.maximum(jnp.max(s_c, axis=1, keepdims=True), jnp.max(s_n, axis=1, keepdims=True))
    p_c = jnp.exp2(s_c - m)
    p_n = jnp.exp2(s_n - m)
    l = jnp.sum(p_c, axis=1, keepdims=True) + jnp.sum(p_n, axis=1, keepdims=True)
    o = (_dot3(p_c, vc_ref[0]) + _dot3(p_n, stack(vn_ref))) / l
    for h in range(nh):
        o_ref[:, h * B_DH:(h + 1) * B_DH] = o[h * ds:(h + 1) * ds]


def _fox_sample(pb, kc, vc, f_cache, f_new, prev_out, *, nb, ds, row_block0):
    n_c = kc.shape[1]
    base = (A_QK * 2 + A_V * 2) // B_W
    rb = lambda b: row_block0 + b
    return pl.pallas_call(
        functools.partial(_fox_sample_kernel, ds=ds),
        grid=(nb,),
        in_specs=[pl.BlockSpec((ds, B_W), lambda b: (rb(b), base)),
                  pl.BlockSpec((ds, B_W), lambda b: (rb(b), base + 1)),
                  pl.BlockSpec((ds, B_W), lambda b: (rb(b), base + 2)),
                  pl.BlockSpec((1, n_c, B_DH), lambda b: (b, 0, 0)),
                  pl.BlockSpec((1, n_c, B_DH), lambda b: (b, 0, 0)),
                  pl.BlockSpec((1, 1, n_c), lambda b: (b, 0, 0)),
                  pl.BlockSpec((1, 1, B_HEADS * ds), lambda b: (b, 0, 0)),
                  pl.BlockSpec(memory_space=pl.ANY)],
        out_specs=pl.BlockSpec((ds, B_W), lambda b: (rb(b), 0)),
        out_shape=jax.ShapeDtypeStruct(prev_out.shape, F32),
        input_output_aliases={7: 0},
        compiler_params=_params(("parallel",)),
        name="fox_sample",
    )(pb, pb, pb, kc, vc, f_cache, f_new, prev_out)


def _mlstm_kernel(q_ref, k_ref, v_ref, lir_ref, br_ref, lic_ref, bc_ref, c0_ref, n0_ref, m0_ref, *rest,
                  L, cps, aliased):
    if aliased:
        rest = rest[1:]
    h_ref, cout_ref, nout_ref, mout_ref, c_scr, n_scr, m_scr = rest
    step = pl.program_id(1)

    @pl.when(step == 0)
    def _():
        c_scr[...] = c0_ref[0]
        n_scr[...] = n0_ref[0]
        m_scr[...] = m0_ref[0]

    for j in range(cps):
        rows = pl.ds(j * L, L)
        _mlstm_chunk(q_ref.at[rows], k_ref.at[rows], v_ref.at[rows], lir_ref.at[0, j], br_ref.at[0, j],
                     lic_ref.at[0, j], bc_ref.at[0, j], h_ref.at[rows], c_scr, n_scr, m_scr, L=L)

    @pl.when(step == pl.num_programs(1) - 1)
    def _():
        cout_ref[0] = c_scr[...]
        nout_ref[0] = n_scr[...]
        mout_ref[0] = m_scr[...]


def _mlstm_chunk(q_ref, k_ref, v_ref, lir_ref, br_ref, lic_ref, bc_ref, h_ref, c_scr, n_scr, m_scr, *, L):
    nh = C_HEADS
    hl = nh * L
    seg = lax.broadcasted_iota(jnp.int32, (1, hl), 1) // L
    causal = (lax.broadcasted_iota(jnp.int32, (L, hl), 1) % L) <= lax.broadcasted_iota(jnp.int32, (L, hl), 0)

    def per_head(vals):
        out = vals[0]
        for h in range(1, nh):
            out = jnp.where(seg == h, vals[h], out)
        return out

    seg_max = lambda x, h: jnp.max(jnp.where(seg == h, x, -jnp.inf), axis=1, keepdims=True)

    qf = q_ref[...]
    kf = k_ref[...] * (C_DQK ** -0.5)
    qb = qf.astype(BF16)
    kb = kf.astype(BF16)
    vb = v_ref[...].astype(BF16)
    b_row = br_ref[...]
    li_row = lir_ref[...]
    b_col = [bc_ref[:, h:h + 1] for h in range(nh)]
    li_col = [lic_ref[:, h:h + 1] for h in range(nh)]
    m_prev = [m_scr[h:h + 1, :] for h in range(nh)]

    d = jnp.where(causal, per_head(b_col) - b_row + li_row, -jnp.inf)
    inter = [b_col[h] + m_prev[h] for h in range(nh)]
    m_t = [jnp.maximum(inter[h], seg_max(d, h)) for h in range(nh)]
    pm = jnp.exp(d - per_head(m_t))
    w_inter = [jnp.exp(inter[h] - m_t[h]) for h in range(nh)]

    khead = lax.broadcasted_iota(jnp.int32, (1, C_QK), 1) // C_DQK
    vhead = lax.broadcasted_iota(jnp.int32, (1, C_V), 1) // C_DV
    zero_b = jnp.zeros((), BF16)
    k_stack = jnp.concatenate([jnp.where(khead == h, kb, zero_b) for h in range(nh)], axis=0)
    v_bd = jnp.concatenate([jnp.where(vhead == h, vb, zero_b) for h in range(nh)], axis=0)
    sqk = _dot(qb, k_stack, _NT) * pm
    sv = _dot(sqk.astype(BF16), v_bd)

    b_last = [b_row[:, h * L + L - 1:h * L + L] for h in range(nh)]
    g_row = per_head(b_last) - b_row + li_row
    for h in range(nh):
        qk_sl = slice(h * C_DQK, (h + 1) * C_DQK)
        v_sl = slice(h * C_DV, (h + 1) * C_DV)
        c_prev = c_scr[h]
        n_prev = n_scr[h:h + 1, :]
        num = w_inter[h] * _dot(qb[:, qk_sl], c_prev.astype(BF16), _NT) + sv[:, v_sl]
        den = (w_inter[h] * jnp.sum(qf[:, qk_sl] * n_prev, axis=1, keepdims=True)
               + jnp.sum(jnp.where(seg == h, sqk, 0.0), axis=1, keepdims=True))
        h_ref[:, v_sl] = num / jnp.maximum(jnp.abs(den), jnp.exp(-m_t[h]))

        m_new = jnp.maximum(b_last[h] + m_prev[h], seg_max(g_row, h))
        w_c = jnp.exp(b_last[h] + m_prev[h] - m_new)
        kw = kf[:, qk_sl] * jnp.exp(b_last[h] - b_col[h] + li_col[h] - m_new)
        c_scr[h] = w_c * c_prev + _dot(vb[:, v_sl], kw.astype(BF16), _TN)
        n_scr[h:h + 1, :] = w_c * n_prev + jnp.sum(kw, axis=0, keepdims=True)
        m_scr[h:h + 1, :] = m_new


def _mlstm(p, li_row, b_row, li_col, b_col, c0, n0, m0, *, nb, nc, L, row_block0, n_total, prev_out=None):
    aliased = prev_out is not None
    cps = _chunks_per_step(nc, MLSTM_CHUNKS_PER_STEP)
    assert row_block0 % cps == 0
    nc, row_block0, chunk, L = nc // cps, row_block0 // cps, L, L * cps
    rb = lambda b, c: row_block0 + b * nc + c
    in_specs = [pl.BlockSpec((L, C_QK), lambda b, c: (rb(b, c), 0)),
                pl.BlockSpec((L, C_QK), lambda b, c: (rb(b, c), 1)),
                pl.BlockSpec((L, C_V), lambda b, c: (rb(b, c), 2 * C_QK // C_V)),
                pl.BlockSpec((1, cps, 1, C_HEADS * chunk), lambda b, c: (b, c, 0, 0)),
                pl.BlockSpec((1, cps, 1, C_HEADS * chunk), lambda b, c: (b, c, 0, 0)),
                pl.BlockSpec((1, cps, chunk, C_HEADS), lambda b, c: (b, c, 0, 0)),
                pl.BlockSpec((1, cps, chunk, C_HEADS), lambda b, c: (b, c, 0, 0)),
                pl.BlockSpec((1, C_HEADS, C_DV, C_DQK), lambda b, c: (b, 0, 0, 0)),
                pl.BlockSpec((1, C_HEADS, C_DQK), lambda b, c: (b, 0, 0)),
                pl.BlockSpec((1, C_HEADS, 1), lambda b, c: (b, 0, 0))]
    args = [p, p, p, li_row, b_row, li_col, b_col, c0, n0, m0]
    io_alias = {}
    if aliased:
        in_specs.append(pl.BlockSpec(memory_space=pl.ANY))
        args.append(prev_out)
        io_alias = {len(args) - 1: 0}
    return pl.pallas_call(
        functools.partial(_mlstm_kernel, L=chunk, cps=cps, aliased=aliased),
        grid=(nb, nc),
        in_specs=in_specs,
        out_specs=[pl.BlockSpec((L, C_V), lambda b, c: (rb(b, c), 0)),
                   pl.BlockSpec((1, C_HEADS, C_DV, C_DQK), lambda b, c: (b, 0, 0, 0)),
                   pl.BlockSpec((1, C_HEADS, C_DQK), lambda b, c: (b, 0, 0)),
                   pl.BlockSpec((1, C_HEADS, 1), lambda b, c: (b, 0, 0))],
        out_shape=[jax.ShapeDtypeStruct((n_total, C_V), F32),
                   jax.ShapeDtypeStruct((nb, C_HEADS, C_DV, C_DQK), F32),
                   jax.ShapeDtypeStruct((nb, C_HEADS, C_DQK), F32),
                   jax.ShapeDtypeStruct((nb, C_HEADS, 1), F32)],
        scratch_shapes=[pltpu.VMEM((C_HEADS, C_DV, C_DQK), F32),
                        pltpu.VMEM((C_HEADS, C_DQK), F32),
                        pltpu.VMEM((C_HEADS, 1), F32)],
        input_output_aliases=io_alias,
        compiler_params=_params(("parallel", "arbitrary")),
        name="mlstm_L%d" % chunk,
    )(*args)


def _mixout_kernel(*refs, hd, act, has_b, precise, n_prev):
    refs = list(refs)
    h_ref, a_ref, r_ref = refs[:3]
    b_ref = refs[3] if has_b else None
    k = 4 if has_b else 3
    ga_ref, wo_ref, gf_ref, wrh_ref, wrl_ref, br_ref, cnt0_ref = refs[k:k + 7]
    h1_ref, xn_ref, info_ref, cnt_ref, cnt_scr = refs[k + 7 + n_prev:]
    cast = (lambda x: x) if precise else (lambda x: x.astype(BF16))
    a = a_ref[...]
    r = r_ref[...]
    gate = r * _sigmoid(r) if act == "silu" else _sigmoid(r)
    parts = []
    for hh in range(a.shape[1] // hd):
        sl = slice(hh * hd, (hh + 1) * hd)
        parts.append(cast(_rms(a[:, sl], ga_ref[...]) * gate[:, sl]))
    if has_b:
        parts.append(cast(b_ref[...]))
    cat = jnp.concatenate(parts, axis=1)
    h1 = h_ref[...] + (_dot3(cat, wo_ref[...]) if precise else _dot(cat, wo_ref[...]))
    h1_ref[...] = h1
    xn = _rms(h1, gf_ref[...])
    xn_ref[...] = _pack_bf16_pairs(xn)
    xh, xl = _split(xn)
    logits = _dot(xh, wrh_ref[...]) + _dot(xh, wrl_ref[...]) + _dot(xl, wrh_ref[...]) + br_ref[...]

    lane = lax.broadcasted_iota(jnp.int32, logits.shape, 1)
    lanef = lane.astype(F32)
    is_g = lane < N_GROUPS
    gl = jnp.where(is_g, logits, -jnp.inf)
    gmax = jnp.max(gl, axis=1, keepdims=True)
    gidx = jnp.min(jnp.where(gl == gmax, lanef, float(LANES)), axis=1, keepdims=True)
    wg = 1.0 / jnp.sum(jnp.where(is_g, jnp.exp(gl - gmax), 0.0), axis=1, keepdims=True)
    lo = N_GROUPS + N_EXP * gidx
    el = jnp.where((lanef >= lo) & (lanef < lo + N_EXP), logits, -jnp.inf)
    m1 = jnp.max(el, axis=1, keepdims=True)
    i1 = jnp.min(jnp.where(el == m1, lanef, float(LANES)), axis=1, keepdims=True)
    el2 = jnp.where(lanef == i1, -jnp.inf, el)
    m2 = jnp.max(el2, axis=1, keepdims=True)
    i2 = jnp.min(jnp.where(el2 == m2, lanef, float(LANES)), axis=1, keepdims=True)
    t = jnp.exp(m2 - m1)
    w1 = wg / (1.0 + t)
    w2 = wg * t / (1.0 + t)
    e1 = i1 - N_GROUPS
    e2 = i2 - N_GROUPS

    @pl.when(pl.program_id(0) == 0)
    def _():
        cnt_scr[...] = cnt0_ref[...]

    tm = logits.shape[0]
    pick = jnp.where((lanef == e1) | (lanef == e2), 1.0, 0.0)
    earlier = (lax.broadcasted_iota(jnp.int32, (tm, tm), 1) < lax.broadcasted_iota(jnp.int32, (tm, tm), 0))
    before = _dot(jnp.where(earlier, 1.0, 0.0).astype(BF16), pick.astype(BF16)) + cnt_scr[...]
    r1 = jnp.sum(jnp.where(lanef == e1, before, 0.0), axis=1, keepdims=True)
    r2 = jnp.sum(jnp.where(lanef == e2, before, 0.0), axis=1, keepdims=True)
    cnt_new = cnt_scr[...] + jnp.sum(pick, axis=0, keepdims=True)
    cnt_scr[...] = cnt_new
    cnt_ref[...] = cnt_new
    info_ref[...] = jnp.where(lane == 0, e1, jnp.where(lane == 1, e2, jnp.where(lane == 2, w1, jnp.where(
        lane == 3, w2, jnp.where(lane == 4, r1, jnp.where(lane == 5, r2, 0.0))))))


def _mixout(h, a, a_col, r, r_col, b, g_head, w_o, g_ffn, wr_hi, wr_lo, b_r, counts0, *, hd, act,
            row0, n_rows, prev=None):
    n, d = h.shape
    precise = prev is not None
    tm = _tile(n_rows, 512, SUBLANES)
    assert row0 % tm == 0
    blk0 = row0 // tm
    wa = w_o.shape[0] if b is None else w_o.shape[0] - B_W
    has_b = b is not None
    row = lambda i: (blk0 + i, 0)
    const = lambda i: (0, 0)
    in_specs = [pl.BlockSpec((tm, d), row),
                pl.BlockSpec((tm, wa), lambda i: (blk0 + i, a_col)),
                pl.BlockSpec((tm, wa), lambda i: (blk0 + i, r_col))]
    args = [h, a, r]
    if has_b:
        in_specs.append(pl.BlockSpec((tm, B_W), row))
        args.append(b)
    in_specs += [pl.BlockSpec((1, hd), const), pl.BlockSpec(w_o.shape, const), pl.BlockSpec((1, d), const),
                 pl.BlockSpec((d, LANES), const), pl.BlockSpec((d, LANES), const), pl.BlockSpec((1, LANES), const),
                 pl.BlockSpec((1, LANES), const)]
    args += [g_head.reshape(1, hd), w_o if precise else w_o.astype(BF16), g_ffn.reshape(1, d), wr_hi, wr_lo, b_r,
             counts0]
    io_alias = {}
    if precise:
        for k, arr in enumerate(prev):
            in_specs.append(pl.BlockSpec(memory_space=pl.ANY))
            args.append(arr)
            io_alias[len(args) - 1] = k
    return pl.pallas_call(
        functools.partial(_mixout_kernel, hd=hd, act=act, has_b=has_b, precise=precise, n_prev=len(io_alias)),
        grid=(n_rows // tm,),
        in_specs=in_specs,
        out_specs=[pl.BlockSpec((tm, d), row), pl.BlockSpec((tm, d // 2), row), pl.BlockSpec((tm, LANES), row),
                   pl.BlockSpec((1, LANES), const)],
        out_shape=[jax.ShapeDtypeStruct((n, d), F32), jax.ShapeDtypeStruct((n, d // 2), jnp.int32),
                   jax.ShapeDtypeStruct((n, LANES), F32), jax.ShapeDtypeStruct((1, LANES), F32)],
        scratch_shapes=[pltpu.VMEM((1, LANES), F32)],
        input_output_aliases=io_alias,
        compiler_params=_params(("arbitrary",)),
        name="mixout_" + act + ("_precise" if precise else ""),
    )(*args)


def _sc_gather(table, idx):
    r = idx.shape[0]
    w = table.shape[1]
    n_workers = SC_CORES * SC_SUBCORES
    per_worker = r // n_workers
    step = SC_GATHER_ROWS * SC_GATHER_BUFS
    assert r % n_workers == 0 and per_worker % step == 0, (r, n_workers, step)
    mesh = plsc.VectorSubcoreMesh(core_axis_name="c", subcore_axis_name="s")

    @functools.partial(
        pl.kernel, mesh=mesh,
        out_type=jax.ShapeDtypeStruct((r, w), table.dtype),
        scratch_types=[pltpu.VMEM((SC_GATHER_BUFS, SC_GATHER_ROWS), jnp.int32),
                       pltpu.VMEM((SC_GATHER_BUFS, SC_GATHER_ROWS, w), table.dtype),
                       pltpu.SemaphoreType.DMA((SC_GATHER_BUFS,)),
                       pltpu.SemaphoreType.DMA((SC_GATHER_BUFS,))],
    )
    def gather(table_hbm, idx_hbm, out_hbm, idx_v, rows_v, gather_sem, store_sem):
        worker = lax.axis_index("s") * SC_CORES + lax.axis_index("c")
        base = worker * per_worker

        @pl.loop(0, per_worker // step)
        def _(j):
            off = pl.multiple_of(base + j * step, step)
            rows = lambda b: pl.ds(off + b * SC_GATHER_ROWS, SC_GATHER_ROWS)
            gathers, stores = [], []
            for b in range(SC_GATHER_BUFS):
                pltpu.sync_copy(idx_hbm.at[rows(b)], idx_v.at[b])
                gathers.append(pltpu.async_copy(table_hbm.at[idx_v.at[b]], rows_v.at[b], gather_sem.at[b]))
            for b in range(SC_GATHER_BUFS):
                gathers[b].wait()
                stores.append(pltpu.async_copy(rows_v.at[b], out_hbm.at[rows(b)], store_sem.at[b]))
            for b in range(SC_GATHER_BUFS):
                stores[b].wait()

    return gather(table, idx)


def _sc_scatter_pairs(rows, slots, n_slots):
    n, w = rows.shape
    n_workers = SC_CORES * SC_SUBCORES
    per_worker = n // n_workers
    step = SC_SCATTER_ROWS * SC_SCATTER_BUFS
    assert n % n_workers == 0 and per_worker % step == 0, (n, n_workers, step)
    mesh = plsc.VectorSubcoreMesh(core_axis_name="c", subcore_axis_name="s")

    @functools.partial(
        pl.kernel, mesh=mesh,
        out_type=jax.ShapeDtypeStruct((n_slots, w), rows.dtype),
        scratch_types=[pltpu.VMEM((2 * SC_SCATTER_BUFS, SC_SCATTER_ROWS), jnp.int32),
                       pltpu.VMEM((SC_SCATTER_BUFS, SC_SCATTER_ROWS, w), rows.dtype),
                       pltpu.SemaphoreType.DMA((SC_SCATTER_BUFS,)),
                       pltpu.SemaphoreType.DMA((SC_SCATTER_BUFS,))],
    )
    def scatter(rows_hbm, slots_hbm, out_hbm, idx_v, rows_v, load_sem, store_sem):
        worker = lax.axis_index("s") * SC_CORES + lax.axis_index("c")
        base = worker * per_worker

        @pl.loop(0, per_worker // step)
        def _(j):
            off = pl.multiple_of(base + j * step, SC_SCATTER_ROWS)
            loads, stores = [], []
            for b in range(SC_SCATTER_BUFS):
                r0 = off + b * SC_SCATTER_ROWS
                pltpu.sync_copy(slots_hbm.at[pl.ds(r0, SC_SCATTER_ROWS)], idx_v.at[2 * b])
                pltpu.sync_copy(slots_hbm.at[pl.ds(n + r0, SC_SCATTER_ROWS)], idx_v.at[2 * b + 1])
                loads.append(pltpu.async_copy(rows_hbm.at[pl.ds(r0, SC_SCATTER_ROWS)], rows_v.at[b], load_sem.at[b]))
            for b in range(SC_SCATTER_BUFS):
                loads[b].wait()
                for k in range(2):
                    stores.append(pltpu.async_copy(rows_v.at[b], out_hbm.at[idx_v.at[2 * b + k]], store_sem.at[b]))
            for copy in stores:
                copy.wait()

    return scatter(rows, slots)


def _expert_kernel(te_ref, nu_ref, x_ref, wg_ref, wu_ref, wd_ref, y_ref, wgb, wub, wdb):
    i = pl.program_id(0)
    live = i < nu_ref[0]
    new_expert = (i == 0) | (te_ref[i] != te_ref[jnp.maximum(i - 1, 0)])

    @pl.when(live & new_expert)
    def _():
        wgb[...] = wg_ref[0].astype(BF16)
        wub[...] = wu_ref[0].astype(BF16)
        wdb[...] = wd_ref[0].astype(BF16)

    @pl.when(live)
    def _():
        x = _unpack_bf16_pairs(x_ref[...]).astype(BF16)
        g = _dot(x, wgb[...])
        u = _dot(x, wub[...])
        y_ref[...] = _pack_bf16_pairs(_dot((g * _sigmoid(g) * u).astype(BF16), wdb[...]))

    @pl.when(jnp.logical_not(live))
    def _():
        y_ref[...] = jnp.zeros_like(y_ref)


def _experts(xs, tile_expert, n_used, w_gate, w_up, w_down, *, tm):
    n_slots, dp = xs.shape
    d, f = w_gate.shape[-2:]
    grid_spec = pltpu.PrefetchScalarGridSpec(
        num_scalar_prefetch=2,
        grid=(n_slots // tm,),
        in_specs=[pl.BlockSpec((tm, dp), lambda i, te, nu: (i, 0)),
                  pl.BlockSpec((1, d, f), lambda i, te, nu: (te[i], 0, 0)),
                  pl.BlockSpec((1, d, f), lambda i, te, nu: (te[i], 0, 0)),
                  pl.BlockSpec((1, f, d), lambda i, te, nu: (te[i], 0, 0))],
        out_specs=pl.BlockSpec((tm, dp), lambda i, te, nu: (i, 0)),
        scratch_shapes=[pltpu.VMEM((d, f), BF16), pltpu.VMEM((d, f), BF16), pltpu.VMEM((f, d), BF16)],
    )
    return pl.pallas_call(
        _expert_kernel,
        grid_spec=grid_spec,
        out_shape=jax.ShapeDtypeStruct((n_slots, dp), jnp.int32),
        compiler_params=_params(("arbitrary",)),
        name="moe_experts",
    )(tile_expert, n_used, xs, w_gate, w_up, w_down)


def _combine_kernel(h_ref, info_ref, g_ref, y0_ref, y1_ref, o_ref, *, final_norm):
    h2 = h_ref[...] + (info_ref[:, 2:3] * _unpack_bf16_pairs(y0_ref[...])
                       + info_ref[:, 3:4] * _unpack_bf16_pairs(y1_ref[...]))
    o_ref[...] = _rms(h2, g_ref[...]) if final_norm else h2


def _combine(h, yg, info, g, *, final_norm):
    n, d = h.shape
    tm = _tile(n, 512, SUBLANES)
    nt = n // tm
    return pl.pallas_call(
        functools.partial(_combine_kernel, final_norm=final_norm),
        grid=(nt,),
        in_specs=[pl.BlockSpec((tm, d), lambda i: (i, 0)),
                  pl.BlockSpec((tm, LANES), lambda i: (i, 0)),
                  pl.BlockSpec((1, d), lambda i: (0, 0)),
                  pl.BlockSpec((tm, d // 2), lambda i: (i, 0)),
                  pl.BlockSpec((tm, d // 2), lambda i: (nt + i, 0))],
        out_specs=pl.BlockSpec((tm, d), lambda i: (i, 0)),
        out_shape=jax.ShapeDtypeStruct((n, d), F32),
        compiler_params=_params(("parallel",)),
        name="moe_combine",
    )(h, info, g.reshape(1, d), yg, yg)


def _final_kernel(*refs, sub, prompt_steps):
    g_ref = refs[0]
    h_refs, info_refs, y0_refs, y1_refs = (refs[1 + k * sub:1 + (k + 1) * sub] for k in range(4))
    yp_ref, ys_ref = refs[1 + 4 * sub:]
    tin = h_refs[0].shape[0]
    i = pl.program_id(0)

    def store(o_ref):
        for k in range(sub):
            h2 = h_refs[k][...] + (info_refs[k][:, 2:3] * _unpack_bf16_pairs(y0_refs[k][...])
                                   + info_refs[k][:, 3:4] * _unpack_bf16_pairs(y1_refs[k][...]))
            o_ref[k * tin:(k + 1) * tin, :] = _rms(h2, g_ref[...])

    pl.when(i < prompt_steps)(functools.partial(store, yp_ref))
    pl.when(i >= prompt_steps)(functools.partial(store, ys_ref))


def _final_combine(h, yg, info, g, *, nb, t, n_sample):
    n, d = h.shape
    seq = t - LEAD
    tin = _tile(math.gcd(LEAD, n_sample, seq), LANES, SUBLANES)
    sub = FINAL_SUBTILES if seq % (FINAL_SUBTILES * tin) == 0 and n_sample % (FINAL_SUBTILES * tin) == 0 else 1
    tout = sub * tin
    steps_per_batch = seq // tout
    prompt_steps = nb * steps_per_batch
    tiles_per_batch, lead_tiles, n_tiles = t // tin, LEAD // tin, n // tin

    def in_tile(k, offset=0):
        def index(i):
            ip = jnp.minimum(i, prompt_steps - 1)
            prompt = (ip // steps_per_batch) * tiles_per_batch + lead_tiles + (ip % steps_per_batch) * sub + k
            sample = nb * tiles_per_batch + (i - prompt_steps) * sub + k
            return offset + jnp.where(i < prompt_steps, prompt, sample), 0
        return index

    in_specs = ([pl.BlockSpec((1, d), lambda i: (0, 0))]
                + [pl.BlockSpec((tin, d), in_tile(k)) for k in range(sub)]
                + [pl.BlockSpec((tin, LANES), in_tile(k)) for k in range(sub)]
                + [pl.BlockSpec((tin, d // 2), in_tile(k)) for k in range(sub)]
                + [pl.BlockSpec((tin, d // 2), in_tile(k, n_tiles)) for k in range(sub)])
    return pl.pallas_call(
        functools.partial(_final_kernel, sub=sub, prompt_steps=prompt_steps),
        grid=(prompt_steps + n_sample // tout,),
        in_specs=in_specs,
        out_specs=[pl.BlockSpec((tout, d), lambda i: (jnp.minimum(i, prompt_steps - 1), 0)),
                   pl.BlockSpec((tout, d), lambda i: (jnp.maximum(i - prompt_steps, 0), 0))],
        out_shape=[jax.ShapeDtypeStruct((nb * seq, d), F32), jax.ShapeDtypeStruct((n_sample, d), F32)],
        compiler_params=_params(("arbitrary",)),
        name="moe_combine_final",
    )(g.reshape(1, d), *([h] * sub + [info] * sub + [yg] * (2 * sub)))


def _moe(h1, xn, info, counts_f, w_gate, w_up, w_down, expert_base, g_next, *, final=None):
    n, d = h1.shape
    n_e = N_GROUPS * N_EXP
    tm = _tile(2 * n, MOE_TILE, SUBLANES)
    n_tiles = (2 * n) // tm + n_e
    n_slots = n_tiles * tm
    experts = jnp.arange(n_e, dtype=jnp.int32)
    counts = counts_f[0, :n_e].astype(jnp.int32)
    padded = ((counts + tm - 1) // tm) * tm
    pad_ends = jnp.cumsum(padded)
    pad_starts = pad_ends - padded
    n_used = (pad_ends[-1] // tm).astype(jnp.int32)
    eid = info[:, 0:2].astype(jnp.int32)
    rank = info[:, 4:6].astype(jnp.int32)
    slot_of_pick = jnp.sum(jnp.where(eid[..., None] == experts, pad_starts, 0), axis=-1) + rank
    slots = slot_of_pick.T.reshape(-1)
    tile_starts = jnp.arange(n_tiles, dtype=jnp.int32) * tm
    tile_expert = expert_base + jnp.minimum(jnp.sum(tile_starts[:, None] >= pad_ends[None, :], axis=1), n_e - 1)

    xs = _sc_scatter_pairs(xn, slots, n_slots)
    ys = _experts(xs, tile_expert.astype(jnp.int32), n_used.reshape(1), w_gate, w_up, w_down, tm=tm)
    yg = _sc_gather(ys, slots)
    if final is not None:
        return _final_combine(h1, yg, info, g_next, **final)
    return _combine(h1, yg, info, g_next, final_norm=False)


def _router_weights(w_rg, b_rg, w_re, b_re):
    d = w_rg.shape[0]
    pad = LANES - N_GROUPS - N_GROUPS * N_EXP
    w = jnp.concatenate([w_rg, w_re, jnp.zeros((d, pad), F32)], axis=1)
    b = jnp.concatenate([b_rg, b_re, jnp.zeros((pad,), F32)]).reshape(1, LANES)
    hi = w.astype(BF16)
    lo = (w - hi.astype(F32)).astype(BF16)
    return hi, lo, b


def _rows_to_lanes(x, nb, t, nch):
    return x.reshape(nb, t, nch).transpose(0, 2, 1).reshape(nb * nch, t)


def _even_layer(h, dims, g_mix, w_in, w_g2, b_g, b_f, g_a, w_o, state_gla, ck, cv, clf):
    nb, t, db, ds, npr, n = dims
    d = h.shape[1]
    qa, ka, va, ra, ga, qb, kb, vb, fb = jnp.split(
        w_in, [A_QK, 2 * A_QK, 2 * A_QK + A_V, 2 * A_QK + 2 * A_V, 2 * A_QK + 2 * A_V + A_RANK,
               2 * A_QK + 2 * A_V + A_RANK + B_W, 2 * A_QK + 2 * A_V + A_RANK + 2 * B_W,
               2 * A_QK + 2 * A_V + A_RANK + 3 * B_W], axis=1)
    w_packed = jnp.concatenate(
        [qa, ka, va, ra, qb, kb, vb, ga, fb, jnp.zeros((d, LANES - A_RANK - B_HEADS), F32)], axis=1)
    q_off = 2 * A_QK + 2 * A_V
    colscale = jnp.ones((1, MAIN_W), F32).at[:, q_off:q_off + B_W].set(B_DH ** -0.5 * LOG2E)
    p, pb, k_rows, v_rows = _proj(h, g_mix, w_packed, n_rows=npr, colscale=colscale, kv_col=q_off + B_W)
    p = _proj_precise(h, g_mix, w_packed, p, row0=npr)

    nh = B_HEADS
    fcol = MAIN_W + A_RANK
    zeros_col = lambda r: jnp.zeros((r, 1), F32)
    bias_row = lambda lanes: jnp.tile(b_f, lanes // nh).reshape(1, lanes)
    fb_p = p[:npr, fcol:fcol + nh].reshape(nb, t * nh)
    logf_p, f_p = _gate_scan(fb_p, bias_row(t * nh), zeros_col(nb), mode="fox", act_start=0,
                             valid_start=N_PAD * nh, valid_end=t * nh, seg=None, stride=nh)
    past = ck.shape[1]
    n_c = past * nh
    x_s = _pad_lanes(jnp.concatenate([clf.reshape(db, n_c), p[npr:, fcol:fcol + nh].reshape(db, ds * nh)], axis=1))
    logf_s, f_s = _gate_scan(x_s, bias_row(x_s.shape[1]), zeros_col(db), mode="fox", act_start=n_c,
                             valid_start=0, valid_end=n_c + ds * nh, seg=None, stride=nh)

    nc = t // CHUNK
    oa, s_p = _gla(p, jnp.zeros((nb, A_HEADS, A_DK, A_DV), F32), w_g2, b_g,
                   nb=nb, nc=nc, L=CHUNK, row_block0=0, lead_pad=N_PAD, n_total=n)
    oa, s_s = _gla(p, state_gla, w_g2, b_g, nb=db, nc=1, L=ds, row_block0=npr // ds, lead_pad=0,
                   n_total=n, prev_out=oa, precise=True)

    fk = (f_p * LOG2E).reshape(nb, t, nh).transpose(0, 2, 1).reshape(nb, nh, 1, t)
    ob = _flash(pb, fk, nb=nb, t=t, n_total=n)
    f_cache = (f_s[:, :n_c] * LOG2E).reshape(db, 1, n_c)
    f_new = (f_s[:, n_c:n_c + ds * nh] * LOG2E).reshape(db, ds, nh).transpose(0, 2, 1).reshape(db, 1, nh * ds)
    ob = _fox_sample(p, ck.reshape(db, n_c, B_DH), cv.reshape(db, n_c, B_DH), f_cache, f_new, ob,
                     nb=db, ds=ds, row_block0=npr // ds)

    kcol = q_off + B_W
    states = dict(
        s_p=s_p, s_s=s_s,
        k_p=k_rows.reshape(nb, t, B_HEADS, B_DH)[:, N_PAD:],
        v_p=v_rows.reshape(nb, t, B_HEADS, B_DH)[:, N_PAD:],
        f_p=logf_p.reshape(nb, t, nh)[:, N_PAD:],
        k_s=p[npr:, kcol:kcol + B_W].reshape(db, ds, B_HEADS, B_DH),
        v_s=p[npr:, kcol + B_W:kcol + 2 * B_W].reshape(db, ds, B_HEADS, B_DH),
        f_s=logf_s[:, n_c:n_c + ds * nh].reshape(db, ds, nh))
    return (oa, 0, p, (2 * A_QK + A_V) // A_V, ob, g_a, w_o), states


def _chunk_rows(x, nb, nch, nc, L):
    x = x[:, :nc * L].reshape(nb, nch, nc, L)
    return x.transpose(0, 2, 1, 3).reshape(nb, nc, 1, nch * L), x.transpose(0, 2, 3, 1)


def _odd_layer(h, dims, g_mix, w_in, b_gate, g_c, w_o, c0, n0, m0):
    nb, t, db, ds, npr, n = dims
    d = h.shape[1]
    w_packed = jnp.concatenate(
        [w_in, jnp.zeros((d, LANES - 2 * C_HEADS), F32)], axis=1)
    (p,) = _proj(h, g_mix, w_packed, n_rows=npr)
    p = _proj_precise(h, g_mix, w_packed, p, row0=npr)

    ng = 2 * C_HEADS
    isf = (jnp.arange(ng) >= C_HEADS).astype(F32)
    nc = t // CHUNK

    def gates(rows, nbatch, tt, valid_start, seg):
        x = _pad_lanes(_rows_to_lanes(rows, nbatch, tt, ng))
        val, cum = _gate_scan(x, jnp.tile(b_gate, nbatch).reshape(-1, 1), jnp.tile(isf, nbatch).reshape(-1, 1),
                              mode="mlstm", act_start=0, valid_start=valid_start, valid_end=tt, seg=seg)
        return val.reshape(nbatch, ng, -1), cum.reshape(nbatch, ng, -1)

    val_p, cum_p = gates(p[:npr, MAIN_W:MAIN_W + ng], nb, t, N_PAD, CHUNK)
    val_s, cum_s = gates(p[npr:, MAIN_W:MAIN_W + ng], db, ds, 0, ds)

    def chunked(val, cum, nbatch, ncs, L):
        li_row, li_col = _chunk_rows(val[:, :C_HEADS].reshape(nbatch * C_HEADS, -1), nbatch, C_HEADS, ncs, L)
        b_row, b_col = _chunk_rows(cum[:, C_HEADS:].reshape(nbatch * C_HEADS, -1), nbatch, C_HEADS, ncs, L)
        return li_row, b_row, li_col, b_col

    zc = jnp.zeros((nb, C_HEADS, C_DV, C_DQK), F32)
    zn = jnp.zeros((nb, C_HEADS, C_DQK), F32)
    zm = jnp.zeros((nb, C_HEADS, 1), F32)
    hm, c_p, n_p, m_p = _mlstm(p, *chunked(val_p, cum_p, nb, nc, CHUNK), zc, zn, zm,
                               nb=nb, nc=nc, L=CHUNK, row_block0=0, n_total=n)
    hm, c_s, n_s, m_s = _mlstm(p, *chunked(val_s, cum_s, db, 1, ds), c0, n0, m0.reshape(db, C_HEADS, 1),
                               nb=db, nc=1, L=ds, row_block0=npr // ds, n_total=n, prev_out=hm)
    states = dict(c_p=c_p, n_p=n_p, m_p=m_p.reshape(nb, C_HEADS), c_s=c_s, n_s=n_s, m_s=m_s.reshape(db, C_HEADS))
    return (hm, 0, p, (2 * C_QK + C_V) // C_V, None, g_c, w_o), states


def kernel(x_prompt, x_sample, state_gla, cache_fox_k, cache_fox_v, cache_fox_logf, state_mlstm_c, state_mlstm_n, state_mlstm_m, meta_tokens, norm_mix, norm_ffn, norm_final, w_in_even, w_gla_gate2, b_gla_gate, b_fox_f, g_gla_out, w_out_even, w_in_odd, b_mlstm_gate, g_mlstm_out, w_out_odd, w_router_group, b_router_group, w_router_expert, b_router_expert, w_exp_gate, w_exp_up, w_exp_down):
    nb, seq, d = x_prompt.shape
    db, ds, _ = x_sample.shape
    t = LEAD + seq
    npr, nsm = nb * t, db * ds
    n = npr + nsm
    dims = (nb, t, db, ds, npr, n)
    depth = norm_mix.shape[0]
    n_e = N_GROUPS * N_EXP
    f = w_exp_gate.shape[-1]

    pad_rows = jnp.zeros((N_PAD, d), F32)
    h = jnp.concatenate([piece for b in range(nb) for piece in (pad_rows, meta_tokens, x_prompt[b])]
                        + [x_sample.reshape(nsm, d)], axis=0)
    wg_all = w_exp_gate.reshape(depth * n_e, d, f)
    wu_all = w_exp_up.reshape(depth * n_e, d, f)
    wd_all = w_exp_down.reshape(depth * n_e, f, d)

    even, odd = [], []
    for l in range(depth):
        if l % 2 == 0:
            e = l // 2
            mix, st = _even_layer(h, dims, norm_mix[l], w_in_even[e], w_gla_gate2[e], b_gla_gate[e], b_fox_f[e],
                                  g_gla_out[e], w_out_even[e], state_gla[e], cache_fox_k[e], cache_fox_v[e],
                                  cache_fox_logf[e])
            even.append(st)
            hd, act = A_DV, "silu"
        else:
            o = l // 2
            mix, st = _odd_layer(h, dims, norm_mix[l], w_in_odd[o], b_mlstm_gate[o], g_mlstm_out[o], w_out_odd[o],
                                 state_mlstm_c[o], state_mlstm_n[o], state_mlstm_m[o])
            odd.append(st)
            hd, act = C_DV, "sigmoid"
        a, a_col, r, r_col, b, g_head, w_o = mix
        wr_hi, wr_lo, b_r = _router_weights(w_router_group[l], b_router_group[l], w_router_expert[l],
                                            b_router_expert[l])
        mix_args = (h, a, a_col, r, r_col, b, g_head, w_o, norm_ffn[l], wr_hi, wr_lo, b_r)
        h1, xn, info, counts = _mixout(*mix_args, jnp.zeros((1, LANES), F32), hd=hd, act=act, row0=0, n_rows=npr)
        h1, xn, info, counts = _mixout(*mix_args, counts, hd=hd, act=act, row0=npr, n_rows=nsm,
                                       prev=(h1, xn, info))
        last = l == depth - 1
        h = _moe(h1, xn, info, counts, wg_all, wu_all, wd_all, l * n_e, norm_final if last else norm_ffn[l],
                 final=dict(nb=nb, t=t, n_sample=nsm) if last else None)

    y_prompt = h[0].reshape(nb, seq, d)
    y_sample = h[1].reshape(db, ds, d)
    stack = lambda sts, key: jnp.stack([s[key] for s in sts])
    return (y_prompt, y_sample,
            stack(even, "s_p"), stack(even, "k_p"), stack(even, "v_p"), stack(even, "f_p"),
            stack(odd, "c_p"), stack(odd, "n_p"), stack(odd, "m_p"),
            stack(even, "s_s"), stack(even, "k_s"), stack(even, "v_s"), stack(even, "f_s"),
            stack(odd, "c_s"), stack(odd, "n_s"), stack(odd, "m_s"))
```

```python
import functools
import math

import jax
import jax.numpy as jnp
from jax import lax
from jax.experimental import pallas as pl
from jax.experimental.pallas import tpu as pltpu
from jax.experimental.pallas import tpu_sc as plsc

F32 = jnp.float32
BF16 = jnp.bfloat16

CHUNK = 64
N_META = 16
LEAD = 128
N_PAD = LEAD - N_META
A_HEADS, A_DK, A_DV, A_RANK = 4, 64, 128, 16
A_GATE_NORM = 16.0
B_HEADS, B_DH = 4, 128
C_HEADS, C_DQK, C_DV = 4, 128, 256
GATE_CAP = 15.0
N_GROUPS, N_EXP = 4, 8
EPS = 1e-6
NEG = -1e30
LOG2E = 1.4426950408889634
A_QK = A_HEADS * A_DK
A_V = A_HEADS * A_DV
B_W = B_HEADS * B_DH
C_QK = C_HEADS * C_DQK
C_V = C_HEADS * C_DV

LANES = 128
SUBLANES = 8
VMEM_LIMIT_BYTES = 56 * 1024 * 1024
GLA_SUB = 16
SC_CORES, SC_SUBCORES = 2, 16
SC_GATHER_ROWS = 16
SC_GATHER_BUFS = 4
MOE_TILE = 1024
GLA_CHUNKS_PER_STEP = 5
MLSTM_CHUNKS_PER_STEP = 1
FINAL_SUBTILES = 4
SC_SCATTER_ROWS = 16
SC_SCATTER_BUFS = 3
FLASH_HEADS = 4
MAIN_W = 3072
PROJ_W = MAIN_W + LANES

_NT = (((1,), (1,)), ((), ()))
_TN = (((0,), (0,)), ((), ()))
_NN = (((1,), (0,)), ((), ()))


def _params(sem):
    return pltpu.CompilerParams(dimension_semantics=sem, vmem_limit_bytes=VMEM_LIMIT_BYTES)


def _tile(n, pref, mult):
    t = (min(pref, n) // mult) * mult
    while t > mult and n % t:
        t -= mult
    assert t >= mult and n % t == 0, (n, pref, mult)
    return t


def _dot(a, b, dims=_NN):
    return lax.dot_general(a, b, dims, preferred_element_type=F32)


def _split(x):
    hi = x.astype(BF16)
    lo = (x - hi.astype(F32)).astype(BF16)
    return hi, lo


def _dot3(a, b, dims=_NN):
    ah, al = _split(a)
    bh, bl = _split(b)
    return _dot(ah, bh, dims) + _dot(ah, bl, dims) + _dot(al, bh, dims)


def _log_sigmoid(x):
    return jnp.minimum(x, 0.0) - jnp.log1p(jnp.exp(-jnp.abs(x)))


def _sigmoid(x):
    return 1.0 / (1.0 + jnp.exp(-x))


def _rms(x, g):
    return x * lax.rsqrt(jnp.mean(x * x, axis=-1, keepdims=True) + EPS) * g


def _pack_bf16_pairs(x):
    w = x.shape[1] // 2
    hi = lax.bitcast_convert_type(x[:, :w].astype(BF16).astype(F32), jnp.int32)
    lo = lax.bitcast_convert_type(x[:, w:].astype(BF16).astype(F32), jnp.int32)
    return hi | lax.shift_right_logical(lo, 16)


def _unpack_bf16_pairs(p):
    hi = lax.bitcast_convert_type(p & jnp.int32(-65536), F32)
    lo = lax.bitcast_convert_type(lax.shift_left(p, 16), F32)
    return jnp.concatenate([hi, lo], axis=1)


def _cumsum_rows(x):
    n = x.shape[0]
    row = lax.broadcasted_iota(jnp.int32, x.shape, 0)
    s = 1
    while s < n:
        x = x + jnp.where(row >= s, pltpu.roll(x, s, axis=0), 0.0)
        s *= 2
    return x


def _proj_kernel(x_ref, g_ref, w_ref, *rest, col_chunk, kv_col):
    if kv_col is None:
        (o_ref,) = rest
    else:
        cs_ref, o_ref, ob_ref, *kv_refs = rest
    tm = x_ref.shape[0]
    xn = _rms(x_ref[...], g_ref[...]).astype(BF16)
    for c0 in range(0, PROJ_W, col_chunk):
        c1 = min(c0 + col_chunk, PROJ_W)
        y = _dot(xn, w_ref[:, c0:c1])
        o_ref[:, c0:c1] = y
        if kv_col is not None:
            if c0 < MAIN_W:
                m1 = min(c1, MAIN_W)
                ob_ref[:, c0:m1] = (y[:, :m1 - c0] * cs_ref[:, c0:m1]).astype(BF16)
            for g0 in range(c0, c1, LANES):
                rel = g0 - kv_col
                if 0 <= rel < 2 * B_W:
                    head = (rel % B_W) // B_DH
                    kv_refs[rel // B_W][pl.ds(head, tm, stride=B_HEADS), :] = y[:, g0 - c0:g0 - c0 + LANES]


def _proj_precise_kernel(x_ref, g_ref, w_ref, prev_ref, o_ref, *, col_chunk):
    del prev_ref
    xn = _rms(x_ref[...], g_ref[...])
    for c0 in range(0, PROJ_W, col_chunk):
        c1 = min(c0 + col_chunk, PROJ_W)
        o_ref[:, c0:c1] = _dot3(xn, w_ref[:, c0:c1])


def _proj(h, g, w_packed, *, n_rows, colscale=None, kv_col=None):
    n, d = h.shape
    tm = _tile(n_rows, 512, 16)
    in_specs = [pl.BlockSpec((tm, d), lambda i: (i, 0)),
                pl.BlockSpec((1, d), lambda i: (0, 0)),
                pl.BlockSpec((d, PROJ_W), lambda i: (0, 0))]
    args = [h, g.reshape(1, d), w_packed.astype(BF16)]
    out_specs = [pl.BlockSpec((tm, PROJ_W), lambda i: (i, 0))]
    out_shape = [jax.ShapeDtypeStruct((n, PROJ_W), F32)]
    if kv_col is not None:
        in_specs.append(pl.BlockSpec((1, MAIN_W), lambda i: (0, 0)))
        args.append(colscale)
        out_specs += [pl.BlockSpec((tm, MAIN_W), lambda i: (i, 0))] + [pl.BlockSpec((tm * B_HEADS, B_DH), lambda i: (i, 0))] * 2
        out_shape += ([jax.ShapeDtypeStruct((n_rows, MAIN_W), BF16)]
                      + [jax.ShapeDtypeStruct((n_rows * B_HEADS, B_DH), F32)] * 2)
    return pl.pallas_call(
        functools.partial(_proj_kernel, col_chunk=640, kv_col=kv_col),
        grid=(n_rows // tm,),
        in_specs=in_specs,
        out_specs=out_specs,
        out_shape=out_shape,
        compiler_params=_params(("parallel",)),
        name="proj",
    )(*args)


def _proj_precise(h, g, w_packed, prev, *, row0):
    n, d = h.shape
    tm = _tile(n - row0, 512, SUBLANES)
    assert row0 % tm == 0
    return pl.pallas_call(
        functools.partial(_proj_precise_kernel, col_chunk=640),
        grid=((n - row0) // tm,),
        in_specs=[pl.BlockSpec((tm, d), lambda i: (row0 // tm + i, 0)),
                  pl.BlockSpec((1, d), lambda i: (0, 0)),
                  pl.BlockSpec((d, PROJ_W), lambda i: (0, 0)),
                  pl.BlockSpec(memory_space=pl.ANY)],
        out_specs=pl.BlockSpec((tm, PROJ_W), lambda i: (row0 // tm + i, 0)),
        out_shape=jax.ShapeDtypeStruct((n, PROJ_W), F32),
        input_output_aliases={3: 0},
        compiler_params=_params(("parallel",)),
        name="proj_precise",
    )(h, g.reshape(1, d), w_packed, prev)


def _gate_scan_kernel(x_ref, bias_ref, isf_ref, val_ref, cum_ref, *, mode, act_start, valid_start, valid_end, seg, stride):
    x = x_ref[...]
    lane = lax.broadcasted_iota(jnp.int32, x.shape, 1)
    valid = (lane >= valid_start) & (lane < valid_end)
    if mode == "fox":
        val = jnp.where(lane >= act_start, _log_sigmoid(x + bias_ref[...]), x)
        val = jnp.where(valid, val, 0.0)
        add = val
    else:
        gate = GATE_CAP * jnp.tanh((x + bias_ref[...]) / GATE_CAP)
        isf = isf_ref[...] > 0.5
        val = jnp.where(isf, jnp.where(valid, _log_sigmoid(gate), 0.0),
                        jnp.where(valid, gate, -jnp.inf))
        add = jnp.where(isf, val, 0.0)
    val_ref[...] = val
    n = x.shape[1]
    pos = lane if seg is None else lane % seg
    limit = n if seg is None else seg
    s = stride
    while s < limit:
        add = add + jnp.where(pos >= s, pltpu.roll(add, s, axis=1), 0.0)
        s *= 2
    cum_ref[...] = add


def _gate_scan(x, bias, isf, *, mode, act_start, valid_start, valid_end, seg, stride=1):
    r, n = x.shape
    full = lambda shape: pl.BlockSpec(shape, lambda i: (0,) * len(shape))
    return pl.pallas_call(
        functools.partial(_gate_scan_kernel, mode=mode, act_start=act_start,
                          valid_start=valid_start, valid_end=valid_end, seg=seg, stride=stride),
        grid=(1,),
        in_specs=[full((r, n)), full(bias.shape), full((r, 1))],
        out_specs=[full((r, n)), full((r, n))],
        out_shape=[jax.ShapeDtypeStruct((r, n), F32)] * 2,
        compiler_params=_params(("arbitrary",)),
        name="gate_scan_" + mode,
    )(x, bias, isf)


def _pad_lanes(x):
    n = x.shape[-1]
    m = -(-n // LANES) * LANES
    return x if m == n else jnp.pad(x, ((0, 0), (0, m - n)))


def _gla_kernel(qk_ref, v_ref, sm_ref, s0_ref, wg2_ref, wg2t_ref, bgr_ref, bgc_ref, *rest,
                L, cps, sub, lead_pad, aliased, precise):
    if aliased:
        rest = rest[1:]
    o_ref, sout_ref, s_scr = rest
    step = pl.program_id(1)
    for j in range(cps):
        rows = pl.ds(j * L, L)
        _gla_chunk(qk_ref.at[rows], v_ref.at[rows], sm_ref.at[rows], s0_ref, wg2_ref, wg2t_ref, bgr_ref, bgc_ref,
                   o_ref.at[rows], sout_ref, s_scr, c=step * cps + j, first=(j == 0), last=(j == cps - 1),
                   L=L, sub=sub, lead_pad=lead_pad, precise=precise)


def _gla_chunk(qk_ref, v_ref, sm_ref, s0_ref, wg2_ref, wg2t_ref, bgr_ref, bgc_ref, o_ref, sout_ref, s_scr, *,
               c, first, last, L, sub, lead_pad, precise):
    nh, dk, dv = A_HEADS, A_DK, A_DV
    cast = (lambda x: x) if precise else (lambda x: x.astype(BF16))
    mm = _dot3 if precise else _dot

    if first:
        @pl.when(pl.program_id(1) == 0)
        def _():
            s_scr[...] = jnp.zeros_like(s_scr)
            for h in range(nh):
                s_scr[h * dk:(h + 1) * dk, h * dv:(h + 1) * dv] = s0_ref[0, h]

    qk = qk_ref[...]
    q = qk[:, :A_QK] * (A_DK ** -0.5)
    k = qk[:, A_QK:]
    v = v_ref[...]
    ga = sm_ref[:, :A_RANK]
    row = lax.broadcasted_iota(jnp.int32, (L, 1), 0)
    valid = (c * L + row) >= lead_pad
    z = _dot3(ga, wg2_ref[...]) + bgr_ref[...]
    loga = jnp.where(valid, _log_sigmoid(z) / A_GATE_NORM, 0.0)
    k = jnp.where(valid, k, 0.0)
    b = _cumsum_rows(loga)
    b_last = b[L - 1:L, :]
    lane_t = lax.broadcasted_iota(jnp.int32, (1, L), 1)
    zt = _dot3(wg2t_ref[...], ga, _NT) + bgc_ref[...]
    logat = jnp.where((c * L + lane_t) >= lead_pad, _log_sigmoid(zt) / A_GATE_NORM, 0.0)
    b_last_col = jnp.sum(logat, axis=1, keepdims=True)

    qhead = lax.broadcasted_iota(jnp.int32, (1, A_QK), 1) // dk
    vhead = lax.broadcasted_iota(jnp.int32, (1, A_V), 1) // dv
    vb = cast(v)
    zero_b = jnp.zeros((), vb.dtype)
    v_bd = jnp.concatenate([jnp.where(vhead == h, vb, zero_b) for h in range(nh)], axis=0)

    rows_all = lax.broadcasted_iota(jnp.int32, (L, 1), 0)
    a_rows = []
    for i in range(L // sub):
        r0 = i * sub
        ci = jnp.zeros((1, A_QK), F32) if i == 0 else b[r0 - 1:r0, :]
        qt = cast(q[r0:r0 + sub] * jnp.exp(b[r0:r0 + sub] - ci))
        kt = cast(jnp.where(rows_all < r0 + sub, k * jnp.exp(ci - b), 0.0))
        k_stack = jnp.concatenate([jnp.where(qhead == h, kt, zero_b) for h in range(nh)], axis=0)
        a_rows.append(mm(qt, k_stack, _NT))
    a = a_rows[0] if len(a_rows) == 1 else jnp.concatenate(a_rows, axis=0)
    t_idx = lax.broadcasted_iota(jnp.int32, (L, nh * L), 0)
    s_idx = lax.broadcasted_iota(jnp.int32, (L, nh * L), 1) % L
    a = jnp.where(s_idx <= t_idx, a, 0.0)
    o_intra = mm(cast(a), v_bd)

    s_full = s_scr[...]
    o_inter = mm(cast(q * jnp.exp(b)), cast(s_full))
    o_ref[...] = o_inter + o_intra

    k_hat = cast(k * jnp.exp(b_last - b))
    upd = mm(k_hat, vb, _TN)
    khead_col = lax.broadcasted_iota(jnp.int32, (A_QK, 1), 0) // dk
    s_new = jnp.exp(b_last_col) * s_full + jnp.where(khead_col == vhead, upd, 0.0)
    s_scr[...] = s_new

    if last:
        @pl.when(pl.program_id(1) == pl.num_programs(1) - 1)
        def _():
            for h in range(nh):
                sout_ref[0, h] = s_new[h * dk:(h + 1) * dk, h * dv:(h + 1) * dv]


def _chunks_per_step(nc, want):
    return next(k for k in (want, 2, 1) if k <= want and nc % k == 0)


def _gla(p, s0, wg2, bg, *, nb, nc, L, row_block0, lead_pad, n_total, prev_out=None, precise=False):
    aliased = prev_out is not None
    cps = _chunks_per_step(nc, GLA_CHUNKS_PER_STEP)
    assert row_block0 % cps == 0
    nc, L, row_block0, chunk = nc // cps, L * cps, row_block0 // cps, L
    rb = lambda b, c: row_block0 + b * nc + c
    in_specs = [pl.BlockSpec((L, 2 * A_QK), lambda b, c: (rb(b, c), 0)),
                pl.BlockSpec((L, A_V), lambda b, c: (rb(b, c), 2 * A_QK // A_V)),
                pl.BlockSpec((L, LANES), lambda b, c: (rb(b, c), MAIN_W // LANES)),
                pl.BlockSpec((1, A_HEADS, A_DK, A_DV), lambda b, c: (b, 0, 0, 0)),
                pl.BlockSpec((A_RANK, A_QK), lambda b, c: (0, 0)),
                pl.BlockSpec((A_QK, A_RANK), lambda b, c: (0, 0)),
                pl.BlockSpec((1, A_QK), lambda b, c: (0, 0)),
                pl.BlockSpec((A_QK, 1), lambda b, c: (0, 0))]
    args = [p, p, p, s0, wg2, wg2.T, bg.reshape(1, A_QK), bg.reshape(A_QK, 1)]
    io_alias = {}
    if aliased:
        in_specs.append(pl.BlockSpec(memory_space=pl.ANY))
        args.append(prev_out)
        io_alias = {len(args) - 1: 0}
    return pl.pallas_call(
        functools.partial(_gla_kernel, L=chunk, cps=cps, sub=min(GLA_SUB, chunk), lead_pad=lead_pad,
                          aliased=aliased, precise=precise),
        grid=(nb, nc),
        in_specs=in_specs,
        out_specs=[pl.BlockSpec((L, A_V), lambda b, c: (rb(b, c), 0)),
                   pl.BlockSpec((1, A_HEADS, A_DK, A_DV), lambda b, c: (b, 0, 0, 0))],
        out_shape=[jax.ShapeDtypeStruct((n_total, A_V), F32),
                   jax.ShapeDtypeStruct((nb, A_HEADS, A_DK, A_DV), F32)],
        scratch_shapes=[pltpu.VMEM((A_QK, A_V), F32)],
        input_output_aliases=io_alias,
        compiler_params=_params(("parallel", "arbitrary")),
        name="gla_L%d" % chunk,
    )(*args)


def _flash_kernel(qi_ref, kj_ref, q_ref, k_ref, v_ref, fk_ref, o_ref, m_scr, l_scr, acc_scr, *, blk, lead_pad):
    step = pl.program_id(2)
    i = qi_ref[step]
    j = kj_ref[step]

    @pl.when(j == 0)
    def _():
        m_scr[...] = jnp.full_like(m_scr, -jnp.inf)
        l_scr[...] = jnp.zeros_like(l_scr)
        acc_scr[...] = jnp.zeros_like(acc_scr)

    def update(masked):
        for g in range(FLASH_HEADS):
            sl = slice(g * B_DH, (g + 1) * B_DH)
            s = _dot(q_ref[:, sl], k_ref[:, sl], _NT) - fk_ref[0, g]
            if masked:
                qpos = i * blk + lax.broadcasted_iota(jnp.int32, (blk, blk), 0)
                kpos = j * blk + lax.broadcasted_iota(jnp.int32, (blk, blk), 1)
                s = jnp.where((kpos <= qpos) & (kpos >= lead_pad), s, NEG)
            m_prev = m_scr[g]
            m_new = jnp.maximum(m_prev, jnp.max(s, axis=1, keepdims=True))
            alpha = jnp.exp2(m_prev - m_new)
            p = jnp.exp2(s - m_new)
            l_scr[g] = alpha * l_scr[g] + jnp.sum(p, axis=1, keepdims=True)
            acc_scr[g] = alpha * acc_scr[g] + _dot(p.astype(BF16), v_ref[:, sl])
            m_scr[g] = m_new

    edge = (j == i) | (j == 0)
    pl.when(edge)(functools.partial(update, True))
    pl.when(jnp.logical_not(edge))(functools.partial(update, False))

    @pl.when(j == i)
    def _():
        for g in range(FLASH_HEADS):
            o_ref[:, g * B_DH:(g + 1) * B_DH] = acc_scr[g] / l_scr[g]


def _flash(pb, fk, *, nb, t, n_total):
    blk = _tile(t, 640, LANES)
    nq = t // blk
    hg = FLASH_HEADS
    w = hg * B_DH
    pairs = [(i, j) for i in range(nq) for j in range(i + 1)]
    qi = jnp.asarray([p[0] for p in pairs], jnp.int32)
    kj = jnp.asarray([p[1] for p in pairs], jnp.int32)
    qc, kc, vc = (A_QK * 2 + A_V * 2) // w, (A_QK * 2 + A_V * 2 + B_W) // w, (A_QK * 2 + A_V * 2 + 2 * B_W) // w
    grid_spec = pltpu.PrefetchScalarGridSpec(
        num_scalar_prefetch=2,
        grid=(nb, B_HEADS // hg, len(pairs)),
        in_specs=[pl.BlockSpec((blk, w), lambda b, h, s, qi, kj: (b * nq + qi[s], qc + h)),
                  pl.BlockSpec((blk, w), lambda b, h, s, qi, kj: (b * nq + kj[s], kc + h)),
                  pl.BlockSpec((blk, w), lambda b, h, s, qi, kj: (b * nq + kj[s], vc + h)),
                  pl.BlockSpec((1, hg, 1, blk), lambda b, h, s, qi, kj: (b, h, 0, kj[s]))],
        out_specs=pl.BlockSpec((blk, w), lambda b, h, s, qi, kj: (b * nq + qi[s], h)),
        scratch_shapes=[pltpu.VMEM((hg, blk, 1), F32), pltpu.VMEM((hg, blk, 1), F32),
                        pltpu.VMEM((hg, blk, B_DH), F32)],
    )
    return pl.pallas_call(
        functools.partial(_flash_kernel, blk=blk, lead_pad=N_PAD),
        grid_spec=grid_spec,
        out_shape=jax.ShapeDtypeStruct((n_total, B_W), F32),
        compiler_params=_params(("parallel", "parallel", "arbitrary")),
        name="fox_flash",
    )(qi, kj, pb, pb, pb, fk)


def _fox_sample_kernel(q_ref, kn_ref, vn_ref, kc_ref, vc_ref, fc_ref, fn_ref, prev_ref, o_ref, *, ds):
    del prev_ref
    nh = B_HEADS
    stack = lambda ref: jnp.concatenate([ref[:, h * B_DH:(h + 1) * B_DH] for h in range(nh)], axis=0)
    q = stack(q_ref) * (B_DH ** -0.5 * LOG2E)
    rows = nh * ds
    n_c = kc_ref.shape[1]
    qh_c = lax.broadcasted_iota(jnp.int32, (rows, n_c), 0) // ds
    kh_c = lax.broadcasted_iota(jnp.int32, (rows, n_c), 1) % nh
    s_c = _dot3(q, kc_ref[0], _NT) - fc_ref[0]
    s_c = jnp.where(qh_c == kh_c, s_c, NEG)
    r_i = lax.broadcasted_iota(jnp.int32, (rows, rows), 0)
    c_i = lax.broadcasted_iota(jnp.int32, (rows, rows), 1)
    s_n = _dot3(q, stack(kn_ref), _NT) - fn_ref[0]
    s_n = jnp.where((r_i // ds == c_i // ds) & (c_i % ds <= r_i % ds), s_n, NEG)
    m = jnp.maximum(jnp.max(s_c, axis=1, keepdims=True), jnp.max(s_n, axis=1, keepdims=True))
    p_c = jnp.exp2(s_c - m)
    p_n = jnp.exp2(s_n - m)
    l = jnp.sum(p_c, axis=1, keepdims=True) + jnp.sum(p_n, axis=1, keepdims=True)
    o = (_dot3(p_c, vc_ref[0]) + _dot3(p_n, stack(vn_ref))) / l
    for h in range(nh):
        o_ref[:, h * B_DH:(h + 1) * B_DH] = o[h * ds:(h + 1) * ds]


def _fox_sample(pb, kc, vc, f_cache, f_new, prev_out, *, nb, ds, row_block0):
    n_c = kc.shape[1]
    base = (A_QK * 2 + A_V * 2) // B_W
    rb = lambda b: row_block0 + b
    return pl.pallas_call(
        functools.partial(_fox_sample_kernel, ds=ds),
        grid=(nb,),
        in_specs=[pl.BlockSpec((ds, B_W), lambda b: (rb(b), base)),
                  pl.BlockSpec((ds, B_W), lambda b: (rb(b), base + 1)),
                  pl.BlockSpec((ds, B_W), lambda b: (rb(b), base + 2)),
                  pl.BlockSpec((1, n_c, B_DH), lambda b: (b, 0, 0)),
                  pl.BlockSpec((1, n_c, B_DH), lambda b: (b, 0, 0)),
                  pl.BlockSpec((1, 1, n_c), lambda b: (b, 0, 0)),
                  pl.BlockSpec((1, 1, B_HEADS * ds), lambda b: (b, 0, 0)),
                  pl.BlockSpec(memory_space=pl.ANY)],
        out_specs=pl.BlockSpec((ds, B_W), lambda b: (rb(b), 0)),
        out_shape=jax.ShapeDtypeStruct(prev_out.shape, F32),
        input_output_aliases={7: 0},
        compiler_params=_params(("parallel",)),
        name="fox_sample",
    )(pb, pb, pb, kc, vc, f_cache, f_new, prev_out)


def _mlstm_kernel(q_ref, k_ref, v_ref, lir_ref, br_ref, lic_ref, bc_ref, c0_ref, n0_ref, m0_ref, *rest,
                  L, cps, aliased):
    if aliased:
        rest = rest[1:]
    h_ref, cout_ref, nout_ref, mout_ref, c_scr, n_scr, m_scr = rest
    step = pl.program_id(1)

    @pl.when(step == 0)
    def _():
        c_scr[...] = c0_ref[0]
        n_scr[...] = n0_ref[0]
        m_scr[...] = m0_ref[0]

    for j in range(cps):
        rows = pl.ds(j * L, L)
        _mlstm_chunk(q_ref.at[rows], k_ref.at[rows], v_ref.at[rows], lir_ref.at[0, j], br_ref.at[0, j],
                     lic_ref.at[0, j], bc_ref.at[0, j], h_ref.at[rows], c_scr, n_scr, m_scr, L=L)

    @pl.when(step == pl.num_programs(1) - 1)
    def _():
        cout_ref[0] = c_scr[...]
        nout_ref[0] = n_scr[...]
        mout_ref[0] = m_scr[...]


def _mlstm_chunk(q_ref, k_ref, v_ref, lir_ref, br_ref, lic_ref, bc_ref, h_ref, c_scr, n_scr, m_scr, *, L):
    nh = C_HEADS
    hl = nh * L
    seg = lax.broadcasted_iota(jnp.int32, (1, hl), 1) // L
    causal = (lax.broadcasted_iota(jnp.int32, (L, hl), 1) % L) <= lax.broadcasted_iota(jnp.int32, (L, hl), 0)

    def per_head(vals):
        out = vals[0]
        for h in range(1, nh):
            out = jnp.where(seg == h, vals[h], out)
        return out

    seg_max = lambda x, h: jnp.max(jnp.where(seg == h, x, -jnp.inf), axis=1, keepdims=True)

    qf = q_ref[...]
    kf = k_ref[...] * (C_DQK ** -0.5)
    qb = qf.astype(BF16)
    kb = kf.astype(BF16)
    vb = v_ref[...].astype(BF16)
    b_row = br_ref[...]
    li_row = lir_ref[...]
    b_col = [bc_ref[:, h:h + 1] for h in range(nh)]
    li_col = [lic_ref[:, h:h + 1] for h in range(nh)]
    m_prev = [m_scr[h:h + 1, :] for h in range(nh)]

    d = jnp.where(causal, per_head(b_col) - b_row + li_row, -jnp.inf)
    inter = [b_col[h] + m_prev[h] for h in range(nh)]
    m_t = [jnp.maximum(inter[h], seg_max(d, h)) for h in range(nh)]
    pm = jnp.exp(d - per_head(m_t))
    w_inter = [jnp.exp(inter[h] - m_t[h]) for h in range(nh)]

    khead = lax.broadcasted_iota(jnp.int32, (1, C_QK), 1) // C_DQK
    vhead = lax.broadcasted_iota(jnp.int32, (1, C_V), 1) // C_DV
    zero_b = jnp.zeros((), BF16)
    k_stack = jnp.concatenate([jnp.where(khead == h, kb, zero_b) for h in range(nh)], axis=0)
    v_bd = jnp.concatenate([jnp.where(vhead == h, vb, zero_b) for h in range(nh)], axis=0)
    sqk = _dot(qb, k_stack, _NT) * pm
    sv = _dot(sqk.astype(BF16), v_bd)

    b_last = [b_row[:, h * L + L - 1:h * L + L] for h in range(nh)]
    g_row = per_head(b_last) - b_row + li_row
    for h in range(nh):
        qk_sl = slice(h * C_DQK, (h + 1) * C_DQK)
        v_sl = slice(h * C_DV, (h + 1) * C_DV)
        c_prev = c_scr[h]
        n_prev = n_scr[h:h + 1, :]
        num = w_inter[h] * _dot(qb[:, qk_sl], c_prev.astype(BF16), _NT) + sv[:, v_sl]
        den = (w_inter[h] * jnp.sum(qf[:, qk_sl] * n_prev, axis=1, keepdims=True)
               + jnp.sum(jnp.where(seg == h, sqk, 0.0), axis=1, keepdims=True))
        h_ref[:, v_sl] = num / jnp.maximum(jnp.abs(den), jnp.exp(-m_t[h]))

        m_new = jnp.maximum(b_last[h] + m_prev[h], seg_max(g_row, h))
        w_c = jnp.exp(b_last[h] + m_prev[h] - m_new)
        kw = kf[:, qk_sl] * jnp.exp(b_last[h] - b_col[h] + li_col[h] - m_new)
        c_scr[h] = w_c * c_prev + _dot(vb[:, v_sl], kw.astype(BF16), _TN)
        n_scr[h:h + 1, :] = w_c * n_prev + jnp.sum(kw, axis=0, keepdims=True)
        m_scr[h:h + 1, :] = m_new


def _mlstm(p, li_row, b_row, li_col, b_col, c0, n0, m0, *, nb, nc, L, row_block0, n_total, prev_out=None):
    aliased = prev_out is not None
    cps = _chunks_per_step(nc, MLSTM_CHUNKS_PER_STEP)
    assert row_block0 % cps == 0
    nc, row_block0, chunk, L = nc // cps, row_block0 // cps, L, L * cps
    rb = lambda b, c: row_block0 + b * nc + c
    in_specs = [pl.BlockSpec((L, C_QK), lambda b, c: (rb(b, c), 0)),
                pl.BlockSpec((L, C_QK), lambda b, c: (rb(b, c), 1)),
                pl.BlockSpec((L, C_V), lambda b, c: (rb(b, c), 2 * C_QK // C_V)),
                pl.BlockSpec((1, cps, 1, C_HEADS * chunk), lambda b, c: (b, c, 0, 0)),
                pl.BlockSpec((1, cps, 1, C_HEADS * chunk), lambda b, c: (b, c, 0, 0)),
                pl.BlockSpec((1, cps, chunk, C_HEADS), lambda b, c: (b, c, 0, 0)),
                pl.BlockSpec((1, cps, chunk, C_HEADS), lambda b, c: (b, c, 0, 0)),
                pl.BlockSpec((1, C_HEADS, C_DV, C_DQK), lambda b, c: (b, 0, 0, 0)),
                pl.BlockSpec((1, C_HEADS, C_DQK), lambda b, c: (b, 0, 0)),
                pl.BlockSpec((1, C_HEADS, 1), lambda b, c: (b, 0, 0))]
    args = [p, p, p, li_row, b_row, li_col, b_col, c0, n0, m0]
    io_alias = {}
    if aliased:
        in_specs.append(pl.BlockSpec(memory_space=pl.ANY))
        args.append(prev_out)
        io_alias = {len(args) - 1: 0}
    return pl.pallas_call(
        functools.partial(_mlstm_kernel, L=chunk, cps=cps, aliased=aliased),
        grid=(nb, nc),
        in_specs=in_specs,
        out_specs=[pl.BlockSpec((L, C_V), lambda b, c: (rb(b, c), 0)),
                   pl.BlockSpec((1, C_HEADS, C_DV, C_DQK), lambda b, c: (b, 0, 0, 0)),
                   pl.BlockSpec((1, C_HEADS, C_DQK), lambda b, c: (b, 0, 0)),
                   pl.BlockSpec((1, C_HEADS, 1), lambda b, c: (b, 0, 0))],
        out_shape=[jax.ShapeDtypeStruct((n_total, C_V), F32),
                   jax.ShapeDtypeStruct((nb, C_HEADS, C_DV, C_DQK), F32),
                   jax.ShapeDtypeStruct((nb, C_HEADS, C_DQK), F32),
                   jax.ShapeDtypeStruct((nb, C_HEADS, 1), F32)],
        scratch_shapes=[pltpu.VMEM((C_HEADS, C_DV, C_DQK), F32),
                        pltpu.VMEM((C_HEADS, C_DQK), F32),
                        pltpu.VMEM((C_HEADS, 1), F32)],
        input_output_aliases=io_alias,
        compiler_params=_params(("parallel", "arbitrary")),
        name="mlstm_L%d" % chunk,
    )(*args)


def _mixout_kernel(*refs, hd, act, has_b, precise, n_prev):
    refs = list(refs)
    h_ref, a_ref, r_ref = refs[:3]
    b_ref = refs[3] if has_b else None
    k = 4 if has_b else 3
    ga_ref, wo_ref, gf_ref, wrh_ref, wrl_ref, br_ref, cnt0_ref = refs[k:k + 7]
    h1_ref, xn_ref, info_ref, cnt_ref, cnt_scr = refs[k + 7 + n_prev:]
    cast = (lambda x: x) if precise else (lambda x: x.astype(BF16))
    a = a_ref[...]
    r = r_ref[...]
    gate = r * _sigmoid(r) if act == "silu" else _sigmoid(r)
    parts = []
    for hh in range(a.shape[1] // hd):
        sl = slice(hh * hd, (hh + 1) * hd)
        parts.append(cast(_rms(a[:, sl], ga_ref[...]) * gate[:, sl]))
    if has_b:
        parts.append(cast(b_ref[...]))
    cat = jnp.concatenate(parts, axis=1)
    h1 = h_ref[...] + (_dot3(cat, wo_ref[...]) if precise else _dot(cat, wo_ref[...]))
    h1_ref[...] = h1
    xn = _rms(h1, gf_ref[...])
    xn_ref[...] = _pack_bf16_pairs(xn)
    xh, xl = _split(xn)
    logits = _dot(xh, wrh_ref[...]) + _dot(xh, wrl_ref[...]) + _dot(xl, wrh_ref[...]) + br_ref[...]

    lane = lax.broadcasted_iota(jnp.int32, logits.shape, 1)
    lanef = lane.astype(F32)
    is_g = lane < N_GROUPS
    gl = jnp.where(is_g, logits, -jnp.inf)
    gmax = jnp.max(gl, axis=1, keepdims=True)
    gidx = jnp.min(jnp.where(gl == gmax, lanef, float(LANES)), axis=1, keepdims=True)
    wg = 1.0 / jnp.sum(jnp.where(is_g, jnp.exp(gl - gmax), 0.0), axis=1, keepdims=True)
    lo = N_GROUPS + N_EXP * gidx
    el = jnp.where((lanef >= lo) & (lanef < lo + N_EXP), logits, -jnp.inf)
    m1 = jnp.max(el, axis=1, keepdims=True)
    i1 = jnp.min(jnp.where(el == m1, lanef, float(LANES)), axis=1, keepdims=True)
    el2 = jnp.where(lanef == i1, -jnp.inf, el)
    m2 = jnp.max(el2, axis=1, keepdims=True)
    i2 = jnp.min(jnp.where(el2 == m2, lanef, float(LANES)), axis=1, keepdims=True)
    t = jnp.exp(m2 - m1)
    w1 = wg / (1.0 + t)
    w2 = wg * t / (1.0 + t)
    e1 = i1 - N_GROUPS
    e2 = i2 - N_GROUPS

    @pl.when(pl.program_id(0) == 0)
    def _():
        cnt_scr[...] = cnt0_ref[...]

    tm = logits.shape[0]
    pick = jnp.where((lanef == e1) | (lanef == e2), 1.0, 0.0)
    earlier = (lax.broadcasted_iota(jnp.int32, (tm, tm), 1) < lax.broadcasted_iota(jnp.int32, (tm, tm), 0))
    before = _dot(jnp.where(earlier, 1.0, 0.0).astype(BF16), pick.astype(BF16)) + cnt_scr[...]
    r1 = jnp.sum(jnp.where(lanef == e1, before, 0.0), axis=1, keepdims=True)
    r2 = jnp.sum(jnp.where(lanef == e2, before, 0.0), axis=1, keepdims=True)
    cnt_new = cnt_scr[...] + jnp.sum(pick, axis=0, keepdims=True)
    cnt_scr[...] = cnt_new
    cnt_ref[...] = cnt_new
    info_ref[...] = jnp.where(lane == 0, e1, jnp.where(lane == 1, e2, jnp.where(lane == 2, w1, jnp.where(
        lane == 3, w2, jnp.where(lane == 4, r1, jnp.where(lane == 5, r2, 0.0))))))


def _mixout(h, a, a_col, r, r_col, b, g_head, w_o, g_ffn, wr_hi, wr_lo, b_r, counts0, *, hd, act,
            row0, n_rows, prev=None):
    n, d = h.shape
    precise = prev is not None
    tm = _tile(n_rows, 512, SUBLANES)
    assert row0 % tm == 0
    blk0 = row0 // tm
    wa = w_o.shape[0] if b is None else w_o.shape[0] - B_W
    has_b = b is not None
    row = lambda i: (blk0 + i, 0)
    const = lambda i: (0, 0)
    in_specs = [pl.BlockSpec((tm, d), row),
                pl.BlockSpec((tm, wa), lambda i: (blk0 + i, a_col)),
                pl.BlockSpec((tm, wa), lambda i: (blk0 + i, r_col))]
    args = [h, a, r]
    if has_b:
        in_specs.append(pl.BlockSpec((tm, B_W), row))
        args.append(b)
    in_specs += [pl.BlockSpec((1, hd), const), pl.BlockSpec(w_o.shape, const), pl.BlockSpec((1, d), const),
                 pl.BlockSpec((d, LANES), const), pl.BlockSpec((d, LANES), const), pl.BlockSpec((1, LANES), const),
                 pl.BlockSpec((1, LANES), const)]
    args += [g_head.reshape(1, hd), w_o if precise else w_o.astype(BF16), g_ffn.reshape(1, d), wr_hi, wr_lo, b_r,
             counts0]
    io_alias = {}
    if precise:
        for k, arr in enumerate(prev):
            in_specs.append(pl.BlockSpec(memory_space=pl.ANY))
            args.append(arr)
            io_alias[len(args) - 1] = k
    return pl.pallas_call(
        functools.partial(_mixout_kernel, hd=hd, act=act, has_b=has_b, precise=precise, n_prev=len(io_alias)),
        grid=(n_rows // tm,),
        in_specs=in_specs,
        out_specs=[pl.BlockSpec((tm, d), row), pl.BlockSpec((tm, d // 2), row), pl.BlockSpec((tm, LANES), row),
                   pl.BlockSpec((1, LANES), const)],
        out_shape=[jax.ShapeDtypeStruct((n, d), F32), jax.ShapeDtypeStruct((n, d // 2), jnp.int32),
                   jax.ShapeDtypeStruct((n, LANES), F32), jax.ShapeDtypeStruct((1, LANES), F32)],
        scratch_shapes=[pltpu.VMEM((1, LANES), F32)],
        input_output_aliases=io_alias,
        compiler_params=_params(("arbitrary",)),
        name="mixout_" + act + ("_precise" if precise else ""),
    )(*args)


def _sc_gather(table, idx):
    r = idx.shape[0]
    w = table.shape[1]
    n_workers = SC_CORES * SC_SUBCORES
    per_worker = r // n_workers
    step = SC_GATHER_ROWS * SC_GATHER_BUFS
    assert r % n_workers == 0 and per_worker % step == 0, (r, n_workers, step)
    mesh = plsc.VectorSubcoreMesh(core_axis_name="c", subcore_axis_name="s")

    @functools.partial(
        pl.kernel, mesh=mesh,
        out_type=jax.ShapeDtypeStruct((r, w), table.dtype),
        scratch_types=[pltpu.VMEM((SC_GATHER_BUFS, SC_GATHER_ROWS), jnp.int32),
                       pltpu.VMEM((SC_GATHER_BUFS, SC_GATHER_ROWS, w), table.dtype),
                       pltpu.SemaphoreType.DMA((SC_GATHER_BUFS,)),
                       pltpu.SemaphoreType.DMA((SC_GATHER_BUFS,))],
    )
    def gather(table_hbm, idx_hbm, out_hbm, idx_v, rows_v, gather_sem, store_sem):
        worker = lax.axis_index("s") * SC_CORES + lax.axis_index("c")
        base = worker * per_worker

        @pl.loop(0, per_worker // step)
        def _(j):
            off = pl.multiple_of(base + j * step, step)
            rows = lambda b: pl.ds(off + b * SC_GATHER_ROWS, SC_GATHER_ROWS)
            gathers, stores = [], []
            for b in range(SC_GATHER_BUFS):
                pltpu.sync_copy(idx_hbm.at[rows(b)], idx_v.at[b])
                gathers.append(pltpu.async_copy(table_hbm.at[idx_v.at[b]], rows_v.at[b], gather_sem.at[b]))
            for b in range(SC_GATHER_BUFS):
                gathers[b].wait()
                stores.append(pltpu.async_copy(rows_v.at[b], out_hbm.at[rows(b)], store_sem.at[b]))
            for b in range(SC_GATHER_BUFS):
                stores[b].wait()

    return gather(table, idx)


def _sc_scatter_pairs(rows, slots, n_slots):
    n, w = rows.shape
    n_workers = SC_CORES * SC_SUBCORES
    per_worker = n // n_workers
    step = SC_SCATTER_ROWS * SC_SCATTER_BUFS
    assert n % n_workers == 0 and per_worker % step == 0, (n, n_workers, step)
    mesh = plsc.VectorSubcoreMesh(core_axis_name="c", subcore_axis_name="s")

    @functools.partial(
        pl.kernel, mesh=mesh,
        out_type=jax.ShapeDtypeStruct((n_slots, w), rows.dtype),
        scratch_types=[pltpu.VMEM((2 * SC_SCATTER_BUFS, SC_SCATTER_ROWS), jnp.int32),
                       pltpu.VMEM((SC_SCATTER_BUFS, SC_SCATTER_ROWS, w), rows.dtype),
                       pltpu.SemaphoreType.DMA((SC_SCATTER_BUFS,)),
                       pltpu.SemaphoreType.DMA((SC_SCATTER_BUFS,))],
    )
    def scatter(rows_hbm, slots_hbm, out_hbm, idx_v, rows_v, load_sem, store_sem):
        worker = lax.axis_index("s") * SC_CORES + lax.axis_index("c")
        base = worker * per_worker

        @pl.loop(0, per_worker // step)
        def _(j):
            off = pl.multiple_of(base + j * step, SC_SCATTER_ROWS)
            loads, stores = [], []
            for b in range(SC_SCATTER_BUFS):
                r0 = off + b * SC_SCATTER_ROWS
                pltpu.sync_copy(slots_hbm.at[pl.ds(r0, SC_SCATTER_ROWS)], idx_v.at[2 * b])
                pltpu.sync_copy(slots_hbm.at[pl.ds(n + r0, SC_SCATTER_ROWS)], idx_v.at[2 * b + 1])
                loads.append(pltpu.async_copy(rows_hbm.at[pl.ds(r0, SC_SCATTER_ROWS)], rows_v.at[b], load_sem.at[b]))
            for b in range(SC_SCATTER_BUFS):
                loads[b].wait()
                for k in range(2):
                    stores.append(pltpu.async_copy(rows_v.at[b], out_hbm.at[idx_v.at[2 * b + k]], store_sem.at[b]))
            for copy in stores:
                copy.wait()

    return scatter(rows, slots)


def _expert_kernel(te_ref, nu_ref, x_ref, wg_ref, wu_ref, wd_ref, y_ref, wgb, wub, wdb):
    i = pl.program_id(0)
    live = i < nu_ref[0]
    new_expert = (i == 0) | (te_ref[i] != te_ref[jnp.maximum(i - 1, 0)])

    @pl.when(live & new_expert)
    def _():
        wgb[...] = wg_ref[0].astype(BF16)
        wub[...] = wu_ref[0].astype(BF16)
        wdb[...] = wd_ref[0].astype(BF16)

    @pl.when(live)
    def _():
        x = _unpack_bf16_pairs(x_ref[...]).astype(BF16)
        g = _dot(x, wgb[...])
        u = _dot(x, wub[...])
        y_ref[...] = _pack_bf16_pairs(_dot((g * _sigmoid(g) * u).astype(BF16), wdb[...]))

    @pl.when(jnp.logical_not(live))
    def _():
        y_ref[...] = jnp.zeros_like(y_ref)


def _experts(xs, tile_expert, n_used, w_gate, w_up, w_down, *, tm):
    n_slots, dp = xs.shape
    d, f = w_gate.shape[-2:]
    grid_spec = pltpu.PrefetchScalarGridSpec(
        num_scalar_prefetch=2,
        grid=(n_slots // tm,),
        in_specs=[pl.BlockSpec((tm, dp), lambda i, te, nu: (i, 0)),
                  pl.BlockSpec((1, d, f), lambda i, te, nu: (te[i], 0, 0)),
                  pl.BlockSpec((1, d, f), lambda i, te, nu: (te[i], 0, 0)),
                  pl.BlockSpec((1, f, d), lambda i, te, nu: (te[i], 0, 0))],
        out_specs=pl.BlockSpec((tm, dp), lambda i, te, nu: (i, 0)),
        scratch_shapes=[pltpu.VMEM((d, f), BF16), pltpu.VMEM((d, f), BF16), pltpu.VMEM((f, d), BF16)],
    )
    return pl.pallas_call(
        _expert_kernel,
        grid_spec=grid_spec,
        out_shape=jax.ShapeDtypeStruct((n_slots, dp), jnp.int32),
        compiler_params=_params(("arbitrary",)),
        name="moe_experts",
    )(tile_expert, n_used, xs, w_gate, w_up, w_down)


def _combine_kernel(h_ref, info_ref, y0_ref, y1_ref, o_ref):
    o_ref[...] = h_ref[...] + (info_ref[:, 2:3] * _unpack_bf16_pairs(y0_ref[...])
                               + info_ref[:, 3:4] * _unpack_bf16_pairs(y1_ref[...]))


def _combine(h, yg, info):
    n, d = h.shape
    tm = _tile(n, 512, SUBLANES)
    nt = n // tm
    return pl.pallas_call(
        _combine_kernel,
        grid=(nt,),
        in_specs=[pl.BlockSpec((tm, d), lambda i: (i, 0)),
                  pl.BlockSpec((tm, LANES), lambda i: (i, 0)),
                  pl.BlockSpec((tm, d // 2), lambda i: (i, 0)),
                  pl.BlockSpec((tm, d // 2), lambda i: (nt + i, 0))],
        out_specs=pl.BlockSpec((tm, d), lambda i: (i, 0)),
        out_shape=jax.ShapeDtypeStruct((n, d), F32),
        compiler_params=_params(("parallel",)),
        name="moe_combine",
    )(h, info, yg, yg)


def _final_kernel(*refs, sub, prompt_steps):
    g_ref = refs[0]
    h_refs, info_refs, y0_refs, y1_refs = (refs[1 + k * sub:1 + (k + 1) * sub] for k in range(4))
    yp_ref, ys_ref = refs[1 + 4 * sub:]
    tin = h_refs[0].shape[0]
    i = pl.program_id(0)

    def store(o_ref):
        for k in range(sub):
            h2 = h_refs[k][...] + (info_refs[k][:, 2:3] * _unpack_bf16_pairs(y0_refs[k][...])
                                   + info_refs[k][:, 3:4] * _unpack_bf16_pairs(y1_refs[k][...]))
            o_ref[k * tin:(k + 1) * tin, :] = _rms(h2, g_ref[...])

    pl.when(i < prompt_steps)(functools.partial(store, yp_ref))
    pl.when(i >= prompt_steps)(functools.partial(store, ys_ref))


def _final_combine(h, yg, info, g, *, nb, t, n_sample):
    n, d = h.shape
    seq = t - LEAD
    tin = _tile(math.gcd(LEAD, n_sample, seq), LANES, SUBLANES)
    sub = FINAL_SUBTILES if seq % (FINAL_SUBTILES * tin) == 0 and n_sample % (FINAL_SUBTILES * tin) == 0 else 1
    tout = sub * tin
    steps_per_batch = seq // tout
    prompt_steps = nb * steps_per_batch
    tiles_per_batch, lead_tiles, n_tiles = t // tin, LEAD // tin, n // tin

    def in_tile(k, offset=0):
        def index(i):
            ip = jnp.minimum(i, prompt_steps - 1)
            prompt = (ip // steps_per_batch) * tiles_per_batch + lead_tiles + (ip % steps_per_batch) * sub + k
            sample = nb * tiles_per_batch + (i - prompt_steps) * sub + k
            return offset + jnp.where(i < prompt_steps, prompt, sample), 0
        return index

    in_specs = ([pl.BlockSpec((1, d), lambda i: (0, 0))]
                + [pl.BlockSpec((tin, d), in_tile(k)) for k in range(sub)]
                + [pl.BlockSpec((tin, LANES), in_tile(k)) for k in range(sub)]
                + [pl.BlockSpec((tin, d // 2), in_tile(k)) for k in range(sub)]
                + [pl.BlockSpec((tin, d // 2), in_tile(k, n_tiles)) for k in range(sub)])
    return pl.pallas_call(
        functools.partial(_final_kernel, sub=sub, prompt_steps=prompt_steps),
        grid=(prompt_steps + n_sample // tout,),
        in_specs=in_specs,
        out_specs=[pl.BlockSpec((tout, d), lambda i: (jnp.minimum(i, prompt_steps - 1), 0)),
                   pl.BlockSpec((tout, d), lambda i: (jnp.maximum(i - prompt_steps, 0), 0))],
        out_shape=[jax.ShapeDtypeStruct((nb * seq, d), F32), jax.ShapeDtypeStruct((n_sample, d), F32)],
        compiler_params=_params(("arbitrary",)),
        name="moe_combine_final",
    )(g.reshape(1, d), *([h] * sub + [info] * sub + [yg] * (2 * sub)))


def _moe(h1, xn, info, counts_f, w_gate, w_up, w_down, expert_base, *, final=None):
    n, d = h1.shape
    n_e = N_GROUPS * N_EXP
    tm = _tile(2 * n, MOE_TILE, SUBLANES)
    n_tiles = (2 * n) // tm + n_e
    n_slots = n_tiles * tm
    experts = jnp.arange(n_e, dtype=jnp.int32)
    counts = counts_f[0, :n_e].astype(jnp.int32)
    padded = ((counts + tm - 1) // tm) * tm
    pad_ends = jnp.cumsum(padded)
    pad_starts = pad_ends - padded
    n_used = (pad_ends[-1] // tm).astype(jnp.int32)
    eid = info[:, 0:2].astype(jnp.int32)
    rank = info[:, 4:6].astype(jnp.int32)
    slot_of_pick = jnp.sum(jnp.where(eid[..., None] == experts, pad_starts, 0), axis=-1) + rank
    slots = slot_of_pick.T.reshape(-1)
    tile_starts = jnp.arange(n_tiles, dtype=jnp.int32) * tm
    tile_expert = expert_base + jnp.minimum(jnp.sum(tile_starts[:, None] >= pad_ends[None, :], axis=1), n_e - 1)

    xs = _sc_scatter_pairs(xn, slots, n_slots)
    ys = _experts(xs, tile_expert.astype(jnp.int32), n_used.reshape(1), w_gate, w_up, w_down, tm=tm)
    yg = _sc_gather(ys, slots)
    if final is not None:
        return _final_combine(h1, yg, info, final["g"], nb=final["nb"], t=final["t"], n_sample=final["n_sample"])
    return _combine(h1, yg, info)


def _router_weights(w_rg, b_rg, w_re, b_re):
    d = w_rg.shape[0]
    pad = LANES - N_GROUPS - N_GROUPS * N_EXP
    w = jnp.concatenate([w_rg, w_re, jnp.zeros((d, pad), F32)], axis=1)
    b = jnp.concatenate([b_rg, b_re, jnp.zeros((pad,), F32)]).reshape(1, LANES)
    hi = w.astype(BF16)
    lo = (w - hi.astype(F32)).astype(BF16)
    return hi, lo, b


def _rows_to_lanes(x, nb, t, nch):
    return x.reshape(nb, t, nch).transpose(0, 2, 1).reshape(nb * nch, t)


def _even_layer(h, dims, g_mix, w_in, w_g2, b_g, b_f, g_a, w_o, state_gla, ck, cv, clf):
    nb, t, db, ds, npr, n = dims
    d = h.shape[1]
    qa, ka, va, ra, ga, qb, kb, vb, fb = jnp.split(
        w_in, [A_QK, 2 * A_QK, 2 * A_QK + A_V, 2 * A_QK + 2 * A_V, 2 * A_QK + 2 * A_V + A_RANK,
               2 * A_QK + 2 * A_V + A_RANK + B_W, 2 * A_QK + 2 * A_V + A_RANK + 2 * B_W,
               2 * A_QK + 2 * A_V + A_RANK + 3 * B_W], axis=1)
    w_packed = jnp.concatenate(
        [qa, ka, va, ra, qb, kb, vb, ga, fb, jnp.zeros((d, LANES - A_RANK - B_HEADS), F32)], axis=1)
    q_off = 2 * A_QK + 2 * A_V
    colscale = jnp.ones((1, MAIN_W), F32).at[:, q_off:q_off + B_W].set(B_DH ** -0.5 * LOG2E)
    p, pb, k_rows, v_rows = _proj(h, g_mix, w_packed, n_rows=npr, colscale=colscale, kv_col=q_off + B_W)
    p = _proj_precise(h, g_mix, w_packed, p, row0=npr)

    nh = B_HEADS
    fcol = MAIN_W + A_RANK
    zeros_col = lambda r: jnp.zeros((r, 1), F32)
    bias_row = lambda lanes: jnp.tile(b_f, lanes // nh).reshape(1, lanes)
    fb_p = p[:npr, fcol:fcol + nh].reshape(nb, t * nh)
    logf_p, f_p = _gate_scan(fb_p, bias_row(t * nh), zeros_col(nb), mode="fox", act_start=0,
                             valid_start=N_PAD * nh, valid_end=t * nh, seg=None, stride=nh)
    past = ck.shape[1]
    n_c = past * nh
    x_s = _pad_lanes(jnp.concatenate([clf.reshape(db, n_c), p[npr:, fcol:fcol + nh].reshape(db, ds * nh)], axis=1))
    logf_s, f_s = _gate_scan(x_s, bias_row(x_s.shape[1]), zeros_col(db), mode="fox", act_start=n_c,
                             valid_start=0, valid_end=n_c + ds * nh, seg=None, stride=nh)

    nc = t // CHUNK
    oa, s_p = _gla(p, jnp.zeros((nb, A_HEADS, A_DK, A_DV), F32), w_g2, b_g,
                   nb=nb, nc=nc, L=CHUNK, row_block0=0, lead_pad=N_PAD, n_total=n)
    oa, s_s = _gla(p, state_gla, w_g2, b_g, nb=db, nc=1, L=ds, row_block0=npr // ds, lead_pad=0,
                   n_total=n, prev_out=oa, precise=True)

    fk = (f_p * LOG2E).reshape(nb, t, nh).transpose(0, 2, 1).reshape(nb, nh, 1, t)
    ob = _flash(pb, fk, nb=nb, t=t, n_total=n)
    f_cache = (f_s[:, :n_c] * LOG2E).reshape(db, 1, n_c)
    f_new = (f_s[:, n_c:n_c + ds * nh] * LOG2E).reshape(db, ds, nh).transpose(0, 2, 1).reshape(db, 1, nh * ds)
    ob = _fox_sample(p, ck.reshape(db, n_c, B_DH), cv.reshape(db, n_c, B_DH), f_cache, f_new, ob,
                     nb=db, ds=ds, row_block0=npr // ds)

    kcol = q_off + B_W
    states = dict(
        s_p=s_p, s_s=s_s,
        k_p=k_rows.reshape(nb, t, B_HEADS, B_DH)[:, N_PAD:],
        v_p=v_rows.reshape(nb, t, B_HEADS, B_DH)[:, N_PAD:],
        f_p=logf_p.reshape(nb, t, nh)[:, N_PAD:],
        k_s=p[npr:, kcol:kcol + B_W].reshape(db, ds, B_HEADS, B_DH),
        v_s=p[npr:, kcol + B_W:kcol + 2 * B_W].reshape(db, ds, B_HEADS, B_DH),
        f_s=logf_s[:, n_c:n_c + ds * nh].reshape(db, ds, nh))
    return (oa, 0, p, (2 * A_QK + A_V) // A_V, ob, g_a, w_o), states


def _chunk_rows(x, nb, nch, nc, L):
    x = x[:, :nc * L].reshape(nb, nch, nc, L)
    return x.transpose(0, 2, 1, 3).reshape(nb, nc, 1, nch * L), x.transpose(0, 2, 3, 1)


def _odd_layer(h, dims, g_mix, w_in, b_gate, g_c, w_o, c0, n0, m0):
    nb, t, db, ds, npr, n = dims
    d = h.shape[1]
    w_packed = jnp.concatenate(
        [w_in, jnp.zeros((d, LANES - 2 * C_HEADS), F32)], axis=1)
    (p,) = _proj(h, g_mix, w_packed, n_rows=npr)
    p = _proj_precise(h, g_mix, w_packed, p, row0=npr)

    ng = 2 * C_HEADS
    isf = (jnp.arange(ng) >= C_HEADS).astype(F32)
    nc = t // CHUNK

    def gates(rows, nbatch, tt, valid_start, seg):
        x = _pad_lanes(_rows_to_lanes(rows, nbatch, tt, ng))
        val, cum = _gate_scan(x, jnp.tile(b_gate, nbatch).reshape(-1, 1), jnp.tile(isf, nbatch).reshape(-1, 1),
                              mode="mlstm", act_start=0, valid_start=valid_start, valid_end=tt, seg=seg)
        return val.reshape(nbatch, ng, -1), cum.reshape(nbatch, ng, -1)

    val_p, cum_p = gates(p[:npr, MAIN_W:MAIN_W + ng], nb, t, N_PAD, CHUNK)
    val_s, cum_s = gates(p[npr:, MAIN_W:MAIN_W + ng], db, ds, 0, ds)

    def chunked(val, cum, nbatch, ncs, L):
        li_row, li_col = _chunk_rows(val[:, :C_HEADS].reshape(nbatch * C_HEADS, -1), nbatch, C_HEADS, ncs, L)
        b_row, b_col = _chunk_rows(cum[:, C_HEADS:].reshape(nbatch * C_HEADS, -1), nbatch, C_HEADS, ncs, L)
        return li_row, b_row, li_col, b_col

    zc = jnp.zeros((nb, C_HEADS, C_DV, C_DQK), F32)
    zn = jnp.zeros((nb, C_HEADS, C_DQK), F32)
    zm = jnp.zeros((nb, C_HEADS, 1), F32)
    hm, c_p, n_p, m_p = _mlstm(p, *chunked(val_p, cum_p, nb, nc, CHUNK), zc, zn, zm,
                               nb=nb, nc=nc, L=CHUNK, row_block0=0, n_total=n)
    hm, c_s, n_s, m_s = _mlstm(p, *chunked(val_s, cum_s, db, 1, ds), c0, n0, m0.reshape(db, C_HEADS, 1),
                               nb=db, nc=1, L=ds, row_block0=npr // ds, n_total=n, prev_out=hm)
    states = dict(c_p=c_p, n_p=n_p, m_p=m_p.reshape(nb, C_HEADS), c_s=c_s, n_s=n_s, m_s=m_s.reshape(db, C_HEADS))
    return (hm, 0, p, (2 * C_QK + C_V) // C_V, None, g_c, w_o), states


def kernel(x_prompt, x_sample, state_gla, cache_fox_k, cache_fox_v, cache_fox_logf, state_mlstm_c, state_mlstm_n, state_mlstm_m, meta_tokens, norm_mix, norm_ffn, norm_final, w_in_even, w_gla_gate2, b_gla_gate, b_fox_f, g_gla_out, w_out_even, w_in_odd, b_mlstm_gate, g_mlstm_out, w_out_odd, w_router_group, b_router_group, w_router_expert, b_router_expert, w_exp_gate, w_exp_up, w_exp_down):
    nb, seq, d = x_prompt.shape
    db, ds, _ = x_sample.shape
    t = LEAD + seq
    npr, nsm = nb * t, db * ds
    n = npr + nsm
    dims = (nb, t, db, ds, npr, n)
    depth = norm_mix.shape[0]
    n_e = N_GROUPS * N_EXP
    f = w_exp_gate.shape[-1]

    pad_rows = jnp.zeros((N_PAD, d), F32)
    h = jnp.concatenate([piece for b in range(nb) for piece in (pad_rows, meta_tokens, x_prompt[b])]
                        + [x_sample.reshape(nsm, d)], axis=0)
    wg_all = w_exp_gate.reshape(depth * n_e, d, f)
    wu_all = w_exp_up.reshape(depth * n_e, d, f)
    wd_all = w_exp_down.reshape(depth * n_e, f, d)

    even, odd = [], []
    for l in range(depth):
        if l % 2 == 0:
            e = l // 2
            mix, st = _even_layer(h, dims, norm_mix[l], w_in_even[e], w_gla_gate2[e], b_gla_gate[e], b_fox_f[e],
                                  g_gla_out[e], w_out_even[e], state_gla[e], cache_fox_k[e], cache_fox_v[e],
                                  cache_fox_logf[e])
            even.append(st)
            hd, act = A_DV, "silu"
        else:
            o = l // 2
            mix, st = _odd_layer(h, dims, norm_mix[l], w_in_odd[o], b_mlstm_gate[o], g_mlstm_out[o], w_out_odd[o],
                                 state_mlstm_c[o], state_mlstm_n[o], state_mlstm_m[o])
            odd.append(st)
            hd, act = C_DV, "sigmoid"
        a, a_col, r, r_col, b, g_head, w_o = mix
        wr_hi, wr_lo, b_r = _router_weights(w_router_group[l], b_router_group[l], w_router_expert[l],
                                            b_router_expert[l])
        mix_args = (h, a, a_col, r, r_col, b, g_head, w_o, norm_ffn[l], wr_hi, wr_lo, b_r)
        h1, xn, info, counts = _mixout(*mix_args, jnp.zeros((1, LANES), F32), hd=hd, act=act, row0=0, n_rows=npr)
        h1, xn, info, counts = _mixout(*mix_args, counts, hd=hd, act=act, row0=npr, n_rows=nsm,
                                       prev=(h1, xn, info))
        last = l == depth - 1
        h = _moe(h1, xn, info, counts, wg_all, wu_all, wd_all, l * n_e,
                 final=dict(g=norm_final, nb=nb, t=t, n_sample=nsm) if last else None)

    y_prompt = h[0].reshape(nb, seq, d)
    y_sample = h[1].reshape(db, ds, d)
    stack = lambda sts, key: jnp.stack([s[key] for s in sts])
    return (y_prompt, y_sample,
            stack(even, "s_p"), stack(even, "k_p"), stack(even, "v_p"), stack(even, "f_p"),
            stack(odd, "c_p"), stack(odd, "n_p"), stack(odd, "m_p"),
            stack(even, "s_s"), stack(even, "k_s"), stack(even, "v_s"), stack(even, "f_s"),
            stack(odd, "c_s"), stack(odd, "n_s"), stack(odd, "m_s"))
```

```python
import functools
import math

import jax
import jax.numpy as jnp
from jax import lax
from jax.experimental import pallas as pl
from jax.experimental.pallas import tpu as pltpu
from jax.experimental.pallas import tpu_sc as plsc

F32 = jnp.float32
BF16 = jnp.bfloat16

CHUNK = 64
N_META = 16
LEAD = 128
N_PAD = LEAD - N_META
A_HEADS, A_DK, A_DV, A_RANK = 4, 64, 128, 16
A_GATE_NORM = 16.0
B_HEADS, B_DH = 4, 128
C_HEADS, C_DQK, C_DV = 4, 128, 256
GATE_CAP = 15.0
N_GROUPS, N_EXP = 4, 8
EPS = 1e-6
NEG = -1e30
LOG2E = 1.4426950408889634
A_QK = A_HEADS * A_DK
A_V = A_HEADS * A_DV
B_W = B_HEADS * B_DH
C_QK = C_HEADS * C_DQK
C_V = C_HEADS * C_DV

LANES = 128
SUBLANES = 8
VMEM_LIMIT_BYTES = 56 * 1024 * 1024
GLA_SUB = 16
SC_CORES, SC_SUBCORES = 2, 16
SC_GATHER_ROWS = 32
SC_GATHER_BUFS = 3
MOE_TILE = 1024
GLA_CHUNKS_PER_STEP = 10
MLSTM_CHUNKS_PER_STEP = 1
FINAL_SUBTILES = 4
SC_SCATTER_ROWS = 32
SC_SCATTER_BUFS = 3
FLASH_HEADS = 4
MAIN_W = 3072
PROJ_W = MAIN_W + LANES

_NT = (((1,), (1,)), ((), ()))
_TN = (((0,), (0,)), ((), ()))
_NN = (((1,), (0,)), ((), ()))


def _params(sem):
    return pltpu.CompilerParams(dimension_semantics=sem, vmem_limit_bytes=VMEM_LIMIT_BYTES)


def _tile(n, pref, mult):
    t = (min(pref, n) // mult) * mult
    while t > mult and n % t:
        t -= mult
    assert t >= mult and n % t == 0, (n, pref, mult)
    return t


def _dot(a, b, dims=_NN):
    return lax.dot_general(a, b, dims, preferred_element_type=F32)


def _split(x):
    hi = x.astype(BF16)
    lo = (x - hi.astype(F32)).astype(BF16)
    return hi, lo


def _dot3(a, b, dims=_NN):
    ah, al = _split(a)
    bh, bl = _split(b)
    return _dot(ah, bh, dims) + _dot(ah, bl, dims) + _dot(al, bh, dims)


def _log_sigmoid(x):
    return jnp.minimum(x, 0.0) - jnp.log1p(jnp.exp(-jnp.abs(x)))


def _sigmoid(x):
    return 1.0 / (1.0 + jnp.exp(-x))


def _rms(x, g):
    return x * lax.rsqrt(jnp.mean(x * x, axis=-1, keepdims=True) + EPS) * g


def _pack_bf16_pairs(x):
    w = x.shape[1] // 2
    hi = lax.bitcast_convert_type(x[:, :w].astype(BF16).astype(F32), jnp.int32)
    lo = lax.bitcast_convert_type(x[:, w:].astype(BF16).astype(F32), jnp.int32)
    return hi | lax.shift_right_logical(lo, 16)


def _unpack_bf16_pairs(p):
    hi = lax.bitcast_convert_type(p & jnp.int32(-65536), F32)
    lo = lax.bitcast_convert_type(lax.shift_left(p, 16), F32)
    return jnp.concatenate([hi, lo], axis=1)


def _cumsum_rows(x):
    n = x.shape[0]
    row = lax.broadcasted_iota(jnp.int32, x.shape, 0)
    s = 1
    while s < n:
        x = x + jnp.where(row >= s, pltpu.roll(x, s, axis=0), 0.0)
        s *= 2
    return x


def _proj_kernel(x_ref, g_ref, w_ref, *rest, col_chunk, kv_col):
    if kv_col is None:
        (o_ref,) = rest
    else:
        cs_ref, o_ref, ob_ref, *kv_refs = rest
    tm = x_ref.shape[0]
    xn = _rms(x_ref[...], g_ref[...]).astype(BF16)
    for c0 in range(0, PROJ_W, col_chunk):
        c1 = min(c0 + col_chunk, PROJ_W)
        y = _dot(xn, w_ref[:, c0:c1])
        o_ref[:, c0:c1] = y
        if kv_col is not None:
            if c0 < MAIN_W:
                m1 = min(c1, MAIN_W)
                ob_ref[:, c0:m1] = (y[:, :m1 - c0] * cs_ref[:, c0:m1]).astype(BF16)
            for g0 in range(c0, c1, LANES):
                rel = g0 - kv_col
                if 0 <= rel < 2 * B_W:
                    head = (rel % B_W) // B_DH
                    kv_refs[rel // B_W][pl.ds(head, tm, stride=B_HEADS), :] = y[:, g0 - c0:g0 - c0 + LANES]


def _proj_precise_kernel(x_ref, g_ref, w_ref, prev_ref, o_ref, *, col_chunk):
    del prev_ref
    xn = _rms(x_ref[...], g_ref[...])
    for c0 in range(0, PROJ_W, col_chunk):
        c1 = min(c0 + col_chunk, PROJ_W)
        o_ref[:, c0:c1] = _dot3(xn, w_ref[:, c0:c1])


def _proj(h, g, w_packed, *, n_rows, colscale=None, kv_col=None):
    n, d = h.shape
    tm = _tile(n_rows, 512, 16)
    in_specs = [pl.BlockSpec((tm, d), lambda i: (i, 0)),
                pl.BlockSpec((1, d), lambda i: (0, 0)),
                pl.BlockSpec((d, PROJ_W), lambda i: (0, 0))]
    args = [h, g.reshape(1, d), w_packed.astype(BF16)]
    out_specs = [pl.BlockSpec((tm, PROJ_W), lambda i: (i, 0))]
    out_shape = [jax.ShapeDtypeStruct((n, PROJ_W), F32)]
    if kv_col is not None:
        in_specs.append(pl.BlockSpec((1, MAIN_W), lambda i: (0, 0)))
        args.append(colscale)
        out_specs += [pl.BlockSpec((tm, MAIN_W), lambda i: (i, 0))] + [pl.BlockSpec((tm * B_HEADS, B_DH), lambda i: (i, 0))] * 2
        out_shape += ([jax.ShapeDtypeStruct((n_rows, MAIN_W), BF16)]
                      + [jax.ShapeDtypeStruct((n_rows * B_HEADS, B_DH), F32)] * 2)
    return pl.pallas_call(
        functools.partial(_proj_kernel, col_chunk=640, kv_col=kv_col),
        grid=(n_rows // tm,),
        in_specs=in_specs,
        out_specs=out_specs,
        out_shape=out_shape,
        compiler_params=_params(("parallel",)),
        name="proj",
    )(*args)


def _proj_precise(h, g, w_packed, prev, *, row0):
    n, d = h.shape
    tm = _tile(n - row0, 512, SUBLANES)
    assert row0 % tm == 0
    return pl.pallas_call(
        functools.partial(_proj_precise_kernel, col_chunk=640),
        grid=((n - row0) // tm,),
        in_specs=[pl.BlockSpec((tm, d), lambda i: (row0 // tm + i, 0)),
                  pl.BlockSpec((1, d), lambda i: (0, 0)),
                  pl.BlockSpec((d, PROJ_W), lambda i: (0, 0)),
                  pl.BlockSpec(memory_space=pl.ANY)],
        out_specs=pl.BlockSpec((tm, PROJ_W), lambda i: (row0 // tm + i, 0)),
        out_shape=jax.ShapeDtypeStruct((n, PROJ_W), F32),
        input_output_aliases={3: 0},
        compiler_params=_params(("parallel",)),
        name="proj_precise",
    )(h, g.reshape(1, d), w_packed, prev)


def _gate_scan_kernel(x_ref, bias_ref, isf_ref, val_ref, cum_ref, *, mode, act_start, valid_start, valid_end, seg, stride):
    x = x_ref[...]
    lane = lax.broadcasted_iota(jnp.int32, x.shape, 1)
    valid = (lane >= valid_start) & (lane < valid_end)
    if mode == "fox":
        val = jnp.where(lane >= act_start, _log_sigmoid(x + bias_ref[...]), x)
        val = jnp.where(valid, val, 0.0)
        add = val
    else:
        gate = GATE_CAP * jnp.tanh((x + bias_ref[...]) / GATE_CAP)
        isf = isf_ref[...] > 0.5
        val = jnp.where(isf, jnp.where(valid, _log_sigmoid(gate), 0.0),
                        jnp.where(valid, gate, -jnp.inf))
        add = jnp.where(isf, val, 0.0)
    val_ref[...] = val
    n = x.shape[1]
    pos = lane if seg is None else lane % seg
    limit = n if seg is None else seg
    s = stride
    while s < limit:
        add = add + jnp.where(pos >= s, pltpu.roll(add, s, axis=1), 0.0)
        s *= 2
    cum_ref[...] = add


def _gate_scan(x, bias, isf, *, mode, act_start, valid_start, valid_end, seg, stride=1):
    r, n = x.shape
    full = lambda shape: pl.BlockSpec(shape, lambda i: (0,) * len(shape))
    return pl.pallas_call(
        functools.partial(_gate_scan_kernel, mode=mode, act_start=act_start,
                          valid_start=valid_start, valid_end=valid_end, seg=seg, stride=stride),
        grid=(1,),
        in_specs=[full((r, n)), full(bias.shape), full((r, 1))],
        out_specs=[full((r, n)), full((r, n))],
        out_shape=[jax.ShapeDtypeStruct((r, n), F32)] * 2,
        compiler_params=_params(("arbitrary",)),
        name="gate_scan_" + mode,
    )(x, bias, isf)


def _pad_lanes(x):
    n = x.shape[-1]
    m = -(-n // LANES) * LANES
    return x if m == n else jnp.pad(x, ((0, 0), (0, m - n)))


def _gla_kernel(qk_ref, v_ref, sm_ref, s0_ref, wg2_ref, wg2t_ref, bgr_ref, bgc_ref, *rest,
                L, cps, sub, lead_pad, aliased, precise):
    if aliased:
        rest = rest[1:]
    o_ref, sout_ref, s_scr = rest
    step = pl.program_id(1)
    for j in range(cps):
        rows = pl.ds(j * L, L)
        _gla_chunk(qk_ref.at[rows], v_ref.at[rows], sm_ref.at[rows], s0_ref, wg2_ref, wg2t_ref, bgr_ref, bgc_ref,
                   o_ref.at[rows], sout_ref, s_scr, c=step * cps + j, first=(j == 0), last=(j == cps - 1),
                   L=L, sub=sub, lead_pad=lead_pad, precise=precise)


def _gla_chunk(qk_ref, v_ref, sm_ref, s0_ref, wg2_ref, wg2t_ref, bgr_ref, bgc_ref, o_ref, sout_ref, s_scr, *,
               c, first, last, L, sub, lead_pad, precise):
    nh, dk, dv = A_HEADS, A_DK, A_DV
    cast = (lambda x: x) if precise else (lambda x: x.astype(BF16))
    mm = _dot3 if precise else _dot

    if first:
        @pl.when(pl.program_id(1) == 0)
        def _():
            s_scr[...] = jnp.zeros_like(s_scr)
            for h in range(nh):
                s_scr[h * dk:(h + 1) * dk, h * dv:(h + 1) * dv] = s0_ref[0, h]

    qk = qk_ref[...]
    q = qk[:, :A_QK] * (A_DK ** -0.5)
    k = qk[:, A_QK:]
    v = v_ref[...]
    ga = sm_ref[:, :A_RANK]
    row = lax.broadcasted_iota(jnp.int32, (L, 1), 0)
    valid = (c * L + row) >= lead_pad
    z = _dot3(ga, wg2_ref[...]) + bgr_ref[...]
    loga = jnp.where(valid, _log_sigmoid(z) / A_GATE_NORM, 0.0)
    k = jnp.where(valid, k, 0.0)
    b = _cumsum_rows(loga)
    b_last = b[L - 1:L, :]
    lane_t = lax.broadcasted_iota(jnp.int32, (1, L), 1)
    zt = _dot3(wg2t_ref[...], ga, _NT) + bgc_ref[...]
    logat = jnp.where((c * L + lane_t) >= lead_pad, _log_sigmoid(zt) / A_GATE_NORM, 0.0)
    b_last_col = jnp.sum(logat, axis=1, keepdims=True)

    qhead = lax.broadcasted_iota(jnp.int32, (1, A_QK), 1) // dk
    vhead = lax.broadcasted_iota(jnp.int32, (1, A_V), 1) // dv
    vb = cast(v)
    zero_b = jnp.zeros((), vb.dtype)
    v_bd = jnp.concatenate([jnp.where(vhead == h, vb, zero_b) for h in range(nh)], axis=0)

    rows_all = lax.broadcasted_iota(jnp.int32, (L, 1), 0)
    a_rows = []
    for i in range(L // sub):
        r0 = i * sub
        ci = jnp.zeros((1, A_QK), F32) if i == 0 else b[r0 - 1:r0, :]
        qt = cast(q[r0:r0 + sub] * jnp.exp(b[r0:r0 + sub] - ci))
        kt = cast(jnp.where(rows_all < r0 + sub, k * jnp.exp(ci - b), 0.0))
        k_stack = jnp.concatenate([jnp.where(qhead == h, kt, zero_b) for h in range(nh)], axis=0)
        a_rows.append(mm(qt, k_stack, _NT))
    a = a_rows[0] if len(a_rows) == 1 else jnp.concatenate(a_rows, axis=0)
    t_idx = lax.broadcasted_iota(jnp.int32, (L, nh * L), 0)
    s_idx = lax.broadcasted_iota(jnp.int32, (L, nh * L), 1) % L
    a = jnp.where(s_idx <= t_idx, a, 0.0)
    o_intra = mm(cast(a), v_bd)

    s_full = s_scr[...]
    o_inter = mm(cast(q * jnp.exp(b)), cast(s_full))
    o_ref[...] = o_inter + o_intra

    k_hat = cast(k * jnp.exp(b_last - b))
    upd = mm(k_hat, vb, _TN)
    khead_col = lax.broadcasted_iota(jnp.int32, (A_QK, 1), 0) // dk
    s_new = jnp.exp(b_last_col) * s_full + jnp.where(khead_col == vhead, upd, 0.0)
    s_scr[...] = s_new

    if last:
        @pl.when(pl.program_id(1) == pl.num_programs(1) - 1)
        def _():
            for h in range(nh):
                sout_ref[0, h] = s_new[h * dk:(h + 1) * dk, h * dv:(h + 1) * dv]


def _chunks_per_step(nc, want):
    return next(k for k in (want, 2, 1) if k <= want and nc % k == 0)


def _gla(p, s0, wg2, bg, *, nb, nc, L, row_block0, lead_pad, n_total, prev_out=None, precise=False):
    aliased = prev_out is not None
    cps = _chunks_per_step(nc, GLA_CHUNKS_PER_STEP)
    assert row_block0 % cps == 0
    nc, L, row_block0, chunk = nc // cps, L * cps, row_block0 // cps, L
    rb = lambda b, c: row_block0 + b * nc + c
    in_specs = [pl.BlockSpec((L, 2 * A_QK), lambda b, c: (rb(b, c), 0)),
                pl.BlockSpec((L, A_V), lambda b, c: (rb(b, c), 2 * A_QK // A_V)),
                pl.BlockSpec((L, LANES), lambda b, c: (rb(b, c), MAIN_W // LANES)),
                pl.BlockSpec((1, A_HEADS, A_DK, A_DV), lambda b, c: (b, 0, 0, 0)),
                pl.BlockSpec((A_RANK, A_QK), lambda b, c: (0, 0)),
                pl.BlockSpec((A_QK, A_RANK), lambda b, c: (0, 0)),
                pl.BlockSpec((1, A_QK), lambda b, c: (0, 0)),
                pl.BlockSpec((A_QK, 1), lambda b, c: (0, 0))]
    args = [p, p, p, s0, wg2, wg2.T, bg.reshape(1, A_QK), bg.reshape(A_QK, 1)]
    io_alias = {}
    if aliased:
        in_specs.append(pl.BlockSpec(memory_space=pl.ANY))
        args.append(prev_out)
        io_alias = {len(args) - 1: 0}
    return pl.pallas_call(
        functools.partial(_gla_kernel, L=chunk, cps=cps, sub=min(GLA_SUB, chunk), lead_pad=lead_pad,
                          aliased=aliased, precise=precise),
        grid=(nb, nc),
        in_specs=in_specs,
        out_specs=[pl.BlockSpec((L, A_V), lambda b, c: (rb(b, c), 0)),
                   pl.BlockSpec((1, A_HEADS, A_DK, A_DV), lambda b, c: (b, 0, 0, 0))],
        out_shape=[jax.ShapeDtypeStruct((n_total, A_V), F32),
                   jax.ShapeDtypeStruct((nb, A_HEADS, A_DK, A_DV), F32)],
        scratch_shapes=[pltpu.VMEM((A_QK, A_V), F32)],
        input_output_aliases=io_alias,
        compiler_params=_params(("parallel", "arbitrary")),
        name="gla_L%d" % chunk,
    )(*args)


def _flash_kernel(qi_ref, kj_ref, q_ref, k_ref, v_ref, fk_ref, o_ref, m_scr, l_scr, acc_scr, *, blk, lead_pad):
    step = pl.program_id(2)
    i = qi_ref[step]
    j = kj_ref[step]

    @pl.when(j == 0)
    def _():
        m_scr[...] = jnp.full_like(m_scr, -jnp.inf)
        l_scr[...] = jnp.zeros_like(l_scr)
        acc_scr[...] = jnp.zeros_like(acc_scr)

    def update(masked):
        for g in range(FLASH_HEADS):
            sl = slice(g * B_DH, (g + 1) * B_DH)
            s = _dot(q_ref[:, sl], k_ref[:, sl], _NT) - fk_ref[0, g]
            if masked:
                qpos = i * blk + lax.broadcasted_iota(jnp.int32, (blk, blk), 0)
                kpos = j * blk + lax.broadcasted_iota(jnp.int32, (blk, blk), 1)
                s = jnp.where((kpos <= qpos) & (kpos >= lead_pad), s, NEG)
            m_prev = m_scr[g]
            m_new = jnp.maximum(m_prev, jnp.max(s, axis=1, keepdims=True))
            alpha = jnp.exp2(m_prev - m_new)
            p = jnp.exp2(s - m_new)
            l_scr[g] = alpha * l_scr[g] + jnp.sum(p, axis=1, keepdims=True)
            acc_scr[g] = alpha * acc_scr[g] + _dot(p.astype(BF16), v_ref[:, sl])
            m_scr[g] = m_new

    edge = (j == i) | (j == 0)
    pl.when(edge)(functools.partial(update, True))
    pl.when(jnp.logical_not(edge))(functools.partial(update, False))

    @pl.when(j == i)
    def _():
        for g in range(FLASH_HEADS):
            o_ref[:, g * B_DH:(g + 1) * B_DH] = acc_scr[g] / l_scr[g]


def _flash(pb, fk, *, nb, t, n_total):
    blk = _tile(t, 640, LANES)
    nq = t // blk
    hg = FLASH_HEADS
    w = hg * B_DH
    pairs = [(i, j) for i in range(nq) for j in range(i + 1)]
    qi = jnp.asarray([p[0] for p in pairs], jnp.int32)
    kj = jnp.asarray([p[1] for p in pairs], jnp.int32)
    qc, kc, vc = (A_QK * 2 + A_V * 2) // w, (A_QK * 2 + A_V * 2 + B_W) // w, (A_QK * 2 + A_V * 2 + 2 * B_W) // w
    grid_spec = pltpu.PrefetchScalarGridSpec(
        num_scalar_prefetch=2,
        grid=(nb, B_HEADS // hg, len(pairs)),
        in_specs=[pl.BlockSpec((blk, w), lambda b, h, s, qi, kj: (b * nq + qi[s], qc + h)),
                  pl.BlockSpec((blk, w), lambda b, h, s, qi, kj: (b * nq + kj[s], kc + h)),
                  pl.BlockSpec((blk, w), lambda b, h, s, qi, kj: (b * nq + kj[s], vc + h)),
                  pl.BlockSpec((1, hg, 1, blk), lambda b, h, s, qi, kj: (b, h, 0, kj[s]))],
        out_specs=pl.BlockSpec((blk, w), lambda b, h, s, qi, kj: (b * nq + qi[s], h)),
        scratch_shapes=[pltpu.VMEM((hg, blk, 1), F32), pltpu.VMEM((hg, blk, 1), F32),
                        pltpu.VMEM((hg, blk, B_DH), F32)],
    )
    return pl.pallas_call(
        functools.partial(_flash_kernel, blk=blk, lead_pad=N_PAD),
        grid_spec=grid_spec,
        out_shape=jax.ShapeDtypeStruct((n_total, B_W), F32),
        compiler_params=_params(("parallel", "parallel", "arbitrary")),
        name="fox_flash",
    )(qi, kj, pb, pb, pb, fk)


def _fox_sample_kernel(q_ref, kn_ref, vn_ref, kc_ref, vc_ref, fc_ref, fn_ref, prev_ref, o_ref, *, ds):
    del prev_ref
    nh = B_HEADS
    stack = lambda ref: jnp.concatenate([ref[:, h * B_DH:(h + 1) * B_DH] for h in range(nh)], axis=0)
    q = stack(q_ref) * (B_DH ** -0.5 * LOG2E)
    rows = nh * ds
    n_c = kc_ref.shape[1]
    qh_c = lax.broadcasted_iota(jnp.int32, (rows, n_c), 0) // ds
    kh_c = lax.broadcasted_iota(jnp.int32, (rows, n_c), 1) % nh
    s_c = _dot3(q, kc_ref[0], _NT) - fc_ref[0]
    s_c = jnp.where(qh_c == kh_c, s_c, NEG)
    r_i = lax.broadcasted_iota(jnp.int32, (rows, rows), 0)
    c_i = lax.broadcasted_iota(jnp.int32, (rows, rows), 1)
    s_n = _dot3(q, stack(kn_ref), _NT) - fn_ref[0]
    s_n = jnp.where((r_i // ds == c_i // ds) & (c_i % ds <= r_i % ds), s_n, NEG)
    m = jnp.maximum(jnp.max(s_c, axis=1, keepdims=True), jnp.max(s_n, axis=1, keepdims=True))
    p_c = jnp.exp2(s_c - m)
    p_n = jnp.exp2(s_n - m)
    l = jnp.sum(p_c, axis=1, keepdims=True) + jnp.sum(p_n, axis=1, keepdims=True)
    o = (_dot3(p_c, vc_ref[0]) + _dot3(p_n, stack(vn_ref))) / l
    for h in range(nh):
        o_ref[:, h * B_DH:(h + 1) * B_DH] = o[h * ds:(h + 1) * ds]


def _fox_sample(pb, kc, vc, f_cache, f_new, prev_out, *, nb, ds, row_block0):
    n_c = kc.shape[1]
    base = (A_QK * 2 + A_V * 2) // B_W
    rb = lambda b: row_block0 + b
    return pl.pallas_call(
        functools.partial(_fox_sample_kernel, ds=ds),
        grid=(nb,),
        in_specs=[pl.BlockSpec((ds, B_W), lambda b: (rb(b), base)),
                  pl.BlockSpec((ds, B_W), lambda b: (rb(b), base + 1)),
                  pl.BlockSpec((ds, B_W), lambda b: (rb(b), base + 2)),
                  pl.BlockSpec((1, n_c, B_DH), lambda b: (b, 0, 0)),
                  pl.BlockSpec((1, n_c, B_DH), lambda b: (b, 0, 0)),
                  pl.BlockSpec((1, 1, n_c), lambda b: (b, 0, 0)),
                  pl.BlockSpec((1, 1, B_HEADS * ds), lambda b: (b, 0, 0)),
                  pl.BlockSpec(memory_space=pl.ANY)],
        out_specs=pl.BlockSpec((ds, B_W), lambda b: (rb(b), 0)),
        out_shape=jax.ShapeDtypeStruct(prev_out.shape, F32),
        input_output_aliases={7: 0},
        compiler_params=_params(("parallel",)),
        name="fox_sample",
    )(pb, pb, pb, kc, vc, f_cache, f_new, prev_out)


def _mlstm_kernel(q_ref, k_ref, v_ref, lir_ref, br_ref, lic_ref, bc_ref, c0_ref, n0_ref, m0_ref, *rest,
                  L, cps, aliased):
    if aliased:
        rest = rest[1:]
    h_ref, cout_ref, nout_ref, mout_ref, c_scr, n_scr, m_scr = rest
    step = pl.program_id(1)

    @pl.when(step == 0)
    def _():
        c_scr[...] = c0_ref[0]
        n_scr[...] = n0_ref[0]
        m_scr[...] = m0_ref[0]

    for j in range(cps):
        rows = pl.ds(j * L, L)
        _mlstm_chunk(q_ref.at[rows], k_ref.at[rows], v_ref.at[rows], lir_ref.at[0, j], br_ref.at[0, j],
                     lic_ref.at[0, j], bc_ref.at[0, j], h_ref.at[rows], c_scr, n_scr, m_scr, L=L)

    @pl.when(step == pl.num_programs(1) - 1)
    def _():
        cout_ref[0] = c_scr[...]
        nout_ref[0] = n_scr[...]
        mout_ref[0] = m_scr[...]


def _mlstm_chunk(q_ref, k_ref, v_ref, lir_ref, br_ref, lic_ref, bc_ref, h_ref, c_scr, n_scr, m_scr, *, L):
    nh = C_HEADS
    hl = nh * L
    seg = lax.broadcasted_iota(jnp.int32, (1, hl), 1) // L
    causal = (lax.broadcasted_iota(jnp.int32, (L, hl), 1) % L) <= lax.broadcasted_iota(jnp.int32, (L, hl), 0)

    def per_head(vals):
        out = vals[0]
        for h in range(1, nh):
            out = jnp.where(seg == h, vals[h], out)
        return out

    seg_max = lambda x, h: jnp.max(jnp.where(seg == h, x, -jnp.inf), axis=1, keepdims=True)

    qf = q_ref[...]
    kf = k_ref[...] * (C_DQK ** -0.5)
    qb = qf.astype(BF16)
    kb = kf.astype(BF16)
    vb = v_ref[...].astype(BF16)
    b_row = br_ref[...]
    li_row = lir_ref[...]
    b_col = [bc_ref[:, h:h + 1] for h in range(nh)]
    li_col = [lic_ref[:, h:h + 1] for h in range(nh)]
    m_prev = [m_scr[h:h + 1, :] for h in range(nh)]

    d = jnp.where(causal, per_head(b_col) - b_row + li_row, -jnp.inf)
    inter = [b_col[h] + m_prev[h] for h in range(nh)]
    m_t = [jnp.maximum(inter[h], seg_max(d, h)) for h in range(nh)]
    pm = jnp.exp(d - per_head(m_t))
    w_inter = [jnp.exp(inter[h] - m_t[h]) for h in range(nh)]

    khead = lax.broadcasted_iota(jnp.int32, (1, C_QK), 1) // C_DQK
    vhead = lax.broadcasted_iota(jnp.int32, (1, C_V), 1) // C_DV
    zero_b = jnp.zeros((), BF16)
    k_stack = jnp.concatenate([jnp.where(khead == h, kb, zero_b) for h in range(nh)], axis=0)
    v_bd = jnp.concatenate([jnp.where(vhead == h, vb, zero_b) for h in range(nh)], axis=0)
    sqk = _dot(qb, k_stack, _NT) * pm
    sv = _dot(sqk.astype(BF16), v_bd)

    b_last = [b_row[:, h * L + L - 1:h * L + L] for h in range(nh)]
    g_row = per_head(b_last) - b_row + li_row
    for h in range(nh):
        qk_sl = slice(h * C_DQK, (h + 1) * C_DQK)
        v_sl = slice(h * C_DV, (h + 1) * C_DV)
        c_prev = c_scr[h]
        n_prev = n_scr[h:h + 1, :]
        num = w_inter[h] * _dot(qb[:, qk_sl], c_prev.astype(BF16), _NT) + sv[:, v_sl]
        den = (w_inter[h] * jnp.sum(qf[:, qk_sl] * n_prev, axis=1, keepdims=True)
               + jnp.sum(jnp.where(seg == h, sqk, 0.0), axis=1, keepdims=True))
        h_ref[:, v_sl] = num / jnp.maximum(jnp.abs(den), jnp.exp(-m_t[h]))

        m_new = jnp.maximum(b_last[h] + m_prev[h], seg_max(g_row, h))
        w_c = jnp.exp(b_last[h] + m_prev[h] - m_new)
        kw = kf[:, qk_sl] * jnp.exp(b_last[h] - b_col[h] + li_col[h] - m_new)
        c_scr[h] = w_c * c_prev + _dot(vb[:, v_sl], kw.astype(BF16), _TN)
        n_scr[h:h + 1, :] = w_c * n_prev + jnp.sum(kw, axis=0, keepdims=True)
        m_scr[h:h + 1, :] = m_new


def _mlstm(p, li_row, b_row, li_col, b_col, c0, n0, m0, *, nb, nc, L, row_block0, n_total, prev_out=None):
    aliased = prev_out is not None
    cps = _chunks_per_step(nc, MLSTM_CHUNKS_PER_STEP)
    assert row_block0 % cps == 0
    nc, row_block0, chunk, L = nc // cps, row_block0 // cps, L, L * cps
    rb = lambda b, c: row_block0 + b * nc + c
    in_specs = [pl.BlockSpec((L, C_QK), lambda b, c: (rb(b, c), 0)),
                pl.BlockSpec((L, C_QK), lambda b, c: (rb(b, c), 1)),
                pl.BlockSpec((L, C_V), lambda b, c: (rb(b, c), 2 * C_QK // C_V)),
                pl.BlockSpec((1, cps, 1, C_HEADS * chunk), lambda b, c: (b, c, 0, 0)),
                pl.BlockSpec((1, cps, 1, C_HEADS * chunk), lambda b, c: (b, c, 0, 0)),
                pl.BlockSpec((1, cps, chunk, C_HEADS), lambda b, c: (b, c, 0, 0)),
                pl.BlockSpec((1, cps, chunk, C_HEADS), lambda b, c: (b, c, 0, 0)),
                pl.BlockSpec((1, C_HEADS, C_DV, C_DQK), lambda b, c: (b, 0, 0, 0)),
                pl.BlockSpec((1, C_HEADS, C_DQK), lambda b, c: (b, 0, 0)),
                pl.BlockSpec((1, C_HEADS, 1), lambda b, c: (b, 0, 0))]
    args = [p, p, p, li_row, b_row, li_col, b_col, c0, n0, m0]
    io_alias = {}
    if aliased:
        in_specs.append(pl.BlockSpec(memory_space=pl.ANY))
        args.append(prev_out)
        io_alias = {len(args) - 1: 0}
    return pl.pallas_call(
        functools.partial(_mlstm_kernel, L=chunk, cps=cps, aliased=aliased),
        grid=(nb, nc),
        in_specs=in_specs,
        out_specs=[pl.BlockSpec((L, C_V), lambda b, c: (rb(b, c), 0)),
                   pl.BlockSpec((1, C_HEADS, C_DV, C_DQK), lambda b, c: (b, 0, 0, 0)),
                   pl.BlockSpec((1, C_HEADS, C_DQK), lambda b, c: (b, 0, 0)),
                   pl.BlockSpec((1, C_HEADS, 1), lambda b, c: (b, 0, 0))],
        out_shape=[jax.ShapeDtypeStruct((n_total, C_V), F32),
                   jax.ShapeDtypeStruct((nb, C_HEADS, C_DV, C_DQK), F32),
                   jax.ShapeDtypeStruct((nb, C_HEADS, C_DQK), F32),
                   jax.ShapeDtypeStruct((nb, C_HEADS, 1), F32)],
        scratch_shapes=[pltpu.VMEM((C_HEADS, C_DV, C_DQK), F32),
                        pltpu.VMEM((C_HEADS, C_DQK), F32),
                        pltpu.VMEM((C_HEADS, 1), F32)],
        input_output_aliases=io_alias,
        compiler_params=_params(("parallel", "arbitrary")),
        name="mlstm_L%d" % chunk,
    )(*args)


def _mixout_kernel(*refs, hd, act, has_b, precise, n_prev):
    refs = list(refs)
    h_ref, a_ref, r_ref = refs[:3]
    b_ref = refs[3] if has_b else None
    k = 4 if has_b else 3
    ga_ref, wo_ref, gf_ref, wrh_ref, wrl_ref, br_ref, cnt0_ref = refs[k:k + 7]
    h1_ref, xn_ref, info_ref, cnt_ref, cnt_scr = refs[k + 7 + n_prev:]
    cast = (lambda x: x) if precise else (lambda x: x.astype(BF16))
    a = a_ref[...]
    r = r_ref[...]
    gate = r * _sigmoid(r) if act == "silu" else _sigmoid(r)
    parts = []
    for hh in range(a.shape[1] // hd):
        sl = slice(hh * hd, (hh + 1) * hd)
        parts.append(cast(_rms(a[:, sl], ga_ref[...]) * gate[:, sl]))
    if has_b:
        parts.append(cast(b_ref[...]))
    cat = jnp.concatenate(parts, axis=1)
    h1 = h_ref[...] + (_dot3(cat, wo_ref[...]) if precise else _dot(cat, wo_ref[...]))
    h1_ref[...] = h1
    xn = _rms(h1, gf_ref[...])
    xn_ref[...] = _pack_bf16_pairs(xn)
    xh, xl = _split(xn)
    logits = _dot(xh, wrh_ref[...]) + _dot(xh, wrl_ref[...]) + _dot(xl, wrh_ref[...]) + br_ref[...]

    lane = lax.broadcasted_iota(jnp.int32, logits.shape, 1)
    lanef = lane.astype(F32)
    is_g = lane < N_GROUPS
    gl = jnp.where(is_g, logits, -jnp.inf)
    gmax = jnp.max(gl, axis=1, keepdims=True)
    gidx = jnp.min(jnp.where(gl == gmax, lanef, float(LANES)), axis=1, keepdims=True)
    wg = 1.0 / jnp.sum(jnp.where(is_g, jnp.exp(gl - gmax), 0.0), axis=1, keepdims=True)
    lo = N_GROUPS + N_EXP * gidx
    el = jnp.where((lanef >= lo) & (lanef < lo + N_EXP), logits, -jnp.inf)
    m1 = jnp.max(el, axis=1, keepdims=True)
    i1 = jnp.min(jnp.where(el == m1, lanef, float(LANES)), axis=1, keepdims=True)
    el2 = jnp.where(lanef == i1, -jnp.inf, el)
    m2 = jnp.max(el2, axis=1, keepdims=True)
    i2 = jnp.min(jnp.where(el2 == m2, lanef, float(LANES)), axis=1, keepdims=True)
    t = jnp.exp(m2 - m1)
    w1 = wg / (1.0 + t)
    w2 = wg * t / (1.0 + t)
    e1 = i1 - N_GROUPS
    e2 = i2 - N_GROUPS

    @pl.when(pl.program_id(0) == 0)
    def _():
        cnt_scr[...] = cnt0_ref[...]

    tm = logits.shape[0]
    pick = jnp.where((lanef == e1) | (lanef == e2), 1.0, 0.0)
    earlier = (lax.broadcasted_iota(jnp.int32, (tm, tm), 1) < lax.broadcasted_iota(jnp.int32, (tm, tm), 0))
    before = _dot(jnp.where(earlier, 1.0, 0.0).astype(BF16), pick.astype(BF16)) + cnt_scr[...]
    r1 = jnp.sum(jnp.where(lanef == e1, before, 0.0), axis=1, keepdims=True)
    r2 = jnp.sum(jnp.where(lanef == e2, before, 0.0), axis=1, keepdims=True)
    cnt_new = cnt_scr[...] + jnp.sum(pick, axis=0, keepdims=True)
    cnt_scr[...] = cnt_new
    cnt_ref[...] = cnt_new
    info_ref[...] = jnp.where(lane == 0, e1, jnp.where(lane == 1, e2, jnp.where(lane == 2, w1, jnp.where(
        lane == 3, w2, jnp.where(lane == 4, r1, jnp.where(lane == 5, r2, 0.0))))))


def _mixout(h, a, a_col, r, r_col, b, g_head, w_o, g_ffn, wr_hi, wr_lo, b_r, counts0, *, hd, act,
            row0, n_rows, prev=None):
    n, d = h.shape
    precise = prev is not None
    tm = _tile(n_rows, 512, SUBLANES)
    assert row0 % tm == 0
    blk0 = row0 // tm
    wa = w_o.shape[0] if b is None else w_o.shape[0] - B_W
    has_b = b is not None
    row = lambda i: (blk0 + i, 0)
    const = lambda i: (0, 0)
    in_specs = [pl.BlockSpec((tm, d), row),
                pl.BlockSpec((tm, wa), lambda i: (blk0 + i, a_col)),
                pl.BlockSpec((tm, wa), lambda i: (blk0 + i, r_col))]
    args = [h, a, r]
    if has_b:
        in_specs.append(pl.BlockSpec((tm, B_W), row))
        args.append(b)
    in_specs += [pl.BlockSpec((1, hd), const), pl.BlockSpec(w_o.shape, const), pl.BlockSpec((1, d), const),
                 pl.BlockSpec((d, LANES), const), pl.BlockSpec((d, LANES), const), pl.BlockSpec((1, LANES), const),
                 pl.BlockSpec((1, LANES), const)]
    args += [g_head.reshape(1, hd), w_o if precise else w_o.astype(BF16), g_ffn.reshape(1, d), wr_hi, wr_lo, b_r,
             counts0]
    io_alias = {}
    if precise:
        for k, arr in enumerate(prev):
            in_specs.append(pl.BlockSpec(memory_space=pl.ANY))
            args.append(arr)
            io_alias[len(args) - 1] = k
    return pl.pallas_call(
        functools.partial(_mixout_kernel, hd=hd, act=act, has_b=has_b, precise=precise, n_prev=len(io_alias)),
        grid=(n_rows // tm,),
        in_specs=in_specs,
        out_specs=[pl.BlockSpec((tm, d), row), pl.BlockSpec((tm, d // 2), row), pl.BlockSpec((tm, LANES), row),
                   pl.BlockSpec((1, LANES), const)],
        out_shape=[jax.ShapeDtypeStruct((n, d), F32), jax.ShapeDtypeStruct((n, d // 2), jnp.int32),
                   jax.ShapeDtypeStruct((n, LANES), F32), jax.ShapeDtypeStruct((1, LANES), F32)],
        scratch_shapes=[pltpu.VMEM((1, LANES), F32)],
        input_output_aliases=io_alias,
        compiler_params=_params(("arbitrary",)),
        name="mixout_" + act + ("_precise" if precise else ""),
    )(*args)


def _sc_gather(table, idx):
    r = idx.shape[0]
    w = table.shape[1]
    n_workers = SC_CORES * SC_SUBCORES
    per_worker = r // n_workers
    step = SC_GATHER_ROWS * SC_GATHER_BUFS
    assert r % n_workers == 0 and per_worker % step == 0, (r, n_workers, step)
    mesh = plsc.VectorSubcoreMesh(core_axis_name="c", subcore_axis_name="s")

    @functools.partial(
        pl.kernel, mesh=mesh,
        out_type=jax.ShapeDtypeStruct((r, w), table.dtype),
        scratch_types=[pltpu.VMEM((SC_GATHER_BUFS, SC_GATHER_ROWS), jnp.int32),
                       pltpu.VMEM((SC_GATHER_BUFS, SC_GATHER_ROWS, w), table.dtype),
                       pltpu.SemaphoreType.DMA((SC_GATHER_BUFS,)),
                       pltpu.SemaphoreType.DMA((SC_GATHER_BUFS,))],
    )
    def gather(table_hbm, idx_hbm, out_hbm, idx_v, rows_v, gather_sem, store_sem):
        worker = lax.axis_index("s") * SC_CORES + lax.axis_index("c")
        base = worker * per_worker

        @pl.loop(0, per_worker // step)
        def _(j):
            off = pl.multiple_of(base + j * step, step)
            rows = lambda b: pl.ds(off + b * SC_GATHER_ROWS, SC_GATHER_ROWS)
            gathers, stores = [], []
            for b in range(SC_GATHER_BUFS):
                pltpu.sync_copy(idx_hbm.at[rows(b)], idx_v.at[b])
                gathers.append(pltpu.async_copy(table_hbm.at[idx_v.at[b]], rows_v.at[b], gather_sem.at[b]))
            for b in range(SC_GATHER_BUFS):
                gathers[b].wait()
                stores.append(pltpu.async_copy(rows_v.at[b], out_hbm.at[rows(b)], store_sem.at[b]))
            for b in range(SC_GATHER_BUFS):
                stores[b].wait()

    return gather(table, idx)


def _sc_scatter_pairs(rows, slots, n_slots):
    n, w = rows.shape
    n_workers = SC_CORES * SC_SUBCORES
    per_worker = n // n_workers
    step = SC_SCATTER_ROWS * SC_SCATTER_BUFS
    assert n % n_workers == 0 and per_worker % step == 0, (n, n_workers, step)
    mesh = plsc.VectorSubcoreMesh(core_axis_name="c", subcore_axis_name="s")

    @functools.partial(
        pl.kernel, mesh=mesh,
        out_type=jax.ShapeDtypeStruct((n_slots, w), rows.dtype),
        scratch_types=[pltpu.VMEM((2 * SC_SCATTER_BUFS, SC_SCATTER_ROWS), jnp.int32),
                       pltpu.VMEM((SC_SCATTER_BUFS, SC_SCATTER_ROWS, w), rows.dtype),
                       pltpu.SemaphoreType.DMA((SC_SCATTER_BUFS,)),
                       pltpu.SemaphoreType.DMA((SC_SCATTER_BUFS,))],
    )
    def scatter(rows_hbm, slots_hbm, out_hbm, idx_v, rows_v, load_sem, store_sem):
        worker = lax.axis_index("s") * SC_CORES + lax.axis_index("c")
        base = worker * per_worker

        @pl.loop(0, per_worker // step)
        def _(j):
            off = pl.multiple_of(base + j * step, SC_SCATTER_ROWS)
            loads, stores = [], []
            for b in range(SC_SCATTER_BUFS):
                r0 = off + b * SC_SCATTER_ROWS
                pltpu.sync_copy(slots_hbm.at[pl.ds(r0, SC_SCATTER_ROWS)], idx_v.at[2 * b])
                pltpu.sync_copy(slots_hbm.at[pl.ds(n + r0, SC_SCATTER_ROWS)], idx_v.at[2 * b + 1])
                loads.append(pltpu.async_copy(rows_hbm.at[pl.ds(r0, SC_SCATTER_ROWS)], rows_v.at[b], load_sem.at[b]))
            for b in range(SC_SCATTER_BUFS):
                loads[b].wait()
                for k in range(2):
                    stores.append(pltpu.async_copy(rows_v.at[b], out_hbm.at[idx_v.at[2 * b + k]], store_sem.at[b]))
            for copy in stores:
                copy.wait()

    return scatter(rows, slots)


def _expert_kernel(te_ref, nu_ref, x_ref, wg_ref, wu_ref, wd_ref, y_ref, wgb, wub, wdb):
    i = pl.program_id(0)
    live = i < nu_ref[0]
    new_expert = (i == 0) | (te_ref[i] != te_ref[jnp.maximum(i - 1, 0)])

    @pl.when(live & new_expert)
    def _():
        wgb[...] = wg_ref[0].astype(BF16)
        wub[...] = wu_ref[0].astype(BF16)
        wdb[...] = wd_ref[0].astype(BF16)

    @pl.when(live)
    def _():
        x = _unpack_bf16_pairs(x_ref[...]).astype(BF16)
        g = _dot(x, wgb[...])
        u = _dot(x, wub[...])
        y_ref[...] = _pack_bf16_pairs(_dot((g * _sigmoid(g) * u).astype(BF16), wdb[...]))

    @pl.when(jnp.logical_not(live))
    def _():
        y_ref[...] = jnp.zeros_like(y_ref)


def _experts(xs, tile_expert, n_used, w_gate, w_up, w_down, *, tm):
    n_slots, dp = xs.shape
    d, f = w_gate.shape[-2:]
    grid_spec = pltpu.PrefetchScalarGridSpec(
        num_scalar_prefetch=2,
        grid=(n_slots // tm,),
        in_specs=[pl.BlockSpec((tm, dp), lambda i, te, nu: (i, 0)),
                  pl.BlockSpec((1, d, f), lambda i, te, nu: (te[i], 0, 0)),
                  pl.BlockSpec((1, d, f), lambda i, te, nu: (te[i], 0, 0)),
                  pl.BlockSpec((1, f, d), lambda i, te, nu: (te[i], 0, 0))],
        out_specs=pl.BlockSpec((tm, dp), lambda i, te, nu: (i, 0)),
        scratch_shapes=[pltpu.VMEM((d, f), BF16), pltpu.VMEM((d, f), BF16), pltpu.VMEM((f, d), BF16)],
    )
    return pl.pallas_call(
        _expert_kernel,
        grid_spec=grid_spec,
        out_shape=jax.ShapeDtypeStruct((n_slots, dp), jnp.int32),
        compiler_params=_params(("arbitrary",)),
        name="moe_experts",
    )(tile_expert, n_used, xs, w_gate, w_up, w_down)


def _combine_kernel(h_ref, info_ref, y0_ref, y1_ref, o_ref):
    o_ref[...] = h_ref[...] + (info_ref[:, 2:3] * _unpack_bf16_pairs(y0_ref[...])
                               + info_ref[:, 3:4] * _unpack_bf16_pairs(y1_ref[...]))


def _combine(h, yg, info):
    n, d = h.shape
    tm = _tile(n, 512, SUBLANES)
    nt = n // tm
    return pl.pallas_call(
        _combine_kernel,
        grid=(nt,),
        in_specs=[pl.BlockSpec((tm, d), lambda i: (i, 0)),
                  pl.BlockSpec((tm, LANES), lambda i: (i, 0)),
                  pl.BlockSpec((tm, d // 2), lambda i: (i, 0)),
                  pl.BlockSpec((tm, d // 2), lambda i: (nt + i, 0))],
        out_specs=pl.BlockSpec((tm, d), lambda i: (i, 0)),
        out_shape=jax.ShapeDtypeStruct((n, d), F32),
        compiler_params=_params(("parallel",)),
        name="moe_combine",
    )(h, info, yg, yg)


def _final_kernel(*refs, sub, prompt_steps):
    g_ref = refs[0]
    h_refs, info_refs, y0_refs, y1_refs = (refs[1 + k * sub:1 + (k + 1) * sub] for k in range(4))
    yp_ref, ys_ref = refs[1 + 4 * sub:]
    tin = h_refs[0].shape[0]
    i = pl.program_id(0)

    def store(o_ref):
        for k in range(sub):
            h2 = h_refs[k][...] + (info_refs[k][:, 2:3] * _unpack_bf16_pairs(y0_refs[k][...])
                                   + info_refs[k][:, 3:4] * _unpack_bf16_pairs(y1_refs[k][...]))
            o_ref[k * tin:(k + 1) * tin, :] = _rms(h2, g_ref[...])

    pl.when(i < prompt_steps)(functools.partial(store, yp_ref))
    pl.when(i >= prompt_steps)(functools.partial(store, ys_ref))


def _final_combine(h, yg, info, g, *, nb, t, n_sample):
    n, d = h.shape
    seq = t - LEAD
    tin = _tile(math.gcd(LEAD, n_sample, seq), LANES, SUBLANES)
    sub = FINAL_SUBTILES if seq % (FINAL_SUBTILES * tin) == 0 and n_sample % (FINAL_SUBTILES * tin) == 0 else 1
    tout = sub * tin
    steps_per_batch = seq // tout
    prompt_steps = nb * steps_per_batch
    tiles_per_batch, lead_tiles, n_tiles = t // tin, LEAD // tin, n // tin

    def in_tile(k, offset=0):
        def index(i):
            ip = jnp.minimum(i, prompt_steps - 1)
            prompt = (ip // steps_per_batch) * tiles_per_batch + lead_tiles + (ip % steps_per_batch) * sub + k
            sample = nb * tiles_per_batch + (i - prompt_steps) * sub + k
            return offset + jnp.where(i < prompt_steps, prompt, sample), 0
        return index

    in_specs = ([pl.BlockSpec((1, d), lambda i: (0, 0))]
                + [pl.BlockSpec((tin, d), in_tile(k)) for k in range(sub)]
                + [pl.BlockSpec((tin, LANES), in_tile(k)) for k in range(sub)]
                + [pl.BlockSpec((tin, d // 2), in_tile(k)) for k in range(sub)]
                + [pl.BlockSpec((tin, d // 2), in_tile(k, n_tiles)) for k in range(sub)])
    return pl.pallas_call(
        functools.partial(_final_kernel, sub=sub, prompt_steps=prompt_steps),
        grid=(prompt_steps + n_sample // tout,),
        in_specs=in_specs,
        out_specs=[pl.BlockSpec((tout, d), lambda i: (jnp.minimum(i, prompt_steps - 1), 0)),
                   pl.BlockSpec((tout, d), lambda i: (jnp.maximum(i - prompt_steps, 0), 0))],
        out_shape=[jax.ShapeDtypeStruct((nb * seq, d), F32), jax.ShapeDtypeStruct((n_sample, d), F32)],
        compiler_params=_params(("arbitrary",)),
        name="moe_combine_final",
    )(g.reshape(1, d), *([h] * sub + [info] * sub + [yg] * (2 * sub)))


def _moe(h1, xn, info, counts_f, w_gate, w_up, w_down, expert_base, *, final=None):
    n, d = h1.shape
    n_e = N_GROUPS * N_EXP
    tm = _tile(2 * n, MOE_TILE, SUBLANES)
    n_tiles = (2 * n) // tm + n_e
    n_slots = n_tiles * tm
    experts = jnp.arange(n_e, dtype=jnp.int32)
    counts = counts_f[0, :n_e].astype(jnp.int32)
    padded = ((counts + tm - 1) // tm) * tm
    pad_ends = jnp.cumsum(padded)
    pad_starts = pad_ends - padded
    n_used = (pad_ends[-1] // tm).astype(jnp.int32)
    eid = info[:, 0:2].astype(jnp.int32)
    rank = info[:, 4:6].astype(jnp.int32)
    slot_of_pick = jnp.sum(jnp.where(eid[..., None] == experts, pad_starts, 0), axis=-1) + rank
    slots = slot_of_pick.T.reshape(-1)
    tile_starts = jnp.arange(n_tiles, dtype=jnp.int32) * tm
    tile_expert = expert_base + jnp.minimum(jnp.sum(tile_starts[:, None] >= pad_ends[None, :], axis=1), n_e - 1)

    xs = _sc_scatter_pairs(xn, slots, n_slots)
    ys = _experts(xs, tile_expert.astype(jnp.int32), n_used.reshape(1), w_gate, w_up, w_down, tm=tm)
    yg = _sc_gather(ys, slots)
    if final is not None:
        return _final_combine(h1, yg, info, final["g"], nb=final["nb"], t=final["t"], n_sample=final["n_sample"])
    return _combine(h1, yg, info)


def _router_weights(w_rg, b_rg, w_re, b_re):
    d = w_rg.shape[0]
    pad = LANES - N_GROUPS - N_GROUPS * N_EXP
    w = jnp.concatenate([w_rg, w_re, jnp.zeros((d, pad), F32)], axis=1)
    b = jnp.concatenate([b_rg, b_re, jnp.zeros((pad,), F32)]).reshape(1, LANES)
    hi = w.astype(BF16)
    lo = (w - hi.astype(F32)).astype(BF16)
    return hi, lo, b


def _rows_to_lanes(x, nb, t, nch):
    return x.reshape(nb, t, nch).transpose(0, 2, 1).reshape(nb * nch, t)


def _even_layer(h, dims, g_mix, w_in, w_g2, b_g, b_f, g_a, w_o, state_gla, ck, cv, clf):
    nb, t, db, ds, npr, n = dims
    d = h.shape[1]
    qa, ka, va, ra, ga, qb, kb, vb, fb = jnp.split(
        w_in, [A_QK, 2 * A_QK, 2 * A_QK + A_V, 2 * A_QK + 2 * A_V, 2 * A_QK + 2 * A_V + A_RANK,
               2 * A_QK + 2 * A_V + A_RANK + B_W, 2 * A_QK + 2 * A_V + A_RANK + 2 * B_W,
               2 * A_QK + 2 * A_V + A_RANK + 3 * B_W], axis=1)
    w_packed = jnp.concatenate(
        [qa, ka, va, ra, qb, kb, vb, ga, fb, jnp.zeros((d, LANES - A_RANK - B_HEADS), F32)], axis=1)
    q_off = 2 * A_QK + 2 * A_V
    colscale = jnp.ones((1, MAIN_W), F32).at[:, q_off:q_off + B_W].set(B_DH ** -0.5 * LOG2E)
    p, pb, k_rows, v_rows = _proj(h, g_mix, w_packed, n_rows=npr, colscale=colscale, kv_col=q_off + B_W)
    p = _proj_precise(h, g_mix, w_packed, p, row0=npr)

    nh = B_HEADS
    fcol = MAIN_W + A_RANK
    zeros_col = lambda r: jnp.zeros((r, 1), F32)
    bias_row = lambda lanes: jnp.tile(b_f, lanes // nh).reshape(1, lanes)
    fb_p = p[:npr, fcol:fcol + nh].reshape(nb, t * nh)
    logf_p, f_p = _gate_scan(fb_p, bias_row(t * nh), zeros_col(nb), mode="fox", act_start=0,
                             valid_start=N_PAD * nh, valid_end=t * nh, seg=None, stride=nh)
    past = ck.shape[1]
    n_c = past * nh
    x_s = _pad_lanes(jnp.concatenate([clf.reshape(db, n_c), p[npr:, fcol:fcol + nh].reshape(db, ds * nh)], axis=1))
    logf_s, f_s = _gate_scan(x_s, bias_row(x_s.shape[1]), zeros_col(db), mode="fox", act_start=n_c,
                             valid_start=0, valid_end=n_c + ds * nh, seg=None, stride=nh)

    nc = t // CHUNK
    oa, s_p = _gla(p, jnp.zeros((nb, A_HEADS, A_DK, A_DV), F32), w_g2, b_g,
                   nb=nb, nc=nc, L=CHUNK, row_block0=0, lead_pad=N_PAD, n_total=n)
    oa, s_s = _gla(p, state_gla, w_g2, b_g, nb=db, nc=1, L=ds, row_block0=npr // ds, lead_pad=0,
                   n_total=n, prev_out=oa, precise=True)

    fk = (f_p * LOG2E).reshape(nb, t, nh).transpose(0, 2, 1).reshape(nb, nh, 1, t)
    ob = _flash(pb, fk, nb=nb, t=t, n_total=n)
    f_cache = (f_s[:, :n_c] * LOG2E).reshape(db, 1, n_c)
    f_new = (f_s[:, n_c:n_c + ds * nh] * LOG2E).reshape(db, ds, nh).transpose(0, 2, 1).reshape(db, 1, nh * ds)
    ob = _fox_sample(p, ck.reshape(db, n_c, B_DH), cv.reshape(db, n_c, B_DH), f_cache, f_new, ob,
                     nb=db, ds=ds, row_block0=npr // ds)

    kcol = q_off + B_W
    states = dict(
        s_p=s_p, s_s=s_s,
        k_p=k_rows.reshape(nb, t, B_HEADS, B_DH)[:, N_PAD:],
        v_p=v_rows.reshape(nb, t, B_HEADS, B_DH)[:, N_PAD:],
        f_p=logf_p.reshape(nb, t, nh)[:, N_PAD:],
        k_s=p[npr:, kcol:kcol + B_W].reshape(db, ds, B_HEADS, B_DH),
        v_s=p[npr:, kcol + B_W:kcol + 2 * B_W].reshape(db, ds, B_HEADS, B_DH),
        f_s=logf_s[:, n_c:n_c + ds * nh].reshape(db, ds, nh))
    return (oa, 0, p, (2 * A_QK + A_V) // A_V, ob, g_a, w_o), states


def _chunk_rows(x, nb, nch, nc, L):
    x = x[:, :nc * L].reshape(nb, nch, nc, L)
    return x.transpose(0, 2, 1, 3).reshape(nb, nc, 1, nch * L), x.transpose(0, 2, 3, 1)


def _odd_layer(h, dims, g_mix, w_in, b_gate, g_c, w_o, c0, n0, m0):
    nb, t, db, ds, npr, n = dims
    d = h.shape[1]
    w_packed = jnp.concatenate(
        [w_in, jnp.zeros((d, LANES - 2 * C_HEADS), F32)], axis=1)
    (p,) = _proj(h, g_mix, w_packed, n_rows=npr)
    p = _proj_precise(h, g_mix, w_packed, p, row0=npr)

    ng = 2 * C_HEADS
    isf = (jnp.arange(ng) >= C_HEADS).astype(F32)
    nc = t // CHUNK

    def gates(rows, nbatch, tt, valid_start, seg):
        x = _pad_lanes(_rows_to_lanes(rows, nbatch, tt, ng))
        val, cum = _gate_scan(x, jnp.tile(b_gate, nbatch).reshape(-1, 1), jnp.tile(isf, nbatch).reshape(-1, 1),
                              mode="mlstm", act_start=0, valid_start=valid_start, valid_end=tt, seg=seg)
        return val.reshape(nbatch, ng, -1), cum.reshape(nbatch, ng, -1)

    val_p, cum_p = gates(p[:npr, MAIN_W:MAIN_W + ng], nb, t, N_PAD, CHUNK)
    val_s, cum_s = gates(p[npr:, MAIN_W:MAIN_W + ng], db, ds, 0, ds)

    def chunked(val, cum, nbatch, ncs, L):
        li_row, li_col = _chunk_rows(val[:, :C_HEADS].reshape(nbatch * C_HEADS, -1), nbatch, C_HEADS, ncs, L)
        b_row, b_col = _chunk_rows(cum[:, C_HEADS:].reshape(nbatch * C_HEADS, -1), nbatch, C_HEADS, ncs, L)
        return li_row, b_row, li_col, b_col

    zc = jnp.zeros((nb, C_HEADS, C_DV, C_DQK), F32)
    zn = jnp.zeros((nb, C_HEADS, C_DQK), F32)
    zm = jnp.zeros((nb, C_HEADS, 1), F32)
    hm, c_p, n_p, m_p = _mlstm(p, *chunked(val_p, cum_p, nb, nc, CHUNK), zc, zn, zm,
                               nb=nb, nc=nc, L=CHUNK, row_block0=0, n_total=n)
    hm, c_s, n_s, m_s = _mlstm(p, *chunked(val_s, cum_s, db, 1, ds), c0, n0, m0.reshape(db, C_HEADS, 1),
                               nb=db, nc=1, L=ds, row_block0=npr // ds, n_total=n, prev_out=hm)
    states = dict(c_p=c_p, n_p=n_p, m_p=m_p.reshape(nb, C_HEADS), c_s=c_s, n_s=n_s, m_s=m_s.reshape(db, C_HEADS))
    return (hm, 0, p, (2 * C_QK + C_V) // C_V, None, g_c, w_o), states


def kernel(x_prompt, x_sample, state_gla, cache_fox_k, cache_fox_v, cache_fox_logf, state_mlstm_c, state_mlstm_n, state_mlstm_m, meta_tokens, norm_mix, norm_ffn, norm_final, w_in_even, w_gla_gate2, b_gla_gate, b_fox_f, g_gla_out, w_out_even, w_in_odd, b_mlstm_gate, g_mlstm_out, w_out_odd, w_router_group, b_router_group, w_router_expert, b_router_expert, w_exp_gate, w_exp_up, w_exp_down):
    nb, seq, d = x_prompt.shape
    db, ds, _ = x_sample.shape
    t = LEAD + seq
    npr, nsm = nb * t, db * ds
    n = npr + nsm
    dims = (nb, t, db, ds, npr, n)
    depth = norm_mix.shape[0]
    n_e = N_GROUPS * N_EXP
    f = w_exp_gate.shape[-1]

    pad_rows = jnp.zeros((N_PAD, d), F32)
    h = jnp.concatenate([piece for b in range(nb) for piece in (pad_rows, meta_tokens, x_prompt[b])]
                        + [x_sample.reshape(nsm, d)], axis=0)
    wg_all = w_exp_gate.reshape(depth * n_e, d, f)
    wu_all = w_exp_up.reshape(depth * n_e, d, f)
    wd_all = w_exp_down.reshape(depth * n_e, f, d)

    even, odd = [], []
    for l in range(depth):
        if l % 2 == 0:
            e = l // 2
            mix, st = _even_layer(h, dims, norm_mix[l], w_in_even[e], w_gla_gate2[e], b_gla_gate[e], b_fox_f[e],
                                  g_gla_out[e], w_out_even[e], state_gla[e], cache_fox_k[e], cache_fox_v[e],
                                  cache_fox_logf[e])
            even.append(st)
            hd, act = A_DV, "silu"
        else:
            o = l // 2
            mix, st = _odd_layer(h, dims, norm_mix[l], w_in_odd[o], b_mlstm_gate[o], g_mlstm_out[o], w_out_odd[o],
                                 state_mlstm_c[o], state_mlstm_n[o], state_mlstm_m[o])
            odd.append(st)
            hd, act = C_DV, "sigmoid"
        a, a_col, r, r_col, b, g_head, w_o = mix
        wr_hi, wr_lo, b_r = _router_weights(w_router_group[l], b_router_group[l], w_router_expert[l],
                                            b_router_expert[l])
        mix_args = (h, a, a_col, r, r_col, b, g_head, w_o, norm_ffn[l], wr_hi, wr_lo, b_r)
        h1, xn, info, counts = _mixout(*mix_args, jnp.zeros((1, LANES), F32), hd=hd, act=act, row0=0, n_rows=npr)
        h1, xn, info, counts = _mixout(*mix_args, counts, hd=hd, act=act, row0=npr, n_rows=nsm,
                                       prev=(h1, xn, info))
        last = l == depth - 1
        h = _moe(h1, xn, info, counts, wg_all, wu_all, wd_all, l * n_e,
                 final=dict(g=norm_final, nb=nb, t=t, n_sample=nsm) if last else None)

    y_prompt = h[0].reshape(nb, seq, d)
    y_sample = h[1].reshape(db, ds, d)
    stack = lambda sts, key: jnp.stack([s[key] for s in sts])
    return (y_prompt, y_sample,
            stack(even, "s_p"), stack(even, "k_p"), stack(even, "v_p"), stack(even, "f_p"),
            stack(odd, "c_p"), stack(odd, "n_p"), stack(odd, "m_p"),
            stack(even, "s_s"), stack(even, "k_s"), stack(even, "v_s"), stack(even, "f_s"),
            stack(odd, "c_s"), stack(odd, "n_s"), stack(odd, "m_s"))
```

```python
import functools
import math

import jax
import jax.numpy as jnp
from jax import lax
from jax.experimental import pallas as pl
from jax.experimental.pallas import tpu as pltpu
from jax.experimental.pallas import tpu_sc as plsc

F32 = jnp.float32
BF16 = jnp.bfloat16

CHUNK = 64
N_META = 16
LEAD = 128
N_PAD = LEAD - N_META
A_HEADS, A_DK, A_DV, A_RANK = 4, 64, 128, 16
A_GATE_NORM = 16.0
B_HEADS, B_DH = 4, 128
C_HEADS, C_DQK, C_DV = 4, 128, 256
GATE_CAP = 15.0
N_GROUPS, N_EXP = 4, 8
EPS = 1e-6
NEG = -1e30
LOG2E = 1.4426950408889634
A_QK = A_HEADS * A_DK
A_V = A_HEADS * A_DV
B_W = B_HEADS * B_DH
C_QK = C_HEADS * C_DQK
C_V = C_HEADS * C_DV

LANES = 128
SUBLANES = 8
VMEM_LIMIT_BYTES = 56 * 1024 * 1024
SC_CORES, SC_SUBCORES = 2, 16
SC_GATHER_ROWS, SC_GATHER_BUFS = 64, 3
SC_SCATTER_ROWS, SC_SCATTER_BUFS = 32, 3
GLA_SUB = 16
GLA_CHUNKS_PER_STEP = 10
MLSTM_CHUNKS_PER_STEP = 1
FLASH_HEADS = 4
MOE_TILE = 1024
FINAL_SUBTILES = 4
MAIN_W = 3072
PROJ_W = MAIN_W + LANES

_NT = (((1,), (1,)), ((), ()))
_TN = (((0,), (0,)), ((), ()))
_NN = (((1,), (0,)), ((), ()))


def _params(sem):
    return pltpu.CompilerParams(dimension_semantics=sem, vmem_limit_bytes=VMEM_LIMIT_BYTES)


def _tile(n, pref, mult):
    t = (min(pref, n) // mult) * mult
    while t > mult and n % t:
        t -= mult
    assert t >= mult and n % t == 0, (n, pref, mult)
    return t


def _dot(a, b, dims=_NN):
    return lax.dot_general(a, b, dims, preferred_element_type=F32)


def _split(x):
    hi = x.astype(BF16)
    lo = (x - hi.astype(F32)).astype(BF16)
    return hi, lo


def _dot3(a, b, dims=_NN):
    ah, al = _split(a)
    bh, bl = _split(b)
    return _dot(ah, bh, dims) + _dot(ah, bl, dims) + _dot(al, bh, dims)


def _log_sigmoid(x):
    return jnp.minimum(x, 0.0) - jnp.log1p(jnp.exp(-jnp.abs(x)))


def _sigmoid(x):
    return 1.0 / (1.0 + jnp.exp(-x))


def _rms(x, g):
    return x * lax.rsqrt(jnp.mean(x * x, axis=-1, keepdims=True) + EPS) * g


def _pack_bf16_pairs(x):
    w = x.shape[1] // 2
    hi = lax.bitcast_convert_type(x[:, :w].astype(BF16).astype(F32), jnp.int32)
    lo = lax.bitcast_convert_type(x[:, w:].astype(BF16).astype(F32), jnp.int32)
    return hi | lax.shift_right_logical(lo, 16)


def _unpack_bf16_pairs(p):
    hi = lax.bitcast_convert_type(p & jnp.int32(-65536), F32)
    lo = lax.bitcast_convert_type(lax.shift_left(p, 16), F32)
    return jnp.concatenate([hi, lo], axis=1)


def _cumsum_rows(x):
    n = x.shape[0]
    row = lax.broadcasted_iota(jnp.int32, x.shape, 0)
    s = 1
    while s < n:
        x = x + jnp.where(row >= s, pltpu.roll(x, s, axis=0), 0.0)
        s *= 2
    return x


def _proj_kernel(x_ref, g_ref, w_ref, *rest, col_chunk, kv_col):
    if kv_col is None:
        (o_ref,) = rest
    else:
        cs_ref, o_ref, ob_ref, *kv_refs = rest
    tm = x_ref.shape[0]
    xn = _rms(x_ref[...], g_ref[...]).astype(BF16)
    for c0 in range(0, PROJ_W, col_chunk):
        c1 = min(c0 + col_chunk, PROJ_W)
        y = _dot(xn, w_ref[:, c0:c1])
        o_ref[:, c0:c1] = y
        if kv_col is not None:
            if c0 < MAIN_W:
                m1 = min(c1, MAIN_W)
                ob_ref[:, c0:m1] = (y[:, :m1 - c0] * cs_ref[:, c0:m1]).astype(BF16)
            for g0 in range(c0, c1, LANES):
                rel = g0 - kv_col
                if 0 <= rel < 2 * B_W:
                    head = (rel % B_W) // B_DH
                    kv_refs[rel // B_W][pl.ds(head, tm, stride=B_HEADS), :] = y[:, g0 - c0:g0 - c0 + LANES]


def _proj_precise_kernel(x_ref, g_ref, w_ref, prev_ref, o_ref, *, col_chunk):
    del prev_ref
    xn = _rms(x_ref[...], g_ref[...])
    for c0 in range(0, PROJ_W, col_chunk):
        c1 = min(c0 + col_chunk, PROJ_W)
        o_ref[:, c0:c1] = _dot3(xn, w_ref[:, c0:c1])


def _proj(h, g, w_packed, *, n_rows, colscale=None, kv_col=None):
    n, d = h.shape
    tm = _tile(n_rows, 512, 16)
    in_specs = [pl.BlockSpec((tm, d), lambda i: (i, 0)),
                pl.BlockSpec((1, d), lambda i: (0, 0)),
                pl.BlockSpec((d, PROJ_W), lambda i: (0, 0))]
    args = [h, g.reshape(1, d), w_packed.astype(BF16)]
    out_specs = [pl.BlockSpec((tm, PROJ_W), lambda i: (i, 0))]
    out_shape = [jax.ShapeDtypeStruct((n, PROJ_W), F32)]
    if kv_col is not None:
        in_specs.append(pl.BlockSpec((1, MAIN_W), lambda i: (0, 0)))
        args.append(colscale)
        out_specs += [pl.BlockSpec((tm, MAIN_W), lambda i: (i, 0))] + [pl.BlockSpec((tm * B_HEADS, B_DH), lambda i: (i, 0))] * 2
        out_shape += ([jax.ShapeDtypeStruct((n_rows, MAIN_W), BF16)]
                      + [jax.ShapeDtypeStruct((n_rows * B_HEADS, B_DH), F32)] * 2)
    return pl.pallas_call(
        functools.partial(_proj_kernel, col_chunk=640, kv_col=kv_col),
        grid=(n_rows // tm,),
        in_specs=in_specs,
        out_specs=out_specs,
        out_shape=out_shape,
        compiler_params=_params(("parallel",)),
        name="proj",
    )(*args)


def _proj_precise(h, g, w_packed, prev, *, row0):
    n, d = h.shape
    tm = _tile(n - row0, 512, SUBLANES)
    assert row0 % tm == 0
    return pl.pallas_call(
        functools.partial(_proj_precise_kernel, col_chunk=640),
        grid=((n - row0) // tm,),
        in_specs=[pl.BlockSpec((tm, d), lambda i: (row0 // tm + i, 0)),
                  pl.BlockSpec((1, d), lambda i: (0, 0)),
                  pl.BlockSpec((d, PROJ_W), lambda i: (0, 0)),
                  pl.BlockSpec(memory_space=pl.ANY)],
        out_specs=pl.BlockSpec((tm, PROJ_W), lambda i: (row0 // tm + i, 0)),
        out_shape=jax.ShapeDtypeStruct((n, PROJ_W), F32),
        input_output_aliases={3: 0},
        compiler_params=_params(("parallel",)),
        name="proj_precise",
    )(h, g.reshape(1, d), w_packed, prev)


def _gate_scan_kernel(x_ref, bias_ref, isf_ref, val_ref, cum_ref, *, mode, act_start, valid_start, valid_end, seg, stride):
    x = x_ref[...]
    lane = lax.broadcasted_iota(jnp.int32, x.shape, 1)
    valid = (lane >= valid_start) & (lane < valid_end)
    if mode == "fox":
        val = jnp.where(lane >= act_start, _log_sigmoid(x + bias_ref[...]), x)
        val = jnp.where(valid, val, 0.0)
        add = val
    else:
        gate = GATE_CAP * jnp.tanh((x + bias_ref[...]) / GATE_CAP)
        isf = isf_ref[...] > 0.5
        val = jnp.where(isf, jnp.where(valid, _log_sigmoid(gate), 0.0),
                        jnp.where(valid, gate, -jnp.inf))
        add = jnp.where(isf, val, 0.0)
    val_ref[...] = val
    n = x.shape[1]
    pos = lane if seg is None else lane % seg
    limit = n if seg is None else seg
    s = stride
    while s < limit:
        add = add + jnp.where(pos >= s, pltpu.roll(add, s, axis=1), 0.0)
        s *= 2
    cum_ref[...] = add


def _gate_scan(x, bias, isf, *, mode, act_start, valid_start, valid_end, seg, stride=1):
    r, n = x.shape
    full = lambda shape: pl.BlockSpec(shape, lambda i: (0,) * len(shape))
    return pl.pallas_call(
        functools.partial(_gate_scan_kernel, mode=mode, act_start=act_start,
                          valid_start=valid_start, valid_end=valid_end, seg=seg, stride=stride),
        grid=(1,),
        in_specs=[full((r, n)), full(bias.shape), full((r, 1))],
        out_specs=[full((r, n)), full((r, n))],
        out_shape=[jax.ShapeDtypeStruct((r, n), F32)] * 2,
        compiler_params=_params(("arbitrary",)),
        name="gate_scan_" + mode,
    )(x, bias, isf)


def _pad_lanes(x):
    n = x.shape[-1]
    m = -(-n // LANES) * LANES
    return x if m == n else jnp.pad(x, ((0, 0), (0, m - n)))


def _gla_kernel(qk_ref, v_ref, sm_ref, s0_ref, wg2_ref, wg2t_ref, bgr_ref, bgc_ref, *rest,
                L, cps, sub, lead_pad, aliased, precise):
    if aliased:
        rest = rest[1:]
    o_ref, sout_ref, s_scr = rest
    step = pl.program_id(1)
    for j in range(cps):
        rows = pl.ds(j * L, L)
        _gla_chunk(qk_ref.at[rows], v_ref.at[rows], sm_ref.at[rows], s0_ref, wg2_ref, wg2t_ref, bgr_ref, bgc_ref,
                   o_ref.at[rows], sout_ref, s_scr, c=step * cps + j, first=(j == 0), last=(j == cps - 1),
                   L=L, sub=sub, lead_pad=lead_pad, precise=precise)


def _gla_chunk(qk_ref, v_ref, sm_ref, s0_ref, wg2_ref, wg2t_ref, bgr_ref, bgc_ref, o_ref, sout_ref, s_scr, *,
               c, first, last, L, sub, lead_pad, precise):
    nh, dk, dv = A_HEADS, A_DK, A_DV
    cast = (lambda x: x) if precise else (lambda x: x.astype(BF16))
    mm = _dot3 if precise else _dot

    if first:
        @pl.when(pl.program_id(1) == 0)
        def _():
            s_scr[...] = jnp.zeros_like(s_scr)
            for h in range(nh):
                s_scr[h * dk:(h + 1) * dk, h * dv:(h + 1) * dv] = s0_ref[0, h]

    qk = qk_ref[...]
    q = qk[:, :A_QK] * (A_DK ** -0.5)
    k = qk[:, A_QK:]
    v = v_ref[...]
    ga = sm_ref[:, :A_RANK]
    row = lax.broadcasted_iota(jnp.int32, (L, 1), 0)
    valid = (c * L + row) >= lead_pad
    z = _dot3(ga, wg2_ref[...]) + bgr_ref[...]
    loga = jnp.where(valid, _log_sigmoid(z) / A_GATE_NORM, 0.0)
    k = jnp.where(valid, k, 0.0)
    b = _cumsum_rows(loga)
    b_last = b[L - 1:L, :]
    lane_t = lax.broadcasted_iota(jnp.int32, (1, L), 1)
    zt = _dot3(wg2t_ref[...], ga, _NT) + bgc_ref[...]
    logat = jnp.where((c * L + lane_t) >= lead_pad, _log_sigmoid(zt) / A_GATE_NORM, 0.0)
    b_last_col = jnp.sum(logat, axis=1, keepdims=True)

    qhead = lax.broadcasted_iota(jnp.int32, (1, A_QK), 1) // dk
    vhead = lax.broadcasted_iota(jnp.int32, (1, A_V), 1) // dv
    vb = cast(v)
    zero_b = jnp.zeros((), vb.dtype)
    v_bd = jnp.concatenate([jnp.where(vhead == h, vb, zero_b) for h in range(nh)], axis=0)

    rows_all = lax.broadcasted_iota(jnp.int32, (L, 1), 0)
    a_rows = []
    for i in range(L // sub):
        r0 = i * sub
        ci = jnp.zeros((1, A_QK), F32) if i == 0 else b[r0 - 1:r0, :]
        qt = cast(q[r0:r0 + sub] * jnp.exp(b[r0:r0 + sub] - ci))
        kt = cast(jnp.where(rows_all < r0 + sub, k * jnp.exp(ci - b), 0.0))
        k_stack = jnp.concatenate([jnp.where(qhead == h, kt, zero_b) for h in range(nh)], axis=0)
        a_rows.append(mm(qt, k_stack, _NT))
    a = a_rows[0] if len(a_rows) == 1 else jnp.concatenate(a_rows, axis=0)
    t_idx = lax.broadcasted_iota(jnp.int32, (L, nh * L), 0)
    s_idx = lax.broadcasted_iota(jnp.int32, (L, nh * L), 1) % L
    a = jnp.where(s_idx <= t_idx, a, 0.0)
    o_intra = mm(cast(a), v_bd)

    s_full = s_scr[...]
    o_inter = mm(cast(q * jnp.exp(b)), cast(s_full))
    o_ref[...] = o_inter + o_intra

    k_hat = cast(k * jnp.exp(b_last - b))
    upd = mm(k_hat, vb, _TN)
    khead_col = lax.broadcasted_iota(jnp.int32, (A_QK, 1), 0) // dk
    s_new = jnp.exp(b_last_col) * s_full + jnp.where(khead_col == vhead, upd, 0.0)
    s_scr[...] = s_new

    if last:
        @pl.when(pl.program_id(1) == pl.num_programs(1) - 1)
        def _():
            for h in range(nh):
                sout_ref[0, h] = s_new[h * dk:(h + 1) * dk, h * dv:(h + 1) * dv]


def _chunks_per_step(nc, want):
    return next(k for k in (want, 2, 1) if k <= want and nc % k == 0)


def _gla(p, s0, wg2, bg, *, nb, nc, L, row_block0, lead_pad, n_total, prev_out=None, precise=False):
    aliased = prev_out is not None
    cps = _chunks_per_step(nc, GLA_CHUNKS_PER_STEP)
    assert row_block0 % cps == 0
    nc, L, row_block0, chunk = nc // cps, L * cps, row_block0 // cps, L
    rb = lambda b, c: row_block0 + b * nc + c
    in_specs = [pl.BlockSpec((L, 2 * A_QK), lambda b, c: (rb(b, c), 0)),
                pl.BlockSpec((L, A_V), lambda b, c: (rb(b, c), 2 * A_QK // A_V)),
                pl.BlockSpec((L, LANES), lambda b, c: (rb(b, c), MAIN_W // LANES)),
                pl.BlockSpec((1, A_HEADS, A_DK, A_DV), lambda b, c: (b, 0, 0, 0)),
                pl.BlockSpec((A_RANK, A_QK), lambda b, c: (0, 0)),
                pl.BlockSpec((A_QK, A_RANK), lambda b, c: (0, 0)),
                pl.BlockSpec((1, A_QK), lambda b, c: (0, 0)),
                pl.BlockSpec((A_QK, 1), lambda b, c: (0, 0))]
    args = [p, p, p, s0, wg2, wg2.T, bg.reshape(1, A_QK), bg.reshape(A_QK, 1)]
    io_alias = {}
    if aliased:
        in_specs.append(pl.BlockSpec(memory_space=pl.ANY))
        args.append(prev_out)
        io_alias = {len(args) - 1: 0}
    return pl.pallas_call(
        functools.partial(_gla_kernel, L=chunk, cps=cps, sub=min(GLA_SUB, chunk), lead_pad=lead_pad,
                          aliased=aliased, precise=precise),
        grid=(nb, nc),
        in_specs=in_specs,
        out_specs=[pl.BlockSpec((L, A_V), lambda b, c: (rb(b, c), 0)),
                   pl.BlockSpec((1, A_HEADS, A_DK, A_DV), lambda b, c: (b, 0, 0, 0))],
        out_shape=[jax.ShapeDtypeStruct((n_total, A_V), F32),
                   jax.ShapeDtypeStruct((nb, A_HEADS, A_DK, A_DV), F32)],
        scratch_shapes=[pltpu.VMEM((A_QK, A_V), F32)],
        input_output_aliases=io_alias,
        compiler_params=_params(("parallel", "arbitrary")),
        name="gla_L%d" % chunk,
    )(*args)


def _flash_kernel(qi_ref, kj_ref, q_ref, k_ref, v_ref, fk_ref, o_ref, m_scr, l_scr, acc_scr, *, blk, lead_pad):
    step = pl.program_id(2)
    i = qi_ref[step]
    j = kj_ref[step]

    @pl.when(j == 0)
    def _():
        m_scr[...] = jnp.full_like(m_scr, -jnp.inf)
        l_scr[...] = jnp.zeros_like(l_scr)
        acc_scr[...] = jnp.zeros_like(acc_scr)

    def update(masked):
        for g in range(FLASH_HEADS):
            sl = slice(g * B_DH, (g + 1) * B_DH)
            s = _dot(q_ref[:, sl], k_ref[:, sl], _NT) - fk_ref[0, g]
            if masked:
                qpos = i * blk + lax.broadcasted_iota(jnp.int32, (blk, blk), 0)
                kpos = j * blk + lax.broadcasted_iota(jnp.int32, (blk, blk), 1)
                s = jnp.where((kpos <= qpos) & (kpos >= lead_pad), s, NEG)
            m_prev = m_scr[g]
            m_new = jnp.maximum(m_prev, jnp.max(s, axis=1, keepdims=True))
            alpha = jnp.exp2(m_prev - m_new)
            p = jnp.exp2(s - m_new)
            l_scr[g] = alpha * l_scr[g] + jnp.sum(p, axis=1, keepdims=True)
            acc_scr[g] = alpha * acc_scr[g] + _dot(p.astype(BF16), v_ref[:, sl])
            m_scr[g] = m_new

    edge = (j == i) | (j == 0)
    pl.when(edge)(functools.partial(update, True))
    pl.when(jnp.logical_not(edge))(functools.partial(update, False))

    @pl.when(j == i)
    def _():
        for g in range(FLASH_HEADS):
            o_ref[:, g * B_DH:(g + 1) * B_DH] = acc_scr[g] / l_scr[g]


def _flash(pb, fk, *, nb, t, n_total):
    blk = _tile(t, 640, LANES)
    nq = t // blk
    hg = FLASH_HEADS
    w = hg * B_DH
    pairs = [(i, j) for i in range(nq) for j in range(i + 1)]
    qi = jnp.asarray([p[0] for p in pairs], jnp.int32)
    kj = jnp.asarray([p[1] for p in pairs], jnp.int32)
    qc, kc, vc = (A_QK * 2 + A_V * 2) // w, (A_QK * 2 + A_V * 2 + B_W) // w, (A_QK * 2 + A_V * 2 + 2 * B_W) // w
    grid_spec = pltpu.PrefetchScalarGridSpec(
        num_scalar_prefetch=2,
        grid=(nb, B_HEADS // hg, len(pairs)),
        in_specs=[pl.BlockSpec((blk, w), lambda b, h, s, qi, kj: (b * nq + qi[s], qc + h)),
                  pl.BlockSpec((blk, w), lambda b, h, s, qi, kj: (b * nq + kj[s], kc + h)),
                  pl.BlockSpec((blk, w), lambda b, h, s, qi, kj: (b * nq + kj[s], vc + h)),
                  pl.BlockSpec((1, hg, 1, blk), lambda b, h, s, qi, kj: (b, h, 0, kj[s]))],
        out_specs=pl.BlockSpec((blk, w), lambda b, h, s, qi, kj: (b * nq + qi[s], h)),
        scratch_shapes=[pltpu.VMEM((hg, blk, 1), F32), pltpu.VMEM((hg, blk, 1), F32),
                        pltpu.VMEM((hg, blk, B_DH), F32)],
    )
    return pl.pallas_call(
        functools.partial(_flash_kernel, blk=blk, lead_pad=N_PAD),
        grid_spec=grid_spec,
        out_shape=jax.ShapeDtypeStruct((n_total, B_W), F32),
        compiler_params=_params(("parallel", "parallel", "arbitrary")),
        name="fox_flash",
    )(qi, kj, pb, pb, pb, fk)


def _fox_sample_kernel(q_ref, kn_ref, vn_ref, kc_ref, vc_ref, fc_ref, fn_ref, prev_ref, o_ref, *, ds):
    del prev_ref
    nh = B_HEADS
    stack = lambda ref: jnp.concatenate([ref[:, h * B_DH:(h + 1) * B_DH] for h in range(nh)], axis=0)
    q = stack(q_ref) * (B_DH ** -0.5 * LOG2E)
    rows = nh * ds
    n_c = kc_ref.shape[1]
    qh_c = lax.broadcasted_iota(jnp.int32, (rows, n_c), 0) // ds
    kh_c = lax.broadcasted_iota(jnp.int32, (rows, n_c), 1) % nh
    s_c = _dot3(q, kc_ref[0], _NT) - fc_ref[0]
    s_c = jnp.where(qh_c == kh_c, s_c, NEG)
    r_i = lax.broadcasted_iota(jnp.int32, (rows, rows), 0)
    c_i = lax.broadcasted_iota(jnp.int32, (rows, rows), 1)
    s_n = _dot3(q, stack(kn_ref), _NT) - fn_ref[0]
    s_n = jnp.where((r_i // ds == c_i // ds) & (c_i % ds <= r_i % ds), s_n, NEG)
    m = jnp.maximum(jnp.max(s_c, axis=1, keepdims=True), jnp.max(s_n, axis=1, keepdims=True))
    p_c = jnp.exp2(s_c - m)
    p_n = jnp.exp2(s_n - m)
    l = jnp.sum(p_c, axis=1, keepdims=True) + jnp.sum(p_n, axis=1, keepdims=True)
    o = (_dot3(p_c, vc_ref[0]) + _dot3(p_n, stack(vn_ref))) / l
    for h in range(nh):
        o_ref[:, h * B_DH:(h + 1) * B_DH] = o[h * ds:(h + 1) * ds]


def _fox_sample(pb, kc, vc, f_cache, f_new, prev_out, *, nb, ds, row_block0):
    n_c = kc.shape[1]
    base = (A_QK * 2 + A_V * 2) // B_W
    rb = lambda b: row_block0 + b
    return pl.pallas_call(
        functools.partial(_fox_sample_kernel, ds=ds),
        grid=(nb,),
        in_specs=[pl.BlockSpec((ds, B_W), lambda b: (rb(b), base)),
                  pl.BlockSpec((ds, B_W), lambda b: (rb(b), base + 1)),
                  pl.BlockSpec((ds, B_W), lambda b: (rb(b), base + 2)),
                  pl.BlockSpec((1, n_c, B_DH), lambda b: (b, 0, 0)),
                  pl.BlockSpec((1, n_c, B_DH), lambda b: (b, 0, 0)),
                  pl.BlockSpec((1, 1, n_c), lambda b: (b, 0, 0)),
                  pl.BlockSpec((1, 1, B_HEADS * ds), lambda b: (b, 0, 0)),
                  pl.BlockSpec(memory_space=pl.ANY)],
        out_specs=pl.BlockSpec((ds, B_W), lambda b: (rb(b), 0)),
        out_shape=jax.ShapeDtypeStruct(prev_out.shape, F32),
        input_output_aliases={7: 0},
        compiler_params=_params(("parallel",)),
        name="fox_sample",
    )(pb, pb, pb, kc, vc, f_cache, f_new, prev_out)


def _mlstm_kernel(q_ref, k_ref, v_ref, lir_ref, br_ref, lic_ref, bc_ref, c0_ref, n0_ref, m0_ref, *rest,
                  L, cps, aliased):
    if aliased:
        rest = rest[1:]
    h_ref, cout_ref, nout_ref, mout_ref, c_scr, n_scr, m_scr = rest
    step = pl.program_id(1)

    @pl.when(step == 0)
    def _():
        c_scr[...] = c0_ref[0]
        n_scr[...] = n0_ref[0]
        m_scr[...] = m0_ref[0]

    for j in range(cps):
        rows = pl.ds(j * L, L)
        _mlstm_chunk(q_ref.at[rows], k_ref.at[rows], v_ref.at[rows], lir_ref.at[0, j], br_ref.at[0, j],
                     lic_ref.at[0, j], bc_ref.at[0, j], h_ref.at[rows], c_scr, n_scr, m_scr, L=L)

    @pl.when(step == pl.num_programs(1) - 1)
    def _():
        cout_ref[0] = c_scr[...]
        nout_ref[0] = n_scr[...]
        mout_ref[0] = m_scr[...]


def _mlstm_chunk(q_ref, k_ref, v_ref, lir_ref, br_ref, lic_ref, bc_ref, h_ref, c_scr, n_scr, m_scr, *, L):
    nh = C_HEADS
    hl = nh * L
    seg = lax.broadcasted_iota(jnp.int32, (1, hl), 1) // L
    causal = (lax.broadcasted_iota(jnp.int32, (L, hl), 1) % L) <= lax.broadcasted_iota(jnp.int32, (L, hl), 0)

    def per_head(vals):
        out = vals[0]
        for h in range(1, nh):
            out = jnp.where(seg == h, vals[h], out)
        return out

    seg_max = lambda x, h: jnp.max(jnp.where(seg == h, x, -jnp.inf), axis=1, keepdims=True)

    qf = q_ref[...]
    kf = k_ref[...] * (C_DQK ** -0.5)
    qb = qf.astype(BF16)
    kb = kf.astype(BF16)
    vb = v_ref[...].astype(BF16)
    b_row = br_ref[...]
    li_row = lir_ref[...]
    b_col = [bc_ref[:, h:h + 1] for h in range(nh)]
    li_col = [lic_ref[:, h:h + 1] for h in range(nh)]
    m_prev = [m_scr[h:h + 1, :] for h in range(nh)]

    d = jnp.where(causal, per_head(b_col) - b_row + li_row, -jnp.inf)
    inter = [b_col[h] + m_prev[h] for h in range(nh)]
    m_t = [jnp.maximum(inter[h], seg_max(d, h)) for h in range(nh)]
    pm = jnp.exp(d - per_head(m_t))
    w_inter = [jnp.exp(inter[h] - m_t[h]) for h in range(nh)]

    khead = lax.broadcasted_iota(jnp.int32, (1, C_QK), 1) // C_DQK
    vhead = lax.broadcasted_iota(jnp.int32, (1, C_V), 1) // C_DV
    zero_b = jnp.zeros((), BF16)
    k_stack = jnp.concatenate([jnp.where(khead == h, kb, zero_b) for h in range(nh)], axis=0)
    v_bd = jnp.concatenate([jnp.where(vhead == h, vb, zero_b) for h in range(nh)], axis=0)
    sqk = _dot(qb, k_stack, _NT) * pm
    sv = _dot(sqk.astype(BF16), v_bd)

    b_last = [b_row[:, h * L + L - 1:h * L + L] for h in range(nh)]
    g_row = per_head(b_last) - b_row + li_row
    for h in range(nh):
        qk_sl = slice(h * C_DQK, (h + 1) * C_DQK)
        v_sl = slice(h * C_DV, (h + 1) * C_DV)
        c_prev = c_scr[h]
        n_prev = n_scr[h:h + 1, :]
        num = w_inter[h] * _dot(qb[:, qk_sl], c_prev.astype(BF16), _NT) + sv[:, v_sl]
        den = (w_inter[h] * jnp.sum(qf[:, qk_sl] * n_prev, axis=1, keepdims=True)
               + jnp.sum(jnp.where(seg == h, sqk, 0.0), axis=1, keepdims=True))
        h_ref[:, v_sl] = num / jnp.maximum(jnp.abs(den), jnp.exp(-m_t[h]))

        m_new = jnp.maximum(b_last[h] + m_prev[h], seg_max(g_row, h))
        w_c = jnp.exp(b_last[h] + m_prev[h] - m_new)
        kw = kf[:, qk_sl] * jnp.exp(b_last[h] - b_col[h] + li_col[h] - m_new)
        c_scr[h] = w_c * c_prev + _dot(vb[:, v_sl], kw.astype(BF16), _TN)
        n_scr[h:h + 1, :] = w_c * n_prev + jnp.sum(kw, axis=0, keepdims=True)
        m_scr[h:h + 1, :] = m_new


def _mlstm(p, li_row, b_row, li_col, b_col, c0, n0, m0, *, nb, nc, L, row_block0, n_total, prev_out=None):
    aliased = prev_out is not None
    cps = _chunks_per_step(nc, MLSTM_CHUNKS_PER_STEP)
    assert row_block0 % cps == 0
    nc, row_block0, chunk, L = nc // cps, row_block0 // cps, L, L * cps
    rb = lambda b, c: row_block0 + b * nc + c
    in_specs = [pl.BlockSpec((L, C_QK), lambda b, c: (rb(b, c), 0)),
                pl.BlockSpec((L, C_QK), lambda b, c: (rb(b, c), 1)),
                pl.BlockSpec((L, C_V), lambda b, c: (rb(b, c), 2 * C_QK // C_V)),
                pl.BlockSpec((1, cps, 1, C_HEADS * chunk), lambda b, c: (b, c, 0, 0)),
                pl.BlockSpec((1, cps, 1, C_HEADS * chunk), lambda b, c: (b, c, 0, 0)),
                pl.BlockSpec((1, cps, chunk, C_HEADS), lambda b, c: (b, c, 0, 0)),
                pl.BlockSpec((1, cps, chunk, C_HEADS), lambda b, c: (b, c, 0, 0)),
                pl.BlockSpec((1, C_HEADS, C_DV, C_DQK), lambda b, c: (b, 0, 0, 0)),
                pl.BlockSpec((1, C_HEADS, C_DQK), lambda b, c: (b, 0, 0)),
                pl.BlockSpec((1, C_HEADS, 1), lambda b, c: (b, 0, 0))]
    args = [p, p, p, li_row, b_row, li_col, b_col, c0, n0, m0]
    io_alias = {}
    if aliased:
        in_specs.append(pl.BlockSpec(memory_space=pl.ANY))
        args.append(prev_out)
        io_alias = {len(args) - 1: 0}
    return pl.pallas_call(
        functools.partial(_mlstm_kernel, L=chunk, cps=cps, aliased=aliased),
        grid=(nb, nc),
        in_specs=in_specs,
        out_specs=[pl.BlockSpec((L, C_V), lambda b, c: (rb(b, c), 0)),
                   pl.BlockSpec((1, C_HEADS, C_DV, C_DQK), lambda b, c: (b, 0, 0, 0)),
                   pl.BlockSpec((1, C_HEADS, C_DQK), lambda b, c: (b, 0, 0)),
                   pl.BlockSpec((1, C_HEADS, 1), lambda b, c: (b, 0, 0))],
        out_shape=[jax.ShapeDtypeStruct((n_total, C_V), F32),
                   jax.ShapeDtypeStruct((nb, C_HEADS, C_DV, C_DQK), F32),
                   jax.ShapeDtypeStruct((nb, C_HEADS, C_DQK), F32),
                   jax.ShapeDtypeStruct((nb, C_HEADS, 1), F32)],
        scratch_shapes=[pltpu.VMEM((C_HEADS, C_DV, C_DQK), F32),
                        pltpu.VMEM((C_HEADS, C_DQK), F32),
                        pltpu.VMEM((C_HEADS, 1), F32)],
        input_output_aliases=io_alias,
        compiler_params=_params(("parallel", "arbitrary")),
        name="mlstm_L%d" % chunk,
    )(*args)


def _mixout_kernel(*refs, hd, act, has_b, precise, n_prev):
    refs = list(refs)
    h_ref, a_ref, r_ref = refs[:3]
    b_ref = refs[3] if has_b else None
    k = 4 if has_b else 3
    ga_ref, wo_ref, gf_ref, wrh_ref, wrl_ref, br_ref, cnt0_ref = refs[k:k + 7]
    h1_ref, xn_ref, info_ref, cnt_ref, cnt_scr = refs[k + 7 + n_prev:]
    cast = (lambda x: x) if precise else (lambda x: x.astype(BF16))
    a = a_ref[...]
    r = r_ref[...]
    gate = r * _sigmoid(r) if act == "silu" else _sigmoid(r)
    parts = []
    for hh in range(a.shape[1] // hd):
        sl = slice(hh * hd, (hh + 1) * hd)
        parts.append(cast(_rms(a[:, sl], ga_ref[...]) * gate[:, sl]))
    if has_b:
        parts.append(cast(b_ref[...]))
    cat = jnp.concatenate(parts, axis=1)
    h1 = h_ref[...] + (_dot3(cat, wo_ref[...]) if precise else _dot(cat, wo_ref[...]))
    h1_ref[...] = h1
    xn = _rms(h1, gf_ref[...])
    xn_ref[...] = _pack_bf16_pairs(xn)
    xh, xl = _split(xn)
    logits = _dot(xh, wrh_ref[...]) + _dot(xh, wrl_ref[...]) + _dot(xl, wrh_ref[...]) + br_ref[...]

    lane = lax.broadcasted_iota(jnp.int32, logits.shape, 1)
    lanef = lane.astype(F32)
    is_g = lane < N_GROUPS
    gl = jnp.where(is_g, logits, -jnp.inf)
    gmax = jnp.max(gl, axis=1, keepdims=True)
    gidx = jnp.min(jnp.where(gl == gmax, lanef, float(LANES)), axis=1, keepdims=True)
    wg = 1.0 / jnp.sum(jnp.where(is_g, jnp.exp(gl - gmax), 0.0), axis=1, keepdims=True)
    lo = N_GROUPS + N_EXP * gidx
    el = jnp.where((lanef >= lo) & (lanef < lo + N_EXP), logits, -jnp.inf)
    m1 = jnp.max(el, axis=1, keepdims=True)
    i1 = jnp.min(jnp.where(el == m1, lanef, float(LANES)), axis=1, keepdims=True)
    el2 = jnp.where(lanef == i1, -jnp.inf, el)
    m2 = jnp.max(el2, axis=1, keepdims=True)
    i2 = jnp.min(jnp.where(el2 == m2, lanef, float(LANES)), axis=1, keepdims=True)
    t = jnp.exp(m2 - m1)
    w1 = wg / (1.0 + t)
    w2 = wg * t / (1.0 + t)
    e1 = i1 - N_GROUPS
    e2 = i2 - N_GROUPS

    @pl.when(pl.program_id(0) == 0)
    def _():
        cnt_scr[...] = cnt0_ref[...]

    tm = logits.shape[0]
    pick = jnp.where((lanef == e1) | (lanef == e2), 1.0, 0.0)
    earlier = (lax.broadcasted_iota(jnp.int32, (tm, tm), 1) < lax.broadcasted_iota(jnp.int32, (tm, tm), 0))
    before = _dot(jnp.where(earlier, 1.0, 0.0).astype(BF16), pick.astype(BF16)) + cnt_scr[...]
    r1 = jnp.sum(jnp.where(lanef == e1, before, 0.0), axis=1, keepdims=True)
    r2 = jnp.sum(jnp.where(lanef == e2, before, 0.0), axis=1, keepdims=True)
    cnt_new = cnt_scr[...] + jnp.sum(pick, axis=0, keepdims=True)
    cnt_scr[...] = cnt_new
    cnt_ref[...] = cnt_new
    info_ref[...] = jnp.where(lane == 0, e1, jnp.where(lane == 1, e2, jnp.where(lane == 2, w1, jnp.where(
        lane == 3, w2, jnp.where(lane == 4, r1, jnp.where(lane == 5, r2, 0.0))))))


def _mixout(h, a, a_col, r, r_col, b, g_head, w_o, g_ffn, wr_hi, wr_lo, b_r, counts0, *, hd, act,
            row0, n_rows, prev=None):
    n, d = h.shape
    precise = prev is not None
    tm = _tile(n_rows, 512, SUBLANES)
    assert row0 % tm == 0
    blk0 = row0 // tm
    wa = w_o.shape[0] if b is None else w_o.shape[0] - B_W
    has_b = b is not None
    row = lambda i: (blk0 + i, 0)
    const = lambda i: (0, 0)
    in_specs = [pl.BlockSpec((tm, d), row),
                pl.BlockSpec((tm, wa), lambda i: (blk0 + i, a_col)),
                pl.BlockSpec((tm, wa), lambda i: (blk0 + i, r_col))]
    args = [h, a, r]
    if has_b:
        in_specs.append(pl.BlockSpec((tm, B_W), row))
        args.append(b)
    in_specs += [pl.BlockSpec((1, hd), const), pl.BlockSpec(w_o.shape, const), pl.BlockSpec((1, d), const),
                 pl.BlockSpec((d, LANES), const), pl.BlockSpec((d, LANES), const), pl.BlockSpec((1, LANES), const),
                 pl.BlockSpec((1, LANES), const)]
    args += [g_head.reshape(1, hd), w_o if precise else w_o.astype(BF16), g_ffn.reshape(1, d), wr_hi, wr_lo, b_r,
             counts0]
    io_alias = {}
    if precise:
        for k, arr in enumerate(prev):
            in_specs.append(pl.BlockSpec(memory_space=pl.ANY))
            args.append(arr)
            io_alias[len(args) - 1] = k
    return pl.pallas_call(
        functools.partial(_mixout_kernel, hd=hd, act=act, has_b=has_b, precise=precise, n_prev=len(io_alias)),
        grid=(n_rows // tm,),
        in_specs=in_specs,
        out_specs=[pl.BlockSpec((tm, d), row), pl.BlockSpec((tm, d // 2), row), pl.BlockSpec((tm, LANES), row),
                   pl.BlockSpec((1, LANES), const)],
        out_shape=[jax.ShapeDtypeStruct((n, d), F32), jax.ShapeDtypeStruct((n, d // 2), jnp.int32),
                   jax.ShapeDtypeStruct((n, LANES), F32), jax.ShapeDtypeStruct((1, LANES), F32)],
        scratch_shapes=[pltpu.VMEM((1, LANES), F32)],
        input_output_aliases=io_alias,
        compiler_params=_params(("arbitrary",)),
        name="mixout_" + act + ("_precise" if precise else ""),
    )(*args)


def _sc_gather(table, idx):
    r = idx.shape[0]
    w = table.shape[1]
    n_workers = SC_CORES * SC_SUBCORES
    per_worker = r // n_workers
    step = SC_GATHER_ROWS * SC_GATHER_BUFS
    assert r % n_workers == 0 and per_worker % step == 0, (r, n_workers, step)
    mesh = plsc.VectorSubcoreMesh(core_axis_name="c", subcore_axis_name="s")

    @functools.partial(
        pl.kernel, mesh=mesh,
        out_type=jax.ShapeDtypeStruct((r, w), table.dtype),
        scratch_types=[pltpu.VMEM((SC_GATHER_BUFS, SC_GATHER_ROWS), jnp.int32),
                       pltpu.VMEM((SC_GATHER_BUFS, SC_GATHER_ROWS, w), table.dtype),
                       pltpu.SemaphoreType.DMA((SC_GATHER_BUFS,)),
                       pltpu.SemaphoreType.DMA((SC_GATHER_BUFS,))],
    )
    def gather(table_hbm, idx_hbm, out_hbm, idx_v, rows_v, gather_sem, store_sem):
        worker = lax.axis_index("s") * SC_CORES + lax.axis_index("c")
        base = worker * per_worker

        @pl.loop(0, per_worker // step)
        def _(j):
            off = pl.multiple_of(base + j * step, step)
            rows = lambda b: pl.ds(off + b * SC_GATHER_ROWS, SC_GATHER_ROWS)
            gathers, stores = [], []
            for b in range(SC_GATHER_BUFS):
                pltpu.sync_copy(idx_hbm.at[rows(b)], idx_v.at[b])
                gathers.append(pltpu.async_copy(table_hbm.at[idx_v.at[b]], rows_v.at[b], gather_sem.at[b]))
            for b in range(SC_GATHER_BUFS):
                gathers[b].wait()
                stores.append(pltpu.async_copy(rows_v.at[b], out_hbm.at[rows(b)], store_sem.at[b]))
            for b in range(SC_GATHER_BUFS):
                stores[b].wait()

    return gather(table, idx)


def _sc_scatter_pairs(rows, slots, n_slots):
    n, w = rows.shape
    n_workers = SC_CORES * SC_SUBCORES
    per_worker = n // n_workers
    step = SC_SCATTER_ROWS * SC_SCATTER_BUFS
    assert n % n_workers == 0 and per_worker % step == 0, (n, n_workers, step)
    mesh = plsc.VectorSubcoreMesh(core_axis_name="c", subcore_axis_name="s")

    @functools.partial(
        pl.kernel, mesh=mesh,
        out_type=jax.ShapeDtypeStruct((n_slots, w), rows.dtype),
        scratch_types=[pltpu.VMEM((2 * SC_SCATTER_BUFS, SC_SCATTER_ROWS), jnp.int32),
                       pltpu.VMEM((SC_SCATTER_BUFS, SC_SCATTER_ROWS, w), rows.dtype),
                       pltpu.SemaphoreType.DMA((SC_SCATTER_BUFS,)),
                       pltpu.SemaphoreType.DMA((SC_SCATTER_BUFS,))],
    )
    def scatter(rows_hbm, slots_hbm, out_hbm, idx_v, rows_v, load_sem, store_sem):
        worker = lax.axis_index("s") * SC_CORES + lax.axis_index("c")
        base = worker * per_worker

        @pl.loop(0, per_worker // step)
        def _(j):
            off = pl.multiple_of(base + j * step, SC_SCATTER_ROWS)
            loads, stores = [], []
            for b in range(SC_SCATTER_BUFS):
                r0 = off + b * SC_SCATTER_ROWS
                pltpu.sync_copy(slots_hbm.at[pl.ds(r0, SC_SCATTER_ROWS)], idx_v.at[2 * b])
                pltpu.sync_copy(slots_hbm.at[pl.ds(n + r0, SC_SCATTER_ROWS)], idx_v.at[2 * b + 1])
                loads.append(pltpu.async_copy(rows_hbm.at[pl.ds(r0, SC_SCATTER_ROWS)], rows_v.at[b], load_sem.at[b]))
            for b in range(SC_SCATTER_BUFS):
                loads[b].wait()
                for k in range(2):
                    stores.append(pltpu.async_copy(rows_v.at[b], out_hbm.at[idx_v.at[2 * b + k]], store_sem.at[b]))
            for copy in stores:
                copy.wait()

    return scatter(rows, slots)


def _expert_kernel(te_ref, nu_ref, x_ref, wg_ref, wu_ref, wd_ref, y_ref, wgb, wub, wdb):
    i = pl.program_id(0)
    live = i < nu_ref[0]
    new_expert = (i == 0) | (te_ref[i] != te_ref[jnp.maximum(i - 1, 0)])

    @pl.when(live & new_expert)
    def _():
        wgb[...] = wg_ref[0].astype(BF16)
        wub[...] = wu_ref[0].astype(BF16)
        wdb[...] = wd_ref[0].astype(BF16)

    @pl.when(live)
    def _():
        x = _unpack_bf16_pairs(x_ref[...]).astype(BF16)
        g = _dot(x, wgb[...])
        u = _dot(x, wub[...])
        y_ref[...] = _pack_bf16_pairs(_dot((g * _sigmoid(g) * u).astype(BF16), wdb[...]))

    @pl.when(jnp.logical_not(live))
    def _():
        y_ref[...] = jnp.zeros_like(y_ref)


def _experts(xs, tile_expert, n_used, w_gate, w_up, w_down, *, tm):
    n_slots, dp = xs.shape
    d, f = w_gate.shape[-2:]
    grid_spec = pltpu.PrefetchScalarGridSpec(
        num_scalar_prefetch=2,
        grid=(n_slots // tm,),
        in_specs=[pl.BlockSpec((tm, dp), lambda i, te, nu: (i, 0)),
                  pl.BlockSpec((1, d, f), lambda i, te, nu: (te[i], 0, 0)),
                  pl.BlockSpec((1, d, f), lambda i, te, nu: (te[i], 0, 0)),
                  pl.BlockSpec((1, f, d), lambda i, te, nu: (te[i], 0, 0))],
        out_specs=pl.BlockSpec((tm, dp), lambda i, te, nu: (i, 0)),
        scratch_shapes=[pltpu.VMEM((d, f), BF16), pltpu.VMEM((d, f), BF16), pltpu.VMEM((f, d), BF16)],
    )
    return pl.pallas_call(
        _expert_kernel,
        grid_spec=grid_spec,
        out_shape=jax.ShapeDtypeStruct((n_slots, dp), jnp.int32),
        compiler_params=_params(("arbitrary",)),
        name="moe_experts",
    )(tile_expert, n_used, xs, w_gate, w_up, w_down)


def _combine_kernel(h_ref, info_ref, y0_ref, y1_ref, o_ref):
    o_ref[...] = h_ref[...] + (info_ref[:, 2:3] * _unpack_bf16_pairs(y0_ref[...])
                               + info_ref[:, 3:4] * _unpack_bf16_pairs(y1_ref[...]))


def _combine(h, yg, info):
    n, d = h.shape
    tm = _tile(n, 512, SUBLANES)
    nt = n // tm
    return pl.pallas_call(
        _combine_kernel,
        grid=(nt,),
        in_specs=[pl.BlockSpec((tm, d), lambda i: (i, 0)),
                  pl.BlockSpec((tm, LANES), lambda i: (i, 0)),
                  pl.BlockSpec((tm, d // 2), lambda i: (i, 0)),
                  pl.BlockSpec((tm, d // 2), lambda i: (nt + i, 0))],
        out_specs=pl.BlockSpec((tm, d), lambda i: (i, 0)),
        out_shape=jax.ShapeDtypeStruct((n, d), F32),
        compiler_params=_params(("parallel",)),
        name="moe_combine",
    )(h, info, yg, yg)


def _final_kernel(*refs, sub, prompt_steps):
    g_ref = refs[0]
    h_refs, info_refs, y0_refs, y1_refs = (refs[1 + k * sub:1 + (k + 1) * sub] for k in range(4))
    yp_ref, ys_ref = refs[1 + 4 * sub:]
    tin = h_refs[0].shape[0]
    i = pl.program_id(0)

    def store(o_ref):
        for k in range(sub):
            h2 = h_refs[k][...] + (info_refs[k][:, 2:3] * _unpack_bf16_pairs(y0_refs[k][...])
                                   + info_refs[k][:, 3:4] * _unpack_bf16_pairs(y1_refs[k][...]))
            o_ref[k * tin:(k + 1) * tin, :] = _rms(h2, g_ref[...])

    pl.when(i < prompt_steps)(functools.partial(store, yp_ref))
    pl.when(i >= prompt_steps)(functools.partial(store, ys_ref))


def _final_combine(h, yg, info, g, *, nb, t, n_sample):
    n, d = h.shape
    seq = t - LEAD
    tin = _tile(math.gcd(LEAD, n_sample, seq), LANES, SUBLANES)
    sub = FINAL_SUBTILES if seq % (FINAL_SUBTILES * tin) == 0 and n_sample % (FINAL_SUBTILES * tin) == 0 else 1
    tout = sub * tin
    steps_per_batch = seq // tout
    prompt_steps = nb * steps_per_batch
    tiles_per_batch, lead_tiles, n_tiles = t // tin, LEAD // tin, n // tin

    def in_tile(k, offset=0):
        def index(i):
            ip = jnp.minimum(i, prompt_steps - 1)
            prompt = (ip // steps_per_batch) * tiles_per_batch + lead_tiles + (ip % steps_per_batch) * sub + k
            sample = nb * tiles_per_batch + (i - prompt_steps) * sub + k
            return offset + jnp.where(i < prompt_steps, prompt, sample), 0
        return index

    in_specs = ([pl.BlockSpec((1, d), lambda i: (0, 0))]
                + [pl.BlockSpec((tin, d), in_tile(k)) for k in range(sub)]
                + [pl.BlockSpec((tin, LANES), in_tile(k)) for k in range(sub)]
                + [pl.BlockSpec((tin, d // 2), in_tile(k)) for k in range(sub)]
                + [pl.BlockSpec((tin, d // 2), in_tile(k, n_tiles)) for k in range(sub)])
    return pl.pallas_call(
        functools.partial(_final_kernel, sub=sub, prompt_steps=prompt_steps),
        grid=(prompt_steps + n_sample // tout,),
        in_specs=in_specs,
        out_specs=[pl.BlockSpec((tout, d), lambda i: (jnp.minimum(i, prompt_steps - 1), 0)),
                   pl.BlockSpec((tout, d), lambda i: (jnp.maximum(i - prompt_steps, 0), 0))],
        out_shape=[jax.ShapeDtypeStruct((nb * seq, d), F32), jax.ShapeDtypeStruct((n_sample, d), F32)],
        compiler_params=_params(("arbitrary",)),
        name="moe_combine_final",
    )(g.reshape(1, d), *([h] * sub + [info] * sub + [yg] * (2 * sub)))


def _moe(h1, xn, info, counts_f, w_gate, w_up, w_down, expert_base, *, final=None):
    n, d = h1.shape
    n_e = N_GROUPS * N_EXP
    tm = _tile(2 * n, MOE_TILE, SUBLANES)
    n_tiles = (2 * n) // tm + n_e
    n_slots = n_tiles * tm
    experts = jnp.arange(n_e, dtype=jnp.int32)
    counts = counts_f[0, :n_e].astype(jnp.int32)
    padded = ((counts + tm - 1) // tm) * tm
    pad_ends = jnp.cumsum(padded)
    pad_starts = pad_ends - padded
    n_used = (pad_ends[-1] // tm).astype(jnp.int32)
    eid = info[:, 0:2].astype(jnp.int32)
    rank = info[:, 4:6].astype(jnp.int32)
    slot_of_pick = jnp.sum(jnp.where(eid[..., None] == experts, pad_starts, 0), axis=-1) + rank
    slots = slot_of_pick.T.reshape(-1)
    tile_starts = jnp.arange(n_tiles, dtype=jnp.int32) * tm
    tile_expert = expert_base + jnp.minimum(jnp.sum(tile_starts[:, None] >= pad_ends[None, :], axis=1), n_e - 1)

    xs = _sc_scatter_pairs(xn, slots, n_slots)
    ys = _experts(xs, tile_expert.astype(jnp.int32), n_used.reshape(1), w_gate, w_up, w_down, tm=tm)
    yg = _sc_gather(ys, slots)
    if final is not None:
        return _final_combine(h1, yg, info, final["g"], nb=final["nb"], t=final["t"], n_sample=final["n_sample"])
    return _combine(h1, yg, info)


def _router_weights(w_rg, b_rg, w_re, b_re):
    d = w_rg.shape[0]
    pad = LANES - N_GROUPS - N_GROUPS * N_EXP
    w = jnp.concatenate([w_rg, w_re, jnp.zeros((d, pad), F32)], axis=1)
    b = jnp.concatenate([b_rg, b_re, jnp.zeros((pad,), F32)]).reshape(1, LANES)
    hi = w.astype(BF16)
    lo = (w - hi.astype(F32)).astype(BF16)
    return hi, lo, b


def _rows_to_lanes(x, nb, t, nch):
    return x.reshape(nb, t, nch).transpose(0, 2, 1).reshape(nb * nch, t)


def _even_layer(h, dims, g_mix, w_in, w_g2, b_g, b_f, g_a, w_o, state_gla, ck, cv, clf):
    nb, t, db, ds, npr, n = dims
    d = h.shape[1]
    qa, ka, va, ra, ga, qb, kb, vb, fb = jnp.split(
        w_in, [A_QK, 2 * A_QK, 2 * A_QK + A_V, 2 * A_QK + 2 * A_V, 2 * A_QK + 2 * A_V + A_RANK,
               2 * A_QK + 2 * A_V + A_RANK + B_W, 2 * A_QK + 2 * A_V + A_RANK + 2 * B_W,
               2 * A_QK + 2 * A_V + A_RANK + 3 * B_W], axis=1)
    w_packed = jnp.concatenate(
        [qa, ka, va, ra, qb, kb, vb, ga, fb, jnp.zeros((d, LANES - A_RANK - B_HEADS), F32)], axis=1)
    q_off = 2 * A_QK + 2 * A_V
    colscale = jnp.ones((1, MAIN_W), F32).at[:, q_off:q_off + B_W].set(B_DH ** -0.5 * LOG2E)
    p, pb, k_rows, v_rows = _proj(h, g_mix, w_packed, n_rows=npr, colscale=colscale, kv_col=q_off + B_W)
    p = _proj_precise(h, g_mix, w_packed, p, row0=npr)

    nh = B_HEADS
    fcol = MAIN_W + A_RANK
    zeros_col = lambda r: jnp.zeros((r, 1), F32)
    bias_row = lambda lanes: jnp.tile(b_f, lanes // nh).reshape(1, lanes)
    fb_p = p[:npr, fcol:fcol + nh].reshape(nb, t * nh)
    logf_p, f_p = _gate_scan(fb_p, bias_row(t * nh), zeros_col(nb), mode="fox", act_start=0,
                             valid_start=N_PAD * nh, valid_end=t * nh, seg=None, stride=nh)
    past = ck.shape[1]
    n_c = past * nh
    x_s = _pad_lanes(jnp.concatenate([clf.reshape(db, n_c), p[npr:, fcol:fcol + nh].reshape(db, ds * nh)], axis=1))
    logf_s, f_s = _gate_scan(x_s, bias_row(x_s.shape[1]), zeros_col(db), mode="fox", act_start=n_c,
                             valid_start=0, valid_end=n_c + ds * nh, seg=None, stride=nh)

    nc = t // CHUNK
    oa, s_p = _gla(p, jnp.zeros((nb, A_HEADS, A_DK, A_DV), F32), w_g2, b_g,
                   nb=nb, nc=nc, L=CHUNK, row_block0=0, lead_pad=N_PAD, n_total=n)
    oa, s_s = _gla(p, state_gla, w_g2, b_g, nb=db, nc=1, L=ds, row_block0=npr // ds, lead_pad=0,
                   n_total=n, prev_out=oa, precise=True)

    fk = (f_p * LOG2E).reshape(nb, t, nh).transpose(0, 2, 1).reshape(nb, nh, 1, t)
    ob = _flash(pb, fk, nb=nb, t=t, n_total=n)
    f_cache = (f_s[:, :n_c] * LOG2E).reshape(db, 1, n_c)
    f_new = (f_s[:, n_c:n_c + ds * nh] * LOG2E).reshape(db, ds, nh).transpose(0, 2, 1).reshape(db, 1, nh * ds)
    ob = _fox_sample(p, ck.reshape(db, n_c, B_DH), cv.reshape(db, n_c, B_DH), f_cache, f_new, ob,
                     nb=db, ds=ds, row_block0=npr // ds)

    kcol = q_off + B_W
    states = dict(
        s_p=s_p, s_s=s_s,
        k_p=k_rows.reshape(nb, t, B_HEADS, B_DH)[:, N_PAD:],
        v_p=v_rows.reshape(nb, t, B_HEADS, B_DH)[:, N_PAD:],
        f_p=logf_p.reshape(nb, t, nh)[:, N_PAD:],
        k_s=p[npr:, kcol:kcol + B_W].reshape(db, ds, B_HEADS, B_DH),
        v_s=p[npr:, kcol + B_W:kcol + 2 * B_W].reshape(db, ds, B_HEADS, B_DH),
        f_s=logf_s[:, n_c:n_c + ds * nh].reshape(db, ds, nh))
    return (oa, 0, p, (2 * A_QK + A_V) // A_V, ob, g_a, w_o), states


def _chunk_rows(x, nb, nch, nc, L):
    x = x[:, :nc * L].reshape(nb, nch, nc, L)
    return x.transpose(0, 2, 1, 3).reshape(nb, nc, 1, nch * L), x.transpose(0, 2, 3, 1)


def _odd_layer(h, dims, g_mix, w_in, b_gate, g_c, w_o, c0, n0, m0):
    nb, t, db, ds, npr, n = dims
    d = h.shape[1]
    w_packed = jnp.concatenate(
        [w_in, jnp.zeros((d, LANES - 2 * C_HEADS), F32)], axis=1)
    (p,) = _proj(h, g_mix, w_packed, n_rows=npr)
    p = _proj_precise(h, g_mix, w_packed, p, row0=npr)

    ng = 2 * C_HEADS
    isf = (jnp.arange(ng) >= C_HEADS).astype(F32)
    nc = t // CHUNK

    def gates(rows, nbatch, tt, valid_start, seg):
        x = _pad_lanes(_rows_to_lanes(rows, nbatch, tt, ng))
        val, cum = _gate_scan(x, jnp.tile(b_gate, nbatch).reshape(-1, 1), jnp.tile(isf, nbatch).reshape(-1, 1),
                              mode="mlstm", act_start=0, valid_start=valid_start, valid_end=tt, seg=seg)
        return val.reshape(nbatch, ng, -1), cum.reshape(nbatch, ng, -1)

    val_p, cum_p = gates(p[:npr, MAIN_W:MAIN_W + ng], nb, t, N_PAD, CHUNK)
    val_s, cum_s = gates(p[npr:, MAIN_W:MAIN_W + ng], db, ds, 0, ds)

    def chunked(val, cum, nbatch, ncs, L):
        li_row, li_col = _chunk_rows(val[:, :C_HEADS].reshape(nbatch * C_HEADS, -1), nbatch, C_HEADS, ncs, L)
        b_row, b_col = _chunk_rows(cum[:, C_HEADS:].reshape(nbatch * C_HEADS, -1), nbatch, C_HEADS, ncs, L)
        return li_row, b_row, li_col, b_col

    zc = jnp.zeros((nb, C_HEADS, C_DV, C_DQK), F32)
    zn = jnp.zeros((nb, C_HEADS, C_DQK), F32)
    zm = jnp.zeros((nb, C_HEADS, 1), F32)
    hm, c_p, n_p, m_p = _mlstm(p, *chunked(val_p, cum_p, nb, nc, CHUNK), zc, zn, zm,
                               nb=nb, nc=nc, L=CHUNK, row_block0=0, n_total=n)
    hm, c_s, n_s, m_s = _mlstm(p, *chunked(val_s, cum_s, db, 1, ds), c0, n0, m0.reshape(db, C_HEADS, 1),
                               nb=db, nc=1, L=ds, row_block0=npr // ds, n_total=n, prev_out=hm)
    states = dict(c_p=c_p, n_p=n_p, m_p=m_p.reshape(nb, C_HEADS), c_s=c_s, n_s=n_s, m_s=m_s.reshape(db, C_HEADS))
    return (hm, 0, p, (2 * C_QK + C_V) // C_V, None, g_c, w_o), states


def kernel(x_prompt, x_sample, state_gla, cache_fox_k, cache_fox_v, cache_fox_logf, state_mlstm_c, state_mlstm_n, state_mlstm_m, meta_tokens, norm_mix, norm_ffn, norm_final, w_in_even, w_gla_gate2, b_gla_gate, b_fox_f, g_gla_out, w_out_even, w_in_odd, b_mlstm_gate, g_mlstm_out, w_out_odd, w_router_group, b_router_group, w_router_expert, b_router_expert, w_exp_gate, w_exp_up, w_exp_down):
    nb, seq, d = x_prompt.shape
    db, ds, _ = x_sample.shape
    t = LEAD + seq
    npr, nsm = nb * t, db * ds
    n = npr + nsm
    dims = (nb, t, db, ds, npr, n)
    depth = norm_mix.shape[0]
    n_e = N_GROUPS * N_EXP
    f = w_exp_gate.shape[-1]

    pad_rows = jnp.zeros((N_PAD, d), F32)
    h = jnp.concatenate([piece for b in range(nb) for piece in (pad_rows, meta_tokens, x_prompt[b])]
                        + [x_sample.reshape(nsm, d)], axis=0)
    wg_all = w_exp_gate.reshape(depth * n_e, d, f)
    wu_all = w_exp_up.reshape(depth * n_e, d, f)
    wd_all = w_exp_down.reshape(depth * n_e, f, d)

    even, odd = [], []
    for l in range(depth):
        if l % 2 == 0:
            e = l // 2
            mix, st = _even_layer(h, dims, norm_mix[l], w_in_even[e], w_gla_gate2[e], b_gla_gate[e], b_fox_f[e],
                                  g_gla_out[e], w_out_even[e], state_gla[e], cache_fox_k[e], cache_fox_v[e],
                                  cache_fox_logf[e])
            even.append(st)
            hd, act = A_DV, "silu"
        else:
            o = l // 2
            mix, st = _odd_layer(h, dims, norm_mix[l], w_in_odd[o], b_mlstm_gate[o], g_mlstm_out[o], w_out_odd[o],
                                 state_mlstm_c[o], state_mlstm_n[o], state_mlstm_m[o])
            odd.append(st)
            hd, act = C_DV, "sigmoid"
        a, a_col, r, r_col, b, g_head, w_o = mix
        wr_hi, wr_lo, b_r = _router_weights(w_router_group[l], b_router_group[l], w_router_expert[l],
                                            b_router_expert[l])
        mix_args = (h, a, a_col, r, r_col, b, g_head, w_o, norm_ffn[l], wr_hi, wr_lo, b_r)
        h1, xn, info, counts = _mixout(*mix_args, jnp.zeros((1, LANES), F32), hd=hd, act=act, row0=0, n_rows=npr)
        h1, xn, info, counts = _mixout(*mix_args, counts, hd=hd, act=act, row0=npr, n_rows=nsm,
                                       prev=(h1, xn, info))
        last = l == depth - 1
        h = _moe(h1, xn, info, counts, wg_all, wu_all, wd_all, l * n_e,
                 final=dict(g=norm_final, nb=nb, t=t, n_sample=nsm) if last else None)

    y_prompt = h[0].reshape(nb, seq, d)
    y_sample = h[1].reshape(db, ds, d)
    stack = lambda sts, key: jnp.stack([s[key] for s in sts])
    return (y_prompt, y_sample,
            stack(even, "s_p"), stack(even, "k_p"), stack(even, "v_p"), stack(even, "f_p"),
            stack(odd, "c_p"), stack(odd, "n_p"), stack(odd, "m_p"),
            stack(even, "s_s"), stack(even, "k_s"), stack(even, "v_s"), stack(even, "f_s"),
            stack(odd, "c_s"), stack(odd, "n_s"), stack(odd, "m_s"))
```

```python
import functools
import math

import jax
import jax.numpy as jnp
from jax import lax
from jax.experimental import pallas as pl
from jax.experimental.pallas import tpu as pltpu
from jax.experimental.pallas import tpu_sc as plsc

F32 = jnp.float32
BF16 = jnp.bfloat16

CHUNK = 64
N_META = 16
LEAD = 128
N_PAD = LEAD - N_META
A_HEADS, A_DK, A_DV, A_RANK = 4, 64, 128, 16
A_GATE_NORM = 16.0
B_HEADS, B_DH = 4, 128
C_HEADS, C_DQK, C_DV = 4, 128, 256
GATE_CAP = 15.0
N_GROUPS, N_EXP = 4, 8
EPS = 1e-6
NEG = -1e30
LOG2E = 1.4426950408889634
A_QK = A_HEADS * A_DK
A_V = A_HEADS * A_DV
B_W = B_HEADS * B_DH
C_QK = C_HEADS * C_DQK
C_V = C_HEADS * C_DV

LANES = 128
SUBLANES = 8
VMEM_LIMIT_BYTES = 56 * 1024 * 1024
SC_CORES, SC_SUBCORES = 2, 16
SC_GATHER_ROWS, SC_GATHER_BUFS = 64, 3
SC_SCATTER_ROWS, SC_SCATTER_BUFS = 32, 3
GLA_SUB = 16
GLA_CHUNKS_PER_STEP = 10
MLSTM_CHUNKS_PER_STEP = 1
FLASH_HEADS = 4
MOE_TILE = 1024
FINAL_SUBTILES = 4
MAIN_W = 3072
PROJ_W = MAIN_W + LANES

_NT = (((1,), (1,)), ((), ()))
_TN = (((0,), (0,)), ((), ()))
_NN = (((1,), (0,)), ((), ()))


def _params(sem):
    return pltpu.CompilerParams(dimension_semantics=sem, vmem_limit_bytes=VMEM_LIMIT_BYTES)


def _tile(n, pref, mult):
    t = (min(pref, n) // mult) * mult
    while t > mult and n % t:
        t -= mult
    assert t >= mult and n % t == 0, (n, pref, mult)
    return t


def _dot(a, b, dims=_NN):
    return lax.dot_general(a, b, dims, preferred_element_type=F32)


def _split(x):
    hi = x.astype(BF16)
    lo = (x - hi.astype(F32)).astype(BF16)
    return hi, lo


def _dot3(a, b, dims=_NN):
    ah, al = _split(a)
    bh, bl = _split(b)
    return _dot(ah, bh, dims) + _dot(ah, bl, dims) + _dot(al, bh, dims)


def _log_sigmoid(x):
    return jnp.minimum(x, 0.0) - jnp.log1p(jnp.exp(-jnp.abs(x)))


def _sigmoid(x):
    return 1.0 / (1.0 + jnp.exp(-x))


def _rms(x, g):
    return x * lax.rsqrt(jnp.mean(x * x, axis=-1, keepdims=True) + EPS) * g


def _pack_bf16_pairs(x):
    w = x.shape[1] // 2
    hi = lax.bitcast_convert_type(x[:, :w].astype(BF16).astype(F32), jnp.int32)
    lo = lax.bitcast_convert_type(x[:, w:].astype(BF16).astype(F32), jnp.int32)
    return hi | lax.shift_right_logical(lo, 16)


def _unpack_bf16_pairs(p):
    hi = lax.bitcast_convert_type(p & jnp.int32(-65536), F32)
    lo = lax.bitcast_convert_type(lax.shift_left(p, 16), F32)
    return jnp.concatenate([hi, lo], axis=1)


def _cumsum_rows(x):
    n = x.shape[0]
    row = lax.broadcasted_iota(jnp.int32, x.shape, 0)
    s = 1
    while s < n:
        x = x + jnp.where(row >= s, pltpu.roll(x, s, axis=0), 0.0)
        s *= 2
    return x


def _proj_kernel(x_ref, g_ref, w_ref, *rest, col_chunk, kv_col):
    if kv_col is None:
        (o_ref,) = rest
    else:
        cs_ref, o_ref, ob_ref, *kv_refs = rest
    tm = x_ref.shape[0]
    xn = _rms(x_ref[...], g_ref[...]).astype(BF16)
    for c0 in range(0, PROJ_W, col_chunk):
        c1 = min(c0 + col_chunk, PROJ_W)
        y = _dot(xn, w_ref[:, c0:c1])
        o_ref[:, c0:c1] = y
        if kv_col is not None:
            if c0 < MAIN_W:
                m1 = min(c1, MAIN_W)
                ob_ref[:, c0:m1] = (y[:, :m1 - c0] * cs_ref[:, c0:m1]).astype(BF16)
            for g0 in range(c0, c1, LANES):
                rel = g0 - kv_col
                if 0 <= rel < 2 * B_W:
                    head = (rel % B_W) // B_DH
                    kv_refs[rel // B_W][pl.ds(head, tm, stride=B_HEADS), :] = y[:, g0 - c0:g0 - c0 + LANES]


def _proj_precise_kernel(x_ref, g_ref, w_ref, prev_ref, o_ref, *, col_chunk):
    del prev_ref
    xn = _rms(x_ref[...], g_ref[...])
    for c0 in range(0, PROJ_W, col_chunk):
        c1 = min(c0 + col_chunk, PROJ_W)
        o_ref[:, c0:c1] = _dot3(xn, w_ref[:, c0:c1])


def _proj(h, g, w_packed, *, n_rows, colscale=None, kv_col=None):
    n, d = h.shape
    tm = _tile(n_rows, 512, 16)
    in_specs = [pl.BlockSpec((tm, d), lambda i: (i, 0)),
                pl.BlockSpec((1, d), lambda i: (0, 0)),
                pl.BlockSpec((d, PROJ_W), lambda i: (0, 0))]
    args = [h, g.reshape(1, d), w_packed.astype(BF16)]
    out_specs = [pl.BlockSpec((tm, PROJ_W), lambda i: (i, 0))]
    out_shape = [jax.ShapeDtypeStruct((n, PROJ_W), F32)]
    if kv_col is not None:
        in_specs.append(pl.BlockSpec((1, MAIN_W), lambda i: (0, 0)))
        args.append(colscale)
        out_specs += [pl.BlockSpec((tm, MAIN_W), lambda i: (i, 0))] + [pl.BlockSpec((tm * B_HEADS, B_DH), lambda i: (i, 0))] * 2
        out_shape += ([jax.ShapeDtypeStruct((n_rows, MAIN_W), BF16)]
                      + [jax.ShapeDtypeStruct((n_rows * B_HEADS, B_DH), F32)] * 2)
    return pl.pallas_call(
        functools.partial(_proj_kernel, col_chunk=640, kv_col=kv_col),
        grid=(n_rows // tm,),
        in_specs=in_specs,
        out_specs=out_specs,
        out_shape=out_shape,
        compiler_params=_params(("parallel",)),
        name="proj",
    )(*args)


def _proj_precise(h, g, w_packed, prev, *, row0):
    n, d = h.shape
    tm = _tile(n - row0, 512, SUBLANES)
    assert row0 % tm == 0
    return pl.pallas_call(
        functools.partial(_proj_precise_kernel, col_chunk=640),
        grid=((n - row0) // tm,),
        in_specs=[pl.BlockSpec((tm, d), lambda i: (row0 // tm + i, 0)),
                  pl.BlockSpec((1, d), lambda i: (0, 0)),
                  pl.BlockSpec((d, PROJ_W), lambda i: (0, 0)),
                  pl.BlockSpec(memory_space=pl.ANY)],
        out_specs=pl.BlockSpec((tm, PROJ_W), lambda i: (row0 // tm + i, 0)),
        out_shape=jax.ShapeDtypeStruct((n, PROJ_W), F32),
        input_output_aliases={3: 0},
        compiler_params=_params(("parallel",)),
        name="proj_precise",
    )(h, g.reshape(1, d), w_packed, prev)


def _gate_scan_kernel(x_ref, bias_ref, isf_ref, val_ref, cum_ref, *, mode, act_start, valid_start, valid_end, seg, stride):
    x = x_ref[...]
    lane = lax.broadcasted_iota(jnp.int32, x.shape, 1)
    valid = (lane >= valid_start) & (lane < valid_end)
    if mode == "fox":
        val = jnp.where(lane >= act_start, _log_sigmoid(x + bias_ref[...]), x)
        val = jnp.where(valid, val, 0.0)
        add = val
    else:
        gate = GATE_CAP * jnp.tanh((x + bias_ref[...]) / GATE_CAP)
        isf = isf_ref[...] > 0.5
        val = jnp.where(isf, jnp.where(valid, _log_sigmoid(gate), 0.0),
                        jnp.where(valid, gate, -jnp.inf))
        add = jnp.where(isf, val, 0.0)
    val_ref[...] = val
    n = x.shape[1]
    pos = lane if seg is None else lane % seg
    limit = n if seg is None else seg
    s = stride
    while s < limit:
        add = add + jnp.where(pos >= s, pltpu.roll(add, s, axis=1), 0.0)
        s *= 2
    cum_ref[...] = add


def _gate_scan(x, bias, isf, *, mode, act_start, valid_start, valid_end, seg, stride=1):
    r, n = x.shape
    full = lambda shape: pl.BlockSpec(shape, lambda i: (0,) * len(shape))
    return pl.pallas_call(
        functools.partial(_gate_scan_kernel, mode=mode, act_start=act_start,
                          valid_start=valid_start, valid_end=valid_end, seg=seg, stride=stride),
        grid=(1,),
        in_specs=[full((r, n)), full(bias.shape), full((r, 1))],
        out_specs=[full((r, n)), full((r, n))],
        out_shape=[jax.ShapeDtypeStruct((r, n), F32)] * 2,
        compiler_params=_params(("arbitrary",)),
        name="gate_scan_" + mode,
    )(x, bias, isf)


def _pad_lanes(x):
    n = x.shape[-1]
    m = -(-n // LANES) * LANES
    return x if m == n else jnp.pad(x, ((0, 0), (0, m - n)))


def _gla_kernel(qk_ref, v_ref, sm_ref, s0_ref, wg2_ref, wg2t_ref, bgr_ref, bgc_ref, *rest,
                L, cps, sub, lead_pad, aliased, precise):
    if aliased:
        rest = rest[1:]
    o_ref, sout_ref, s_scr = rest
    step = pl.program_id(1)
    for j in range(cps):
        rows = pl.ds(j * L, L)
        _gla_chunk(qk_ref.at[rows], v_ref.at[rows], sm_ref.at[rows], s0_ref, wg2_ref, wg2t_ref, bgr_ref, bgc_ref,
                   o_ref.at[rows], sout_ref, s_scr, c=step * cps + j, first=(j == 0), last=(j == cps - 1),
                   L=L, sub=sub, lead_pad=lead_pad, precise=precise)


def _gla_chunk(qk_ref, v_ref, sm_ref, s0_ref, wg2_ref, wg2t_ref, bgr_ref, bgc_ref, o_ref, sout_ref, s_scr, *,
               c, first, last, L, sub, lead_pad, precise):
    nh, dk, dv = A_HEADS, A_DK, A_DV
    cast = (lambda x: x) if precise else (lambda x: x.astype(BF16))
    mm = _dot3 if precise else _dot

    if first:
        @pl.when(pl.program_id(1) == 0)
        def _():
            s_scr[...] = jnp.zeros_like(s_scr)
            for h in range(nh):
                s_scr[h * dk:(h + 1) * dk, h * dv:(h + 1) * dv] = s0_ref[0, h]

    qk = qk_ref[...]
    q = qk[:, :A_QK] * (A_DK ** -0.5)
    k = qk[:, A_QK:]
    v = v_ref[...]
    ga = sm_ref[:, :A_RANK]
    row = lax.broadcasted_iota(jnp.int32, (L, 1), 0)
    valid = (c * L + row) >= lead_pad
    z = _dot3(ga, wg2_ref[...]) + bgr_ref[...]
    loga = jnp.where(valid, _log_sigmoid(z) / A_GATE_NORM, 0.0)
    k = jnp.where(valid, k, 0.0)
    b = _cumsum_rows(loga)
    b_last = b[L - 1:L, :]
    lane_t = lax.broadcasted_iota(jnp.int32, (1, L), 1)
    zt = _dot3(wg2t_ref[...], ga, _NT) + bgc_ref[...]
    logat = jnp.where((c * L + lane_t) >= lead_pad, _log_sigmoid(zt) / A_GATE_NORM, 0.0)
    b_last_col = jnp.sum(logat, axis=1, keepdims=True)

    qhead = lax.broadcasted_iota(jnp.int32, (1, A_QK), 1) // dk
    vhead = lax.broadcasted_iota(jnp.int32, (1, A_V), 1) // dv
    vb = cast(v)
    zero_b = jnp.zeros((), vb.dtype)
    v_bd = jnp.concatenate([jnp.where(vhead == h, vb, zero_b) for h in range(nh)], axis=0)

    rows_all = lax.broadcasted_iota(jnp.int32, (L, 1), 0)
    a_rows = []
    for i in range(L // sub):
        r0 = i * sub
        ci = jnp.zeros((1, A_QK), F32) if i == 0 else b[r0 - 1:r0, :]
        qt = cast(q[r0:r0 + sub] * jnp.exp(b[r0:r0 + sub] - ci))
        kt = cast(jnp.where(rows_all < r0 + sub, k * jnp.exp(ci - b), 0.0))
        k_stack = jnp.concatenate([jnp.where(qhead == h, kt, zero_b) for h in range(nh)], axis=0)
        a_rows.append(mm(qt, k_stack, _NT))
    a = a_rows[0] if len(a_rows) == 1 else jnp.concatenate(a_rows, axis=0)
    t_idx = lax.broadcasted_iota(jnp.int32, (L, nh * L), 0)
    s_idx = lax.broadcasted_iota(jnp.int32, (L, nh * L), 1) % L
    a = jnp.where(s_idx <= t_idx, a, 0.0)
    o_intra = mm(cast(a), v_bd)

    s_full = s_scr[...]
    o_inter = mm(cast(q * jnp.exp(b)), cast(s_full))
    o_ref[...] = o_inter + o_intra

    k_hat = cast(k * jnp.exp(b_last - b))
    upd = mm(k_hat, vb, _TN)
    khead_col = lax.broadcasted_iota(jnp.int32, (A_QK, 1), 0) // dk
    s_new = jnp.exp(b_last_col) * s_full + jnp.where(khead_col == vhead, upd, 0.0)
    s_scr[...] = s_new

    if last:
        @pl.when(pl.program_id(1) == pl.num_programs(1) - 1)
        def _():
            for h in range(nh):
                sout_ref[0, h] = s_new[h * dk:(h + 1) * dk, h * dv:(h + 1) * dv]


def _chunks_per_step(nc, want):
    return next(k for k in (want, 2, 1) if k <= want and nc % k == 0)


def _gla(p, s0, wg2, bg, *, nb, nc, L, row_block0, lead_pad, n_total, prev_out=None, precise=False):
    aliased = prev_out is not None
    cps = _chunks_per_step(nc, GLA_CHUNKS_PER_STEP)
    assert row_block0 % cps == 0
    nc, L, row_block0, chunk = nc // cps, L * cps, row_block0 // cps, L
    rb = lambda b, c: row_block0 + b * nc + c
    in_specs = [pl.BlockSpec((L, 2 * A_QK), lambda b, c: (rb(b, c), 0)),
                pl.BlockSpec((L, A_V), lambda b, c: (rb(b, c), 2 * A_QK // A_V)),
                pl.BlockSpec((L, LANES), lambda b, c: (rb(b, c), MAIN_W // LANES)),
                pl.BlockSpec((1, A_HEADS, A_DK, A_DV), lambda b, c: (b, 0, 0, 0)),
                pl.BlockSpec((A_RANK, A_QK), lambda b, c: (0, 0)),
                pl.BlockSpec((A_QK, A_RANK), lambda b, c: (0, 0)),
                pl.BlockSpec((1, A_QK), lambda b, c: (0, 0)),
                pl.BlockSpec((A_QK, 1), lambda b, c: (0, 0))]
    args = [p, p, p, s0, wg2, wg2.T, bg.reshape(1, A_QK), bg.reshape(A_QK, 1)]
    io_alias = {}
    if aliased:
        in_specs.append(pl.BlockSpec(memory_space=pl.ANY))
        args.append(prev_out)
        io_alias = {len(args) - 1: 0}
    return pl.pallas_call(
        functools.partial(_gla_kernel, L=chunk, cps=cps, sub=min(GLA_SUB, chunk), lead_pad=lead_pad,
                          aliased=aliased, precise=precise),
        grid=(nb, nc),
        in_specs=in_specs,
        out_specs=[pl.BlockSpec((L, A_V), lambda b, c: (rb(b, c), 0)),
                   pl.BlockSpec((1, A_HEADS, A_DK, A_DV), lambda b, c: (b, 0, 0, 0))],
        out_shape=[jax.ShapeDtypeStruct((n_total, A_V), F32),
                   jax.ShapeDtypeStruct((nb, A_HEADS, A_DK, A_DV), F32)],
        scratch_shapes=[pltpu.VMEM((A_QK, A_V), F32)],
        input_output_aliases=io_alias,
        compiler_params=_params(("parallel", "arbitrary")),
        name="gla_L%d" % chunk,
    )(*args)


def _flash_kernel(qi_ref, kj_ref, q_ref, k_ref, v_ref, fk_ref, o_ref, m_scr, l_scr, acc_scr, *, blk, lead_pad):
    step = pl.program_id(2)
    i = qi_ref[step]
    j = kj_ref[step]

    @pl.when(j == 0)
    def _():
        m_scr[...] = jnp.full_like(m_scr, -jnp.inf)
        l_scr[...] = jnp.zeros_like(l_scr)
        acc_scr[...] = jnp.zeros_like(acc_scr)

    def update(masked):
        for g in range(FLASH_HEADS):
            sl = slice(g * B_DH, (g + 1) * B_DH)
            s = _dot(q_ref[:, sl], k_ref[:, sl], _NT) - fk_ref[0, g]
            if masked:
                qpos = i * blk + lax.broadcasted_iota(jnp.int32, (blk, blk), 0)
                kpos = j * blk + lax.broadcasted_iota(jnp.int32, (blk, blk), 1)
                s = jnp.where((kpos <= qpos) & (kpos >= lead_pad), s, NEG)
            m_prev = m_scr[g]
            m_new = jnp.maximum(m_prev, jnp.max(s, axis=1, keepdims=True))
            alpha = jnp.exp2(m_prev - m_new)
            p = jnp.exp2(s - m_new)
            l_scr[g] = alpha * l_scr[g] + jnp.sum(p, axis=1, keepdims=True)
            acc_scr[g] = alpha * acc_scr[g] + _dot(p.astype(BF16), v_ref[:, sl])
            m_scr[g] = m_new

    edge = (j == i) | (j == 0)
    pl.when(edge)(functools.partial(update, True))
    pl.when(jnp.logical_not(edge))(functools.partial(update, False))

    @pl.when(j == i)
    def _():
        for g in range(FLASH_HEADS):
            o_ref[:, g * B_DH:(g + 1) * B_DH] = acc_scr[g] / l_scr[g]


def _flash(pb, fk, *, nb, t, n_total):
    blk = _tile(t, 640, LANES)
    nq = t // blk
    hg = FLASH_HEADS
    w = hg * B_DH
    pairs = [(i, j) for i in range(nq) for j in range(i + 1)]
    qi = jnp.asarray([p[0] for p in pairs], jnp.int32)
    kj = jnp.asarray([p[1] for p in pairs], jnp.int32)
    qc, kc, vc = (A_QK * 2 + A_V * 2) // w, (A_QK * 2 + A_V * 2 + B_W) // w, (A_QK * 2 + A_V * 2 + 2 * B_W) // w
    grid_spec = pltpu.PrefetchScalarGridSpec(
        num_scalar_prefetch=2,
        grid=(nb, B_HEADS // hg, len(pairs)),
        in_specs=[pl.BlockSpec((blk, w), lambda b, h, s, qi, kj: (b * nq + qi[s], qc + h)),
                  pl.BlockSpec((blk, w), lambda b, h, s, qi, kj: (b * nq + kj[s], kc + h)),
                  pl.BlockSpec((blk, w), lambda b, h, s, qi, kj: (b * nq + kj[s], vc + h)),
                  pl.BlockSpec((1, hg, 1, blk), lambda b, h, s, qi, kj: (b, h, 0, kj[s]))],
        out_specs=pl.BlockSpec((blk, w), lambda b, h, s, qi, kj: (b * nq + qi[s], h)),
        scratch_shapes=[pltpu.VMEM((hg, blk, 1), F32), pltpu.VMEM((hg, blk, 1), F32),
                        pltpu.VMEM((hg, blk, B_DH), F32)],
    )
    return pl.pallas_call(
        functools.partial(_flash_kernel, blk=blk, lead_pad=N_PAD),
        grid_spec=grid_spec,
        out_shape=jax.ShapeDtypeStruct((n_total, B_W), F32),
        compiler_params=_params(("parallel", "parallel", "arbitrary")),
        name="fox_flash",
    )(qi, kj, pb, pb, pb, fk)


def _fox_sample_kernel(q_ref, kn_ref, vn_ref, kc_ref, vc_ref, fc_ref, fn_ref, prev_ref, o_ref, *, ds):
    del prev_ref
    nh = B_HEADS
    stack = lambda ref: jnp.concatenate([ref[:, h * B_DH:(h + 1) * B_DH] for h in range(nh)], axis=0)
    q = stack(q_ref) * (B_DH ** -0.5 * LOG2E)
    rows = nh * ds
    n_c = kc_ref.shape[1]
    qh_c = lax.broadcasted_iota(jnp.int32, (rows, n_c), 0) // ds
    kh_c = lax.broadcasted_iota(jnp.int32, (rows, n_c), 1) % nh
    s_c = _dot3(q, kc_ref[0], _NT) - fc_ref[0]
    s_c = jnp.where(qh_c == kh_c, s_c, NEG)
    r_i = lax.broadcasted_iota(jnp.int32, (rows, rows), 0)
    c_i = lax.broadcasted_iota(jnp.int32, (rows, rows), 1)
    s_n = _dot3(q, stack(kn_ref), _NT) - fn_ref[0]
    s_n = jnp.where((r_i // ds == c_i // ds) & (c_i % ds <= r_i % ds), s_n, NEG)
    m = jnp.maximum(jnp.max(s_c, axis=1, keepdims=True), jnp.max(s_n, axis=1, keepdims=True))
    p_c = jnp.exp2(s_c - m)
    p_n = jnp.exp2(s_n - m)
    l = jnp.sum(p_c, axis=1, keepdims=True) + jnp.sum(p_n, axis=1, keepdims=True)
    o = (_dot3(p_c, vc_ref[0]) + _dot3(p_n, stack(vn_ref))) / l
    for h in range(nh):
        o_ref[:, h * B_DH:(h + 1) * B_DH] = o[h * ds:(h + 1) * ds]


def _fox_sample(pb, kc, vc, f_cache, f_new, prev_out, *, nb, ds, row_block0):
    n_c = kc.shape[1]
    base = (A_QK * 2 + A_V * 2) // B_W
    rb = lambda b: row_block0 + b
    return pl.pallas_call(
        functools.partial(_fox_sample_kernel, ds=ds),
        grid=(nb,),
        in_specs=[pl.BlockSpec((ds, B_W), lambda b: (rb(b), base)),
                  pl.BlockSpec((ds, B_W), lambda b: (rb(b), base + 1)),
                  pl.BlockSpec((ds, B_W), lambda b: (rb(b), base + 2)),
                  pl.BlockSpec((1, n_c, B_DH), lambda b: (b, 0, 0)),
                  pl.BlockSpec((1, n_c, B_DH), lambda b: (b, 0, 0)),
                  pl.BlockSpec((1, 1, n_c), lambda b: (b, 0, 0)),
                  pl.BlockSpec((1, 1, B_HEADS * ds), lambda b: (b, 0, 0)),
                  pl.BlockSpec(memory_space=pl.ANY)],
        out_specs=pl.BlockSpec((ds, B_W), lambda b: (rb(b), 0)),
        out_shape=jax.ShapeDtypeStruct(prev_out.shape, F32),
        input_output_aliases={7: 0},
        compiler_params=_params(("parallel",)),
        name="fox_sample",
    )(pb, pb, pb, kc, vc, f_cache, f_new, prev_out)


def _mlstm_kernel(q_ref, k_ref, v_ref, lir_ref, br_ref, lic_ref, bc_ref, c0_ref, n0_ref, m0_ref, *rest,
                  L, cps, aliased):
    if aliased:
        rest = rest[1:]
    h_ref, cout_ref, nout_ref, mout_ref, c_scr, n_scr, m_scr = rest
    step = pl.program_id(1)

    @pl.when(step == 0)
    def _():
        c_scr[...] = c0_ref[0]
        n_scr[...] = n0_ref[0]
        m_scr[...] = m0_ref[0]

    for j in range(cps):
        rows = pl.ds(j * L, L)
        _mlstm_chunk(q_ref.at[rows], k_ref.at[rows], v_ref.at[rows], lir_ref.at[0, j], br_ref.at[0, j],
                     lic_ref.at[0, j], bc_ref.at[0, j], h_ref.at[rows], c_scr, n_scr, m_scr, L=L)

    @pl.when(step == pl.num_programs(1) - 1)
    def _():
        cout_ref[0] = c_scr[...]
        nout_ref[0] = n_scr[...]
        mout_ref[0] = m_scr[...]


def _mlstm_chunk(q_ref, k_ref, v_ref, lir_ref, br_ref, lic_ref, bc_ref, h_ref, c_scr, n_scr, m_scr, *, L):
    nh = C_HEADS
    hl = nh * L
    seg = lax.broadcasted_iota(jnp.int32, (1, hl), 1) // L
    causal = (lax.broadcasted_iota(jnp.int32, (L, hl), 1) % L) <= lax.broadcasted_iota(jnp.int32, (L, hl), 0)

    def per_head(vals):
        out = vals[0]
        for h in range(1, nh):
            out = jnp.where(seg == h, vals[h], out)
        return out

    seg_max = lambda x, h: jnp.max(jnp.where(seg == h, x, -jnp.inf), axis=1, keepdims=True)

    qf = q_ref[...]
    kf = k_ref[...] * (C_DQK ** -0.5)
    qb = qf.astype(BF16)
    kb = kf.astype(BF16)
    vb = v_ref[...].astype(BF16)
    b_row = br_ref[...]
    li_row = lir_ref[...]
    b_col = [bc_ref[:, h:h + 1] for h in range(nh)]
    li_col = [lic_ref[:, h:h + 1] for h in range(nh)]
    m_prev = [m_scr[h:h + 1, :] for h in range(nh)]

    d = jnp.where(causal, per_head(b_col) - b_row + li_row, -jnp.inf)
    inter = [b_col[h] + m_prev[h] for h in range(nh)]
    m_t = [jnp.maximum(inter[h], seg_max(d, h)) for h in range(nh)]
    pm = jnp.exp(d - per_head(m_t))
    w_inter = [jnp.exp(inter[h] - m_t[h]) for h in range(nh)]

    khead = lax.broadcasted_iota(jnp.int32, (1, C_QK), 1) // C_DQK
    vhead = lax.broadcasted_iota(jnp.int32, (1, C_V), 1) // C_DV
    zero_b = jnp.zeros((), BF16)
    k_stack = jnp.concatenate([jnp.where(khead == h, kb, zero_b) for h in range(nh)], axis=0)
    v_bd = jnp.concatenate([jnp.where(vhead == h, vb, zero_b) for h in range(nh)], axis=0)
    sqk = _dot(qb, k_stack, _NT) * pm
    sv = _dot(sqk.astype(BF16), v_bd)

    b_last = [b_row[:, h * L + L - 1:h * L + L] for h in range(nh)]
    g_row = per_head(b_last) - b_row + li_row
    for h in range(nh):
        qk_sl = slice(h * C_DQK, (h + 1) * C_DQK)
        v_sl = slice(h * C_DV, (h + 1) * C_DV)
        c_prev = c_scr[h]
        n_prev = n_scr[h:h + 1, :]
        num = w_inter[h] * _dot(qb[:, qk_sl], c_prev.astype(BF16), _NT) + sv[:, v_sl]
        den = (w_inter[h] * jnp.sum(qf[:, qk_sl] * n_prev, axis=1, keepdims=True)
               + jnp.sum(jnp.where(seg == h, sqk, 0.0), axis=1, keepdims=True))
        h_ref[:, v_sl] = num / jnp.maximum(jnp.abs(den), jnp.exp(-m_t[h]))

        m_new = jnp.maximum(b_last[h] + m_prev[h], seg_max(g_row, h))
        w_c = jnp.exp(b_last[h] + m_prev[h] - m_new)
        kw = kf[:, qk_sl] * jnp.exp(b_last[h] - b_col[h] + li_col[h] - m_new)
        c_scr[h] = w_c * c_prev + _dot(vb[:, v_sl], kw.astype(BF16), _TN)
        n_scr[h:h + 1, :] = w_c * n_prev + jnp.sum(kw, axis=0, keepdims=True)
        m_scr[h:h + 1, :] = m_new


def _mlstm(p, li_row, b_row, li_col, b_col, c0, n0, m0, *, nb, nc, L, row_block0, n_total, prev_out=None):
    aliased = prev_out is not None
    cps = _chunks_per_step(nc, MLSTM_CHUNKS_PER_STEP)
    assert row_block0 % cps == 0
    nc, row_block0, chunk, L = nc // cps, row_block0 // cps, L, L * cps
    rb = lambda b, c: row_block0 + b * nc + c
    in_specs = [pl.BlockSpec((L, C_QK), lambda b, c: (rb(b, c), 0)),
                pl.BlockSpec((L, C_QK), lambda b, c: (rb(b, c), 1)),
                pl.BlockSpec((L, C_V), lambda b, c: (rb(b, c), 2 * C_QK // C_V)),
                pl.BlockSpec((1, cps, 1, C_HEADS * chunk), lambda b, c: (b, c, 0, 0)),
                pl.BlockSpec((1, cps, 1, C_HEADS * chunk), lambda b, c: (b, c, 0, 0)),
                pl.BlockSpec((1, cps, chunk, C_HEADS), lambda b, c: (b, c, 0, 0)),
                pl.BlockSpec((1, cps, chunk, C_HEADS), lambda b, c: (b, c, 0, 0)),
                pl.BlockSpec((1, C_HEADS, C_DV, C_DQK), lambda b, c: (b, 0, 0, 0)),
                pl.BlockSpec((1, C_HEADS, C_DQK), lambda b, c: (b, 0, 0)),
                pl.BlockSpec((1, C_HEADS, 1), lambda b, c: (b, 0, 0))]
    args = [p, p, p, li_row, b_row, li_col, b_col, c0, n0, m0]
    io_alias = {}
    if aliased:
        in_specs.append(pl.BlockSpec(memory_space=pl.ANY))
        args.append(prev_out)
        io_alias = {len(args) - 1: 0}
    return pl.pallas_call(
        functools.partial(_mlstm_kernel, L=chunk, cps=cps, aliased=aliased),
        grid=(nb, nc),
        in_specs=in_specs,
        out_specs=[pl.BlockSpec((L, C_V), lambda b, c: (rb(b, c), 0)),
                   pl.BlockSpec((1, C_HEADS, C_DV, C_DQK), lambda b, c: (b, 0, 0, 0)),
                   pl.BlockSpec((1, C_HEADS, C_DQK), lambda b, c: (b, 0, 0)),
                   pl.BlockSpec((1, C_HEADS, 1), lambda b, c: (b, 0, 0))],
        out_shape=[jax.ShapeDtypeStruct((n_total, C_V), F32),
                   jax.ShapeDtypeStruct((nb, C_HEADS, C_DV, C_DQK), F32),
                   jax.ShapeDtypeStruct((nb, C_HEADS, C_DQK), F32),
                   jax.ShapeDtypeStruct((nb, C_HEADS, 1), F32)],
        scratch_shapes=[pltpu.VMEM((C_HEADS, C_DV, C_DQK), F32),
                        pltpu.VMEM((C_HEADS, C_DQK), F32),
                        pltpu.VMEM((C_HEADS, 1), F32)],
        input_output_aliases=io_alias,
        compiler_params=_params(("parallel", "arbitrary")),
        name="mlstm_L%d" % chunk,
    )(*args)


def _mixout_kernel(*refs, hd, act, has_b, precise, n_prev):
    refs = list(refs)
    h_ref, a_ref, r_ref = refs[:3]
    b_ref = refs[3] if has_b else None
    k = 4 if has_b else 3
    ga_ref, wo_ref, gf_ref, wrh_ref, wrl_ref, br_ref, cnt0_ref = refs[k:k + 7]
    h1_ref, xn_ref, info_ref, cnt_ref, cnt_scr = refs[k + 7 + n_prev:]
    cast = (lambda x: x) if precise else (lambda x: x.astype(BF16))
    a = a_ref[...]
    r = r_ref[...]
    gate = r * _sigmoid(r) if act == "silu" else _sigmoid(r)
    parts = []
    for hh in range(a.shape[1] // hd):
        sl = slice(hh * hd, (hh + 1) * hd)
        parts.append(cast(_rms(a[:, sl], ga_ref[...]) * gate[:, sl]))
    if has_b:
        parts.append(cast(b_ref[...]))
    cat = jnp.concatenate(parts, axis=1)
    h1 = h_ref[...] + (_dot3(cat, wo_ref[...]) if precise else _dot(cat, wo_ref[...]))
    h1_ref[...] = h1
    xn = _rms(h1, gf_ref[...])
    xn_ref[...] = _pack_bf16_pairs(xn)
    xh, xl = _split(xn)
    hh_hl = _dot(xh, jnp.concatenate([wrh_ref[...], wrl_ref[...]], axis=1))
    logits = hh_hl[:, :LANES] + hh_hl[:, LANES:] + _dot(xl, wrh_ref[...]) + br_ref[...]

    lane = lax.broadcasted_iota(jnp.int32, logits.shape, 1)
    lanef = lane.astype(F32)
    is_g = lane < N_GROUPS
    gl = jnp.where(is_g, logits, -jnp.inf)
    gmax = jnp.max(gl, axis=1, keepdims=True)
    gidx = jnp.min(jnp.where(gl == gmax, lanef, float(LANES)), axis=1, keepdims=True)
    wg = 1.0 / jnp.sum(jnp.where(is_g, jnp.exp(gl - gmax), 0.0), axis=1, keepdims=True)
    lo = N_GROUPS + N_EXP * gidx
    el = jnp.where((lanef >= lo) & (lanef < lo + N_EXP), logits, -jnp.inf)
    m1 = jnp.max(el, axis=1, keepdims=True)
    i1 = jnp.min(jnp.where(el == m1, lanef, float(LANES)), axis=1, keepdims=True)
    el2 = jnp.where(lanef == i1, -jnp.inf, el)
    m2 = jnp.max(el2, axis=1, keepdims=True)
    i2 = jnp.min(jnp.where(el2 == m2, lanef, float(LANES)), axis=1, keepdims=True)
    t = jnp.exp(m2 - m1)
    w1 = wg / (1.0 + t)
    w2 = wg * t / (1.0 + t)
    e1 = i1 - N_GROUPS
    e2 = i2 - N_GROUPS

    @pl.when(pl.program_id(0) == 0)
    def _():
        cnt_scr[...] = cnt0_ref[...]

    tm = logits.shape[0]
    pick = jnp.where((lanef == e1) | (lanef == e2), 1.0, 0.0)
    earlier = (lax.broadcasted_iota(jnp.int32, (tm, tm), 1) < lax.broadcasted_iota(jnp.int32, (tm, tm), 0))
    before = _dot(jnp.where(earlier, 1.0, 0.0).astype(BF16), pick.astype(BF16)) + cnt_scr[...]
    r1 = jnp.sum(jnp.where(lanef == e1, before, 0.0), axis=1, keepdims=True)
    r2 = jnp.sum(jnp.where(lanef == e2, before, 0.0), axis=1, keepdims=True)
    cnt_new = cnt_scr[...] + jnp.sum(pick, axis=0, keepdims=True)
    cnt_scr[...] = cnt_new
    cnt_ref[...] = cnt_new
    info_ref[...] = jnp.where(lane == 0, e1, jnp.where(lane == 1, e2, jnp.where(lane == 2, w1, jnp.where(
        lane == 3, w2, jnp.where(lane == 4, r1, jnp.where(lane == 5, r2, 0.0))))))


def _mixout(h, a, a_col, r, r_col, b, g_head, w_o, g_ffn, wr_hi, wr_lo, b_r, counts0, *, hd, act,
            row0, n_rows, prev=None):
    n, d = h.shape
    precise = prev is not None
    tm = _tile(n_rows, 512, SUBLANES)
    assert row0 % tm == 0
    blk0 = row0 // tm
    wa = w_o.shape[0] if b is None else w_o.shape[0] - B_W
    has_b = b is not None
    row = lambda i: (blk0 + i, 0)
    const = lambda i: (0, 0)
    in_specs = [pl.BlockSpec((tm, d), row),
                pl.BlockSpec((tm, wa), lambda i: (blk0 + i, a_col)),
                pl.BlockSpec((tm, wa), lambda i: (blk0 + i, r_col))]
    args = [h, a, r]
    if has_b:
        in_specs.append(pl.BlockSpec((tm, B_W), row))
        args.append(b)
    in_specs += [pl.BlockSpec((1, hd), const), pl.BlockSpec(w_o.shape, const), pl.BlockSpec((1, d), const),
                 pl.BlockSpec((d, LANES), const), pl.BlockSpec((d, LANES), const), pl.BlockSpec((1, LANES), const),
                 pl.BlockSpec((1, LANES), const)]
    args += [g_head.reshape(1, hd), w_o if precise else w_o.astype(BF16), g_ffn.reshape(1, d), wr_hi, wr_lo, b_r,
             counts0]
    io_alias = {}
    if precise:
        for k, arr in enumerate(prev):
            in_specs.append(pl.BlockSpec(memory_space=pl.ANY))
            args.append(arr)
            io_alias[len(args) - 1] = k
    return pl.pallas_call(
        functools.partial(_mixout_kernel, hd=hd, act=act, has_b=has_b, precise=precise, n_prev=len(io_alias)),
        grid=(n_rows // tm,),
        in_specs=in_specs,
        out_specs=[pl.BlockSpec((tm, d), row), pl.BlockSpec((tm, d // 2), row), pl.BlockSpec((tm, LANES), row),
                   pl.BlockSpec((1, LANES), const)],
        out_shape=[jax.ShapeDtypeStruct((n, d), F32), jax.ShapeDtypeStruct((n, d // 2), jnp.int32),
                   jax.ShapeDtypeStruct((n, LANES), F32), jax.ShapeDtypeStruct((1, LANES), F32)],
        scratch_shapes=[pltpu.VMEM((1, LANES), F32)],
        input_output_aliases=io_alias,
        compiler_params=_params(("arbitrary",)),
        name="mixout_" + act + ("_precise" if precise else ""),
    )(*args)


def _sc_gather(table, idx):
    r = idx.shape[0]
    w = table.shape[1]
    n_workers = SC_CORES * SC_SUBCORES
    per_worker = r // n_workers
    step = SC_GATHER_ROWS * SC_GATHER_BUFS
    assert r % n_workers == 0 and per_worker % step == 0, (r, n_workers, step)
    mesh = plsc.VectorSubcoreMesh(core_axis_name="c", subcore_axis_name="s")

    @functools.partial(
        pl.kernel, mesh=mesh,
        out_type=jax.ShapeDtypeStruct((r, w), table.dtype),
        scratch_types=[pltpu.VMEM((SC_GATHER_BUFS, SC_GATHER_ROWS), jnp.int32),
                       pltpu.VMEM((SC_GATHER_BUFS, SC_GATHER_ROWS, w), table.dtype),
                       pltpu.SemaphoreType.DMA((SC_GATHER_BUFS,)),
                       pltpu.SemaphoreType.DMA((SC_GATHER_BUFS,))],
    )
    def gather(table_hbm, idx_hbm, out_hbm, idx_v, rows_v, gather_sem, store_sem):
        worker = lax.axis_index("s") * SC_CORES + lax.axis_index("c")
        base = worker * per_worker

        @pl.loop(0, per_worker // step)
        def _(j):
            off = pl.multiple_of(base + j * step, step)
            rows = lambda b: pl.ds(off + b * SC_GATHER_ROWS, SC_GATHER_ROWS)
            gathers, stores = [], []
            for b in range(SC_GATHER_BUFS):
                pltpu.sync_copy(idx_hbm.at[rows(b)], idx_v.at[b])
                gathers.append(pltpu.async_copy(table_hbm.at[idx_v.at[b]], rows_v.at[b], gather_sem.at[b]))
            for b in range(SC_GATHER_BUFS):
                gathers[b].wait()
                stores.append(pltpu.async_copy(rows_v.at[b], out_hbm.at[rows(b)], store_sem.at[b]))
            for b in range(SC_GATHER_BUFS):
                stores[b].wait()

    return gather(table, idx)


def _sc_scatter_pairs(rows, slots, n_slots):
    n, w = rows.shape
    n_workers = SC_CORES * SC_SUBCORES
    per_worker = n // n_workers
    step = SC_SCATTER_ROWS * SC_SCATTER_BUFS
    assert n % n_workers == 0 and per_worker % step == 0, (n, n_workers, step)
    mesh = plsc.VectorSubcoreMesh(core_axis_name="c", subcore_axis_name="s")

    @functools.partial(
        pl.kernel, mesh=mesh,
        out_type=jax.ShapeDtypeStruct((n_slots, w), rows.dtype),
        scratch_types=[pltpu.VMEM((2 * SC_SCATTER_BUFS, SC_SCATTER_ROWS), jnp.int32),
                       pltpu.VMEM((SC_SCATTER_BUFS, SC_SCATTER_ROWS, w), rows.dtype),
                       pltpu.SemaphoreType.DMA((SC_SCATTER_BUFS,)),
                       pltpu.SemaphoreType.DMA((SC_SCATTER_BUFS,))],
    )
    def scatter(rows_hbm, slots_hbm, out_hbm, idx_v, rows_v, load_sem, store_sem):
        worker = lax.axis_index("s") * SC_CORES + lax.axis_index("c")
        base = worker * per_worker

        @pl.loop(0, per_worker // step)
        def _(j):
            off = pl.multiple_of(base + j * step, SC_SCATTER_ROWS)
            loads, stores = [], []
            for b in range(SC_SCATTER_BUFS):
                r0 = off + b * SC_SCATTER_ROWS
                pltpu.sync_copy(slots_hbm.at[pl.ds(r0, SC_SCATTER_ROWS)], idx_v.at[2 * b])
                pltpu.sync_copy(slots_hbm.at[pl.ds(n + r0, SC_SCATTER_ROWS)], idx_v.at[2 * b + 1])
                loads.append(pltpu.async_copy(rows_hbm.at[pl.ds(r0, SC_SCATTER_ROWS)], rows_v.at[b], load_sem.at[b]))
            for b in range(SC_SCATTER_BUFS):
                loads[b].wait()
                for k in range(2):
                    stores.append(pltpu.async_copy(rows_v.at[b], out_hbm.at[idx_v.at[2 * b + k]], store_sem.at[b]))
            for copy in stores:
                copy.wait()

    return scatter(rows, slots)


def _expert_kernel(te_ref, nu_ref, x_ref, wg_ref, wu_ref, wd_ref, y_ref, wgb, wub, wdb):
    i = pl.program_id(0)
    live = i < nu_ref[0]
    new_expert = (i == 0) | (te_ref[i] != te_ref[jnp.maximum(i - 1, 0)])

    @pl.when(live & new_expert)
    def _():
        wgb[...] = wg_ref[0].astype(BF16)
        wub[...] = wu_ref[0].astype(BF16)
        wdb[...] = wd_ref[0].astype(BF16)

    @pl.when(live)
    def _():
        x = _unpack_bf16_pairs(x_ref[...]).astype(BF16)
        g = _dot(x, wgb[...])
        u = _dot(x, wub[...])
        y_ref[...] = _pack_bf16_pairs(_dot((g * _sigmoid(g) * u).astype(BF16), wdb[...]))

    @pl.when(jnp.logical_not(live))
    def _():
        y_ref[...] = jnp.zeros_like(y_ref)


def _experts(xs, tile_expert, n_used, w_gate, w_up, w_down, *, tm):
    n_slots, dp = xs.shape
    d, f = w_gate.shape[-2:]
    grid_spec = pltpu.PrefetchScalarGridSpec(
        num_scalar_prefetch=2,
        grid=(n_slots // tm,),
        in_specs=[pl.BlockSpec((tm, dp), lambda i, te, nu: (i, 0)),
                  pl.BlockSpec((1, d, f), lambda i, te, nu: (te[i], 0, 0)),
                  pl.BlockSpec((1, d, f), lambda i, te, nu: (te[i], 0, 0)),
                  pl.BlockSpec((1, f, d), lambda i, te, nu: (te[i], 0, 0))],
        out_specs=pl.BlockSpec((tm, dp), lambda i, te, nu: (i, 0)),
        scratch_shapes=[pltpu.VMEM((d, f), BF16), pltpu.VMEM((d, f), BF16), pltpu.VMEM((f, d), BF16)],
    )
    return pl.pallas_call(
        _expert_kernel,
        grid_spec=grid_spec,
        out_shape=jax.ShapeDtypeStruct((n_slots, dp), jnp.int32),
        compiler_params=_params(("arbitrary",)),
        name="moe_experts",
    )(tile_expert, n_used, xs, w_gate, w_up, w_down)


def _combine_kernel(h_ref, info_ref, y0_ref, y1_ref, o_ref):
    o_ref[...] = h_ref[...] + (info_ref[:, 2:3] * _unpack_bf16_pairs(y0_ref[...])
                               + info_ref[:, 3:4] * _unpack_bf16_pairs(y1_ref[...]))


def _combine(h, yg, info):
    n, d = h.shape
    tm = _tile(n, 512, SUBLANES)
    nt = n // tm
    return pl.pallas_call(
        _combine_kernel,
        grid=(nt,),
        in_specs=[pl.BlockSpec((tm, d), lambda i: (i, 0)),
                  pl.BlockSpec((tm, LANES), lambda i: (i, 0)),
                  pl.BlockSpec((tm, d // 2), lambda i: (i, 0)),
                  pl.BlockSpec((tm, d // 2), lambda i: (nt + i, 0))],
        out_specs=pl.BlockSpec((tm, d), lambda i: (i, 0)),
        out_shape=jax.ShapeDtypeStruct((n, d), F32),
        compiler_params=_params(("parallel",)),
        name="moe_combine",
    )(h, info, yg, yg)


def _final_kernel(*refs, sub, prompt_steps):
    g_ref = refs[0]
    h_refs, info_refs, y0_refs, y1_refs = (refs[1 + k * sub:1 + (k + 1) * sub] for k in range(4))
    yp_ref, ys_ref = refs[1 + 4 * sub:]
    tin = h_refs[0].shape[0]
    i = pl.program_id(0)

    def store(o_ref):
        for k in range(sub):
            h2 = h_refs[k][...] + (info_refs[k][:, 2:3] * _unpack_bf16_pairs(y0_refs[k][...])
                                   + info_refs[k][:, 3:4] * _unpack_bf16_pairs(y1_refs[k][...]))
            o_ref[k * tin:(k + 1) * tin, :] = _rms(h2, g_ref[...])

    pl.when(i < prompt_steps)(functools.partial(store, yp_ref))
    pl.when(i >= prompt_steps)(functools.partial(store, ys_ref))


def _final_combine(h, yg, info, g, *, nb, t, n_sample):
    n, d = h.shape
    seq = t - LEAD
    tin = _tile(math.gcd(LEAD, n_sample, seq), LANES, SUBLANES)
    sub = FINAL_SUBTILES if seq % (FINAL_SUBTILES * tin) == 0 and n_sample % (FINAL_SUBTILES * tin) == 0 else 1
    tout = sub * tin
    steps_per_batch = seq // tout
    prompt_steps = nb * steps_per_batch
    tiles_per_batch, lead_tiles, n_tiles = t // tin, LEAD // tin, n // tin

    def in_tile(k, offset=0):
        def index(i):
            ip = jnp.minimum(i, prompt_steps - 1)
            prompt = (ip // steps_per_batch) * tiles_per_batch + lead_tiles + (ip % steps_per_batch) * sub + k
            sample = nb * tiles_per_batch + (i - prompt_steps) * sub + k
            return offset + jnp.where(i < prompt_steps, prompt, sample), 0
        return index

    in_specs = ([pl.BlockSpec((1, d), lambda i: (0, 0))]
                + [pl.BlockSpec((tin, d), in_tile(k)) for k in range(sub)]
                + [pl.BlockSpec((tin, LANES), in_tile(k)) for k in range(sub)]
                + [pl.BlockSpec((tin, d // 2), in_tile(k)) for k in range(sub)]
                + [pl.BlockSpec((tin, d // 2), in_tile(k, n_tiles)) for k in range(sub)])
    return pl.pallas_call(
        functools.partial(_final_kernel, sub=sub, prompt_steps=prompt_steps),
        grid=(prompt_steps + n_sample // tout,),
        in_specs=in_specs,
        out_specs=[pl.BlockSpec((tout, d), lambda i: (jnp.minimum(i, prompt_steps - 1), 0)),
                   pl.BlockSpec((tout, d), lambda i: (jnp.maximum(i - prompt_steps, 0), 0))],
        out_shape=[jax.ShapeDtypeStruct((nb * seq, d), F32), jax.ShapeDtypeStruct((n_sample, d), F32)],
        compiler_params=_params(("arbitrary",)),
        name="moe_combine_final",
    )(g.reshape(1, d), *([h] * sub + [info] * sub + [yg] * (2 * sub)))


def _moe(h1, xn, info, counts_f, w_gate, w_up, w_down, expert_base, *, final=None):
    n, d = h1.shape
    n_e = N_GROUPS * N_EXP
    tm = _tile(2 * n, MOE_TILE, SUBLANES)
    n_tiles = (2 * n) // tm + n_e
    n_slots = n_tiles * tm
    experts = jnp.arange(n_e, dtype=jnp.int32)
    counts = counts_f[0, :n_e].astype(jnp.int32)
    padded = ((counts + tm - 1) // tm) * tm
    pad_ends = jnp.cumsum(padded)
    pad_starts = pad_ends - padded
    n_used = (pad_ends[-1] // tm).astype(jnp.int32)
    eid = info[:, 0:2].astype(jnp.int32)
    rank = info[:, 4:6].astype(jnp.int32)
    slot_of_pick = jnp.sum(jnp.where(eid[..., None] == experts, pad_starts, 0), axis=-1) + rank
    slots = slot_of_pick.T.reshape(-1)
    tile_starts = jnp.arange(n_tiles, dtype=jnp.int32) * tm
    tile_expert = expert_base + jnp.minimum(jnp.sum(tile_starts[:, None] >= pad_ends[None, :], axis=1), n_e - 1)

    xs = _sc_scatter_pairs(xn, slots, n_slots)
    ys = _experts(xs, tile_expert.astype(jnp.int32), n_used.reshape(1), w_gate, w_up, w_down, tm=tm)
    yg = _sc_gather(ys, slots)
    if final is not None:
        return _final_combine(h1, yg, info, final["g"], nb=final["nb"], t=final["t"], n_sample=final["n_sample"])
    return _combine(h1, yg, info)


def _router_weights(w_rg, b_rg, w_re, b_re):
    d = w_rg.shape[0]
    pad = LANES - N_GROUPS - N_GROUPS * N_EXP
    w = jnp.concatenate([w_rg, w_re, jnp.zeros((d, pad), F32)], axis=1)
    b = jnp.concatenate([b_rg, b_re, jnp.zeros((pad,), F32)]).reshape(1, LANES)
    hi = w.astype(BF16)
    lo = (w - hi.astype(F32)).astype(BF16)
    return hi, lo, b


def _rows_to_lanes(x, nb, t, nch):
    return x.reshape(nb, t, nch).transpose(0, 2, 1).reshape(nb * nch, t)


def _even_layer(h, dims, g_mix, w_in, w_g2, b_g, b_f, g_a, w_o, state_gla, ck, cv, clf):
    nb, t, db, ds, npr, n = dims
    d = h.shape[1]
    qa, ka, va, ra, ga, qb, kb, vb, fb = jnp.split(
        w_in, [A_QK, 2 * A_QK, 2 * A_QK + A_V, 2 * A_QK + 2 * A_V, 2 * A_QK + 2 * A_V + A_RANK,
               2 * A_QK + 2 * A_V + A_RANK + B_W, 2 * A_QK + 2 * A_V + A_RANK + 2 * B_W,
               2 * A_QK + 2 * A_V + A_RANK + 3 * B_W], axis=1)
    w_packed = jnp.concatenate(
        [qa, ka, va, ra, qb, kb, vb, ga, fb, jnp.zeros((d, LANES - A_RANK - B_HEADS), F32)], axis=1)
    q_off = 2 * A_QK + 2 * A_V
    colscale = jnp.ones((1, MAIN_W), F32).at[:, q_off:q_off + B_W].set(B_DH ** -0.5 * LOG2E)
    p, pb, k_rows, v_rows = _proj(h, g_mix, w_packed, n_rows=npr, colscale=colscale, kv_col=q_off + B_W)
    p = _proj_precise(h, g_mix, w_packed, p, row0=npr)

    nh = B_HEADS
    fcol = MAIN_W + A_RANK
    zeros_col = lambda r: jnp.zeros((r, 1), F32)
    bias_row = lambda lanes: jnp.tile(b_f, lanes // nh).reshape(1, lanes)
    fb_p = p[:npr, fcol:fcol + nh].reshape(nb, t * nh)
    logf_p, f_p = _gate_scan(fb_p, bias_row(t * nh), zeros_col(nb), mode="fox", act_start=0,
                             valid_start=N_PAD * nh, valid_end=t * nh, seg=None, stride=nh)
    past = ck.shape[1]
    n_c = past * nh
    x_s = _pad_lanes(jnp.concatenate([clf.reshape(db, n_c), p[npr:, fcol:fcol + nh].reshape(db, ds * nh)], axis=1))
    logf_s, f_s = _gate_scan(x_s, bias_row(x_s.shape[1]), zeros_col(db), mode="fox", act_start=n_c,
                             valid_start=0, valid_end=n_c + ds * nh, seg=None, stride=nh)

    nc = t // CHUNK
    oa, s_p = _gla(p, jnp.zeros((nb, A_HEADS, A_DK, A_DV), F32), w_g2, b_g,
                   nb=nb, nc=nc, L=CHUNK, row_block0=0, lead_pad=N_PAD, n_total=n)
    oa, s_s = _gla(p, state_gla, w_g2, b_g, nb=db, nc=1, L=ds, row_block0=npr // ds, lead_pad=0,
                   n_total=n, prev_out=oa, precise=True)

    fk = (f_p * LOG2E).reshape(nb, t, nh).transpose(0, 2, 1).reshape(nb, nh, 1, t)
    ob = _flash(pb, fk, nb=nb, t=t, n_total=n)
    f_cache = (f_s[:, :n_c] * LOG2E).reshape(db, 1, n_c)
    f_new = (f_s[:, n_c:n_c + ds * nh] * LOG2E).reshape(db, ds, nh).transpose(0, 2, 1).reshape(db, 1, nh * ds)
    ob = _fox_sample(p, ck.reshape(db, n_c, B_DH), cv.reshape(db, n_c, B_DH), f_cache, f_new, ob,
                     nb=db, ds=ds, row_block0=npr // ds)

    kcol = q_off + B_W
    states = dict(
        s_p=s_p, s_s=s_s,
        k_p=k_rows.reshape(nb, t, B_HEADS, B_DH)[:, N_PAD:],
        v_p=v_rows.reshape(nb, t, B_HEADS, B_DH)[:, N_PAD:],
        f_p=logf_p.reshape(nb, t, nh)[:, N_PAD:],
        k_s=p[npr:, kcol:kcol + B_W].reshape(db, ds, B_HEADS, B_DH),
        v_s=p[npr:, kcol + B_W:kcol + 2 * B_W].reshape(db, ds, B_HEADS, B_DH),
        f_s=logf_s[:, n_c:n_c + ds * nh].reshape(db, ds, nh))
    return (oa, 0, p, (2 * A_QK + A_V) // A_V, ob, g_a, w_o), states


def _chunk_rows(x, nb, nch, nc, L):
    x = x[:, :nc * L].reshape(nb, nch, nc, L)
    return x.transpose(0, 2, 1, 3).reshape(nb, nc, 1, nch * L), x.transpose(0, 2, 3, 1)


def _odd_layer(h, dims, g_mix, w_in, b_gate, g_c, w_o, c0, n0, m0):
    nb, t, db, ds, npr, n = dims
    d = h.shape[1]
    w_packed = jnp.concatenate(
        [w_in, jnp.zeros((d, LANES - 2 * C_HEADS), F32)], axis=1)
    (p,) = _proj(h, g_mix, w_packed, n_rows=npr)
    p = _proj_precise(h, g_mix, w_packed, p, row0=npr)

    ng = 2 * C_HEADS
    isf = (jnp.arange(ng) >= C_HEADS).astype(F32)
    nc = t // CHUNK

    def gates(rows, nbatch, tt, valid_start, seg):
        x = _pad_lanes(_rows_to_lanes(rows, nbatch, tt, ng))
        val, cum = _gate_scan(x, jnp.tile(b_gate, nbatch).reshape(-1, 1), jnp.tile(isf, nbatch).reshape(-1, 1),
                              mode="mlstm", act_start=0, valid_start=valid_start, valid_end=tt, seg=seg)
        return val.reshape(nbatch, ng, -1), cum.reshape(nbatch, ng, -1)

    val_p, cum_p = gates(p[:npr, MAIN_W:MAIN_W + ng], nb, t, N_PAD, CHUNK)
    val_s, cum_s = gates(p[npr:, MAIN_W:MAIN_W + ng], db, ds, 0, ds)

    def chunked(val, cum, nbatch, ncs, L):
        li_row, li_col = _chunk_rows(val[:, :C_HEADS].reshape(nbatch * C_HEADS, -1), nbatch, C_HEADS, ncs, L)
        b_row, b_col = _chunk_rows(cum[:, C_HEADS:].reshape(nbatch * C_HEADS, -1), nbatch, C_HEADS, ncs, L)
        return li_row, b_row, li_col, b_col

    zc = jnp.zeros((nb, C_HEADS, C_DV, C_DQK), F32)
    zn = jnp.zeros((nb, C_HEADS, C_DQK), F32)
    zm = jnp.zeros((nb, C_HEADS, 1), F32)
    hm, c_p, n_p, m_p = _mlstm(p, *chunked(val_p, cum_p, nb, nc, CHUNK), zc, zn, zm,
                               nb=nb, nc=nc, L=CHUNK, row_block0=0, n_total=n)
    hm, c_s, n_s, m_s = _mlstm(p, *chunked(val_s, cum_s, db, 1, ds), c0, n0, m0.reshape(db, C_HEADS, 1),
                               nb=db, nc=1, L=ds, row_block0=npr // ds, n_total=n, prev_out=hm)
    states = dict(c_p=c_p, n_p=n_p, m_p=m_p.reshape(nb, C_HEADS), c_s=c_s, n_s=n_s, m_s=m_s.reshape(db, C_HEADS))
    return (hm, 0, p, (2 * C_QK + C_V) // C_V, None, g_c, w_o), states


def kernel(x_prompt, x_sample, state_gla, cache_fox_k, cache_fox_v, cache_fox_logf, state_mlstm_c, state_mlstm_n, state_mlstm_m, meta_tokens, norm_mix, norm_ffn, norm_final, w_in_even, w_gla_gate2, b_gla_gate, b_fox_f, g_gla_out, w_out_even, w_in_odd, b_mlstm_gate, g_mlstm_out, w_out_odd, w_router_group, b_router_group, w_router_expert, b_router_expert, w_exp_gate, w_exp_up, w_exp_down):
    nb, seq, d = x_prompt.shape
    db, ds, _ = x_sample.shape
    t = LEAD + seq
    npr, nsm = nb * t, db * ds
    n = npr + nsm
    dims = (nb, t, db, ds, npr, n)
    depth = norm_mix.shape[0]
    n_e = N_GROUPS * N_EXP
    f = w_exp_gate.shape[-1]

    pad_rows = jnp.zeros((N_PAD, d), F32)
    h = jnp.concatenate([piece for b in range(nb) for piece in (pad_rows, meta_tokens, x_prompt[b])]
                        + [x_sample.reshape(nsm, d)], axis=0)
    wg_all = w_exp_gate.reshape(depth * n_e, d, f)
    wu_all = w_exp_up.reshape(depth * n_e, d, f)
    wd_all = w_exp_down.reshape(depth * n_e, f, d)

    even, odd = [], []
    for l in range(depth):
        if l % 2 == 0:
            e = l // 2
            mix, st = _even_layer(h, dims, norm_mix[l], w_in_even[e], w_gla_gate2[e], b_gla_gate[e], b_fox_f[e],
                                  g_gla_out[e], w_out_even[e], state_gla[e], cache_fox_k[e], cache_fox_v[e],
                                  cache_fox_logf[e])
            even.append(st)
            hd, act = A_DV, "silu"
        else:
            o = l // 2
            mix, st = _odd_layer(h, dims, norm_mix[l], w_in_odd[o], b_mlstm_gate[o], g_mlstm_out[o], w_out_odd[o],
                                 state_mlstm_c[o], state_mlstm_n[o], state_mlstm_m[o])
            odd.append(st)
            hd, act = C_DV, "sigmoid"
        a, a_col, r, r_col, b, g_head, w_o = mix
        wr_hi, wr_lo, b_r = _router_weights(w_router_group[l], b_router_group[l], w_router_expert[l],
                                            b_router_expert[l])
        mix_args = (h, a, a_col, r, r_col, b, g_head, w_o, norm_ffn[l], wr_hi, wr_lo, b_r)
        h1, xn, info, counts = _mixout(*mix_args, jnp.zeros((1, LANES), F32), hd=hd, act=act, row0=0, n_rows=npr)
        h1, xn, info, counts = _mixout(*mix_args, counts, hd=hd, act=act, row0=npr, n_rows=nsm,
                                       prev=(h1, xn, info))
        last = l == depth - 1
        h = _moe(h1, xn, info, counts, wg_all, wu_all, wd_all, l * n_e,
                 final=dict(g=norm_final, nb=nb, t=t, n_sample=nsm) if last else None)

    y_prompt = h[0].reshape(nb, seq, d)
    y_sample = h[1].reshape(db, ds, d)
    stack = lambda sts, key: jnp.stack([s[key] for s in sts])
    return (y_prompt, y_sample,
            stack(even, "s_p"), stack(even, "k_p"), stack(even, "v_p"), stack(even, "f_p"),
            stack(odd, "c_p"), stack(odd, "n_p"), stack(odd, "m_p"),
            stack(even, "s_s"), stack(even, "k_s"), stack(even, "v_s"), stack(even, "f_s"),
            stack(odd, "c_s"), stack(odd, "n_s"), stack(odd, "m_s"))
```

```python
import functools
import math

import jax
import jax.numpy as jnp
from jax import lax
from jax.experimental import pallas as pl
from jax.experimental.pallas import tpu as pltpu
from jax.experimental.pallas import tpu_sc as plsc

F32 = jnp.float32
BF16 = jnp.bfloat16

CHUNK = 64
N_META = 16
LEAD = 128
N_PAD = LEAD - N_META
A_HEADS, A_DK, A_DV, A_RANK = 4, 64, 128, 16
A_GATE_NORM = 16.0
B_HEADS, B_DH = 4, 128
C_HEADS, C_DQK, C_DV = 4, 128, 256
GATE_CAP = 15.0
N_GROUPS, N_EXP = 4, 8
EPS = 1e-6
NEG = -1e30
LOG2E = 1.4426950408889634
A_QK = A_HEADS * A_DK
A_V = A_HEADS * A_DV
B_W = B_HEADS * B_DH
C_QK = C_HEADS * C_DQK
C_V = C_HEADS * C_DV

LANES = 128
SUBLANES = 8
VMEM_LIMIT_BYTES = 56 * 1024 * 1024
SC_CORES, SC_SUBCORES = 2, 16
SC_GATHER_ROWS, SC_GATHER_BUFS = 64, 3
SC_SCATTER_ROWS, SC_SCATTER_BUFS = 32, 3
GLA_SUB = 16
GLA_CHUNKS_PER_STEP = 10
MLSTM_CHUNKS_PER_STEP = 1
FLASH_HEADS = 4
MOE_TILE = 1024
MIXOUT_TILE = 1024
FINAL_SUBTILES = 4
MAIN_W = 3072
PROJ_W = MAIN_W + LANES

_NT = (((1,), (1,)), ((), ()))
_TN = (((0,), (0,)), ((), ()))
_NN = (((1,), (0,)), ((), ()))


def _params(sem):
    return pltpu.CompilerParams(dimension_semantics=sem, vmem_limit_bytes=VMEM_LIMIT_BYTES)


def _tile(n, pref, mult):
    t = (min(pref, n) // mult) * mult
    while t > mult and n % t:
        t -= mult
    assert t >= mult and n % t == 0, (n, pref, mult)
    return t


def _dot(a, b, dims=_NN):
    return lax.dot_general(a, b, dims, preferred_element_type=F32)


def _split(x):
    hi = x.astype(BF16)
    lo = (x - hi.astype(F32)).astype(BF16)
    return hi, lo


def _dot3(a, b, dims=_NN):
    ah, al = _split(a)
    bh, bl = _split(b)
    return _dot(ah, bh, dims) + _dot(ah, bl, dims) + _dot(al, bh, dims)


def _log_sigmoid(x):
    return jnp.minimum(x, 0.0) - jnp.log1p(jnp.exp(-jnp.abs(x)))


def _sigmoid(x):
    return 1.0 / (1.0 + jnp.exp(-x))


def _rms(x, g):
    return x * lax.rsqrt(jnp.mean(x * x, axis=-1, keepdims=True) + EPS) * g


def _pack_bf16_pairs(x):
    w = x.shape[1] // 2
    hi = lax.bitcast_convert_type(x[:, :w].astype(BF16).astype(F32), jnp.int32)
    lo = lax.bitcast_convert_type(x[:, w:].astype(BF16).astype(F32), jnp.int32)
    return hi | lax.shift_right_logical(lo, 16)


def _unpack_bf16_pairs(p):
    hi = lax.bitcast_convert_type(p & jnp.int32(-65536), F32)
    lo = lax.bitcast_convert_type(lax.shift_left(p, 16), F32)
    return jnp.concatenate([hi, lo], axis=1)


def _cumsum_rows(x):
    n = x.shape[0]
    row = lax.broadcasted_iota(jnp.int32, x.shape, 0)
    s = 1
    while s < n:
        x = x + jnp.where(row >= s, pltpu.roll(x, s, axis=0), 0.0)
        s *= 2
    return x


def _proj_kernel(x_ref, g_ref, w_ref, *rest, col_chunk, kv_col):
    if kv_col is None:
        (o_ref,) = rest
    else:
        cs_ref, o_ref, ob_ref, *kv_refs = rest
    tm = x_ref.shape[0]
    xn = _rms(x_ref[...], g_ref[...]).astype(BF16)
    for c0 in range(0, PROJ_W, col_chunk):
        c1 = min(c0 + col_chunk, PROJ_W)
        y = _dot(xn, w_ref[:, c0:c1])
        o_ref[:, c0:c1] = y
        if kv_col is not None:
            if c0 < MAIN_W:
                m1 = min(c1, MAIN_W)
                ob_ref[:, c0:m1] = (y[:, :m1 - c0] * cs_ref[:, c0:m1]).astype(BF16)
            for g0 in range(c0, c1, LANES):
                rel = g0 - kv_col
                if 0 <= rel < 2 * B_W:
                    head = (rel % B_W) // B_DH
                    kv_refs[rel // B_W][pl.ds(head, tm, stride=B_HEADS), :] = y[:, g0 - c0:g0 - c0 + LANES]


def _proj_precise_kernel(x_ref, g_ref, w_ref, prev_ref, o_ref, *, col_chunk):
    del prev_ref
    xn = _rms(x_ref[...], g_ref[...])
    for c0 in range(0, PROJ_W, col_chunk):
        c1 = min(c0 + col_chunk, PROJ_W)
        o_ref[:, c0:c1] = _dot3(xn, w_ref[:, c0:c1])


def _proj(h, g, w_packed, *, n_rows, colscale=None, kv_col=None):
    n, d = h.shape
    tm = _tile(n_rows, 512, 16)
    in_specs = [pl.BlockSpec((tm, d), lambda i: (i, 0)),
                pl.BlockSpec((1, d), lambda i: (0, 0)),
                pl.BlockSpec((d, PROJ_W), lambda i: (0, 0))]
    args = [h, g.reshape(1, d), w_packed.astype(BF16)]
    out_specs = [pl.BlockSpec((tm, PROJ_W), lambda i: (i, 0))]
    out_shape = [jax.ShapeDtypeStruct((n, PROJ_W), F32)]
    if kv_col is not None:
        in_specs.append(pl.BlockSpec((1, MAIN_W), lambda i: (0, 0)))
        args.append(colscale)
        out_specs += [pl.BlockSpec((tm, MAIN_W), lambda i: (i, 0))] + [pl.BlockSpec((tm * B_HEADS, B_DH), lambda i: (i, 0))] * 2
        out_shape += ([jax.ShapeDtypeStruct((n_rows, MAIN_W), BF16)]
                      + [jax.ShapeDtypeStruct((n_rows * B_HEADS, B_DH), F32)] * 2)
    return pl.pallas_call(
        functools.partial(_proj_kernel, col_chunk=640, kv_col=kv_col),
        grid=(n_rows // tm,),
        in_specs=in_specs,
        out_specs=out_specs,
        out_shape=out_shape,
        compiler_params=_params(("parallel",)),
        name="proj",
    )(*args)


def _proj_precise(h, g, w_packed, prev, *, row0):
    n, d = h.shape
    tm = _tile(n - row0, 512, SUBLANES)
    assert row0 % tm == 0
    return pl.pallas_call(
        functools.partial(_proj_precise_kernel, col_chunk=640),
        grid=((n - row0) // tm,),
        in_specs=[pl.BlockSpec((tm, d), lambda i: (row0 // tm + i, 0)),
                  pl.BlockSpec((1, d), lambda i: (0, 0)),
                  pl.BlockSpec((d, PROJ_W), lambda i: (0, 0)),
                  pl.BlockSpec(memory_space=pl.ANY)],
        out_specs=pl.BlockSpec((tm, PROJ_W), lambda i: (row0 // tm + i, 0)),
        out_shape=jax.ShapeDtypeStruct((n, PROJ_W), F32),
        input_output_aliases={3: 0},
        compiler_params=_params(("parallel",)),
        name="proj_precise",
    )(h, g.reshape(1, d), w_packed, prev)


def _gate_scan_kernel(x_ref, bias_ref, isf_ref, val_ref, cum_ref, *, mode, act_start, valid_start, valid_end, seg, stride):
    x = x_ref[...]
    lane = lax.broadcasted_iota(jnp.int32, x.shape, 1)
    valid = (lane >= valid_start) & (lane < valid_end)
    if mode == "fox":
        val = jnp.where(lane >= act_start, _log_sigmoid(x + bias_ref[...]), x)
        val = jnp.where(valid, val, 0.0)
        add = val
    else:
        gate = GATE_CAP * jnp.tanh((x + bias_ref[...]) / GATE_CAP)
        isf = isf_ref[...] > 0.5
        val = jnp.where(isf, jnp.where(valid, _log_sigmoid(gate), 0.0),
                        jnp.where(valid, gate, -jnp.inf))
        add = jnp.where(isf, val, 0.0)
    val_ref[...] = val
    n = x.shape[1]
    pos = lane if seg is None else lane % seg
    limit = n if seg is None else seg
    s = stride
    while s < limit:
        add = add + jnp.where(pos >= s, pltpu.roll(add, s, axis=1), 0.0)
        s *= 2
    cum_ref[...] = add


def _gate_scan(x, bias, isf, *, mode, act_start, valid_start, valid_end, seg, stride=1):
    r, n = x.shape
    full = lambda shape: pl.BlockSpec(shape, lambda i: (0,) * len(shape))
    return pl.pallas_call(
        functools.partial(_gate_scan_kernel, mode=mode, act_start=act_start,
                          valid_start=valid_start, valid_end=valid_end, seg=seg, stride=stride),
        grid=(1,),
        in_specs=[full((r, n)), full(bias.shape), full((r, 1))],
        out_specs=[full((r, n)), full((r, n))],
        out_shape=[jax.ShapeDtypeStruct((r, n), F32)] * 2,
        compiler_params=_params(("arbitrary",)),
        name="gate_scan_" + mode,
    )(x, bias, isf)


def _pad_lanes(x):
    n = x.shape[-1]
    m = -(-n // LANES) * LANES
    return x if m == n else jnp.pad(x, ((0, 0), (0, m - n)))


def _gla_kernel(qk_ref, v_ref, sm_ref, s0_ref, wg2_ref, wg2t_ref, bgr_ref, bgc_ref, *rest,
                L, cps, sub, lead_pad, aliased, precise):
    if aliased:
        rest = rest[1:]
    o_ref, sout_ref, s_scr = rest
    step = pl.program_id(1)
    for j in range(cps):
        rows = pl.ds(j * L, L)
        _gla_chunk(qk_ref.at[rows], v_ref.at[rows], sm_ref.at[rows], s0_ref, wg2_ref, wg2t_ref, bgr_ref, bgc_ref,
                   o_ref.at[rows], sout_ref, s_scr, c=step * cps + j, first=(j == 0), last=(j == cps - 1),
                   L=L, sub=sub, lead_pad=lead_pad, precise=precise)


def _gla_chunk(qk_ref, v_ref, sm_ref, s0_ref, wg2_ref, wg2t_ref, bgr_ref, bgc_ref, o_ref, sout_ref, s_scr, *,
               c, first, last, L, sub, lead_pad, precise):
    nh, dk, dv = A_HEADS, A_DK, A_DV
    cast = (lambda x: x) if precise else (lambda x: x.astype(BF16))
    mm = _dot3 if precise else _dot

    if first:
        @pl.when(pl.program_id(1) == 0)
        def _():
            s_scr[...] = jnp.zeros_like(s_scr)
            for h in range(nh):
                s_scr[h * dk:(h + 1) * dk, h * dv:(h + 1) * dv] = s0_ref[0, h]

    qk = qk_ref[...]
    q = qk[:, :A_QK] * (A_DK ** -0.5)
    k = qk[:, A_QK:]
    v = v_ref[...]
    ga = sm_ref[:, :A_RANK]
    row = lax.broadcasted_iota(jnp.int32, (L, 1), 0)
    valid = (c * L + row) >= lead_pad
    z = _dot3(ga, wg2_ref[...]) + bgr_ref[...]
    loga = jnp.where(valid, _log_sigmoid(z) / A_GATE_NORM, 0.0)
    k = jnp.where(valid, k, 0.0)
    b = _cumsum_rows(loga)
    b_last = b[L - 1:L, :]
    lane_t = lax.broadcasted_iota(jnp.int32, (1, L), 1)
    zt = _dot3(wg2t_ref[...], ga, _NT) + bgc_ref[...]
    logat = jnp.where((c * L + lane_t) >= lead_pad, _log_sigmoid(zt) / A_GATE_NORM, 0.0)
    b_last_col = jnp.sum(logat, axis=1, keepdims=True)

    qhead = lax.broadcasted_iota(jnp.int32, (1, A_QK), 1) // dk
    vhead = lax.broadcasted_iota(jnp.int32, (1, A_V), 1) // dv
    vb = cast(v)
    zero_b = jnp.zeros((), vb.dtype)
    v_bd = jnp.concatenate([jnp.where(vhead == h, vb, zero_b) for h in range(nh)], axis=0)

    rows_all = lax.broadcasted_iota(jnp.int32, (L, 1), 0)
    a_rows = []
    for i in range(L // sub):
        r0 = i * sub
        ci = jnp.zeros((1, A_QK), F32) if i == 0 else b[r0 - 1:r0, :]
        qt = cast(q[r0:r0 + sub] * jnp.exp(b[r0:r0 + sub] - ci))
        kt = cast(jnp.where(rows_all < r0 + sub, k * jnp.exp(ci - b), 0.0))
        k_stack = jnp.concatenate([jnp.where(qhead == h, kt, zero_b) for h in range(nh)], axis=0)
        a_rows.append(mm(qt, k_stack, _NT))
    a = a_rows[0] if len(a_rows) == 1 else jnp.concatenate(a_rows, axis=0)
    t_idx = lax.broadcasted_iota(jnp.int32, (L, nh * L), 0)
    s_idx = lax.broadcasted_iota(jnp.int32, (L, nh * L), 1) % L
    a = jnp.where(s_idx <= t_idx, a, 0.0)
    o_intra = mm(cast(a), v_bd)

    s_full = s_scr[...]
    o_inter = mm(cast(q * jnp.exp(b)), cast(s_full))
    o_ref[...] = o_inter + o_intra

    k_hat = cast(k * jnp.exp(b_last - b))
    upd = mm(k_hat, vb, _TN)
    khead_col = lax.broadcasted_iota(jnp.int32, (A_QK, 1), 0) // dk
    s_new = jnp.exp(b_last_col) * s_full + jnp.where(khead_col == vhead, upd, 0.0)
    s_scr[...] = s_new

    if last:
        @pl.when(pl.program_id(1) == pl.num_programs(1) - 1)
        def _():
            for h in range(nh):
                sout_ref[0, h] = s_new[h * dk:(h + 1) * dk, h * dv:(h + 1) * dv]


def _chunks_per_step(nc, want):
    return next(k for k in (want, 2, 1) if k <= want and nc % k == 0)


def _gla(p, s0, wg2, bg, *, nb, nc, L, row_block0, lead_pad, n_total, prev_out=None, precise=False):
    aliased = prev_out is not None
    cps = _chunks_per_step(nc, GLA_CHUNKS_PER_STEP)
    assert row_block0 % cps == 0
    nc, L, row_block0, chunk = nc // cps, L * cps, row_block0 // cps, L
    rb = lambda b, c: row_block0 + b * nc + c
    in_specs = [pl.BlockSpec((L, 2 * A_QK), lambda b, c: (rb(b, c), 0)),
                pl.BlockSpec((L, A_V), lambda b, c: (rb(b, c), 2 * A_QK // A_V)),
                pl.BlockSpec((L, LANES), lambda b, c: (rb(b, c), MAIN_W // LANES)),
                pl.BlockSpec((1, A_HEADS, A_DK, A_DV), lambda b, c: (b, 0, 0, 0)),
                pl.BlockSpec((A_RANK, A_QK), lambda b, c: (0, 0)),
                pl.BlockSpec((A_QK, A_RANK), lambda b, c: (0, 0)),
                pl.BlockSpec((1, A_QK), lambda b, c: (0, 0)),
                pl.BlockSpec((A_QK, 1), lambda b, c: (0, 0))]
    args = [p, p, p, s0, wg2, wg2.T, bg.reshape(1, A_QK), bg.reshape(A_QK, 1)]
    io_alias = {}
    if aliased:
        in_specs.append(pl.BlockSpec(memory_space=pl.ANY))
        args.append(prev_out)
        io_alias = {len(args) - 1: 0}
    return pl.pallas_call(
        functools.partial(_gla_kernel, L=chunk, cps=cps, sub=min(GLA_SUB, chunk), lead_pad=lead_pad,
                          aliased=aliased, precise=precise),
        grid=(nb, nc),
        in_specs=in_specs,
        out_specs=[pl.BlockSpec((L, A_V), lambda b, c: (rb(b, c), 0)),
                   pl.BlockSpec((1, A_HEADS, A_DK, A_DV), lambda b, c: (b, 0, 0, 0))],
        out_shape=[jax.ShapeDtypeStruct((n_total, A_V), F32),
                   jax.ShapeDtypeStruct((nb, A_HEADS, A_DK, A_DV), F32)],
        scratch_shapes=[pltpu.VMEM((A_QK, A_V), F32)],
        input_output_aliases=io_alias,
        compiler_params=_params(("parallel", "arbitrary")),
        name="gla_L%d" % chunk,
    )(*args)


def _flash_kernel(qi_ref, kj_ref, q_ref, k_ref, v_ref, fk_ref, o_ref, m_scr, l_scr, acc_scr, *, blk, lead_pad):
    step = pl.program_id(2)
    i = qi_ref[step]
    j = kj_ref[step]

    @pl.when(j == 0)
    def _():
        m_scr[...] = jnp.full_like(m_scr, -jnp.inf)
        l_scr[...] = jnp.zeros_like(l_scr)
        acc_scr[...] = jnp.zeros_like(acc_scr)

    def update(masked):
        for g in range(FLASH_HEADS):
            sl = slice(g * B_DH, (g + 1) * B_DH)
            s = _dot(q_ref[:, sl], k_ref[:, sl], _NT) - fk_ref[0, g]
            if masked:
                qpos = i * blk + lax.broadcasted_iota(jnp.int32, (blk, blk), 0)
                kpos = j * blk + lax.broadcasted_iota(jnp.int32, (blk, blk), 1)
                s = jnp.where((kpos <= qpos) & (kpos >= lead_pad), s, NEG)
            m_prev = m_scr[g]
            m_new = jnp.maximum(m_prev, jnp.max(s, axis=1, keepdims=True))
            alpha = jnp.exp2(m_prev - m_new)
            p = jnp.exp2(s - m_new)
            l_scr[g] = alpha * l_scr[g] + jnp.sum(p, axis=1, keepdims=True)
            acc_scr[g] = alpha * acc_scr[g] + _dot(p.astype(BF16), v_ref[:, sl])
            m_scr[g] = m_new

    edge = (j == i) | (j == 0)
    pl.when(edge)(functools.partial(update, True))
    pl.when(jnp.logical_not(edge))(functools.partial(update, False))

    @pl.when(j == i)
    def _():
        for g in range(FLASH_HEADS):
            o_ref[:, g * B_DH:(g + 1) * B_DH] = acc_scr[g] / l_scr[g]


def _flash(pb, fk, *, nb, t, n_total):
    blk = _tile(t, 640, LANES)
    nq = t // blk
    hg = FLASH_HEADS
    w = hg * B_DH
    pairs = [(i, j) for i in range(nq) for j in range(i + 1)]
    qi = jnp.asarray([p[0] for p in pairs], jnp.int32)
    kj = jnp.asarray([p[1] for p in pairs], jnp.int32)
    qc, kc, vc = (A_QK * 2 + A_V * 2) // w, (A_QK * 2 + A_V * 2 + B_W) // w, (A_QK * 2 + A_V * 2 + 2 * B_W) // w
    grid_spec = pltpu.PrefetchScalarGridSpec(
        num_scalar_prefetch=2,
        grid=(nb, B_HEADS // hg, len(pairs)),
        in_specs=[pl.BlockSpec((blk, w), lambda b, h, s, qi, kj: (b * nq + qi[s], qc + h)),
                  pl.BlockSpec((blk, w), lambda b, h, s, qi, kj: (b * nq + kj[s], kc + h)),
                  pl.BlockSpec((blk, w), lambda b, h, s, qi, kj: (b * nq + kj[s], vc + h)),
                  pl.BlockSpec((1, hg, 1, blk), lambda b, h, s, qi, kj: (b, h, 0, kj[s]))],
        out_specs=pl.BlockSpec((blk, w), lambda b, h, s, qi, kj: (b * nq + qi[s], h)),
        scratch_shapes=[pltpu.VMEM((hg, blk, 1), F32), pltpu.VMEM((hg, blk, 1), F32),
                        pltpu.VMEM((hg, blk, B_DH), F32)],
    )
    return pl.pallas_call(
        functools.partial(_flash_kernel, blk=blk, lead_pad=N_PAD),
        grid_spec=grid_spec,
        out_shape=jax.ShapeDtypeStruct((n_total, B_W), F32),
        compiler_params=_params(("parallel", "parallel", "arbitrary")),
        name="fox_flash",
    )(qi, kj, pb, pb, pb, fk)


def _fox_sample_kernel(q_ref, kn_ref, vn_ref, kc_ref, vc_ref, fc_ref, fn_ref, prev_ref, o_ref, *, ds):
    del prev_ref
    nh = B_HEADS
    stack = lambda ref: jnp.concatenate([ref[:, h * B_DH:(h + 1) * B_DH] for h in range(nh)], axis=0)
    q = stack(q_ref) * (B_DH ** -0.5 * LOG2E)
    rows = nh * ds
    n_c = kc_ref.shape[1]
    qh_c = lax.broadcasted_iota(jnp.int32, (rows, n_c), 0) // ds
    kh_c = lax.broadcasted_iota(jnp.int32, (rows, n_c), 1) % nh
    s_c = _dot3(q, kc_ref[0], _NT) - fc_ref[0]
    s_c = jnp.where(qh_c == kh_c, s_c, NEG)
    r_i = lax.broadcasted_iota(jnp.int32, (rows, rows), 0)
    c_i = lax.broadcasted_iota(jnp.int32, (rows, rows), 1)
    s_n = _dot3(q, stack(kn_ref), _NT) - fn_ref[0]
    s_n = jnp.where((r_i // ds == c_i // ds) & (c_i % ds <= r_i % ds), s_n, NEG)
    m = jnp.maximum(jnp.max(s_c, axis=1, keepdims=True), jnp.max(s_n, axis=1, keepdims=True))
    p_c = jnp.exp2(s_c - m)
    p_n = jnp.exp2(s_n - m)
    l = jnp.sum(p_c, axis=1, keepdims=True) + jnp.sum(p_n, axis=1, keepdims=True)
    o = (_dot3(p_c, vc_ref[0]) + _dot3(p_n, stack(vn_ref))) / l
    for h in range(nh):
        o_ref[:, h * B_DH:(h + 1) * B_DH] = o[h * ds:(h + 1) * ds]


def _fox_sample(pb, kc, vc, f_cache, f_new, prev_out, *, nb, ds, row_block0):
    n_c = kc.shape[1]
    base = (A_QK * 2 + A_V * 2) // B_W
    rb = lambda b: row_block0 + b
    return pl.pallas_call(
        functools.partial(_fox_sample_kernel, ds=ds),
        grid=(nb,),
        in_specs=[pl.BlockSpec((ds, B_W), lambda b: (rb(b), base)),
                  pl.BlockSpec((ds, B_W), lambda b: (rb(b), base + 1)),
                  pl.BlockSpec((ds, B_W), lambda b: (rb(b), base + 2)),
                  pl.BlockSpec((1, n_c, B_DH), lambda b: (b, 0, 0)),
                  pl.BlockSpec((1, n_c, B_DH), lambda b: (b, 0, 0)),
                  pl.BlockSpec((1, 1, n_c), lambda b: (b, 0, 0)),
                  pl.BlockSpec((1, 1, B_HEADS * ds), lambda b: (b, 0, 0)),
                  pl.BlockSpec(memory_space=pl.ANY)],
        out_specs=pl.BlockSpec((ds, B_W), lambda b: (rb(b), 0)),
        out_shape=jax.ShapeDtypeStruct(prev_out.shape, F32),
        input_output_aliases={7: 0},
        compiler_params=_params(("parallel",)),
        name="fox_sample",
    )(pb, pb, pb, kc, vc, f_cache, f_new, prev_out)


def _mlstm_kernel(q_ref, k_ref, v_ref, lir_ref, br_ref, lic_ref, bc_ref, c0_ref, n0_ref, m0_ref, *rest,
                  L, cps, aliased):
    if aliased:
        rest = rest[1:]
    h_ref, cout_ref, nout_ref, mout_ref, c_scr, n_scr, m_scr = rest
    step = pl.program_id(1)

    @pl.when(step == 0)
    def _():
        c_scr[...] = c0_ref[0]
        n_scr[...] = n0_ref[0]
        m_scr[...] = m0_ref[0]

    for j in range(cps):
        rows = pl.ds(j * L, L)
        _mlstm_chunk(q_ref.at[rows], k_ref.at[rows], v_ref.at[rows], lir_ref.at[0, j], br_ref.at[0, j],
                     lic_ref.at[0, j], bc_ref.at[0, j], h_ref.at[rows], c_scr, n_scr, m_scr, L=L)

    @pl.when(step == pl.num_programs(1) - 1)
    def _():
        cout_ref[0] = c_scr[...]
        nout_ref[0] = n_scr[...]
        mout_ref[0] = m_scr[...]


def _mlstm_chunk(q_ref, k_ref, v_ref, lir_ref, br_ref, lic_ref, bc_ref, h_ref, c_scr, n_scr, m_scr, *, L):
    nh = C_HEADS
    hl = nh * L
    seg = lax.broadcasted_iota(jnp.int32, (1, hl), 1) // L
    causal = (lax.broadcasted_iota(jnp.int32, (L, hl), 1) % L) <= lax.broadcasted_iota(jnp.int32, (L, hl), 0)

    def per_head(vals):
        out = vals[0]
        for h in range(1, nh):
            out = jnp.where(seg == h, vals[h], out)
        return out

    seg_max = lambda x, h: jnp.max(jnp.where(seg == h, x, -jnp.inf), axis=1, keepdims=True)

    qf = q_ref[...]
    kf = k_ref[...] * (C_DQK ** -0.5)
    qb = qf.astype(BF16)
    kb = kf.astype(BF16)
    vb = v_ref[...].astype(BF16)
    b_row = br_ref[...]
    li_row = lir_ref[...]
    b_col = [bc_ref[:, h:h + 1] for h in range(nh)]
    li_col = [lic_ref[:, h:h + 1] for h in range(nh)]
    m_prev = [m_scr[h:h + 1, :] for h in range(nh)]

    d = jnp.where(causal, per_head(b_col) - b_row + li_row, -jnp.inf)
    inter = [b_col[h] + m_prev[h] for h in range(nh)]
    m_t = [jnp.maximum(inter[h], seg_max(d, h)) for h in range(nh)]
    pm = jnp.exp(d - per_head(m_t))
    w_inter = [jnp.exp(inter[h] - m_t[h]) for h in range(nh)]

    khead = lax.broadcasted_iota(jnp.int32, (1, C_QK), 1) // C_DQK
    vhead = lax.broadcasted_iota(jnp.int32, (1, C_V), 1) // C_DV
    zero_b = jnp.zeros((), BF16)
    k_stack = jnp.concatenate([jnp.where(khead == h, kb, zero_b) for h in range(nh)], axis=0)
    v_bd = jnp.concatenate([jnp.where(vhead == h, vb, zero_b) for h in range(nh)], axis=0)
    sqk = _dot(qb, k_stack, _NT) * pm
    sv = _dot(sqk.astype(BF16), v_bd)

    b_last = [b_row[:, h * L + L - 1:h * L + L] for h in range(nh)]
    g_row = per_head(b_last) - b_row + li_row
    for h in range(nh):
        qk_sl = slice(h * C_DQK, (h + 1) * C_DQK)
        v_sl = slice(h * C_DV, (h + 1) * C_DV)
        c_prev = c_scr[h]
        n_prev = n_scr[h:h + 1, :]
        num = w_inter[h] * _dot(qb[:, qk_sl], c_prev.astype(BF16), _NT) + sv[:, v_sl]
        den = (w_inter[h] * jnp.sum(qf[:, qk_sl] * n_prev, axis=1, keepdims=True)
               + jnp.sum(jnp.where(seg == h, sqk, 0.0), axis=1, keepdims=True))
        h_ref[:, v_sl] = num / jnp.maximum(jnp.abs(den), jnp.exp(-m_t[h]))

        m_new = jnp.maximum(b_last[h] + m_prev[h], seg_max(g_row, h))
        w_c = jnp.exp(b_last[h] + m_prev[h] - m_new)
        kw = kf[:, qk_sl] * jnp.exp(b_last[h] - b_col[h] + li_col[h] - m_new)
        c_scr[h] = w_c * c_prev + _dot(vb[:, v_sl], kw.astype(BF16), _TN)
        n_scr[h:h + 1, :] = w_c * n_prev + jnp.sum(kw, axis=0, keepdims=True)
        m_scr[h:h + 1, :] = m_new


def _mlstm(p, li_row, b_row, li_col, b_col, c0, n0, m0, *, nb, nc, L, row_block0, n_total, prev_out=None):
    aliased = prev_out is not None
    cps = _chunks_per_step(nc, MLSTM_CHUNKS_PER_STEP)
    assert row_block0 % cps == 0
    nc, row_block0, chunk, L = nc // cps, row_block0 // cps, L, L * cps
    rb = lambda b, c: row_block0 + b * nc + c
    in_specs = [pl.BlockSpec((L, C_QK), lambda b, c: (rb(b, c), 0)),
                pl.BlockSpec((L, C_QK), lambda b, c: (rb(b, c), 1)),
                pl.BlockSpec((L, C_V), lambda b, c: (rb(b, c), 2 * C_QK // C_V)),
                pl.BlockSpec((1, cps, 1, C_HEADS * chunk), lambda b, c: (b, c, 0, 0)),
                pl.BlockSpec((1, cps, 1, C_HEADS * chunk), lambda b, c: (b, c, 0, 0)),
                pl.BlockSpec((1, cps, chunk, C_HEADS), lambda b, c: (b, c, 0, 0)),
                pl.BlockSpec((1, cps, chunk, C_HEADS), lambda b, c: (b, c, 0, 0)),
                pl.BlockSpec((1, C_HEADS, C_DV, C_DQK), lambda b, c: (b, 0, 0, 0)),
                pl.BlockSpec((1, C_HEADS, C_DQK), lambda b, c: (b, 0, 0)),
                pl.BlockSpec((1, C_HEADS, 1), lambda b, c: (b, 0, 0))]
    args = [p, p, p, li_row, b_row, li_col, b_col, c0, n0, m0]
    io_alias = {}
    if aliased:
        in_specs.append(pl.BlockSpec(memory_space=pl.ANY))
        args.append(prev_out)
        io_alias = {len(args) - 1: 0}
    return pl.pallas_call(
        functools.partial(_mlstm_kernel, L=chunk, cps=cps, aliased=aliased),
        grid=(nb, nc),
        in_specs=in_specs,
        out_specs=[pl.BlockSpec((L, C_V), lambda b, c: (rb(b, c), 0)),
                   pl.BlockSpec((1, C_HEADS, C_DV, C_DQK), lambda b, c: (b, 0, 0, 0)),
                   pl.BlockSpec((1, C_HEADS, C_DQK), lambda b, c: (b, 0, 0)),
                   pl.BlockSpec((1, C_HEADS, 1), lambda b, c: (b, 0, 0))],
        out_shape=[jax.ShapeDtypeStruct((n_total, C_V), F32),
                   jax.ShapeDtypeStruct((nb, C_HEADS, C_DV, C_DQK), F32),
                   jax.ShapeDtypeStruct((nb, C_HEADS, C_DQK), F32),
                   jax.ShapeDtypeStruct((nb, C_HEADS, 1), F32)],
        scratch_shapes=[pltpu.VMEM((C_HEADS, C_DV, C_DQK), F32),
                        pltpu.VMEM((C_HEADS, C_DQK), F32),
                        pltpu.VMEM((C_HEADS, 1), F32)],
        input_output_aliases=io_alias,
        compiler_params=_params(("parallel", "arbitrary")),
        name="mlstm_L%d" % chunk,
    )(*args)


def _mixout_kernel(*refs, hd, act, has_b, precise, n_prev):
    refs = list(refs)
    h_ref, a_ref, r_ref = refs[:3]
    b_ref = refs[3] if has_b else None
    k = 4 if has_b else 3
    ga_ref, wo_ref, gf_ref, wrh_ref, wrl_ref, br_ref, cnt0_ref = refs[k:k + 7]
    h1_ref, xn_ref, info_ref, cnt_ref, cnt_scr = refs[k + 7 + n_prev:]
    cast = (lambda x: x) if precise else (lambda x: x.astype(BF16))
    a = a_ref[...]
    r = r_ref[...]
    gate = r * _sigmoid(r) if act == "silu" else _sigmoid(r)
    parts = []
    for hh in range(a.shape[1] // hd):
        sl = slice(hh * hd, (hh + 1) * hd)
        parts.append(cast(_rms(a[:, sl], ga_ref[...]) * gate[:, sl]))
    if has_b:
        parts.append(cast(b_ref[...]))
    cat = jnp.concatenate(parts, axis=1)
    h1 = h_ref[...] + (_dot3(cat, wo_ref[...]) if precise else _dot(cat, wo_ref[...]))
    h1_ref[...] = h1
    xn = _rms(h1, gf_ref[...])
    xn_ref[...] = _pack_bf16_pairs(xn)
    xh, xl = _split(xn)
    hh_hl = _dot(xh, jnp.concatenate([wrh_ref[...], wrl_ref[...]], axis=1))
    logits = hh_hl[:, :LANES] + hh_hl[:, LANES:] + _dot(xl, wrh_ref[...]) + br_ref[...]

    lane = lax.broadcasted_iota(jnp.int32, logits.shape, 1)
    lanef = lane.astype(F32)
    is_g = lane < N_GROUPS
    gl = jnp.where(is_g, logits, -jnp.inf)
    gmax = jnp.max(gl, axis=1, keepdims=True)
    gidx = jnp.min(jnp.where(gl == gmax, lanef, float(LANES)), axis=1, keepdims=True)
    wg = 1.0 / jnp.sum(jnp.where(is_g, jnp.exp(gl - gmax), 0.0), axis=1, keepdims=True)
    lo = N_GROUPS + N_EXP * gidx
    el = jnp.where((lanef >= lo) & (lanef < lo + N_EXP), logits, -jnp.inf)
    m1 = jnp.max(el, axis=1, keepdims=True)
    i1 = jnp.min(jnp.where(el == m1, lanef, float(LANES)), axis=1, keepdims=True)
    el2 = jnp.where(lanef == i1, -jnp.inf, el)
    m2 = jnp.max(el2, axis=1, keepdims=True)
    i2 = jnp.min(jnp.where(el2 == m2, lanef, float(LANES)), axis=1, keepdims=True)
    t = jnp.exp(m2 - m1)
    w1 = wg / (1.0 + t)
    w2 = wg * t / (1.0 + t)
    e1 = i1 - N_GROUPS
    e2 = i2 - N_GROUPS

    @pl.when(pl.program_id(0) == 0)
    def _():
        cnt_scr[...] = cnt0_ref[...]

    tm = logits.shape[0]
    pick = jnp.where((lanef == e1) | (lanef == e2), 1.0, 0.0)
    earlier = (lax.broadcasted_iota(jnp.int32, (tm, tm), 1) < lax.broadcasted_iota(jnp.int32, (tm, tm), 0))
    before = _dot(jnp.where(earlier, 1.0, 0.0).astype(BF16), pick.astype(BF16)) + cnt_scr[...]
    r1 = jnp.sum(jnp.where(lanef == e1, before, 0.0), axis=1, keepdims=True)
    r2 = jnp.sum(jnp.where(lanef == e2, before, 0.0), axis=1, keepdims=True)
    cnt_new = cnt_scr[...] + jnp.sum(pick, axis=0, keepdims=True)
    cnt_scr[...] = cnt_new
    cnt_ref[...] = cnt_new
    info_ref[...] = jnp.where(lane == 0, e1, jnp.where(lane == 1, e2, jnp.where(lane == 2, w1, jnp.where(
        lane == 3, w2, jnp.where(lane == 4, r1, jnp.where(lane == 5, r2, 0.0))))))


def _mixout(h, a, a_col, r, r_col, b, g_head, w_o, g_ffn, wr_hi, wr_lo, b_r, counts0, *, hd, act,
            row0, n_rows, prev=None):
    n, d = h.shape
    precise = prev is not None
    tm = _tile(n_rows, MIXOUT_TILE, SUBLANES)
    assert row0 % tm == 0
    blk0 = row0 // tm
    wa = w_o.shape[0] if b is None else w_o.shape[0] - B_W
    has_b = b is not None
    row = lambda i: (blk0 + i, 0)
    const = lambda i: (0, 0)
    in_specs = [pl.BlockSpec((tm, d), row),
                pl.BlockSpec((tm, wa), lambda i: (blk0 + i, a_col)),
                pl.BlockSpec((tm, wa), lambda i: (blk0 + i, r_col))]
    args = [h, a, r]
    if has_b:
        in_specs.append(pl.BlockSpec((tm, B_W), row))
        args.append(b)
    in_specs += [pl.BlockSpec((1, hd), const), pl.BlockSpec(w_o.shape, const), pl.BlockSpec((1, d), const),
                 pl.BlockSpec((d, LANES), const), pl.BlockSpec((d, LANES), const), pl.BlockSpec((1, LANES), const),
                 pl.BlockSpec((1, LANES), const)]
    args += [g_head.reshape(1, hd), w_o if precise else w_o.astype(BF16), g_ffn.reshape(1, d), wr_hi, wr_lo, b_r,
             counts0]
    io_alias = {}
    if precise:
        for k, arr in enumerate(prev):
            in_specs.append(pl.BlockSpec(memory_space=pl.ANY))
            args.append(arr)
            io_alias[len(args) - 1] = k
    return pl.pallas_call(
        functools.partial(_mixout_kernel, hd=hd, act=act, has_b=has_b, precise=precise, n_prev=len(io_alias)),
        grid=(n_rows // tm,),
        in_specs=in_specs,
        out_specs=[pl.BlockSpec((tm, d), row), pl.BlockSpec((tm, d // 2), row), pl.BlockSpec((tm, LANES), row),
                   pl.BlockSpec((1, LANES), const)],
        out_shape=[jax.ShapeDtypeStruct((n, d), F32), jax.ShapeDtypeStruct((n, d // 2), jnp.int32),
                   jax.ShapeDtypeStruct((n, LANES), F32), jax.ShapeDtypeStruct((1, LANES), F32)],
        scratch_shapes=[pltpu.VMEM((1, LANES), F32)],
        input_output_aliases=io_alias,
        compiler_params=_params(("arbitrary",)),
        name="mixout_" + act + ("_precise" if precise else ""),
    )(*args)


def _sc_gather(table, idx):
    r = idx.shape[0]
    w = table.shape[1]
    n_workers = SC_CORES * SC_SUBCORES
    per_worker = r // n_workers
    step = SC_GATHER_ROWS * SC_GATHER_BUFS
    assert r % n_workers == 0 and per_worker % step == 0, (r, n_workers, step)
    mesh = plsc.VectorSubcoreMesh(core_axis_name="c", subcore_axis_name="s")

    @functools.partial(
        pl.kernel, mesh=mesh,
        out_type=jax.ShapeDtypeStruct((r, w), table.dtype),
        scratch_types=[pltpu.VMEM((SC_GATHER_BUFS, SC_GATHER_ROWS), jnp.int32),
                       pltpu.VMEM((SC_GATHER_BUFS, SC_GATHER_ROWS, w), table.dtype),
                       pltpu.SemaphoreType.DMA((SC_GATHER_BUFS,)),
                       pltpu.SemaphoreType.DMA((SC_GATHER_BUFS,))],
    )
    def gather(table_hbm, idx_hbm, out_hbm, idx_v, rows_v, gather_sem, store_sem):
        worker = lax.axis_index("s") * SC_CORES + lax.axis_index("c")
        base = worker * per_worker

        @pl.loop(0, per_worker // step)
        def _(j):
            off = pl.multiple_of(base + j * step, step)
            rows = lambda b: pl.ds(off + b * SC_GATHER_ROWS, SC_GATHER_ROWS)
            gathers, stores = [], []
            for b in range(SC_GATHER_BUFS):
                pltpu.sync_copy(idx_hbm.at[rows(b)], idx_v.at[b])
                gathers.append(pltpu.async_copy(table_hbm.at[idx_v.at[b]], rows_v.at[b], gather_sem.at[b]))
            for b in range(SC_GATHER_BUFS):
                gathers[b].wait()
                stores.append(pltpu.async_copy(rows_v.at[b], out_hbm.at[rows(b)], store_sem.at[b]))
            for b in range(SC_GATHER_BUFS):
                stores[b].wait()

    return gather(table, idx)


def _sc_scatter_pairs(rows, slots, n_slots):
    n, w = rows.shape
    n_workers = SC_CORES * SC_SUBCORES
    per_worker = n // n_workers
    step = SC_SCATTER_ROWS * SC_SCATTER_BUFS
    assert n % n_workers == 0 and per_worker % step == 0, (n, n_workers, step)
    mesh = plsc.VectorSubcoreMesh(core_axis_name="c", subcore_axis_name="s")

    @functools.partial(
        pl.kernel, mesh=mesh,
        out_type=jax.ShapeDtypeStruct((n_slots, w), rows.dtype),
        scratch_types=[pltpu.VMEM((2 * SC_SCATTER_BUFS, SC_SCATTER_ROWS), jnp.int32),
                       pltpu.VMEM((SC_SCATTER_BUFS, SC_SCATTER_ROWS, w), rows.dtype),
                       pltpu.SemaphoreType.DMA((SC_SCATTER_BUFS,)),
                       pltpu.SemaphoreType.DMA((SC_SCATTER_BUFS,))],
    )
    def scatter(rows_hbm, slots_hbm, out_hbm, idx_v, rows_v, load_sem, store_sem):
        worker = lax.axis_index("s") * SC_CORES + lax.axis_index("c")
        base = worker * per_worker

        @pl.loop(0, per_worker // step)
        def _(j):
            off = pl.multiple_of(base + j * step, SC_SCATTER_ROWS)
            loads, stores = [], []
            for b in range(SC_SCATTER_BUFS):
                r0 = off + b * SC_SCATTER_ROWS
                pltpu.sync_copy(slots_hbm.at[pl.ds(r0, SC_SCATTER_ROWS)], idx_v.at[2 * b])
                pltpu.sync_copy(slots_hbm.at[pl.ds(n + r0, SC_SCATTER_ROWS)], idx_v.at[2 * b + 1])
                loads.append(pltpu.async_copy(rows_hbm.at[pl.ds(r0, SC_SCATTER_ROWS)], rows_v.at[b], load_sem.at[b]))
            for b in range(SC_SCATTER_BUFS):
                loads[b].wait()
                for k in range(2):
                    stores.append(pltpu.async_copy(rows_v.at[b], out_hbm.at[idx_v.at[2 * b + k]], store_sem.at[b]))
            for copy in stores:
                copy.wait()

    return scatter(rows, slots)


def _expert_kernel(te_ref, nu_ref, x_ref, wg_ref, wu_ref, wd_ref, y_ref, wgb, wub, wdb):
    i = pl.program_id(0)
    live = i < nu_ref[0]
    new_expert = (i == 0) | (te_ref[i] != te_ref[jnp.maximum(i - 1, 0)])

    @pl.when(live & new_expert)
    def _():
        wgb[...] = wg_ref[0].astype(BF16)
        wub[...] = wu_ref[0].astype(BF16)
        wdb[...] = wd_ref[0].astype(BF16)

    @pl.when(live)
    def _():
        x = _unpack_bf16_pairs(x_ref[...]).astype(BF16)
        g = _dot(x, wgb[...])
        u = _dot(x, wub[...])
        y_ref[...] = _pack_bf16_pairs(_dot((g * _sigmoid(g) * u).astype(BF16), wdb[...]))

    @pl.when(jnp.logical_not(live))
    def _():
        y_ref[...] = jnp.zeros_like(y_ref)


def _experts(xs, tile_expert, n_used, w_gate, w_up, w_down, *, tm):
    n_slots, dp = xs.shape
    d, f = w_gate.shape[-2:]
    grid_spec = pltpu.PrefetchScalarGridSpec(
        num_scalar_prefetch=2,
        grid=(n_slots // tm,),
        in_specs=[pl.BlockSpec((tm, dp), lambda i, te, nu: (i, 0)),
                  pl.BlockSpec((1, d, f), lambda i, te, nu: (te[i], 0, 0)),
                  pl.BlockSpec((1, d, f), lambda i, te, nu: (te[i], 0, 0)),
                  pl.BlockSpec((1, f, d), lambda i, te, nu: (te[i], 0, 0))],
        out_specs=pl.BlockSpec((tm, dp), lambda i, te, nu: (i, 0)),
        scratch_shapes=[pltpu.VMEM((d, f), BF16), pltpu.VMEM((d, f), BF16), pltpu.VMEM((f, d), BF16)],
    )
    return pl.pallas_call(
        _expert_kernel,
        grid_spec=grid_spec,
        out_shape=jax.ShapeDtypeStruct((n_slots, dp), jnp.int32),
        compiler_params=_params(("arbitrary",)),
        name="moe_experts",
    )(tile_expert, n_used, xs, w_gate, w_up, w_down)


def _combine_kernel(h_ref, info_ref, y0_ref, y1_ref, o_ref):
    o_ref[...] = h_ref[...] + (info_ref[:, 2:3] * _unpack_bf16_pairs(y0_ref[...])
                               + info_ref[:, 3:4] * _unpack_bf16_pairs(y1_ref[...]))


def _combine(h, yg, info):
    n, d = h.shape
    tm = _tile(n, 512, SUBLANES)
    nt = n // tm
    return pl.pallas_call(
        _combine_kernel,
        grid=(nt,),
        in_specs=[pl.BlockSpec((tm, d), lambda i: (i, 0)),
                  pl.BlockSpec((tm, LANES), lambda i: (i, 0)),
                  pl.BlockSpec((tm, d // 2), lambda i: (i, 0)),
                  pl.BlockSpec((tm, d // 2), lambda i: (nt + i, 0))],
        out_specs=pl.BlockSpec((tm, d), lambda i: (i, 0)),
        out_shape=jax.ShapeDtypeStruct((n, d), F32),
        compiler_params=_params(("parallel",)),
        name="moe_combine",
    )(h, info, yg, yg)


def _final_kernel(*refs, sub, prompt_steps):
    g_ref = refs[0]
    h_refs, info_refs, y0_refs, y1_refs = (refs[1 + k * sub:1 + (k + 1) * sub] for k in range(4))
    yp_ref, ys_ref = refs[1 + 4 * sub:]
    tin = h_refs[0].shape[0]
    i = pl.program_id(0)

    def store(o_ref):
        for k in range(sub):
            h2 = h_refs[k][...] + (info_refs[k][:, 2:3] * _unpack_bf16_pairs(y0_refs[k][...])
                                   + info_refs[k][:, 3:4] * _unpack_bf16_pairs(y1_refs[k][...]))
            o_ref[k * tin:(k + 1) * tin, :] = _rms(h2, g_ref[...])

    pl.when(i < prompt_steps)(functools.partial(store, yp_ref))
    pl.when(i >= prompt_steps)(functools.partial(store, ys_ref))


def _final_combine(h, yg, info, g, *, nb, t, n_sample):
    n, d = h.shape
    seq = t - LEAD
    tin = _tile(math.gcd(LEAD, n_sample, seq), LANES, SUBLANES)
    sub = FINAL_SUBTILES if seq % (FINAL_SUBTILES * tin) == 0 and n_sample % (FINAL_SUBTILES * tin) == 0 else 1
    tout = sub * tin
    steps_per_batch = seq // tout
    prompt_steps = nb * steps_per_batch
    tiles_per_batch, lead_tiles, n_tiles = t // tin, LEAD // tin, n // tin

    def in_tile(k, offset=0):
        def index(i):
            ip = jnp.minimum(i, prompt_steps - 1)
            prompt = (ip // steps_per_batch) * tiles_per_batch + lead_tiles + (ip % steps_per_batch) * sub + k
            sample = nb * tiles_per_batch + (i - prompt_steps) * sub + k
            return offset + jnp.where(i < prompt_steps, prompt, sample), 0
        return index

    in_specs = ([pl.BlockSpec((1, d), lambda i: (0, 0))]
                + [pl.BlockSpec((tin, d), in_tile(k)) for k in range(sub)]
                + [pl.BlockSpec((tin, LANES), in_tile(k)) for k in range(sub)]
                + [pl.BlockSpec((tin, d // 2), in_tile(k)) for k in range(sub)]
                + [pl.BlockSpec((tin, d // 2), in_tile(k, n_tiles)) for k in range(sub)])
    return pl.pallas_call(
        functools.partial(_final_kernel, sub=sub, prompt_steps=prompt_steps),
        grid=(prompt_steps + n_sample // tout,),
        in_specs=in_specs,
        out_specs=[pl.BlockSpec((tout, d), lambda i: (jnp.minimum(i, prompt_steps - 1), 0)),
                   pl.BlockSpec((tout, d), lambda i: (jnp.maximum(i - prompt_steps, 0), 0))],
        out_shape=[jax.ShapeDtypeStruct((nb * seq, d), F32), jax.ShapeDtypeStruct((n_sample, d), F32)],
        compiler_params=_params(("arbitrary",)),
        name="moe_combine_final",
    )(g.reshape(1, d), *([h] * sub + [info] * sub + [yg] * (2 * sub)))


def _moe(h1, xn, info, counts_f, w_gate, w_up, w_down, expert_base, *, final=None):
    n, d = h1.shape
    n_e = N_GROUPS * N_EXP
    tm = _tile(2 * n, MOE_TILE, SUBLANES)
    n_tiles = (2 * n) // tm + n_e
    n_slots = n_tiles * tm
    experts = jnp.arange(n_e, dtype=jnp.int32)
    counts = counts_f[0, :n_e].astype(jnp.int32)
    padded = ((counts + tm - 1) // tm) * tm
    pad_ends = jnp.cumsum(padded)
    pad_starts = pad_ends - padded
    n_used = (pad_ends[-1] // tm).astype(jnp.int32)
    eid = info[:, 0:2].astype(jnp.int32)
    rank = info[:, 4:6].astype(jnp.int32)
    slot_of_pick = jnp.sum(jnp.where(eid[..., None] == experts, pad_starts, 0), axis=-1) + rank
    slots = slot_of_pick.T.reshape(-1)
    tile_starts = jnp.arange(n_tiles, dtype=jnp.int32) * tm
    tile_expert = expert_base + jnp.minimum(jnp.sum(tile_starts[:, None] >= pad_ends[None, :], axis=1), n_e - 1)

    xs = _sc_scatter_pairs(xn, slots, n_slots)
    ys = _experts(xs, tile_expert.astype(jnp.int32), n_used.reshape(1), w_gate, w_up, w_down, tm=tm)
    yg = _sc_gather(ys, slots)
    if final is not None:
        return _final_combine(h1, yg, info, final["g"], nb=final["nb"], t=final["t"], n_sample=final["n_sample"])
    return _combine(h1, yg, info)


def _router_weights(w_rg, b_rg, w_re, b_re):
    d = w_rg.shape[0]
    pad = LANES - N_GROUPS - N_GROUPS * N_EXP
    w = jnp.concatenate([w_rg, w_re, jnp.zeros((d, pad), F32)], axis=1)
    b = jnp.concatenate([b_rg, b_re, jnp.zeros((pad,), F32)]).reshape(1, LANES)
    hi = w.astype(BF16)
    lo = (w - hi.astype(F32)).astype(BF16)
    return hi, lo, b


def _rows_to_lanes(x, nb, t, nch):
    return x.reshape(nb, t, nch).transpose(0, 2, 1).reshape(nb * nch, t)


def _even_layer(h, dims, g_mix, w_in, w_g2, b_g, b_f, g_a, w_o, state_gla, ck, cv, clf):
    nb, t, db, ds, npr, n = dims
    d = h.shape[1]
    qa, ka, va, ra, ga, qb, kb, vb, fb = jnp.split(
        w_in, [A_QK, 2 * A_QK, 2 * A_QK + A_V, 2 * A_QK + 2 * A_V, 2 * A_QK + 2 * A_V + A_RANK,
               2 * A_QK + 2 * A_V + A_RANK + B_W, 2 * A_QK + 2 * A_V + A_RANK + 2 * B_W,
               2 * A_QK + 2 * A_V + A_RANK + 3 * B_W], axis=1)
    w_packed = jnp.concatenate(
        [qa, ka, va, ra, qb, kb, vb, ga, fb, jnp.zeros((d, LANES - A_RANK - B_HEADS), F32)], axis=1)
    q_off = 2 * A_QK + 2 * A_V
    colscale = jnp.ones((1, MAIN_W), F32).at[:, q_off:q_off + B_W].set(B_DH ** -0.5 * LOG2E)
    p, pb, k_rows, v_rows = _proj(h, g_mix, w_packed, n_rows=npr, colscale=colscale, kv_col=q_off + B_W)
    p = _proj_precise(h, g_mix, w_packed, p, row0=npr)

    nh = B_HEADS
    fcol = MAIN_W + A_RANK
    zeros_col = lambda r: jnp.zeros((r, 1), F32)
    bias_row = lambda lanes: jnp.tile(b_f, lanes // nh).reshape(1, lanes)
    fb_p = p[:npr, fcol:fcol + nh].reshape(nb, t * nh)
    logf_p, f_p = _gate_scan(fb_p, bias_row(t * nh), zeros_col(nb), mode="fox", act_start=0,
                             valid_start=N_PAD * nh, valid_end=t * nh, seg=None, stride=nh)
    past = ck.shape[1]
    n_c = past * nh
    x_s = _pad_lanes(jnp.concatenate([clf.reshape(db, n_c), p[npr:, fcol:fcol + nh].reshape(db, ds * nh)], axis=1))
    logf_s, f_s = _gate_scan(x_s, bias_row(x_s.shape[1]), zeros_col(db), mode="fox", act_start=n_c,
                             valid_start=0, valid_end=n_c + ds * nh, seg=None, stride=nh)

    nc = t // CHUNK
    oa, s_p = _gla(p, jnp.zeros((nb, A_HEADS, A_DK, A_DV), F32), w_g2, b_g,
                   nb=nb, nc=nc, L=CHUNK, row_block0=0, lead_pad=N_PAD, n_total=n)
    oa, s_s = _gla(p, state_gla, w_g2, b_g, nb=db, nc=1, L=ds, row_block0=npr // ds, lead_pad=0,
                   n_total=n, prev_out=oa, precise=True)

    fk = (f_p * LOG2E).reshape(nb, t, nh).transpose(0, 2, 1).reshape(nb, nh, 1, t)
    ob = _flash(pb, fk, nb=nb, t=t, n_total=n)
    f_cache = (f_s[:, :n_c] * LOG2E).reshape(db, 1, n_c)
    f_new = (f_s[:, n_c:n_c + ds * nh] * LOG2E).reshape(db, ds, nh).transpose(0, 2, 1).reshape(db, 1, nh * ds)
    ob = _fox_sample(p, ck.reshape(db, n_c, B_DH), cv.reshape(db, n_c, B_DH), f_cache, f_new, ob,
                     nb=db, ds=ds, row_block0=npr // ds)

    kcol = q_off + B_W
    states = dict(
        s_p=s_p, s_s=s_s,
        k_p=k_rows.reshape(nb, t, B_HEADS, B_DH)[:, N_PAD:],
        v_p=v_rows.reshape(nb, t, B_HEADS, B_DH)[:, N_PAD:],
        f_p=logf_p.reshape(nb, t, nh)[:, N_PAD:],
        k_s=p[npr:, kcol:kcol + B_W].reshape(db, ds, B_HEADS, B_DH),
        v_s=p[npr:, kcol + B_W:kcol + 2 * B_W].reshape(db, ds, B_HEADS, B_DH),
        f_s=logf_s[:, n_c:n_c + ds * nh].reshape(db, ds, nh))
    return (oa, 0, p, (2 * A_QK + A_V) // A_V, ob, g_a, w_o), states


def _chunk_rows(x, nb, nch, nc, L):
    x = x[:, :nc * L].reshape(nb, nch, nc, L)
    return x.transpose(0, 2, 1, 3).reshape(nb, nc, 1, nch * L), x.transpose(0, 2, 3, 1)


def _odd_layer(h, dims, g_mix, w_in, b_gate, g_c, w_o, c0, n0, m0):
    nb, t, db, ds, npr, n = dims
    d = h.shape[1]
    w_packed = jnp.concatenate(
        [w_in, jnp.zeros((d, LANES - 2 * C_HEADS), F32)], axis=1)
    (p,) = _proj(h, g_mix, w_packed, n_rows=npr)
    p = _proj_precise(h, g_mix, w_packed, p, row0=npr)

    ng = 2 * C_HEADS
    isf = (jnp.arange(ng) >= C_HEADS).astype(F32)
    nc = t // CHUNK

    def gates(rows, nbatch, tt, valid_start, seg):
        x = _pad_lanes(_rows_to_lanes(rows, nbatch, tt, ng))
        val, cum = _gate_scan(x, jnp.tile(b_gate, nbatch).reshape(-1, 1), jnp.tile(isf, nbatch).reshape(-1, 1),
                              mode="mlstm", act_start=0, valid_start=valid_start, valid_end=tt, seg=seg)
        return val.reshape(nbatch, ng, -1), cum.reshape(nbatch, ng, -1)

    val_p, cum_p = gates(p[:npr, MAIN_W:MAIN_W + ng], nb, t, N_PAD, CHUNK)
    val_s, cum_s = gates(p[npr:, MAIN_W:MAIN_W + ng], db, ds, 0, ds)

    def chunked(val, cum, nbatch, ncs, L):
        li_row, li_col = _chunk_rows(val[:, :C_HEADS].reshape(nbatch * C_HEADS, -1), nbatch, C_HEADS, ncs, L)
        b_row, b_col = _chunk_rows(cum[:, C_HEADS:].reshape(nbatch * C_HEADS, -1), nbatch, C_HEADS, ncs, L)
        return li_row, b_row, li_col, b_col

    zc = jnp.zeros((nb, C_HEADS, C_DV, C_DQK), F32)
    zn = jnp.zeros((nb, C_HEADS, C_DQK), F32)
    zm = jnp.zeros((nb, C_HEADS, 1), F32)
    hm, c_p, n_p, m_p = _mlstm(p, *chunked(val_p, cum_p, nb, nc, CHUNK), zc, zn, zm,
                               nb=nb, nc=nc, L=CHUNK, row_block0=0, n_total=n)
    hm, c_s, n_s, m_s = _mlstm(p, *chunked(val_s, cum_s, db, 1, ds), c0, n0, m0.reshape(db, C_HEADS, 1),
                               nb=db, nc=1, L=ds, row_block0=npr // ds, n_total=n, prev_out=hm)
    states = dict(c_p=c_p, n_p=n_p, m_p=m_p.reshape(nb, C_HEADS), c_s=c_s, n_s=n_s, m_s=m_s.reshape(db, C_HEADS))
    return (hm, 0, p, (2 * C_QK + C_V) // C_V, None, g_c, w_o), states


def kernel(x_prompt, x_sample, state_gla, cache_fox_k, cache_fox_v, cache_fox_logf, state_mlstm_c, state_mlstm_n, state_mlstm_m, meta_tokens, norm_mix, norm_ffn, norm_final, w_in_even, w_gla_gate2, b_gla_gate, b_fox_f, g_gla_out, w_out_even, w_in_odd, b_mlstm_gate, g_mlstm_out, w_out_odd, w_router_group, b_router_group, w_router_expert, b_router_expert, w_exp_gate, w_exp_up, w_exp_down):
    nb, seq, d = x_prompt.shape
    db, ds, _ = x_sample.shape
    t = LEAD + seq
    npr, nsm = nb * t, db * ds
    n = npr + nsm
    dims = (nb, t, db, ds, npr, n)
    depth = norm_mix.shape[0]
    n_e = N_GROUPS * N_EXP
    f = w_exp_gate.shape[-1]

    pad_rows = jnp.zeros((N_PAD, d), F32)
    h = jnp.concatenate([piece for b in range(nb) for piece in (pad_rows, meta_tokens, x_prompt[b])]
                        + [x_sample.reshape(nsm, d)], axis=0)
    wg_all = w_exp_gate.reshape(depth * n_e, d, f)
    wu_all = w_exp_up.reshape(depth * n_e, d, f)
    wd_all = w_exp_down.reshape(depth * n_e, f, d)

    even, odd = [], []
    for l in range(depth):
        if l % 2 == 0:
            e = l // 2
            mix, st = _even_layer(h, dims, norm_mix[l], w_in_even[e], w_gla_gate2[e], b_gla_gate[e], b_fox_f[e],
                                  g_gla_out[e], w_out_even[e], state_gla[e], cache_fox_k[e], cache_fox_v[e],
                                  cache_fox_logf[e])
            even.append(st)
            hd, act = A_DV, "silu"
        else:
            o = l // 2
            mix, st = _odd_layer(h, dims, norm_mix[l], w_in_odd[o], b_mlstm_gate[o], g_mlstm_out[o], w_out_odd[o],
                                 state_mlstm_c[o], state_mlstm_n[o], state_mlstm_m[o])
            odd.append(st)
            hd, act = C_DV, "sigmoid"
        a, a_col, r, r_col, b, g_head, w_o = mix
        wr_hi, wr_lo, b_r = _router_weights(w_router_group[l], b_router_group[l], w_router_expert[l],
                                            b_router_expert[l])
        mix_args = (h, a, a_col, r, r_col, b, g_head, w_o, norm_ffn[l], wr_hi, wr_lo, b_r)
        h1, xn, info, counts = _mixout(*mix_args, jnp.zeros((1, LANES), F32), hd=hd, act=act, row0=0, n_rows=npr)
        h1, xn, info, counts = _mixout(*mix_args, counts, hd=hd, act=act, row0=npr, n_rows=nsm,
                                       prev=(h1, xn, info))
        last = l == depth - 1
        h = _moe(h1, xn, info, counts, wg_all, wu_all, wd_all, l * n_e,
                 final=dict(g=norm_final, nb=nb, t=t, n_sample=nsm) if last else None)

    y_prompt = h[0].reshape(nb, seq, d)
    y_sample = h[1].reshape(db, ds, d)
    stack = lambda sts, key: jnp.stack([s[key] for s in sts])
    return (y_prompt, y_sample,
            stack(even, "s_p"), stack(even, "k_p"), stack(even, "v_p"), stack(even, "f_p"),
            stack(odd, "c_p"), stack(odd, "n_p"), stack(odd, "m_p"),
            stack(even, "s_s"), stack(even, "k_s"), stack(even, "v_s"), stack(even, "f_s"),
            stack(odd, "c_s"), stack(odd, "n_s"), stack(odd, "m_s"))
```
